```python
import jax, jax.numpy as jnp
from jax import lax
import numpy as np

D_MODEL = 1024
BATCH = 32
SEQ = 2048
DEPTH = 4

N_MIXERS = 2
N_HEADS = 16
N_KV_HEADS = 2
HEAD_DIM = 64
GROUP = N_HEADS // N_KV_HEADS
WINDOW = 128
BLOCK = 128
SPAN = BLOCK + WINDOW
Q_DIM = N_HEADS * HEAD_DIM
KV_DIM = N_KV_HEADS * HEAD_DIM
QKV_DIM = Q_DIM + 2 * KV_DIM
CONV_WIDTH = 31
CONV_DIM = D_MODEL
D_FF = 4 * D_MODEL
N_MOD = 6
EPS = 1e-6
N_ATTN_LAYERS = (DEPTH + 1) // 2
N_CONV_LAYERS = DEPTH // 2

kernel_name = "interleaved_swa_sink_conformer_conv_adaln"


def alibi_slopes(n_heads):
    return jnp.asarray(np.array([2.0 ** (-8.0 * (h + 1) / n_heads) for h in range(n_heads)], dtype=np.float32))


def rmsnorm(x, g):
    xf = x.astype(jnp.float32)
    r = lax.rsqrt(jnp.mean(xf * xf, axis=-1, keepdims=True) + EPS)
    return (xf * r * g.astype(jnp.float32)).astype(x.dtype)


def layernorm(x, g, b):
    xf = x.astype(jnp.float32)
    mu = jnp.mean(xf, axis=-1, keepdims=True)
    var = jnp.mean(jnp.square(xf - mu), axis=-1, keepdims=True)
    y = (xf - mu) * lax.rsqrt(var + EPS) * g.astype(jnp.float32) + b.astype(jnp.float32)
    return y.astype(x.dtype)


def modulate(h, shift, scale):
    return h * (1 + scale[:, None, :]) + shift[:, None, :]


def sliding_window_attention(h, w_qkv, b_qkv, w_o, b_o, sinks):
    B, S, _ = h.shape
    qkv = h @ w_qkv + b_qkv
    q, k, v = jnp.split(qkv, [Q_DIM, Q_DIM + KV_DIM], axis=-1)
    q = q.reshape(B, S, N_KV_HEADS, GROUP, HEAD_DIM) * (HEAD_DIM ** -0.5)
    k = k.reshape(B, S, N_KV_HEADS, HEAD_DIM)
    v = v.reshape(B, S, N_KV_HEADS, HEAD_DIM)
    pad = ((0, 0), (WINDOW, 0), (0, 0), (0, 0))
    k_pad = jnp.pad(k, pad)
    v_pad = jnp.pad(v, pad)
    q_idx = jnp.arange(BLOCK)[:, None] + WINDOW
    k_idx = jnp.arange(SPAN)[None, :]
    dist = q_idx - k_idx
    band = (dist >= 0) & (dist < WINDOW)
    slopes = alibi_slopes(N_HEADS).reshape(N_KV_HEADS, GROUP)
    alibi = -slopes[:, :, None, None] * dist.astype(jnp.float32)[None, None]
    sink = sinks.astype(jnp.float32).reshape(N_KV_HEADS, GROUP)[None, :, :, None, None]

    def one_block(i):
        start = i * BLOCK
        qb = lax.dynamic_slice_in_dim(q, start, BLOCK, axis=1)
        kb = lax.dynamic_slice_in_dim(k_pad, start, SPAN, axis=1)
        vb = lax.dynamic_slice_in_dim(v_pad, start, SPAN, axis=1)
        s = jnp.einsum('bqkgd,bskd->bkgqs', qb, kb).astype(jnp.float32) + alibi
        mask = band & ((start - WINDOW + k_idx) >= 0)
        s = jnp.where(mask[None, None, None], s, -jnp.inf)
        m = jnp.maximum(jnp.max(s, axis=-1, keepdims=True), sink)
        p = jnp.exp(s - m)
        denom = jnp.sum(p, axis=-1, keepdims=True) + jnp.exp(sink - m)
        p = (p / denom).astype(vb.dtype)
        return jnp.einsum('bkgqs,bskd->bqkgd', p, vb)

    out = lax.map(one_block, jnp.arange(S // BLOCK))
    out = jnp.moveaxis(out, 0, 1).reshape(B, S, Q_DIM)
    return out @ w_o + b_o


def conformer_conv(h, w_pw1, b_pw1, w_dw, b_dw, ln_g, ln_b, w_pw2, b_pw2):
    u = h @ w_pw1 + b_pw1
    a, g = jnp.split(u, 2, axis=-1)
    u = a * jax.nn.sigmoid(g)
    u = jnp.pad(u, ((0, 0), (CONV_WIDTH - 1, 0), (0, 0)))
    u = lax.conv_general_dilated(u, w_dw[:, None, :].astype(u.dtype), window_strides=(1,), padding='VALID',
                                 dimension_numbers=('NWC', 'WIO', 'NWC'),
                                 feature_group_count=CONV_DIM) + b_dw
    u = jax.nn.silu(layernorm(u, ln_g, ln_b))
    return u @ w_pw2 + b_pw2


def squared_relu_mlp(h, w_up, w_down):
    return jnp.square(jax.nn.relu(h @ w_up)) @ w_down


def _fwd_setup_inputs(seed: int = 0) -> dict:
    key = jax.random.key(seed)
    ks = jax.random.split(key, 32)
    D = D_MODEL
    nrm = lambda k, shape, s: jax.random.normal(k, shape, jnp.float32) * s
    return {
        "x": nrm(ks[0], (BATCH, SEQ, D), 1.0),
        "c": nrm(ks[1], (BATCH, D), 1.0),
        "w_mod": nrm(ks[2], (DEPTH, D, N_MOD * D), 0.5 * D ** -0.5),
        "b_mod": nrm(ks[3], (DEPTH, N_MOD * D), 0.01),
        "norm_mix": 1.0 + nrm(ks[4], (DEPTH, D), 0.02),
        "norm_mlp": 1.0 + nrm(ks[5], (DEPTH, D), 0.02),
        "w_qkv": nrm(ks[6], (N_ATTN_LAYERS, D, QKV_DIM), D ** -0.5),
        "b_qkv": nrm(ks[7], (N_ATTN_LAYERS, QKV_DIM), 0.01),
        "w_o": nrm(ks[8], (N_ATTN_LAYERS, Q_DIM, D), Q_DIM ** -0.5),
        "b_o": nrm(ks[9], (N_ATTN_LAYERS, D), 0.01),
        "sinks": nrm(ks[10], (N_ATTN_LAYERS, N_HEADS), 0.5),
        "w_pw1": nrm(ks[11], (N_CONV_LAYERS, D, 2 * CONV_DIM), D ** -0.5),
        "b_pw1": nrm(ks[12], (N_CONV_LAYERS, 2 * CONV_DIM), 0.01),
        "w_dw": nrm(ks[13], (N_CONV_LAYERS, CONV_WIDTH, CONV_DIM), CONV_WIDTH ** -0.5),
        "b_dw": nrm(ks[14], (N_CONV_LAYERS, CONV_DIM), 0.01),
        "conv_ln_g": 1.0 + nrm(ks[15], (N_CONV_LAYERS, CONV_DIM), 0.02),
        "conv_ln_b": nrm(ks[16], (N_CONV_LAYERS, CONV_DIM), 0.01),
        "w_pw2": nrm(ks[17], (N_CONV_LAYERS, CONV_DIM, D), CONV_DIM ** -0.5),
        "b_pw2": nrm(ks[18], (N_CONV_LAYERS, D), 0.01),
        "w_up": nrm(ks[19], (DEPTH, D, D_FF), D ** -0.5),
        "w_down": nrm(ks[20], (DEPTH, D_FF, D), D_FF ** -0.5),
        "final_norm": 1.0 + nrm(ks[21], (D,), 0.02),
    }


def _fwd_reference(x, c, w_mod, b_mod, norm_mix, norm_mlp, w_qkv, b_qkv, w_o, b_o, sinks,
              w_pw1, b_pw1, w_dw, b_dw, conv_ln_g, conv_ln_b, w_pw2, b_pw2,
              w_up, w_down, final_norm):
    cs = jax.nn.silu(c)
    for i in range(DEPTH):
        mod = cs @ w_mod[i] + b_mod[i]
        sh1, sc1, g1, sh2, sc2, g2 = jnp.split(mod, N_MOD, axis=-1)
        h = modulate(rmsnorm(x, norm_mix[i]), sh1, sc1)
        j = i // N_MIXERS
        if i % N_MIXERS == 0:
            y = sliding_window_attention(h, w_qkv[j], b_qkv[j], w_o[j], b_o[j], sinks[j])
        else:
            y = conformer_conv(h, w_pw1[j], b_pw1[j], w_dw[j], b_dw[j], conv_ln_g[j], conv_ln_b[j],
                               w_pw2[j], b_pw2[j])
        x = x + g1[:, None, :] * y
        h = modulate(rmsnorm(x, norm_mlp[i]), sh2, sc2)
        x = x + g2[:, None, :] * squared_relu_mlp(h, w_up[i], w_down[i])
    return rmsnorm(x, final_norm)


import jax as _jax
import jax.numpy as _jnp

TWIN_FORMAT = 'train_step'
FWD_PARAMS = ['x', 'c', 'w_mod', 'b_mod', 'norm_mix', 'norm_mlp', 'w_qkv', 'b_qkv', 'w_o', 'b_o', 'sinks', 'w_pw1', 'b_pw1', 'w_dw', 'b_dw', 'conv_ln_g', 'conv_ln_b', 'w_pw2', 'b_pw2', 'w_up', 'w_down', 'final_norm']
TWIN_WEIGHTS = ['w_mod', 'b_mod', 'norm_mix', 'norm_mlp', 'w_qkv', 'b_qkv', 'w_o', 'b_o', 'sinks', 'w_pw1', 'b_pw1', 'w_dw', 'b_dw', 'conv_ln_g', 'conv_ln_b', 'w_pw2', 'b_pw2', 'w_up', 'w_down', 'final_norm']
TWIN_DIFF_INPUT = 'x'
TWIN_INPUTS = ['x', 'c', 'w_mod', 'b_mod', 'norm_mix', 'norm_mlp', 'w_qkv', 'b_qkv', 'w_o', 'b_o', 'sinks', 'w_pw1', 'b_pw1', 'w_dw', 'b_dw', 'conv_ln_g', 'conv_ln_b', 'w_pw2', 'b_pw2', 'w_up', 'w_down', 'final_norm', 'loss_target', 'm_w_mod', 'm_b_mod', 'm_norm_mix', 'm_norm_mlp', 'm_w_qkv', 'm_b_qkv', 'm_w_o', 'm_b_o', 'm_sinks', 'm_w_pw1', 'm_b_pw1', 'm_w_dw', 'm_b_dw', 'm_conv_ln_g', 'm_conv_ln_b', 'm_w_pw2', 'm_b_pw2', 'm_w_up', 'm_w_down', 'm_final_norm', 'v_w_mod', 'v_b_mod', 'v_norm_mix', 'v_norm_mlp', 'v_w_qkv', 'v_b_qkv', 'v_w_o', 'v_b_o', 'v_sinks', 'v_w_pw1', 'v_b_pw1', 'v_w_dw', 'v_b_dw', 'v_conv_ln_g', 'v_conv_ln_b', 'v_w_pw2', 'v_b_pw2', 'v_w_up', 'v_w_down', 'v_final_norm']
TWIN_OUTPUTS = ['loss', 'grad_x', 'grad_w_mod', 'grad_b_mod', 'grad_norm_mix', 'grad_norm_mlp', 'grad_w_qkv', 'grad_b_qkv', 'grad_w_o', 'grad_b_o', 'grad_sinks', 'grad_w_pw1', 'grad_b_pw1', 'grad_w_dw', 'grad_b_dw', 'grad_conv_ln_g', 'grad_conv_ln_b', 'grad_w_pw2', 'grad_b_pw2', 'grad_w_up', 'grad_w_down', 'grad_final_norm', 'delta_w_mod', 'delta_b_mod', 'delta_norm_mix', 'delta_norm_mlp', 'delta_w_qkv', 'delta_b_qkv', 'delta_w_o', 'delta_b_o', 'delta_sinks', 'delta_w_pw1', 'delta_b_pw1', 'delta_w_dw', 'delta_b_dw', 'delta_conv_ln_g', 'delta_conv_ln_b', 'delta_w_pw2', 'delta_b_pw2', 'delta_w_up', 'delta_w_down', 'delta_final_norm', 'new_m_w_mod', 'new_m_b_mod', 'new_m_norm_mix', 'new_m_norm_mlp', 'new_m_w_qkv', 'new_m_b_qkv', 'new_m_w_o', 'new_m_b_o', 'new_m_sinks', 'new_m_w_pw1', 'new_m_b_pw1', 'new_m_w_dw', 'new_m_b_dw', 'new_m_conv_ln_g', 'new_m_conv_ln_b', 'new_m_w_pw2', 'new_m_b_pw2', 'new_m_w_up', 'new_m_w_down', 'new_m_final_norm', 'new_v_w_mod', 'new_v_b_mod', 'new_v_norm_mix', 'new_v_norm_mlp', 'new_v_w_qkv', 'new_v_b_qkv', 'new_v_w_o', 'new_v_b_o', 'new_v_sinks', 'new_v_w_pw1', 'new_v_b_pw1', 'new_v_w_dw', 'new_v_b_dw', 'new_v_conv_ln_g', 'new_v_conv_ln_b', 'new_v_w_pw2', 'new_v_b_pw2', 'new_v_w_up', 'new_v_w_down', 'new_v_final_norm']
TWIN_LEAF_KINDS = {'loss': 'loss', 'grad_x': 'grad_x', 'grad_w_mod': 'grad_w', 'grad_b_mod': 'grad_w', 'grad_norm_mix': 'grad_w', 'grad_norm_mlp': 'grad_w', 'grad_w_qkv': 'grad_w', 'grad_b_qkv': 'grad_w', 'grad_w_o': 'grad_w', 'grad_b_o': 'grad_w', 'grad_sinks': 'grad_w', 'grad_w_pw1': 'grad_w', 'grad_b_pw1': 'grad_w', 'grad_w_dw': 'grad_w', 'grad_b_dw': 'grad_w', 'grad_conv_ln_g': 'grad_w', 'grad_conv_ln_b': 'grad_w', 'grad_w_pw2': 'grad_w', 'grad_b_pw2': 'grad_w', 'grad_w_up': 'grad_w', 'grad_w_down': 'grad_w', 'grad_final_norm': 'grad_w', 'delta_w_mod': 'delta_w', 'delta_b_mod': 'delta_w', 'delta_norm_mix': 'delta_w', 'delta_norm_mlp': 'delta_w', 'delta_w_qkv': 'delta_w', 'delta_b_qkv': 'delta_w', 'delta_w_o': 'delta_w', 'delta_b_o': 'delta_w', 'delta_sinks': 'delta_w', 'delta_w_pw1': 'delta_w', 'delta_b_pw1': 'delta_w', 'delta_w_dw': 'delta_w', 'delta_b_dw': 'delta_w', 'delta_conv_ln_g': 'delta_w', 'delta_conv_ln_b': 'delta_w', 'delta_w_pw2': 'delta_w', 'delta_b_pw2': 'delta_w', 'delta_w_up': 'delta_w', 'delta_w_down': 'delta_w', 'delta_final_norm': 'delta_w', 'new_m_w_mod': 'new_m', 'new_m_b_mod': 'new_m', 'new_m_norm_mix': 'new_m', 'new_m_norm_mlp': 'new_m', 'new_m_w_qkv': 'new_m', 'new_m_b_qkv': 'new_m', 'new_m_w_o': 'new_m', 'new_m_b_o': 'new_m', 'new_m_sinks': 'new_m', 'new_m_w_pw1': 'new_m', 'new_m_b_pw1': 'new_m', 'new_m_w_dw': 'new_m', 'new_m_b_dw': 'new_m', 'new_m_conv_ln_g': 'new_m', 'new_m_conv_ln_b': 'new_m', 'new_m_w_pw2': 'new_m', 'new_m_b_pw2': 'new_m', 'new_m_w_up': 'new_m', 'new_m_w_down': 'new_m', 'new_m_final_norm': 'new_m', 'new_v_w_mod': 'new_v', 'new_v_b_mod': 'new_v', 'new_v_norm_mix': 'new_v', 'new_v_norm_mlp': 'new_v', 'new_v_w_qkv': 'new_v', 'new_v_b_qkv': 'new_v', 'new_v_w_o': 'new_v', 'new_v_b_o': 'new_v', 'new_v_sinks': 'new_v', 'new_v_w_pw1': 'new_v', 'new_v_b_pw1': 'new_v', 'new_v_w_dw': 'new_v', 'new_v_b_dw': 'new_v', 'new_v_conv_ln_g': 'new_v', 'new_v_conv_ln_b': 'new_v', 'new_v_w_pw2': 'new_v', 'new_v_b_pw2': 'new_v', 'new_v_w_up': 'new_v', 'new_v_w_down': 'new_v', 'new_v_final_norm': 'new_v'}


def _forward(args):
    return _fwd_reference(*[args[k] for k in FWD_PARAMS])


def _output_shape():
    out = _jax.eval_shape(lambda: _forward(_fwd_setup_inputs(0)))
    return out.shape, out.dtype

N_MICROBATCH = 1
ADAM_LR = 0.001
ADAM_B1 = 0.9
ADAM_B2 = 0.999
ADAM_EPS = 1e-08
ADAM_WD = 0.01
ADAM_STEP = 10
PER_EXAMPLE_BATCH_AXIS = {'x': 0, 'c': 0, 'loss_target': 0}
SHARED_INPUTS = []
_WEIGHT_DTYPES = {'w_mod': _jnp.float32, 'b_mod': _jnp.float32, 'norm_mix': _jnp.float32, 'norm_mlp': _jnp.float32, 'w_qkv': _jnp.float32, 'b_qkv': _jnp.float32, 'w_o': _jnp.float32, 'b_o': _jnp.float32, 'sinks': _jnp.float32, 'w_pw1': _jnp.float32, 'b_pw1': _jnp.float32, 'w_dw': _jnp.float32, 'b_dw': _jnp.float32, 'conv_ln_g': _jnp.float32, 'conv_ln_b': _jnp.float32, 'w_pw2': _jnp.float32, 'b_pw2': _jnp.float32, 'w_up': _jnp.float32, 'w_down': _jnp.float32, 'final_norm': _jnp.float32}
MOMENT_SCALE = {'w_mod': 1.065567e-01, 'b_mod': 1.778175e-01, 'norm_mix': 4.183615e-02, 'norm_mlp': 1.084373e-01, 'w_qkv': 4.178416e-02, 'b_qkv': 7.002594e-02, 'w_o': 3.622896e-02, 'b_o': 7.829887e-02, 'sinks': 7.394752e-02, 'w_pw1': 3.199547e-02, 'b_pw1': 3.175358e-02, 'w_dw': 4.087525e-02, 'b_dw': 8.011172e-02, 'conv_ln_g': 4.749657e-02, 'conv_ln_b': 4.524105e-02, 'w_pw2': 4.146321e-02, 'b_pw2': 7.880051e-02, 'w_up': 5.474066e-02, 'w_down': 1.007092e-01, 'final_norm': 6.516366e+01}


def _to_microbatches(a, axis):
    t = _jnp.moveaxis(a, axis, 0)
    t = t.reshape((N_MICROBATCH, t.shape[0] // N_MICROBATCH) + t.shape[1:])
    return _jnp.moveaxis(t, 1, axis + 1)


def setup_inputs(seed: int = 0) -> dict:
    inp = _fwd_setup_inputs(seed)
    key = _jax.random.fold_in(_jax.random.key(seed), 7919)
    shape, _ = _output_shape()
    out = dict(inp)
    out["loss_target"] = _jax.random.normal(_jax.random.fold_in(key, 0), shape, _jnp.float32)
    for i, name in enumerate(TWIN_WEIGHTS):
        w = inp[name].astype(_jnp.float32)
        if MOMENT_SCALE is None:
            s = _jnp.sqrt(_jnp.mean(_jnp.square(w)) + 1e-30)
        else:
            s = MOMENT_SCALE[name]
        km, kv = _jax.random.split(_jax.random.fold_in(key, i + 1))
        out[name] = w
        out["m_" + name] = s * _jax.random.normal(km, w.shape, _jnp.float32)
        out["v_" + name] = (s * s) * _jax.random.uniform(kv, w.shape, _jnp.float32, 0.5, 1.5)
    if N_MICROBATCH > 1:
        for name, axis in PER_EXAMPLE_BATCH_AXIS.items():
            out[name] = _to_microbatches(out[name], axis)
    return {'x': out['x'], 'c': out['c'], 'w_mod': out['w_mod'], 'b_mod': out['b_mod'], 'norm_mix': out['norm_mix'], 'norm_mlp': out['norm_mlp'], 'w_qkv': out['w_qkv'], 'b_qkv': out['b_qkv'], 'w_o': out['w_o'], 'b_o': out['b_o'], 'sinks': out['sinks'], 'w_pw1': out['w_pw1'], 'b_pw1': out['b_pw1'], 'w_dw': out['w_dw'], 'b_dw': out['b_dw'], 'conv_ln_g': out['conv_ln_g'], 'conv_ln_b': out['conv_ln_b'], 'w_pw2': out['w_pw2'], 'b_pw2': out['b_pw2'], 'w_up': out['w_up'], 'w_down': out['w_down'], 'final_norm': out['final_norm'], 'loss_target': out['loss_target'], 'm_w_mod': out['m_w_mod'], 'm_b_mod': out['m_b_mod'], 'm_norm_mix': out['m_norm_mix'], 'm_norm_mlp': out['m_norm_mlp'], 'm_w_qkv': out['m_w_qkv'], 'm_b_qkv': out['m_b_qkv'], 'm_w_o': out['m_w_o'], 'm_b_o': out['m_b_o'], 'm_sinks': out['m_sinks'], 'm_w_pw1': out['m_w_pw1'], 'm_b_pw1': out['m_b_pw1'], 'm_w_dw': out['m_w_dw'], 'm_b_dw': out['m_b_dw'], 'm_conv_ln_g': out['m_conv_ln_g'], 'm_conv_ln_b': out['m_conv_ln_b'], 'm_w_pw2': out['m_w_pw2'], 'm_b_pw2': out['m_b_pw2'], 'm_w_up': out['m_w_up'], 'm_w_down': out['m_w_down'], 'm_final_norm': out['m_final_norm'], 'v_w_mod': out['v_w_mod'], 'v_b_mod': out['v_b_mod'], 'v_norm_mix': out['v_norm_mix'], 'v_norm_mlp': out['v_norm_mlp'], 'v_w_qkv': out['v_w_qkv'], 'v_b_qkv': out['v_b_qkv'], 'v_w_o': out['v_w_o'], 'v_b_o': out['v_b_o'], 'v_sinks': out['v_sinks'], 'v_w_pw1': out['v_w_pw1'], 'v_b_pw1': out['v_b_pw1'], 'v_w_dw': out['v_w_dw'], 'v_b_dw': out['v_b_dw'], 'v_conv_ln_g': out['v_conv_ln_g'], 'v_conv_ln_b': out['v_conv_ln_b'], 'v_w_pw2': out['v_w_pw2'], 'v_b_pw2': out['v_b_pw2'], 'v_w_up': out['v_w_up'], 'v_w_down': out['v_w_down'], 'v_final_norm': out['v_final_norm']}


def _loss(weights, diff, rest, loss_target):
    with _jax.named_scope("forward"):
        args = {**rest, TWIN_DIFF_INPUT: diff, **{k: w.astype(_WEIGHT_DTYPES[k]) for k, w in weights.items()}}
        y = _forward(args)
    with _jax.named_scope("loss_head"):
        err = _jnp.square(y.astype(_jnp.float32) - loss_target)
        return 0.5 * _jnp.sum(_jnp.mean(err, axis=-1)) if err.ndim else 0.5 * err


def _adamw(w, g, m, v):
    m = ADAM_B1 * m + (1.0 - ADAM_B1) * g
    v = ADAM_B2 * v + (1.0 - ADAM_B2) * _jnp.square(g)
    m_hat = m / (1.0 - ADAM_B1 ** ADAM_STEP)
    v_hat = v / (1.0 - ADAM_B2 ** ADAM_STEP)
    delta = -ADAM_LR * (m_hat / (_jnp.sqrt(v_hat) + ADAM_EPS) + ADAM_WD * w)
    return delta, m, v


def reference(x, c, w_mod, b_mod, norm_mix, norm_mlp, w_qkv, b_qkv, w_o, b_o, sinks, w_pw1, b_pw1, w_dw, b_dw, conv_ln_g, conv_ln_b, w_pw2, b_pw2, w_up, w_down, final_norm, loss_target, m_w_mod, m_b_mod, m_norm_mix, m_norm_mlp, m_w_qkv, m_b_qkv, m_w_o, m_b_o, m_sinks, m_w_pw1, m_b_pw1, m_w_dw, m_b_dw, m_conv_ln_g, m_conv_ln_b, m_w_pw2, m_b_pw2, m_w_up, m_w_down, m_final_norm, v_w_mod, v_b_mod, v_norm_mix, v_norm_mlp, v_w_qkv, v_b_qkv, v_w_o, v_b_o, v_sinks, v_w_pw1, v_b_pw1, v_w_dw, v_b_dw, v_conv_ln_g, v_conv_ln_b, v_w_pw2, v_b_pw2, v_w_up, v_w_down, v_final_norm):
    given = dict(x=x, c=c, w_mod=w_mod, b_mod=b_mod, norm_mix=norm_mix, norm_mlp=norm_mlp, w_qkv=w_qkv, b_qkv=b_qkv, w_o=w_o, b_o=b_o, sinks=sinks, w_pw1=w_pw1, b_pw1=b_pw1, w_dw=w_dw, b_dw=b_dw, conv_ln_g=conv_ln_g, conv_ln_b=conv_ln_b, w_pw2=w_pw2, b_pw2=b_pw2, w_up=w_up, w_down=w_down, final_norm=final_norm, loss_target=loss_target, m_w_mod=m_w_mod, m_b_mod=m_b_mod, m_norm_mix=m_norm_mix, m_norm_mlp=m_norm_mlp, m_w_qkv=m_w_qkv, m_b_qkv=m_b_qkv, m_w_o=m_w_o, m_b_o=m_b_o, m_sinks=m_sinks, m_w_pw1=m_w_pw1, m_b_pw1=m_b_pw1, m_w_dw=m_w_dw, m_b_dw=m_b_dw, m_conv_ln_g=m_conv_ln_g, m_conv_ln_b=m_conv_ln_b, m_w_pw2=m_w_pw2, m_b_pw2=m_b_pw2, m_w_up=m_w_up, m_w_down=m_w_down, m_final_norm=m_final_norm, v_w_mod=v_w_mod, v_b_mod=v_b_mod, v_norm_mix=v_norm_mix, v_norm_mlp=v_norm_mlp, v_w_qkv=v_w_qkv, v_b_qkv=v_b_qkv, v_w_o=v_w_o, v_b_o=v_b_o, v_sinks=v_sinks, v_w_pw1=v_w_pw1, v_b_pw1=v_b_pw1, v_w_dw=v_w_dw, v_b_dw=v_b_dw, v_conv_ln_g=v_conv_ln_g, v_conv_ln_b=v_conv_ln_b, v_w_pw2=v_w_pw2, v_b_pw2=v_b_pw2, v_w_up=v_w_up, v_w_down=v_w_down, v_final_norm=v_final_norm)
    weights = {n: given[n] for n in TWIN_WEIGHTS}
    shared = {n: given[n] for n in SHARED_INPUTS}
    per_example = {n: given[n] for n in ['x', 'c']}
    grad_fn = _jax.value_and_grad(_loss, argnums=(0, 1))

    def one_microbatch(ex, loss_target):
        ex = dict(ex)
        diff = ex.pop(TWIN_DIFF_INPUT)
        return grad_fn(weights, diff, {**shared, **ex}, loss_target)

    if N_MICROBATCH == 1:
        loss, (grad_w, grad_x) = one_microbatch(per_example, given["loss_target"])
    else:
        def body(carry, xs):
            loss_sum, grad_sum = carry
            l_k, (gw_k, gx_k) = one_microbatch(xs[0], xs[1])
            with _jax.named_scope("update"):
                return (loss_sum + l_k, _jax.tree.map(_jnp.add, grad_sum, gw_k)), gx_k

        init = (_jnp.zeros((), _jnp.float32), _jax.tree.map(_jnp.zeros_like, weights))
        (loss, grad_w), grad_x = _jax.lax.scan(body, init, (per_example, given["loss_target"]))
    with _jax.named_scope("update"):
        delta_w, new_m, new_v = {}, {}, {}
        for n in TWIN_WEIGHTS:
            delta_w[n], new_m[n], new_v[n] = _adamw(weights[n], grad_w[n], given["m_" + n], given["v_" + n])
    return (loss, grad_x, *[grad_w[n] for n in TWIN_WEIGHTS], *[delta_w[n] for n in TWIN_WEIGHTS],
            *[new_m[n] for n in TWIN_WEIGHTS], *[new_v[n] for n in TWIN_WEIGHTS])
```

```python
import functools

import numpy as np
import jax
import jax.numpy as jnp
from jax import lax
from jax.experimental import pallas as pl
from jax.experimental.pallas import tpu as pltpu

F32 = jnp.float32
BF16 = jnp.bfloat16

N_DEV = 8
N_HEADS = 16
N_KV_HEADS = 2
HEAD_DIM = 64
GROUP = N_HEADS // N_KV_HEADS
ATT_BLOCK = 128
CONV_WIDTH = 31
CONV_HALO = 32
N_MOD = 6
EPS = 1e-6
ADAM_LR = 0.001
ADAM_B1 = 0.9
ADAM_B2 = 0.999
ADAM_EPS = 1e-08
ADAM_WD = 0.01
ADAM_STEP = 10
NEG_BIG = -1e30
PACK_COLS = 1024
VMEM_LIMIT_BYTES = 56 * 1024 * 1024
MESH_ID = pl.DeviceIdType.MESH


def _params(*sem):
    return pltpu.CompilerParams(dimension_semantics=sem, vmem_limit_bytes=VMEM_LIMIT_BYTES)


def _pick(n, pref, mult=8):
    if n <= pref:
        return n
    for t in range(pref, 0, -1):
        if n % t == 0 and t % mult == 0:
            return t
    return n


def _sigmoid(z):
    return 1.0 / (1.0 + jnp.exp(-z))


def _exchange(name, src, bcast):
    shape = src.shape if bcast else src.shape[1:]

    def body(src_ref, out_ref, send_sems, recv_sems, local_sem):
        x, y, c = lax.axis_index("x"), lax.axis_index("y"), lax.axis_index("c")
        me = 4 * x + 2 * y + c
        copies = []
        for k in range(1, N_DEV):
            px = 1 - x if (k >> 2) & 1 else x
            py = 1 - y if (k >> 1) & 1 else y
            pc = 1 - c if k & 1 else c
            peer = 4 * px + 2 * py + pc
            cp = pltpu.make_async_remote_copy(
                src_ref=src_ref if bcast else src_ref.at[peer],
                dst_ref=out_ref.at[me],
                send_sem=send_sems.at[k - 1],
                recv_sem=recv_sems.at[k - 1],
                device_id=(px, py, pc),
                device_id_type=MESH_ID,
            )
            cp.start()
            copies.append(cp)
        mine = pltpu.make_async_copy(src_ref if bcast else src_ref.at[me], out_ref.at[me], local_sem)
        mine.start()
        for cp in copies:
            cp.wait()
        mine.wait()

    return pl.pallas_call(
        body,
        name=name,
        out_shape=jax.ShapeDtypeStruct((N_DEV,) + tuple(shape), src.dtype),
        in_specs=[pl.BlockSpec(memory_space=pl.ANY)],
        out_specs=pl.BlockSpec(memory_space=pl.ANY),
        scratch_shapes=[
            pltpu.SemaphoreType.DMA((N_DEV - 1,)),
            pltpu.SemaphoreType.DMA((N_DEV - 1,)),
            pltpu.SemaphoreType.DMA,
        ],
    )(src)


def _pack(arrays, dtype, lead):
    lead_shape = arrays[0].shape[:lead]
    flat = [a.astype(dtype).reshape(lead_shape + (-1,)) for a in arrays]
    sizes = [f.shape[-1] for f in flat]
    total = sum(sizes)
    chunk = 16 * PACK_COLS
    padded = -(-total // chunk) * chunk
    if padded > total:
        flat.append(jnp.zeros(lead_shape + (padded - total,), dtype))
    buf = jnp.concatenate(flat, axis=-1)
    return buf.reshape(lead_shape + (padded // PACK_COLS, PACK_COLS)), sizes


def _unpack(buf, sizes, shapes, lead):
    lead_shape = buf.shape[:lead]
    flat = buf.reshape(lead_shape + (-1,))
    out, off = [], 0
    for n, shp in zip(sizes, shapes):
        out.append(lax.slice_in_dim(flat, off, off + n, axis=lead).reshape(lead_shape + tuple(shp)))
        off += n
    return out


def _to_slots(a, dim):
    shp = a.shape
    a = a.reshape(shp[:dim] + (N_DEV, shp[dim] // N_DEV) + shp[dim + 1:])
    return jnp.moveaxis(a, dim, 0)


def _from_slots(a, dim):
    a = jnp.moveaxis(a, 0, dim)
    shp = a.shape
    return a.reshape(shp[:dim] + (shp[dim] * shp[dim + 1],) + shp[dim + 2:])


def _mod_fwd(c_all, w, b):
    L, D, n = w.shape
    M = c_all.shape[0]

    def body(c_ref, w_ref, b_ref, o_ref):
        cv = c_ref[...]
        cs = (cv * _sigmoid(cv)).astype(BF16)
        o_ref[...] = jnp.dot(cs, w_ref[...], preferred_element_type=F32) + b_ref[...]

    return pl.pallas_call(
        body,
        name="mod_fwd",
        grid=(L,),
        out_shape=jax.ShapeDtypeStruct((L, M, n), F32),
        in_specs=[
            pl.BlockSpec((M, D), lambda l: (0, 0)),
            pl.BlockSpec((None, D, n), lambda l: (l, 0, 0)),
            pl.BlockSpec((None, 1, n), lambda l: (l, 0, 0)),
        ],
        out_specs=pl.BlockSpec((None, M, n), lambda l: (l, 0, 0)),
        compiler_params=_params("arbitrary"),
    )(c_all, w, b)


def _mod_bwd(c_all, dmod_all):
    L, M, n = dmod_all.shape
    D = c_all.shape[1]

    def body(c_ref, d_ref, dw_ref, db_ref):
        cv = c_ref[...]
        cs = (cv * _sigmoid(cv)).astype(BF16)
        d = d_ref[...]
        dw_ref[...] = lax.dot_general(cs, d.astype(BF16), (((0,), (0,)), ((), ())), preferred_element_type=F32)
        db_ref[...] = jnp.sum(d, axis=0, keepdims=True)

    return pl.pallas_call(
        body,
        name="mod_bwd",
        grid=(L,),
        out_shape=(jax.ShapeDtypeStruct((L, D, n), F32), jax.ShapeDtypeStruct((L, 1, n), F32)),
        in_specs=[
            pl.BlockSpec((M, D), lambda l: (0, 0)),
            pl.BlockSpec((None, M, n), lambda l: (l, 0, 0)),
        ],
        out_specs=(
            pl.BlockSpec((None, D, n), lambda l: (l, 0, 0)),
            pl.BlockSpec((None, 1, n), lambda l: (l, 0, 0)),
        ),
        compiler_params=_params("arbitrary"),
    )(c_all, dmod_all)


def _mm(name, a, w, *, out_dtype=BF16, bias=None, a_pro=None, epi=None, x=None, gate=None, u=None,
        rows_per_batch=None, tm_pref=512):
    M, K = a.shape
    N = w.shape[1]
    tm = _pick(M if rows_per_batch is None else rows_per_batch, tm_pref, 16)
    nc = _pick(N, 1024, 128)
    n_chunks = N // nc
    has_bias = bias is not None

    def body(*refs):
        it = iter(refs)
        a_ref = next(it)
        w_ref = next(it)
        b_ref = next(it) if has_bias else None
        x_ref = next(it) if epi == "resid" else None
        g_ref = next(it) if epi == "resid" else None
        u_ref = next(it) if epi == "relu2d" else None
        o_ref = next(it)
        y_ref = next(it) if epi == "resid" else None
        av = a_ref[...]
        if a_pro == "relu2":
            r = jnp.maximum(av.astype(F32), 0.0)
            av = (r * r).astype(BF16)
        for ci in range(n_chunks):
            cols = slice(ci * nc, (ci + 1) * nc)
            acc = jnp.dot(av, w_ref[:, cols], preferred_element_type=F32)
            if has_bias:
                acc = acc + b_ref[:, cols]
            if epi == "resid":
                o_ref[:, cols] = x_ref[:, cols] + g_ref[:, cols] * acc
                y_ref[:, cols] = acc.astype(BF16)
            elif epi == "relu2d":
                o_ref[:, cols] = (acc * (2.0 * jnp.maximum(u_ref[:, cols].astype(F32), 0.0))).astype(out_dtype)
            else:
                o_ref[:, cols] = acc.astype(out_dtype)

    args = [a, w]
    specs = [pl.BlockSpec((tm, K), lambda i: (i, 0)), pl.BlockSpec((K, N), lambda i: (0, 0))]
    if has_bias:
        args.append(bias.reshape(1, N).astype(F32))
        specs.append(pl.BlockSpec((1, N), lambda i: (0, 0)))
    row_spec = pl.BlockSpec((tm, N), lambda i: (i, 0))
    if epi == "resid":
        tpb = rows_per_batch // tm
        args += [x, gate]
        specs += [row_spec, pl.BlockSpec((None, 1, N), lambda i: (i // tpb, 0, 0))]
        out_shape = (jax.ShapeDtypeStruct((M, N), F32), jax.ShapeDtypeStruct((M, N), BF16))
        out_specs = (row_spec, row_spec)
    else:
        if epi == "relu2d":
            args.append(u)
            specs.append(row_spec)
        out_shape = jax.ShapeDtypeStruct((M, N), out_dtype)
        out_specs = row_spec
    return pl.pallas_call(
        body,
        name=name,
        grid=(M // tm,),
        out_shape=out_shape,
        in_specs=specs,
        out_specs=out_specs,
        compiler_params=_params("parallel"),
    )(*args)


def _mm_tn(name, a, b, *, a_pro=None, tt_pref=1024, tk_pref=1024, tn_pref=1024):
    T, K = a.shape
    N = b.shape[1]
    tt = _pick(T, tt_pref, 16)
    tk = _pick(K, tk_pref, 128)
    tn = _pick(N, tn_pref, 128)

    def body(a_ref, b_ref, o_ref):
        @pl.when(pl.program_id(2) == 0)
        def _():
            o_ref[...] = jnp.zeros_like(o_ref)

        av = a_ref[...]
        if a_pro == "relu2":
            r = jnp.maximum(av.astype(F32), 0.0)
            av = (r * r).astype(BF16)
        o_ref[...] += lax.dot_general(av, b_ref[...], (((0,), (0,)), ((), ())), preferred_element_type=F32)

    return pl.pallas_call(
        body,
        name=name,
        grid=(K // tk, N // tn, T // tt),
        out_shape=jax.ShapeDtypeStruct((K, N), F32),
        in_specs=[
            pl.BlockSpec((tt, tk), lambda k, n, t: (t, k)),
            pl.BlockSpec((tt, tn), lambda k, n, t: (t, n)),
        ],
        out_specs=pl.BlockSpec((tk, tn), lambda k, n, t: (k, n)),
        compiler_params=_params("parallel", "parallel", "arbitrary"),
    )(a, b)


def _colsum(name, a, tm_pref=1024):
    T, N = a.shape
    tm = _pick(T, tm_pref, 16)

    def body(a_ref, o_ref):
        @pl.when(pl.program_id(0) == 0)
        def _():
            o_ref[...] = jnp.zeros_like(o_ref)

        o_ref[...] += jnp.sum(a_ref[...].astype(F32), axis=0, keepdims=True)

    return pl.pallas_call(
        body,
        name=name,
        grid=(T // tm,),
        out_shape=jax.ShapeDtypeStruct((1, N), F32),
        in_specs=[pl.BlockSpec((tm, N), lambda i: (i, 0))],
        out_specs=pl.BlockSpec((1, N), lambda i: (0, 0)),
        compiler_params=_params("arbitrary"),
    )(a)


def _normmod_fwd(name, x, gnorm, sc, sh, S, tm_pref=512):
    T, D = x.shape
    tm = _pick(S, tm_pref, 16)
    tpb = S // tm

    def body(x_ref, g_ref, sc_ref, sh_ref, o_ref):
        xv = x_ref[...]
        r = lax.rsqrt(jnp.mean(xv * xv, axis=-1, keepdims=True) + EPS)
        n = xv * r * g_ref[...]
        o_ref[...] = (n * (1.0 + sc_ref[...]) + sh_ref[...]).astype(BF16)

    row = pl.BlockSpec((tm, D), lambda i: (i, 0))
    vec = pl.BlockSpec((None, 1, D), lambda i: (i // tpb, 0, 0))
    return pl.pallas_call(
        body,
        name=name,
        grid=(T // tm,),
        out_shape=jax.ShapeDtypeStruct((T, D), BF16),
        in_specs=[row, pl.BlockSpec((1, D), lambda i: (0, 0)), vec, vec],
        out_specs=row,
        compiler_params=_params("parallel"),
    )(x, gnorm, sc, sh)


def _normmod_bwd(name, dh, x, dx_in, gnorm, sc, S, tm_pref=256):
    T, D = x.shape
    B = T // S
    tm = _pick(S, tm_pref, 16)
    tpb = S // tm

    def body(dh_ref, x_ref, dxin_ref, g_ref, sc_ref, dx_ref, dsh_ref, dsc_ref, dg_ref):
        b, j = pl.program_id(0), pl.program_id(1)

        @pl.when(j == 0)
        def _():
            dsh_ref[...] = jnp.zeros_like(dsh_ref)
            dsc_ref[...] = jnp.zeros_like(dsc_ref)

        @pl.when((b == 0) & (j == 0))
        def _():
            dg_ref[...] = jnp.zeros_like(dg_ref)

        xv = x_ref[...]
        g = g_ref[...]
        r = lax.rsqrt(jnp.mean(xv * xv, axis=-1, keepdims=True) + EPS)
        xhat = xv * r
        n = xhat * g
        dh_v = dh_ref[...]
        dsh_ref[...] += jnp.sum(dh_v, axis=0, keepdims=True)
        dsc_ref[...] += jnp.sum(dh_v * n, axis=0, keepdims=True)
        dn = dh_v * (1.0 + sc_ref[...])
        dg_ref[...] += jnp.sum(dn * xhat, axis=0, keepdims=True)
        dxhat = dn * g
        dx_ref[...] = dxin_ref[...] + r * (dxhat - xhat * jnp.mean(dxhat * xhat, axis=-1, keepdims=True))

    row = pl.BlockSpec((tm, D), lambda b, j: (b * tpb + j, 0))
    vec = pl.BlockSpec((None, 1, D), lambda b, j: (b, 0, 0))
    one = pl.BlockSpec((1, D), lambda b, j: (0, 0))
    return pl.pallas_call(
        body,
        name=name,
        grid=(B, tpb),
        out_shape=(
            jax.ShapeDtypeStruct((T, D), F32),
            jax.ShapeDtypeStruct((B, 1, D), F32),
            jax.ShapeDtypeStruct((B, 1, D), F32),
            jax.ShapeDtypeStruct((1, D), F32),
        ),
        in_specs=[row, row, row, one, vec],
        out_specs=(row, vec, vec, one),
        compiler_params=_params("arbitrary", "arbitrary"),
    )(dh, x, dx_in, gnorm, sc)


def _gate_bwd(name, dx, y, gate, S, tm_pref=512):
    T, D = dx.shape
    B = T // S
    tm = _pick(S, tm_pref, 16)
    tpb = S // tm

    def body(dx_ref, y_ref, g_ref, dy_ref, dgate_ref, cs_ref):
        b, j = pl.program_id(0), pl.program_id(1)

        @pl.when(j == 0)
        def _():
            dgate_ref[...] = jnp.zeros_like(dgate_ref)

        @pl.when((b == 0) & (j == 0))
        def _():
            cs_ref[...] = jnp.zeros_like(cs_ref)

        dxv = dx_ref[...]
        dy = dxv * g_ref[...]
        dy_ref[...] = dy.astype(BF16)
        dgate_ref[...] += jnp.sum(dxv * y_ref[...].astype(F32), axis=0, keepdims=True)
        cs_ref[...] += jnp.sum(dy, axis=0, keepdims=True)

    row = pl.BlockSpec((tm, D), lambda b, j: (b * tpb + j, 0))
    vec = pl.BlockSpec((None, 1, D), lambda b, j: (b, 0, 0))
    one = pl.BlockSpec((1, D), lambda b, j: (0, 0))
    return pl.pallas_call(
        body,
        name=name,
        grid=(B, tpb),
        out_shape=(
            jax.ShapeDtypeStruct((T, D), BF16),
            jax.ShapeDtypeStruct((B, 1, D), F32),
            jax.ShapeDtypeStruct((1, D), F32),
        ),
        in_specs=[row, row, vec],
        out_specs=(row, vec, one),
        compiler_params=_params("arbitrary", "arbitrary"),
    )(dx, y, gate)


def _attn_probs(qg, kc, kp, slope, sink, dist_c, mask_c, mask_p):
    scale = HEAD_DIM ** -0.5
    nt = (((1,), (1,)), ((), ()))
    s_c = lax.dot_general(qg, kc, nt, preferred_element_type=F32) * scale - slope * dist_c
    s_p = lax.dot_general(qg, kp, nt, preferred_element_type=F32) * scale - slope * (dist_c + float(ATT_BLOCK))
    s_c = jnp.where(mask_c, s_c, NEG_BIG)
    s_p = jnp.where(mask_p, s_p, NEG_BIG)
    m = jnp.maximum(jnp.maximum(jnp.max(s_c, axis=-1, keepdims=True), jnp.max(s_p, axis=-1, keepdims=True)), sink)
    e_c = jnp.exp(s_c - m)
    e_p = jnp.exp(s_p - m)
    e_s = jnp.exp(sink - m)
    inv = 1.0 / (jnp.sum(e_c, axis=-1, keepdims=True) + jnp.sum(e_p, axis=-1, keepdims=True) + e_s)
    return e_c * inv, e_p * inv, e_s * inv


def _attn_masks(i):
    r = lax.broadcasted_iota(jnp.int32, (ATT_BLOCK, ATT_BLOCK), 0)
    c = lax.broadcasted_iota(jnp.int32, (ATT_BLOCK, ATT_BLOCK), 1)
    dist_c = (r - c).astype(F32)
    mask_c = r >= c
    mask_p = c > r + jnp.where(i > 0, 0, 2 * ATT_BLOCK)
    return dist_c, mask_c, mask_p


def _attn_fwd(q, k, v, slope_b, sink_b):
    B, H, S, hd = q.shape
    nb = S // ATT_BLOCK

    def body(q_ref, kc_ref, kp_ref, vc_ref, vp_ref, sl_ref, sk_ref, o_ref):
        i = pl.program_id(2)
        dist_c, mask_c, mask_p = _attn_masks(i)
        kc, kp, vc, vp = kc_ref[...], kp_ref[...], vc_ref[...], vp_ref[...]
        for g in range(GROUP):
            slope = sl_ref[g:g + 1, 0:1]
            sink = sk_ref[g:g + 1, 0:1]
            p_c, p_p, _ = _attn_probs(q_ref[g], kc, kp, slope, sink, dist_c, mask_c, mask_p)
            o = jnp.dot(p_c.astype(BF16), vc, preferred_element_type=F32)
            o = o + jnp.dot(p_p.astype(BF16), vp, preferred_element_type=F32)
            o_ref[g] = o.astype(BF16)

    qspec = pl.BlockSpec((None, GROUP, ATT_BLOCK, hd), lambda b, h, i: (b, h, i, 0))
    cur = pl.BlockSpec((None, None, ATT_BLOCK, hd), lambda b, h, i: (b, h, i, 0))
    prev = pl.BlockSpec((None, None, ATT_BLOCK, hd), lambda b, h, i: (b, h, jnp.maximum(i - 1, 0), 0))
    hvec = pl.BlockSpec((GROUP, 128), lambda b, h, i: (h, 0))
    return pl.pallas_call(
        body,
        name="attn_fwd",
        grid=(B, N_KV_HEADS, nb),
        out_shape=jax.ShapeDtypeStruct((B, H, S, hd), BF16),
        in_specs=[qspec, cur, prev, cur, prev, hvec, hvec],
        out_specs=qspec,
        compiler_params=_params("parallel", "parallel", "arbitrary"),
    )(q, k, k, v, v, slope_b, sink_b)


def _attn_bwd(q, k, v, o, do, slope_b, sink_b):
    B, H, S, hd = q.shape
    nb = S // ATT_BLOCK
    scale = HEAD_DIM ** -0.5
    tn = (((0,), (0,)), ((), ()))
    nt = (((1,), (1,)), ((), ()))

    def body(q_ref, o_ref, do_ref, kc_ref, kp_ref, vc_ref, vp_ref, sl_ref, sk_ref,
             dq_ref, dk_ref, dv_ref, dsink_ref, carry_k, carry_v):
        i = pl.program_id(2)

        @pl.when(i == 0)
        def _():
            dsink_ref[...] = jnp.zeros_like(dsink_ref)

        @pl.when(i < nb)
        def _():
            dist_c, mask_c, mask_p = _attn_masks(i)
            kc, kp, vc, vp = kc_ref[...], kp_ref[...], vc_ref[...], vp_ref[...]
            dk_c = jnp.zeros((ATT_BLOCK, hd), F32)
            dk_p = jnp.zeros((ATT_BLOCK, hd), F32)
            dv_c = jnp.zeros((ATT_BLOCK, hd), F32)
            dv_p = jnp.zeros((ATT_BLOCK, hd), F32)
            for g in range(GROUP):
                slope = sl_ref[g:g + 1, 0:1]
                sink = sk_ref[g:g + 1, 0:1]
                qg = q_ref[g]
                dog = do_ref[g]
                p_c, p_p, p_s = _attn_probs(qg, kc, kp, slope, sink, dist_c, mask_c, mask_p)
                delta = jnp.sum(dog.astype(F32) * o_ref[g].astype(F32), axis=-1, keepdims=True)
                dp_c = lax.dot_general(dog, vc, nt, preferred_element_type=F32)
                dp_p = lax.dot_general(dog, vp, nt, preferred_element_type=F32)
                ds_c = (p_c * (dp_c - delta)).astype(BF16)
                ds_p = (p_p * (dp_p - delta)).astype(BF16)
                dq = jnp.dot(ds_c, kc, preferred_element_type=F32) + jnp.dot(ds_p, kp, preferred_element_type=F32)
                dq_ref[g] = (dq * scale).astype(BF16)
                dk_c = dk_c + lax.dot_general(ds_c, qg, tn, preferred_element_type=F32)
                dk_p = dk_p + lax.dot_general(ds_p, qg, tn, preferred_element_type=F32)
                dv_c = dv_c + lax.dot_general(p_c.astype(BF16), dog, tn, preferred_element_type=F32)
                dv_p = dv_p + lax.dot_general(p_p.astype(BF16), dog, tn, preferred_element_type=F32)
                dsk = -jnp.sum(p_s * delta, axis=0, keepdims=True)
                dsink_ref[g:g + 1, :] += jnp.broadcast_to(dsk, (1, 128))

            @pl.when(i > 0)
            def _():
                dk_ref[...] = (carry_k[...] + dk_p * scale).astype(BF16)
                dv_ref[...] = (carry_v[...] + dv_p).astype(BF16)

            carry_k[...] = dk_c * scale
            carry_v[...] = dv_c

        @pl.when(i == nb)
        def _():
            dk_ref[...] = carry_k[...].astype(BF16)
            dv_ref[...] = carry_v[...].astype(BF16)

    last = nb - 1
    qspec = pl.BlockSpec((None, GROUP, ATT_BLOCK, hd), lambda b, h, i: (b, h, jnp.minimum(i, last), 0))
    cur = pl.BlockSpec((None, None, ATT_BLOCK, hd), lambda b, h, i: (b, h, jnp.minimum(i, last), 0))
    prev = pl.BlockSpec((None, None, ATT_BLOCK, hd), lambda b, h, i: (b, h, jnp.clip(i - 1, 0, last), 0))
    hvec = pl.BlockSpec((GROUP, 128), lambda b, h, i: (h, 0))
    dkv = pl.BlockSpec((None, None, ATT_BLOCK, hd), lambda b, h, i: (b, h, jnp.maximum(i - 1, 0), 0))
    dsk = pl.BlockSpec((None, None, GROUP, 128), lambda b, h, i: (b, h, 0, 0))
    return pl.pallas_call(
        body,
        name="attn_bwd",
        grid=(B, N_KV_HEADS, nb + 1),
        out_shape=(
            jax.ShapeDtypeStruct((B, H, S, hd), BF16),
            jax.ShapeDtypeStruct((B, N_KV_HEADS, S, hd), BF16),
            jax.ShapeDtypeStruct((B, N_KV_HEADS, S, hd), BF16),
            jax.ShapeDtypeStruct((B, N_KV_HEADS, GROUP, 128), F32),
        ),
        in_specs=[qspec, qspec, qspec, cur, prev, cur, prev, hvec, hvec],
        out_specs=(qspec, dkv, dkv, dsk),
        scratch_shapes=[pltpu.VMEM((ATT_BLOCK, hd), F32), pltpu.VMEM((ATT_BLOCK, hd), F32)],
        compiler_params=_params("parallel", "parallel", "arbitrary"),
    )(q, o, do, k, k, v, v, slope_b, sink_b)


def _conv_tiles(S):
    ts = _pick(S, 256, CONV_HALO)
    return ts, S // ts


def _conv_chunks(C, ts):
    lane = _pick(C, 128, 128)
    return lane, C // lane, _pick(ts, 64, 8)


def _conv_weight_chunks(w_dw, C):
    lane = _pick(C, 128, 128)
    w = jnp.pad(w_dw, ((0, CONV_HALO - CONV_WIDTH), (0, 0)))
    return w.reshape(CONV_HALO, C // lane, lane).transpose(1, 0, 2)


def _conv_fwd(u, w3, b_dw, ln_g, ln_b, S):
    T, C2 = u.shape
    C = C2 // 2
    B = T // S
    ts, nj = _conv_tiles(S)
    hb = ts // CONV_HALO
    lane, nc, rc = _conv_chunks(C, ts)

    def body(a_ref, g_ref, ap_ref, gp_ref, w_ref, bdw_ref, lg_ref, lb_ref, cv_ref, s_ref, buf, cvb):
        j = pl.program_id(1)
        glu_prev = ap_ref[...] * _sigmoid(gp_ref[...]) * (j > 0).astype(F32)
        glu = a_ref[...] * _sigmoid(g_ref[...])
        for cc in range(nc):
            buf[cc, 0:CONV_HALO, :] = glu_prev[:, cc * lane:(cc + 1) * lane]
            buf[cc, CONV_HALO:, :] = glu[:, cc * lane:(cc + 1) * lane]

        def chunk(cc, carry):
            for r0 in range(0, ts, rc):
                acc = jnp.zeros((rc, lane), F32)
                for kk in range(CONV_WIDTH):
                    lo = CONV_HALO - (CONV_WIDTH - 1 - kk) + r0
                    acc = acc + w_ref[cc, kk:kk + 1, :] * buf[cc, lo:lo + rc, :]
                cvb[cc, r0:r0 + rc, :] = acc
            return carry

        lax.fori_loop(0, nc, chunk, 0)
        for cc in range(nc):
            cv_ref[:, cc * lane:(cc + 1) * lane] = cvb[cc] + bdw_ref[:, cc * lane:(cc + 1) * lane]
        cv = cv_ref[...]
        mu = jnp.mean(cv, axis=-1, keepdims=True)
        xc = cv - mu
        rstd = lax.rsqrt(jnp.mean(xc * xc, axis=-1, keepdims=True) + EPS)
        ln = xc * rstd * lg_ref[...] + lb_ref[...]
        s_ref[...] = (ln * _sigmoid(ln)).astype(BF16)

    a_cur = pl.BlockSpec((ts, C), lambda b, j: (b * nj + j, 0))
    g_cur = pl.BlockSpec((ts, C), lambda b, j: (b * nj + j, 1))
    a_prev = pl.BlockSpec((CONV_HALO, C), lambda b, j: (jnp.maximum((b * nj + j) * hb - 1, 0), 0))
    g_prev = pl.BlockSpec((CONV_HALO, C), lambda b, j: (jnp.maximum((b * nj + j) * hb - 1, 0), 1))
    wspec = pl.BlockSpec((nc, CONV_HALO, lane), lambda b, j: (0, 0, 0))
    one = pl.BlockSpec((1, C), lambda b, j: (0, 0))
    return pl.pallas_call(
        body,
        name="conv_fwd",
        grid=(B, nj),
        out_shape=(jax.ShapeDtypeStruct((T, C), F32), jax.ShapeDtypeStruct((T, C), BF16)),
        in_specs=[a_cur, g_cur, a_prev, g_prev, wspec, one, one, one],
        out_specs=(a_cur, a_cur),
        scratch_shapes=[pltpu.VMEM((nc, CONV_HALO + ts, lane), F32), pltpu.VMEM((nc, ts, lane), F32)],
        compiler_params=_params("parallel", "arbitrary"),
    )(u, u, u, u, w3, b_dw, ln_g, ln_b)


def _lnsilu_bwd(ds, cv, ln_g, ln_b, tm_pref=256):
    T, C = cv.shape
    tm = _pick(T, tm_pref, 16)

    def body(ds_ref, cv_ref, lg_ref, lb_ref, dcv_ref, dlg_ref, dlb_ref, dbdw_ref):
        @pl.when(pl.program_id(0) == 0)
        def _():
            dlg_ref[...] = jnp.zeros_like(dlg_ref)
            dlb_ref[...] = jnp.zeros_like(dlb_ref)
            dbdw_ref[...] = jnp.zeros_like(dbdw_ref)

        cv_v = cv_ref[...]
        g = lg_ref[...]
        mu = jnp.mean(cv_v, axis=-1, keepdims=True)
        xc = cv_v - mu
        rstd = lax.rsqrt(jnp.mean(xc * xc, axis=-1, keepdims=True) + EPS)
        xhat = xc * rstd
        ln = xhat * g + lb_ref[...]
        sg = _sigmoid(ln)
        dln = ds_ref[...] * (sg * (1.0 + ln * (1.0 - sg)))
        dlg_ref[...] += jnp.sum(dln * xhat, axis=0, keepdims=True)
        dlb_ref[...] += jnp.sum(dln, axis=0, keepdims=True)
        dxhat = dln * g
        dcv = rstd * (dxhat - jnp.mean(dxhat, axis=-1, keepdims=True)
                      - xhat * jnp.mean(dxhat * xhat, axis=-1, keepdims=True))
        dcv_ref[...] = dcv
        dbdw_ref[...] += jnp.sum(dcv, axis=0, keepdims=True)

    row = pl.BlockSpec((tm, C), lambda i: (i, 0))
    one = pl.BlockSpec((1, C), lambda i: (0, 0))
    return pl.pallas_call(
        body,
        name="lnsilu_bwd",
        grid=(T // tm,),
        out_shape=(jax.ShapeDtypeStruct((T, C), F32),) + (jax.ShapeDtypeStruct((1, C), F32),) * 3,
        in_specs=[row, row, one, one],
        out_specs=(row, one, one, one),
        compiler_params=_params("arbitrary"),
    )(ds, cv, ln_g, ln_b)


def _conv_bwd(dcv, u, w3, S):
    T, C2 = u.shape
    C = C2 // 2
    B = T // S
    ts, nj = _conv_tiles(S)
    hb = ts // CONV_HALO
    n_halo_blocks = T // CONV_HALO
    lane, nc, rc = _conv_chunks(C, ts)

    def body(dcv_ref, dnx_ref, a_ref, g_ref, ap_ref, gp_ref, w_ref, du_ref, dw_ref, gbuf, dbuf, dglu, dw8):
        b, j = pl.program_id(0), pl.program_id(1)

        @pl.when((b == 0) & (j == 0))
        def _():
            dw8[...] = jnp.zeros_like(dw8)

        a = a_ref[...]
        sg = _sigmoid(g_ref[...])
        glu_prev = ap_ref[...] * _sigmoid(gp_ref[...]) * (j > 0).astype(F32)
        glu = a * sg
        dcur = dcv_ref[...]
        dnext = dnx_ref[...] * (j < nj - 1).astype(F32)
        for cc in range(nc):
            cols = slice(cc * lane, (cc + 1) * lane)
            gbuf[cc, 0:CONV_HALO, :] = glu_prev[:, cols]
            gbuf[cc, CONV_HALO:, :] = glu[:, cols]
            dbuf[cc, 0:ts, :] = dcur[:, cols]
            dbuf[cc, ts:, :] = dnext[:, cols]

        def chunk(cc, carry):
            for r0 in range(0, ts, rc):
                acc = jnp.zeros((rc, lane), F32)
                for kk in range(CONV_WIDTH):
                    d = CONV_WIDTH - 1 - kk
                    acc = acc + w_ref[cc, kk:kk + 1, :] * dbuf[cc, r0 + d:r0 + d + rc, :]
                dglu[cc, r0:r0 + rc, :] = acc
            for kk in range(CONV_WIDTH):
                d = CONV_WIDTH - 1 - kk
                p = jnp.zeros((rc, lane), F32)
                for r0 in range(0, ts, rc):
                    lo = CONV_HALO - d + r0
                    p = p + dbuf[cc, r0:r0 + rc, :] * gbuf[cc, lo:lo + rc, :]
                dw8[cc, kk * 8:(kk + 1) * 8, :] += jnp.sum(p.reshape(rc // 8, 8, lane), axis=0)
            return carry

        lax.fori_loop(0, nc, chunk, 0)
        for cc in range(nc):
            cols = slice(cc * lane, (cc + 1) * lane)
            dgl = dglu[cc]
            du_ref[:, cc * lane:(cc + 1) * lane] = (dgl * sg[:, cols]).astype(BF16)
            du_ref[:, C + cc * lane:C + (cc + 1) * lane] = (dgl * a[:, cols] * sg[:, cols] * (1.0 - sg[:, cols])).astype(BF16)

        @pl.when((b == B - 1) & (j == nj - 1))
        def _():
            dw_ref[...] = jnp.zeros_like(dw_ref)
            for kk in range(CONV_WIDTH):
                dw_ref[:, kk:kk + 1, :] = jnp.sum(dw8[:, kk * 8:(kk + 1) * 8, :], axis=1, keepdims=True)

    a_cur = pl.BlockSpec((ts, C), lambda b, j: (b * nj + j, 0))
    g_cur = pl.BlockSpec((ts, C), lambda b, j: (b * nj + j, 1))
    a_prev = pl.BlockSpec((CONV_HALO, C), lambda b, j: (jnp.maximum((b * nj + j) * hb - 1, 0), 0))
    g_prev = pl.BlockSpec((CONV_HALO, C), lambda b, j: (jnp.maximum((b * nj + j) * hb - 1, 0), 1))
    d_next = pl.BlockSpec((CONV_HALO, C), lambda b, j: (jnp.minimum((b * nj + j + 1) * hb, n_halo_blocks - 1), 0))
    wspec = pl.BlockSpec((nc, CONV_HALO, lane), lambda b, j: (0, 0, 0))
    return pl.pallas_call(
        body,
        name="conv_bwd",
        grid=(B, nj),
        out_shape=(jax.ShapeDtypeStruct((T, C2), BF16), jax.ShapeDtypeStruct((nc, CONV_HALO, lane), F32)),
        in_specs=[a_cur, d_next, a_cur, g_cur, a_prev, g_prev, wspec],
        out_specs=(pl.BlockSpec((ts, C2), lambda b, j: (b * nj + j, 0)), wspec),
        scratch_shapes=[
            pltpu.VMEM((nc, CONV_HALO + ts, lane), F32),
            pltpu.VMEM((nc, ts + CONV_HALO, lane), F32),
            pltpu.VMEM((nc, ts, lane), F32),
            pltpu.VMEM((nc, CONV_HALO * 8, lane), F32),
        ],
        compiler_params=_params("arbitrary", "arbitrary"),
    )(dcv, dcv, u, u, u, u, w3)


def _loss_head(x, tgt, gfin, tm_pref=256):
    T, D = x.shape
    tm = _pick(T, tm_pref, 16)

    def body(x_ref, t_ref, g_ref, dx_ref, loss_ref, dg_ref):
        @pl.when(pl.program_id(0) == 0)
        def _():
            loss_ref[...] = jnp.zeros_like(loss_ref)
            dg_ref[...] = jnp.zeros_like(dg_ref)

        xv = x_ref[...]
        g = g_ref[...]
        r = lax.rsqrt(jnp.mean(xv * xv, axis=-1, keepdims=True) + EPS)
        xhat = xv * r
        e = xhat * g - t_ref[...]
        row_loss = jnp.mean(e * e, axis=-1, keepdims=True)
        loss_ref[...] += 0.5 * jnp.sum(row_loss, axis=0, keepdims=True)
        dy = e * (1.0 / D)
        dg_ref[...] += jnp.sum(dy * xhat, axis=0, keepdims=True)
        dxhat = dy * g
        dx_ref[...] = r * (dxhat - xhat * jnp.mean(dxhat * xhat, axis=-1, keepdims=True))

    row = pl.BlockSpec((tm, D), lambda i: (i, 0))
    return pl.pallas_call(
        body,
        name="loss_head",
        grid=(T // tm,),
        out_shape=(
            jax.ShapeDtypeStruct((T, D), F32),
            jax.ShapeDtypeStruct((8, 128), F32),
            jax.ShapeDtypeStruct((1, D), F32),
        ),
        in_specs=[row, row, pl.BlockSpec((1, D), lambda i: (0, 0))],
        out_specs=(row, pl.BlockSpec((8, 128), lambda i: (0, 0)), pl.BlockSpec((1, D), lambda i: (0, 0))),
        compiler_params=_params("arbitrary"),
    )(x, tgt, gfin)


def _adam(name, parts, w, m, v, tm_pref=256):
    P, R, C = parts.shape
    tm = _pick(R, tm_pref, 16)
    c1 = 1.0 - ADAM_B1 ** ADAM_STEP
    c2 = 1.0 - ADAM_B2 ** ADAM_STEP

    def body(p_ref, w_ref, m_ref, v_ref, g_ref, d_ref, mo_ref, vo_ref):
        g = p_ref[0].astype(F32)
        for i in range(1, P):
            g = g + p_ref[i].astype(F32)
        m_new = ADAM_B1 * m_ref[...] + (1.0 - ADAM_B1) * g
        v_new = ADAM_B2 * v_ref[...] + (1.0 - ADAM_B2) * (g * g)
        m_hat = m_new / c1
        v_hat = v_new / c2
        g_ref[...] = g
        d_ref[...] = -ADAM_LR * (m_hat / (jnp.sqrt(v_hat) + ADAM_EPS) + ADAM_WD * w_ref[...])
        mo_ref[...] = m_new
        vo_ref[...] = v_new

    row = pl.BlockSpec((tm, C), lambda i: (i, 0))
    return pl.pallas_call(
        body,
        name=name,
        grid=(R // tm,),
        out_shape=(jax.ShapeDtypeStruct((R, C), F32),) * 4,
        in_specs=[pl.BlockSpec((P, tm, C), lambda i: (0, i, 0)), row, row, row],
        out_specs=(row, row, row, row),
        compiler_params=_params("parallel"),
    )(parts, w, m, v)


def _heads(a, B, S, n):
    return a.reshape(B, S, n, HEAD_DIM).transpose(0, 2, 1, 3)


def _tokens(a):
    B, n, S, hd = a.shape
    return a.transpose(0, 2, 1, 3).reshape(B * S, n * hd)


BIG = (("w_qkv", 2), ("w_o", 1), ("w_pw1", 2), ("w_pw2", 1), ("w_up", 2), ("w_down", 1))
SMALL_SHARDED = (("b_pw1", 1), ("w_dw", 2), ("b_dw", 1), ("conv_ln_g", 1), ("conv_ln_b", 1), ("b_pw2", 1))
SMALL_REPL = ("b_mod", "norm_mix", "norm_mlp", "b_qkv", "b_o", "sinks", "final_norm")
WEIGHTS = ("w_mod", "b_mod", "norm_mix", "norm_mlp", "w_qkv", "b_qkv", "w_o", "b_o", "sinks", "w_pw1", "b_pw1",
           "w_dw", "b_dw", "conv_ln_g", "conv_ln_b", "w_pw2", "b_pw2", "w_up", "w_down", "final_norm")


def _step(x, c, loss_target, W, M, V):
    B, S, D = x.shape
    T = B * S
    L = W["w_mod"].shape[0]
    n_mod = W["w_mod"].shape[2]
    me = 4 * lax.axis_index("x") + 2 * lax.axis_index("y") + lax.axis_index("c")
    q_dim = N_HEADS * HEAD_DIM
    kv_dim = N_KV_HEADS * HEAD_DIM

    small_names = [n for n, _ in SMALL_SHARDED]
    small_src, small_sizes = _pack([c] + [W[n] for n in small_names], F32, 0)
    small_all = _exchange("gather_small", small_src, True)
    small_parts = _unpack(small_all, small_sizes, [c.shape] + [W[n].shape for n in small_names], 1)
    c_all = small_parts[0].reshape(N_DEV * B, D)
    full = {n: _from_slots(p, d) for (n, d), p in zip(SMALL_SHARDED, small_parts[1:])}

    big_src, big_sizes = _pack([W[n] for n, _ in BIG], BF16, 0)
    big_all = _exchange("gather_big", big_src, True)
    big_parts = _unpack(big_all, big_sizes, [W[n].shape for n, _ in BIG], 1)
    for (n, d), p in zip(BIG, big_parts):
        full[n] = _from_slots(p, d)

    b_mod_mine = lax.dynamic_slice_in_dim(W["b_mod"], me * n_mod, n_mod, axis=1).reshape(L, 1, n_mod)
    mod_part = _mod_fwd(c_all, W["w_mod"].astype(BF16), b_mod_mine)
    mod_slots = mod_part.reshape(L, N_DEV, B, n_mod).transpose(1, 0, 2, 3).reshape(N_DEV, L * B, n_mod)
    mod_recv = _exchange("scatter_mod", mod_slots, False)
    mod = mod_recv.reshape(N_DEV, L, B, n_mod).transpose(1, 2, 0, 3).reshape(L, B, N_MOD, 1, D)

    slope_b = jnp.broadcast_to(
        jnp.asarray(np.array([2.0 ** (-8.0 * (h + 1) / N_HEADS) for h in range(N_HEADS)], np.float32))[:, None],
        (N_HEADS, 128))
    w_dw3 = [_conv_weight_chunks(full["w_dw"][j], D) for j in range(full["w_dw"].shape[0])]

    xc = x.reshape(T, D)
    saved = []
    for i in range(L):
        jm = i // 2
        sh1, sc1, g1, sh2, sc2, g2 = (mod[i, :, t] for t in range(N_MOD))
        sv = {"x_in": xc}
        h1 = _normmod_fwd(f"normmod_mix_fwd", xc, W["norm_mix"][i][None], sc1, sh1, S)
        sv["h1"] = h1
        if i % 2 == 0:
            qkv = _mm("qkv_fwd", h1, full["w_qkv"][jm], bias=W["b_qkv"][jm])
            qh = _heads(qkv[:, :q_dim], B, S, N_HEADS)
            kh = _heads(qkv[:, q_dim:q_dim + kv_dim], B, S, N_KV_HEADS)
            vh = _heads(qkv[:, q_dim + kv_dim:], B, S, N_KV_HEADS)
            sink_b = jnp.broadcast_to(W["sinks"][jm][:, None], (N_HEADS, 128))
            oh = _attn_fwd(qh, kh, vh, slope_b, sink_b)
            sv.update(qh=qh, kh=kh, vh=vh, oh=oh, sink_b=sink_b)
            mix_in = _tokens(oh)
            w_out, b_out = full["w_o"][jm], W["b_o"][jm]
        else:
            u = _mm("pw1_fwd", h1, full["w_pw1"][jm], bias=full["b_pw1"][jm], out_dtype=F32)
            cv, mix_in = _conv_fwd(u, w_dw3[jm], full["b_dw"][jm][None], full["conv_ln_g"][jm][None],
                                   full["conv_ln_b"][jm][None], S)
            sv.update(u=u, cv=cv)
            w_out, b_out = full["w_pw2"][jm], full["b_pw2"][jm]
        sv["mix_in"] = mix_in
        x1, y1 = _mm("mix_out_fwd", mix_in, w_out, bias=b_out, epi="resid", x=xc, gate=g1, rows_per_batch=S)
        sv.update(y1=y1, x1=x1)
        h2 = _normmod_fwd("normmod_mlp_fwd", x1, W["norm_mlp"][i][None], sc2, sh2, S)
        up = _mm("mlp_up_fwd", h2, full["w_up"][i])
        x2, y2 = _mm("mlp_down_fwd", up, full["w_down"][i], a_pro="relu2", epi="resid", x=x1, gate=g2,
                     rows_per_batch=S, tm_pref=256)
        sv.update(h2=h2, up=up, y2=y2)
        saved.append(sv)
        xc = x2

    dx, loss_blk, dgfin = _loss_head(xc, loss_target.reshape(T, D), W["final_norm"][None])
    loss = lax.psum(loss_blk[0, 0], ("x", "y", "c"))

    G = {"final_norm": dgfin.reshape(D)}
    dmod_layers = [None] * L
    acc = {n: [None] * W[n].shape[0] for n in ("norm_mix", "norm_mlp", "w_up", "w_down")}
    acc.update({n: [None] * W[n].shape[0] for n in ("b_qkv", "b_o", "sinks", "w_qkv", "w_o")})
    acc.update({n: [None] * W[n].shape[0] for n in ("w_pw1", "b_pw1", "w_dw", "b_dw", "conv_ln_g", "conv_ln_b",
                                                      "w_pw2", "b_pw2")})
    for i in reversed(range(L)):
        jm = i // 2
        sv = saved[i]
        sh1, sc1, g1, sh2, sc2, g2 = (mod[i, :, t] for t in range(N_MOD))
        dy2, dg2, _ = _gate_bwd("gate_mlp_bwd", dx, sv["y2"], g2, S)
        du = _mm("mlp_down_bwd", dy2, full["w_down"][i].T, epi="relu2d", u=sv["up"], tm_pref=256)
        acc["w_down"][i] = _mm_tn("w_down_grad", sv["up"], dy2, a_pro="relu2")
        acc["w_up"][i] = _mm_tn("w_up_grad", sv["h2"], du)
        dh2 = _mm("mlp_up_bwd", du, full["w_up"][i].T, out_dtype=F32, tm_pref=256)
        dx, dsh2, dsc2, dgn = _normmod_bwd("normmod_mlp_bwd", dh2, sv["x1"], dx, W["norm_mlp"][i][None], sc2, S)
        acc["norm_mlp"][i] = dgn.reshape(D)
        dy1, dg1, dy1_sum = _gate_bwd("gate_mix_bwd", dx, sv["y1"], g1, S)
        if i % 2 == 0:
            acc["b_o"][jm] = dy1_sum.reshape(D)
            acc["w_o"][jm] = _mm_tn("w_o_grad", sv["mix_in"], dy1)
            do = _mm("attn_out_bwd", dy1, full["w_o"][jm].T)
            dqh, dkh, dvh, dsk = _attn_bwd(sv["qh"], sv["kh"], sv["vh"], sv["oh"], _heads(do, B, S, N_HEADS),
                                           slope_b, sv["sink_b"])
            acc["sinks"][jm] = jnp.sum(dsk[:, :, :, 0], axis=0).reshape(N_HEADS)
            dqkv = jnp.concatenate([_tokens(dqh), _tokens(dkh), _tokens(dvh)], axis=1)
            acc["b_qkv"][jm] = _colsum("b_qkv_grad", dqkv).reshape(-1)
            acc["w_qkv"][jm] = _mm_tn("w_qkv_grad", sv["h1"], dqkv)
            dh1 = _mm("qkv_bwd", dqkv, full["w_qkv"][jm].T, out_dtype=F32)
        else:
            acc["b_pw2"][jm] = dy1_sum.reshape(D)
            acc["w_pw2"][jm] = _mm_tn("w_pw2_grad", sv["mix_in"], dy1)
            ds = _mm("pw2_bwd", dy1, full["w_pw2"][jm].T, out_dtype=F32)
            dcv, dlg, dlb, dbdw = _lnsilu_bwd(ds, sv["cv"], full["conv_ln_g"][jm][None], full["conv_ln_b"][jm][None])
            acc["conv_ln_g"][jm], acc["conv_ln_b"][jm], acc["b_dw"][jm] = dlg.reshape(-1), dlb.reshape(-1), dbdw.reshape(-1)
            du1, dwdw = _conv_bwd(dcv, sv["u"], w_dw3[jm], S)
            acc["w_dw"][jm] = dwdw.transpose(1, 0, 2).reshape(CONV_HALO, D)[:CONV_WIDTH]
            acc["b_pw1"][jm] = _colsum("b_pw1_grad", du1).reshape(-1)
            acc["w_pw1"][jm] = _mm_tn("w_pw1_grad", sv["h1"], du1)
            dh1 = _mm("pw1_bwd", du1, full["w_pw1"][jm].T, out_dtype=F32)
        dx, dsh1, dsc1, dgn = _normmod_bwd("normmod_mix_bwd", dh1, sv["x_in"], dx, W["norm_mix"][i][None], sc1, S)
        acc["norm_mix"][i] = dgn.reshape(D)
        dmod_layers[i] = jnp.concatenate([dsh1, dsc1, dg1, dsh2, dsc2, dg2], axis=1).reshape(B, N_MOD * D)
    grad_x = dx.reshape(B, S, D)
    for n, parts in acc.items():
        G[n] = jnp.stack(parts)

    dmod = jnp.stack(dmod_layers)
    dmod_slots = dmod.reshape(L, B, N_DEV, n_mod).transpose(2, 0, 1, 3).reshape(N_DEV, L * B, n_mod)
    dmod_recv = _exchange("gather_dmod", dmod_slots, False)
    dmod_all = dmod_recv.reshape(N_DEV, L, B, n_mod).transpose(1, 0, 2, 3).reshape(L, N_DEV * B, n_mod)
    g_w_mod, db_mod_mine = _mod_bwd(c_all, dmod_all)
    G["b_mod"] = lax.dynamic_update_slice_in_dim(jnp.zeros_like(W["b_mod"]), db_mod_mine.reshape(L, n_mod),
                                                 me * n_mod, axis=1)

    big_slots, big_sizes2 = _pack([_to_slots(G[n], d) for n, d in BIG], BF16, 1)
    big_recv = _exchange("reduce_big", big_slots, False)
    small_items = [jnp.broadcast_to(G[n][None], (N_DEV,) + G[n].shape) for n in SMALL_REPL]
    small_items += [_to_slots(G[n], d) for n, d in SMALL_SHARDED]
    small_slots, small_sizes2 = _pack(small_items, F32, 1)
    small_recv = _exchange("reduce_small", small_slots, False)

    out = {}

    def run_adam(name, parts, names):
        shapes = [W[n].shape for n in names]
        wp, sizes = _pack([W[n] for n in names], F32, 0)
        mp, _ = _pack([M[n] for n in names], F32, 0)
        vp, _ = _pack([V[n] for n in names], F32, 0)
        res = _adam(name, parts, wp, mp, vp)
        for kind, buf in zip(("grad", "delta", "new_m", "new_v"), res):
            for n, a in zip(names, _unpack(buf, sizes, shapes, 0)):
                out[kind + "_" + n] = a

    run_adam("adam_big", big_recv, [n for n, _ in BIG])
    run_adam("adam_small", small_recv, list(SMALL_REPL) + [n for n, _ in SMALL_SHARDED])
    run_adam("adam_w_mod", _pack([g_w_mod], F32, 0)[0][None], ["w_mod"])

    res = [loss, grad_x]
    for kind in ("grad", "delta", "new_m", "new_v"):
        res += [out[kind + "_" + n] for n in WEIGHTS]
    return tuple(res)


def kernel(x, c, w_mod, b_mod, norm_mix, norm_mlp, w_qkv, b_qkv, w_o, b_o, sinks, w_pw1, b_pw1, w_dw, b_dw, conv_ln_g, conv_ln_b, w_pw2, b_pw2, w_up, w_down, final_norm, loss_target, m_w_mod, m_b_mod, m_norm_mix, m_norm_mlp, m_w_qkv, m_b_qkv, m_w_o, m_b_o, m_sinks, m_w_pw1, m_b_pw1, m_w_dw, m_b_dw, m_conv_ln_g, m_conv_ln_b, m_w_pw2, m_b_pw2, m_w_up, m_w_down, m_final_norm, v_w_mod, v_b_mod, v_norm_mix, v_norm_mlp, v_w_qkv, v_b_qkv, v_w_o, v_b_o, v_sinks, v_w_pw1, v_b_pw1, v_w_dw, v_b_dw, v_conv_ln_g, v_conv_ln_b, v_w_pw2, v_b_pw2, v_w_up, v_w_down, v_final_norm):
    W = dict(w_mod=w_mod, b_mod=b_mod, norm_mix=norm_mix, norm_mlp=norm_mlp, w_qkv=w_qkv, b_qkv=b_qkv, w_o=w_o,
             b_o=b_o, sinks=sinks, w_pw1=w_pw1, b_pw1=b_pw1, w_dw=w_dw, b_dw=b_dw, conv_ln_g=conv_ln_g,
             conv_ln_b=conv_ln_b, w_pw2=w_pw2, b_pw2=b_pw2, w_up=w_up, w_down=w_down, final_norm=final_norm)
    M = dict(w_mod=m_w_mod, b_mod=m_b_mod, norm_mix=m_norm_mix, norm_mlp=m_norm_mlp, w_qkv=m_w_qkv, b_qkv=m_b_qkv,
             w_o=m_w_o, b_o=m_b_o, sinks=m_sinks, w_pw1=m_w_pw1, b_pw1=m_b_pw1, w_dw=m_w_dw, b_dw=m_b_dw,
             conv_ln_g=m_conv_ln_g, conv_ln_b=m_conv_ln_b, w_pw2=m_w_pw2, b_pw2=m_b_pw2, w_up=m_w_up,
             w_down=m_w_down, final_norm=m_final_norm)
    V = dict(w_mod=v_w_mod, b_mod=v_b_mod, norm_mix=v_norm_mix, norm_mlp=v_norm_mlp, w_qkv=v_w_qkv, b_qkv=v_b_qkv,
             w_o=v_w_o, b_o=v_b_o, sinks=v_sinks, w_pw1=v_w_pw1, b_pw1=v_b_pw1, w_dw=v_w_dw, b_dw=v_b_dw,
             conv_ln_g=v_conv_ln_g, conv_ln_b=v_conv_ln_b, w_pw2=v_w_pw2, b_pw2=v_b_pw2, w_up=v_w_up,
             w_down=v_w_down, final_norm=v_final_norm)
    return _step(x, c, loss_target, W, M, V)
```

```python
import functools

import numpy as np
import jax
import jax.numpy as jnp
from jax import lax
from jax.experimental import pallas as pl
from jax.experimental.pallas import tpu as pltpu

F32 = jnp.float32
BF16 = jnp.bfloat16

N_DEV = 8
N_HEADS = 16
N_KV_HEADS = 2
HEAD_DIM = 64
GROUP = N_HEADS // N_KV_HEADS
ATT_BLOCK = 128
CONV_WIDTH = 31
CONV_HALO = 32
N_MOD = 6
EPS = 1e-6
ADAM_LR = 0.001
ADAM_B1 = 0.9
ADAM_B2 = 0.999
ADAM_EPS = 1e-08
ADAM_WD = 0.01
ADAM_STEP = 10
NEG_BIG = -1e30
PACK_COLS = 1024
VMEM_LIMIT_BYTES = 56 * 1024 * 1024
MESH_ID = pl.DeviceIdType.MESH


def _params(*sem):
    return pltpu.CompilerParams(dimension_semantics=sem, vmem_limit_bytes=VMEM_LIMIT_BYTES)


def _pick(n, pref, mult=8):
    if n <= pref:
        return n
    for t in range(pref, 0, -1):
        if n % t == 0 and t % mult == 0:
            return t
    return n


def _sigmoid(z):
    return 1.0 / (1.0 + jnp.exp(-z))


def _payload_layout(payload):
    n = len(payload)
    out_shapes = [jax.ShapeDtypeStruct((N_DEV,) + tuple(a.shape if bc else a.shape[1:]), a.dtype) for a, bc in payload]
    hbm = pl.BlockSpec(memory_space=pl.ANY)
    scratch = [pltpu.SemaphoreType.DMA((n * (N_DEV - 1),)), pltpu.SemaphoreType.DMA((n * (N_DEV - 1),)),
               pltpu.SemaphoreType.DMA((n,))]
    return [a for a, _ in payload], [hbm] * n, out_shapes, [hbm] * n, scratch


def _payload_copies(bcasts, src_refs, dst_refs, send_sems, recv_sems, local_sems):
    x, y, c = lax.axis_index("x"), lax.axis_index("y"), lax.axis_index("c")
    me = 4 * x + 2 * y + c
    copies = []
    for t, (bc, s_ref, d_ref) in enumerate(zip(bcasts, src_refs, dst_refs)):
        for k in range(1, N_DEV):
            px = 1 - x if (k >> 2) & 1 else x
            py = 1 - y if (k >> 1) & 1 else y
            pc = 1 - c if k & 1 else c
            sem = t * (N_DEV - 1) + k - 1
            copies.append(pltpu.make_async_remote_copy(
                src_ref=s_ref if bc else s_ref.at[4 * px + 2 * py + pc],
                dst_ref=d_ref.at[me],
                send_sem=send_sems.at[sem],
                recv_sem=recv_sems.at[sem],
                device_id=(px, py, pc),
                device_id_type=MESH_ID,
            ))
        copies.append(pltpu.make_async_copy(s_ref if bc else s_ref.at[me], d_ref.at[me], local_sems.at[t]))
    return copies


def _exchange(name, payload):
    n = len(payload)
    bcasts = [bc for _, bc in payload]
    arrays, in_specs, out_shapes, out_specs, scratch = _payload_layout(payload)

    def body(*refs):
        copies = _payload_copies(bcasts, refs[:n], refs[n:2 * n], *refs[2 * n:])
        for cp in copies:
            cp.start()
        for cp in copies:
            cp.wait()

    return pl.pallas_call(
        body,
        name=name,
        out_shape=tuple(out_shapes),
        in_specs=in_specs,
        out_specs=tuple(out_specs),
        scratch_shapes=scratch,
    )(*arrays)


def _pack(arrays, dtype, lead):
    lead_shape = arrays[0].shape[:lead]
    flat = [a.astype(dtype).reshape(lead_shape + (-1,)) for a in arrays]
    sizes = [f.shape[-1] for f in flat]
    total = sum(sizes)
    chunk = 16 * PACK_COLS
    padded = -(-total // chunk) * chunk
    if padded > total:
        flat.append(jnp.zeros(lead_shape + (padded - total,), dtype))
    buf = jnp.concatenate(flat, axis=-1)
    return buf.reshape(lead_shape + (padded // PACK_COLS, PACK_COLS)), sizes


def _unpack(buf, sizes, shapes, lead):
    lead_shape = buf.shape[:lead]
    flat = buf.reshape(lead_shape + (-1,))
    out, off = [], 0
    for n, shp in zip(sizes, shapes):
        out.append(lax.slice_in_dim(flat, off, off + n, axis=lead).reshape(lead_shape + tuple(shp)))
        off += n
    return out


def _to_slots(a, dim):
    shp = a.shape
    a = a.reshape(shp[:dim] + (N_DEV, shp[dim] // N_DEV) + shp[dim + 1:])
    return jnp.moveaxis(a, dim, 0)


def _from_slots(a, dim):
    a = jnp.moveaxis(a, 0, dim)
    shp = a.shape
    return a.reshape(shp[:dim] + (shp[dim] * shp[dim + 1],) + shp[dim + 2:])


def _mod_fwd(c_all, w, b):
    L, D, n = w.shape
    M = c_all.shape[0]

    def body(c_ref, w_ref, b_ref, o_ref):
        cv = c_ref[...]
        cs = (cv * _sigmoid(cv)).astype(BF16)
        o_ref[...] = jnp.dot(cs, w_ref[...], preferred_element_type=F32) + b_ref[...]

    return pl.pallas_call(
        body,
        name="mod_fwd",
        grid=(L,),
        out_shape=jax.ShapeDtypeStruct((L, M, n), F32),
        in_specs=[
            pl.BlockSpec((M, D), lambda l: (0, 0)),
            pl.BlockSpec((None, D, n), lambda l: (l, 0, 0)),
            pl.BlockSpec((None, 1, n), lambda l: (l, 0, 0)),
        ],
        out_specs=pl.BlockSpec((None, M, n), lambda l: (l, 0, 0)),
        compiler_params=_params("arbitrary"),
    )(c_all, w, b)


def _mod_bwd(c_all, dmod_all):
    L, M, n = dmod_all.shape
    D = c_all.shape[1]

    def body(c_ref, d_ref, dw_ref, db_ref):
        cv = c_ref[...]
        cs = (cv * _sigmoid(cv)).astype(BF16)
        d = d_ref[...]
        dw_ref[...] = lax.dot_general(cs, d.astype(BF16), (((0,), (0,)), ((), ())), preferred_element_type=F32)
        db_ref[...] = jnp.sum(d, axis=0, keepdims=True)

    return pl.pallas_call(
        body,
        name="mod_bwd",
        grid=(L,),
        out_shape=(jax.ShapeDtypeStruct((L, D, n), F32), jax.ShapeDtypeStruct((L, 1, n), F32)),
        in_specs=[
            pl.BlockSpec((M, D), lambda l: (0, 0)),
            pl.BlockSpec((None, M, n), lambda l: (l, 0, 0)),
        ],
        out_specs=(
            pl.BlockSpec((None, D, n), lambda l: (l, 0, 0)),
            pl.BlockSpec((None, 1, n), lambda l: (l, 0, 0)),
        ),
        compiler_params=_params("arbitrary"),
    )(c_all, dmod_all)


def _mm(name, a, w, *, w_form="full", out_dtype=BF16, bias=None, a_pro=None, epi=None, x=None, gate=None, u=None,
        rows_per_batch=None, tm_pref=512, payload=()):
    M, K = a.shape
    if w_form == "full":
        N = w.shape[1]
        nc = _pick(N, 1024, 128)
    elif w_form == "full_t":
        N = w.shape[0]
        nc = _pick(N, 1024, 128)
    elif w_form == "nslots":
        nc = w.shape[2]
        N = N_DEV * nc
    else:
        N = w.shape[1]
        nc = N
    ks = K // N_DEV
    n_chunks = N // nc
    tm = _pick(M if rows_per_batch is None else rows_per_batch, tm_pref, 16)
    steps = M // tm
    has_bias = bias is not None
    n_pay = len(payload)
    bcasts = [bc for _, bc in payload]
    nt = (((1,), (1,)), ((), ()))

    def prologue(v):
        if a_pro == "relu2":
            r = jnp.maximum(v.astype(F32), 0.0)
            return (r * r).astype(BF16)
        return v

    def body(*refs):
        it = iter(refs)
        a_ref = next(it)
        w_ref = next(it)
        b_ref = next(it) if has_bias else None
        x_ref = next(it) if epi == "resid" else None
        g_ref = next(it) if epi == "resid" else None
        u_ref = next(it) if epi == "relu2d" else None
        pay_src = [next(it) for _ in range(n_pay)]
        o_ref = next(it)
        y_ref = next(it) if epi == "resid" else None
        pay_dst = [next(it) for _ in range(n_pay)]
        sems = list(it)

        if n_pay:
            @pl.when(pl.program_id(0) == 0)
            def _():
                for cp in _payload_copies(bcasts, pay_src, pay_dst, *sems):
                    cp.start()

        av = None if w_form == "kslots_t" else prologue(a_ref[...])
        for ci in range(n_chunks):
            cols = slice(ci * nc, (ci + 1) * nc)
            if w_form == "full":
                acc = jnp.dot(av, w_ref[:, cols], preferred_element_type=F32)
            elif w_form == "full_t":
                acc = lax.dot_general(av, w_ref[cols, :], nt, preferred_element_type=F32)
            elif w_form == "nslots":
                acc = jnp.dot(av, w_ref[ci], preferred_element_type=F32)
            else:
                acc = lax.dot_general(prologue(a_ref[:, 0:ks]), w_ref[0], nt, preferred_element_type=F32)
                for j in range(1, N_DEV):
                    acc = acc + lax.dot_general(prologue(a_ref[:, j * ks:(j + 1) * ks]), w_ref[j], nt,
                                                preferred_element_type=F32)
            if has_bias:
                acc = acc + b_ref[:, cols]
            if epi == "resid":
                o_ref[:, cols] = x_ref[:, cols] + g_ref[:, cols] * acc
                y_ref[:, cols] = acc.astype(BF16)
            elif epi == "relu2d":
                o_ref[:, cols] = (acc * (2.0 * jnp.maximum(u_ref[:, cols].astype(F32), 0.0))).astype(out_dtype)
            else:
                o_ref[:, cols] = acc.astype(out_dtype)

        if n_pay:
            @pl.when(pl.program_id(0) == steps - 1)
            def _():
                for cp in _payload_copies(bcasts, pay_src, pay_dst, *sems):
                    cp.wait()

    args = [a, w]
    w_block = w.shape
    specs = [pl.BlockSpec((tm, K), lambda i: (i, 0)), pl.BlockSpec(w_block, lambda i: (0,) * len(w_block))]
    if has_bias:
        args.append(bias.reshape(1, N).astype(F32))
        specs.append(pl.BlockSpec((1, N), lambda i: (0, 0)))
    row_spec = pl.BlockSpec((tm, N), lambda i: (i, 0))
    if epi == "resid":
        tpb = rows_per_batch // tm
        args += [x, gate]
        specs += [row_spec, pl.BlockSpec((None, 1, N), lambda i: (i // tpb, 0, 0))]
        out_shape = [jax.ShapeDtypeStruct((M, N), F32), jax.ShapeDtypeStruct((M, N), BF16)]
        out_specs = [row_spec, row_spec]
    else:
        if epi == "relu2d":
            args.append(u)
            specs.append(row_spec)
        out_shape = [jax.ShapeDtypeStruct((M, N), out_dtype)]
        out_specs = [row_spec]
    scratch = []
    if n_pay:
        p_arrays, p_in, p_shapes, p_out, scratch = _payload_layout(payload)
        args += p_arrays
        specs += p_in
        out_shape += p_shapes
        out_specs += p_out
    res = pl.pallas_call(
        body,
        name=name,
        grid=(steps,),
        out_shape=tuple(out_shape),
        in_specs=specs,
        out_specs=tuple(out_specs),
        scratch_shapes=scratch,
        compiler_params=_params("arbitrary" if n_pay else "parallel"),
    )(*args)
    n_own = 2 if epi == "resid" else 1
    own = res[0] if n_own == 1 else tuple(res[:n_own])
    return (own, list(res[n_own:])) if n_pay else own


def _mm_tn(name, a, b, *, a_pro=None, out_form="full", tt_pref=1024, tk_pref=1024, tn_pref=1024):
    T, K = a.shape
    N = b.shape[1]
    tt = _pick(T, tt_pref, 16)
    tk = K // N_DEV if out_form == "kslots" else _pick(K, tk_pref, 128)
    tn = N // N_DEV if out_form == "nslots" else _pick(N, tn_pref, 128)
    nt_steps = T // tt

    def body(a_ref, b_ref, o_ref, *acc_ref):
        acc = acc_ref[0] if acc_ref else o_ref

        @pl.when(pl.program_id(2) == 0)
        def _():
            acc[...] = jnp.zeros_like(acc)

        av = a_ref[...]
        if a_pro == "relu2":
            r = jnp.maximum(av.astype(F32), 0.0)
            av = (r * r).astype(BF16)
        acc[...] += lax.dot_general(av, b_ref[...], (((0,), (0,)), ((), ())), preferred_element_type=F32)

        if acc_ref:
            @pl.when(pl.program_id(2) == nt_steps - 1)
            def _():
                o_ref[...] = acc[...].astype(o_ref.dtype)

    if out_form == "full":
        out_shape = jax.ShapeDtypeStruct((K, N), F32)
        out_spec = pl.BlockSpec((tk, tn), lambda k, n, t: (k, n))
        scratch = []
    elif out_form == "nslots":
        out_shape = jax.ShapeDtypeStruct((N_DEV, K, tn), BF16)
        out_spec = pl.BlockSpec((None, tk, tn), lambda k, n, t: (n, k, 0))
        scratch = [pltpu.VMEM((tk, tn), F32)]
    else:
        out_shape = jax.ShapeDtypeStruct((N_DEV, tk, N), BF16)
        out_spec = pl.BlockSpec((None, tk, tn), lambda k, n, t: (k, 0, n))
        scratch = [pltpu.VMEM((tk, tn), F32)]
    return pl.pallas_call(
        body,
        name=name,
        grid=(K // tk, N // tn, nt_steps),
        out_shape=out_shape,
        in_specs=[
            pl.BlockSpec((tt, tk), lambda k, n, t: (t, k)),
            pl.BlockSpec((tt, tn), lambda k, n, t: (t, n)),
        ],
        out_specs=out_spec,
        scratch_shapes=scratch,
        compiler_params=_params("parallel", "parallel", "arbitrary"),
    )(a, b)


def _colsum(name, a, tm_pref=1024):
    T, N = a.shape
    tm = _pick(T, tm_pref, 16)

    def body(a_ref, o_ref):
        @pl.when(pl.program_id(0) == 0)
        def _():
            o_ref[...] = jnp.zeros_like(o_ref)

        o_ref[...] += jnp.sum(a_ref[...].astype(F32), axis=0, keepdims=True)

    return pl.pallas_call(
        body,
        name=name,
        grid=(T // tm,),
        out_shape=jax.ShapeDtypeStruct((1, N), F32),
        in_specs=[pl.BlockSpec((tm, N), lambda i: (i, 0))],
        out_specs=pl.BlockSpec((1, N), lambda i: (0, 0)),
        compiler_params=_params("arbitrary"),
    )(a)


def _normmod_fwd(name, x, gnorm, sc, sh, S, tm_pref=512):
    T, D = x.shape
    tm = _pick(S, tm_pref, 16)
    tpb = S // tm

    def body(x_ref, g_ref, sc_ref, sh_ref, o_ref):
        xv = x_ref[...]
        r = lax.rsqrt(jnp.mean(xv * xv, axis=-1, keepdims=True) + EPS)
        n = xv * r * g_ref[...]
        o_ref[...] = (n * (1.0 + sc_ref[...]) + sh_ref[...]).astype(BF16)

    row = pl.BlockSpec((tm, D), lambda i: (i, 0))
    vec = pl.BlockSpec((None, 1, D), lambda i: (i // tpb, 0, 0))
    return pl.pallas_call(
        body,
        name=name,
        grid=(T // tm,),
        out_shape=jax.ShapeDtypeStruct((T, D), BF16),
        in_specs=[row, pl.BlockSpec((1, D), lambda i: (0, 0)), vec, vec],
        out_specs=row,
        compiler_params=_params("parallel"),
    )(x, gnorm, sc, sh)


def _normmod_bwd(name, dh, x, dx_in, gnorm, sc, S, tm_pref=256):
    T, D = x.shape
    B = T // S
    tm = _pick(S, tm_pref, 16)
    tpb = S // tm

    def body(dh_ref, x_ref, dxin_ref, g_ref, sc_ref, dx_ref, dsh_ref, dsc_ref, dg_ref):
        b, j = pl.program_id(0), pl.program_id(1)

        @pl.when(j == 0)
        def _():
            dsh_ref[...] = jnp.zeros_like(dsh_ref)
            dsc_ref[...] = jnp.zeros_like(dsc_ref)

        @pl.when((b == 0) & (j == 0))
        def _():
            dg_ref[...] = jnp.zeros_like(dg_ref)

        xv = x_ref[...]
        g = g_ref[...]
        r = lax.rsqrt(jnp.mean(xv * xv, axis=-1, keepdims=True) + EPS)
        xhat = xv * r
        n = xhat * g
        dh_v = dh_ref[...]
        dsh_ref[...] += jnp.sum(dh_v, axis=0, keepdims=True)
        dsc_ref[...] += jnp.sum(dh_v * n, axis=0, keepdims=True)
        dn = dh_v * (1.0 + sc_ref[...])
        dg_ref[...] += jnp.sum(dn * xhat, axis=0, keepdims=True)
        dxhat = dn * g
        dx_ref[...] = dxin_ref[...] + r * (dxhat - xhat * jnp.mean(dxhat * xhat, axis=-1, keepdims=True))

    row = pl.BlockSpec((tm, D), lambda b, j: (b * tpb + j, 0))
    vec = pl.BlockSpec((None, 1, D), lambda b, j: (b, 0, 0))
    one = pl.BlockSpec((1, D), lambda b, j: (0, 0))
    return pl.pallas_call(
        body,
        name=name,
        grid=(B, tpb),
        out_shape=(
            jax.ShapeDtypeStruct((T, D), F32),
            jax.ShapeDtypeStruct((B, 1, D), F32),
            jax.ShapeDtypeStruct((B, 1, D), F32),
            jax.ShapeDtypeStruct((1, D), F32),
        ),
        in_specs=[row, row, row, one, vec],
        out_specs=(row, vec, vec, one),
        compiler_params=_params("arbitrary", "arbitrary"),
    )(dh, x, dx_in, gnorm, sc)


def _gate_bwd(name, dx, y, gate, S, tm_pref=512):
    T, D = dx.shape
    B = T // S
    tm = _pick(S, tm_pref, 16)
    tpb = S // tm

    def body(dx_ref, y_ref, g_ref, dy_ref, dgate_ref, cs_ref):
        b, j = pl.program_id(0), pl.program_id(1)

        @pl.when(j == 0)
        def _():
            dgate_ref[...] = jnp.zeros_like(dgate_ref)

        @pl.when((b == 0) & (j == 0))
        def _():
            cs_ref[...] = jnp.zeros_like(cs_ref)

        dxv = dx_ref[...]
        dy = dxv * g_ref[...]
        dy_ref[...] = dy.astype(BF16)
        dgate_ref[...] += jnp.sum(dxv * y_ref[...].astype(F32), axis=0, keepdims=True)
        cs_ref[...] += jnp.sum(dy, axis=0, keepdims=True)

    row = pl.BlockSpec((tm, D), lambda b, j: (b * tpb + j, 0))
    vec = pl.BlockSpec((None, 1, D), lambda b, j: (b, 0, 0))
    one = pl.BlockSpec((1, D), lambda b, j: (0, 0))
    return pl.pallas_call(
        body,
        name=name,
        grid=(B, tpb),
        out_shape=(
            jax.ShapeDtypeStruct((T, D), BF16),
            jax.ShapeDtypeStruct((B, 1, D), F32),
            jax.ShapeDtypeStruct((1, D), F32),
        ),
        in_specs=[row, row, vec],
        out_specs=(row, vec, one),
        compiler_params=_params("arbitrary", "arbitrary"),
    )(dx, y, gate)


ATT_ROWS = GROUP * ATT_BLOCK
ATT_SPAN = 2 * ATT_BLOCK
ATT_SCALE = HEAD_DIM ** -0.5


def _attn_table(sinks):
    slopes = jnp.asarray(np.array([2.0 ** (-8.0 * (h + 1) / N_HEADS) for h in range(N_HEADS)], np.float32))
    r = jnp.arange(ATT_BLOCK)[:, None]
    cc = jnp.arange(ATT_SPAN)[None, :]
    dist = r + ATT_BLOCK - cc
    ok = (dist >= 0) & (dist < ATT_BLOCK)
    tab = jnp.where(ok[None], -slopes[:, None, None] * dist.astype(F32)[None], NEG_BIG)
    tab = jnp.where((cc == 0)[None], sinks.astype(F32)[:, None, None], tab)
    return tab.reshape(N_KV_HEADS, ATT_ROWS, ATT_SPAN)


def _load_span(prev_ref, cur_ref, buf, mult):
    row = lax.broadcasted_iota(jnp.int32, prev_ref.shape, 0)
    pv = prev_ref[...]
    cv = cur_ref[...]
    if mult != 1.0:
        pv = pv * mult
        cv = cv * mult
    buf[0:ATT_BLOCK, :] = jnp.where(row > 0, pv, jnp.zeros_like(pv))
    buf[ATT_BLOCK:, :] = cv


def _first_block_penalty(i):
    col = lax.broadcasted_iota(jnp.int32, (1, ATT_SPAN), 1)
    return jnp.where((col < ATT_BLOCK) & (col > 0), jnp.where(i > 0, 0.0, NEG_BIG), 0.0).astype(F32)


def _attn_probs(qs, kbuf, bias, first_pen):
    nt = (((1,), (1,)), ((), ()))
    s = lax.dot_general(qs, kbuf, nt, preferred_element_type=F32) + bias + first_pen
    m = jnp.max(s, axis=-1, keepdims=True)
    e = jnp.exp(s - m)
    return e * (1.0 / jnp.sum(e, axis=-1, keepdims=True))


def _attn_fwd(q, k, v, tab):
    B, H, S, hd = q.shape
    nb = S // ATT_BLOCK

    def body(q_ref, kc_ref, kp_ref, vc_ref, vp_ref, tab_ref, o_ref, kbuf, vbuf):
        i = pl.program_id(2)
        _load_span(kp_ref, kc_ref, kbuf, ATT_SCALE)
        _load_span(vp_ref, vc_ref, vbuf, 1.0)
        p = _attn_probs(q_ref[...].reshape(ATT_ROWS, hd), kbuf[...], tab_ref[...], _first_block_penalty(i))
        o = jnp.dot(p.astype(BF16), vbuf[...], preferred_element_type=F32)
        o_ref[...] = o.astype(BF16).reshape(GROUP, ATT_BLOCK, hd)

    qspec = pl.BlockSpec((None, GROUP, ATT_BLOCK, hd), lambda b, h, i: (b, h, i, 0))
    cur = pl.BlockSpec((None, None, ATT_BLOCK, hd), lambda b, h, i: (b, h, i, 0))
    prev = pl.BlockSpec((None, None, ATT_BLOCK, hd), lambda b, h, i: (b, h, jnp.maximum(i - 1, 0), 0))
    tspec = pl.BlockSpec((None, ATT_ROWS, ATT_SPAN), lambda b, h, i: (h, 0, 0))
    return pl.pallas_call(
        body,
        name="attn_fwd",
        grid=(B, N_KV_HEADS, nb),
        out_shape=jax.ShapeDtypeStruct((B, H, S, hd), BF16),
        in_specs=[qspec, cur, prev, cur, prev, tspec],
        out_specs=qspec,
        scratch_shapes=[pltpu.VMEM((ATT_SPAN, hd), BF16), pltpu.VMEM((ATT_SPAN, hd), BF16)],
        compiler_params=_params("parallel", "parallel", "arbitrary"),
    )(q, k, k, v, v, tab)


def _attn_bwd(q, k, v, o, do, tab):
    B, H, S, hd = q.shape
    nb = S // ATT_BLOCK
    tn = (((0,), (0,)), ((), ()))
    nt = (((1,), (1,)), ((), ()))

    def body(q_ref, o_ref, do_ref, kc_ref, kp_ref, vc_ref, vp_ref, tab_ref,
             dq_ref, dk_ref, dv_ref, dsink_ref, carry_k, carry_v, kbuf, vbuf):
        i = pl.program_id(2)

        @pl.when(i == 0)
        def _():
            dsink_ref[...] = jnp.zeros_like(dsink_ref)

        @pl.when(i < nb)
        def _():
            _load_span(kp_ref, kc_ref, kbuf, ATT_SCALE)
            _load_span(vp_ref, vc_ref, vbuf, 1.0)
            qs = q_ref[...].reshape(ATT_ROWS, hd)
            dos = do_ref[...].reshape(ATT_ROWS, hd)
            os_ = o_ref[...].reshape(ATT_ROWS, hd)
            p = _attn_probs(qs, kbuf[...], tab_ref[...], _first_block_penalty(i))
            delta = jnp.sum(dos.astype(F32) * os_.astype(F32), axis=-1, keepdims=True)
            dp = lax.dot_general(dos, vbuf[...], nt, preferred_element_type=F32)
            ds = (p * (dp - delta)).astype(BF16)
            dq = jnp.dot(ds, kbuf[...], preferred_element_type=F32)
            dq_ref[...] = dq.astype(BF16).reshape(GROUP, ATT_BLOCK, hd)
            hg = lax.broadcasted_iota(jnp.int32, (GROUP, ATT_ROWS), 0)
            hr = lax.broadcasted_iota(jnp.int32, (GROUP, ATT_ROWS), 1)
            head_of = jnp.where(hr // ATT_BLOCK == hg, 1.0, 0.0).astype(BF16)
            dsink_ref[...] += jnp.dot(head_of, ds, preferred_element_type=F32)[:, 0:128]
            keep = lax.broadcasted_iota(jnp.int32, (ATT_SPAN, hd), 0) > 0
            dk = jnp.where(keep, lax.dot_general(ds, qs, tn, preferred_element_type=F32) * ATT_SCALE, 0.0)
            dv = jnp.where(keep, lax.dot_general(p.astype(BF16), dos, tn, preferred_element_type=F32), 0.0)

            @pl.when(i > 0)
            def _():
                dk_ref[...] = (carry_k[...] + dk[0:ATT_BLOCK]).astype(BF16)
                dv_ref[...] = (carry_v[...] + dv[0:ATT_BLOCK]).astype(BF16)

            carry_k[...] = dk[ATT_BLOCK:]
            carry_v[...] = dv[ATT_BLOCK:]

        @pl.when(i == nb)
        def _():
            dk_ref[...] = carry_k[...].astype(BF16)
            dv_ref[...] = carry_v[...].astype(BF16)

    last = nb - 1
    qspec = pl.BlockSpec((None, GROUP, ATT_BLOCK, hd), lambda b, h, i: (b, h, jnp.minimum(i, last), 0))
    cur = pl.BlockSpec((None, None, ATT_BLOCK, hd), lambda b, h, i: (b, h, jnp.minimum(i, last), 0))
    prev = pl.BlockSpec((None, None, ATT_BLOCK, hd), lambda b, h, i: (b, h, jnp.clip(i - 1, 0, last), 0))
    hvec = pl.BlockSpec((None, ATT_ROWS, ATT_SPAN), lambda b, h, i: (h, 0, 0))
    dkv = pl.BlockSpec((None, None, ATT_BLOCK, hd), lambda b, h, i: (b, h, jnp.maximum(i - 1, 0), 0))
    dsk = pl.BlockSpec((None, None, GROUP, 128), lambda b, h, i: (b, h, 0, 0))
    return pl.pallas_call(
        body,
        name="attn_bwd",
        grid=(B, N_KV_HEADS, nb + 1),
        out_shape=(
            jax.ShapeDtypeStruct((B, H, S, hd), BF16),
            jax.ShapeDtypeStruct((B, N_KV_HEADS, S, hd), BF16),
            jax.ShapeDtypeStruct((B, N_KV_HEADS, S, hd), BF16),
            jax.ShapeDtypeStruct((B, N_KV_HEADS, GROUP, 128), F32),
        ),
        in_specs=[qspec, qspec, qspec, cur, prev, cur, prev, hvec],
        out_specs=(qspec, dkv, dkv, dsk),
        scratch_shapes=[pltpu.VMEM((ATT_BLOCK, hd), F32), pltpu.VMEM((ATT_BLOCK, hd), F32),
                        pltpu.VMEM((ATT_SPAN, hd), BF16), pltpu.VMEM((ATT_SPAN, hd), BF16)],
        compiler_params=_params("parallel", "parallel", "arbitrary"),
    )(q, o, do, k, k, v, v, tab)


def _conv_tiles(S):
    ts = _pick(S, 256, CONV_HALO)
    return ts, S // ts


def _conv_chunks(C, ts):
    lane = _pick(C, 128, 128)
    return lane, C // lane, _pick(ts, 64, 8)


def _conv_weight_chunks(w_dw, C):
    lane = _pick(C, 128, 128)
    w = jnp.pad(w_dw, ((0, CONV_HALO - CONV_WIDTH), (0, 0)))
    return w.reshape(CONV_HALO, C // lane, lane).transpose(1, 0, 2)


def _conv_fwd(u, w3, b_dw, ln_g, ln_b, S):
    T, C2 = u.shape
    C = C2 // 2
    B = T // S
    ts, nj = _conv_tiles(S)
    hb = ts // CONV_HALO
    lane, nc, rc = _conv_chunks(C, ts)

    def body(a_ref, g_ref, ap_ref, gp_ref, w_ref, bdw_ref, lg_ref, lb_ref, cv_ref, s_ref, buf, cvb):
        j = pl.program_id(1)
        glu_prev = ap_ref[...] * _sigmoid(gp_ref[...]) * (j > 0).astype(F32)
        glu = a_ref[...] * _sigmoid(g_ref[...])
        for cc in range(nc):
            buf[cc, 0:CONV_HALO, :] = glu_prev[:, cc * lane:(cc + 1) * lane]
            buf[cc, CONV_HALO:, :] = glu[:, cc * lane:(cc + 1) * lane]

        def chunk(cc, carry):
            for r0 in range(0, ts, rc):
                acc = jnp.zeros((rc, lane), F32)
                for kk in range(CONV_WIDTH):
                    lo = CONV_HALO - (CONV_WIDTH - 1 - kk) + r0
                    acc = acc + w_ref[cc, kk:kk + 1, :] * buf[cc, lo:lo + rc, :]
                cvb[cc, r0:r0 + rc, :] = acc
            return carry

        lax.fori_loop(0, nc, chunk, 0)
        for cc in range(nc):
            cv_ref[:, cc * lane:(cc + 1) * lane] = cvb[cc] + bdw_ref[:, cc * lane:(cc + 1) * lane]
        cv = cv_ref[...]
        mu = jnp.mean(cv, axis=-1, keepdims=True)
        xc = cv - mu
        rstd = lax.rsqrt(jnp.mean(xc * xc, axis=-1, keepdims=True) + EPS)
        ln = xc * rstd * lg_ref[...] + lb_ref[...]
        s_ref[...] = (ln * _sigmoid(ln)).astype(BF16)

    a_cur = pl.BlockSpec((ts, C), lambda b, j: (b * nj + j, 0))
    g_cur = pl.BlockSpec((ts, C), lambda b, j: (b * nj + j, 1))
    a_prev = pl.BlockSpec((CONV_HALO, C), lambda b, j: (jnp.maximum((b * nj + j) * hb - 1, 0), 0))
    g_prev = pl.BlockSpec((CONV_HALO, C), lambda b, j: (jnp.maximum((b * nj + j) * hb - 1, 0), 1))
    wspec = pl.BlockSpec((nc, CONV_HALO, lane), lambda b, j: (0, 0, 0))
    one = pl.BlockSpec((1, C), lambda b, j: (0, 0))
    return pl.pallas_call(
        body,
        name="conv_fwd",
        grid=(B, nj),
        out_shape=(jax.ShapeDtypeStruct((T, C), F32), jax.ShapeDtypeStruct((T, C), BF16)),
        in_specs=[a_cur, g_cur, a_prev, g_prev, wspec, one, one, one],
        out_specs=(a_cur, a_cur),
        scratch_shapes=[pltpu.VMEM((nc, CONV_HALO + ts, lane), F32), pltpu.VMEM((nc, ts, lane), F32)],
        compiler_params=_params("parallel", "arbitrary"),
    )(u, u, u, u, w3, b_dw, ln_g, ln_b)


def _lnsilu_bwd(ds, cv, ln_g, ln_b, tm_pref=256):
    T, C = cv.shape
    tm = _pick(T, tm_pref, 16)

    def body(ds_ref, cv_ref, lg_ref, lb_ref, dcv_ref, dlg_ref, dlb_ref, dbdw_ref):
        @pl.when(pl.program_id(0) == 0)
        def _():
            dlg_ref[...] = jnp.zeros_like(dlg_ref)
            dlb_ref[...] = jnp.zeros_like(dlb_ref)
            dbdw_ref[...] = jnp.zeros_like(dbdw_ref)

        cv_v = cv_ref[...]
        g = lg_ref[...]
        mu = jnp.mean(cv_v, axis=-1, keepdims=True)
        xc = cv_v - mu
        rstd = lax.rsqrt(jnp.mean(xc * xc, axis=-1, keepdims=True) + EPS)
        xhat = xc * rstd
        ln = xhat * g + lb_ref[...]
        sg = _sigmoid(ln)
        dln = ds_ref[...] * (sg * (1.0 + ln * (1.0 - sg)))
        dlg_ref[...] += jnp.sum(dln * xhat, axis=0, keepdims=True)
        dlb_ref[...] += jnp.sum(dln, axis=0, keepdims=True)
        dxhat = dln * g
        dcv = rstd * (dxhat - jnp.mean(dxhat, axis=-1, keepdims=True)
                      - xhat * jnp.mean(dxhat * xhat, axis=-1, keepdims=True))
        dcv_ref[...] = dcv
        dbdw_ref[...] += jnp.sum(dcv, axis=0, keepdims=True)

    row = pl.BlockSpec((tm, C), lambda i: (i, 0))
    one = pl.BlockSpec((1, C), lambda i: (0, 0))
    return pl.pallas_call(
        body,
        name="lnsilu_bwd",
        grid=(T // tm,),
        out_shape=(jax.ShapeDtypeStruct((T, C), F32),) + (jax.ShapeDtypeStruct((1, C), F32),) * 3,
        in_specs=[row, row, one, one],
        out_specs=(row, one, one, one),
        compiler_params=_params("arbitrary"),
    )(ds, cv, ln_g, ln_b)


def _conv_bwd(dcv, u, w3, S):
    T, C2 = u.shape
    C = C2 // 2
    B = T // S
    ts, nj = _conv_tiles(S)
    hb = ts // CONV_HALO
    n_halo_blocks = T // CONV_HALO
    lane, nc, rc = _conv_chunks(C, ts)

    def body(dcv_ref, dnx_ref, a_ref, g_ref, ap_ref, gp_ref, w_ref, du_ref, dw_ref, gbuf, dbuf, dglu, dw8):
        b, j = pl.program_id(0), pl.program_id(1)

        @pl.when((b == 0) & (j == 0))
        def _():
            dw8[...] = jnp.zeros_like(dw8)

        a = a_ref[...]
        sg = _sigmoid(g_ref[...])
        glu_prev = ap_ref[...] * _sigmoid(gp_ref[...]) * (j > 0).astype(F32)
        glu = a * sg
        dcur = dcv_ref[...]
        dnext = dnx_ref[...] * (j < nj - 1).astype(F32)
        for cc in range(nc):
            cols = slice(cc * lane, (cc + 1) * lane)
            gbuf[cc, 0:CONV_HALO, :] = glu_prev[:, cols]
            gbuf[cc, CONV_HALO:, :] = glu[:, cols]
            dbuf[cc, 0:ts, :] = dcur[:, cols]
            dbuf[cc, ts:, :] = dnext[:, cols]

        def chunk(cc, carry):
            for r0 in range(0, ts, rc):
                acc = jnp.zeros((rc, lane), F32)
                for kk in range(CONV_WIDTH):
                    d = CONV_WIDTH - 1 - kk
                    acc = acc + w_ref[cc, kk:kk + 1, :] * dbuf[cc, r0 + d:r0 + d + rc, :]
                dglu[cc, r0:r0 + rc, :] = acc
            for kk in range(CONV_WIDTH):
                d = CONV_WIDTH - 1 - kk
                p = jnp.zeros((rc, lane), F32)
                for r0 in range(0, ts, rc):
                    lo = CONV_HALO - d + r0
                    p = p + dbuf[cc, r0:r0 + rc, :] * gbuf[cc, lo:lo + rc, :]
                dw8[cc, kk * 8:(kk + 1) * 8, :] += jnp.sum(p.reshape(rc // 8, 8, lane), axis=0)
            return carry

        lax.fori_loop(0, nc, chunk, 0)
        for cc in range(nc):
            cols = slice(cc * lane, (cc + 1) * lane)
            dgl = dglu[cc]
            du_ref[:, cc * lane:(cc + 1) * lane] = (dgl * sg[:, cols]).astype(BF16)
            du_ref[:, C + cc * lane:C + (cc + 1) * lane] = (dgl * a[:, cols] * sg[:, cols] * (1.0 - sg[:, cols])).astype(BF16)

        @pl.when((b == B - 1) & (j == nj - 1))
        def _():
            dw_ref[...] = jnp.zeros_like(dw_ref)
            for kk in range(CONV_WIDTH):
                dw_ref[:, kk:kk + 1, :] = jnp.sum(dw8[:, kk * 8:(kk + 1) * 8, :], axis=1, keepdims=True)

    a_cur = pl.BlockSpec((ts, C), lambda b, j: (b * nj + j, 0))
    g_cur = pl.BlockSpec((ts, C), lambda b, j: (b * nj + j, 1))
    a_prev = pl.BlockSpec((CONV_HALO, C), lambda b, j: (jnp.maximum((b * nj + j) * hb - 1, 0), 0))
    g_prev = pl.BlockSpec((CONV_HALO, C), lambda b, j: (jnp.maximum((b * nj + j) * hb - 1, 0), 1))
    d_next = pl.BlockSpec((CONV_HALO, C), lambda b, j: (jnp.minimum((b * nj + j + 1) * hb, n_halo_blocks - 1), 0))
    wspec = pl.BlockSpec((nc, CONV_HALO, lane), lambda b, j: (0, 0, 0))
    return pl.pallas_call(
        body,
        name="conv_bwd",
        grid=(B, nj),
        out_shape=(jax.ShapeDtypeStruct((T, C2), BF16), jax.ShapeDtypeStruct((nc, CONV_HALO, lane), F32)),
        in_specs=[a_cur, d_next, a_cur, g_cur, a_prev, g_prev, wspec],
        out_specs=(pl.BlockSpec((ts, C2), lambda b, j: (b * nj + j, 0)), wspec),
        scratch_shapes=[
            pltpu.VMEM((nc, CONV_HALO + ts, lane), F32),
            pltpu.VMEM((nc, ts + CONV_HALO, lane), F32),
            pltpu.VMEM((nc, ts, lane), F32),
            pltpu.VMEM((nc, CONV_HALO * 8, lane), F32),
        ],
        compiler_params=_params("arbitrary", "arbitrary"),
    )(dcv, dcv, u, u, u, u, w3)


def _loss_head(x, tgt, gfin, tm_pref=256):
    T, D = x.shape
    tm = _pick(T, tm_pref, 16)

    def body(x_ref, t_ref, g_ref, dx_ref, loss_ref, dg_ref):
        @pl.when(pl.program_id(0) == 0)
        def _():
            loss_ref[...] = jnp.zeros_like(loss_ref)
            dg_ref[...] = jnp.zeros_like(dg_ref)

        xv = x_ref[...]
        g = g_ref[...]
        r = lax.rsqrt(jnp.mean(xv * xv, axis=-1, keepdims=True) + EPS)
        xhat = xv * r
        e = xhat * g - t_ref[...]
        row_loss = jnp.mean(e * e, axis=-1, keepdims=True)
        loss_ref[...] += 0.5 * jnp.sum(row_loss, axis=0, keepdims=True)
        dy = e * (1.0 / D)
        dg_ref[...] += jnp.sum(dy * xhat, axis=0, keepdims=True)
        dxhat = dy * g
        dx_ref[...] = r * (dxhat - xhat * jnp.mean(dxhat * xhat, axis=-1, keepdims=True))

    row = pl.BlockSpec((tm, D), lambda i: (i, 0))
    return pl.pallas_call(
        body,
        name="loss_head",
        grid=(T // tm,),
        out_shape=(
            jax.ShapeDtypeStruct((T, D), F32),
            jax.ShapeDtypeStruct((8, 128), F32),
            jax.ShapeDtypeStruct((1, D), F32),
        ),
        in_specs=[row, row, pl.BlockSpec((1, D), lambda i: (0, 0))],
        out_specs=(row, pl.BlockSpec((8, 128), lambda i: (0, 0)), pl.BlockSpec((1, D), lambda i: (0, 0))),
        compiler_params=_params("arbitrary"),
    )(x, tgt, gfin)


def _adam(name, parts, w, m, v, tm_pref=256):
    P, R, C = parts.shape
    tm = _pick(R, tm_pref, 16)
    c1 = 1.0 - ADAM_B1 ** ADAM_STEP
    c2 = 1.0 - ADAM_B2 ** ADAM_STEP

    def body(p_ref, w_ref, m_ref, v_ref, g_ref, d_ref, mo_ref, vo_ref):
        g = p_ref[0].astype(F32)
        for i in range(1, P):
            g = g + p_ref[i].astype(F32)
        m_new = ADAM_B1 * m_ref[...] + (1.0 - ADAM_B1) * g
        v_new = ADAM_B2 * v_ref[...] + (1.0 - ADAM_B2) * (g * g)
        m_hat = m_new / c1
        v_hat = v_new / c2
        g_ref[...] = g
        d_ref[...] = -ADAM_LR * (m_hat / (jnp.sqrt(v_hat) + ADAM_EPS) + ADAM_WD * w_ref[...])
        mo_ref[...] = m_new
        vo_ref[...] = v_new

    row = pl.BlockSpec((tm, C), lambda i: (i, 0))
    return pl.pallas_call(
        body,
        name=name,
        grid=(R // tm,),
        out_shape=(jax.ShapeDtypeStruct((R, C), F32),) * 4,
        in_specs=[pl.BlockSpec((P, tm, C), lambda i: (0, i, 0)), row, row, row],
        out_specs=(row, row, row, row),
        compiler_params=_params("parallel"),
    )(parts, w, m, v)


def _heads(a, B, S, n):
    return a.reshape(B, S, n, HEAD_DIM).transpose(0, 2, 1, 3)


def _tokens(a):
    B, n, S, hd = a.shape
    return a.transpose(0, 2, 1, 3).reshape(B * S, n * hd)


BIG = ("w_qkv", "w_o", "w_pw1", "w_pw2", "w_up", "w_down")
SMALL_SHARDED = (("b_pw1", 1), ("w_dw", 2), ("b_dw", 1), ("conv_ln_g", 1), ("conv_ln_b", 1), ("b_pw2", 1))
SMALL_REPL = ("b_mod", "norm_mix", "norm_mlp", "b_qkv", "b_o", "sinks", "final_norm")
WEIGHTS = ("w_mod", "b_mod", "norm_mix", "norm_mlp", "w_qkv", "b_qkv", "w_o", "b_o", "sinks", "w_pw1", "b_pw1",
           "w_dw", "b_dw", "conv_ln_g", "conv_ln_b", "w_pw2", "b_pw2", "w_up", "w_down", "final_norm")


def _step(x, c, loss_target, W, M, V):
    B, S, D = x.shape
    T = B * S
    L = W["w_mod"].shape[0]
    n_mod = W["w_mod"].shape[2]
    me = 4 * lax.axis_index("x") + 2 * lax.axis_index("y") + lax.axis_index("c")
    q_dim = N_HEADS * HEAD_DIM
    kv_dim = N_KV_HEADS * HEAD_DIM

    Wb = {n: W[n].astype(BF16) for n in BIG}

    def shards(i):
        jm = i // 2
        first, last = ("w_qkv", "w_o") if i % 2 == 0 else ("w_pw1", "w_pw2")
        return [Wb[first][jm], Wb[last][jm], Wb["w_up"][i], Wb["w_down"][i]]

    def carried(res, payload):
        return res if payload else (res, [])

    small_names = [n for n, _ in SMALL_SHARDED]
    small_src, small_sizes = _pack([c] + [W[n] for n in small_names], F32, 0)
    got = _exchange("gather_first", [(small_src, True)] + [(s, True) for s in shards(0)])
    small_parts = _unpack(got[0], small_sizes, [c.shape] + [W[n].shape for n in small_names], 1)
    c_all = small_parts[0].reshape(N_DEV * B, D)
    full = {n: _from_slots(p, d) for (n, d), p in zip(SMALL_SHARDED, small_parts[1:])}
    gathered = [got[1:]] + [None] * (L - 1)

    b_mod_mine = lax.dynamic_slice_in_dim(W["b_mod"], me * n_mod, n_mod, axis=1).reshape(L, 1, n_mod)
    mod_part = _mod_fwd(c_all, W["w_mod"].astype(BF16), b_mod_mine)
    mod_slots = mod_part.reshape(L, N_DEV, B, n_mod).transpose(1, 0, 2, 3).reshape(N_DEV, L * B, n_mod)
    mod_recv = _exchange("scatter_mod", [(mod_slots, False)])[0]
    mod = mod_recv.reshape(N_DEV, L, B, n_mod).transpose(1, 2, 0, 3).reshape(L, B, N_MOD, 1, D)

    w_dw3 = [_conv_weight_chunks(full["w_dw"][j], D) for j in range(full["w_dw"].shape[0])]

    xc = x.reshape(T, D)
    saved = []
    for i in range(L):
        jm = i // 2
        sh1, sc1, g1, sh2, sc2, g2 = (mod[i, :, t] for t in range(N_MOD))
        nxt = [[(s, True)] for s in shards(i + 1)] if i + 1 < L else [[]] * 4
        g_first, g_last, g_up, g_down = gathered[i]
        w_last = g_last.reshape(-1, D)
        w_down_full = g_down.reshape(-1, D)
        sv = {"x_in": xc, "w_last": w_last, "g_up": g_up, "w_down": w_down_full}
        h1 = _normmod_fwd("normmod_mix_fwd", xc, W["norm_mix"][i][None], sc1, sh1, S)
        sv["h1"] = h1
        if i % 2 == 0:
            w_qkv_full = _from_slots(g_first, 1)
            sv["w_first"] = w_qkv_full
            qkv, n_first = carried(_mm("qkv_fwd", h1, w_qkv_full, bias=W["b_qkv"][jm], payload=nxt[0]), nxt[0])
            qh = _heads(qkv[:, :q_dim], B, S, N_HEADS)
            kh = _heads(qkv[:, q_dim:q_dim + kv_dim], B, S, N_KV_HEADS)
            vh = _heads(qkv[:, q_dim + kv_dim:], B, S, N_KV_HEADS)
            tab = _attn_table(W["sinks"][jm])
            oh = _attn_fwd(qh, kh, vh, tab)
            sv.update(qh=qh, kh=kh, vh=vh, oh=oh, tab=tab)
            mix_in = _tokens(oh)
            b_out = W["b_o"][jm]
        else:
            sv["w_first"] = g_first
            u, n_first = carried(_mm("pw1_fwd", h1, g_first, w_form="nslots", bias=full["b_pw1"][jm], out_dtype=F32,
                                     payload=nxt[0]), nxt[0])
            cv, mix_in = _conv_fwd(u, w_dw3[jm], full["b_dw"][jm][None], full["conv_ln_g"][jm][None],
                                   full["conv_ln_b"][jm][None], S)
            sv.update(u=u, cv=cv)
            b_out = full["b_pw2"][jm]
        sv["mix_in"] = mix_in
        (x1, y1), n_last = carried(_mm("mix_out_fwd", mix_in, w_last, bias=b_out, epi="resid", x=xc, gate=g1,
                                       rows_per_batch=S, payload=nxt[1]), nxt[1])
        sv.update(y1=y1, x1=x1)
        h2 = _normmod_fwd("normmod_mlp_fwd", x1, W["norm_mlp"][i][None], sc2, sh2, S)
        up, n_up = carried(_mm("mlp_up_fwd", h2, g_up, w_form="nslots", payload=nxt[2]), nxt[2])
        (x2, y2), n_down = carried(_mm("mlp_down_fwd", up, w_down_full, a_pro="relu2", epi="resid", x=x1, gate=g2,
                                       rows_per_batch=S, tm_pref=256, payload=nxt[3]), nxt[3])
        sv.update(h2=h2, up=up, y2=y2)
        saved.append(sv)
        xc = x2
        if i + 1 < L:
            gathered[i + 1] = [n_first[0], n_last[0], n_up[0], n_down[0]]

    dx, loss_blk, dgfin = _loss_head(xc, loss_target.reshape(T, D), W["final_norm"][None])
    loss = lax.psum(loss_blk[0, 0], ("x", "y", "c"))

    G = {"final_norm": dgfin.reshape(D)}
    dmod_layers = [None] * L
    small_grads = ("norm_mix", "norm_mlp", "b_qkv", "b_o", "sinks", "b_pw1", "w_dw", "b_dw", "conv_ln_g", "conv_ln_b",
                   "b_pw2")
    acc = {n: [None] * W[n].shape[0] for n in small_grads}
    reduced = {n: [None] * W[n].shape[0] for n in BIG}

    for i in reversed(range(L)):
        jm = i // 2
        sv = saved[i]
        sh1, sc1, g1, sh2, sc2, g2 = (mod[i, :, t] for t in range(N_MOD))
        dy2, dg2, _ = _gate_bwd("gate_mlp_bwd", dx, sv["y2"], g2, S)
        gw_down = _mm_tn("w_down_grad", sv["up"], dy2, a_pro="relu2", out_form="kslots", tt_pref=4096)
        du, r = _mm("mlp_down_bwd", dy2, sv["w_down"], w_form="full_t", epi="relu2d", u=sv["up"], tm_pref=256,
                    payload=[(gw_down, False)])
        reduced["w_down"][i] = r[0]
        gw_up = _mm_tn("w_up_grad", sv["h2"], du, out_form="nslots", tt_pref=4096)
        dh2, r = _mm("mlp_up_bwd", du, sv["g_up"], w_form="kslots_t", out_dtype=F32, tm_pref=256,
                     payload=[(gw_up, False)])
        reduced["w_up"][i] = r[0]
        dx, dsh2, dsc2, dgn = _normmod_bwd("normmod_mlp_bwd", dh2, sv["x1"], dx, W["norm_mlp"][i][None], sc2, S)
        acc["norm_mlp"][i] = dgn.reshape(D)
        dy1, dg1, dy1_sum = _gate_bwd("gate_mix_bwd", dx, sv["y1"], g1, S)
        gw_last = _mm_tn("mix_out_grad", sv["mix_in"], dy1)
        gw_last = gw_last.reshape((N_DEV, -1) + gw_last.shape[1:]).astype(BF16)
        if i % 2 == 0:
            acc["b_o"][jm] = dy1_sum.reshape(D)
            do, r = _mm("attn_out_bwd", dy1, sv["w_last"], w_form="full_t", payload=[(gw_last, False)])
            reduced["w_o"][jm] = r[0]
            dqh, dkh, dvh, dsk = _attn_bwd(sv["qh"], sv["kh"], sv["vh"], sv["oh"], _heads(do, B, S, N_HEADS),
                                           sv["tab"])
            acc["sinks"][jm] = jnp.sum(dsk[:, :, :, 0], axis=0).reshape(N_HEADS)
            dqkv = jnp.concatenate([_tokens(dqh), _tokens(dkh), _tokens(dvh)], axis=1)
            acc["b_qkv"][jm] = _colsum("b_qkv_grad", dqkv).reshape(-1)
            gw_qkv = _to_slots(_mm_tn("w_qkv_grad", sv["h1"], dqkv), 1).astype(BF16)
            dh1, r = _mm("qkv_bwd", dqkv, sv["w_first"], w_form="full_t", out_dtype=F32, payload=[(gw_qkv, False)])
            reduced["w_qkv"][jm] = r[0]
        else:
            acc["b_pw2"][jm] = dy1_sum.reshape(D)
            ds, r = _mm("pw2_bwd", dy1, sv["w_last"], w_form="full_t", out_dtype=F32, payload=[(gw_last, False)])
            reduced["w_pw2"][jm] = r[0]
            dcv, dlg, dlb, dbdw = _lnsilu_bwd(ds, sv["cv"], full["conv_ln_g"][jm][None], full["conv_ln_b"][jm][None])
            acc["conv_ln_g"][jm], acc["conv_ln_b"][jm], acc["b_dw"][jm] = dlg.reshape(-1), dlb.reshape(-1), dbdw.reshape(-1)
            du1, dwdw = _conv_bwd(dcv, sv["u"], w_dw3[jm], S)
            acc["w_dw"][jm] = dwdw.transpose(1, 0, 2).reshape(CONV_HALO, D)[:CONV_WIDTH]
            acc["b_pw1"][jm] = _colsum("b_pw1_grad", du1).reshape(-1)
            gw_pw1 = _mm_tn("w_pw1_grad", sv["h1"], du1, out_form="nslots", tt_pref=4096)
            dh1, r = _mm("pw1_bwd", du1, sv["w_first"], w_form="kslots_t", out_dtype=F32, payload=[(gw_pw1, False)])
            reduced["w_pw1"][jm] = r[0]
        dx, dsh1, dsc1, dgn = _normmod_bwd("normmod_mix_bwd", dh1, sv["x_in"], dx, W["norm_mix"][i][None], sc1, S)
        acc["norm_mix"][i] = dgn.reshape(D)
        dmod_layers[i] = jnp.concatenate([dsh1, dsc1, dg1, dsh2, dsc2, dg2], axis=1).reshape(B, N_MOD * D)
    grad_x = dx.reshape(B, S, D)
    for n, parts in acc.items():
        G[n] = jnp.stack(parts)

    dmod = jnp.stack(dmod_layers)
    dmod_slots = dmod.reshape(L, B, N_DEV, n_mod).transpose(2, 0, 1, 3).reshape(N_DEV, L * B, n_mod)
    dmod_recv = _exchange("gather_dmod", [(dmod_slots, False)])[0]
    dmod_all = dmod_recv.reshape(N_DEV, L, B, n_mod).transpose(1, 0, 2, 3).reshape(L, N_DEV * B, n_mod)
    g_w_mod, db_mod_mine = _mod_bwd(c_all, dmod_all)
    G["b_mod"] = lax.dynamic_update_slice_in_dim(jnp.zeros_like(W["b_mod"]), db_mod_mine.reshape(L, n_mod),
                                                 me * n_mod, axis=1)

    small_items = [jnp.broadcast_to(G[n][None], (N_DEV,) + G[n].shape) for n in SMALL_REPL]
    small_items += [_to_slots(G[n], d) for n, d in SMALL_SHARDED]
    small_slots, small_sizes2 = _pack(small_items, F32, 1)
    small_recv = _exchange("reduce_small", [(small_slots, False)])[0]

    out = {}

    def run_adam(name, parts, names):
        shapes = [W[n].shape for n in names]
        wp, sizes = _pack([W[n] for n in names], F32, 0)
        mp, _ = _pack([M[n] for n in names], F32, 0)
        vp, _ = _pack([V[n] for n in names], F32, 0)
        res = _adam(name, parts, wp, mp, vp)
        for kind, buf in zip(("grad", "delta", "new_m", "new_v"), res):
            for n, a in zip(names, _unpack(buf, sizes, shapes, 0)):
                out[kind + "_" + n] = a

    for n in BIG:
        cols = W[n].shape[-1]
        parts = jnp.stack(reduced[n], axis=1).reshape(N_DEV, -1, cols)
        res = _adam("adam_" + n, parts, W[n].reshape(-1, cols), M[n].reshape(-1, cols), V[n].reshape(-1, cols))
        for kind, buf in zip(("grad", "delta", "new_m", "new_v"), res):
            out[kind + "_" + n] = buf.reshape(W[n].shape)
    run_adam("adam_small", small_recv, list(SMALL_REPL) + [n for n, _ in SMALL_SHARDED])
    run_adam("adam_w_mod", _pack([g_w_mod], F32, 0)[0][None], ["w_mod"])

    res = [loss, grad_x]
    for kind in ("grad", "delta", "new_m", "new_v"):
        res += [out[kind + "_" + n] for n in WEIGHTS]
    return tuple(res)


def kernel(x, c, w_mod, b_mod, norm_mix, norm_mlp, w_qkv, b_qkv, w_o, b_o, sinks, w_pw1, b_pw1, w_dw, b_dw, conv_ln_g, conv_ln_b, w_pw2, b_pw2, w_up, w_down, final_norm, loss_target, m_w_mod, m_b_mod, m_norm_mix, m_norm_mlp, m_w_qkv, m_b_qkv, m_w_o, m_b_o, m_sinks, m_w_pw1, m_b_pw1, m_w_dw, m_b_dw, m_conv_ln_g, m_conv_ln_b, m_w_pw2, m_b_pw2, m_w_up, m_w_down, m_final_norm, v_w_mod, v_b_mod, v_norm_mix, v_norm_mlp, v_w_qkv, v_b_qkv, v_w_o, v_b_o, v_sinks, v_w_pw1, v_b_pw1, v_w_dw, v_b_dw, v_conv_ln_g, v_conv_ln_b, v_w_pw2, v_b_pw2, v_w_up, v_w_down, v_final_norm):
    W = dict(w_mod=w_mod, b_mod=b_mod, norm_mix=norm_mix, norm_mlp=norm_mlp, w_qkv=w_qkv, b_qkv=b_qkv, w_o=w_o,
             b_o=b_o, sinks=sinks, w_pw1=w_pw1, b_pw1=b_pw1, w_dw=w_dw, b_dw=b_dw, conv_ln_g=conv_ln_g,
             conv_ln_b=conv_ln_b, w_pw2=w_pw2, b_pw2=b_pw2, w_up=w_up, w_down=w_down, final_norm=final_norm)
    M = dict(w_mod=m_w_mod, b_mod=m_b_mod, norm_mix=m_norm_mix, norm_mlp=m_norm_mlp, w_qkv=m_w_qkv, b_qkv=m_b_qkv,
             w_o=m_w_o, b_o=m_b_o, sinks=m_sinks, w_pw1=m_w_pw1, b_pw1=m_b_pw1, w_dw=m_w_dw, b_dw=m_b_dw,
             conv_ln_g=m_conv_ln_g, conv_ln_b=m_conv_ln_b, w_pw2=m_w_pw2, b_pw2=m_b_pw2, w_up=m_w_up,
             w_down=m_w_down, final_norm=m_final_norm)
    V = dict(w_mod=v_w_mod, b_mod=v_b_mod, norm_mix=v_norm_mix, norm_mlp=v_norm_mlp, w_qkv=v_w_qkv, b_qkv=v_b_qkv,
             w_o=v_w_o, b_o=v_b_o, sinks=v_sinks, w_pw1=v_w_pw1, b_pw1=v_b_pw1, w_dw=v_w_dw, b_dw=v_b_dw,
             conv_ln_g=v_conv_ln_g, conv_ln_b=v_conv_ln_b, w_pw2=v_w_pw2, b_pw2=v_b_pw2, w_up=v_w_up,
             w_down=v_w_down, final_norm=v_final_norm)
    return _step(x, c, loss_target, W, M, V)
```

```python
import functools

import numpy as np
import jax
import jax.numpy as jnp
from jax import lax
from jax.experimental import pallas as pl
from jax.experimental.pallas import tpu as pltpu

F32 = jnp.float32
BF16 = jnp.bfloat16

N_DEV = 8
N_HEADS = 16
N_KV_HEADS = 2
HEAD_DIM = 64
GROUP = N_HEADS // N_KV_HEADS
ATT_BLOCK = 128
CONV_WIDTH = 31
CONV_HALO = 32
N_MOD = 6
EPS = 1e-6
ADAM_LR = 0.001
ADAM_B1 = 0.9
ADAM_B2 = 0.999
ADAM_EPS = 1e-08
ADAM_WD = 0.01
ADAM_STEP = 10
NEG_BIG = -1e30
PACK_COLS = 1024
VMEM_LIMIT_BYTES = 56 * 1024 * 1024
MESH_ID = pl.DeviceIdType.MESH


def _params(*sem):
    return pltpu.CompilerParams(dimension_semantics=sem, vmem_limit_bytes=VMEM_LIMIT_BYTES)


def _pick(n, pref, mult=8):
    if n <= pref:
        return n
    for t in range(pref, 0, -1):
        if n % t == 0 and t % mult == 0:
            return t
    return n


def _sigmoid(z):
    return 1.0 / (1.0 + jnp.exp(-z))


def _payload_layout(payload):
    n = len(payload)
    out_shapes = [jax.ShapeDtypeStruct((N_DEV,) + tuple(a.shape if bc else a.shape[1:]), a.dtype) for a, bc in payload]
    hbm = pl.BlockSpec(memory_space=pl.ANY)
    scratch = [pltpu.SemaphoreType.DMA((n * (N_DEV - 1),)), pltpu.SemaphoreType.DMA((n * (N_DEV - 1),)),
               pltpu.SemaphoreType.DMA((n,))]
    return [a for a, _ in payload], [hbm] * n, out_shapes, [hbm] * n, scratch


def _payload_copies(bcasts, src_refs, dst_refs, send_sems, recv_sems, local_sems):
    x, y, c = lax.axis_index("x"), lax.axis_index("y"), lax.axis_index("c")
    me = 4 * x + 2 * y + c
    copies = []
    for t, (bc, s_ref, d_ref) in enumerate(zip(bcasts, src_refs, dst_refs)):
        for k in range(1, N_DEV):
            px = 1 - x if (k >> 2) & 1 else x
            py = 1 - y if (k >> 1) & 1 else y
            pc = 1 - c if k & 1 else c
            sem = t * (N_DEV - 1) + k - 1
            copies.append(pltpu.make_async_remote_copy(
                src_ref=s_ref if bc else s_ref.at[4 * px + 2 * py + pc],
                dst_ref=d_ref.at[me],
                send_sem=send_sems.at[sem],
                recv_sem=recv_sems.at[sem],
                device_id=(px, py, pc),
                device_id_type=MESH_ID,
            ))
        copies.append(pltpu.make_async_copy(s_ref if bc else s_ref.at[me], d_ref.at[me], local_sems.at[t]))
    return copies


def _exchange(name, payload):
    n = len(payload)
    bcasts = [bc for _, bc in payload]
    arrays, in_specs, out_shapes, out_specs, scratch = _payload_layout(payload)

    def body(*refs):
        copies = _payload_copies(bcasts, refs[:n], refs[n:2 * n], *refs[2 * n:])
        for cp in copies:
            cp.start()
        for cp in copies:
            cp.wait()

    return pl.pallas_call(
        body,
        name=name,
        out_shape=tuple(out_shapes),
        in_specs=in_specs,
        out_specs=tuple(out_specs),
        scratch_shapes=scratch,
    )(*arrays)


def _pack(arrays, dtype, lead):
    lead_shape = arrays[0].shape[:lead]
    flat = [a.astype(dtype).reshape(lead_shape + (-1,)) for a in arrays]
    sizes = [f.shape[-1] for f in flat]
    total = sum(sizes)
    chunk = 16 * PACK_COLS
    padded = -(-total // chunk) * chunk
    if padded > total:
        flat.append(jnp.zeros(lead_shape + (padded - total,), dtype))
    buf = jnp.concatenate(flat, axis=-1)
    return buf.reshape(lead_shape + (padded // PACK_COLS, PACK_COLS)), sizes


def _unpack(buf, sizes, shapes, lead):
    lead_shape = buf.shape[:lead]
    flat = buf.reshape(lead_shape + (-1,))
    out, off = [], 0
    for n, shp in zip(sizes, shapes):
        out.append(lax.slice_in_dim(flat, off, off + n, axis=lead).reshape(lead_shape + tuple(shp)))
        off += n
    return out


def _to_slots(a, dim):
    shp = a.shape
    a = a.reshape(shp[:dim] + (N_DEV, shp[dim] // N_DEV) + shp[dim + 1:])
    return jnp.moveaxis(a, dim, 0)


def _from_slots(a, dim):
    a = jnp.moveaxis(a, 0, dim)
    shp = a.shape
    return a.reshape(shp[:dim] + (shp[dim] * shp[dim + 1],) + shp[dim + 2:])


def _mod_fwd(c_all, w, b):
    L, D, n = w.shape
    M = c_all.shape[0]

    def body(c_ref, w_ref, b_ref, o_ref):
        cv = c_ref[...]
        cs = (cv * _sigmoid(cv)).astype(BF16)
        o_ref[...] = jnp.dot(cs, w_ref[...], preferred_element_type=F32) + b_ref[...]

    return pl.pallas_call(
        body,
        name="mod_fwd",
        grid=(L,),
        out_shape=jax.ShapeDtypeStruct((L, M, n), F32),
        in_specs=[
            pl.BlockSpec((M, D), lambda l: (0, 0)),
            pl.BlockSpec((None, D, n), lambda l: (l, 0, 0)),
            pl.BlockSpec((None, 1, n), lambda l: (l, 0, 0)),
        ],
        out_specs=pl.BlockSpec((None, M, n), lambda l: (l, 0, 0)),
        compiler_params=_params("arbitrary"),
    )(c_all, w, b)


def _mod_bwd(c_all, dmod_all):
    L, M, n = dmod_all.shape
    D = c_all.shape[1]

    def body(c_ref, d_ref, dw_ref, db_ref):
        cv = c_ref[...]
        cs = (cv * _sigmoid(cv)).astype(BF16)
        d = d_ref[...]
        dw_ref[...] = lax.dot_general(cs, d.astype(BF16), (((0,), (0,)), ((), ())), preferred_element_type=F32)
        db_ref[...] = jnp.sum(d, axis=0, keepdims=True)

    return pl.pallas_call(
        body,
        name="mod_bwd",
        grid=(L,),
        out_shape=(jax.ShapeDtypeStruct((L, D, n), F32), jax.ShapeDtypeStruct((L, 1, n), F32)),
        in_specs=[
            pl.BlockSpec((M, D), lambda l: (0, 0)),
            pl.BlockSpec((None, M, n), lambda l: (l, 0, 0)),
        ],
        out_specs=(
            pl.BlockSpec((None, D, n), lambda l: (l, 0, 0)),
            pl.BlockSpec((None, 1, n), lambda l: (l, 0, 0)),
        ),
        compiler_params=_params("arbitrary"),
    )(c_all, dmod_all)


def _mm(name, a, w, *, w_form="full", out_dtype=BF16, bias=None, a_pro=None, epi=None, x=None, gate=None, u=None,
        norm=None, rows_per_batch=None, tm_pref=512, payload=()):
    M, K = a.shape
    if w_form == "full":
        N = w.shape[1]
        nc = _pick(N, 1024, 128)
    elif w_form == "full_t":
        N = w.shape[0]
        nc = _pick(N, 1024, 128)
    elif w_form == "nslots":
        nc = w.shape[2]
        N = N_DEV * nc
    else:
        N = w.shape[1]
        nc = N
    ks = K // N_DEV
    n_chunks = N // nc
    tm = _pick(M if rows_per_batch is None else rows_per_batch, tm_pref, 16)
    steps = M // tm
    has_bias = bias is not None
    n_pay = len(payload)
    bcasts = [bc for _, bc in payload]
    nt = (((1,), (1,)), ((), ()))

    def prologue(v):
        if a_pro == "relu2":
            r = jnp.maximum(v.astype(F32), 0.0)
            return (r * r).astype(BF16)
        return v

    def body(*refs):
        it = iter(refs)
        a_ref = next(it)
        w_ref = next(it)
        b_ref = next(it) if has_bias else None
        x_ref = next(it) if epi == "resid" else None
        g_ref = next(it) if epi == "resid" else None
        gn_ref, sc_ref, sh_ref = (next(it), next(it), next(it)) if norm is not None else (None, None, None)
        u_ref = next(it) if epi == "relu2d" else None
        pay_src = [next(it) for _ in range(n_pay)]
        o_ref = next(it)
        y_ref = next(it) if epi == "resid" else None
        h_ref = next(it) if norm is not None else None
        pay_dst = [next(it) for _ in range(n_pay)]
        sems = list(it)

        if n_pay:
            @pl.when(pl.program_id(0) == 0)
            def _():
                for cp in _payload_copies(bcasts, pay_src, pay_dst, *sems):
                    cp.start()

        av = None if w_form == "kslots_t" else prologue(a_ref[...])
        for ci in range(n_chunks):
            cols = slice(ci * nc, (ci + 1) * nc)
            if w_form == "full":
                acc = jnp.dot(av, w_ref[:, cols], preferred_element_type=F32)
            elif w_form == "full_t":
                acc = lax.dot_general(av, w_ref[cols, :], nt, preferred_element_type=F32)
            elif w_form == "nslots":
                acc = jnp.dot(av, w_ref[ci], preferred_element_type=F32)
            else:
                acc = lax.dot_general(prologue(a_ref[:, 0:ks]), w_ref[0], nt, preferred_element_type=F32)
                for j in range(1, N_DEV):
                    acc = acc + lax.dot_general(prologue(a_ref[:, j * ks:(j + 1) * ks]), w_ref[j], nt,
                                                preferred_element_type=F32)
            if has_bias:
                acc = acc + b_ref[:, cols]
            if epi == "resid":
                xn = x_ref[:, cols] + g_ref[:, cols] * acc
                o_ref[:, cols] = xn
                y_ref[:, cols] = acc.astype(BF16)
                if norm is not None:
                    r = lax.rsqrt(jnp.mean(xn * xn, axis=-1, keepdims=True) + EPS)
                    h_ref[...] = ((xn * r * gn_ref[...]) * (1.0 + sc_ref[...]) + sh_ref[...]).astype(BF16)
            elif epi == "relu2d":
                o_ref[:, cols] = (acc * (2.0 * jnp.maximum(u_ref[:, cols].astype(F32), 0.0))).astype(out_dtype)
            else:
                o_ref[:, cols] = acc.astype(out_dtype)

        if n_pay:
            @pl.when(pl.program_id(0) == steps - 1)
            def _():
                for cp in _payload_copies(bcasts, pay_src, pay_dst, *sems):
                    cp.wait()

    args = [a, w]
    w_block = w.shape
    specs = [pl.BlockSpec((tm, K), lambda i: (i, 0)), pl.BlockSpec(w_block, lambda i: (0,) * len(w_block))]
    if has_bias:
        args.append(bias.reshape(1, N).astype(F32))
        specs.append(pl.BlockSpec((1, N), lambda i: (0, 0)))
    row_spec = pl.BlockSpec((tm, N), lambda i: (i, 0))
    if epi == "resid":
        tpb = rows_per_batch // tm
        vec_spec = pl.BlockSpec((None, 1, N), lambda i: (i // tpb, 0, 0))
        args += [x, gate]
        specs += [row_spec, vec_spec]
        out_shape = [jax.ShapeDtypeStruct((M, N), F32), jax.ShapeDtypeStruct((M, N), BF16)]
        out_specs = [row_spec, row_spec]
        if norm is not None:
            assert n_chunks == 1
            args += list(norm)
            specs += [pl.BlockSpec((1, N), lambda i: (0, 0)), vec_spec, vec_spec]
            out_shape.append(jax.ShapeDtypeStruct((M, N), BF16))
            out_specs.append(row_spec)
    else:
        if epi == "relu2d":
            args.append(u)
            specs.append(row_spec)
        out_shape = [jax.ShapeDtypeStruct((M, N), out_dtype)]
        out_specs = [row_spec]
    scratch = []
    if n_pay:
        p_arrays, p_in, p_shapes, p_out, scratch = _payload_layout(payload)
        args += p_arrays
        specs += p_in
        out_shape += p_shapes
        out_specs += p_out
    res = pl.pallas_call(
        body,
        name=name,
        grid=(steps,),
        out_shape=tuple(out_shape),
        in_specs=specs,
        out_specs=tuple(out_specs),
        scratch_shapes=scratch,
        compiler_params=_params("arbitrary" if n_pay else "parallel"),
    )(*args)
    n_own = (3 if norm is not None else 2) if epi == "resid" else 1
    own = res[0] if n_own == 1 else tuple(res[:n_own])
    return (own, list(res[n_own:])) if n_pay else own


def _mm_tn(name, a, b, *, a_pro=None, out_form="full", tt_pref=1024, tk_pref=1024, tn_pref=1024):
    T, K = a.shape
    N = b.shape[1]
    tt = _pick(T, tt_pref, 16)
    tk = K // N_DEV if out_form == "kslots" else _pick(K, tk_pref, 128)
    tn = N // N_DEV if out_form == "nslots" else _pick(N, tn_pref, 128)
    nt_steps = T // tt

    def body(a_ref, b_ref, o_ref, *acc_ref):
        acc = acc_ref[0] if acc_ref else o_ref

        @pl.when(pl.program_id(2) == 0)
        def _():
            acc[...] = jnp.zeros_like(acc)

        av = a_ref[...]
        if a_pro == "relu2":
            r = jnp.maximum(av.astype(F32), 0.0)
            av = (r * r).astype(BF16)
        acc[...] += lax.dot_general(av, b_ref[...], (((0,), (0,)), ((), ())), preferred_element_type=F32)

        if acc_ref:
            @pl.when(pl.program_id(2) == nt_steps - 1)
            def _():
                o_ref[...] = acc[...].astype(o_ref.dtype)

    if out_form == "full":
        out_shape = jax.ShapeDtypeStruct((K, N), F32)
        out_spec = pl.BlockSpec((tk, tn), lambda k, n, t: (k, n))
        scratch = []
    elif out_form == "nslots":
        out_shape = jax.ShapeDtypeStruct((N_DEV, K, tn), BF16)
        out_spec = pl.BlockSpec((None, tk, tn), lambda k, n, t: (n, k, 0))
        scratch = [pltpu.VMEM((tk, tn), F32)]
    else:
        out_shape = jax.ShapeDtypeStruct((N_DEV, tk, N), BF16)
        out_spec = pl.BlockSpec((None, tk, tn), lambda k, n, t: (k, 0, n))
        scratch = [pltpu.VMEM((tk, tn), F32)]
    return pl.pallas_call(
        body,
        name=name,
        grid=(K // tk, N // tn, nt_steps),
        out_shape=out_shape,
        in_specs=[
            pl.BlockSpec((tt, tk), lambda k, n, t: (t, k)),
            pl.BlockSpec((tt, tn), lambda k, n, t: (t, n)),
        ],
        out_specs=out_spec,
        scratch_shapes=scratch,
        compiler_params=_params("parallel", "parallel", "arbitrary"),
    )(a, b)


def _colsum(name, a, tm_pref=1024):
    T, N = a.shape
    tm = _pick(T, tm_pref, 16)

    def body(a_ref, o_ref):
        @pl.when(pl.program_id(0) == 0)
        def _():
            o_ref[...] = jnp.zeros_like(o_ref)

        o_ref[...] += jnp.sum(a_ref[...].astype(F32), axis=0, keepdims=True)

    return pl.pallas_call(
        body,
        name=name,
        grid=(T // tm,),
        out_shape=jax.ShapeDtypeStruct((1, N), F32),
        in_specs=[pl.BlockSpec((tm, N), lambda i: (i, 0))],
        out_specs=pl.BlockSpec((1, N), lambda i: (0, 0)),
        compiler_params=_params("arbitrary"),
    )(a)


def _normmod_fwd(name, x, gnorm, sc, sh, S, tm_pref=512):
    T, D = x.shape
    tm = _pick(S, tm_pref, 16)
    tpb = S // tm

    def body(x_ref, g_ref, sc_ref, sh_ref, o_ref):
        xv = x_ref[...]
        r = lax.rsqrt(jnp.mean(xv * xv, axis=-1, keepdims=True) + EPS)
        n = xv * r * g_ref[...]
        o_ref[...] = (n * (1.0 + sc_ref[...]) + sh_ref[...]).astype(BF16)

    row = pl.BlockSpec((tm, D), lambda i: (i, 0))
    vec = pl.BlockSpec((None, 1, D), lambda i: (i // tpb, 0, 0))
    return pl.pallas_call(
        body,
        name=name,
        grid=(T // tm,),
        out_shape=jax.ShapeDtypeStruct((T, D), BF16),
        in_specs=[row, pl.BlockSpec((1, D), lambda i: (0, 0)), vec, vec],
        out_specs=row,
        compiler_params=_params("parallel"),
    )(x, gnorm, sc, sh)


def _normmod_bwd(name, dh, x, dx_in, gnorm, sc, S, gate=None, tm_pref=256):
    T, D = x.shape
    B = T // S
    tm = _pick(S, tm_pref, 16)
    tpb = S // tm

    def body(*refs):
        if gate is None:
            dh_ref, x_ref, dxin_ref, g_ref, sc_ref, dx_ref, dsh_ref, dsc_ref, dg_ref = refs
        else:
            (dh_ref, x_ref, dxin_ref, g_ref, sc_ref, y_ref, gt_ref,
             dx_ref, dsh_ref, dsc_ref, dg_ref, dy_ref, dgate_ref, cs_ref) = refs
        b, j = pl.program_id(0), pl.program_id(1)

        @pl.when(j == 0)
        def _():
            dsh_ref[...] = jnp.zeros_like(dsh_ref)
            dsc_ref[...] = jnp.zeros_like(dsc_ref)
            if gate is not None:
                dgate_ref[...] = jnp.zeros_like(dgate_ref)

        @pl.when((b == 0) & (j == 0))
        def _():
            dg_ref[...] = jnp.zeros_like(dg_ref)
            if gate is not None:
                cs_ref[...] = jnp.zeros_like(cs_ref)

        xv = x_ref[...]
        g = g_ref[...]
        r = lax.rsqrt(jnp.mean(xv * xv, axis=-1, keepdims=True) + EPS)
        xhat = xv * r
        n = xhat * g
        dh_v = dh_ref[...]
        dsh_ref[...] += jnp.sum(dh_v, axis=0, keepdims=True)
        dsc_ref[...] += jnp.sum(dh_v * n, axis=0, keepdims=True)
        dn = dh_v * (1.0 + sc_ref[...])
        dg_ref[...] += jnp.sum(dn * xhat, axis=0, keepdims=True)
        dxhat = dn * g
        dx = dxin_ref[...] + r * (dxhat - xhat * jnp.mean(dxhat * xhat, axis=-1, keepdims=True))
        dx_ref[...] = dx
        if gate is not None:
            dy = dx * gt_ref[...]
            dy_ref[...] = dy.astype(BF16)
            dgate_ref[...] += jnp.sum(dx * y_ref[...].astype(F32), axis=0, keepdims=True)
            cs_ref[...] += jnp.sum(dy, axis=0, keepdims=True)

    row = pl.BlockSpec((tm, D), lambda b, j: (b * tpb + j, 0))
    vec = pl.BlockSpec((None, 1, D), lambda b, j: (b, 0, 0))
    one = pl.BlockSpec((1, D), lambda b, j: (0, 0))
    f_row, f_vec, f_one = (jax.ShapeDtypeStruct(s, F32) for s in ((T, D), (B, 1, D), (1, D)))
    args, in_specs = [dh, x, dx_in, gnorm, sc], [row, row, row, one, vec]
    out_shape, out_specs = [f_row, f_vec, f_vec, f_one], [row, vec, vec, one]
    if gate is not None:
        args += list(gate)
        in_specs += [row, vec]
        out_shape += [jax.ShapeDtypeStruct((T, D), BF16), f_vec, f_one]
        out_specs += [row, vec, one]
    return pl.pallas_call(
        body,
        name=name,
        grid=(B, tpb),
        out_shape=tuple(out_shape),
        in_specs=in_specs,
        out_specs=tuple(out_specs),
        compiler_params=_params("arbitrary", "arbitrary"),
    )(*args)


def _gate_bwd(name, dx, y, gate, S, tm_pref=512):
    T, D = dx.shape
    B = T // S
    tm = _pick(S, tm_pref, 16)
    tpb = S // tm

    def body(dx_ref, y_ref, g_ref, dy_ref, dgate_ref, cs_ref):
        b, j = pl.program_id(0), pl.program_id(1)

        @pl.when(j == 0)
        def _():
            dgate_ref[...] = jnp.zeros_like(dgate_ref)

        @pl.when((b == 0) & (j == 0))
        def _():
            cs_ref[...] = jnp.zeros_like(cs_ref)

        dxv = dx_ref[...]
        dy = dxv * g_ref[...]
        dy_ref[...] = dy.astype(BF16)
        dgate_ref[...] += jnp.sum(dxv * y_ref[...].astype(F32), axis=0, keepdims=True)
        cs_ref[...] += jnp.sum(dy, axis=0, keepdims=True)

    row = pl.BlockSpec((tm, D), lambda b, j: (b * tpb + j, 0))
    vec = pl.BlockSpec((None, 1, D), lambda b, j: (b, 0, 0))
    one = pl.BlockSpec((1, D), lambda b, j: (0, 0))
    return pl.pallas_call(
        body,
        name=name,
        grid=(B, tpb),
        out_shape=(
            jax.ShapeDtypeStruct((T, D), BF16),
            jax.ShapeDtypeStruct((B, 1, D), F32),
            jax.ShapeDtypeStruct((1, D), F32),
        ),
        in_specs=[row, row, vec],
        out_specs=(row, vec, one),
        compiler_params=_params("arbitrary", "arbitrary"),
    )(dx, y, gate)


ATT_ROWS = GROUP * ATT_BLOCK
ATT_SPAN = 2 * ATT_BLOCK
ATT_SCALE = HEAD_DIM ** -0.5


def _attn_table(sinks):
    slopes = jnp.asarray(np.array([2.0 ** (-8.0 * (h + 1) / N_HEADS) for h in range(N_HEADS)], np.float32))
    r = jnp.arange(ATT_BLOCK)[:, None]
    cc = jnp.arange(ATT_SPAN)[None, :]
    dist = r + ATT_BLOCK - cc
    ok = (dist >= 0) & (dist < ATT_BLOCK)
    tab = jnp.where(ok[None], -slopes[:, None, None] * dist.astype(F32)[None], NEG_BIG)
    tab = jnp.where((cc == 0)[None], sinks.astype(F32)[:, None, None], tab)
    return tab.reshape(N_KV_HEADS, ATT_ROWS, ATT_SPAN)


KV_COLS = N_KV_HEADS * HEAD_DIM


def _stack_heads(ref):
    return jnp.concatenate([ref[:, g * HEAD_DIM:(g + 1) * HEAD_DIM] for g in range(GROUP)], axis=0)


def _unstack_heads(v):
    return jnp.concatenate([v[g * ATT_BLOCK:(g + 1) * ATT_BLOCK, :] for g in range(GROUP)], axis=1)


def _load_span(prev_ref, cur_ref, buf, kv, mult):
    hd = HEAD_DIM
    row = lax.broadcasted_iota(jnp.int32, (ATT_BLOCK, hd), 0)
    pv2, cv2 = prev_ref[...], cur_ref[...]
    pv = pv2[:, 0:hd]
    cv = cv2[:, 0:hd]
    for h in range(1, N_KV_HEADS):
        pv = jnp.where(kv == h, pv2[:, h * hd:(h + 1) * hd], pv)
        cv = jnp.where(kv == h, cv2[:, h * hd:(h + 1) * hd], cv)
    if mult != 1.0:
        pv = pv * mult
        cv = cv * mult
    buf[0:ATT_BLOCK, :] = jnp.where(row > 0, pv, jnp.zeros_like(pv))
    buf[ATT_BLOCK:, :] = cv


def _first_block_penalty(i):
    col = lax.broadcasted_iota(jnp.int32, (1, ATT_SPAN), 1)
    return jnp.where((col < ATT_BLOCK) & (col > 0), jnp.where(i > 0, 0.0, NEG_BIG), 0.0).astype(F32)


def _attn_probs(qs, kbuf, bias, first_pen):
    nt = (((1,), (1,)), ((), ()))
    s = lax.dot_general(qs, kbuf, nt, preferred_element_type=F32) + bias + first_pen
    m = jnp.max(s, axis=-1, keepdims=True)
    e = jnp.exp(s - m)
    return e * (1.0 / jnp.sum(e, axis=-1, keepdims=True))


def _attn_specs(nb, q_map, row_cur, row_prev):
    gw = GROUP * HEAD_DIM
    kblk = N_KV_HEADS * gw // KV_COLS
    qspec = pl.BlockSpec((ATT_BLOCK, gw), q_map)
    kv_specs = [pl.BlockSpec((ATT_BLOCK, KV_COLS), (lambda b, i, kv, r=r, c=c: (r(b, i), c)))
                for c in (kblk, kblk + 1) for r in (row_cur, row_prev)]
    tspec = pl.BlockSpec((N_KV_HEADS, ATT_ROWS, ATT_SPAN), lambda b, i, kv: (0, 0, 0))
    return qspec, kv_specs, tspec


def _attn_fwd(qkv, tab, B, S):
    T = qkv.shape[0]
    nb = S // ATT_BLOCK
    q_dim = N_HEADS * HEAD_DIM

    def body(q_ref, kc_ref, kp_ref, vc_ref, vp_ref, tab_ref, o_ref, kbuf, vbuf):
        i, kv = pl.program_id(1), pl.program_id(2)
        _load_span(kp_ref, kc_ref, kbuf, kv, ATT_SCALE)
        _load_span(vp_ref, vc_ref, vbuf, kv, 1.0)
        p = _attn_probs(_stack_heads(q_ref), kbuf[...], tab_ref[kv], _first_block_penalty(i))
        o = jnp.dot(p.astype(BF16), vbuf[...], preferred_element_type=F32).astype(BF16)
        o_ref[...] = _unstack_heads(o)

    qspec, kv_specs, tspec = _attn_specs(nb, lambda b, i, kv: (b * nb + i, kv), lambda b, i: b * nb + i,
                                         lambda b, i: b * nb + jnp.maximum(i - 1, 0))
    return pl.pallas_call(
        body,
        name="attn_fwd",
        grid=(B, nb, N_KV_HEADS),
        out_shape=jax.ShapeDtypeStruct((T, q_dim), BF16),
        in_specs=[qspec] + kv_specs + [tspec],
        out_specs=qspec,
        scratch_shapes=[pltpu.VMEM((ATT_SPAN, HEAD_DIM), BF16), pltpu.VMEM((ATT_SPAN, HEAD_DIM), BF16)],
        compiler_params=_params("parallel", "parallel", "arbitrary"),
    )(qkv, qkv, qkv, qkv, qkv, tab)


def _attn_bwd(qkv, o, do, tab, B, S):
    T = qkv.shape[0]
    hd = HEAD_DIM
    nb = S // ATT_BLOCK
    last = nb - 1
    q_dim = N_HEADS * hd
    tn = (((0,), (0,)), ((), ()))
    nt = (((1,), (1,)), ((), ()))

    def body(q_ref, kc_ref, kp_ref, vc_ref, vp_ref, tab_ref, o_ref, do_ref,
             dq_ref, dkv_ref, dsink_ref, carry_k, carry_v, kbuf, vbuf):
        i, kv = pl.program_id(1), pl.program_id(2)

        @pl.when((i == 0) & (kv == 0))
        def _():
            dsink_ref[...] = jnp.zeros_like(dsink_ref)

        def emit(dk_rows, dv_rows):
            for h in range(N_KV_HEADS):
                @pl.when(kv == h)
                def _():
                    dkv_ref[:, h * hd:(h + 1) * hd] = dk_rows.astype(BF16)
                    dkv_ref[:, KV_COLS + h * hd:KV_COLS + (h + 1) * hd] = dv_rows.astype(BF16)

        @pl.when(i < nb)
        def _():
            _load_span(kp_ref, kc_ref, kbuf, kv, ATT_SCALE)
            _load_span(vp_ref, vc_ref, vbuf, kv, 1.0)
            qs = _stack_heads(q_ref)
            dos = _stack_heads(do_ref)
            os_ = _stack_heads(o_ref)
            p = _attn_probs(qs, kbuf[...], tab_ref[kv], _first_block_penalty(i))
            delta = jnp.sum(dos.astype(F32) * os_.astype(F32), axis=-1, keepdims=True)
            dp = lax.dot_general(dos, vbuf[...], nt, preferred_element_type=F32)
            ds = (p * (dp - delta)).astype(BF16)
            dq = jnp.dot(ds, kbuf[...], preferred_element_type=F32).astype(BF16)
            dq_ref[...] = _unstack_heads(dq)
            hg = lax.broadcasted_iota(jnp.int32, (GROUP, ATT_ROWS), 0)
            hr = lax.broadcasted_iota(jnp.int32, (GROUP, ATT_ROWS), 1)
            head_of = jnp.where(hr // ATT_BLOCK == hg, 1.0, 0.0).astype(BF16)
            dsink_ref[kv] += jnp.dot(head_of, ds, preferred_element_type=F32)[:, 0:128]
            keep = lax.broadcasted_iota(jnp.int32, (ATT_SPAN, hd), 0) > 0
            dk = jnp.where(keep, lax.dot_general(ds, qs, tn, preferred_element_type=F32) * ATT_SCALE, 0.0)
            dv = jnp.where(keep, lax.dot_general(p.astype(BF16), dos, tn, preferred_element_type=F32), 0.0)

            @pl.when(i > 0)
            def _():
                emit(carry_k[kv] + dk[0:ATT_BLOCK], carry_v[kv] + dv[0:ATT_BLOCK])

            carry_k[kv] = dk[ATT_BLOCK:]
            carry_v[kv] = dv[ATT_BLOCK:]

        @pl.when(i == nb)
        def _():
            emit(carry_k[kv], carry_v[kv])

    def q_map(b, i, kv):
        return (b * nb + jnp.minimum(i, last), jnp.where(i == nb, N_KV_HEADS - 1, kv))

    qspec, kv_specs, tspec = _attn_specs(nb, q_map, lambda b, i: b * nb + jnp.minimum(i, last),
                                         lambda b, i: b * nb + jnp.clip(i - 1, 0, last))
    dkv = pl.BlockSpec((ATT_BLOCK, 2 * KV_COLS), lambda b, i, kv: (b * nb + jnp.maximum(i - 1, 0), 0))
    dsk = pl.BlockSpec((None, N_KV_HEADS, GROUP, 128), lambda b, i, kv: (b, 0, 0, 0))
    return pl.pallas_call(
        body,
        name="attn_bwd",
        grid=(B, nb + 1, N_KV_HEADS),
        out_shape=(
            jax.ShapeDtypeStruct((T, q_dim), BF16),
            jax.ShapeDtypeStruct((T, 2 * KV_COLS), BF16),
            jax.ShapeDtypeStruct((B, N_KV_HEADS, GROUP, 128), F32),
        ),
        in_specs=[qspec] + kv_specs + [tspec, qspec, qspec],
        out_specs=(qspec, dkv, dsk),
        scratch_shapes=[pltpu.VMEM((N_KV_HEADS, ATT_BLOCK, hd), F32), pltpu.VMEM((N_KV_HEADS, ATT_BLOCK, hd), F32),
                        pltpu.VMEM((ATT_SPAN, hd), BF16), pltpu.VMEM((ATT_SPAN, hd), BF16)],
        compiler_params=_params("arbitrary", "arbitrary", "arbitrary"),
    )(qkv, qkv, qkv, qkv, qkv, tab, o, do)


def _conv_tiles(S):
    ts = _pick(S, 256, CONV_HALO)
    return ts, S // ts


def _conv_chunks(C, ts):
    lane = _pick(C, 128, 128)
    return lane, C // lane, _pick(ts, 64, 8)


def _conv_weight_chunks(w_dw, C):
    lane = _pick(C, 128, 128)
    w = jnp.pad(w_dw, ((0, CONV_HALO - CONV_WIDTH), (0, 0)))
    return w.reshape(CONV_HALO, C // lane, lane).transpose(1, 0, 2)


def _conv_fwd(u, w3, b_dw, ln_g, ln_b, S):
    T, C2 = u.shape
    C = C2 // 2
    B = T // S
    ts, nj = _conv_tiles(S)
    hb = ts // CONV_HALO
    lane, nc, rc = _conv_chunks(C, ts)

    def body(a_ref, g_ref, ap_ref, gp_ref, w_ref, bdw_ref, lg_ref, lb_ref, cv_ref, s_ref, buf, cvb):
        j = pl.program_id(1)
        glu_prev = ap_ref[...] * _sigmoid(gp_ref[...]) * (j > 0).astype(F32)
        glu = a_ref[...] * _sigmoid(g_ref[...])
        for cc in range(nc):
            buf[cc, 0:CONV_HALO, :] = glu_prev[:, cc * lane:(cc + 1) * lane]
            buf[cc, CONV_HALO:, :] = glu[:, cc * lane:(cc + 1) * lane]

        def chunk(cc, carry):
            for r0 in range(0, ts, rc):
                acc = jnp.zeros((rc, lane), F32)
                for kk in range(CONV_WIDTH):
                    lo = CONV_HALO - (CONV_WIDTH - 1 - kk) + r0
                    acc = acc + w_ref[cc, kk:kk + 1, :] * buf[cc, lo:lo + rc, :]
                cvb[cc, r0:r0 + rc, :] = acc
            return carry

        lax.fori_loop(0, nc, chunk, 0)
        for cc in range(nc):
            cv_ref[:, cc * lane:(cc + 1) * lane] = cvb[cc] + bdw_ref[:, cc * lane:(cc + 1) * lane]
        cv = cv_ref[...]
        mu = jnp.mean(cv, axis=-1, keepdims=True)
        xc = cv - mu
        rstd = lax.rsqrt(jnp.mean(xc * xc, axis=-1, keepdims=True) + EPS)
        ln = xc * rstd * lg_ref[...] + lb_ref[...]
        s_ref[...] = (ln * _sigmoid(ln)).astype(BF16)

    a_cur = pl.BlockSpec((ts, C), lambda b, j: (b * nj + j, 0))
    g_cur = pl.BlockSpec((ts, C), lambda b, j: (b * nj + j, 1))
    a_prev = pl.BlockSpec((CONV_HALO, C), lambda b, j: (jnp.maximum((b * nj + j) * hb - 1, 0), 0))
    g_prev = pl.BlockSpec((CONV_HALO, C), lambda b, j: (jnp.maximum((b * nj + j) * hb - 1, 0), 1))
    wspec = pl.BlockSpec((nc, CONV_HALO, lane), lambda b, j: (0, 0, 0))
    one = pl.BlockSpec((1, C), lambda b, j: (0, 0))
    return pl.pallas_call(
        body,
        name="conv_fwd",
        grid=(B, nj),
        out_shape=(jax.ShapeDtypeStruct((T, C), F32), jax.ShapeDtypeStruct((T, C), BF16)),
        in_specs=[a_cur, g_cur, a_prev, g_prev, wspec, one, one, one],
        out_specs=(a_cur, a_cur),
        scratch_shapes=[pltpu.VMEM((nc, CONV_HALO + ts, lane), F32), pltpu.VMEM((nc, ts, lane), F32)],
        compiler_params=_params("parallel", "arbitrary"),
    )(u, u, u, u, w3, b_dw, ln_g, ln_b)


def _lnsilu_bwd(ds, cv, ln_g, ln_b, tm_pref=256):
    T, C = cv.shape
    tm = _pick(T, tm_pref, 16)

    def body(ds_ref, cv_ref, lg_ref, lb_ref, dcv_ref, dlg_ref, dlb_ref, dbdw_ref):
        @pl.when(pl.program_id(0) == 0)
        def _():
            dlg_ref[...] = jnp.zeros_like(dlg_ref)
            dlb_ref[...] = jnp.zeros_like(dlb_ref)
            dbdw_ref[...] = jnp.zeros_like(dbdw_ref)

        cv_v = cv_ref[...]
        g = lg_ref[...]
        mu = jnp.mean(cv_v, axis=-1, keepdims=True)
        xc = cv_v - mu
        rstd = lax.rsqrt(jnp.mean(xc * xc, axis=-1, keepdims=True) + EPS)
        xhat = xc * rstd
        ln = xhat * g + lb_ref[...]
        sg = _sigmoid(ln)
        dln = ds_ref[...] * (sg * (1.0 + ln * (1.0 - sg)))
        dlg_ref[...] += jnp.sum(dln * xhat, axis=0, keepdims=True)
        dlb_ref[...] += jnp.sum(dln, axis=0, keepdims=True)
        dxhat = dln * g
        dcv = rstd * (dxhat - jnp.mean(dxhat, axis=-1, keepdims=True)
                      - xhat * jnp.mean(dxhat * xhat, axis=-1, keepdims=True))
        dcv_ref[...] = dcv
        dbdw_ref[...] += jnp.sum(dcv, axis=0, keepdims=True)

    row = pl.BlockSpec((tm, C), lambda i: (i, 0))
    one = pl.BlockSpec((1, C), lambda i: (0, 0))
    return pl.pallas_call(
        body,
        name="lnsilu_bwd",
        grid=(T // tm,),
        out_shape=(jax.ShapeDtypeStruct((T, C), F32),) + (jax.ShapeDtypeStruct((1, C), F32),) * 3,
        in_specs=[row, row, one, one],
        out_specs=(row, one, one, one),
        compiler_params=_params("arbitrary"),
    )(ds, cv, ln_g, ln_b)


def _conv_bwd(dcv, u, w3, S):
    T, C2 = u.shape
    C = C2 // 2
    B = T // S
    ts, nj = _conv_tiles(S)
    hb = ts // CONV_HALO
    n_halo_blocks = T // CONV_HALO
    lane, nc, rc = _conv_chunks(C, ts)

    def body(dcv_ref, dnx_ref, a_ref, g_ref, ap_ref, gp_ref, w_ref, du_ref, dw_ref, gbuf, dbuf, dglu, dw8):
        b, j = pl.program_id(0), pl.program_id(1)

        @pl.when((b == 0) & (j == 0))
        def _():
            dw8[...] = jnp.zeros_like(dw8)

        a = a_ref[...]
        sg = _sigmoid(g_ref[...])
        glu_prev = ap_ref[...] * _sigmoid(gp_ref[...]) * (j > 0).astype(F32)
        glu = a * sg
        dcur = dcv_ref[...]
        dnext = dnx_ref[...] * (j < nj - 1).astype(F32)
        for cc in range(nc):
            cols = slice(cc * lane, (cc + 1) * lane)
            gbuf[cc, 0:CONV_HALO, :] = glu_prev[:, cols]
            gbuf[cc, CONV_HALO:, :] = glu[:, cols]
            dbuf[cc, 0:ts, :] = dcur[:, cols]
            dbuf[cc, ts:, :] = dnext[:, cols]

        def chunk(cc, carry):
            for r0 in range(0, ts, rc):
                acc = jnp.zeros((rc, lane), F32)
                for kk in range(CONV_WIDTH):
                    d = CONV_WIDTH - 1 - kk
                    acc = acc + w_ref[cc, kk:kk + 1, :] * dbuf[cc, r0 + d:r0 + d + rc, :]
                dglu[cc, r0:r0 + rc, :] = acc
            for kk in range(CONV_WIDTH):
                d = CONV_WIDTH - 1 - kk
                p = jnp.zeros((rc, lane), F32)
                for r0 in range(0, ts, rc):
                    lo = CONV_HALO - d + r0
                    p = p + dbuf[cc, r0:r0 + rc, :] * gbuf[cc, lo:lo + rc, :]
                dw8[cc, kk * 8:(kk + 1) * 8, :] += jnp.sum(p.reshape(rc // 8, 8, lane), axis=0)
            return carry

        lax.fori_loop(0, nc, chunk, 0)
        for cc in range(nc):
            cols = slice(cc * lane, (cc + 1) * lane)
            dgl = dglu[cc]
            du_ref[:, cc * lane:(cc + 1) * lane] = (dgl * sg[:, cols]).astype(BF16)
            du_ref[:, C + cc * lane:C + (cc + 1) * lane] = (dgl * a[:, cols] * sg[:, cols] * (1.0 - sg[:, cols])).astype(BF16)

        @pl.when((b == B - 1) & (j == nj - 1))
        def _():
            dw_ref[...] = jnp.zeros_like(dw_ref)
            for kk in range(CONV_WIDTH):
                dw_ref[:, kk:kk + 1, :] = jnp.sum(dw8[:, kk * 8:(kk + 1) * 8, :], axis=1, keepdims=True)

    a_cur = pl.BlockSpec((ts, C), lambda b, j: (b * nj + j, 0))
    g_cur = pl.BlockSpec((ts, C), lambda b, j: (b * nj + j, 1))
    a_prev = pl.BlockSpec((CONV_HALO, C), lambda b, j: (jnp.maximum((b * nj + j) * hb - 1, 0), 0))
    g_prev = pl.BlockSpec((CONV_HALO, C), lambda b, j: (jnp.maximum((b * nj + j) * hb - 1, 0), 1))
    d_next = pl.BlockSpec((CONV_HALO, C), lambda b, j: (jnp.minimum((b * nj + j + 1) * hb, n_halo_blocks - 1), 0))
    wspec = pl.BlockSpec((nc, CONV_HALO, lane), lambda b, j: (0, 0, 0))
    return pl.pallas_call(
        body,
        name="conv_bwd",
        grid=(B, nj),
        out_shape=(jax.ShapeDtypeStruct((T, C2), BF16), jax.ShapeDtypeStruct((nc, CONV_HALO, lane), F32)),
        in_specs=[a_cur, d_next, a_cur, g_cur, a_prev, g_prev, wspec],
        out_specs=(pl.BlockSpec((ts, C2), lambda b, j: (b * nj + j, 0)), wspec),
        scratch_shapes=[
            pltpu.VMEM((nc, CONV_HALO + ts, lane), F32),
            pltpu.VMEM((nc, ts + CONV_HALO, lane), F32),
            pltpu.VMEM((nc, ts, lane), F32),
            pltpu.VMEM((nc, CONV_HALO * 8, lane), F32),
        ],
        compiler_params=_params("arbitrary", "arbitrary"),
    )(dcv, dcv, u, u, u, u, w3)


def _loss_head(x, tgt, gfin, tm_pref=256):
    T, D = x.shape
    tm = _pick(T, tm_pref, 16)

    def body(x_ref, t_ref, g_ref, dx_ref, loss_ref, dg_ref):
        @pl.when(pl.program_id(0) == 0)
        def _():
            loss_ref[...] = jnp.zeros_like(loss_ref)
            dg_ref[...] = jnp.zeros_like(dg_ref)

        xv = x_ref[...]
        g = g_ref[...]
        r = lax.rsqrt(jnp.mean(xv * xv, axis=-1, keepdims=True) + EPS)
        xhat = xv * r
        e = xhat * g - t_ref[...]
        row_loss = jnp.mean(e * e, axis=-1, keepdims=True)
        loss_ref[...] += 0.5 * jnp.sum(row_loss, axis=0, keepdims=True)
        dy = e * (1.0 / D)
        dg_ref[...] += jnp.sum(dy * xhat, axis=0, keepdims=True)
        dxhat = dy * g
        dx_ref[...] = r * (dxhat - xhat * jnp.mean(dxhat * xhat, axis=-1, keepdims=True))

    row = pl.BlockSpec((tm, D), lambda i: (i, 0))
    return pl.pallas_call(
        body,
        name="loss_head",
        grid=(T // tm,),
        out_shape=(
            jax.ShapeDtypeStruct((T, D), F32),
            jax.ShapeDtypeStruct((8, 128), F32),
            jax.ShapeDtypeStruct((1, D), F32),
        ),
        in_specs=[row, row, pl.BlockSpec((1, D), lambda i: (0, 0))],
        out_specs=(row, pl.BlockSpec((8, 128), lambda i: (0, 0)), pl.BlockSpec((1, D), lambda i: (0, 0))),
        compiler_params=_params("arbitrary"),
    )(x, tgt, gfin)


def _adam(name, parts, w, m, v, tm_pref=256):
    P, R, C = parts.shape
    tm = _pick(R, tm_pref, 16)
    c1 = 1.0 - ADAM_B1 ** ADAM_STEP
    c2 = 1.0 - ADAM_B2 ** ADAM_STEP

    def body(p_ref, w_ref, m_ref, v_ref, g_ref, d_ref, mo_ref, vo_ref):
        g = p_ref[0].astype(F32)
        for i in range(1, P):
            g = g + p_ref[i].astype(F32)
        m_new = ADAM_B1 * m_ref[...] + (1.0 - ADAM_B1) * g
        v_new = ADAM_B2 * v_ref[...] + (1.0 - ADAM_B2) * (g * g)
        m_hat = m_new / c1
        v_hat = v_new / c2
        g_ref[...] = g
        d_ref[...] = -ADAM_LR * (m_hat / (jnp.sqrt(v_hat) + ADAM_EPS) + ADAM_WD * w_ref[...])
        mo_ref[...] = m_new
        vo_ref[...] = v_new

    row = pl.BlockSpec((tm, C), lambda i: (i, 0))
    return pl.pallas_call(
        body,
        name=name,
        grid=(R // tm,),
        out_shape=(jax.ShapeDtypeStruct((R, C), F32),) * 4,
        in_specs=[pl.BlockSpec((P, tm, C), lambda i: (0, i, 0)), row, row, row],
        out_specs=(row, row, row, row),
        compiler_params=_params("parallel"),
    )(parts, w, m, v)


BIG = ("w_qkv", "w_o", "w_pw1", "w_pw2", "w_up", "w_down")
SMALL_SHARDED = (("b_pw1", 1), ("w_dw", 2), ("b_dw", 1), ("conv_ln_g", 1), ("conv_ln_b", 1), ("b_pw2", 1))
SMALL_REPL = ("b_mod", "norm_mix", "norm_mlp", "b_qkv", "b_o", "sinks", "final_norm")
WEIGHTS = ("w_mod", "b_mod", "norm_mix", "norm_mlp", "w_qkv", "b_qkv", "w_o", "b_o", "sinks", "w_pw1", "b_pw1",
           "w_dw", "b_dw", "conv_ln_g", "conv_ln_b", "w_pw2", "b_pw2", "w_up", "w_down", "final_norm")


def _step(x, c, loss_target, W, M, V):
    B, S, D = x.shape
    T = B * S
    L = W["w_mod"].shape[0]
    n_mod = W["w_mod"].shape[2]
    me = 4 * lax.axis_index("x") + 2 * lax.axis_index("y") + lax.axis_index("c")
    q_dim = N_HEADS * HEAD_DIM
    kv_dim = N_KV_HEADS * HEAD_DIM

    Wb = {n: W[n].astype(BF16) for n in BIG}

    def shards(i):
        jm = i // 2
        first, last = ("w_qkv", "w_o") if i % 2 == 0 else ("w_pw1", "w_pw2")
        return [Wb[first][jm], Wb[last][jm], Wb["w_up"][i], Wb["w_down"][i]]

    def carried(res, payload):
        return res if payload else (res, [])

    small_names = [n for n, _ in SMALL_SHARDED]
    small_src, small_sizes = _pack([c] + [W[n] for n in small_names], F32, 0)
    got = _exchange("gather_first", [(small_src, True)] + [(s, True) for s in shards(0)])
    small_parts = _unpack(got[0], small_sizes, [c.shape] + [W[n].shape for n in small_names], 1)
    c_all = small_parts[0].reshape(N_DEV * B, D)
    full = {n: _from_slots(p, d) for (n, d), p in zip(SMALL_SHARDED, small_parts[1:])}
    gathered = [got[1:]] + [None] * (L - 1)

    b_mod_mine = lax.dynamic_slice_in_dim(W["b_mod"], me * n_mod, n_mod, axis=1).reshape(L, 1, n_mod)
    mod_part = _mod_fwd(c_all, W["w_mod"].astype(BF16), b_mod_mine)
    mod_slots = mod_part.reshape(L, N_DEV, B, n_mod).transpose(1, 0, 2, 3).reshape(N_DEV, L * B, n_mod)
    mod_recv = _exchange("scatter_mod", [(mod_slots, False)])[0]
    mod = mod_recv.reshape(N_DEV, L, B, n_mod).transpose(1, 2, 0, 3).reshape(L, B, N_MOD, 1, D)

    w_dw3 = [_conv_weight_chunks(full["w_dw"][j], D) for j in range(full["w_dw"].shape[0])]

    xc = x.reshape(T, D)
    saved = []
    h1 = _normmod_fwd("normmod_mix_fwd", xc, W["norm_mix"][0][None], mod[0, :, 1], mod[0, :, 0], S)
    for i in range(L):
        jm = i // 2
        sh1, sc1, g1, sh2, sc2, g2 = (mod[i, :, t] for t in range(N_MOD))
        nxt = [[(s, True)] for s in shards(i + 1)] if i + 1 < L else [[]] * 4
        g_first, g_last, g_up, g_down = gathered[i]
        w_last = g_last.reshape(-1, D)
        w_down_full = g_down.reshape(-1, D)
        sv = {"x_in": xc, "w_last": w_last, "g_up": g_up, "w_down": w_down_full}
        sv["h1"] = h1
        if i % 2 == 0:
            w_qkv_full = _from_slots(g_first, 1)
            sv["w_first"] = w_qkv_full
            qkv, n_first = carried(_mm("qkv_fwd", h1, w_qkv_full, bias=W["b_qkv"][jm], payload=nxt[0]), nxt[0])
            tab = _attn_table(W["sinks"][jm])
            mix_in = _attn_fwd(qkv, tab, B, S)
            sv.update(qkv=qkv, tab=tab)
            b_out = W["b_o"][jm]
        else:
            sv["w_first"] = g_first
            u, n_first = carried(_mm("pw1_fwd", h1, g_first, w_form="nslots", bias=full["b_pw1"][jm], out_dtype=F32,
                                     payload=nxt[0]), nxt[0])
            cv, mix_in = _conv_fwd(u, w_dw3[jm], full["b_dw"][jm][None], full["conv_ln_g"][jm][None],
                                   full["conv_ln_b"][jm][None], S)
            sv.update(u=u, cv=cv)
            b_out = full["b_pw2"][jm]
        sv["mix_in"] = mix_in
        (x1, y1, h2), n_last = carried(_mm("mix_out_fwd", mix_in, w_last, bias=b_out, epi="resid", x=xc, gate=g1,
                                           norm=(W["norm_mlp"][i][None], sc2, sh2), rows_per_batch=S,
                                           payload=nxt[1]), nxt[1])
        sv.update(y1=y1, x1=x1)
        up, n_up = carried(_mm("mlp_up_fwd", h2, g_up, w_form="nslots", payload=nxt[2]), nxt[2])
        norm_next = (W["norm_mix"][i + 1][None], mod[i + 1, :, 1], mod[i + 1, :, 0]) if i + 1 < L else None
        res, n_down = carried(_mm("mlp_down_fwd", up, w_down_full, a_pro="relu2", epi="resid", x=x1, gate=g2,
                                  norm=norm_next, rows_per_batch=S, tm_pref=256, payload=nxt[3]), nxt[3])
        x2, y2 = res[0], res[1]
        h1 = res[2] if i + 1 < L else None
        sv.update(h2=h2, up=up, y2=y2)
        saved.append(sv)
        xc = x2
        if i + 1 < L:
            gathered[i + 1] = [n_first[0], n_last[0], n_up[0], n_down[0]]

    dx, loss_blk, dgfin = _loss_head(xc, loss_target.reshape(T, D), W["final_norm"][None])
    loss = lax.psum(loss_blk[0, 0], ("x", "y", "c"))

    G = {"final_norm": dgfin.reshape(D)}
    dmod_layers = [None] * L
    small_grads = ("norm_mix", "norm_mlp", "b_qkv", "b_o", "sinks", "b_pw1", "w_dw", "b_dw", "conv_ln_g", "conv_ln_b",
                   "b_pw2")
    acc = {n: [None] * W[n].shape[0] for n in small_grads}
    reduced = {n: [None] * W[n].shape[0] for n in BIG}

    dy2, dg2, _ = _gate_bwd("gate_mlp_bwd", dx, saved[L - 1]["y2"], mod[L - 1, :, 5], S)
    for i in reversed(range(L)):
        jm = i // 2
        sv = saved[i]
        sh1, sc1, g1, sh2, sc2, g2 = (mod[i, :, t] for t in range(N_MOD))
        gw_down = _mm_tn("w_down_grad", sv["up"], dy2, a_pro="relu2", out_form="kslots", tt_pref=4096)
        du, r = _mm("mlp_down_bwd", dy2, sv["w_down"], w_form="full_t", epi="relu2d", u=sv["up"], tm_pref=256,
                    payload=[(gw_down, False)])
        reduced["w_down"][i] = r[0]
        gw_up = _mm_tn("w_up_grad", sv["h2"], du, out_form="nslots", tt_pref=4096)
        dh2, r = _mm("mlp_up_bwd", du, sv["g_up"], w_form="kslots_t", out_dtype=F32, tm_pref=256,
                     payload=[(gw_up, False)])
        reduced["w_up"][i] = r[0]
        dx, dsh2, dsc2, dgn, dy1, dg1, dy1_sum = _normmod_bwd("normmod_mlp_bwd", dh2, sv["x1"], dx,
                                                              W["norm_mlp"][i][None], sc2, S, gate=(sv["y1"], g1))
        acc["norm_mlp"][i] = dgn.reshape(D)
        gw_last = _mm_tn("mix_out_grad", sv["mix_in"], dy1)
        gw_last = gw_last.reshape((N_DEV, -1) + gw_last.shape[1:]).astype(BF16)
        if i % 2 == 0:
            acc["b_o"][jm] = dy1_sum.reshape(D)
            do, r = _mm("attn_out_bwd", dy1, sv["w_last"], w_form="full_t", payload=[(gw_last, False)])
            reduced["w_o"][jm] = r[0]
            dq, dkv, dsk = _attn_bwd(sv["qkv"], sv["mix_in"], do, sv["tab"], B, S)
            acc["sinks"][jm] = jnp.sum(dsk[:, :, :, 0], axis=0).reshape(N_HEADS)
            dqkv = jnp.concatenate([dq, dkv], axis=1)
            acc["b_qkv"][jm] = _colsum("b_qkv_grad", dqkv).reshape(-1)
            gw_qkv = _to_slots(_mm_tn("w_qkv_grad", sv["h1"], dqkv), 1).astype(BF16)
            dh1, r = _mm("qkv_bwd", dqkv, sv["w_first"], w_form="full_t", out_dtype=F32, payload=[(gw_qkv, False)])
            reduced["w_qkv"][jm] = r[0]
        else:
            acc["b_pw2"][jm] = dy1_sum.reshape(D)
            ds, r = _mm("pw2_bwd", dy1, sv["w_last"], w_form="full_t", out_dtype=F32, payload=[(gw_last, False)])
            reduced["w_pw2"][jm] = r[0]
            dcv, dlg, dlb, dbdw = _lnsilu_bwd(ds, sv["cv"], full["conv_ln_g"][jm][None], full["conv_ln_b"][jm][None])
            acc["conv_ln_g"][jm], acc["conv_ln_b"][jm], acc["b_dw"][jm] = dlg.reshape(-1), dlb.reshape(-1), dbdw.reshape(-1)
            du1, dwdw = _conv_bwd(dcv, sv["u"], w_dw3[jm], S)
            acc["w_dw"][jm] = dwdw.transpose(1, 0, 2).reshape(CONV_HALO, D)[:CONV_WIDTH]
            acc["b_pw1"][jm] = _colsum("b_pw1_grad", du1).reshape(-1)
            gw_pw1 = _mm_tn("w_pw1_grad", sv["h1"], du1, out_form="nslots", tt_pref=4096)
            dh1, r = _mm("pw1_bwd", du1, sv["w_first"], w_form="kslots_t", out_dtype=F32, payload=[(gw_pw1, False)])
            reduced["w_pw1"][jm] = r[0]
        below = (saved[i - 1]["y2"], mod[i - 1, :, 5]) if i > 0 else None
        res = _normmod_bwd("normmod_mix_bwd", dh1, sv["x_in"], dx, W["norm_mix"][i][None], sc1, S, gate=below)
        dx, dsh1, dsc1, dgn = res[:4]
        acc["norm_mix"][i] = dgn.reshape(D)
        dmod_layers[i] = jnp.concatenate([dsh1, dsc1, dg1, dsh2, dsc2, dg2], axis=1).reshape(B, N_MOD * D)
        if i > 0:
            dy2, dg2 = res[4], res[5]
    grad_x = dx.reshape(B, S, D)
    for n, parts in acc.items():
        G[n] = jnp.stack(parts)

    dmod = jnp.stack(dmod_layers)
    dmod_slots = dmod.reshape(L, B, N_DEV, n_mod).transpose(2, 0, 1, 3).reshape(N_DEV, L * B, n_mod)
    dmod_recv = _exchange("gather_dmod", [(dmod_slots, False)])[0]
    dmod_all = dmod_recv.reshape(N_DEV, L, B, n_mod).transpose(1, 0, 2, 3).reshape(L, N_DEV * B, n_mod)
    g_w_mod, db_mod_mine = _mod_bwd(c_all, dmod_all)
    G["b_mod"] = lax.dynamic_update_slice_in_dim(jnp.zeros_like(W["b_mod"]), db_mod_mine.reshape(L, n_mod),
                                                 me * n_mod, axis=1)

    small_items = [jnp.broadcast_to(G[n][None], (N_DEV,) + G[n].shape) for n in SMALL_REPL]
    small_items += [_to_slots(G[n], d) for n, d in SMALL_SHARDED]
    small_slots, small_sizes2 = _pack(small_items, F32, 1)
    small_recv = _exchange("reduce_small", [(small_slots, False)])[0]

    out = {}

    def run_adam(name, parts, names):
        shapes = [W[n].shape for n in names]
        wp, sizes = _pack([W[n] for n in names], F32, 0)
        mp, _ = _pack([M[n] for n in names], F32, 0)
        vp, _ = _pack([V[n] for n in names], F32, 0)
        res = _adam(name, parts, wp, mp, vp)
        for kind, buf in zip(("grad", "delta", "new_m", "new_v"), res):
            for n, a in zip(names, _unpack(buf, sizes, shapes, 0)):
                out[kind + "_" + n] = a

    for n in BIG + ("w_mod",):
        cols = W[n].shape[-1]
        if n == "w_mod":
            parts = g_w_mod.reshape(1, -1, cols)
        else:
            parts = jnp.stack(reduced[n], axis=1).reshape(N_DEV, -1, cols)
        res = _adam("adam_" + n, parts, W[n].reshape(-1, cols), M[n].reshape(-1, cols), V[n].reshape(-1, cols))
        for kind, buf in zip(("grad", "delta", "new_m", "new_v"), res):
            out[kind + "_" + n] = buf.reshape(W[n].shape)
    run_adam("adam_small", small_recv, list(SMALL_REPL) + [n for n, _ in SMALL_SHARDED])

    res = [loss, grad_x]
    for kind in ("grad", "delta", "new_m", "new_v"):
        res += [out[kind + "_" + n] for n in WEIGHTS]
    return tuple(res)


def kernel(x, c, w_mod, b_mod, norm_mix, norm_mlp, w_qkv, b_qkv, w_o, b_o, sinks, w_pw1, b_pw1, w_dw, b_dw, conv_ln_g, conv_ln_b, w_pw2, b_pw2, w_up, w_down, final_norm, loss_target, m_w_mod, m_b_mod, m_norm_mix, m_norm_mlp, m_w_qkv, m_b_qkv, m_w_o, m_b_o, m_sinks, m_w_pw1, m_b_pw1, m_w_dw, m_b_dw, m_conv_ln_g, m_conv_ln_b, m_w_pw2, m_b_pw2, m_w_up, m_w_down, m_final_norm, v_w_mod, v_b_mod, v_norm_mix, v_norm_mlp, v_w_qkv, v_b_qkv, v_w_o, v_b_o, v_sinks, v_w_pw1, v_b_pw1, v_w_dw, v_b_dw, v_conv_ln_g, v_conv_ln_b, v_w_pw2, v_b_pw2, v_w_up, v_w_down, v_final_norm):
    W = dict(w_mod=w_mod, b_mod=b_mod, norm_mix=norm_mix, norm_mlp=norm_mlp, w_qkv=w_qkv, b_qkv=b_qkv, w_o=w_o,
             b_o=b_o, sinks=sinks, w_pw1=w_pw1, b_pw1=b_pw1, w_dw=w_dw, b_dw=b_dw, conv_ln_g=conv_ln_g,
             conv_ln_b=conv_ln_b, w_pw2=w_pw2, b_pw2=b_pw2, w_up=w_up, w_down=w_down, final_norm=final_norm)
    M = dict(w_mod=m_w_mod, b_mod=m_b_mod, norm_mix=m_norm_mix, norm_mlp=m_norm_mlp, w_qkv=m_w_qkv, b_qkv=m_b_qkv,
             w_o=m_w_o, b_o=m_b_o, sinks=m_sinks, w_pw1=m_w_pw1, b_pw1=m_b_pw1, w_dw=m_w_dw, b_dw=m_b_dw,
             conv_ln_g=m_conv_ln_g, conv_ln_b=m_conv_ln_b, w_pw2=m_w_pw2, b_pw2=m_b_pw2, w_up=m_w_up,
             w_down=m_w_down, final_norm=m_final_norm)
    V = dict(w_mod=v_w_mod, b_mod=v_b_mod, norm_mix=v_norm_mix, norm_mlp=v_norm_mlp, w_qkv=v_w_qkv, b_qkv=v_b_qkv,
             w_o=v_w_o, b_o=v_b_o, sinks=v_sinks, w_pw1=v_w_pw1, b_pw1=v_b_pw1, w_dw=v_w_dw, b_dw=v_b_dw,
             conv_ln_g=v_conv_ln_g, conv_ln_b=v_conv_ln_b, w_pw2=v_w_pw2, b_pw2=v_b_pw2, w_up=v_w_up,
             w_down=v_w_down, final_norm=v_final_norm)
    return _step(x, c, loss_target, W, M, V)
```

```python
import functools

import numpy as np
import jax
import jax.numpy as jnp
from jax import lax
from jax.experimental import pallas as pl
from jax.experimental.pallas import tpu as pltpu

F32 = jnp.float32
BF16 = jnp.bfloat16

N_DEV = 8
N_HEADS = 16
N_KV_HEADS = 2
HEAD_DIM = 64
GROUP = N_HEADS // N_KV_HEADS
ATT_BLOCK = 128
CONV_WIDTH = 31
CONV_HALO = 32
N_MOD = 6
EPS = 1e-6
ADAM_LR = 0.001
ADAM_B1 = 0.9
ADAM_B2 = 0.999
ADAM_EPS = 1e-08
ADAM_WD = 0.01
ADAM_STEP = 10
NEG_BIG = -1e30
PACK_COLS = 1024
VMEM_LIMIT_BYTES = 56 * 1024 * 1024
MESH_ID = pl.DeviceIdType.MESH


def _params(*sem):
    return pltpu.CompilerParams(dimension_semantics=sem, vmem_limit_bytes=VMEM_LIMIT_BYTES)


def _pick(n, pref, mult=8):
    if n <= pref:
        return n
    for t in range(pref, 0, -1):
        if n % t == 0 and t % mult == 0:
            return t
    return n


def _sigmoid(z):
    return 0.5 * jnp.tanh(0.5 * z) + 0.5


def _payload_layout(payload):
    n = len(payload)
    out_shapes = [jax.ShapeDtypeStruct((N_DEV,) + tuple(a.shape if bc else a.shape[1:]), a.dtype) for a, bc in payload]
    hbm = pl.BlockSpec(memory_space=pl.ANY)
    scratch = [pltpu.SemaphoreType.DMA((n * (N_DEV - 1),)), pltpu.SemaphoreType.DMA((n * (N_DEV - 1),)),
               pltpu.SemaphoreType.DMA((n,))]
    return [a for a, _ in payload], [hbm] * n, out_shapes, [hbm] * n, scratch


class _Plan:
    def __init__(self, bcasts, src_refs, dst_refs, send_sems, recv_sems, local_sems):
        x, y, c = lax.axis_index("x"), lax.axis_index("y"), lax.axis_index("c")
        me = 4 * x + 2 * y + c
        self.first, self.landed, self.relay, self.local = [], [], [], []
        for t, (bc, s_ref, d_ref) in enumerate(zip(bcasts, src_refs, dst_refs)):
            def remote(k, src, slot, to):
                sem = t * (N_DEV - 1) + k
                return pltpu.make_async_remote_copy(src_ref=src, dst_ref=d_ref.at[slot], send_sem=send_sems.at[sem],
                                                    recv_sem=recv_sems.at[sem], device_id=to, device_id_type=MESH_ID)
            if bc:
                chips = [(1 - x, y), (x, 1 - y), (1 - x, 1 - y)]
                self.first.append(remote(0, s_ref, me, (x, y, 1 - c)))
                for j, (px, py) in enumerate(chips):
                    cp = remote(1 + j, s_ref, me, (px, py, c))
                    self.first.append(cp)
                    self.landed.append(cp)
                    theirs = 4 * px + 2 * py + c
                    self.relay.append(remote(4 + j, d_ref.at[theirs], theirs, (x, y, 1 - c)))
                self.local.append(pltpu.make_async_copy(s_ref, d_ref.at[me], local_sems.at[t]))
            else:
                for k in range(1, N_DEV):
                    px = 1 - x if (k >> 2) & 1 else x
                    py = 1 - y if (k >> 1) & 1 else y
                    pc = 1 - c if k & 1 else c
                    self.first.append(remote(k - 1, s_ref.at[4 * px + 2 * py + pc], me, (px, py, pc)))
                self.local.append(pltpu.make_async_copy(s_ref.at[me], d_ref.at[me], local_sems.at[t]))

    def start(self):
        for cp in self.first + self.local:
            cp.start()

    def pass_on(self):
        for cp in self.landed:
            cp.wait_recv()
        for cp in self.relay:
            cp.start()

    def finish(self):
        for cp in self.first:
            cp.wait_send()
            if not any(cp is l for l in self.landed):
                cp.wait_recv()
        for cp in self.relay:
            cp.wait()
        for cp in self.local:
            cp.wait()


def _exchange(name, payload):
    n = len(payload)
    bcasts = [bc for _, bc in payload]
    arrays, in_specs, out_shapes, out_specs, scratch = _payload_layout(payload)

    def body(*refs):
        plan = _Plan(bcasts, refs[:n], refs[n:2 * n], *refs[2 * n:])
        plan.start()
        plan.pass_on()
        plan.finish()

    return pl.pallas_call(
        body,
        name=name,
        out_shape=tuple(out_shapes),
        in_specs=in_specs,
        out_specs=tuple(out_specs),
        scratch_shapes=scratch,
    )(*arrays)


def _pack(arrays, dtype, lead):
    lead_shape = arrays[0].shape[:lead]
    flat = [a.astype(dtype).reshape(lead_shape + (-1,)) for a in arrays]
    sizes = [f.shape[-1] for f in flat]
    total = sum(sizes)
    chunk = 16 * PACK_COLS
    padded = -(-total // chunk) * chunk
    if padded > total:
        flat.append(jnp.zeros(lead_shape + (padded - total,), dtype))
    buf = jnp.concatenate(flat, axis=-1)
    return buf.reshape(lead_shape + (padded // PACK_COLS, PACK_COLS)), sizes


def _unpack(buf, sizes, shapes, lead):
    lead_shape = buf.shape[:lead]
    flat = buf.reshape(lead_shape + (-1,))
    out, off = [], 0
    for n, shp in zip(sizes, shapes):
        out.append(lax.slice_in_dim(flat, off, off + n, axis=lead).reshape(lead_shape + tuple(shp)))
        off += n
    return out


def _to_slots(a, dim):
    shp = a.shape
    a = a.reshape(shp[:dim] + (N_DEV, shp[dim] // N_DEV) + shp[dim + 1:])
    return jnp.moveaxis(a, dim, 0)


def _from_slots(a, dim):
    a = jnp.moveaxis(a, 0, dim)
    shp = a.shape
    return a.reshape(shp[:dim] + (shp[dim] * shp[dim + 1],) + shp[dim + 2:])


def _mod_fwd(c_all, w, b):
    L, D, n = w.shape
    M = c_all.shape[0]

    def body(c_ref, w_ref, b_ref, o_ref):
        cv = c_ref[...]
        cs = (cv * _sigmoid(cv)).astype(BF16)
        o_ref[...] = jnp.dot(cs, w_ref[...], preferred_element_type=F32) + b_ref[...]

    return pl.pallas_call(
        body,
        name="mod_fwd",
        grid=(L,),
        out_shape=jax.ShapeDtypeStruct((L, M, n), F32),
        in_specs=[
            pl.BlockSpec((M, D), lambda l: (0, 0)),
            pl.BlockSpec((None, D, n), lambda l: (l, 0, 0)),
            pl.BlockSpec((None, 1, n), lambda l: (l, 0, 0)),
        ],
        out_specs=pl.BlockSpec((None, M, n), lambda l: (l, 0, 0)),
        compiler_params=_params("arbitrary"),
    )(c_all, w, b)


def _mod_bwd(c_all, dmod_all):
    L, M, n = dmod_all.shape
    D = c_all.shape[1]

    def body(c_ref, d_ref, dw_ref, db_ref):
        cv = c_ref[...]
        cs = (cv * _sigmoid(cv)).astype(BF16)
        d = d_ref[...]
        dw_ref[...] = lax.dot_general(cs, d.astype(BF16), (((0,), (0,)), ((), ())), preferred_element_type=F32)
        db_ref[...] = jnp.sum(d, axis=0, keepdims=True)

    return pl.pallas_call(
        body,
        name="mod_bwd",
        grid=(L,),
        out_shape=(jax.ShapeDtypeStruct((L, D, n), F32), jax.ShapeDtypeStruct((L, 1, n), F32)),
        in_specs=[
            pl.BlockSpec((M, D), lambda l: (0, 0)),
            pl.BlockSpec((None, M, n), lambda l: (l, 0, 0)),
        ],
        out_specs=(
            pl.BlockSpec((None, D, n), lambda l: (l, 0, 0)),
            pl.BlockSpec((None, 1, n), lambda l: (l, 0, 0)),
        ),
        compiler_params=_params("arbitrary"),
    )(c_all, dmod_all)


def _mm(name, a, w, *, w_form="full", out_dtype=BF16, bias=None, a_pro=None, epi=None, x=None, gate=None, u=None,
        norm=None, nb=None, rows_per_batch=None, tm_pref=512, payload=()):
    M, K = a.shape
    if w_form == "full":
        N = w.shape[1]
        nc = _pick(N, 1024, 128)
    elif w_form == "full_t":
        N = w.shape[0]
        nc = _pick(N, 1024, 128)
    elif w_form == "nslots":
        nc = w.shape[2]
        N = N_DEV * nc
    else:
        N = w.shape[1]
        nc = N
    ks = K // N_DEV
    n_chunks = N // nc
    tm = _pick(M if rows_per_batch is None else rows_per_batch, tm_pref, 16)
    steps = M // tm
    relay_step = (3 * steps) // 4
    nb_gate = nb is not None and nb.get("gate") is not None
    tpb_nb = rows_per_batch // tm if epi == "normbwd" else 1
    has_bias = bias is not None
    n_pay = len(payload)
    bcasts = [bc for _, bc in payload]
    nt = (((1,), (1,)), ((), ()))

    def prologue(v):
        if a_pro == "relu2":
            r = jnp.maximum(v.astype(F32), 0.0)
            return (r * r).astype(BF16)
        return v

    def body(*refs):
        it = iter(refs)
        a_ref = next(it)
        w_ref = next(it)
        b_ref = next(it) if has_bias else None
        x_ref = next(it) if epi == "resid" else None
        g_ref = next(it) if epi == "resid" else None
        gn_ref, sc_ref, sh_ref = (next(it), next(it), next(it)) if norm is not None else (None, None, None)
        u_ref = next(it) if epi == "relu2d" else None
        if epi == "normbwd":
            nx_ref, ndx_ref, ngn_ref, nsc_ref = next(it), next(it), next(it), next(it)
            ny_ref, ngt_ref = (next(it), next(it)) if nb_gate else (None, None)
        pay_src = [next(it) for _ in range(n_pay)]
        o_ref = next(it)
        y_ref = next(it) if epi == "resid" else None
        h_ref = next(it) if norm is not None else None
        if epi == "normbwd":
            dsh_ref, dsc_ref, dgn_ref = next(it), next(it), next(it)
            dy_ref, dgate_ref, cs_ref = (next(it), next(it), next(it)) if nb_gate else (None, None, None)
        pay_dst = [next(it) for _ in range(n_pay)]
        sems = list(it)

        if epi == "normbwd":
            @pl.when(pl.program_id(0) % tpb_nb == 0)
            def _():
                dsh_ref[...] = jnp.zeros_like(dsh_ref)
                dsc_ref[...] = jnp.zeros_like(dsc_ref)
                if nb_gate:
                    dgate_ref[...] = jnp.zeros_like(dgate_ref)

            @pl.when(pl.program_id(0) == 0)
            def _():
                dgn_ref[...] = jnp.zeros_like(dgn_ref)
                if nb_gate:
                    cs_ref[...] = jnp.zeros_like(cs_ref)

        if n_pay:
            @pl.when(pl.program_id(0) == 0)
            def _():
                _Plan(bcasts, pay_src, pay_dst, *sems).start()

            @pl.when(pl.program_id(0) == relay_step)
            def _():
                _Plan(bcasts, pay_src, pay_dst, *sems).pass_on()

        av = None if w_form == "kslots_t" else prologue(a_ref[...])
        for ci in range(n_chunks):
            cols = slice(ci * nc, (ci + 1) * nc)
            if w_form == "full":
                acc = jnp.dot(av, w_ref[:, cols], preferred_element_type=F32)
            elif w_form == "full_t":
                acc = lax.dot_general(av, w_ref[cols, :], nt, preferred_element_type=F32)
            elif w_form == "nslots":
                acc = jnp.dot(av, w_ref[ci], preferred_element_type=F32)
            else:
                acc = lax.dot_general(prologue(a_ref[:, 0:ks]), w_ref[0], nt, preferred_element_type=F32)
                for j in range(1, N_DEV):
                    acc = acc + lax.dot_general(prologue(a_ref[:, j * ks:(j + 1) * ks]), w_ref[j], nt,
                                                preferred_element_type=F32)
            if has_bias:
                acc = acc + b_ref[:, cols]
            if epi == "resid":
                xn = x_ref[:, cols] + g_ref[:, cols] * acc
                o_ref[:, cols] = xn
                y_ref[:, cols] = acc.astype(BF16)
                if norm is not None:
                    r = lax.rsqrt(jnp.mean(xn * xn, axis=-1, keepdims=True) + EPS)
                    h_ref[...] = ((xn * r * gn_ref[...]) * (1.0 + sc_ref[...]) + sh_ref[...]).astype(BF16)
            elif epi == "relu2d":
                o_ref[:, cols] = (acc * (2.0 * jnp.maximum(u_ref[:, cols].astype(F32), 0.0))).astype(out_dtype)
            elif epi == "normbwd":
                xv = nx_ref[...]
                gn = ngn_ref[...]
                r = lax.rsqrt(jnp.mean(xv * xv, axis=-1, keepdims=True) + EPS)
                xhat = xv * r
                dsh_ref[...] += jnp.sum(acc, axis=0, keepdims=True)
                dsc_ref[...] += jnp.sum(acc * (xhat * gn), axis=0, keepdims=True)
                dn = acc * (1.0 + nsc_ref[...])
                dgn_ref[...] += jnp.sum(dn * xhat, axis=0, keepdims=True)
                dxhat = dn * gn
                dx = ndx_ref[...] + r * (dxhat - xhat * jnp.mean(dxhat * xhat, axis=-1, keepdims=True))
                o_ref[...] = dx
                if nb_gate:
                    dy = dx * ngt_ref[...]
                    dy_ref[...] = dy.astype(BF16)
                    dgate_ref[...] += jnp.sum(dx * ny_ref[...].astype(F32), axis=0, keepdims=True)
                    cs_ref[...] += jnp.sum(dy, axis=0, keepdims=True)
            else:
                o_ref[:, cols] = acc.astype(out_dtype)

        if n_pay:
            @pl.when(pl.program_id(0) == steps - 1)
            def _():
                _Plan(bcasts, pay_src, pay_dst, *sems).finish()

    args = [a, w]
    w_block = w.shape
    specs = [pl.BlockSpec((tm, K), lambda i: (i, 0)), pl.BlockSpec(w_block, lambda i: (0,) * len(w_block))]
    if has_bias:
        args.append(bias.reshape(1, N).astype(F32))
        specs.append(pl.BlockSpec((1, N), lambda i: (0, 0)))
    row_spec = pl.BlockSpec((tm, N), lambda i: (i, 0))
    if epi == "resid":
        tpb = rows_per_batch // tm
        vec_spec = pl.BlockSpec((None, 1, N), lambda i: (i // tpb, 0, 0))
        args += [x, gate]
        specs += [row_spec, vec_spec]
        out_shape = [jax.ShapeDtypeStruct((M, N), F32), jax.ShapeDtypeStruct((M, N), BF16)]
        out_specs = [row_spec, row_spec]
        if norm is not None:
            assert n_chunks == 1
            args += list(norm)
            specs += [pl.BlockSpec((1, N), lambda i: (0, 0)), vec_spec, vec_spec]
            out_shape.append(jax.ShapeDtypeStruct((M, N), BF16))
            out_specs.append(row_spec)
    elif epi == "normbwd":
        assert n_chunks == 1
        vec_spec = pl.BlockSpec((None, 1, N), lambda i: (i // tpb_nb, 0, 0))
        one_spec = pl.BlockSpec((1, N), lambda i: (0, 0))
        nbat = M // rows_per_batch
        f_vec, f_one = jax.ShapeDtypeStruct((nbat, 1, N), F32), jax.ShapeDtypeStruct((1, N), F32)
        args += [nb["x"], nb["dx_in"], nb["gnorm"], nb["sc"]]
        specs += [row_spec, row_spec, one_spec, vec_spec]
        out_shape = [jax.ShapeDtypeStruct((M, N), F32), f_vec, f_vec, f_one]
        out_specs = [row_spec, vec_spec, vec_spec, one_spec]
        if nb_gate:
            args += list(nb["gate"])
            specs += [row_spec, vec_spec]
            out_shape += [jax.ShapeDtypeStruct((M, N), BF16), f_vec, f_one]
            out_specs += [row_spec, vec_spec, one_spec]
    else:
        if epi == "relu2d":
            args.append(u)
            specs.append(row_spec)
        out_shape = [jax.ShapeDtypeStruct((M, N), out_dtype)]
        out_specs = [row_spec]
    scratch = []
    if n_pay:
        p_arrays, p_in, p_shapes, p_out, scratch = _payload_layout(payload)
        args += p_arrays
        specs += p_in
        out_shape += p_shapes
        out_specs += p_out
    res = pl.pallas_call(
        body,
        name=name,
        grid=(steps,),
        out_shape=tuple(out_shape),
        in_specs=specs,
        out_specs=tuple(out_specs),
        scratch_shapes=scratch,
        compiler_params=_params("arbitrary" if (n_pay or epi == "normbwd") else "parallel"),
    )(*args)
    if epi == "resid":
        n_own = 3 if norm is not None else 2
    elif epi == "normbwd":
        n_own = 7 if nb_gate else 4
    else:
        n_own = 1
    own = res[0] if n_own == 1 else tuple(res[:n_own])
    return (own, list(res[n_own:])) if n_pay else own


def _mm_tn(name, a, b, *, a_pro=None, out_form="full", tt_pref=1024, tk_pref=1024, tn_pref=1024):
    T, K = a.shape
    N = b.shape[1]
    tt = _pick(T, tt_pref, 16)
    tk = K // N_DEV if out_form == "kslots" else _pick(K, tk_pref, 128)
    tn = N // N_DEV if out_form == "nslots" else _pick(N, tn_pref, 128)
    nt_steps = T // tt

    def body(a_ref, b_ref, o_ref, *acc_ref):
        acc = acc_ref[0] if acc_ref else o_ref

        @pl.when(pl.program_id(2) == 0)
        def _():
            acc[...] = jnp.zeros_like(acc)

        av = a_ref[...]
        if a_pro == "relu2":
            r = jnp.maximum(av.astype(F32), 0.0)
            av = (r * r).astype(BF16)
        acc[...] += lax.dot_general(av, b_ref[...], (((0,), (0,)), ((), ())), preferred_element_type=F32)

        if acc_ref:
            @pl.when(pl.program_id(2) == nt_steps - 1)
            def _():
                o_ref[...] = acc[...].astype(o_ref.dtype)

    if out_form == "full":
        out_shape = jax.ShapeDtypeStruct((K, N), F32)
        out_spec = pl.BlockSpec((tk, tn), lambda k, n, t: (k, n))
        scratch = []
    elif out_form == "nslots":
        out_shape = jax.ShapeDtypeStruct((N_DEV, K, tn), BF16)
        out_spec = pl.BlockSpec((None, tk, tn), lambda k, n, t: (n, k, 0))
        scratch = [pltpu.VMEM((tk, tn), F32)]
    else:
        out_shape = jax.ShapeDtypeStruct((N_DEV, tk, N), BF16)
        out_spec = pl.BlockSpec((None, tk, tn), lambda k, n, t: (k, 0, n))
        scratch = [pltpu.VMEM((tk, tn), F32)]
    return pl.pallas_call(
        body,
        name=name,
        grid=(K // tk, N // tn, nt_steps),
        out_shape=out_shape,
        in_specs=[
            pl.BlockSpec((tt, tk), lambda k, n, t: (t, k)),
            pl.BlockSpec((tt, tn), lambda k, n, t: (t, n)),
        ],
        out_specs=out_spec,
        scratch_shapes=scratch,
        compiler_params=_params("parallel", "parallel", "arbitrary"),
    )(a, b)


def _colsum(name, a, tm_pref=1024):
    T, N = a.shape
    tm = _pick(T, tm_pref, 16)

    def body(a_ref, o_ref):
        @pl.when(pl.program_id(0) == 0)
        def _():
            o_ref[...] = jnp.zeros_like(o_ref)

        o_ref[...] += jnp.sum(a_ref[...].astype(F32), axis=0, keepdims=True)

    return pl.pallas_call(
        body,
        name=name,
        grid=(T // tm,),
        out_shape=jax.ShapeDtypeStruct((1, N), F32),
        in_specs=[pl.BlockSpec((tm, N), lambda i: (i, 0))],
        out_specs=pl.BlockSpec((1, N), lambda i: (0, 0)),
        compiler_params=_params("arbitrary"),
    )(a)


def _normmod_fwd(name, x, gnorm, sc, sh, S, tm_pref=512):
    T, D = x.shape
    tm = _pick(S, tm_pref, 16)
    tpb = S // tm

    def body(x_ref, g_ref, sc_ref, sh_ref, o_ref):
        xv = x_ref[...]
        r = lax.rsqrt(jnp.mean(xv * xv, axis=-1, keepdims=True) + EPS)
        n = xv * r * g_ref[...]
        o_ref[...] = (n * (1.0 + sc_ref[...]) + sh_ref[...]).astype(BF16)

    row = pl.BlockSpec((tm, D), lambda i: (i, 0))
    vec = pl.BlockSpec((None, 1, D), lambda i: (i // tpb, 0, 0))
    return pl.pallas_call(
        body,
        name=name,
        grid=(T // tm,),
        out_shape=jax.ShapeDtypeStruct((T, D), BF16),
        in_specs=[row, pl.BlockSpec((1, D), lambda i: (0, 0)), vec, vec],
        out_specs=row,
        compiler_params=_params("parallel"),
    )(x, gnorm, sc, sh)


def _gate_bwd(name, dx, y, gate, S, tm_pref=512):
    T, D = dx.shape
    B = T // S
    tm = _pick(S, tm_pref, 16)
    tpb = S // tm

    def body(dx_ref, y_ref, g_ref, dy_ref, dgate_ref, cs_ref):
        b, j = pl.program_id(0), pl.program_id(1)

        @pl.when(j == 0)
        def _():
            dgate_ref[...] = jnp.zeros_like(dgate_ref)

        @pl.when((b == 0) & (j == 0))
        def _():
            cs_ref[...] = jnp.zeros_like(cs_ref)

        dxv = dx_ref[...]
        dy = dxv * g_ref[...]
        dy_ref[...] = dy.astype(BF16)
        dgate_ref[...] += jnp.sum(dxv * y_ref[...].astype(F32), axis=0, keepdims=True)
        cs_ref[...] += jnp.sum(dy, axis=0, keepdims=True)

    row = pl.BlockSpec((tm, D), lambda b, j: (b * tpb + j, 0))
    vec = pl.BlockSpec((None, 1, D), lambda b, j: (b, 0, 0))
    one = pl.BlockSpec((1, D), lambda b, j: (0, 0))
    return pl.pallas_call(
        body,
        name=name,
        grid=(B, tpb),
        out_shape=(
            jax.ShapeDtypeStruct((T, D), BF16),
            jax.ShapeDtypeStruct((B, 1, D), F32),
            jax.ShapeDtypeStruct((1, D), F32),
        ),
        in_specs=[row, row, vec],
        out_specs=(row, vec, one),
        compiler_params=_params("arbitrary", "arbitrary"),
    )(dx, y, gate)


ATT_ROWS = GROUP * ATT_BLOCK
ATT_SPAN = 2 * ATT_BLOCK
ATT_SCALE = HEAD_DIM ** -0.5


def _attn_table(sinks):
    slopes = jnp.asarray(np.array([2.0 ** (-8.0 * (h + 1) / N_HEADS) for h in range(N_HEADS)], np.float32))
    r = jnp.arange(ATT_BLOCK)[:, None]
    cc = jnp.arange(ATT_SPAN)[None, :]
    dist = r + ATT_BLOCK - cc
    ok = (dist >= 0) & (dist < ATT_BLOCK)
    tab = jnp.where(ok[None], -slopes[:, None, None] * dist.astype(F32)[None], NEG_BIG)
    tab = jnp.where((cc == 0)[None], sinks.astype(F32)[:, None, None], tab)
    return tab.reshape(N_KV_HEADS, ATT_ROWS, ATT_SPAN)


KV_COLS = N_KV_HEADS * HEAD_DIM


def _stack_heads(ref):
    return jnp.concatenate([ref[:, g * HEAD_DIM:(g + 1) * HEAD_DIM] for g in range(GROUP)], axis=0)


def _unstack_heads(v):
    return jnp.concatenate([v[g * ATT_BLOCK:(g + 1) * ATT_BLOCK, :] for g in range(GROUP)], axis=1)


def _load_span(prev_ref, cur_ref, buf, kv, mult):
    hd = HEAD_DIM
    row = lax.broadcasted_iota(jnp.int32, (ATT_BLOCK, hd), 0)
    pv2, cv2 = prev_ref[...], cur_ref[...]
    pv = pv2[:, 0:hd]
    cv = cv2[:, 0:hd]
    for h in range(1, N_KV_HEADS):
        pv = jnp.where(kv == h, pv2[:, h * hd:(h + 1) * hd], pv)
        cv = jnp.where(kv == h, cv2[:, h * hd:(h + 1) * hd], cv)
    if mult != 1.0:
        pv = pv * mult
        cv = cv * mult
    buf[0:ATT_BLOCK, :] = jnp.where(row > 0, pv, jnp.zeros_like(pv))
    buf[ATT_BLOCK:, :] = cv


def _first_block_penalty(i):
    col = lax.broadcasted_iota(jnp.int32, (1, ATT_SPAN), 1)
    return jnp.where((col < ATT_BLOCK) & (col > 0), jnp.where(i > 0, 0.0, NEG_BIG), 0.0).astype(F32)


def _attn_probs(qs, kbuf, bias, first_pen):
    nt = (((1,), (1,)), ((), ()))
    s = lax.dot_general(qs, kbuf, nt, preferred_element_type=F32) + bias + first_pen
    m = jnp.max(s, axis=-1, keepdims=True)
    e = jnp.exp(s - m)
    return e * (1.0 / jnp.sum(e, axis=-1, keepdims=True))


def _attn_specs(nb, q_map, row_cur, row_prev):
    gw = GROUP * HEAD_DIM
    kblk = N_KV_HEADS * gw // KV_COLS
    qspec = pl.BlockSpec((ATT_BLOCK, gw), q_map)
    kv_specs = [pl.BlockSpec((ATT_BLOCK, KV_COLS), (lambda b, i, kv, r=r, c=c: (r(b, i), c)))
                for c in (kblk, kblk + 1) for r in (row_cur, row_prev)]
    tspec = pl.BlockSpec((N_KV_HEADS, ATT_ROWS, ATT_SPAN), lambda b, i, kv: (0, 0, 0))
    return qspec, kv_specs, tspec


def _attn_fwd(qkv, tab, B, S):
    T = qkv.shape[0]
    nb = S // ATT_BLOCK
    q_dim = N_HEADS * HEAD_DIM

    def body(q_ref, kc_ref, kp_ref, vc_ref, vp_ref, tab_ref, o_ref, kbuf, vbuf):
        i, kv = pl.program_id(1), pl.program_id(2)
        _load_span(kp_ref, kc_ref, kbuf, kv, ATT_SCALE)
        _load_span(vp_ref, vc_ref, vbuf, kv, 1.0)
        p = _attn_probs(_stack_heads(q_ref), kbuf[...], tab_ref[kv], _first_block_penalty(i))
        o = jnp.dot(p.astype(BF16), vbuf[...], preferred_element_type=F32).astype(BF16)
        o_ref[...] = _unstack_heads(o)

    qspec, kv_specs, tspec = _attn_specs(nb, lambda b, i, kv: (b * nb + i, kv), lambda b, i: b * nb + i,
                                         lambda b, i: b * nb + jnp.maximum(i - 1, 0))
    return pl.pallas_call(
        body,
        name="attn_fwd",
        grid=(B, nb, N_KV_HEADS),
        out_shape=jax.ShapeDtypeStruct((T, q_dim), BF16),
        in_specs=[qspec] + kv_specs + [tspec],
        out_specs=qspec,
        scratch_shapes=[pltpu.VMEM((ATT_SPAN, HEAD_DIM), BF16), pltpu.VMEM((ATT_SPAN, HEAD_DIM), BF16)],
        compiler_params=_params("parallel", "parallel", "arbitrary"),
    )(qkv, qkv, qkv, qkv, qkv, tab)


def _attn_bwd(qkv, o, do, tab, B, S):
    T = qkv.shape[0]
    hd = HEAD_DIM
    nb = S // ATT_BLOCK
    last = nb - 1
    q_dim = N_HEADS * hd
    tn = (((0,), (0,)), ((), ()))
    nt = (((1,), (1,)), ((), ()))

    def body(q_ref, kc_ref, kp_ref, vc_ref, vp_ref, tab_ref, o_ref, do_ref,
             dq_ref, dkv_ref, dsink_ref, carry_k, carry_v, kbuf, vbuf):
        i, kv = pl.program_id(1), pl.program_id(2)

        @pl.when((i == 0) & (kv == 0))
        def _():
            dsink_ref[...] = jnp.zeros_like(dsink_ref)

        def emit(dk_rows, dv_rows):
            for h in range(N_KV_HEADS):
                @pl.when(kv == h)
                def _():
                    dkv_ref[:, h * hd:(h + 1) * hd] = dk_rows.astype(BF16)
                    dkv_ref[:, KV_COLS + h * hd:KV_COLS + (h + 1) * hd] = dv_rows.astype(BF16)

        @pl.when(i < nb)
        def _():
            _load_span(kp_ref, kc_ref, kbuf, kv, ATT_SCALE)
            _load_span(vp_ref, vc_ref, vbuf, kv, 1.0)
            qs = _stack_heads(q_ref)
            dos = _stack_heads(do_ref)
            os_ = _stack_heads(o_ref)
            p = _attn_probs(qs, kbuf[...], tab_ref[kv], _first_block_penalty(i))
            delta = jnp.sum(dos.astype(F32) * os_.astype(F32), axis=-1, keepdims=True)
            dp = lax.dot_general(dos, vbuf[...], nt, preferred_element_type=F32)
            ds = (p * (dp - delta)).astype(BF16)
            dq = jnp.dot(ds, kbuf[...], preferred_element_type=F32).astype(BF16)
            dq_ref[...] = _unstack_heads(dq)
            hg = lax.broadcasted_iota(jnp.int32, (GROUP, ATT_ROWS), 0)
            hr = lax.broadcasted_iota(jnp.int32, (GROUP, ATT_ROWS), 1)
            head_of = jnp.where(hr // ATT_BLOCK == hg, 1.0, 0.0).astype(BF16)
            dsink_ref[kv] += jnp.dot(head_of, ds, preferred_element_type=F32)[:, 0:128]
            keep = lax.broadcasted_iota(jnp.int32, (ATT_SPAN, hd), 0) > 0
            dk = jnp.where(keep, lax.dot_general(ds, qs, tn, preferred_element_type=F32) * ATT_SCALE, 0.0)
            dv = jnp.where(keep, lax.dot_general(p.astype(BF16), dos, tn, preferred_element_type=F32), 0.0)

            @pl.when(i > 0)
            def _():
                emit(carry_k[kv] + dk[0:ATT_BLOCK], carry_v[kv] + dv[0:ATT_BLOCK])

            carry_k[kv] = dk[ATT_BLOCK:]
            carry_v[kv] = dv[ATT_BLOCK:]

        @pl.when(i == nb)
        def _():
            emit(carry_k[kv], carry_v[kv])

    def q_map(b, i, kv):
        return (b * nb + jnp.minimum(i, last), jnp.where(i == nb, N_KV_HEADS - 1, kv))

    qspec, kv_specs, tspec = _attn_specs(nb, q_map, lambda b, i: b * nb + jnp.minimum(i, last),
                                         lambda b, i: b * nb + jnp.clip(i - 1, 0, last))
    dkv = pl.BlockSpec((ATT_BLOCK, 2 * KV_COLS), lambda b, i, kv: (b * nb + jnp.maximum(i - 1, 0), 0))
    dsk = pl.BlockSpec((None, N_KV_HEADS, GROUP, 128), lambda b, i, kv: (b, 0, 0, 0))
    return pl.pallas_call(
        body,
        name="attn_bwd",
        grid=(B, nb + 1, N_KV_HEADS),
        out_shape=(
            jax.ShapeDtypeStruct((T, q_dim), BF16),
            jax.ShapeDtypeStruct((T, 2 * KV_COLS), BF16),
            jax.ShapeDtypeStruct((B, N_KV_HEADS, GROUP, 128), F32),
        ),
        in_specs=[qspec] + kv_specs + [tspec, qspec, qspec],
        out_specs=(qspec, dkv, dsk),
        scratch_shapes=[pltpu.VMEM((N_KV_HEADS, ATT_BLOCK, hd), F32), pltpu.VMEM((N_KV_HEADS, ATT_BLOCK, hd), F32),
                        pltpu.VMEM((ATT_SPAN, hd), BF16), pltpu.VMEM((ATT_SPAN, hd), BF16)],
        compiler_params=_params("arbitrary", "arbitrary", "arbitrary"),
    )(qkv, qkv, qkv, qkv, qkv, tab, o, do)


def _conv_tiles(S):
    ts = _pick(S, 256, CONV_HALO)
    return ts, S // ts


def _conv_chunks(C, ts):
    lane = _pick(C, 128, 128)
    return lane, C // lane, _pick(ts, 64, 8)


def _conv_weight_chunks(w_dw, C):
    lane = _pick(C, 128, 128)
    w = jnp.pad(w_dw, ((0, CONV_HALO - CONV_WIDTH), (0, 0)))
    return w.reshape(CONV_HALO, C // lane, lane).transpose(1, 0, 2)


def _conv_fwd(u, w3, b_dw, ln_g, ln_b, S):
    T, C2 = u.shape
    C = C2 // 2
    B = T // S
    ts, nj = _conv_tiles(S)
    hb = ts // CONV_HALO
    lane, nc, rc = _conv_chunks(C, ts)

    def body(a_ref, g_ref, ap_ref, gp_ref, w_ref, bdw_ref, lg_ref, lb_ref, cv_ref, s_ref, buf, cvb):
        j = pl.program_id(1)
        glu_prev = ap_ref[...] * _sigmoid(gp_ref[...]) * (j > 0).astype(F32)
        glu = a_ref[...] * _sigmoid(g_ref[...])
        for cc in range(nc):
            buf[cc, 0:CONV_HALO, :] = glu_prev[:, cc * lane:(cc + 1) * lane]
            buf[cc, CONV_HALO:, :] = glu[:, cc * lane:(cc + 1) * lane]

        def chunk(cc, carry):
            for r0 in range(0, ts, rc):
                acc = jnp.zeros((rc, lane), F32)
                for kk in range(CONV_WIDTH):
                    lo = CONV_HALO - (CONV_WIDTH - 1 - kk) + r0
                    acc = acc + w_ref[cc, kk:kk + 1, :] * buf[cc, lo:lo + rc, :]
                cvb[cc, r0:r0 + rc, :] = acc
            return carry

        lax.fori_loop(0, nc, chunk, 0)
        for cc in range(nc):
            cv_ref[:, cc * lane:(cc + 1) * lane] = cvb[cc] + bdw_ref[:, cc * lane:(cc + 1) * lane]
        cv = cv_ref[...]
        mu = jnp.mean(cv, axis=-1, keepdims=True)
        xc = cv - mu
        rstd = lax.rsqrt(jnp.mean(xc * xc, axis=-1, keepdims=True) + EPS)
        ln = xc * rstd * lg_ref[...] + lb_ref[...]
        s_ref[...] = (ln * _sigmoid(ln)).astype(BF16)

    a_cur = pl.BlockSpec((ts, C), lambda b, j: (b * nj + j, 0))
    g_cur = pl.BlockSpec((ts, C), lambda b, j: (b * nj + j, 1))
    a_prev = pl.BlockSpec((CONV_HALO, C), lambda b, j: (jnp.maximum((b * nj + j) * hb - 1, 0), 0))
    g_prev = pl.BlockSpec((CONV_HALO, C), lambda b, j: (jnp.maximum((b * nj + j) * hb - 1, 0), 1))
    wspec = pl.BlockSpec((nc, CONV_HALO, lane), lambda b, j: (0, 0, 0))
    one = pl.BlockSpec((1, C), lambda b, j: (0, 0))
    return pl.pallas_call(
        body,
        name="conv_fwd",
        grid=(B, nj),
        out_shape=(jax.ShapeDtypeStruct((T, C), F32), jax.ShapeDtypeStruct((T, C), BF16)),
        in_specs=[a_cur, g_cur, a_prev, g_prev, wspec, one, one, one],
        out_specs=(a_cur, a_cur),
        scratch_shapes=[pltpu.VMEM((nc, CONV_HALO + ts, lane), F32), pltpu.VMEM((nc, ts, lane), F32)],
        compiler_params=_params("parallel", "arbitrary"),
    )(u, u, u, u, w3, b_dw, ln_g, ln_b)


def _lnsilu_bwd(ds, cv, ln_g, ln_b, tm_pref=256):
    T, C = cv.shape
    tm = _pick(T, tm_pref, 16)

    def body(ds_ref, cv_ref, lg_ref, lb_ref, dcv_ref, dlg_ref, dlb_ref, dbdw_ref):
        @pl.when(pl.program_id(0) == 0)
        def _():
            dlg_ref[...] = jnp.zeros_like(dlg_ref)
            dlb_ref[...] = jnp.zeros_like(dlb_ref)
            dbdw_ref[...] = jnp.zeros_like(dbdw_ref)

        cv_v = cv_ref[...]
        g = lg_ref[...]
        mu = jnp.mean(cv_v, axis=-1, keepdims=True)
        xc = cv_v - mu
        rstd = lax.rsqrt(jnp.mean(xc * xc, axis=-1, keepdims=True) + EPS)
        xhat = xc * rstd
        ln = xhat * g + lb_ref[...]
        sg = _sigmoid(ln)
        dln = ds_ref[...] * (sg * (1.0 + ln * (1.0 - sg)))
        dlg_ref[...] += jnp.sum(dln * xhat, axis=0, keepdims=True)
        dlb_ref[...] += jnp.sum(dln, axis=0, keepdims=True)
        dxhat = dln * g
        dcv = rstd * (dxhat - jnp.mean(dxhat, axis=-1, keepdims=True)
                      - xhat * jnp.mean(dxhat * xhat, axis=-1, keepdims=True))
        dcv_ref[...] = dcv
        dbdw_ref[...] += jnp.sum(dcv, axis=0, keepdims=True)

    row = pl.BlockSpec((tm, C), lambda i: (i, 0))
    one = pl.BlockSpec((1, C), lambda i: (0, 0))
    return pl.pallas_call(
        body,
        name="lnsilu_bwd",
        grid=(T // tm,),
        out_shape=(jax.ShapeDtypeStruct((T, C), F32),) + (jax.ShapeDtypeStruct((1, C), F32),) * 3,
        in_specs=[row, row, one, one],
        out_specs=(row, one, one, one),
        compiler_params=_params("arbitrary"),
    )(ds, cv, ln_g, ln_b)


def _conv_bwd(dcv, u, w3, S):
    T, C2 = u.shape
    C = C2 // 2
    B = T // S
    ts, nj = _conv_tiles(S)
    hb = ts // CONV_HALO
    n_halo_blocks = T // CONV_HALO
    lane, nc, rc = _conv_chunks(C, ts)

    def body(dcv_ref, dnx_ref, a_ref, g_ref, ap_ref, gp_ref, w_ref, du_ref, dw_ref, gbuf, dbuf, dglu, dw8):
        b, j = pl.program_id(0), pl.program_id(1)

        @pl.when((b == 0) & (j == 0))
        def _():
            dw8[...] = jnp.zeros_like(dw8)

        a = a_ref[...]
        sg = _sigmoid(g_ref[...])
        glu_prev = ap_ref[...] * _sigmoid(gp_ref[...]) * (j > 0).astype(F32)
        glu = a * sg
        dcur = dcv_ref[...]
        dnext = dnx_ref[...] * (j < nj - 1).astype(F32)
        for cc in range(nc):
            cols = slice(cc * lane, (cc + 1) * lane)
            gbuf[cc, 0:CONV_HALO, :] = glu_prev[:, cols]
            gbuf[cc, CONV_HALO:, :] = glu[:, cols]
            dbuf[cc, 0:ts, :] = dcur[:, cols]
            dbuf[cc, ts:, :] = dnext[:, cols]

        def chunk(cc, carry):
            for r0 in range(0, ts, rc):
                acc = jnp.zeros((rc, lane), F32)
                for kk in range(CONV_WIDTH):
                    d = CONV_WIDTH - 1 - kk
                    acc = acc + w_ref[cc, kk:kk + 1, :] * dbuf[cc, r0 + d:r0 + d + rc, :]
                dglu[cc, r0:r0 + rc, :] = acc
            for kk in range(CONV_WIDTH):
                d = CONV_WIDTH - 1 - kk
                p = jnp.zeros((rc, lane), F32)
                for r0 in range(0, ts, rc):
                    lo = CONV_HALO - d + r0
                    p = p + dbuf[cc, r0:r0 + rc, :] * gbuf[cc, lo:lo + rc, :]
                dw8[cc, kk * 8:(kk + 1) * 8, :] += jnp.sum(p.reshape(rc // 8, 8, lane), axis=0)
            return carry

        lax.fori_loop(0, nc, chunk, 0)
        for cc in range(nc):
            cols = slice(cc * lane, (cc + 1) * lane)
            dgl = dglu[cc]
            du_ref[:, cc * lane:(cc + 1) * lane] = (dgl * sg[:, cols]).astype(BF16)
            du_ref[:, C + cc * lane:C + (cc + 1) * lane] = (dgl * a[:, cols] * sg[:, cols] * (1.0 - sg[:, cols])).astype(BF16)

        @pl.when((b == B - 1) & (j == nj - 1))
        def _():
            dw_ref[...] = jnp.zeros_like(dw_ref)
            for kk in range(CONV_WIDTH):
                dw_ref[:, kk:kk + 1, :] = jnp.sum(dw8[:, kk * 8:(kk + 1) * 8, :], axis=1, keepdims=True)

    a_cur = pl.BlockSpec((ts, C), lambda b, j: (b * nj + j, 0))
    g_cur = pl.BlockSpec((ts, C), lambda b, j: (b * nj + j, 1))
    a_prev = pl.BlockSpec((CONV_HALO, C), lambda b, j: (jnp.maximum((b * nj + j) * hb - 1, 0), 0))
    g_prev = pl.BlockSpec((CONV_HALO, C), lambda b, j: (jnp.maximum((b * nj + j) * hb - 1, 0), 1))
    d_next = pl.BlockSpec((CONV_HALO, C), lambda b, j: (jnp.minimum((b * nj + j + 1) * hb, n_halo_blocks - 1), 0))
    wspec = pl.BlockSpec((nc, CONV_HALO, lane), lambda b, j: (0, 0, 0))
    return pl.pallas_call(
        body,
        name="conv_bwd",
        grid=(B, nj),
        out_shape=(jax.ShapeDtypeStruct((T, C2), BF16), jax.ShapeDtypeStruct((nc, CONV_HALO, lane), F32)),
        in_specs=[a_cur, d_next, a_cur, g_cur, a_prev, g_prev, wspec],
        out_specs=(pl.BlockSpec((ts, C2), lambda b, j: (b * nj + j, 0)), wspec),
        scratch_shapes=[
            pltpu.VMEM((nc, CONV_HALO + ts, lane), F32),
            pltpu.VMEM((nc, ts + CONV_HALO, lane), F32),
            pltpu.VMEM((nc, ts, lane), F32),
            pltpu.VMEM((nc, CONV_HALO * 8, lane), F32),
        ],
        compiler_params=_params("arbitrary", "arbitrary"),
    )(dcv, dcv, u, u, u, u, w3)


def _loss_head(x, tgt, gfin, tm_pref=256):
    T, D = x.shape
    tm = _pick(T, tm_pref, 16)

    def body(x_ref, t_ref, g_ref, dx_ref, loss_ref, dg_ref):
        @pl.when(pl.program_id(0) == 0)
        def _():
            loss_ref[...] = jnp.zeros_like(loss_ref)
            dg_ref[...] = jnp.zeros_like(dg_ref)

        xv = x_ref[...]
        g = g_ref[...]
        r = lax.rsqrt(jnp.mean(xv * xv, axis=-1, keepdims=True) + EPS)
        xhat = xv * r
        e = xhat * g - t_ref[...]
        row_loss = jnp.mean(e * e, axis=-1, keepdims=True)
        loss_ref[...] += 0.5 * jnp.sum(row_loss, axis=0, keepdims=True)
        dy = e * (1.0 / D)
        dg_ref[...] += jnp.sum(dy * xhat, axis=0, keepdims=True)
        dxhat = dy * g
        dx_ref[...] = r * (dxhat - xhat * jnp.mean(dxhat * xhat, axis=-1, keepdims=True))

    row = pl.BlockSpec((tm, D), lambda i: (i, 0))
    return pl.pallas_call(
        body,
        name="loss_head",
        grid=(T // tm,),
        out_shape=(
            jax.ShapeDtypeStruct((T, D), F32),
            jax.ShapeDtypeStruct((8, 128), F32),
            jax.ShapeDtypeStruct((1, D), F32),
        ),
        in_specs=[row, row, pl.BlockSpec((1, D), lambda i: (0, 0))],
        out_specs=(row, pl.BlockSpec((8, 128), lambda i: (0, 0)), pl.BlockSpec((1, D), lambda i: (0, 0))),
        compiler_params=_params("arbitrary"),
    )(x, tgt, gfin)


def _adam(name, parts, w, m, v, tm_pref=256):
    P, R, C = parts.shape
    tm = _pick(R, tm_pref, 16)
    c1 = 1.0 - ADAM_B1 ** ADAM_STEP
    c2 = 1.0 - ADAM_B2 ** ADAM_STEP

    def body(p_ref, w_ref, m_ref, v_ref, g_ref, d_ref, mo_ref, vo_ref):
        g = p_ref[0].astype(F32)
        for i in range(1, P):
            g = g + p_ref[i].astype(F32)
        m_new = ADAM_B1 * m_ref[...] + (1.0 - ADAM_B1) * g
        v_new = ADAM_B2 * v_ref[...] + (1.0 - ADAM_B2) * (g * g)
        m_hat = m_new / c1
        v_hat = v_new / c2
        g_ref[...] = g
        d_ref[...] = -ADAM_LR * (m_hat / (jnp.sqrt(v_hat) + ADAM_EPS) + ADAM_WD * w_ref[...])
        mo_ref[...] = m_new
        vo_ref[...] = v_new

    row = pl.BlockSpec((tm, C), lambda i: (i, 0))
    return pl.pallas_call(
        body,
        name=name,
        grid=(R // tm,),
        out_shape=(jax.ShapeDtypeStruct((R, C), F32),) * 4,
        in_specs=[pl.BlockSpec((P, tm, C), lambda i: (0, i, 0)), row, row, row],
        out_specs=(row, row, row, row),
        compiler_params=_params("parallel"),
    )(parts, w, m, v)


BIG = ("w_qkv", "w_o", "w_pw1", "w_pw2", "w_up", "w_down")
SMALL_SHARDED = (("b_pw1", 1), ("w_dw", 2), ("b_dw", 1), ("conv_ln_g", 1), ("conv_ln_b", 1), ("b_pw2", 1))
SMALL_REPL = ("b_mod", "norm_mix", "norm_mlp", "b_qkv", "b_o", "sinks", "final_norm")
WEIGHTS = ("w_mod", "b_mod", "norm_mix", "norm_mlp", "w_qkv", "b_qkv", "w_o", "b_o", "sinks", "w_pw1", "b_pw1",
           "w_dw", "b_dw", "conv_ln_g", "conv_ln_b", "w_pw2", "b_pw2", "w_up", "w_down", "final_norm")


def _step(x, c, loss_target, W, M, V):
    B, S, D = x.shape
    T = B * S
    L = W["w_mod"].shape[0]
    n_mod = W["w_mod"].shape[2]
    me = 4 * lax.axis_index("x") + 2 * lax.axis_index("y") + lax.axis_index("c")
    q_dim = N_HEADS * HEAD_DIM
    kv_dim = N_KV_HEADS * HEAD_DIM

    Wb = {n: W[n].astype(BF16) for n in BIG}

    def shards(i):
        jm = i // 2
        first, last = ("w_qkv", "w_o") if i % 2 == 0 else ("w_pw1", "w_pw2")
        return [Wb[first][jm], Wb[last][jm], Wb["w_up"][i], Wb["w_down"][i]]

    def carried(res, payload):
        return res if payload else (res, [])

    small_names = [n for n, _ in SMALL_SHARDED]
    small_src, small_sizes = _pack([c] + [W[n] for n in small_names], F32, 0)
    got = _exchange("gather_first", [(small_src, True)] + [(s, True) for s in shards(0)])
    small_parts = _unpack(got[0], small_sizes, [c.shape] + [W[n].shape for n in small_names], 1)
    c_all = small_parts[0].reshape(N_DEV * B, D)
    full = {n: _from_slots(p, d) for (n, d), p in zip(SMALL_SHARDED, small_parts[1:])}
    gathered = [got[1:]] + [None] * (L - 1)

    b_mod_mine = lax.dynamic_slice_in_dim(W["b_mod"], me * n_mod, n_mod, axis=1).reshape(L, 1, n_mod)
    mod_part = _mod_fwd(c_all, W["w_mod"].astype(BF16), b_mod_mine)
    mod_slots = mod_part.reshape(L, N_DEV, B, n_mod).transpose(1, 0, 2, 3).reshape(N_DEV, L * B, n_mod)
    mod_recv = _exchange("scatter_mod", [(mod_slots, False)])[0]
    mod = mod_recv.reshape(N_DEV, L, B, n_mod).transpose(1, 2, 0, 3).reshape(L, B, N_MOD, 1, D)

    w_dw3 = [_conv_weight_chunks(full["w_dw"][j], D) for j in range(full["w_dw"].shape[0])]

    xc = x.reshape(T, D)
    saved = []
    h1 = _normmod_fwd("normmod_mix_fwd", xc, W["norm_mix"][0][None], mod[0, :, 1], mod[0, :, 0], S)
    for i in range(L):
        jm = i // 2
        sh1, sc1, g1, sh2, sc2, g2 = (mod[i, :, t] for t in range(N_MOD))
        nxt = [[(s, True)] for s in shards(i + 1)] if i + 1 < L else [[]] * 4
        g_first, g_last, g_up, g_down = gathered[i]
        w_last = g_last.reshape(-1, D)
        w_down_full = g_down.reshape(-1, D)
        sv = {"x_in": xc, "w_last": w_last, "g_up": g_up, "w_down": w_down_full}
        sv["h1"] = h1
        if i % 2 == 0:
            w_qkv_full = _from_slots(g_first, 1)
            sv["w_first"] = w_qkv_full
            qkv, n_first = carried(_mm("qkv_fwd", h1, w_qkv_full, bias=W["b_qkv"][jm], payload=nxt[0]), nxt[0])
            tab = _attn_table(W["sinks"][jm])
            mix_in = _attn_fwd(qkv, tab, B, S)
            sv.update(qkv=qkv, tab=tab)
            b_out = W["b_o"][jm]
        else:
            sv["w_first"] = g_first
            u, n_first = carried(_mm("pw1_fwd", h1, g_first, w_form="nslots", bias=full["b_pw1"][jm], out_dtype=F32,
                                     payload=nxt[0]), nxt[0])
            cv, mix_in = _conv_fwd(u, w_dw3[jm], full["b_dw"][jm][None], full["conv_ln_g"][jm][None],
                                   full["conv_ln_b"][jm][None], S)
            sv.update(u=u, cv=cv)
            b_out = full["b_pw2"][jm]
        sv["mix_in"] = mix_in
        (x1, y1, h2), n_last = carried(_mm("mix_out_fwd", mix_in, w_last, bias=b_out, epi="resid", x=xc, gate=g1,
                                           norm=(W["norm_mlp"][i][None], sc2, sh2), rows_per_batch=S,
                                           payload=nxt[1]), nxt[1])
        sv.update(y1=y1, x1=x1)
        up, n_up = carried(_mm("mlp_up_fwd", h2, g_up, w_form="nslots", payload=nxt[2]), nxt[2])
        norm_next = (W["norm_mix"][i + 1][None], mod[i + 1, :, 1], mod[i + 1, :, 0]) if i + 1 < L else None
        res, n_down = carried(_mm("mlp_down_fwd", up, w_down_full, a_pro="relu2", epi="resid", x=x1, gate=g2,
                                  norm=norm_next, rows_per_batch=S, tm_pref=256, payload=nxt[3]), nxt[3])
        x2, y2 = res[0], res[1]
        h1 = res[2] if i + 1 < L else None
        sv.update(h2=h2, up=up, y2=y2)
        saved.append(sv)
        xc = x2
        if i + 1 < L:
            gathered[i + 1] = [n_first[0], n_last[0], n_up[0], n_down[0]]

    dx, loss_blk, dgfin = _loss_head(xc, loss_target.reshape(T, D), W["final_norm"][None])
    loss = lax.psum(loss_blk[0, 0], ("x", "y", "c"))

    G = {"final_norm": dgfin.reshape(D)}
    dmod_layers = [None] * L
    small_grads = ("norm_mix", "norm_mlp", "b_qkv", "b_o", "sinks", "b_pw1", "w_dw", "b_dw", "conv_ln_g", "conv_ln_b",
                   "b_pw2")
    acc = {n: [None] * W[n].shape[0] for n in small_grads}
    reduced = {n: [None] * W[n].shape[0] for n in BIG}

    dy2, dg2, _ = _gate_bwd("gate_mlp_bwd", dx, saved[L - 1]["y2"], mod[L - 1, :, 5], S)
    for i in reversed(range(L)):
        jm = i // 2
        sv = saved[i]
        sh1, sc1, g1, sh2, sc2, g2 = (mod[i, :, t] for t in range(N_MOD))
        gw_down = _mm_tn("w_down_grad", sv["up"], dy2, a_pro="relu2", out_form="kslots", tt_pref=4096)
        du, r = _mm("mlp_down_bwd", dy2, sv["w_down"], w_form="full_t", epi="relu2d", u=sv["up"], tm_pref=256,
                    payload=[(gw_down, False)])
        reduced["w_down"][i] = r[0]
        gw_up = _mm_tn("w_up_grad", sv["h2"], du, out_form="nslots", tt_pref=4096)
        (dx, dsh2, dsc2, dgn, dy1, dg1, dy1_sum), r = _mm(
            "mlp_up_bwd", du, sv["g_up"], w_form="kslots_t", epi="normbwd", rows_per_batch=S, tm_pref=256,
            nb=dict(x=sv["x1"], dx_in=dx, gnorm=W["norm_mlp"][i][None], sc=sc2, gate=(sv["y1"], g1)),
            payload=[(gw_up, False)])
        reduced["w_up"][i] = r[0]
        acc["norm_mlp"][i] = dgn.reshape(D)
        gw_last = _mm_tn("mix_out_grad", sv["mix_in"], dy1)
        gw_last = gw_last.reshape((N_DEV, -1) + gw_last.shape[1:]).astype(BF16)
        below = (saved[i - 1]["y2"], mod[i - 1, :, 5]) if i > 0 else None
        nb_mix = dict(x=sv["x_in"], dx_in=dx, gnorm=W["norm_mix"][i][None], sc=sc1, gate=below)
        if i % 2 == 0:
            acc["b_o"][jm] = dy1_sum.reshape(D)
            do, r = _mm("attn_out_bwd", dy1, sv["w_last"], w_form="full_t", payload=[(gw_last, False)])
            reduced["w_o"][jm] = r[0]
            dq, dkv, dsk = _attn_bwd(sv["qkv"], sv["mix_in"], do, sv["tab"], B, S)
            acc["sinks"][jm] = jnp.sum(dsk[:, :, :, 0], axis=0).reshape(N_HEADS)
            dqkv = jnp.concatenate([dq, dkv], axis=1)
            acc["b_qkv"][jm] = _colsum("b_qkv_grad", dqkv).reshape(-1)
            gw_qkv = _to_slots(_mm_tn("w_qkv_grad", sv["h1"], dqkv), 1).astype(BF16)
            res, r = _mm("qkv_bwd", dqkv, sv["w_first"], w_form="full_t", epi="normbwd", nb=nb_mix, rows_per_batch=S,
                         payload=[(gw_qkv, False)])
            reduced["w_qkv"][jm] = r[0]
        else:
            acc["b_pw2"][jm] = dy1_sum.reshape(D)
            ds, r = _mm("pw2_bwd", dy1, sv["w_last"], w_form="full_t", out_dtype=F32, payload=[(gw_last, False)])
            reduced["w_pw2"][jm] = r[0]
            dcv, dlg, dlb, dbdw = _lnsilu_bwd(ds, sv["cv"], full["conv_ln_g"][jm][None], full["conv_ln_b"][jm][None])
            acc["conv_ln_g"][jm], acc["conv_ln_b"][jm], acc["b_dw"][jm] = dlg.reshape(-1), dlb.reshape(-1), dbdw.reshape(-1)
            du1, dwdw = _conv_bwd(dcv, sv["u"], w_dw3[jm], S)
            acc["w_dw"][jm] = dwdw.transpose(1, 0, 2).reshape(CONV_HALO, D)[:CONV_WIDTH]
            acc["b_pw1"][jm] = _colsum("b_pw1_grad", du1).reshape(-1)
            gw_pw1 = _mm_tn("w_pw1_grad", sv["h1"], du1, out_form="nslots", tt_pref=4096)
            res, r = _mm("pw1_bwd", du1, sv["w_first"], w_form="kslots_t", epi="normbwd", nb=nb_mix, rows_per_batch=S,
                         payload=[(gw_pw1, False)])
            reduced["w_pw1"][jm] = r[0]
        dx, dsh1, dsc1, dgn = res[:4]
        acc["norm_mix"][i] = dgn.reshape(D)
        dmod_layers[i] = jnp.concatenate([dsh1, dsc1, dg1, dsh2, dsc2, dg2], axis=1).reshape(B, N_MOD * D)
        if i > 0:
            dy2, dg2 = res[4], res[5]
    grad_x = dx.reshape(B, S, D)
    for n, parts in acc.items():
        G[n] = jnp.stack(parts)

    dmod = jnp.stack(dmod_layers)
    dmod_slots = dmod.reshape(L, B, N_DEV, n_mod).transpose(2, 0, 1, 3).reshape(N_DEV, L * B, n_mod)
    dmod_recv = _exchange("gather_dmod", [(dmod_slots, False)])[0]
    dmod_all = dmod_recv.reshape(N_DEV, L, B, n_mod).transpose(1, 0, 2, 3).reshape(L, N_DEV * B, n_mod)
    g_w_mod, db_mod_mine = _mod_bwd(c_all, dmod_all)
    G["b_mod"] = lax.dynamic_update_slice_in_dim(jnp.zeros_like(W["b_mod"]), db_mod_mine.reshape(L, n_mod),
                                                 me * n_mod, axis=1)

    small_items = [jnp.broadcast_to(G[n][None], (N_DEV,) + G[n].shape) for n in SMALL_REPL]
    small_items += [_to_slots(G[n], d) for n, d in SMALL_SHARDED]
    small_slots, small_sizes2 = _pack(small_items, F32, 1)
    small_recv = _exchange("reduce_small", [(small_slots, False)])[0]

    out = {}

    def run_adam(name, parts, names):
        shapes = [W[n].shape for n in names]
        wp, sizes = _pack([W[n] for n in names], F32, 0)
        mp, _ = _pack([M[n] for n in names], F32, 0)
        vp, _ = _pack([V[n] for n in names], F32, 0)
        res = _adam(name, parts, wp, mp, vp)
        for kind, buf in zip(("grad", "delta", "new_m", "new_v"), res):
            for n, a in zip(names, _unpack(buf, sizes, shapes, 0)):
                out[kind + "_" + n] = a

    for n in BIG + ("w_mod",):
        cols = W[n].shape[-1]
        if n == "w_mod":
            parts = g_w_mod.reshape(1, -1, cols)
        else:
            parts = jnp.stack(reduced[n], axis=1).reshape(N_DEV, -1, cols)
        res = _adam("adam_" + n, parts, W[n].reshape(-1, cols), M[n].reshape(-1, cols), V[n].reshape(-1, cols))
        for kind, buf in zip(("grad", "delta", "new_m", "new_v"), res):
            out[kind + "_" + n] = buf.reshape(W[n].shape)
    run_adam("adam_small", small_recv, list(SMALL_REPL) + [n for n, _ in SMALL_SHARDED])

    res = [loss, grad_x]
    for kind in ("grad", "delta", "new_m", "new_v"):
        res += [out[kind + "_" + n] for n in WEIGHTS]
    return tuple(res)


def kernel(x, c, w_mod, b_mod, norm_mix, norm_mlp, w_qkv, b_qkv, w_o, b_o, sinks, w_pw1, b_pw1, w_dw, b_dw, conv_ln_g, conv_ln_b, w_pw2, b_pw2, w_up, w_down, final_norm, loss_target, m_w_mod, m_b_mod, m_norm_mix, m_norm_mlp, m_w_qkv, m_b_qkv, m_w_o, m_b_o, m_sinks, m_w_pw1, m_b_pw1, m_w_dw, m_b_dw, m_conv_ln_g, m_conv_ln_b, m_w_pw2, m_b_pw2, m_w_up, m_w_down, m_final_norm, v_w_mod, v_b_mod, v_norm_mix, v_norm_mlp, v_w_qkv, v_b_qkv, v_w_o, v_b_o, v_sinks, v_w_pw1, v_b_pw1, v_w_dw, v_b_dw, v_conv_ln_g, v_conv_ln_b, v_w_pw2, v_b_pw2, v_w_up, v_w_down, v_final_norm):
    W = dict(w_mod=w_mod, b_mod=b_mod, norm_mix=norm_mix, norm_mlp=norm_mlp, w_qkv=w_qkv, b_qkv=b_qkv, w_o=w_o,
             b_o=b_o, sinks=sinks, w_pw1=w_pw1, b_pw1=b_pw1, w_dw=w_dw, b_dw=b_dw, conv_ln_g=conv_ln_g,
             conv_ln_b=conv_ln_b, w_pw2=w_pw2, b_pw2=b_pw2, w_up=w_up, w_down=w_down, final_norm=final_norm)
    M = dict(w_mod=m_w_mod, b_mod=m_b_mod, norm_mix=m_norm_mix, norm_mlp=m_norm_mlp, w_qkv=m_w_qkv, b_qkv=m_b_qkv,
             w_o=m_w_o, b_o=m_b_o, sinks=m_sinks, w_pw1=m_w_pw1, b_pw1=m_b_pw1, w_dw=m_w_dw, b_dw=m_b_dw,
             conv_ln_g=m_conv_ln_g, conv_ln_b=m_conv_ln_b, w_pw2=m_w_pw2, b_pw2=m_b_pw2, w_up=m_w_up,
             w_down=m_w_down, final_norm=m_final_norm)
    V = dict(w_mod=v_w_mod, b_mod=v_b_mod, norm_mix=v_norm_mix, norm_mlp=v_norm_mlp, w_qkv=v_w_qkv, b_qkv=v_b_qkv,
             w_o=v_w_o, b_o=v_b_o, sinks=v_sinks, w_pw1=v_w_pw1, b_pw1=v_b_pw1, w_dw=v_w_dw, b_dw=v_b_dw,
             conv_ln_g=v_conv_ln_g, conv_ln_b=v_conv_ln_b, w_pw2=v_w_pw2, b_pw2=v_b_pw2, w_up=v_w_up,
             w_down=v_w_down, final_norm=v_final_norm)
    return _step(x, c, loss_target, W, M, V)
```

```python
import functools

import numpy as np
import jax
import jax.numpy as jnp
from jax import lax
from jax.experimental import pallas as pl
from jax.experimental.pallas import tpu as pltpu

F32 = jnp.float32
BF16 = jnp.bfloat16

N_DEV = 8
N_HEADS = 16
N_KV_HEADS = 2
HEAD_DIM = 64
GROUP = N_HEADS // N_KV_HEADS
ATT_BLOCK = 128
CONV_WIDTH = 31
CONV_HALO = 32
N_MOD = 6
EPS = 1e-6
ADAM_LR = 0.001
ADAM_B1 = 0.9
ADAM_B2 = 0.999
ADAM_EPS = 1e-08
ADAM_WD = 0.01
ADAM_STEP = 10
NEG_BIG = -1e30
PACK_COLS = 1024
VMEM_LIMIT_BYTES = 56 * 1024 * 1024
MESH_ID = pl.DeviceIdType.MESH


def _params(*sem):
    return pltpu.CompilerParams(dimension_semantics=sem, vmem_limit_bytes=VMEM_LIMIT_BYTES)


def _pick(n, pref, mult=8):
    if n <= pref:
        return n
    for t in range(pref, 0, -1):
        if n % t == 0 and t % mult == 0:
            return t
    return n


def _sigmoid(z):
    return 0.5 * jnp.tanh(0.5 * z) + 0.5


def _payload_layout(payload):
    n = len(payload)
    out_shapes = [jax.ShapeDtypeStruct((N_DEV,) + tuple(a.shape if bc else a.shape[1:]), a.dtype) for a, bc in payload]
    hbm = pl.BlockSpec(memory_space=pl.ANY)
    scratch = [pltpu.SemaphoreType.DMA((n * (N_DEV - 1),)), pltpu.SemaphoreType.DMA((n * (N_DEV - 1),)),
               pltpu.SemaphoreType.DMA((n,))]
    return [a for a, _ in payload], [hbm] * n, out_shapes, [hbm] * n, scratch


class _Plan:
    def __init__(self, bcasts, src_refs, dst_refs, send_sems, recv_sems, local_sems):
        x, y, c = lax.axis_index("x"), lax.axis_index("y"), lax.axis_index("c")
        me = 4 * x + 2 * y + c
        self.first, self.landed, self.relay, self.local = [], [], [], []
        for t, (bc, s_ref, d_ref) in enumerate(zip(bcasts, src_refs, dst_refs)):
            def remote(k, src, slot, to):
                sem = t * (N_DEV - 1) + k
                return pltpu.make_async_remote_copy(src_ref=src, dst_ref=d_ref.at[slot], send_sem=send_sems.at[sem],
                                                    recv_sem=recv_sems.at[sem], device_id=to, device_id_type=MESH_ID)
            if bc:
                chips = [(1 - x, y), (x, 1 - y), (1 - x, 1 - y)]
                self.first.append(remote(0, s_ref, me, (x, y, 1 - c)))
                for j, (px, py) in enumerate(chips):
                    cp = remote(1 + j, s_ref, me, (px, py, c))
                    self.first.append(cp)
                    self.landed.append(cp)
                    theirs = 4 * px + 2 * py + c
                    self.relay.append(remote(4 + j, d_ref.at[theirs], theirs, (x, y, 1 - c)))
                self.local.append(pltpu.make_async_copy(s_ref, d_ref.at[me], local_sems.at[t]))
            else:
                for k in range(1, N_DEV):
                    px = 1 - x if (k >> 2) & 1 else x
                    py = 1 - y if (k >> 1) & 1 else y
                    pc = 1 - c if k & 1 else c
                    self.first.append(remote(k - 1, s_ref.at[4 * px + 2 * py + pc], me, (px, py, pc)))
                self.local.append(pltpu.make_async_copy(s_ref.at[me], d_ref.at[me], local_sems.at[t]))

    def start(self):
        for cp in self.first + self.local:
            cp.start()

    def pass_on(self):
        for cp in self.landed:
            cp.wait_recv()
        for cp in self.relay:
            cp.start()

    def finish(self):
        for cp in self.first:
            cp.wait_send()
            if not any(cp is l for l in self.landed):
                cp.wait_recv()
        for cp in self.relay:
            cp.wait()
        for cp in self.local:
            cp.wait()


def _exchange(name, payload):
    n = len(payload)
    bcasts = [bc for _, bc in payload]
    arrays, in_specs, out_shapes, out_specs, scratch = _payload_layout(payload)

    def body(*refs):
        plan = _Plan(bcasts, refs[:n], refs[n:2 * n], *refs[2 * n:])
        plan.start()
        plan.pass_on()
        plan.finish()

    return pl.pallas_call(
        body,
        name=name,
        out_shape=tuple(out_shapes),
        in_specs=in_specs,
        out_specs=tuple(out_specs),
        scratch_shapes=scratch,
    )(*arrays)


def _pack(arrays, dtype, lead):
    lead_shape = arrays[0].shape[:lead]
    flat = [a.astype(dtype).reshape(lead_shape + (-1,)) for a in arrays]
    sizes = [f.shape[-1] for f in flat]
    total = sum(sizes)
    chunk = 16 * PACK_COLS
    padded = -(-total // chunk) * chunk
    if padded > total:
        flat.append(jnp.zeros(lead_shape + (padded - total,), dtype))
    buf = jnp.concatenate(flat, axis=-1)
    return buf.reshape(lead_shape + (padded // PACK_COLS, PACK_COLS)), sizes


def _unpack(buf, sizes, shapes, lead):
    lead_shape = buf.shape[:lead]
    flat = buf.reshape(lead_shape + (-1,))
    out, off = [], 0
    for n, shp in zip(sizes, shapes):
        out.append(lax.slice_in_dim(flat, off, off + n, axis=lead).reshape(lead_shape + tuple(shp)))
        off += n
    return out


def _to_slots(a, dim):
    shp = a.shape
    a = a.reshape(shp[:dim] + (N_DEV, shp[dim] // N_DEV) + shp[dim + 1:])
    return jnp.moveaxis(a, dim, 0)


def _from_slots(a, dim):
    a = jnp.moveaxis(a, 0, dim)
    shp = a.shape
    return a.reshape(shp[:dim] + (shp[dim] * shp[dim + 1],) + shp[dim + 2:])


def _mod_fwd(c_all, w, b):
    L, D, n = w.shape
    M = c_all.shape[0]

    def body(c_ref, w_ref, b_ref, o_ref):
        cv = c_ref[...]
        cs = (cv * _sigmoid(cv)).astype(BF16)
        o_ref[...] = jnp.dot(cs, w_ref[...], preferred_element_type=F32) + b_ref[...]

    return pl.pallas_call(
        body,
        name="mod_fwd",
        grid=(L,),
        out_shape=jax.ShapeDtypeStruct((L, M, n), F32),
        in_specs=[
            pl.BlockSpec((M, D), lambda l: (0, 0)),
            pl.BlockSpec((None, D, n), lambda l: (l, 0, 0)),
            pl.BlockSpec((None, 1, n), lambda l: (l, 0, 0)),
        ],
        out_specs=pl.BlockSpec((None, M, n), lambda l: (l, 0, 0)),
        compiler_params=_params("arbitrary"),
    )(c_all, w, b)


def _mod_bwd(c_all, dmod_all):
    L, M, n = dmod_all.shape
    D = c_all.shape[1]

    def body(c_ref, d_ref, dw_ref, db_ref):
        cv = c_ref[...]
        cs = (cv * _sigmoid(cv)).astype(BF16)
        d = d_ref[...]
        dw_ref[...] = lax.dot_general(cs, d.astype(BF16), (((0,), (0,)), ((), ())), preferred_element_type=F32)
        db_ref[...] = jnp.sum(d, axis=0, keepdims=True)

    return pl.pallas_call(
        body,
        name="mod_bwd",
        grid=(L,),
        out_shape=(jax.ShapeDtypeStruct((L, D, n), F32), jax.ShapeDtypeStruct((L, 1, n), F32)),
        in_specs=[
            pl.BlockSpec((M, D), lambda l: (0, 0)),
            pl.BlockSpec((None, M, n), lambda l: (l, 0, 0)),
        ],
        out_specs=(
            pl.BlockSpec((None, D, n), lambda l: (l, 0, 0)),
            pl.BlockSpec((None, 1, n), lambda l: (l, 0, 0)),
        ),
        compiler_params=_params("arbitrary"),
    )(c_all, dmod_all)


def _mm(name, a, w, *, w_form="full", out_dtype=BF16, bias=None, a_sq=False, epi=None, x=None, gate=None, u=None,
        norm=None, nb=None, rows_per_batch=None, tm_pref=512, payload=()):
    M, K = a.shape
    if w_form == "full":
        N = w.shape[1]
        nc = _pick(N, 1024, 128)
    elif w_form == "full_t":
        N = w.shape[0]
        nc = _pick(N, 1024, 128)
    elif w_form == "nslots":
        nc = w.shape[2]
        N = N_DEV * nc
    else:
        N = w.shape[1]
        nc = N
    ks = K // N_DEV
    n_chunks = N // nc
    tm = _pick(M if rows_per_batch is None else rows_per_batch, tm_pref, 16)
    steps = M // tm
    relay_step = (3 * steps) // 4
    nb_gate = nb is not None and nb.get("gate") is not None
    tpb_nb = rows_per_batch // tm if epi == "normbwd" else 1
    has_bias = bias is not None
    n_pay = len(payload)
    bcasts = [bc for _, bc in payload]
    nt = (((1,), (1,)), ((), ()))


    def body(*refs):
        it = iter(refs)
        a_ref = next(it)
        w_ref = next(it)
        b_ref = next(it) if has_bias else None
        x_ref = next(it) if epi == "resid" else None
        g_ref = next(it) if epi == "resid" else None
        gn_ref, sc_ref, sh_ref = (next(it), next(it), next(it)) if norm is not None else (None, None, None)
        u_ref = next(it) if epi == "relu2d" else None
        if epi == "normbwd":
            nx_ref, ndx_ref, ngn_ref, nsc_ref = next(it), next(it), next(it), next(it)
            ny_ref, ngt_ref = (next(it), next(it)) if nb_gate else (None, None)
        pay_src = [next(it) for _ in range(n_pay)]
        o_ref = next(it)
        y_ref = next(it) if epi == "resid" else None
        h_ref = next(it) if norm is not None else None
        if epi == "normbwd":
            dsh_ref, dsc_ref, dgn_ref = next(it), next(it), next(it)
            dy_ref, dgate_ref, cs_ref = (next(it), next(it), next(it)) if nb_gate else (None, None, None)
        pay_dst = [next(it) for _ in range(n_pay)]
        sems = list(it)

        if epi == "normbwd":
            @pl.when(pl.program_id(0) % tpb_nb == 0)
            def _():
                dsh_ref[...] = jnp.zeros_like(dsh_ref)
                dsc_ref[...] = jnp.zeros_like(dsc_ref)
                if nb_gate:
                    dgate_ref[...] = jnp.zeros_like(dgate_ref)

            @pl.when(pl.program_id(0) == 0)
            def _():
                dgn_ref[...] = jnp.zeros_like(dgn_ref)
                if nb_gate:
                    cs_ref[...] = jnp.zeros_like(cs_ref)

        if n_pay:
            @pl.when(pl.program_id(0) == 0)
            def _():
                _Plan(bcasts, pay_src, pay_dst, *sems).start()

            @pl.when(pl.program_id(0) == relay_step)
            def _():
                _Plan(bcasts, pay_src, pay_dst, *sems).pass_on()

        for rows in (slice(0, tm),):
            av = None if w_form == "kslots_t" else a_ref[rows, :]
            if a_sq:
                av = av * av
            for ci in range(n_chunks):
                cols = slice(ci * nc, (ci + 1) * nc)
                if w_form == "full":
                    acc = jnp.dot(av, w_ref[:, cols], preferred_element_type=F32)
                elif w_form == "full_t":
                    acc = lax.dot_general(av, w_ref[cols, :], nt, preferred_element_type=F32)
                elif w_form == "nslots":
                    acc = jnp.dot(av, w_ref[ci], preferred_element_type=F32)
                else:
                    acc = lax.dot_general(a_ref[rows, 0:ks], w_ref[0], nt, preferred_element_type=F32)
                    for j in range(1, N_DEV):
                        acc = acc + lax.dot_general(a_ref[rows, j * ks:(j + 1) * ks], w_ref[j], nt,
                                                    preferred_element_type=F32)
                if has_bias:
                    acc = acc + b_ref[:, cols]
                if epi == "resid":
                    xn = x_ref[rows, cols] + g_ref[:, cols] * acc
                    o_ref[rows, cols] = xn
                    y_ref[rows, cols] = acc.astype(BF16)
                    if norm is not None:
                        r = lax.rsqrt(jnp.mean(xn * xn, axis=-1, keepdims=True) + EPS)
                        h_ref[rows, :] = ((xn * r * gn_ref[...]) * (1.0 + sc_ref[...]) + sh_ref[...]).astype(BF16)
                elif epi == "relu":
                    o_ref[rows, cols] = jnp.maximum(acc, 0.0).astype(out_dtype)
                elif epi == "relu2d":
                    o_ref[rows, cols] = (acc * (2.0 * u_ref[rows, cols].astype(F32))).astype(out_dtype)
                elif epi == "normbwd":
                    xv = nx_ref[rows, :]
                    gn = ngn_ref[...]
                    r = lax.rsqrt(jnp.mean(xv * xv, axis=-1, keepdims=True) + EPS)
                    xhat = xv * r
                    dsh_ref[...] += jnp.sum(acc, axis=0, keepdims=True)
                    dsc_ref[...] += jnp.sum(acc * (xhat * gn), axis=0, keepdims=True)
                    dn = acc * (1.0 + nsc_ref[...])
                    dgn_ref[...] += jnp.sum(dn * xhat, axis=0, keepdims=True)
                    dxhat = dn * gn
                    dx = ndx_ref[rows, :] + r * (dxhat - xhat * jnp.mean(dxhat * xhat, axis=-1, keepdims=True))
                    o_ref[rows, :] = dx
                    if nb_gate:
                        dy = dx * ngt_ref[...]
                        dy_ref[rows, :] = dy.astype(BF16)
                        dgate_ref[...] += jnp.sum(dx * ny_ref[rows, :].astype(F32), axis=0, keepdims=True)
                        cs_ref[...] += jnp.sum(dy, axis=0, keepdims=True)
                else:
                    o_ref[rows, cols] = acc.astype(out_dtype)

        if n_pay:
            @pl.when(pl.program_id(0) == steps - 1)
            def _():
                _Plan(bcasts, pay_src, pay_dst, *sems).finish()

    args = [a, w]
    w_block = w.shape
    specs = [pl.BlockSpec((tm, K), lambda i: (i, 0)), pl.BlockSpec(w_block, lambda i: (0,) * len(w_block))]
    if has_bias:
        args.append(bias.reshape(1, N).astype(F32))
        specs.append(pl.BlockSpec((1, N), lambda i: (0, 0)))
    row_spec = pl.BlockSpec((tm, N), lambda i: (i, 0))
    if epi == "resid":
        tpb = rows_per_batch // tm
        vec_spec = pl.BlockSpec((None, 1, N), lambda i: (i // tpb, 0, 0))
        args += [x, gate]
        specs += [row_spec, vec_spec]
        out_shape = [jax.ShapeDtypeStruct((M, N), F32), jax.ShapeDtypeStruct((M, N), BF16)]
        out_specs = [row_spec, row_spec]
        if norm is not None:
            assert n_chunks == 1
            args += list(norm)
            specs += [pl.BlockSpec((1, N), lambda i: (0, 0)), vec_spec, vec_spec]
            out_shape.append(jax.ShapeDtypeStruct((M, N), BF16))
            out_specs.append(row_spec)
    elif epi == "normbwd":
        assert n_chunks == 1
        vec_spec = pl.BlockSpec((None, 1, N), lambda i: (i // tpb_nb, 0, 0))
        one_spec = pl.BlockSpec((1, N), lambda i: (0, 0))
        nbat = M // rows_per_batch
        f_vec, f_one = jax.ShapeDtypeStruct((nbat, 1, N), F32), jax.ShapeDtypeStruct((1, N), F32)
        args += [nb["x"], nb["dx_in"], nb["gnorm"], nb["sc"]]
        specs += [row_spec, row_spec, one_spec, vec_spec]
        out_shape = [jax.ShapeDtypeStruct((M, N), F32), f_vec, f_vec, f_one]
        out_specs = [row_spec, vec_spec, vec_spec, one_spec]
        if nb_gate:
            args += list(nb["gate"])
            specs += [row_spec, vec_spec]
            out_shape += [jax.ShapeDtypeStruct((M, N), BF16), f_vec, f_one]
            out_specs += [row_spec, vec_spec, one_spec]
    else:
        if epi == "relu2d":
            args.append(u)
            specs.append(row_spec)
        out_shape = [jax.ShapeDtypeStruct((M, N), out_dtype)]
        out_specs = [row_spec]
    scratch = []
    if n_pay:
        p_arrays, p_in, p_shapes, p_out, scratch = _payload_layout(payload)
        args += p_arrays
        specs += p_in
        out_shape += p_shapes
        out_specs += p_out
    res = pl.pallas_call(
        body,
        name=name,
        grid=(steps,),
        out_shape=tuple(out_shape),
        in_specs=specs,
        out_specs=tuple(out_specs),
        scratch_shapes=scratch,
        compiler_params=_params("arbitrary" if (n_pay or epi == "normbwd") else "parallel"),
    )(*args)
    if epi == "resid":
        n_own = 3 if norm is not None else 2
    elif epi == "normbwd":
        n_own = 7 if nb_gate else 4
    else:
        n_own = 1
    own = res[0] if n_own == 1 else tuple(res[:n_own])
    return (own, list(res[n_own:])) if n_pay else own


def _mm_tn(name, a, b, *, a_sq=False, out_form="full", colsum=False, tt_pref=1024, tk_pref=1024, tn_pref=1024):
    T, K = a.shape
    N = b.shape[1]
    tt = _pick(T, tt_pref, 16)
    tk = K // N_DEV if out_form == "kslots" else _pick(K, tk_pref, 128)
    tn = N // N_DEV if out_form == "nslots" else _pick(N, tn_pref, 128)
    nt_steps = T // tt
    slots = out_form != "full"
    assert not colsum or K == tk

    def body(*refs):
        it = iter(refs)
        a_ref, b_ref, o_ref = next(it), next(it), next(it)
        cs_ref = next(it) if colsum else None
        acc = next(it) if slots else o_ref

        @pl.when(pl.program_id(2) == 0)
        def _():
            acc[...] = jnp.zeros_like(acc)
            if colsum:
                cs_ref[...] = jnp.zeros_like(cs_ref)

        bv = b_ref[...]
        av = a_ref[...]
        if a_sq:
            av = av * av
        acc[...] += lax.dot_general(av, bv, (((0,), (0,)), ((), ())), preferred_element_type=F32)
        if colsum:
            cs_ref[...] += jnp.sum(bv.astype(F32), axis=0, keepdims=True)

        if slots:
            @pl.when(pl.program_id(2) == nt_steps - 1)
            def _():
                o_ref[...] = acc[...].astype(o_ref.dtype)

    if out_form == "full":
        out_shape = [jax.ShapeDtypeStruct((K, N), F32)]
        out_specs = [pl.BlockSpec((tk, tn), lambda k, n, t: (k, n))]
    elif out_form == "nslots":
        out_shape = [jax.ShapeDtypeStruct((N_DEV, K, tn), BF16)]
        out_specs = [pl.BlockSpec((None, tk, tn), lambda k, n, t: (n, k, 0))]
    else:
        out_shape = [jax.ShapeDtypeStruct((N_DEV, tk, N), BF16)]
        out_specs = [pl.BlockSpec((None, tk, tn), lambda k, n, t: (k, 0, n))]
    if colsum:
        out_shape.append(jax.ShapeDtypeStruct((1, N), F32))
        out_specs.append(pl.BlockSpec((1, tn), lambda k, n, t: (0, n)))
    res = pl.pallas_call(
        body,
        name=name,
        grid=(K // tk, N // tn, nt_steps),
        out_shape=tuple(out_shape),
        in_specs=[
            pl.BlockSpec((tt, tk), lambda k, n, t: (t, k)),
            pl.BlockSpec((tt, tn), lambda k, n, t: (t, n)),
        ],
        out_specs=tuple(out_specs),
        scratch_shapes=[pltpu.VMEM((tk, tn), F32)] if slots else [],
        compiler_params=_params("parallel", "parallel", "arbitrary"),
    )(a, b)
    return res if colsum else res[0]


def _normmod_fwd(name, x, gnorm, sc, sh, S, tm_pref=512):
    T, D = x.shape
    tm = _pick(S, tm_pref, 16)
    tpb = S // tm

    def body(x_ref, g_ref, sc_ref, sh_ref, o_ref):
        xv = x_ref[...]
        r = lax.rsqrt(jnp.mean(xv * xv, axis=-1, keepdims=True) + EPS)
        n = xv * r * g_ref[...]
        o_ref[...] = (n * (1.0 + sc_ref[...]) + sh_ref[...]).astype(BF16)

    row = pl.BlockSpec((tm, D), lambda i: (i, 0))
    vec = pl.BlockSpec((None, 1, D), lambda i: (i // tpb, 0, 0))
    return pl.pallas_call(
        body,
        name=name,
        grid=(T // tm,),
        out_shape=jax.ShapeDtypeStruct((T, D), BF16),
        in_specs=[row, pl.BlockSpec((1, D), lambda i: (0, 0)), vec, vec],
        out_specs=row,
        compiler_params=_params("parallel"),
    )(x, gnorm, sc, sh)


ATT_ROWS = GROUP * ATT_BLOCK
ATT_SPAN = 2 * ATT_BLOCK
ATT_SCALE = HEAD_DIM ** -0.5


def _attn_table(sinks):
    slopes = jnp.asarray(np.array([2.0 ** (-8.0 * (h + 1) / N_HEADS) for h in range(N_HEADS)], np.float32))
    r = jnp.arange(ATT_BLOCK)[:, None]
    cc = jnp.arange(ATT_SPAN)[None, :]
    dist = r + ATT_BLOCK - cc
    ok = (dist >= 0) & (dist < ATT_BLOCK)
    tab = jnp.where(ok[None], -slopes[:, None, None] * dist.astype(F32)[None], NEG_BIG)
    tab = jnp.where((cc == 0)[None], sinks.astype(F32)[:, None, None], tab)
    return tab.reshape(N_KV_HEADS, ATT_ROWS, ATT_SPAN)


KV_COLS = N_KV_HEADS * HEAD_DIM


PAIR = 2 * HEAD_DIM


def _pair_tables(tab):
    hkv, rows, span = tab.shape
    pairs = rows // (2 * ATT_BLOCK)
    return tab.reshape(hkv, pairs, 2, ATT_BLOCK, span).transpose(0, 2, 1, 3, 4).reshape(hkv, 2, pairs * ATT_BLOCK, span)


def _stack_pairs(ref, pairs):
    return jnp.concatenate([ref[:, j * PAIR:(j + 1) * PAIR] for j in range(pairs)], axis=0)


def _unstack_pairs(v, pairs):
    return jnp.concatenate([v[j * ATT_BLOCK:(j + 1) * ATT_BLOCK, :] for j in range(pairs)], axis=1)


def _swap_halves(x):
    return jnp.concatenate([x[:, HEAD_DIM:], x[:, :HEAD_DIM]], axis=1)


def _load_span(prev_ref, cur_ref, buf_e, buf_o, kv, mult):
    lane = lax.broadcasted_iota(jnp.int32, (ATT_BLOCK, PAIR), 1)
    row = lax.broadcasted_iota(jnp.int32, (ATT_BLOCK, PAIR), 0)
    own = lane // HEAD_DIM == kv
    for r0, ref, first in ((0, prev_ref, True), (ATT_BLOCK, cur_ref, False)):
        v = ref[...]
        if mult != 1.0:
            v = v * mult
        zero = jnp.zeros_like(v)
        mine = jnp.where(own, v, zero)
        if first:
            mine = jnp.where(row > 0, mine, zero)
        other = _swap_halves(mine)
        buf_e[r0:r0 + ATT_BLOCK, :] = jnp.where(kv == 0, mine, other)
        buf_o[r0:r0 + ATT_BLOCK, :] = jnp.where(kv == 0, other, mine)


def _first_block_penalty(i):
    col = lax.broadcasted_iota(jnp.int32, (1, ATT_SPAN), 1)
    return jnp.where((col < ATT_BLOCK) & (col > 0), jnp.where(i > 0, 0.0, NEG_BIG), 0.0).astype(F32)


def _attn_probs(qp, kbuf, bias, first_pen):
    nt = (((1,), (1,)), ((), ()))
    s = lax.dot_general(qp, kbuf, nt, preferred_element_type=F32) + bias + first_pen
    m = jnp.max(s, axis=-1, keepdims=True)
    e = jnp.exp(s - m)
    return e * (1.0 / jnp.sum(e, axis=-1, keepdims=True))


def _attn_specs(nb, pairs, q_map, row_cur, row_prev):
    gw = pairs * PAIR
    kblk = N_KV_HEADS * gw // KV_COLS
    qspec = pl.BlockSpec((ATT_BLOCK, gw), q_map)
    kv_specs = [pl.BlockSpec((ATT_BLOCK, KV_COLS), (lambda b, i, kv, r=r, c=c: (r(b, i), c)))
                for c in (kblk, kblk + 1) for r in (row_cur, row_prev)]
    tspec = pl.BlockSpec((N_KV_HEADS, 2, pairs * ATT_BLOCK, ATT_SPAN), lambda b, i, kv: (0, 0, 0, 0))
    return qspec, kv_specs, tspec


def _attn_fwd(qkv, tab2, B, S):
    assert N_KV_HEADS == 2 and GROUP % 2 == 0
    T = qkv.shape[0]
    pairs = GROUP // 2
    nb = S // ATT_BLOCK
    q_dim = N_HEADS * HEAD_DIM

    def body(q_ref, kc_ref, kp_ref, vc_ref, vp_ref, tab_ref, o_ref, ke, ko, ve, vo):
        i, kv = pl.program_id(1), pl.program_id(2)
        _load_span(kp_ref, kc_ref, ke, ko, kv, ATT_SCALE)
        _load_span(vp_ref, vc_ref, ve, vo, kv, 1.0)
        pen = _first_block_penalty(i)
        qp = _stack_pairs(q_ref, pairs)
        p_e = _attn_probs(qp, ke[...], tab_ref[kv, 0], pen)
        p_o = _attn_probs(qp, ko[...], tab_ref[kv, 1], pen)
        o = jnp.dot(p_e.astype(BF16), ve[...], preferred_element_type=F32)
        o = o + jnp.dot(p_o.astype(BF16), vo[...], preferred_element_type=F32)
        o_ref[...] = _unstack_pairs(o.astype(BF16), pairs)

    qspec, kv_specs, tspec = _attn_specs(nb, pairs, lambda b, i, kv: (b * nb + i, kv), lambda b, i: b * nb + i,
                                         lambda b, i: b * nb + jnp.maximum(i - 1, 0))
    return pl.pallas_call(
        body,
        name="attn_fwd",
        grid=(B, nb, N_KV_HEADS),
        out_shape=jax.ShapeDtypeStruct((T, q_dim), BF16),
        in_specs=[qspec] + kv_specs + [tspec],
        out_specs=qspec,
        scratch_shapes=[pltpu.VMEM((ATT_SPAN, PAIR), BF16)] * 4,
        compiler_params=_params("parallel", "parallel", "arbitrary"),
    )(qkv, qkv, qkv, qkv, qkv, tab2)


def _attn_bwd(qkv, o, do, tab2, B, S):
    T = qkv.shape[0]
    hd = HEAD_DIM
    pairs = GROUP // 2
    rows = pairs * ATT_BLOCK
    nb = S // ATT_BLOCK
    last = nb - 1
    q_dim = N_HEADS * hd
    tn = (((0,), (0,)), ((), ()))
    nt = (((1,), (1,)), ((), ()))

    def body(q_ref, kc_ref, kp_ref, vc_ref, vp_ref, tab_ref, o_ref, do_ref,
             dq_ref, dkv_ref, dsink_ref, carry_k, carry_v, ke, ko, ve, vo):
        i, kv = pl.program_id(1), pl.program_id(2)

        @pl.when((i == 0) & (kv == 0))
        def _():
            dsink_ref[...] = jnp.zeros_like(dsink_ref)

        def emit(dk_rows, dv_rows):
            for h in range(N_KV_HEADS):
                @pl.when(kv == h)
                def _():
                    dkv_ref[:, h * hd:(h + 1) * hd] = dk_rows.astype(BF16)
                    dkv_ref[:, KV_COLS + h * hd:KV_COLS + (h + 1) * hd] = dv_rows.astype(BF16)

        @pl.when(i < nb)
        def _():
            _load_span(kp_ref, kc_ref, ke, ko, kv, ATT_SCALE)
            _load_span(vp_ref, vc_ref, ve, vo, kv, 1.0)
            pen = _first_block_penalty(i)
            qp = _stack_pairs(q_ref, pairs)
            dop = _stack_pairs(do_ref, pairs)
            op = _stack_pairs(o_ref, pairs)
            p_e = _attn_probs(qp, ke[...], tab_ref[kv, 0], pen)
            p_o = _attn_probs(qp, ko[...], tab_ref[kv, 1], pen)
            prod = dop.astype(F32) * op.astype(F32)
            low = lax.broadcasted_iota(jnp.int32, (rows, PAIR), 1) < hd
            d_e = jnp.sum(jnp.where(low, prod, 0.0), axis=-1, keepdims=True)
            d_o = jnp.sum(prod, axis=-1, keepdims=True) - d_e
            ds_e = (p_e * (lax.dot_general(dop, ve[...], nt, preferred_element_type=F32) - d_e)).astype(BF16)
            ds_o = (p_o * (lax.dot_general(dop, vo[...], nt, preferred_element_type=F32) - d_o)).astype(BF16)
            dq = jnp.dot(ds_e, ke[...], preferred_element_type=F32) + jnp.dot(ds_o, ko[...], preferred_element_type=F32)
            dq_ref[...] = _unstack_pairs(dq.astype(BF16), pairs)
            hg = lax.broadcasted_iota(jnp.int32, (pairs, rows), 0)
            hr = lax.broadcasted_iota(jnp.int32, (pairs, rows), 1)
            head_of = jnp.where(hr // ATT_BLOCK == hg, 1.0, 0.0).astype(BF16)
            dsink_ref[kv, 0] += jnp.dot(head_of, ds_e, preferred_element_type=F32)[:, 0:128]
            dsink_ref[kv, 1] += jnp.dot(head_of, ds_o, preferred_element_type=F32)[:, 0:128]
            low_k = lax.broadcasted_iota(jnp.int32, (ATT_SPAN, PAIR), 1) < hd
            dk2 = jnp.where(low_k, lax.dot_general(ds_e, qp, tn, preferred_element_type=F32),
                            lax.dot_general(ds_o, qp, tn, preferred_element_type=F32))
            dv2 = jnp.where(low_k, lax.dot_general(p_e.astype(BF16), dop, tn, preferred_element_type=F32),
                            lax.dot_general(p_o.astype(BF16), dop, tn, preferred_element_type=F32))
            keep = lax.broadcasted_iota(jnp.int32, (ATT_SPAN, hd), 0) > 0
            dk = jnp.where(keep, (dk2[:, :hd] + dk2[:, hd:]) * ATT_SCALE, 0.0)
            dv = jnp.where(keep, dv2[:, :hd] + dv2[:, hd:], 0.0)

            @pl.when(i > 0)
            def _():
                emit(carry_k[kv] + dk[0:ATT_BLOCK], carry_v[kv] + dv[0:ATT_BLOCK])

            carry_k[kv] = dk[ATT_BLOCK:]
            carry_v[kv] = dv[ATT_BLOCK:]

        @pl.when(i == nb)
        def _():
            emit(carry_k[kv], carry_v[kv])

    def q_map(b, i, kv):
        return (b * nb + jnp.minimum(i, last), jnp.where(i == nb, N_KV_HEADS - 1, kv))

    qspec, kv_specs, tspec = _attn_specs(nb, pairs, q_map, lambda b, i: b * nb + jnp.minimum(i, last),
                                         lambda b, i: b * nb + jnp.clip(i - 1, 0, last))
    dkv = pl.BlockSpec((ATT_BLOCK, 2 * KV_COLS), lambda b, i, kv: (b * nb + jnp.maximum(i - 1, 0), 0))
    dsk = pl.BlockSpec((None, N_KV_HEADS, 2, pairs, 128), lambda b, i, kv: (b, 0, 0, 0, 0))
    return pl.pallas_call(
        body,
        name="attn_bwd",
        grid=(B, nb + 1, N_KV_HEADS),
        out_shape=(
            jax.ShapeDtypeStruct((T, q_dim), BF16),
            jax.ShapeDtypeStruct((T, 2 * KV_COLS), BF16),
            jax.ShapeDtypeStruct((B, N_KV_HEADS, 2, pairs, 128), F32),
        ),
        in_specs=[qspec] + kv_specs + [tspec, qspec, qspec],
        out_specs=(qspec, dkv, dsk),
        scratch_shapes=[pltpu.VMEM((N_KV_HEADS, ATT_BLOCK, hd), F32), pltpu.VMEM((N_KV_HEADS, ATT_BLOCK, hd), F32)]
        + [pltpu.VMEM((ATT_SPAN, PAIR), BF16)] * 4,
        compiler_params=_params("arbitrary", "arbitrary", "arbitrary"),
    )(qkv, qkv, qkv, qkv, qkv, tab2, o, do)


def _conv_tiles(S):
    ts = _pick(S, 256, CONV_HALO)
    return ts, S // ts


def _conv_chunks(C, ts):
    lane = _pick(C, 128, 128)
    return lane, C // lane, _pick(ts, 64, 8)


def _conv_weight_chunks(w_dw, C):
    lane = _pick(C, 128, 128)
    w = jnp.pad(w_dw, ((0, CONV_HALO - CONV_WIDTH), (0, 0)))
    return w.reshape(CONV_HALO, C // lane, lane).transpose(1, 0, 2)


def _conv_fwd(u, w3, b_dw, ln_g, ln_b, S):
    T, C2 = u.shape
    C = C2 // 2
    B = T // S
    ts, nj = _conv_tiles(S)
    hb = ts // CONV_HALO
    lane, nc, rc = _conv_chunks(C, ts)

    def body(a_ref, g_ref, ap_ref, gp_ref, w_ref, bdw_ref, lg_ref, lb_ref, cv_ref, s_ref, buf, cvb):
        j = pl.program_id(1)
        glu_prev = ap_ref[...] * _sigmoid(gp_ref[...]) * (j > 0).astype(F32)
        glu = a_ref[...] * _sigmoid(g_ref[...])
        for cc in range(nc):
            buf[cc, 0:CONV_HALO, :] = glu_prev[:, cc * lane:(cc + 1) * lane]
            buf[cc, CONV_HALO:, :] = glu[:, cc * lane:(cc + 1) * lane]

        def chunk(cc, carry):
            for r0 in range(0, ts, rc):
                acc = jnp.zeros((rc, lane), F32)
                for kk in range(CONV_WIDTH):
                    lo = CONV_HALO - (CONV_WIDTH - 1 - kk) + r0
                    acc = acc + w_ref[cc, kk:kk + 1, :] * buf[cc, lo:lo + rc, :]
                cvb[cc, r0:r0 + rc, :] = acc
            return carry

        lax.fori_loop(0, nc, chunk, 0)
        for cc in range(nc):
            cv_ref[:, cc * lane:(cc + 1) * lane] = cvb[cc] + bdw_ref[:, cc * lane:(cc + 1) * lane]
        cv = cv_ref[...]
        mu = jnp.mean(cv, axis=-1, keepdims=True)
        xc = cv - mu
        rstd = lax.rsqrt(jnp.mean(xc * xc, axis=-1, keepdims=True) + EPS)
        ln = xc * rstd * lg_ref[...] + lb_ref[...]
        s_ref[...] = (ln * _sigmoid(ln)).astype(BF16)

    a_cur = pl.BlockSpec((ts, C), lambda b, j: (b * nj + j, 0))
    g_cur = pl.BlockSpec((ts, C), lambda b, j: (b * nj + j, 1))
    a_prev = pl.BlockSpec((CONV_HALO, C), lambda b, j: (jnp.maximum((b * nj + j) * hb - 1, 0), 0))
    g_prev = pl.BlockSpec((CONV_HALO, C), lambda b, j: (jnp.maximum((b * nj + j) * hb - 1, 0), 1))
    wspec = pl.BlockSpec((nc, CONV_HALO, lane), lambda b, j: (0, 0, 0))
    one = pl.BlockSpec((1, C), lambda b, j: (0, 0))
    return pl.pallas_call(
        body,
        name="conv_fwd",
        grid=(B, nj),
        out_shape=(jax.ShapeDtypeStruct((T, C), F32), jax.ShapeDtypeStruct((T, C), BF16)),
        in_specs=[a_cur, g_cur, a_prev, g_prev, wspec, one, one, one],
        out_specs=(a_cur, a_cur),
        scratch_shapes=[pltpu.VMEM((nc, CONV_HALO + ts, lane), F32), pltpu.VMEM((nc, ts, lane), F32)],
        compiler_params=_params("parallel", "arbitrary"),
    )(u, u, u, u, w3, b_dw, ln_g, ln_b)


def _lnsilu_bwd(ds, cv, ln_g, ln_b, tm_pref=256):
    T, C = cv.shape
    tm = _pick(T, tm_pref, 16)

    def body(ds_ref, cv_ref, lg_ref, lb_ref, dcv_ref, dlg_ref, dlb_ref, dbdw_ref):
        @pl.when(pl.program_id(0) == 0)
        def _():
            dlg_ref[...] = jnp.zeros_like(dlg_ref)
            dlb_ref[...] = jnp.zeros_like(dlb_ref)
            dbdw_ref[...] = jnp.zeros_like(dbdw_ref)

        cv_v = cv_ref[...]
        g = lg_ref[...]
        mu = jnp.mean(cv_v, axis=-1, keepdims=True)
        xc = cv_v - mu
        rstd = lax.rsqrt(jnp.mean(xc * xc, axis=-1, keepdims=True) + EPS)
        xhat = xc * rstd
        ln = xhat * g + lb_ref[...]
        sg = _sigmoid(ln)
        dln = ds_ref[...] * (sg * (1.0 + ln * (1.0 - sg)))
        dlg_ref[...] += jnp.sum(dln * xhat, axis=0, keepdims=True)
        dlb_ref[...] += jnp.sum(dln, axis=0, keepdims=True)
        dxhat = dln * g
        dcv = rstd * (dxhat - jnp.mean(dxhat, axis=-1, keepdims=True)
                      - xhat * jnp.mean(dxhat * xhat, axis=-1, keepdims=True))
        dcv_ref[...] = dcv
        dbdw_ref[...] += jnp.sum(dcv, axis=0, keepdims=True)

    row = pl.BlockSpec((tm, C), lambda i: (i, 0))
    one = pl.BlockSpec((1, C), lambda i: (0, 0))
    return pl.pallas_call(
        body,
        name="lnsilu_bwd",
        grid=(T // tm,),
        out_shape=(jax.ShapeDtypeStruct((T, C), F32),) + (jax.ShapeDtypeStruct((1, C), F32),) * 3,
        in_specs=[row, row, one, one],
        out_specs=(row, one, one, one),
        compiler_params=_params("arbitrary"),
    )(ds, cv, ln_g, ln_b)


def _conv_bwd(dcv, u, w3, S):
    T, C2 = u.shape
    C = C2 // 2
    B = T // S
    ts, nj = _conv_tiles(S)
    hb = ts // CONV_HALO
    n_halo_blocks = T // CONV_HALO
    lane, nc, rc = _conv_chunks(C, ts)

    def body(dcv_ref, dnx_ref, a_ref, g_ref, ap_ref, gp_ref, w_ref, du_ref, dw_ref, gbuf, dbuf, dglu, dw8):
        b, j = pl.program_id(0), pl.program_id(1)

        @pl.when((b == 0) & (j == 0))
        def _():
            dw8[...] = jnp.zeros_like(dw8)

        a = a_ref[...]
        sg = _sigmoid(g_ref[...])
        glu_prev = ap_ref[...] * _sigmoid(gp_ref[...]) * (j > 0).astype(F32)
        glu = a * sg
        dcur = dcv_ref[...]
        dnext = dnx_ref[...] * (j < nj - 1).astype(F32)
        for cc in range(nc):
            cols = slice(cc * lane, (cc + 1) * lane)
            gbuf[cc, 0:CONV_HALO, :] = glu_prev[:, cols]
            gbuf[cc, CONV_HALO:, :] = glu[:, cols]
            dbuf[cc, 0:ts, :] = dcur[:, cols]
            dbuf[cc, ts:, :] = dnext[:, cols]

        def chunk(cc, carry):
            for r0 in range(0, ts, rc):
                acc = jnp.zeros((rc, lane), F32)
                for kk in range(CONV_WIDTH):
                    d = CONV_WIDTH - 1 - kk
                    acc = acc + w_ref[cc, kk:kk + 1, :] * dbuf[cc, r0 + d:r0 + d + rc, :]
                dglu[cc, r0:r0 + rc, :] = acc
            for kk in range(CONV_WIDTH):
                d = CONV_WIDTH - 1 - kk
                p = jnp.zeros((rc, lane), F32)
                for r0 in range(0, ts, rc):
                    lo = CONV_HALO - d + r0
                    p = p + dbuf[cc, r0:r0 + rc, :] * gbuf[cc, lo:lo + rc, :]
                dw8[cc, kk * 8:(kk + 1) * 8, :] += jnp.sum(p.reshape(rc // 8, 8, lane), axis=0)
            return carry

        lax.fori_loop(0, nc, chunk, 0)
        for cc in range(nc):
            cols = slice(cc * lane, (cc + 1) * lane)
            dgl = dglu[cc]
            du_ref[:, cc * lane:(cc + 1) * lane] = (dgl * sg[:, cols]).astype(BF16)
            du_ref[:, C + cc * lane:C + (cc + 1) * lane] = (dgl * a[:, cols] * sg[:, cols] * (1.0 - sg[:, cols])).astype(BF16)

        @pl.when((b == B - 1) & (j == nj - 1))
        def _():
            dw_ref[...] = jnp.zeros_like(dw_ref)
            for kk in range(CONV_WIDTH):
                dw_ref[:, kk:kk + 1, :] = jnp.sum(dw8[:, kk * 8:(kk + 1) * 8, :], axis=1, keepdims=True)

    a_cur = pl.BlockSpec((ts, C), lambda b, j: (b * nj + j, 0))
    g_cur = pl.BlockSpec((ts, C), lambda b, j: (b * nj + j, 1))
    a_prev = pl.BlockSpec((CONV_HALO, C), lambda b, j: (jnp.maximum((b * nj + j) * hb - 1, 0), 0))
    g_prev = pl.BlockSpec((CONV_HALO, C), lambda b, j: (jnp.maximum((b * nj + j) * hb - 1, 0), 1))
    d_next = pl.BlockSpec((CONV_HALO, C), lambda b, j: (jnp.minimum((b * nj + j + 1) * hb, n_halo_blocks - 1), 0))
    wspec = pl.BlockSpec((nc, CONV_HALO, lane), lambda b, j: (0, 0, 0))
    return pl.pallas_call(
        body,
        name="conv_bwd",
        grid=(B, nj),
        out_shape=(jax.ShapeDtypeStruct((T, C2), BF16), jax.ShapeDtypeStruct((nc, CONV_HALO, lane), F32)),
        in_specs=[a_cur, d_next, a_cur, g_cur, a_prev, g_prev, wspec],
        out_specs=(pl.BlockSpec((ts, C2), lambda b, j: (b * nj + j, 0)), wspec),
        scratch_shapes=[
            pltpu.VMEM((nc, CONV_HALO + ts, lane), F32),
            pltpu.VMEM((nc, ts + CONV_HALO, lane), F32),
            pltpu.VMEM((nc, ts, lane), F32),
            pltpu.VMEM((nc, CONV_HALO * 8, lane), F32),
        ],
        compiler_params=_params("arbitrary", "arbitrary"),
    )(dcv, dcv, u, u, u, u, w3)


def _loss_head(x, tgt, gfin, y, gate, S, tm_pref=256):
    T, D = x.shape
    tm = _pick(S, tm_pref, 16)
    tpb = S // tm

    def body(x_ref, t_ref, g_ref, y_ref, gt_ref, dx_ref, loss_ref, dg_ref, dy_ref, dgate_ref):
        @pl.when(pl.program_id(0) == 0)
        def _():
            loss_ref[...] = jnp.zeros_like(loss_ref)
            dg_ref[...] = jnp.zeros_like(dg_ref)

        @pl.when(pl.program_id(0) % tpb == 0)
        def _():
            dgate_ref[...] = jnp.zeros_like(dgate_ref)

        xv = x_ref[...]
        g = g_ref[...]
        r = lax.rsqrt(jnp.mean(xv * xv, axis=-1, keepdims=True) + EPS)
        xhat = xv * r
        e = xhat * g - t_ref[...]
        row_loss = jnp.mean(e * e, axis=-1, keepdims=True)
        loss_ref[...] += 0.5 * jnp.sum(row_loss, axis=0, keepdims=True)
        dy = e * (1.0 / D)
        dg_ref[...] += jnp.sum(dy * xhat, axis=0, keepdims=True)
        dxhat = dy * g
        dx = r * (dxhat - xhat * jnp.mean(dxhat * xhat, axis=-1, keepdims=True))
        dx_ref[...] = dx
        dy_ref[...] = (dx * gt_ref[...]).astype(BF16)
        dgate_ref[...] += jnp.sum(dx * y_ref[...].astype(F32), axis=0, keepdims=True)

    row = pl.BlockSpec((tm, D), lambda i: (i, 0))
    vec = pl.BlockSpec((None, 1, D), lambda i: (i // tpb, 0, 0))
    one = pl.BlockSpec((1, D), lambda i: (0, 0))
    return pl.pallas_call(
        body,
        name="loss_head",
        grid=(T // tm,),
        out_shape=(
            jax.ShapeDtypeStruct((T, D), F32),
            jax.ShapeDtypeStruct((8, 128), F32),
            jax.ShapeDtypeStruct((1, D), F32),
            jax.ShapeDtypeStruct((T, D), BF16),
            jax.ShapeDtypeStruct((T // S, 1, D), F32),
        ),
        in_specs=[row, row, one, row, vec],
        out_specs=(row, pl.BlockSpec((8, 128), lambda i: (0, 0)), one, row, vec),
        compiler_params=_params("arbitrary"),
    )(x, tgt, gfin, y, gate)


def _adam(name, parts, w, m, v, tm_pref=256):
    P, R, C = parts.shape
    tm = _pick(R, tm_pref, 16)
    c1 = 1.0 - ADAM_B1 ** ADAM_STEP
    c2 = 1.0 - ADAM_B2 ** ADAM_STEP

    def body(p_ref, w_ref, m_ref, v_ref, g_ref, d_ref, mo_ref, vo_ref):
        g = p_ref[0].astype(F32)
        for i in range(1, P):
            g = g + p_ref[i].astype(F32)
        m_new = ADAM_B1 * m_ref[...] + (1.0 - ADAM_B1) * g
        v_new = ADAM_B2 * v_ref[...] + (1.0 - ADAM_B2) * (g * g)
        m_hat = m_new / c1
        v_hat = v_new / c2
        g_ref[...] = g
        d_ref[...] = -ADAM_LR * (m_hat / (jnp.sqrt(v_hat) + ADAM_EPS) + ADAM_WD * w_ref[...])
        mo_ref[...] = m_new
        vo_ref[...] = v_new

    row = pl.BlockSpec((tm, C), lambda i: (i, 0))
    return pl.pallas_call(
        body,
        name=name,
        grid=(R // tm,),
        out_shape=(jax.ShapeDtypeStruct((R, C), F32),) * 4,
        in_specs=[pl.BlockSpec((P, tm, C), lambda i: (0, i, 0)), row, row, row],
        out_specs=(row, row, row, row),
        compiler_params=_params("parallel"),
    )(parts, w, m, v)


BIG = ("w_qkv", "w_o", "w_pw1", "w_pw2", "w_up", "w_down")
SMALL_SHARDED = (("b_pw1", 1), ("w_dw", 2), ("b_dw", 1), ("conv_ln_g", 1), ("conv_ln_b", 1), ("b_pw2", 1))
SMALL_REPL = ("b_mod", "norm_mix", "norm_mlp", "b_qkv", "b_o", "sinks", "final_norm")
WEIGHTS = ("w_mod", "b_mod", "norm_mix", "norm_mlp", "w_qkv", "b_qkv", "w_o", "b_o", "sinks", "w_pw1", "b_pw1",
           "w_dw", "b_dw", "conv_ln_g", "conv_ln_b", "w_pw2", "b_pw2", "w_up", "w_down", "final_norm")


def _step(x, c, loss_target, W, M, V):
    B, S, D = x.shape
    T = B * S
    L = W["w_mod"].shape[0]
    n_mod = W["w_mod"].shape[2]
    me = 4 * lax.axis_index("x") + 2 * lax.axis_index("y") + lax.axis_index("c")
    q_dim = N_HEADS * HEAD_DIM
    kv_dim = N_KV_HEADS * HEAD_DIM

    Wb = {n: W[n].astype(BF16) for n in BIG}

    def shards(i):
        jm = i // 2
        first, last = ("w_qkv", "w_o") if i % 2 == 0 else ("w_pw1", "w_pw2")
        return [Wb[first][jm], Wb[last][jm], Wb["w_up"][i], Wb["w_down"][i]]

    def carried(res, payload):
        return res if payload else (res, [])

    small_names = [n for n, _ in SMALL_SHARDED]
    small_src, small_sizes = _pack([c] + [W[n] for n in small_names], F32, 0)
    got = _exchange("gather_first", [(small_src, True)] + [(s, True) for s in shards(0)])
    small_parts = _unpack(got[0], small_sizes, [c.shape] + [W[n].shape for n in small_names], 1)
    c_all = small_parts[0].reshape(N_DEV * B, D)
    full = {n: _from_slots(p, d) for (n, d), p in zip(SMALL_SHARDED, small_parts[1:])}
    gathered = [got[1:]] + [None] * (L - 1)

    b_mod_mine = lax.dynamic_slice_in_dim(W["b_mod"], me * n_mod, n_mod, axis=1).reshape(L, 1, n_mod)
    mod_part = _mod_fwd(c_all, W["w_mod"].astype(BF16), b_mod_mine)
    mod_slots = mod_part.reshape(L, N_DEV, B, n_mod).transpose(1, 0, 2, 3).reshape(N_DEV, L * B, n_mod)
    mod_recv = _exchange("scatter_mod", [(mod_slots, False)])[0]
    mod = mod_recv.reshape(N_DEV, L, B, n_mod).transpose(1, 2, 0, 3).reshape(L, B, N_MOD, 1, D)

    w_dw3 = [_conv_weight_chunks(full["w_dw"][j], D) for j in range(full["w_dw"].shape[0])]

    xc = x.reshape(T, D)
    saved = []
    h1 = _normmod_fwd("normmod_mix_fwd", xc, W["norm_mix"][0][None], mod[0, :, 1], mod[0, :, 0], S)
    for i in range(L):
        jm = i // 2
        sh1, sc1, g1, sh2, sc2, g2 = (mod[i, :, t] for t in range(N_MOD))
        nxt = [[(s, True)] for s in shards(i + 1)] if i + 1 < L else [[]] * 4
        g_first, g_last, g_up, g_down = gathered[i]
        w_last = g_last.reshape(-1, D)
        w_down_full = g_down.reshape(-1, D)
        sv = {"x_in": xc, "w_last": w_last, "g_up": g_up, "w_down": w_down_full}
        sv["h1"] = h1
        if i % 2 == 0:
            w_qkv_full = _from_slots(g_first, 1)
            sv["w_first"] = w_qkv_full
            qkv, n_first = carried(_mm("qkv_fwd", h1, w_qkv_full, bias=W["b_qkv"][jm], payload=nxt[0]), nxt[0])
            tab = _pair_tables(_attn_table(W["sinks"][jm]))
            mix_in = _attn_fwd(qkv, tab, B, S)
            sv.update(qkv=qkv, tab=tab)
            b_out = W["b_o"][jm]
        else:
            sv["w_first"] = g_first
            u, n_first = carried(_mm("pw1_fwd", h1, g_first, w_form="nslots", bias=full["b_pw1"][jm], out_dtype=F32,
                                     payload=nxt[0]), nxt[0])
            cv, mix_in = _conv_fwd(u, w_dw3[jm], full["b_dw"][jm][None], full["conv_ln_g"][jm][None],
                                   full["conv_ln_b"][jm][None], S)
            sv.update(u=u, cv=cv)
            b_out = full["b_pw2"][jm]
        sv["mix_in"] = mix_in
        (x1, y1, h2), n_last = carried(_mm("mix_out_fwd", mix_in, w_last, bias=b_out, epi="resid", x=xc, gate=g1,
                                           norm=(W["norm_mlp"][i][None], sc2, sh2), rows_per_batch=S,
                                           payload=nxt[1]), nxt[1])
        sv.update(y1=y1, x1=x1)
        up, n_up = carried(_mm("mlp_up_fwd", h2, g_up, w_form="nslots", epi="relu", payload=nxt[2]), nxt[2])
        norm_next = (W["norm_mix"][i + 1][None], mod[i + 1, :, 1], mod[i + 1, :, 0]) if i + 1 < L else None
        res, n_down = carried(_mm("mlp_down_fwd", up, w_down_full, a_sq=True, epi="resid", x=x1, gate=g2,
                                  norm=norm_next, rows_per_batch=S, payload=nxt[3]), nxt[3])
        x2, y2 = res[0], res[1]
        h1 = res[2] if i + 1 < L else None
        sv.update(h2=h2, up=up, y2=y2)
        saved.append(sv)
        xc = x2
        if i + 1 < L:
            gathered[i + 1] = [n_first[0], n_last[0], n_up[0], n_down[0]]

    dx, loss_blk, dgfin, dy2, dg2 = _loss_head(xc, loss_target.reshape(T, D), W["final_norm"][None],
                                               saved[L - 1]["y2"], mod[L - 1, :, 5], S)
    loss = lax.psum(loss_blk[0, 0], ("x", "y", "c"))

    G = {"final_norm": dgfin.reshape(D)}
    dmod_layers = [None] * L
    small_grads = ("norm_mix", "norm_mlp", "b_qkv", "b_o", "sinks", "b_pw1", "w_dw", "b_dw", "conv_ln_g", "conv_ln_b",
                   "b_pw2")
    acc = {n: [None] * W[n].shape[0] for n in small_grads}
    reduced = {n: [None] * W[n].shape[0] for n in BIG}

    for i in reversed(range(L)):
        jm = i // 2
        sv = saved[i]
        sh1, sc1, g1, sh2, sc2, g2 = (mod[i, :, t] for t in range(N_MOD))
        gw_down = _mm_tn("w_down_grad", sv["up"], dy2, a_sq=True, out_form="kslots", tt_pref=4096)
        du, r = _mm("mlp_down_bwd", dy2, sv["w_down"], w_form="full_t", epi="relu2d", u=sv["up"],
                    payload=[(gw_down, False)])
        reduced["w_down"][i] = r[0]
        gw_up = _mm_tn("w_up_grad", sv["h2"], du, out_form="nslots", tt_pref=4096)
        (dx, dsh2, dsc2, dgn, dy1, dg1, dy1_sum), r = _mm(
            "mlp_up_bwd", du, sv["g_up"], w_form="kslots_t", epi="normbwd", rows_per_batch=S,
            nb=dict(x=sv["x1"], dx_in=dx, gnorm=W["norm_mlp"][i][None], sc=sc2, gate=(sv["y1"], g1)),
            payload=[(gw_up, False)])
        reduced["w_up"][i] = r[0]
        acc["norm_mlp"][i] = dgn.reshape(D)
        gw_last = _mm_tn("mix_out_grad", sv["mix_in"], dy1)
        gw_last = gw_last.reshape((N_DEV, -1) + gw_last.shape[1:]).astype(BF16)
        below = (saved[i - 1]["y2"], mod[i - 1, :, 5]) if i > 0 else None
        nb_mix = dict(x=sv["x_in"], dx_in=dx, gnorm=W["norm_mix"][i][None], sc=sc1, gate=below)
        if i % 2 == 0:
            acc["b_o"][jm] = dy1_sum.reshape(D)
            do, r = _mm("attn_out_bwd", dy1, sv["w_last"], w_form="full_t", payload=[(gw_last, False)])
            reduced["w_o"][jm] = r[0]
            dq, dkv, dsk = _attn_bwd(sv["qkv"], sv["mix_in"], do, sv["tab"], B, S)
            acc["sinks"][jm] = jnp.sum(dsk[..., 0], axis=0).transpose(0, 2, 1).reshape(N_HEADS)
            dqkv = jnp.concatenate([dq, dkv], axis=1)
            gw_qkv, db_qkv = _mm_tn("w_qkv_grad", sv["h1"], dqkv, colsum=True)
            acc["b_qkv"][jm] = db_qkv.reshape(-1)
            gw_qkv = _to_slots(gw_qkv, 1).astype(BF16)
            res, r = _mm("qkv_bwd", dqkv, sv["w_first"], w_form="full_t", epi="normbwd", nb=nb_mix, rows_per_batch=S,
                         payload=[(gw_qkv, False)])
            reduced["w_qkv"][jm] = r[0]
        else:
            acc["b_pw2"][jm] = dy1_sum.reshape(D)
            ds, r = _mm("pw2_bwd", dy1, sv["w_last"], w_form="full_t", out_dtype=F32, payload=[(gw_last, False)])
            reduced["w_pw2"][jm] = r[0]
            dcv, dlg, dlb, dbdw = _lnsilu_bwd(ds, sv["cv"], full["conv_ln_g"][jm][None], full["conv_ln_b"][jm][None])
            acc["conv_ln_g"][jm], acc["conv_ln_b"][jm], acc["b_dw"][jm] = dlg.reshape(-1), dlb.reshape(-1), dbdw.reshape(-1)
            du1, dwdw = _conv_bwd(dcv, sv["u"], w_dw3[jm], S)
            acc["w_dw"][jm] = dwdw.transpose(1, 0, 2).reshape(CONV_HALO, D)[:CONV_WIDTH]
            gw_pw1, db_pw1 = _mm_tn("w_pw1_grad", sv["h1"], du1, out_form="nslots", colsum=True, tt_pref=4096)
            acc["b_pw1"][jm] = db_pw1.reshape(-1)
            res, r = _mm("pw1_bwd", du1, sv["w_first"], w_form="kslots_t", epi="normbwd", nb=nb_mix, rows_per_batch=S,
                         payload=[(gw_pw1, False)])
            reduced["w_pw1"][jm] = r[0]
        dx, dsh1, dsc1, dgn = res[:4]
        acc["norm_mix"][i] = dgn.reshape(D)
        dmod_layers[i] = jnp.concatenate([dsh1, dsc1, dg1, dsh2, dsc2, dg2], axis=1).reshape(B, N_MOD * D)
        if i > 0:
            dy2, dg2 = res[4], res[5]
    grad_x = dx.reshape(B, S, D)
    for n, parts in acc.items():
        G[n] = jnp.stack(parts)

    dmod = jnp.stack(dmod_layers)
    dmod_slots = dmod.reshape(L, B, N_DEV, n_mod).transpose(2, 0, 1, 3).reshape(N_DEV, L * B, n_mod)
    dmod_recv = _exchange("gather_dmod", [(dmod_slots, False)])[0]
    dmod_all = dmod_recv.reshape(N_DEV, L, B, n_mod).transpose(1, 0, 2, 3).reshape(L, N_DEV * B, n_mod)
    g_w_mod, db_mod_mine = _mod_bwd(c_all, dmod_all)
    G["b_mod"] = lax.dynamic_update_slice_in_dim(jnp.zeros_like(W["b_mod"]), db_mod_mine.reshape(L, n_mod),
                                                 me * n_mod, axis=1)

    small_items = [jnp.broadcast_to(G[n][None], (N_DEV,) + G[n].shape) for n in SMALL_REPL]
    small_items += [_to_slots(G[n], d) for n, d in SMALL_SHARDED]
    small_slots, small_sizes2 = _pack(small_items, F32, 1)
    small_recv = _exchange("reduce_small", [(small_slots, False)])[0]

    out = {}

    def run_adam(name, parts, names):
        shapes = [W[n].shape for n in names]
        wp, sizes = _pack([W[n] for n in names], F32, 0)
        mp, _ = _pack([M[n] for n in names], F32, 0)
        vp, _ = _pack([V[n] for n in names], F32, 0)
        res = _adam(name, parts, wp, mp, vp)
        for kind, buf in zip(("grad", "delta", "new_m", "new_v"), res):
            for n, a in zip(names, _unpack(buf, sizes, shapes, 0)):
                out[kind + "_" + n] = a

    for n in BIG + ("w_mod",):
        cols = W[n].shape[-1]
        if n == "w_mod":
            parts = g_w_mod.reshape(1, -1, cols)
        else:
            parts = jnp.stack(reduced[n], axis=1).reshape(N_DEV, -1, cols)
        res = _adam("adam_" + n, parts, W[n].reshape(-1, cols), M[n].reshape(-1, cols), V[n].reshape(-1, cols))
        for kind, buf in zip(("grad", "delta", "new_m", "new_v"), res):
            out[kind + "_" + n] = buf.reshape(W[n].shape)
    run_adam("adam_small", small_recv, list(SMALL_REPL) + [n for n, _ in SMALL_SHARDED])

    res = [loss, grad_x]
    for kind in ("grad", "delta", "new_m", "new_v"):
        res += [out[kind + "_" + n] for n in WEIGHTS]
    return tuple(res)


def kernel(x, c, w_mod, b_mod, norm_mix, norm_mlp, w_qkv, b_qkv, w_o, b_o, sinks, w_pw1, b_pw1, w_dw, b_dw, conv_ln_g, conv_ln_b, w_pw2, b_pw2, w_up, w_down, final_norm, loss_target, m_w_mod, m_b_mod, m_norm_mix, m_norm_mlp, m_w_qkv, m_b_qkv, m_w_o, m_b_o, m_sinks, m_w_pw1, m_b_pw1, m_w_dw, m_b_dw, m_conv_ln_g, m_conv_ln_b, m_w_pw2, m_b_pw2, m_w_up, m_w_down, m_final_norm, v_w_mod, v_b_mod, v_norm_mix, v_norm_mlp, v_w_qkv, v_b_qkv, v_w_o, v_b_o, v_sinks, v_w_pw1, v_b_pw1, v_w_dw, v_b_dw, v_conv_ln_g, v_conv_ln_b, v_w_pw2, v_b_pw2, v_w_up, v_w_down, v_final_norm):
    W = dict(w_mod=w_mod, b_mod=b_mod, norm_mix=norm_mix, norm_mlp=norm_mlp, w_qkv=w_qkv, b_qkv=b_qkv, w_o=w_o,
             b_o=b_o, sinks=sinks, w_pw1=w_pw1, b_pw1=b_pw1, w_dw=w_dw, b_dw=b_dw, conv_ln_g=conv_ln_g,
             conv_ln_b=conv_ln_b, w_pw2=w_pw2, b_pw2=b_pw2, w_up=w_up, w_down=w_down, final_norm=final_norm)
    M = dict(w_mod=m_w_mod, b_mod=m_b_mod, norm_mix=m_norm_mix, norm_mlp=m_norm_mlp, w_qkv=m_w_qkv, b_qkv=m_b_qkv,
             w_o=m_w_o, b_o=m_b_o, sinks=m_sinks, w_pw1=m_w_pw1, b_pw1=m_b_pw1, w_dw=m_w_dw, b_dw=m_b_dw,
             conv_ln_g=m_conv_ln_g, conv_ln_b=m_conv_ln_b, w_pw2=m_w_pw2, b_pw2=m_b_pw2, w_up=m_w_up,
             w_down=m_w_down, final_norm=m_final_norm)
    V = dict(w_mod=v_w_mod, b_mod=v_b_mod, norm_mix=v_norm_mix, norm_mlp=v_norm_mlp, w_qkv=v_w_qkv, b_qkv=v_b_qkv,
             w_o=v_w_o, b_o=v_b_o, sinks=v_sinks, w_pw1=v_w_pw1, b_pw1=v_b_pw1, w_dw=v_w_dw, b_dw=v_b_dw,
             conv_ln_g=v_conv_ln_g, conv_ln_b=v_conv_ln_b, w_pw2=v_w_pw2, b_pw2=v_b_pw2, w_up=v_w_up,
             w_down=v_w_down, final_norm=v_final_norm)
    return _step(x, c, loss_target, W, M, V)
```

```python
import functools

import numpy as np
import jax
import jax.numpy as jnp
from jax import lax
from jax.experimental import pallas as pl
from jax.experimental.pallas import tpu as pltpu

F32 = jnp.float32
BF16 = jnp.bfloat16

N_DEV = 8
N_HEADS = 16
N_KV_HEADS = 2
HEAD_DIM = 64
GROUP = N_HEADS // N_KV_HEADS
ATT_BLOCK = 128
CONV_WIDTH = 31
CONV_HALO = 32
N_MOD = 6
EPS = 1e-6
ADAM_LR = 0.001
ADAM_B1 = 0.9
ADAM_B2 = 0.999
ADAM_EPS = 1e-08
ADAM_WD = 0.01
ADAM_STEP = 10
NEG_BIG = -1e30
PACK_COLS = 1024
VMEM_LIMIT_BYTES = 56 * 1024 * 1024
MESH_ID = pl.DeviceIdType.MESH


def _params(*sem):
    return pltpu.CompilerParams(dimension_semantics=sem, vmem_limit_bytes=VMEM_LIMIT_BYTES)


def _pick(n, pref, mult=8):
    if n <= pref:
        return n
    for t in range(pref, 0, -1):
        if n % t == 0 and t % mult == 0:
            return t
    return n


def _sigmoid(z):
    return 0.5 * jnp.tanh(0.5 * z) + 0.5


def _payload_layout(payload):
    n = len(payload)
    out_shapes = [jax.ShapeDtypeStruct((N_DEV,) + tuple(a.shape if bc else a.shape[1:]), a.dtype) for a, bc in payload]
    hbm = pl.BlockSpec(memory_space=pl.ANY)
    scratch = [pltpu.SemaphoreType.DMA((n * (N_DEV - 1),)), pltpu.SemaphoreType.DMA((n * (N_DEV - 1),)),
               pltpu.SemaphoreType.DMA((n,))]
    return [a for a, _ in payload], [hbm] * n, out_shapes, [hbm] * n, scratch


class _Plan:
    def __init__(self, bcasts, src_refs, dst_refs, send_sems, recv_sems, local_sems):
        x, y, c = lax.axis_index("x"), lax.axis_index("y"), lax.axis_index("c")
        me = 4 * x + 2 * y + c
        self.first, self.landed, self.relay, self.local = [], [], [], []
        for t, (bc, s_ref, d_ref) in enumerate(zip(bcasts, src_refs, dst_refs)):
            def remote(k, src, slot, to):
                sem = t * (N_DEV - 1) + k
                return pltpu.make_async_remote_copy(src_ref=src, dst_ref=d_ref.at[slot], send_sem=send_sems.at[sem],
                                                    recv_sem=recv_sems.at[sem], device_id=to, device_id_type=MESH_ID)
            if bc:
                chips = [(1 - x, y), (x, 1 - y), (1 - x, 1 - y)]
                self.first.append(remote(0, s_ref, me, (x, y, 1 - c)))
                for j, (px, py) in enumerate(chips):
                    cp = remote(1 + j, s_ref, me, (px, py, c))
                    self.first.append(cp)
                    self.landed.append(cp)
                    theirs = 4 * px + 2 * py + c
                    self.relay.append(remote(4 + j, d_ref.at[theirs], theirs, (x, y, 1 - c)))
                self.local.append(pltpu.make_async_copy(s_ref, d_ref.at[me], local_sems.at[t]))
            else:
                for k in range(1, N_DEV):
                    px = 1 - x if (k >> 2) & 1 else x
                    py = 1 - y if (k >> 1) & 1 else y
                    pc = 1 - c if k & 1 else c
                    self.first.append(remote(k - 1, s_ref.at[4 * px + 2 * py + pc], me, (px, py, pc)))
                self.local.append(pltpu.make_async_copy(s_ref.at[me], d_ref.at[me], local_sems.at[t]))

    def start(self):
        for cp in self.first + self.local:
            cp.start()

    def pass_on(self):
        for cp in self.landed:
            cp.wait_recv()
        for cp in self.relay:
            cp.start()

    def finish(self):
        for cp in self.first:
            cp.wait_send()
            if not any(cp is l for l in self.landed):
                cp.wait_recv()
        for cp in self.relay:
            cp.wait()
        for cp in self.local:
            cp.wait()


def _exchange(name, payload):
    n = len(payload)
    bcasts = [bc for _, bc in payload]
    arrays, in_specs, out_shapes, out_specs, scratch = _payload_layout(payload)

    def body(*refs):
        plan = _Plan(bcasts, refs[:n], refs[n:2 * n], *refs[2 * n:])
        plan.start()
        plan.pass_on()
        plan.finish()

    return pl.pallas_call(
        body,
        name=name,
        out_shape=tuple(out_shapes),
        in_specs=in_specs,
        out_specs=tuple(out_specs),
        scratch_shapes=scratch,
    )(*arrays)


def _pack(arrays, dtype, lead):
    lead_shape = arrays[0].shape[:lead]
    flat = [a.astype(dtype).reshape(lead_shape + (-1,)) for a in arrays]
    sizes = [f.shape[-1] for f in flat]
    total = sum(sizes)
    chunk = 16 * PACK_COLS
    padded = -(-total // chunk) * chunk
    if padded > total:
        flat.append(jnp.zeros(lead_shape + (padded - total,), dtype))
    buf = jnp.concatenate(flat, axis=-1)
    return buf.reshape(lead_shape + (padded // PACK_COLS, PACK_COLS)), sizes


def _unpack(buf, sizes, shapes, lead):
    lead_shape = buf.shape[:lead]
    flat = buf.reshape(lead_shape + (-1,))
    out, off = [], 0
    for n, shp in zip(sizes, shapes):
        out.append(lax.slice_in_dim(flat, off, off + n, axis=lead).reshape(lead_shape + tuple(shp)))
        off += n
    return out


def _to_slots(a, dim):
    shp = a.shape
    a = a.reshape(shp[:dim] + (N_DEV, shp[dim] // N_DEV) + shp[dim + 1:])
    return jnp.moveaxis(a, dim, 0)


def _from_slots(a, dim):
    a = jnp.moveaxis(a, 0, dim)
    shp = a.shape
    return a.reshape(shp[:dim] + (shp[dim] * shp[dim + 1],) + shp[dim + 2:])


def _mod_fwd(c_all, w, b):
    L, D, n = w.shape
    M = c_all.shape[0]

    def body(c_ref, w_ref, b_ref, o_ref):
        cv = c_ref[...]
        cs = (cv * _sigmoid(cv)).astype(BF16)
        o_ref[...] = jnp.dot(cs, w_ref[...], preferred_element_type=F32) + b_ref[...]

    return pl.pallas_call(
        body,
        name="mod_fwd",
        grid=(L,),
        out_shape=jax.ShapeDtypeStruct((L, M, n), F32),
        in_specs=[
            pl.BlockSpec((M, D), lambda l: (0, 0)),
            pl.BlockSpec((None, D, n), lambda l: (l, 0, 0)),
            pl.BlockSpec((None, 1, n), lambda l: (l, 0, 0)),
        ],
        out_specs=pl.BlockSpec((None, M, n), lambda l: (l, 0, 0)),
        compiler_params=_params("arbitrary"),
    )(c_all, w, b)


def _mod_bwd(c_all, dmod_all):
    L, M, n = dmod_all.shape
    D = c_all.shape[1]

    def body(c_ref, d_ref, dw_ref, db_ref):
        cv = c_ref[...]
        cs = (cv * _sigmoid(cv)).astype(BF16)
        d = d_ref[...]
        dw_ref[...] = lax.dot_general(cs, d.astype(BF16), (((0,), (0,)), ((), ())), preferred_element_type=F32)
        db_ref[...] = jnp.sum(d, axis=0, keepdims=True)

    return pl.pallas_call(
        body,
        name="mod_bwd",
        grid=(L,),
        out_shape=(jax.ShapeDtypeStruct((L, D, n), F32), jax.ShapeDtypeStruct((L, 1, n), F32)),
        in_specs=[
            pl.BlockSpec((M, D), lambda l: (0, 0)),
            pl.BlockSpec((None, M, n), lambda l: (l, 0, 0)),
        ],
        out_specs=(
            pl.BlockSpec((None, D, n), lambda l: (l, 0, 0)),
            pl.BlockSpec((None, 1, n), lambda l: (l, 0, 0)),
        ),
        compiler_params=_params("arbitrary"),
    )(c_all, dmod_all)


def _mm(name, a, w, *, w_form="full", out_dtype=BF16, bias=None, a_sq=False, epi=None, x=None, gate=None, u=None,
        norm=None, nb=None, rows_per_batch=None, tm_pref=512, payload=()):
    M, K = a.shape
    if w_form == "full":
        N = w.shape[1]
        nc = _pick(N, 1024, 128)
    elif w_form == "full_t":
        N = w.shape[0]
        nc = _pick(N, 1024, 128)
    elif w_form == "nslots":
        nc = w.shape[2]
        N = N_DEV * nc
    else:
        N = w.shape[1]
        nc = N
    ks = K // N_DEV
    n_chunks = N // nc
    tm = _pick(M if rows_per_batch is None else rows_per_batch, tm_pref, 16)
    steps = M // tm
    relay_step = (3 * steps) // 4
    nb_gate = nb is not None and nb.get("gate") is not None
    tpb_nb = rows_per_batch // tm if epi == "normbwd" else 1
    has_bias = bias is not None
    n_pay = len(payload)
    bcasts = [bc for _, bc in payload]
    nt = (((1,), (1,)), ((), ()))


    def body(*refs):
        it = iter(refs)
        a_ref = next(it)
        w_ref = next(it)
        b_ref = next(it) if has_bias else None
        x_ref = next(it) if epi == "resid" else None
        g_ref = next(it) if epi == "resid" else None
        gn_ref, sc_ref, sh_ref = (next(it), next(it), next(it)) if norm is not None else (None, None, None)
        u_ref = next(it) if epi == "relu2d" else None
        if epi == "normbwd":
            nx_ref, ndx_ref, ngn_ref, nsc_ref = next(it), next(it), next(it), next(it)
            ny_ref, ngt_ref = (next(it), next(it)) if nb_gate else (None, None)
        pay_src = [next(it) for _ in range(n_pay)]
        o_ref = next(it)
        y_ref = next(it) if epi == "resid" else None
        h_ref = next(it) if norm is not None else None
        if epi == "normbwd":
            dsh_ref, dsc_ref, dgn_ref = next(it), next(it), next(it)
            dy_ref, dgate_ref, cs_ref = (next(it), next(it), next(it)) if nb_gate else (None, None, None)
        pay_dst = [next(it) for _ in range(n_pay)]
        sems = list(it)

        if epi == "normbwd":
            @pl.when(pl.program_id(0) % tpb_nb == 0)
            def _():
                dsh_ref[...] = jnp.zeros_like(dsh_ref)
                dsc_ref[...] = jnp.zeros_like(dsc_ref)
                if nb_gate:
                    dgate_ref[...] = jnp.zeros_like(dgate_ref)

            @pl.when(pl.program_id(0) == 0)
            def _():
                dgn_ref[...] = jnp.zeros_like(dgn_ref)
                if nb_gate:
                    cs_ref[...] = jnp.zeros_like(cs_ref)

        if n_pay:
            @pl.when(pl.program_id(0) == 0)
            def _():
                _Plan(bcasts, pay_src, pay_dst, *sems).start()

            @pl.when(pl.program_id(0) == relay_step)
            def _():
                _Plan(bcasts, pay_src, pay_dst, *sems).pass_on()

        for rows in (slice(0, tm),):
            av = None if w_form == "kslots_t" else a_ref[rows, :]
            if a_sq:
                av = av * av
            for ci in range(n_chunks):
                cols = slice(ci * nc, (ci + 1) * nc)
                if w_form == "full":
                    acc = jnp.dot(av, w_ref[:, cols], preferred_element_type=F32)
                elif w_form == "full_t":
                    acc = lax.dot_general(av, w_ref[cols, :], nt, preferred_element_type=F32)
                elif w_form == "nslots":
                    acc = jnp.dot(av, w_ref[ci], preferred_element_type=F32)
                else:
                    acc = lax.dot_general(a_ref[rows, 0:ks], w_ref[0], nt, preferred_element_type=F32)
                    for j in range(1, N_DEV):
                        acc = acc + lax.dot_general(a_ref[rows, j * ks:(j + 1) * ks], w_ref[j], nt,
                                                    preferred_element_type=F32)
                if has_bias:
                    acc = acc + b_ref[:, cols]
                if epi == "resid":
                    xn = x_ref[rows, cols] + g_ref[:, cols] * acc
                    o_ref[rows, cols] = xn
                    y_ref[rows, cols] = acc.astype(BF16)
                    if norm is not None:
                        r = lax.rsqrt(jnp.mean(xn * xn, axis=-1, keepdims=True) + EPS)
                        h_ref[rows, :] = ((xn * r * gn_ref[...]) * (1.0 + sc_ref[...]) + sh_ref[...]).astype(BF16)
                elif epi == "relu":
                    o_ref[rows, cols] = jnp.maximum(acc, 0.0).astype(out_dtype)
                elif epi == "relu2d":
                    o_ref[rows, cols] = (acc * (2.0 * u_ref[rows, cols].astype(F32))).astype(out_dtype)
                elif epi == "normbwd":
                    xv = nx_ref[rows, :]
                    gn = ngn_ref[...]
                    r = lax.rsqrt(jnp.mean(xv * xv, axis=-1, keepdims=True) + EPS)
                    xhat = xv * r
                    dsh_ref[...] += jnp.sum(acc, axis=0, keepdims=True)
                    dsc_ref[...] += jnp.sum(acc * (xhat * gn), axis=0, keepdims=True)
                    dn = acc * (1.0 + nsc_ref[...])
                    dgn_ref[...] += jnp.sum(dn * xhat, axis=0, keepdims=True)
                    dxhat = dn * gn
                    dx = ndx_ref[rows, :] + r * (dxhat - xhat * jnp.mean(dxhat * xhat, axis=-1, keepdims=True))
                    o_ref[rows, :] = dx
                    if nb_gate:
                        dy = dx * ngt_ref[...]
                        dy_ref[rows, :] = dy.astype(BF16)
                        dgate_ref[...] += jnp.sum(dx * ny_ref[rows, :].astype(F32), axis=0, keepdims=True)
                        cs_ref[...] += jnp.sum(dy, axis=0, keepdims=True)
                else:
                    o_ref[rows, cols] = acc.astype(out_dtype)

        if n_pay:
            @pl.when(pl.program_id(0) == steps - 1)
            def _():
                _Plan(bcasts, pay_src, pay_dst, *sems).finish()

    args = [a, w]
    w_block = w.shape
    specs = [pl.BlockSpec((tm, K), lambda i: (i, 0)), pl.BlockSpec(w_block, lambda i: (0,) * len(w_block))]
    if has_bias:
        args.append(bias.reshape(1, N).astype(F32))
        specs.append(pl.BlockSpec((1, N), lambda i: (0, 0)))
    row_spec = pl.BlockSpec((tm, N), lambda i: (i, 0))
    if epi == "resid":
        tpb = rows_per_batch // tm
        vec_spec = pl.BlockSpec((None, 1, N), lambda i: (i // tpb, 0, 0))
        args += [x, gate]
        specs += [row_spec, vec_spec]
        out_shape = [jax.ShapeDtypeStruct((M, N), F32), jax.ShapeDtypeStruct((M, N), BF16)]
        out_specs = [row_spec, row_spec]
        if norm is not None:
            assert n_chunks == 1
            args += list(norm)
            specs += [pl.BlockSpec((1, N), lambda i: (0, 0)), vec_spec, vec_spec]
            out_shape.append(jax.ShapeDtypeStruct((M, N), BF16))
            out_specs.append(row_spec)
    elif epi == "normbwd":
        assert n_chunks == 1
        vec_spec = pl.BlockSpec((None, 1, N), lambda i: (i // tpb_nb, 0, 0))
        one_spec = pl.BlockSpec((1, N), lambda i: (0, 0))
        nbat = M // rows_per_batch
        f_vec, f_one = jax.ShapeDtypeStruct((nbat, 1, N), F32), jax.ShapeDtypeStruct((1, N), F32)
        args += [nb["x"], nb["dx_in"], nb["gnorm"], nb["sc"]]
        specs += [row_spec, row_spec, one_spec, vec_spec]
        out_shape = [jax.ShapeDtypeStruct((M, N), F32), f_vec, f_vec, f_one]
        out_specs = [row_spec, vec_spec, vec_spec, one_spec]
        if nb_gate:
            args += list(nb["gate"])
            specs += [row_spec, vec_spec]
            out_shape += [jax.ShapeDtypeStruct((M, N), BF16), f_vec, f_one]
            out_specs += [row_spec, vec_spec, one_spec]
    else:
        if epi == "relu2d":
            args.append(u)
            specs.append(row_spec)
        out_shape = [jax.ShapeDtypeStruct((M, N), out_dtype)]
        out_specs = [row_spec]
    scratch = []
    if n_pay:
        p_arrays, p_in, p_shapes, p_out, scratch = _payload_layout(payload)
        args += p_arrays
        specs += p_in
        out_shape += p_shapes
        out_specs += p_out
    res = pl.pallas_call(
        body,
        name=name,
        grid=(steps,),
        out_shape=tuple(out_shape),
        in_specs=specs,
        out_specs=tuple(out_specs),
        scratch_shapes=scratch,
        compiler_params=_params("arbitrary" if (n_pay or epi == "normbwd") else "parallel"),
    )(*args)
    if epi == "resid":
        n_own = 3 if norm is not None else 2
    elif epi == "normbwd":
        n_own = 7 if nb_gate else 4
    else:
        n_own = 1
    own = res[0] if n_own == 1 else tuple(res[:n_own])
    return (own, list(res[n_own:])) if n_pay else own


def _mm_tn(name, a, b, *, a_sq=False, out_form="full", colsum=False, tt_pref=1024, tk_pref=1024, tn_pref=1024,
           payload=()):
    T, K = a.shape
    N = b.shape[1]
    tt = _pick(T, tt_pref, 16)
    tk = K // N_DEV if out_form == "kslots" else _pick(K, tk_pref, 128)
    tn = N // N_DEV if out_form == "nslots" else _pick(N, tn_pref, 128)
    nt_steps = T // tt
    grid = (K // tk, N // tn, nt_steps)
    slots = out_form != "full"
    assert not colsum or K == tk
    n_pay = len(payload)
    bcasts = [bc for _, bc in payload]

    def body(*refs):
        it = iter(refs)
        a_ref, b_ref = next(it), next(it)
        pay_src = [next(it) for _ in range(n_pay)]
        o_ref = next(it)
        cs_ref = next(it) if colsum else None
        pay_dst = [next(it) for _ in range(n_pay)]
        acc = next(it) if slots else o_ref
        sems = list(it)
        step = (pl.program_id(0) * grid[1] + pl.program_id(1)) * grid[2] + pl.program_id(2)

        if n_pay:
            @pl.when(step == 0)
            def _():
                _Plan(bcasts, pay_src, pay_dst, *sems).start()

        @pl.when(pl.program_id(2) == 0)
        def _():
            acc[...] = jnp.zeros_like(acc)
            if colsum:
                cs_ref[...] = jnp.zeros_like(cs_ref)

        bv = b_ref[...]
        av = a_ref[...]
        if a_sq:
            av = av * av
        acc[...] += lax.dot_general(av, bv, (((0,), (0,)), ((), ())), preferred_element_type=F32)
        if colsum:
            cs_ref[...] += jnp.sum(bv.astype(F32), axis=0, keepdims=True)

        if slots:
            @pl.when(pl.program_id(2) == nt_steps - 1)
            def _():
                o_ref[...] = acc[...].astype(o_ref.dtype)

        if n_pay:
            @pl.when(step == grid[0] * grid[1] * grid[2] - 1)
            def _():
                plan = _Plan(bcasts, pay_src, pay_dst, *sems)
                plan.pass_on()
                plan.finish()

    if out_form == "full":
        out_shape = [jax.ShapeDtypeStruct((K, N), F32)]
        out_specs = [pl.BlockSpec((tk, tn), lambda k, n, t: (k, n))]
    elif out_form == "nslots":
        out_shape = [jax.ShapeDtypeStruct((N_DEV, K, tn), BF16)]
        out_specs = [pl.BlockSpec((None, tk, tn), lambda k, n, t: (n, k, 0))]
    else:
        out_shape = [jax.ShapeDtypeStruct((N_DEV, tk, N), BF16)]
        out_specs = [pl.BlockSpec((None, tk, tn), lambda k, n, t: (k, 0, n))]
    if colsum:
        out_shape.append(jax.ShapeDtypeStruct((1, N), F32))
        out_specs.append(pl.BlockSpec((1, tn), lambda k, n, t: (0, n)))
    args = [a, b]
    in_specs = [pl.BlockSpec((tt, tk), lambda k, n, t: (t, k)), pl.BlockSpec((tt, tn), lambda k, n, t: (t, n))]
    scratch = [pltpu.VMEM((tk, tn), F32)] if slots else []
    n_own = len(out_shape)
    if n_pay:
        p_arrays, p_in, p_shapes, p_out, p_scratch = _payload_layout(payload)
        args += p_arrays
        in_specs += p_in
        out_shape += p_shapes
        out_specs += p_out
        scratch += p_scratch
    res = pl.pallas_call(
        body,
        name=name,
        grid=grid,
        out_shape=tuple(out_shape),
        in_specs=in_specs,
        out_specs=tuple(out_specs),
        scratch_shapes=scratch,
        compiler_params=_params(*(("arbitrary",) * 3 if n_pay else ("parallel", "parallel", "arbitrary"))),
    )(*args)
    own = res[0] if n_own == 1 else tuple(res[:n_own])
    return (own, list(res[n_own:])) if n_pay else own


def _normmod_fwd(name, x, gnorm, sc, sh, S, tm_pref=512):
    T, D = x.shape
    tm = _pick(S, tm_pref, 16)
    tpb = S // tm

    def body(x_ref, g_ref, sc_ref, sh_ref, o_ref):
        xv = x_ref[...]
        r = lax.rsqrt(jnp.mean(xv * xv, axis=-1, keepdims=True) + EPS)
        n = xv * r * g_ref[...]
        o_ref[...] = (n * (1.0 + sc_ref[...]) + sh_ref[...]).astype(BF16)

    row = pl.BlockSpec((tm, D), lambda i: (i, 0))
    vec = pl.BlockSpec((None, 1, D), lambda i: (i // tpb, 0, 0))
    return pl.pallas_call(
        body,
        name=name,
        grid=(T // tm,),
        out_shape=jax.ShapeDtypeStruct((T, D), BF16),
        in_specs=[row, pl.BlockSpec((1, D), lambda i: (0, 0)), vec, vec],
        out_specs=row,
        compiler_params=_params("parallel"),
    )(x, gnorm, sc, sh)


ATT_ROWS = GROUP * ATT_BLOCK
ATT_SPAN = 2 * ATT_BLOCK
ATT_SCALE = HEAD_DIM ** -0.5


def _attn_table(sinks):
    slopes = jnp.asarray(np.array([2.0 ** (-8.0 * (h + 1) / N_HEADS) for h in range(N_HEADS)], np.float32))
    r = jnp.arange(ATT_BLOCK)[:, None]
    cc = jnp.arange(ATT_SPAN)[None, :]
    dist = r + ATT_BLOCK - cc
    ok = (dist >= 0) & (dist < ATT_BLOCK)
    tab = jnp.where(ok[None], -slopes[:, None, None] * dist.astype(F32)[None], NEG_BIG)
    tab = jnp.where((cc == 0)[None], sinks.astype(F32)[:, None, None], tab)
    return tab.reshape(N_KV_HEADS, ATT_ROWS, ATT_SPAN)


KV_COLS = N_KV_HEADS * HEAD_DIM


PAIR = 2 * HEAD_DIM


def _pair_tables(tab):
    hkv, rows, span = tab.shape
    pairs = rows // (2 * ATT_BLOCK)
    return tab.reshape(hkv, pairs, 2, ATT_BLOCK, span).transpose(0, 2, 1, 3, 4).reshape(hkv, 2, pairs * ATT_BLOCK, span)


def _stack_pairs(ref, pairs):
    return jnp.concatenate([ref[:, j * PAIR:(j + 1) * PAIR] for j in range(pairs)], axis=0)


def _unstack_pairs(v, pairs):
    return jnp.concatenate([v[j * ATT_BLOCK:(j + 1) * ATT_BLOCK, :] for j in range(pairs)], axis=1)


def _swap_halves(x):
    return jnp.concatenate([x[:, HEAD_DIM:], x[:, :HEAD_DIM]], axis=1)


def _load_span(prev_ref, cur_ref, buf_e, buf_o, kv, mult):
    lane = lax.broadcasted_iota(jnp.int32, (ATT_BLOCK, PAIR), 1)
    row = lax.broadcasted_iota(jnp.int32, (ATT_BLOCK, PAIR), 0)
    own = lane // HEAD_DIM == kv
    for r0, ref, first in ((0, prev_ref, True), (ATT_BLOCK, cur_ref, False)):
        v = ref[...]
        if mult != 1.0:
            v = v * mult
        zero = jnp.zeros_like(v)
        mine = jnp.where(own, v, zero)
        if first:
            mine = jnp.where(row > 0, mine, zero)
        other = _swap_halves(mine)
        buf_e[r0:r0 + ATT_BLOCK, :] = jnp.where(kv == 0, mine, other)
        buf_o[r0:r0 + ATT_BLOCK, :] = jnp.where(kv == 0, other, mine)


def _first_block_penalty(i):
    col = lax.broadcasted_iota(jnp.int32, (1, ATT_SPAN), 1)
    return jnp.where((col < ATT_BLOCK) & (col > 0), jnp.where(i > 0, 0.0, NEG_BIG), 0.0).astype(F32)


def _attn_probs(qp, kbuf, bias, first_pen):
    nt = (((1,), (1,)), ((), ()))
    s = lax.dot_general(qp, kbuf, nt, preferred_element_type=F32) + bias + first_pen
    m = jnp.max(s, axis=-1, keepdims=True)
    e = jnp.exp(s - m)
    return e * (1.0 / jnp.sum(e, axis=-1, keepdims=True))


def _attn_specs(nb, pairs, q_map, row_cur, row_prev):
    gw = pairs * PAIR
    kblk = N_KV_HEADS * gw // KV_COLS
    qspec = pl.BlockSpec((ATT_BLOCK, gw), q_map)
    kv_specs = [pl.BlockSpec((ATT_BLOCK, KV_COLS), (lambda b, i, kv, r=r, c=c: (r(b, i), c)))
                for c in (kblk, kblk + 1) for r in (row_cur, row_prev)]
    tspec = pl.BlockSpec((N_KV_HEADS, 2, pairs * ATT_BLOCK, ATT_SPAN), lambda b, i, kv: (0, 0, 0, 0))
    return qspec, kv_specs, tspec


def _attn_fwd(qkv, tab2, B, S):
    assert N_KV_HEADS == 2 and GROUP % 2 == 0
    T = qkv.shape[0]
    pairs = GROUP // 2
    nb = S // ATT_BLOCK
    q_dim = N_HEADS * HEAD_DIM

    def body(q_ref, kc_ref, kp_ref, vc_ref, vp_ref, tab_ref, o_ref, ke, ko, ve, vo):
        i, kv = pl.program_id(1), pl.program_id(2)
        _load_span(kp_ref, kc_ref, ke, ko, kv, ATT_SCALE)
        _load_span(vp_ref, vc_ref, ve, vo, kv, 1.0)
        pen = _first_block_penalty(i)
        qp = _stack_pairs(q_ref, pairs)
        p_e = _attn_probs(qp, ke[...], tab_ref[kv, 0], pen)
        p_o = _attn_probs(qp, ko[...], tab_ref[kv, 1], pen)
        o = jnp.dot(p_e.astype(BF16), ve[...], preferred_element_type=F32)
        o = o + jnp.dot(p_o.astype(BF16), vo[...], preferred_element_type=F32)
        o_ref[...] = _unstack_pairs(o.astype(BF16), pairs)

    qspec, kv_specs, tspec = _attn_specs(nb, pairs, lambda b, i, kv: (b * nb + i, kv), lambda b, i: b * nb + i,
                                         lambda b, i: b * nb + jnp.maximum(i - 1, 0))
    return pl.pallas_call(
        body,
        name="attn_fwd",
        grid=(B, nb, N_KV_HEADS),
        out_shape=jax.ShapeDtypeStruct((T, q_dim), BF16),
        in_specs=[qspec] + kv_specs + [tspec],
        out_specs=qspec,
        scratch_shapes=[pltpu.VMEM((ATT_SPAN, PAIR), BF16)] * 4,
        compiler_params=_params("parallel", "parallel", "arbitrary"),
    )(qkv, qkv, qkv, qkv, qkv, tab2)


def _attn_bwd(qkv, o, do, tab2, B, S):
    T = qkv.shape[0]
    hd = HEAD_DIM
    pairs = GROUP // 2
    rows = pairs * ATT_BLOCK
    nb = S // ATT_BLOCK
    last = nb - 1
    q_dim = N_HEADS * hd
    tn = (((0,), (0,)), ((), ()))
    nt = (((1,), (1,)), ((), ()))

    def body(q_ref, kc_ref, kp_ref, vc_ref, vp_ref, tab_ref, o_ref, do_ref,
             dq_ref, dkv_ref, dsink_ref, carry_k, carry_v, ke, ko, ve, vo):
        i, kv = pl.program_id(1), pl.program_id(2)

        @pl.when((i == 0) & (kv == 0))
        def _():
            dsink_ref[...] = jnp.zeros_like(dsink_ref)

        def emit(dk_rows, dv_rows):
            for h in range(N_KV_HEADS):
                @pl.when(kv == h)
                def _():
                    dkv_ref[:, h * hd:(h + 1) * hd] = dk_rows.astype(BF16)
                    dkv_ref[:, KV_COLS + h * hd:KV_COLS + (h + 1) * hd] = dv_rows.astype(BF16)

        @pl.when(i < nb)
        def _():
            _load_span(kp_ref, kc_ref, ke, ko, kv, ATT_SCALE)
            _load_span(vp_ref, vc_ref, ve, vo, kv, 1.0)
            pen = _first_block_penalty(i)
            qp = _stack_pairs(q_ref, pairs)
            dop = _stack_pairs(do_ref, pairs)
            op = _stack_pairs(o_ref, pairs)
            p_e = _attn_probs(qp, ke[...], tab_ref[kv, 0], pen)
            p_o = _attn_probs(qp, ko[...], tab_ref[kv, 1], pen)
            prod = dop.astype(F32) * op.astype(F32)
            low = lax.broadcasted_iota(jnp.int32, (rows, PAIR), 1) < hd
            d_e = jnp.sum(jnp.where(low, prod, 0.0), axis=-1, keepdims=True)
            d_o = jnp.sum(prod, axis=-1, keepdims=True) - d_e
            ds_e = (p_e * (lax.dot_general(dop, ve[...], nt, preferred_element_type=F32) - d_e)).astype(BF16)
            ds_o = (p_o * (lax.dot_general(dop, vo[...], nt, preferred_element_type=F32) - d_o)).astype(BF16)
            dq = jnp.dot(ds_e, ke[...], preferred_element_type=F32) + jnp.dot(ds_o, ko[...], preferred_element_type=F32)
            dq_ref[...] = _unstack_pairs(dq.astype(BF16), pairs)
            hg = lax.broadcasted_iota(jnp.int32, (pairs, rows), 0)
            hr = lax.broadcasted_iota(jnp.int32, (pairs, rows), 1)
            head_of = jnp.where(hr // ATT_BLOCK == hg, 1.0, 0.0).astype(BF16)
            dsink_ref[kv, 0] += jnp.dot(head_of, ds_e, preferred_element_type=F32)[:, 0:128]
            dsink_ref[kv, 1] += jnp.dot(head_of, ds_o, preferred_element_type=F32)[:, 0:128]
            low_k = lax.broadcasted_iota(jnp.int32, (ATT_SPAN, PAIR), 1) < hd
            dk2 = jnp.where(low_k, lax.dot_general(ds_e, qp, tn, preferred_element_type=F32),
                            lax.dot_general(ds_o, qp, tn, preferred_element_type=F32))
            dv2 = jnp.where(low_k, lax.dot_general(p_e.astype(BF16), dop, tn, preferred_element_type=F32),
                            lax.dot_general(p_o.astype(BF16), dop, tn, preferred_element_type=F32))
            keep = lax.broadcasted_iota(jnp.int32, (ATT_SPAN, hd), 0) > 0
            dk = jnp.where(keep, (dk2[:, :hd] + dk2[:, hd:]) * ATT_SCALE, 0.0)
            dv = jnp.where(keep, dv2[:, :hd] + dv2[:, hd:], 0.0)

            @pl.when(i > 0)
            def _():
                emit(carry_k[kv] + dk[0:ATT_BLOCK], carry_v[kv] + dv[0:ATT_BLOCK])

            carry_k[kv] = dk[ATT_BLOCK:]
            carry_v[kv] = dv[ATT_BLOCK:]

        @pl.when(i == nb)
        def _():
            emit(carry_k[kv], carry_v[kv])

    def q_map(b, i, kv):
        return (b * nb + jnp.minimum(i, last), jnp.where(i == nb, N_KV_HEADS - 1, kv))

    qspec, kv_specs, tspec = _attn_specs(nb, pairs, q_map, lambda b, i: b * nb + jnp.minimum(i, last),
                                         lambda b, i: b * nb + jnp.clip(i - 1, 0, last))
    dkv = pl.BlockSpec((ATT_BLOCK, 2 * KV_COLS), lambda b, i, kv: (b * nb + jnp.maximum(i - 1, 0), 0))
    dsk = pl.BlockSpec((None, N_KV_HEADS, 2, pairs, 128), lambda b, i, kv: (b, 0, 0, 0, 0))
    return pl.pallas_call(
        body,
        name="attn_bwd",
        grid=(B, nb + 1, N_KV_HEADS),
        out_shape=(
            jax.ShapeDtypeStruct((T, q_dim), BF16),
            jax.ShapeDtypeStruct((T, 2 * KV_COLS), BF16),
            jax.ShapeDtypeStruct((B, N_KV_HEADS, 2, pairs, 128), F32),
        ),
        in_specs=[qspec] + kv_specs + [tspec, qspec, qspec],
        out_specs=(qspec, dkv, dsk),
        scratch_shapes=[pltpu.VMEM((N_KV_HEADS, ATT_BLOCK, hd), F32), pltpu.VMEM((N_KV_HEADS, ATT_BLOCK, hd), F32)]
        + [pltpu.VMEM((ATT_SPAN, PAIR), BF16)] * 4,
        compiler_params=_params("arbitrary", "arbitrary", "arbitrary"),
    )(qkv, qkv, qkv, qkv, qkv, tab2, o, do)


def _conv_tiles(S):
    ts = _pick(S, 256, CONV_HALO)
    return ts, S // ts


def _conv_chunks(C, ts):
    lane = _pick(C, 128, 128)
    return lane, C // lane, _pick(ts, 64, 8)


def _conv_weight_chunks(w_dw, C):
    lane = _pick(C, 128, 128)
    w = jnp.pad(w_dw, ((0, CONV_HALO - CONV_WIDTH), (0, 0)))
    return w.reshape(CONV_HALO, C // lane, lane).transpose(1, 0, 2)


def _conv_fwd(u, w3, b_dw, ln_g, ln_b, S):
    T, C2 = u.shape
    C = C2 // 2
    B = T // S
    ts, nj = _conv_tiles(S)
    hb = ts // CONV_HALO
    lane, nc, rc = _conv_chunks(C, ts)

    def body(a_ref, g_ref, ap_ref, gp_ref, w_ref, bdw_ref, lg_ref, lb_ref, cv_ref, s_ref, buf, cvb):
        j = pl.program_id(1)
        glu_prev = ap_ref[...] * _sigmoid(gp_ref[...]) * (j > 0).astype(F32)
        glu = a_ref[...] * _sigmoid(g_ref[...])
        for cc in range(nc):
            buf[cc, 0:CONV_HALO, :] = glu_prev[:, cc * lane:(cc + 1) * lane]
            buf[cc, CONV_HALO:, :] = glu[:, cc * lane:(cc + 1) * lane]

        def chunk(cc, carry):
            for r0 in range(0, ts, rc):
                acc = jnp.zeros((rc, lane), F32)
                for kk in range(CONV_WIDTH):
                    lo = CONV_HALO - (CONV_WIDTH - 1 - kk) + r0
                    acc = acc + w_ref[cc, kk:kk + 1, :] * buf[cc, lo:lo + rc, :]
                cvb[cc, r0:r0 + rc, :] = acc
            return carry

        lax.fori_loop(0, nc, chunk, 0)
        for cc in range(nc):
            cv_ref[:, cc * lane:(cc + 1) * lane] = cvb[cc] + bdw_ref[:, cc * lane:(cc + 1) * lane]
        cv = cv_ref[...]
        mu = jnp.mean(cv, axis=-1, keepdims=True)
        xc = cv - mu
        rstd = lax.rsqrt(jnp.mean(xc * xc, axis=-1, keepdims=True) + EPS)
        ln = xc * rstd * lg_ref[...] + lb_ref[...]
        s_ref[...] = (ln * _sigmoid(ln)).astype(BF16)

    a_cur = pl.BlockSpec((ts, C), lambda b, j: (b * nj + j, 0))
    g_cur = pl.BlockSpec((ts, C), lambda b, j: (b * nj + j, 1))
    a_prev = pl.BlockSpec((CONV_HALO, C), lambda b, j: (jnp.maximum((b * nj + j) * hb - 1, 0), 0))
    g_prev = pl.BlockSpec((CONV_HALO, C), lambda b, j: (jnp.maximum((b * nj + j) * hb - 1, 0), 1))
    wspec = pl.BlockSpec((nc, CONV_HALO, lane), lambda b, j: (0, 0, 0))
    one = pl.BlockSpec((1, C), lambda b, j: (0, 0))
    return pl.pallas_call(
        body,
        name="conv_fwd",
        grid=(B, nj),
        out_shape=(jax.ShapeDtypeStruct((T, C), F32), jax.ShapeDtypeStruct((T, C), BF16)),
        in_specs=[a_cur, g_cur, a_prev, g_prev, wspec, one, one, one],
        out_specs=(a_cur, a_cur),
        scratch_shapes=[pltpu.VMEM((nc, CONV_HALO + ts, lane), F32), pltpu.VMEM((nc, ts, lane), F32)],
        compiler_params=_params("parallel", "arbitrary"),
    )(u, u, u, u, w3, b_dw, ln_g, ln_b)


def _lnsilu_bwd(ds, cv, ln_g, ln_b, tm_pref=256):
    T, C = cv.shape
    tm = _pick(T, tm_pref, 16)

    def body(ds_ref, cv_ref, lg_ref, lb_ref, dcv_ref, dlg_ref, dlb_ref, dbdw_ref):
        @pl.when(pl.program_id(0) == 0)
        def _():
            dlg_ref[...] = jnp.zeros_like(dlg_ref)
            dlb_ref[...] = jnp.zeros_like(dlb_ref)
            dbdw_ref[...] = jnp.zeros_like(dbdw_ref)

        cv_v = cv_ref[...]
        g = lg_ref[...]
        mu = jnp.mean(cv_v, axis=-1, keepdims=True)
        xc = cv_v - mu
        rstd = lax.rsqrt(jnp.mean(xc * xc, axis=-1, keepdims=True) + EPS)
        xhat = xc * rstd
        ln = xhat * g + lb_ref[...]
        sg = _sigmoid(ln)
        dln = ds_ref[...] * (sg * (1.0 + ln * (1.0 - sg)))
        dlg_ref[...] += jnp.sum(dln * xhat, axis=0, keepdims=True)
        dlb_ref[...] += jnp.sum(dln, axis=0, keepdims=True)
        dxhat = dln * g
        dcv = rstd * (dxhat - jnp.mean(dxhat, axis=-1, keepdims=True)
                      - xhat * jnp.mean(dxhat * xhat, axis=-1, keepdims=True))
        dcv_ref[...] = dcv
        dbdw_ref[...] += jnp.sum(dcv, axis=0, keepdims=True)

    row = pl.BlockSpec((tm, C), lambda i: (i, 0))
    one = pl.BlockSpec((1, C), lambda i: (0, 0))
    return pl.pallas_call(
        body,
        name="lnsilu_bwd",
        grid=(T // tm,),
        out_shape=(jax.ShapeDtypeStruct((T, C), F32),) + (jax.ShapeDtypeStruct((1, C), F32),) * 3,
        in_specs=[row, row, one, one],
        out_specs=(row, one, one, one),
        compiler_params=_params("arbitrary"),
    )(ds, cv, ln_g, ln_b)


def _conv_bwd(dcv, u, w3, S):
    T, C2 = u.shape
    C = C2 // 2
    B = T // S
    ts, nj = _conv_tiles(S)
    hb = ts // CONV_HALO
    n_halo_blocks = T // CONV_HALO
    lane, nc, rc = _conv_chunks(C, ts)

    def body(dcv_ref, dnx_ref, a_ref, g_ref, ap_ref, gp_ref, w_ref, du_ref, dw_ref, gbuf, dbuf, dglu, dw8):
        b, j = pl.program_id(0), pl.program_id(1)

        @pl.when((b == 0) & (j == 0))
        def _():
            dw8[...] = jnp.zeros_like(dw8)

        a = a_ref[...]
        sg = _sigmoid(g_ref[...])
        glu_prev = ap_ref[...] * _sigmoid(gp_ref[...]) * (j > 0).astype(F32)
        glu = a * sg
        dcur = dcv_ref[...]
        dnext = dnx_ref[...] * (j < nj - 1).astype(F32)
        for cc in range(nc):
            cols = slice(cc * lane, (cc + 1) * lane)
            gbuf[cc, 0:CONV_HALO, :] = glu_prev[:, cols]
            gbuf[cc, CONV_HALO:, :] = glu[:, cols]
            dbuf[cc, 0:ts, :] = dcur[:, cols]
            dbuf[cc, ts:, :] = dnext[:, cols]

        def chunk(cc, carry):
            for r0 in range(0, ts, rc):
                acc = jnp.zeros((rc, lane), F32)
                for kk in range(CONV_WIDTH):
                    d = CONV_WIDTH - 1 - kk
                    acc = acc + w_ref[cc, kk:kk + 1, :] * dbuf[cc, r0 + d:r0 + d + rc, :]
                dglu[cc, r0:r0 + rc, :] = acc
            for kk in range(CONV_WIDTH):
                d = CONV_WIDTH - 1 - kk
                p = jnp.zeros((rc, lane), F32)
                for r0 in range(0, ts, rc):
                    lo = CONV_HALO - d + r0
                    p = p + dbuf[cc, r0:r0 + rc, :] * gbuf[cc, lo:lo + rc, :]
                dw8[cc, kk * 8:(kk + 1) * 8, :] += jnp.sum(p.reshape(rc // 8, 8, lane), axis=0)
            return carry

        lax.fori_loop(0, nc, chunk, 0)
        for cc in range(nc):
            cols = slice(cc * lane, (cc + 1) * lane)
            dgl = dglu[cc]
            du_ref[:, cc * lane:(cc + 1) * lane] = (dgl * sg[:, cols]).astype(BF16)
            du_ref[:, C + cc * lane:C + (cc + 1) * lane] = (dgl * a[:, cols] * sg[:, cols] * (1.0 - sg[:, cols])).astype(BF16)

        @pl.when((b == B - 1) & (j == nj - 1))
        def _():
            dw_ref[...] = jnp.zeros_like(dw_ref)
            for kk in range(CONV_WIDTH):
                dw_ref[:, kk:kk + 1, :] = jnp.sum(dw8[:, kk * 8:(kk + 1) * 8, :], axis=1, keepdims=True)

    a_cur = pl.BlockSpec((ts, C), lambda b, j: (b * nj + j, 0))
    g_cur = pl.BlockSpec((ts, C), lambda b, j: (b * nj + j, 1))
    a_prev = pl.BlockSpec((CONV_HALO, C), lambda b, j: (jnp.maximum((b * nj + j) * hb - 1, 0), 0))
    g_prev = pl.BlockSpec((CONV_HALO, C), lambda b, j: (jnp.maximum((b * nj + j) * hb - 1, 0), 1))
    d_next = pl.BlockSpec((CONV_HALO, C), lambda b, j: (jnp.minimum((b * nj + j + 1) * hb, n_halo_blocks - 1), 0))
    wspec = pl.BlockSpec((nc, CONV_HALO, lane), lambda b, j: (0, 0, 0))
    return pl.pallas_call(
        body,
        name="conv_bwd",
        grid=(B, nj),
        out_shape=(jax.ShapeDtypeStruct((T, C2), BF16), jax.ShapeDtypeStruct((nc, CONV_HALO, lane), F32)),
        in_specs=[a_cur, d_next, a_cur, g_cur, a_prev, g_prev, wspec],
        out_specs=(pl.BlockSpec((ts, C2), lambda b, j: (b * nj + j, 0)), wspec),
        scratch_shapes=[
            pltpu.VMEM((nc, CONV_HALO + ts, lane), F32),
            pltpu.VMEM((nc, ts + CONV_HALO, lane), F32),
            pltpu.VMEM((nc, ts, lane), F32),
            pltpu.VMEM((nc, CONV_HALO * 8, lane), F32),
        ],
        compiler_params=_params("arbitrary", "arbitrary"),
    )(dcv, dcv, u, u, u, u, w3)


def _loss_head(x, tgt, gfin, y, gate, S, tm_pref=256):
    T, D = x.shape
    tm = _pick(S, tm_pref, 16)
    tpb = S // tm

    def body(x_ref, t_ref, g_ref, y_ref, gt_ref, dx_ref, loss_ref, dg_ref, dy_ref, dgate_ref):
        @pl.when(pl.program_id(0) == 0)
        def _():
            loss_ref[...] = jnp.zeros_like(loss_ref)
            dg_ref[...] = jnp.zeros_like(dg_ref)

        @pl.when(pl.program_id(0) % tpb == 0)
        def _():
            dgate_ref[...] = jnp.zeros_like(dgate_ref)

        xv = x_ref[...]
        g = g_ref[...]
        r = lax.rsqrt(jnp.mean(xv * xv, axis=-1, keepdims=True) + EPS)
        xhat = xv * r
        e = xhat * g - t_ref[...]
        row_loss = jnp.mean(e * e, axis=-1, keepdims=True)
        loss_ref[...] += 0.5 * jnp.sum(row_loss, axis=0, keepdims=True)
        dy = e * (1.0 / D)
        dg_ref[...] += jnp.sum(dy * xhat, axis=0, keepdims=True)
        dxhat = dy * g
        dx = r * (dxhat - xhat * jnp.mean(dxhat * xhat, axis=-1, keepdims=True))
        dx_ref[...] = dx
        dy_ref[...] = (dx * gt_ref[...]).astype(BF16)
        dgate_ref[...] += jnp.sum(dx * y_ref[...].astype(F32), axis=0, keepdims=True)

    row = pl.BlockSpec((tm, D), lambda i: (i, 0))
    vec = pl.BlockSpec((None, 1, D), lambda i: (i // tpb, 0, 0))
    one = pl.BlockSpec((1, D), lambda i: (0, 0))
    return pl.pallas_call(
        body,
        name="loss_head",
        grid=(T // tm,),
        out_shape=(
            jax.ShapeDtypeStruct((T, D), F32),
            jax.ShapeDtypeStruct((8, 128), F32),
            jax.ShapeDtypeStruct((1, D), F32),
            jax.ShapeDtypeStruct((T, D), BF16),
            jax.ShapeDtypeStruct((T // S, 1, D), F32),
        ),
        in_specs=[row, row, one, row, vec],
        out_specs=(row, pl.BlockSpec((8, 128), lambda i: (0, 0)), one, row, vec),
        compiler_params=_params("arbitrary"),
    )(x, tgt, gfin, y, gate)


def _adam(name, parts, w, m, v, tm_pref=256):
    L = len(parts)
    P, R, C = parts[0].shape
    tm = _pick(R, tm_pref, 16)
    tiles = R // tm
    c1 = 1.0 - ADAM_B1 ** ADAM_STEP
    c2 = 1.0 - ADAM_B2 ** ADAM_STEP

    def body(*refs):
        p_refs = refs[:L]
        w_ref, m_ref, v_ref, g_ref, d_ref, mo_ref, vo_ref = refs[L:]
        for l in range(L):
            @pl.when(pl.program_id(0) == l)
            def _():
                g = p_refs[l][0].astype(F32)
                for i in range(1, P):
                    g = g + p_refs[l][i].astype(F32)
                m_new = ADAM_B1 * m_ref[...] + (1.0 - ADAM_B1) * g
                v_new = ADAM_B2 * v_ref[...] + (1.0 - ADAM_B2) * (g * g)
                m_hat = m_new / c1
                v_hat = v_new / c2
                g_ref[...] = g
                d_ref[...] = -ADAM_LR * (m_hat / (jnp.sqrt(v_hat) + ADAM_EPS) + ADAM_WD * w_ref[...])
                mo_ref[...] = m_new
                vo_ref[...] = v_new

    def p_spec(l):
        return pl.BlockSpec((P, tm, C), lambda li, t: (0, jnp.where(li == l, t, jnp.where(li < l, 0, tiles - 1)), 0))

    row = pl.BlockSpec((tm, C), lambda li, t: (li * tiles + t, 0))
    return pl.pallas_call(
        body,
        name=name,
        grid=(L, tiles),
        out_shape=(jax.ShapeDtypeStruct((L * R, C), F32),) * 4,
        in_specs=[p_spec(l) for l in range(L)] + [row, row, row],
        out_specs=(row, row, row, row),
        compiler_params=_params("arbitrary", "arbitrary"),
    )(*parts, w, m, v)


BIG = ("w_qkv", "w_o", "w_pw1", "w_pw2", "w_up", "w_down")
SMALL_SHARDED = (("b_pw1", 1), ("w_dw", 2), ("b_dw", 1), ("conv_ln_g", 1), ("conv_ln_b", 1), ("b_pw2", 1))
SMALL_REPL = ("b_mod", "norm_mix", "norm_mlp", "b_qkv", "b_o", "sinks", "final_norm")
WEIGHTS = ("w_mod", "b_mod", "norm_mix", "norm_mlp", "w_qkv", "b_qkv", "w_o", "b_o", "sinks", "w_pw1", "b_pw1",
           "w_dw", "b_dw", "conv_ln_g", "conv_ln_b", "w_pw2", "b_pw2", "w_up", "w_down", "final_norm")


def _step(x, c, loss_target, W, M, V):
    B, S, D = x.shape
    T = B * S
    L = W["w_mod"].shape[0]
    n_mod = W["w_mod"].shape[2]
    me = 4 * lax.axis_index("x") + 2 * lax.axis_index("y") + lax.axis_index("c")
    q_dim = N_HEADS * HEAD_DIM
    kv_dim = N_KV_HEADS * HEAD_DIM

    Wb = {n: W[n].astype(BF16) for n in BIG}

    def shards(i):
        jm = i // 2
        first, last = ("w_qkv", "w_o") if i % 2 == 0 else ("w_pw1", "w_pw2")
        return [Wb[first][jm], Wb[last][jm], Wb["w_up"][i], Wb["w_down"][i]]

    def carried(res, payload):
        return res if payload else (res, [])

    small_names = [n for n, _ in SMALL_SHARDED]
    small_src, small_sizes = _pack([c] + [W[n] for n in small_names], F32, 0)
    got = _exchange("gather_first", [(small_src, True)] + [(s, True) for s in shards(0)])
    small_parts = _unpack(got[0], small_sizes, [c.shape] + [W[n].shape for n in small_names], 1)
    c_all = small_parts[0].reshape(N_DEV * B, D)
    full = {n: _from_slots(p, d) for (n, d), p in zip(SMALL_SHARDED, small_parts[1:])}
    gathered = [got[1:]] + [None] * (L - 1)

    b_mod_mine = lax.dynamic_slice_in_dim(W["b_mod"], me * n_mod, n_mod, axis=1).reshape(L, 1, n_mod)
    mod_part = _mod_fwd(c_all, W["w_mod"].astype(BF16), b_mod_mine)
    mod_slots = mod_part.reshape(L, N_DEV, B, n_mod).transpose(1, 0, 2, 3).reshape(N_DEV, L * B, n_mod)
    mod_recv = _exchange("scatter_mod", [(mod_slots, False)])[0]
    mod = mod_recv.reshape(N_DEV, L, B, n_mod).transpose(1, 2, 0, 3).reshape(L, B, N_MOD, 1, D)

    w_dw3 = [_conv_weight_chunks(full["w_dw"][j], D) for j in range(full["w_dw"].shape[0])]

    xc = x.reshape(T, D)
    saved = []
    h1 = _normmod_fwd("normmod_mix_fwd", xc, W["norm_mix"][0][None], mod[0, :, 1], mod[0, :, 0], S)
    for i in range(L):
        jm = i // 2
        sh1, sc1, g1, sh2, sc2, g2 = (mod[i, :, t] for t in range(N_MOD))
        nxt = [[(s, True)] for s in shards(i + 1)] if i + 1 < L else [[]] * 4
        g_first, g_last, g_up, g_down = gathered[i]
        w_last = g_last.reshape(-1, D)
        w_down_full = g_down.reshape(-1, D)
        sv = {"x_in": xc, "w_last": w_last, "g_up": g_up, "w_down": w_down_full}
        sv["h1"] = h1
        if i % 2 == 0:
            w_qkv_full = _from_slots(g_first, 1)
            sv["w_first"] = w_qkv_full
            qkv, n_first = carried(_mm("qkv_fwd", h1, w_qkv_full, bias=W["b_qkv"][jm], payload=nxt[0]), nxt[0])
            tab = _pair_tables(_attn_table(W["sinks"][jm]))
            mix_in = _attn_fwd(qkv, tab, B, S)
            sv.update(qkv=qkv, tab=tab)
            b_out = W["b_o"][jm]
        else:
            sv["w_first"] = g_first
            u, n_first = carried(_mm("pw1_fwd", h1, g_first, w_form="nslots", bias=full["b_pw1"][jm], out_dtype=F32,
                                     payload=nxt[0]), nxt[0])
            cv, mix_in = _conv_fwd(u, w_dw3[jm], full["b_dw"][jm][None], full["conv_ln_g"][jm][None],
                                   full["conv_ln_b"][jm][None], S)
            sv.update(u=u, cv=cv)
            b_out = full["b_pw2"][jm]
        sv["mix_in"] = mix_in
        (x1, y1, h2), n_last = carried(_mm("mix_out_fwd", mix_in, w_last, bias=b_out, epi="resid", x=xc, gate=g1,
                                           norm=(W["norm_mlp"][i][None], sc2, sh2), rows_per_batch=S,
                                           payload=nxt[1]), nxt[1])
        sv.update(y1=y1, x1=x1)
        up, n_up = carried(_mm("mlp_up_fwd", h2, g_up, w_form="nslots", epi="relu", payload=nxt[2]), nxt[2])
        norm_next = (W["norm_mix"][i + 1][None], mod[i + 1, :, 1], mod[i + 1, :, 0]) if i + 1 < L else None
        res, n_down = carried(_mm("mlp_down_fwd", up, w_down_full, a_sq=True, epi="resid", x=x1, gate=g2,
                                  norm=norm_next, rows_per_batch=S, payload=nxt[3]), nxt[3])
        x2, y2 = res[0], res[1]
        h1 = res[2] if i + 1 < L else None
        sv.update(h2=h2, up=up, y2=y2)
        saved.append(sv)
        xc = x2
        if i + 1 < L:
            gathered[i + 1] = [n_first[0], n_last[0], n_up[0], n_down[0]]

    dx, loss_blk, dgfin, dy2, dg2 = _loss_head(xc, loss_target.reshape(T, D), W["final_norm"][None],
                                               saved[L - 1]["y2"], mod[L - 1, :, 5], S)
    loss = lax.psum(loss_blk[0, 0], ("x", "y", "c"))

    G = {"final_norm": dgfin.reshape(D)}
    dmod_layers = [None] * L
    small_grads = ("norm_mix", "norm_mlp", "b_qkv", "b_o", "sinks", "b_pw1", "w_dw", "b_dw", "conv_ln_g", "conv_ln_b",
                   "b_pw2")
    acc = {n: [None] * W[n].shape[0] for n in small_grads}
    reduced = {n: [None] * W[n].shape[0] for n in BIG}

    waiting_up = None
    for i in reversed(range(L)):
        jm = i // 2
        sv = saved[i]
        sh1, sc1, g1, sh2, sc2, g2 = (mod[i, :, t] for t in range(N_MOD))
        pay = [(waiting_up[1], False)] if waiting_up else []
        gw_down, r = carried(_mm_tn("w_down_grad", sv["up"], dy2, a_sq=True, out_form="kslots", tt_pref=4096,
                                    payload=pay), pay)
        if waiting_up:
            reduced["w_up"][waiting_up[0]] = r[0]
        du = _mm("mlp_down_bwd", dy2, sv["w_down"], w_form="full_t", epi="relu2d", u=sv["up"])
        gw_up, r = _mm_tn("w_up_grad", sv["h2"], du, out_form="nslots", tt_pref=4096, payload=[(gw_down, False)])
        reduced["w_down"][i] = r[0]
        waiting_up = (i, gw_up) if i > 0 else None
        pay = [] if i > 0 else [(gw_up, False)]
        (dx, dsh2, dsc2, dgn, dy1, dg1, dy1_sum), r = carried(_mm(
            "mlp_up_bwd", du, sv["g_up"], w_form="kslots_t", epi="normbwd", rows_per_batch=S,
            nb=dict(x=sv["x1"], dx_in=dx, gnorm=W["norm_mlp"][i][None], sc=sc2, gate=(sv["y1"], g1)),
            payload=pay), pay)
        if i == 0:
            reduced["w_up"][0] = r[0]
        acc["norm_mlp"][i] = dgn.reshape(D)
        gw_last = _mm_tn("mix_out_grad", sv["mix_in"], dy1)
        gw_last = gw_last.reshape((N_DEV, -1) + gw_last.shape[1:]).astype(BF16)
        below = (saved[i - 1]["y2"], mod[i - 1, :, 5]) if i > 0 else None
        nb_mix = dict(x=sv["x_in"], dx_in=dx, gnorm=W["norm_mix"][i][None], sc=sc1, gate=below)
        if i % 2 == 0:
            acc["b_o"][jm] = dy1_sum.reshape(D)
            do, r = _mm("attn_out_bwd", dy1, sv["w_last"], w_form="full_t", payload=[(gw_last, False)])
            reduced["w_o"][jm] = r[0]
            dq, dkv, dsk = _attn_bwd(sv["qkv"], sv["mix_in"], do, sv["tab"], B, S)
            acc["sinks"][jm] = jnp.sum(dsk[..., 0], axis=0).transpose(0, 2, 1).reshape(N_HEADS)
            dqkv = jnp.concatenate([dq, dkv], axis=1)
            gw_qkv, db_qkv = _mm_tn("w_qkv_grad", sv["h1"], dqkv, colsum=True)
            acc["b_qkv"][jm] = db_qkv.reshape(-1)
            gw_qkv = _to_slots(gw_qkv, 1).astype(BF16)
            res, r = _mm("qkv_bwd", dqkv, sv["w_first"], w_form="full_t", epi="normbwd", nb=nb_mix, rows_per_batch=S,
                         payload=[(gw_qkv, False)])
            reduced["w_qkv"][jm] = r[0]
        else:
            acc["b_pw2"][jm] = dy1_sum.reshape(D)
            ds, r = _mm("pw2_bwd", dy1, sv["w_last"], w_form="full_t", out_dtype=F32, payload=[(gw_last, False)])
            reduced["w_pw2"][jm] = r[0]
            dcv, dlg, dlb, dbdw = _lnsilu_bwd(ds, sv["cv"], full["conv_ln_g"][jm][None], full["conv_ln_b"][jm][None])
            acc["conv_ln_g"][jm], acc["conv_ln_b"][jm], acc["b_dw"][jm] = dlg.reshape(-1), dlb.reshape(-1), dbdw.reshape(-1)
            du1, dwdw = _conv_bwd(dcv, sv["u"], w_dw3[jm], S)
            acc["w_dw"][jm] = dwdw.transpose(1, 0, 2).reshape(CONV_HALO, D)[:CONV_WIDTH]
            gw_pw1, db_pw1 = _mm_tn("w_pw1_grad", sv["h1"], du1, out_form="nslots", colsum=True, tt_pref=4096)
            acc["b_pw1"][jm] = db_pw1.reshape(-1)
            res, r = _mm("pw1_bwd", du1, sv["w_first"], w_form="kslots_t", epi="normbwd", nb=nb_mix, rows_per_batch=S,
                         payload=[(gw_pw1, False)])
            reduced["w_pw1"][jm] = r[0]
        dx, dsh1, dsc1, dgn = res[:4]
        acc["norm_mix"][i] = dgn.reshape(D)
        dmod_layers[i] = jnp.concatenate([dsh1, dsc1, dg1, dsh2, dsc2, dg2], axis=1).reshape(B, N_MOD * D)
        if i > 0:
            dy2, dg2 = res[4], res[5]
    grad_x = dx.reshape(B, S, D)
    for n, parts in acc.items():
        G[n] = jnp.stack(parts)

    dmod = jnp.stack(dmod_layers)
    dmod_slots = dmod.reshape(L, B, N_DEV, n_mod).transpose(2, 0, 1, 3).reshape(N_DEV, L * B, n_mod)
    dmod_recv = _exchange("gather_dmod", [(dmod_slots, False)])[0]
    dmod_all = dmod_recv.reshape(N_DEV, L, B, n_mod).transpose(1, 0, 2, 3).reshape(L, N_DEV * B, n_mod)
    g_w_mod, db_mod_mine = _mod_bwd(c_all, dmod_all)
    G["b_mod"] = lax.dynamic_update_slice_in_dim(jnp.zeros_like(W["b_mod"]), db_mod_mine.reshape(L, n_mod),
                                                 me * n_mod, axis=1)

    small_items = [jnp.broadcast_to(G[n][None], (N_DEV,) + G[n].shape) for n in SMALL_REPL]
    small_items += [_to_slots(G[n], d) for n, d in SMALL_SHARDED]
    small_slots, small_sizes2 = _pack(small_items, F32, 1)
    small_recv = _exchange("reduce_small", [(small_slots, False)])[0]

    out = {}

    def run_adam(name, parts, names):
        shapes = [W[n].shape for n in names]
        wp, sizes = _pack([W[n] for n in names], F32, 0)
        mp, _ = _pack([M[n] for n in names], F32, 0)
        vp, _ = _pack([V[n] for n in names], F32, 0)
        res = _adam(name, [parts], wp, mp, vp)
        for kind, buf in zip(("grad", "delta", "new_m", "new_v"), res):
            for n, a in zip(names, _unpack(buf, sizes, shapes, 0)):
                out[kind + "_" + n] = a

    for n in BIG + ("w_mod",):
        cols = W[n].shape[-1]
        if n == "w_mod":
            parts = [g_w_mod.reshape(1, -1, cols)]
        else:
            parts = [r.reshape(N_DEV, -1, cols) for r in reduced[n]]
        res = _adam("adam_" + n, parts, W[n].reshape(-1, cols), M[n].reshape(-1, cols), V[n].reshape(-1, cols))
        for kind, buf in zip(("grad", "delta", "new_m", "new_v"), res):
            out[kind + "_" + n] = buf.reshape(W[n].shape)
    run_adam("adam_small", small_recv, list(SMALL_REPL) + [n for n, _ in SMALL_SHARDED])

    res = [loss, grad_x]
    for kind in ("grad", "delta", "new_m", "new_v"):
        res += [out[kind + "_" + n] for n in WEIGHTS]
    return tuple(res)


def kernel(x, c, w_mod, b_mod, norm_mix, norm_mlp, w_qkv, b_qkv, w_o, b_o, sinks, w_pw1, b_pw1, w_dw, b_dw, conv_ln_g, conv_ln_b, w_pw2, b_pw2, w_up, w_down, final_norm, loss_target, m_w_mod, m_b_mod, m_norm_mix, m_norm_mlp, m_w_qkv, m_b_qkv, m_w_o, m_b_o, m_sinks, m_w_pw1, m_b_pw1, m_w_dw, m_b_dw, m_conv_ln_g, m_conv_ln_b, m_w_pw2, m_b_pw2, m_w_up, m_w_down, m_final_norm, v_w_mod, v_b_mod, v_norm_mix, v_norm_mlp, v_w_qkv, v_b_qkv, v_w_o, v_b_o, v_sinks, v_w_pw1, v_b_pw1, v_w_dw, v_b_dw, v_conv_ln_g, v_conv_ln_b, v_w_pw2, v_b_pw2, v_w_up, v_w_down, v_final_norm):
    W = dict(w_mod=w_mod, b_mod=b_mod, norm_mix=norm_mix, norm_mlp=norm_mlp, w_qkv=w_qkv, b_qkv=b_qkv, w_o=w_o,
             b_o=b_o, sinks=sinks, w_pw1=w_pw1, b_pw1=b_pw1, w_dw=w_dw, b_dw=b_dw, conv_ln_g=conv_ln_g,
             conv_ln_b=conv_ln_b, w_pw2=w_pw2, b_pw2=b_pw2, w_up=w_up, w_down=w_down, final_norm=final_norm)
    M = dict(w_mod=m_w_mod, b_mod=m_b_mod, norm_mix=m_norm_mix, norm_mlp=m_norm_mlp, w_qkv=m_w_qkv, b_qkv=m_b_qkv,
             w_o=m_w_o, b_o=m_b_o, sinks=m_sinks, w_pw1=m_w_pw1, b_pw1=m_b_pw1, w_dw=m_w_dw, b_dw=m_b_dw,
             conv_ln_g=m_conv_ln_g, conv_ln_b=m_conv_ln_b, w_pw2=m_w_pw2, b_pw2=m_b_pw2, w_up=m_w_up,
             w_down=m_w_down, final_norm=m_final_norm)
    V = dict(w_mod=v_w_mod, b_mod=v_b_mod, norm_mix=v_norm_mix, norm_mlp=v_norm_mlp, w_qkv=v_w_qkv, b_qkv=v_b_qkv,
             w_o=v_w_o, b_o=v_b_o, sinks=v_sinks, w_pw1=v_w_pw1, b_pw1=v_b_pw1, w_dw=v_w_dw, b_dw=v_b_dw,
             conv_ln_g=v_conv_ln_g, conv_ln_b=v_conv_ln_b, w_pw2=v_w_pw2, b_pw2=v_b_pw2, w_up=v_w_up,
             w_down=v_w_down, final_norm=v_final_norm)
    return _step(x, c, loss_target, W, M, V)
```

```python
import functools

import numpy as np
import jax
import jax.numpy as jnp
from jax import lax
from jax.experimental import pallas as pl
from jax.experimental.pallas import tpu as pltpu

F32 = jnp.float32
BF16 = jnp.bfloat16

N_DEV = 8
N_HEADS = 16
N_KV_HEADS = 2
HEAD_DIM = 64
GROUP = N_HEADS // N_KV_HEADS
ATT_BLOCK = 128
CONV_WIDTH = 31
CONV_HALO = 32
N_MOD = 6
EPS = 1e-6
ADAM_LR = 0.001
ADAM_B1 = 0.9
ADAM_B2 = 0.999
ADAM_EPS = 1e-08
ADAM_WD = 0.01
ADAM_STEP = 10
NEG_BIG = -1e30
PACK_COLS = 1024
VMEM_LIMIT_BYTES = 56 * 1024 * 1024
MESH_ID = pl.DeviceIdType.MESH


def _params(*sem):
    return pltpu.CompilerParams(dimension_semantics=sem, vmem_limit_bytes=VMEM_LIMIT_BYTES)


def _pick(n, pref, mult=8):
    if n <= pref:
        return n
    for t in range(pref, 0, -1):
        if n % t == 0 and t % mult == 0:
            return t
    return n


def _sigmoid(z):
    return 0.5 * jnp.tanh(0.5 * z) + 0.5


def _payload_layout(payload):
    n = len(payload)
    out_shapes = [jax.ShapeDtypeStruct((N_DEV,) + tuple(a.shape if bc else a.shape[1:]), a.dtype) for a, bc in payload]
    hbm = pl.BlockSpec(memory_space=pl.ANY)
    scratch = [pltpu.SemaphoreType.DMA((n * (N_DEV - 1),)), pltpu.SemaphoreType.DMA((n * (N_DEV - 1),)),
               pltpu.SemaphoreType.DMA((n,))]
    return [a for a, _ in payload], [hbm] * n, out_shapes, [hbm] * n, scratch


class _Plan:
    def __init__(self, bcasts, src_refs, dst_refs, send_sems, recv_sems, local_sems):
        x, y, c = lax.axis_index("x"), lax.axis_index("y"), lax.axis_index("c")
        me = 4 * x + 2 * y + c
        self.first, self.landed, self.relay, self.local = [], [], [], []
        for t, (bc, s_ref, d_ref) in enumerate(zip(bcasts, src_refs, dst_refs)):
            def remote(k, src, slot, to):
                sem = t * (N_DEV - 1) + k
                return pltpu.make_async_remote_copy(src_ref=src, dst_ref=d_ref.at[slot], send_sem=send_sems.at[sem],
                                                    recv_sem=recv_sems.at[sem], device_id=to, device_id_type=MESH_ID)
            if bc:
                chips = [(1 - x, y), (x, 1 - y), (1 - x, 1 - y)]
                self.first.append(remote(0, s_ref, me, (x, y, 1 - c)))
                for j, (px, py) in enumerate(chips):
                    cp = remote(1 + j, s_ref, me, (px, py, c))
                    self.first.append(cp)
                    self.landed.append(cp)
                    theirs = 4 * px + 2 * py + c
                    self.relay.append(remote(4 + j, d_ref.at[theirs], theirs, (x, y, 1 - c)))
                self.local.append(pltpu.make_async_copy(s_ref, d_ref.at[me], local_sems.at[t]))
            else:
                for k in range(1, N_DEV):
                    px = 1 - x if (k >> 2) & 1 else x
                    py = 1 - y if (k >> 1) & 1 else y
                    pc = 1 - c if k & 1 else c
                    self.first.append(remote(k - 1, s_ref.at[4 * px + 2 * py + pc], me, (px, py, pc)))
                self.local.append(pltpu.make_async_copy(s_ref.at[me], d_ref.at[me], local_sems.at[t]))

    def start(self):
        for cp in self.first + self.local:
            cp.start()

    def pass_on(self):
        for cp in self.landed:
            cp.wait_recv()
        for cp in self.relay:
            cp.start()

    def finish(self):
        for cp in self.first:
            cp.wait_send()
            if not any(cp is l for l in self.landed):
                cp.wait_recv()
        for cp in self.relay:
            cp.wait()
        for cp in self.local:
            cp.wait()


def _exchange(name, payload):
    n = len(payload)
    bcasts = [bc for _, bc in payload]
    arrays, in_specs, out_shapes, out_specs, scratch = _payload_layout(payload)

    def body(*refs):
        plan = _Plan(bcasts, refs[:n], refs[n:2 * n], *refs[2 * n:])
        plan.start()
        plan.pass_on()
        plan.finish()

    return pl.pallas_call(
        body,
        name=name,
        out_shape=tuple(out_shapes),
        in_specs=in_specs,
        out_specs=tuple(out_specs),
        scratch_shapes=scratch,
    )(*arrays)


def _pack(arrays, dtype, lead):
    lead_shape = arrays[0].shape[:lead]
    flat = [a.astype(dtype).reshape(lead_shape + (-1,)) for a in arrays]
    sizes = [f.shape[-1] for f in flat]
    total = sum(sizes)
    chunk = 16 * PACK_COLS
    padded = -(-total // chunk) * chunk
    if padded > total:
        flat.append(jnp.zeros(lead_shape + (padded - total,), dtype))
    buf = jnp.concatenate(flat, axis=-1)
    return buf.reshape(lead_shape + (padded // PACK_COLS, PACK_COLS)), sizes


def _unpack(buf, sizes, shapes, lead):
    lead_shape = buf.shape[:lead]
    flat = buf.reshape(lead_shape + (-1,))
    out, off = [], 0
    for n, shp in zip(sizes, shapes):
        out.append(lax.slice_in_dim(flat, off, off + n, axis=lead).reshape(lead_shape + tuple(shp)))
        off += n
    return out


def _to_slots(a, dim):
    shp = a.shape
    a = a.reshape(shp[:dim] + (N_DEV, shp[dim] // N_DEV) + shp[dim + 1:])
    return jnp.moveaxis(a, dim, 0)


def _from_slots(a, dim):
    a = jnp.moveaxis(a, 0, dim)
    shp = a.shape
    return a.reshape(shp[:dim] + (shp[dim] * shp[dim + 1],) + shp[dim + 2:])


def _mod_fwd(c_all, w, b):
    L, D, n = w.shape
    M = c_all.shape[0]

    def body(c_ref, w_ref, b_ref, o_ref):
        cv = c_ref[...]
        cs = (cv * _sigmoid(cv)).astype(BF16)
        o_ref[...] = jnp.dot(cs, w_ref[...], preferred_element_type=F32) + b_ref[...]

    return pl.pallas_call(
        body,
        name="mod_fwd",
        grid=(L,),
        out_shape=jax.ShapeDtypeStruct((L, M, n), F32),
        in_specs=[
            pl.BlockSpec((M, D), lambda l: (0, 0)),
            pl.BlockSpec((None, D, n), lambda l: (l, 0, 0)),
            pl.BlockSpec((None, 1, n), lambda l: (l, 0, 0)),
        ],
        out_specs=pl.BlockSpec((None, M, n), lambda l: (l, 0, 0)),
        compiler_params=_params("arbitrary"),
    )(c_all, w, b)


def _mod_bwd(c_all, dmod_all):
    L, M, n = dmod_all.shape
    D = c_all.shape[1]

    def body(c_ref, d_ref, dw_ref, db_ref):
        cv = c_ref[...]
        cs = (cv * _sigmoid(cv)).astype(BF16)
        d = d_ref[...]
        dw_ref[...] = lax.dot_general(cs, d.astype(BF16), (((0,), (0,)), ((), ())), preferred_element_type=F32)
        db_ref[...] = jnp.sum(d, axis=0, keepdims=True)

    return pl.pallas_call(
        body,
        name="mod_bwd",
        grid=(L,),
        out_shape=(jax.ShapeDtypeStruct((L, D, n), F32), jax.ShapeDtypeStruct((L, 1, n), F32)),
        in_specs=[
            pl.BlockSpec((M, D), lambda l: (0, 0)),
            pl.BlockSpec((None, M, n), lambda l: (l, 0, 0)),
        ],
        out_specs=(
            pl.BlockSpec((None, D, n), lambda l: (l, 0, 0)),
            pl.BlockSpec((None, 1, n), lambda l: (l, 0, 0)),
        ),
        compiler_params=_params("arbitrary"),
    )(c_all, dmod_all)


def _mm(name, a, w, *, w_form="full", out_dtype=BF16, bias=None, a_sq=False, epi=None, x=None, gate=None, u=None,
        norm=None, nb=None, rows_per_batch=None, tm_pref=512, payload=()):
    M, K = a.shape
    if w_form == "full":
        N = w.shape[1]
        nc = _pick(N, 1024, 128)
    elif w_form == "full_t":
        N = w.shape[0]
        nc = _pick(N, 1024, 128)
    elif w_form == "nslots":
        nc = w.shape[2]
        N = N_DEV * nc
    else:
        N = w.shape[1]
        nc = N
    ks = K // N_DEV
    n_chunks = N // nc
    tm = _pick(M if rows_per_batch is None else rows_per_batch, tm_pref, 16)
    steps = M // tm
    relay_step = (3 * steps) // 4
    nb_gate = nb is not None and nb.get("gate") is not None
    tpb_nb = rows_per_batch // tm if epi == "normbwd" else 1
    has_bias = bias is not None
    n_pay = len(payload)
    bcasts = [bc for _, bc in payload]
    nt = (((1,), (1,)), ((), ()))


    def body(*refs):
        it = iter(refs)
        a_ref = next(it)
        w_ref = next(it)
        b_ref = next(it) if has_bias else None
        x_ref = next(it) if epi == "resid" else None
        g_ref = next(it) if epi == "resid" else None
        gn_ref, sc_ref, sh_ref = (next(it), next(it), next(it)) if norm is not None else (None, None, None)
        u_ref = next(it) if epi == "relu2d" else None
        if epi == "normbwd":
            nx_ref, ndx_ref, ngn_ref, nsc_ref = next(it), next(it), next(it), next(it)
            ny_ref, ngt_ref = (next(it), next(it)) if nb_gate else (None, None)
        pay_src = [next(it) for _ in range(n_pay)]
        o_ref = next(it)
        y_ref = next(it) if epi == "resid" else None
        h_ref = next(it) if norm is not None else None
        if epi == "normbwd":
            dsh_ref, dsc_ref, dgn_ref = next(it), next(it), next(it)
            dy_ref, dgate_ref, cs_ref = (next(it), next(it), next(it)) if nb_gate else (None, None, None)
        pay_dst = [next(it) for _ in range(n_pay)]
        sems = list(it)

        if epi == "normbwd":
            @pl.when(pl.program_id(0) % tpb_nb == 0)
            def _():
                dsh_ref[...] = jnp.zeros_like(dsh_ref)
                dsc_ref[...] = jnp.zeros_like(dsc_ref)
                if nb_gate:
                    dgate_ref[...] = jnp.zeros_like(dgate_ref)

            @pl.when(pl.program_id(0) == 0)
            def _():
                dgn_ref[...] = jnp.zeros_like(dgn_ref)
                if nb_gate:
                    cs_ref[...] = jnp.zeros_like(cs_ref)

        if n_pay:
            @pl.when(pl.program_id(0) == 0)
            def _():
                _Plan(bcasts, pay_src, pay_dst, *sems).start()

            @pl.when(pl.program_id(0) == relay_step)
            def _():
                _Plan(bcasts, pay_src, pay_dst, *sems).pass_on()

        for rows in (slice(0, tm),):
            av = None if w_form == "kslots_t" else a_ref[rows, :]
            if a_sq:
                av = av * av
            for ci in range(n_chunks):
                cols = slice(ci * nc, (ci + 1) * nc)
                if w_form == "full":
                    acc = jnp.dot(av, w_ref[:, cols], preferred_element_type=F32)
                elif w_form == "full_t":
                    acc = lax.dot_general(av, w_ref[cols, :], nt, preferred_element_type=F32)
                elif w_form == "nslots":
                    acc = jnp.dot(av, w_ref[ci], preferred_element_type=F32)
                else:
                    acc = lax.dot_general(a_ref[rows, 0:ks], w_ref[0], nt, preferred_element_type=F32)
                    for j in range(1, N_DEV):
                        acc = acc + lax.dot_general(a_ref[rows, j * ks:(j + 1) * ks], w_ref[j], nt,
                                                    preferred_element_type=F32)
                if has_bias:
                    acc = acc + b_ref[:, cols]
                if epi == "resid":
                    xn = x_ref[rows, cols] + g_ref[:, cols] * acc
                    o_ref[rows, cols] = xn
                    y_ref[rows, cols] = acc.astype(BF16)
                    if norm is not None:
                        r = lax.rsqrt(jnp.mean(xn * xn, axis=-1, keepdims=True) + EPS)
                        h_ref[rows, :] = ((xn * r * gn_ref[...]) * (1.0 + sc_ref[...]) + sh_ref[...]).astype(BF16)
                elif epi == "relu":
                    o_ref[rows, cols] = jnp.maximum(acc, 0.0).astype(out_dtype)
                elif epi == "relu2d":
                    o_ref[rows, cols] = (acc * (2.0 * u_ref[rows, cols].astype(F32))).astype(out_dtype)
                elif epi == "normbwd":
                    xv = nx_ref[rows, :]
                    gn = ngn_ref[...]
                    r = lax.rsqrt(jnp.mean(xv * xv, axis=-1, keepdims=True) + EPS)
                    xhat = xv * r
                    dsh_ref[...] += jnp.sum(acc, axis=0, keepdims=True)
                    dsc_ref[...] += jnp.sum(acc * (xhat * gn), axis=0, keepdims=True)
                    dn = acc * (1.0 + nsc_ref[...])
                    dgn_ref[...] += jnp.sum(dn * xhat, axis=0, keepdims=True)
                    dxhat = dn * gn
                    dx = ndx_ref[rows, :] + r * (dxhat - xhat * jnp.mean(dxhat * xhat, axis=-1, keepdims=True))
                    o_ref[rows, :] = dx
                    if nb_gate:
                        dy = dx * ngt_ref[...]
                        dy_ref[rows, :] = dy.astype(BF16)
                        dgate_ref[...] += jnp.sum(dx * ny_ref[rows, :].astype(F32), axis=0, keepdims=True)
                        cs_ref[...] += jnp.sum(dy, axis=0, keepdims=True)
                else:
                    o_ref[rows, cols] = acc.astype(out_dtype)

        if n_pay:
            @pl.when(pl.program_id(0) == steps - 1)
            def _():
                _Plan(bcasts, pay_src, pay_dst, *sems).finish()

    args = [a, w]
    w_block = w.shape
    specs = [pl.BlockSpec((tm, K), lambda i: (i, 0)), pl.BlockSpec(w_block, lambda i: (0,) * len(w_block))]
    if has_bias:
        args.append(bias.reshape(1, N).astype(F32))
        specs.append(pl.BlockSpec((1, N), lambda i: (0, 0)))
    row_spec = pl.BlockSpec((tm, N), lambda i: (i, 0))
    if epi == "resid":
        tpb = rows_per_batch // tm
        vec_spec = pl.BlockSpec((None, 1, N), lambda i: (i // tpb, 0, 0))
        args += [x, gate]
        specs += [row_spec, vec_spec]
        out_shape = [jax.ShapeDtypeStruct((M, N), F32), jax.ShapeDtypeStruct((M, N), BF16)]
        out_specs = [row_spec, row_spec]
        if norm is not None:
            assert n_chunks == 1
            args += list(norm)
            specs += [pl.BlockSpec((1, N), lambda i: (0, 0)), vec_spec, vec_spec]
            out_shape.append(jax.ShapeDtypeStruct((M, N), BF16))
            out_specs.append(row_spec)
    elif epi == "normbwd":
        assert n_chunks == 1
        vec_spec = pl.BlockSpec((None, 1, N), lambda i: (i // tpb_nb, 0, 0))
        one_spec = pl.BlockSpec((1, N), lambda i: (0, 0))
        nbat = M // rows_per_batch
        f_vec, f_one = jax.ShapeDtypeStruct((nbat, 1, N), F32), jax.ShapeDtypeStruct((1, N), F32)
        args += [nb["x"], nb["dx_in"], nb["gnorm"], nb["sc"]]
        specs += [row_spec, row_spec, one_spec, vec_spec]
        out_shape = [jax.ShapeDtypeStruct((M, N), F32), f_vec, f_vec, f_one]
        out_specs = [row_spec, vec_spec, vec_spec, one_spec]
        if nb_gate:
            args += list(nb["gate"])
            specs += [row_spec, vec_spec]
            out_shape += [jax.ShapeDtypeStruct((M, N), BF16), f_vec, f_one]
            out_specs += [row_spec, vec_spec, one_spec]
    else:
        if epi == "relu2d":
            args.append(u)
            specs.append(row_spec)
        out_shape = [jax.ShapeDtypeStruct((M, N), out_dtype)]
        out_specs = [row_spec]
    scratch = []
    if n_pay:
        p_arrays, p_in, p_shapes, p_out, scratch = _payload_layout(payload)
        args += p_arrays
        specs += p_in
        out_shape += p_shapes
        out_specs += p_out
    res = pl.pallas_call(
        body,
        name=name,
        grid=(steps,),
        out_shape=tuple(out_shape),
        in_specs=specs,
        out_specs=tuple(out_specs),
        scratch_shapes=scratch,
        compiler_params=_params("arbitrary" if (n_pay or epi == "normbwd") else "parallel"),
    )(*args)
    if epi == "resid":
        n_own = 3 if norm is not None else 2
    elif epi == "normbwd":
        n_own = 7 if nb_gate else 4
    else:
        n_own = 1
    own = res[0] if n_own == 1 else tuple(res[:n_own])
    return (own, list(res[n_own:])) if n_pay else own


def _mm_tn(name, a, b, *, a_sq=False, out_form="full", colsum=False, tt_pref=1024, tk_pref=1024, tn_pref=1024,
           payload=()):
    T, K = a.shape
    N = b.shape[1]
    tt = _pick(T, tt_pref, 16)
    tk = K // N_DEV if out_form == "kslots" else _pick(K, tk_pref, 128)
    tn = N // N_DEV if out_form == "nslots" else _pick(N, tn_pref, 128)
    nt_steps = T // tt
    grid = (K // tk, N // tn, nt_steps)
    slots = out_form != "full"
    assert not colsum or K == tk
    n_pay = len(payload)
    bcasts = [bc for _, bc in payload]

    def body(*refs):
        it = iter(refs)
        a_ref, b_ref = next(it), next(it)
        pay_src = [next(it) for _ in range(n_pay)]
        o_ref = next(it)
        cs_ref = next(it) if colsum else None
        pay_dst = [next(it) for _ in range(n_pay)]
        acc = next(it) if slots else o_ref
        sems = list(it)
        step = (pl.program_id(0) * grid[1] + pl.program_id(1)) * grid[2] + pl.program_id(2)

        if n_pay:
            @pl.when(step == 0)
            def _():
                _Plan(bcasts, pay_src, pay_dst, *sems).start()

        @pl.when(pl.program_id(2) == 0)
        def _():
            acc[...] = jnp.zeros_like(acc)
            if colsum:
                cs_ref[...] = jnp.zeros_like(cs_ref)

        bv = b_ref[...]
        av = a_ref[...]
        if a_sq:
            av = av * av
        acc[...] += lax.dot_general(av, bv, (((0,), (0,)), ((), ())), preferred_element_type=F32)
        if colsum:
            cs_ref[...] += jnp.sum(bv.astype(F32), axis=0, keepdims=True)

        if slots:
            @pl.when(pl.program_id(2) == nt_steps - 1)
            def _():
                o_ref[...] = acc[...].astype(o_ref.dtype)

        if n_pay:
            @pl.when(step == grid[0] * grid[1] * grid[2] - 1)
            def _():
                plan = _Plan(bcasts, pay_src, pay_dst, *sems)
                plan.pass_on()
                plan.finish()

    if out_form == "full":
        out_shape = [jax.ShapeDtypeStruct((K, N), F32)]
        out_specs = [pl.BlockSpec((tk, tn), lambda k, n, t: (k, n))]
    elif out_form == "nslots":
        out_shape = [jax.ShapeDtypeStruct((N_DEV, K, tn), BF16)]
        out_specs = [pl.BlockSpec((None, tk, tn), lambda k, n, t: (n, k, 0))]
    else:
        out_shape = [jax.ShapeDtypeStruct((N_DEV, tk, N), BF16)]
        out_specs = [pl.BlockSpec((None, tk, tn), lambda k, n, t: (k, 0, n))]
    if colsum:
        out_shape.append(jax.ShapeDtypeStruct((1, N), F32))
        out_specs.append(pl.BlockSpec((1, tn), lambda k, n, t: (0, n)))
    args = [a, b]
    in_specs = [pl.BlockSpec((tt, tk), lambda k, n, t: (t, k)), pl.BlockSpec((tt, tn), lambda k, n, t: (t, n))]
    scratch = [pltpu.VMEM((tk, tn), F32)] if slots else []
    n_own = len(out_shape)
    if n_pay:
        p_arrays, p_in, p_shapes, p_out, p_scratch = _payload_layout(payload)
        args += p_arrays
        in_specs += p_in
        out_shape += p_shapes
        out_specs += p_out
        scratch += p_scratch
    res = pl.pallas_call(
        body,
        name=name,
        grid=grid,
        out_shape=tuple(out_shape),
        in_specs=in_specs,
        out_specs=tuple(out_specs),
        scratch_shapes=scratch,
        compiler_params=_params(*(("arbitrary",) * 3 if n_pay else ("parallel", "parallel", "arbitrary"))),
    )(*args)
    own = res[0] if n_own == 1 else tuple(res[:n_own])
    return (own, list(res[n_own:])) if n_pay else own


def _normmod_fwd(name, x, gnorm, sc, sh, S, tm_pref=512):
    T, D = x.shape
    tm = _pick(S, tm_pref, 16)
    tpb = S // tm

    def body(x_ref, g_ref, sc_ref, sh_ref, o_ref):
        xv = x_ref[...]
        r = lax.rsqrt(jnp.mean(xv * xv, axis=-1, keepdims=True) + EPS)
        n = xv * r * g_ref[...]
        o_ref[...] = (n * (1.0 + sc_ref[...]) + sh_ref[...]).astype(BF16)

    row = pl.BlockSpec((tm, D), lambda i: (i, 0))
    vec = pl.BlockSpec((None, 1, D), lambda i: (i // tpb, 0, 0))
    return pl.pallas_call(
        body,
        name=name,
        grid=(T // tm,),
        out_shape=jax.ShapeDtypeStruct((T, D), BF16),
        in_specs=[row, pl.BlockSpec((1, D), lambda i: (0, 0)), vec, vec],
        out_specs=row,
        compiler_params=_params("parallel"),
    )(x, gnorm, sc, sh)


ATT_ROWS = GROUP * ATT_BLOCK
ATT_SPAN = 2 * ATT_BLOCK
ATT_SCALE = HEAD_DIM ** -0.5


def _attn_table(sinks):
    slopes = jnp.asarray(np.array([2.0 ** (-8.0 * (h + 1) / N_HEADS) for h in range(N_HEADS)], np.float32))
    r = jnp.arange(ATT_BLOCK)[:, None]
    cc = jnp.arange(ATT_SPAN)[None, :]
    dist = r + ATT_BLOCK - cc
    ok = (dist >= 0) & (dist < ATT_BLOCK)
    tab = jnp.where(ok[None], -slopes[:, None, None] * dist.astype(F32)[None], NEG_BIG)
    tab = jnp.where((cc == 0)[None], sinks.astype(F32)[:, None, None], tab)
    return tab.reshape(N_KV_HEADS, ATT_ROWS, ATT_SPAN)


KV_COLS = N_KV_HEADS * HEAD_DIM


PAIR = 2 * HEAD_DIM


def _pair_tables(tab):
    hkv, rows, span = tab.shape
    pairs = rows // (2 * ATT_BLOCK)
    return tab.reshape(hkv, pairs, 2, ATT_BLOCK, span).transpose(0, 2, 1, 3, 4).reshape(hkv, 2, pairs * ATT_BLOCK, span)


def _stack_pairs(ref, pairs):
    return jnp.concatenate([ref[:, j * PAIR:(j + 1) * PAIR] for j in range(pairs)], axis=0)


def _unstack_pairs(v, pairs):
    return jnp.concatenate([v[j * ATT_BLOCK:(j + 1) * ATT_BLOCK, :] for j in range(pairs)], axis=1)


def _swap_halves(x):
    return jnp.concatenate([x[:, HEAD_DIM:], x[:, :HEAD_DIM]], axis=1)


def _load_span(prev_ref, cur_ref, buf_e, buf_o, kv, mult):
    lane = lax.broadcasted_iota(jnp.int32, (ATT_BLOCK, PAIR), 1)
    row = lax.broadcasted_iota(jnp.int32, (ATT_BLOCK, PAIR), 0)
    own = lane // HEAD_DIM == kv
    for r0, ref, first in ((0, prev_ref, True), (ATT_BLOCK, cur_ref, False)):
        v = ref[...]
        if mult != 1.0:
            v = v * mult
        zero = jnp.zeros_like(v)
        mine = jnp.where(own, v, zero)
        if first:
            mine = jnp.where(row > 0, mine, zero)
        other = _swap_halves(mine)
        buf_e[r0:r0 + ATT_BLOCK, :] = jnp.where(kv == 0, mine, other)
        buf_o[r0:r0 + ATT_BLOCK, :] = jnp.where(kv == 0, other, mine)


def _first_block_penalty(i):
    col = lax.broadcasted_iota(jnp.int32, (1, ATT_SPAN), 1)
    return jnp.where((col < ATT_BLOCK) & (col > 0), jnp.where(i > 0, 0.0, NEG_BIG), 0.0).astype(F32)


def _attn_probs(qp, kbuf, bias, first_pen):
    nt = (((1,), (1,)), ((), ()))
    s = lax.dot_general(qp, kbuf, nt, preferred_element_type=F32) + bias + first_pen
    m = jnp.max(s, axis=-1, keepdims=True)
    e = jnp.exp(s - m)
    return e * (1.0 / jnp.sum(e, axis=-1, keepdims=True))


def _attn_specs(nb, pairs, q_map, row_cur, row_prev):
    gw = pairs * PAIR
    kblk = N_KV_HEADS * gw // KV_COLS
    qspec = pl.BlockSpec((ATT_BLOCK, gw), q_map)
    kv_specs = [pl.BlockSpec((ATT_BLOCK, KV_COLS), (lambda b, i, kv, r=r, c=c: (r(b, i), c)))
                for c in (kblk, kblk + 1) for r in (row_cur, row_prev)]
    tspec = pl.BlockSpec((N_KV_HEADS, 2, pairs * ATT_BLOCK, ATT_SPAN), lambda b, i, kv: (0, 0, 0, 0))
    return qspec, kv_specs, tspec


def _attn_fwd(qkv, tab2, B, S, payload=()):
    assert N_KV_HEADS == 2 and GROUP % 2 == 0
    T = qkv.shape[0]
    pairs = GROUP // 2
    nb = S // ATT_BLOCK
    q_dim = N_HEADS * HEAD_DIM
    n_pay = len(payload)
    bcasts = [bc for _, bc in payload]
    steps = B * nb * N_KV_HEADS

    def body(*refs):
        q_ref, kc_ref, kp_ref, vc_ref, vp_ref, tab_ref = refs[:6]
        pay_src = refs[6:6 + n_pay]
        o_ref = refs[6 + n_pay]
        pay_dst = refs[7 + n_pay:7 + 2 * n_pay]
        ke, ko, ve, vo = refs[7 + 2 * n_pay:11 + 2 * n_pay]
        sems = refs[11 + 2 * n_pay:]
        i, kv = pl.program_id(1), pl.program_id(2)
        if n_pay:
            step = (pl.program_id(0) * nb + i) * N_KV_HEADS + kv

            @pl.when(step == 0)
            def _():
                _Plan(bcasts, pay_src, pay_dst, *sems).start()

            @pl.when(step == (3 * steps) // 4)
            def _():
                _Plan(bcasts, pay_src, pay_dst, *sems).pass_on()

            @pl.when(step == steps - 1)
            def _():
                _Plan(bcasts, pay_src, pay_dst, *sems).finish()

        _load_span(kp_ref, kc_ref, ke, ko, kv, ATT_SCALE)
        _load_span(vp_ref, vc_ref, ve, vo, kv, 1.0)
        pen = _first_block_penalty(i)
        qp = _stack_pairs(q_ref, pairs)
        p_e = _attn_probs(qp, ke[...], tab_ref[kv, 0], pen)
        p_o = _attn_probs(qp, ko[...], tab_ref[kv, 1], pen)
        o = jnp.dot(p_e.astype(BF16), ve[...], preferred_element_type=F32)
        o = o + jnp.dot(p_o.astype(BF16), vo[...], preferred_element_type=F32)
        o_ref[...] = _unstack_pairs(o.astype(BF16), pairs)

    qspec, kv_specs, tspec = _attn_specs(nb, pairs, lambda b, i, kv: (b * nb + i, kv), lambda b, i: b * nb + i,
                                         lambda b, i: b * nb + jnp.maximum(i - 1, 0))
    args, in_specs = [qkv] * 5 + [tab2], [qspec] + kv_specs + [tspec]
    out_shape, out_specs = [jax.ShapeDtypeStruct((T, q_dim), BF16)], [qspec]
    scratch = [pltpu.VMEM((ATT_SPAN, PAIR), BF16)] * 4
    if n_pay:
        p_arrays, p_in, p_shapes, p_out, p_scratch = _payload_layout(payload)
        args += p_arrays
        in_specs += p_in
        out_shape += p_shapes
        out_specs += p_out
        scratch += p_scratch
    res = pl.pallas_call(
        body,
        name="attn_fwd",
        grid=(B, nb, N_KV_HEADS),
        out_shape=tuple(out_shape),
        in_specs=in_specs,
        out_specs=tuple(out_specs),
        scratch_shapes=scratch,
        compiler_params=_params(*(("arbitrary",) * 3 if n_pay else ("parallel", "parallel", "arbitrary"))),
    )(*args)
    return (res[0], list(res[1:])) if n_pay else res[0]


def _attn_bwd(qkv, o, do, tab2, B, S):
    T = qkv.shape[0]
    hd = HEAD_DIM
    pairs = GROUP // 2
    rows = pairs * ATT_BLOCK
    nb = S // ATT_BLOCK
    last = nb - 1
    q_dim = N_HEADS * hd
    tn = (((0,), (0,)), ((), ()))
    nt = (((1,), (1,)), ((), ()))

    def body(q_ref, kc_ref, kp_ref, vc_ref, vp_ref, tab_ref, o_ref, do_ref,
             dq_ref, dkv_ref, dsink_ref, carry_k, carry_v, ke, ko, ve, vo):
        i, kv = pl.program_id(1), pl.program_id(2)

        @pl.when((i == 0) & (kv == 0))
        def _():
            dsink_ref[...] = jnp.zeros_like(dsink_ref)

        def emit(dk_rows, dv_rows):
            for h in range(N_KV_HEADS):
                @pl.when(kv == h)
                def _():
                    dkv_ref[:, h * hd:(h + 1) * hd] = dk_rows.astype(BF16)
                    dkv_ref[:, KV_COLS + h * hd:KV_COLS + (h + 1) * hd] = dv_rows.astype(BF16)

        @pl.when(i < nb)
        def _():
            _load_span(kp_ref, kc_ref, ke, ko, kv, ATT_SCALE)
            _load_span(vp_ref, vc_ref, ve, vo, kv, 1.0)
            pen = _first_block_penalty(i)
            qp = _stack_pairs(q_ref, pairs)
            dop = _stack_pairs(do_ref, pairs)
            op = _stack_pairs(o_ref, pairs)
            p_e = _attn_probs(qp, ke[...], tab_ref[kv, 0], pen)
            p_o = _attn_probs(qp, ko[...], tab_ref[kv, 1], pen)
            prod = dop.astype(F32) * op.astype(F32)
            low = lax.broadcasted_iota(jnp.int32, (rows, PAIR), 1) < hd
            d_e = jnp.sum(jnp.where(low, prod, 0.0), axis=-1, keepdims=True)
            d_o = jnp.sum(prod, axis=-1, keepdims=True) - d_e
            ds_e = (p_e * (lax.dot_general(dop, ve[...], nt, preferred_element_type=F32) - d_e)).astype(BF16)
            ds_o = (p_o * (lax.dot_general(dop, vo[...], nt, preferred_element_type=F32) - d_o)).astype(BF16)
            dq = jnp.dot(ds_e, ke[...], preferred_element_type=F32) + jnp.dot(ds_o, ko[...], preferred_element_type=F32)
            dq_ref[...] = _unstack_pairs(dq.astype(BF16), pairs)
            hg = lax.broadcasted_iota(jnp.int32, (pairs, rows), 0)
            hr = lax.broadcasted_iota(jnp.int32, (pairs, rows), 1)
            head_of = jnp.where(hr // ATT_BLOCK == hg, 1.0, 0.0).astype(BF16)
            dsink_ref[kv, 0] += jnp.dot(head_of, ds_e, preferred_element_type=F32)[:, 0:128]
            dsink_ref[kv, 1] += jnp.dot(head_of, ds_o, preferred_element_type=F32)[:, 0:128]
            low_k = lax.broadcasted_iota(jnp.int32, (ATT_SPAN, PAIR), 1) < hd
            dk2 = jnp.where(low_k, lax.dot_general(ds_e, qp, tn, preferred_element_type=F32),
                            lax.dot_general(ds_o, qp, tn, preferred_element_type=F32))
            dv2 = jnp.where(low_k, lax.dot_general(p_e.astype(BF16), dop, tn, preferred_element_type=F32),
                            lax.dot_general(p_o.astype(BF16), dop, tn, preferred_element_type=F32))
            keep = lax.broadcasted_iota(jnp.int32, (ATT_SPAN, hd), 0) > 0
            dk = jnp.where(keep, (dk2[:, :hd] + dk2[:, hd:]) * ATT_SCALE, 0.0)
            dv = jnp.where(keep, dv2[:, :hd] + dv2[:, hd:], 0.0)

            @pl.when(i > 0)
            def _():
                emit(carry_k[kv] + dk[0:ATT_BLOCK], carry_v[kv] + dv[0:ATT_BLOCK])

            carry_k[kv] = dk[ATT_BLOCK:]
            carry_v[kv] = dv[ATT_BLOCK:]

        @pl.when(i == nb)
        def _():
            emit(carry_k[kv], carry_v[kv])

    def q_map(b, i, kv):
        return (b * nb + jnp.minimum(i, last), jnp.where(i == nb, N_KV_HEADS - 1, kv))

    qspec, kv_specs, tspec = _attn_specs(nb, pairs, q_map, lambda b, i: b * nb + jnp.minimum(i, last),
                                         lambda b, i: b * nb + jnp.clip(i - 1, 0, last))
    dkv = pl.BlockSpec((ATT_BLOCK, 2 * KV_COLS), lambda b, i, kv: (b * nb + jnp.maximum(i - 1, 0), 0))
    dsk = pl.BlockSpec((None, N_KV_HEADS, 2, pairs, 128), lambda b, i, kv: (b, 0, 0, 0, 0))
    return pl.pallas_call(
        body,
        name="attn_bwd",
        grid=(B, nb + 1, N_KV_HEADS),
        out_shape=(
            jax.ShapeDtypeStruct((T, q_dim), BF16),
            jax.ShapeDtypeStruct((T, 2 * KV_COLS), BF16),
            jax.ShapeDtypeStruct((B, N_KV_HEADS, 2, pairs, 128), F32),
        ),
        in_specs=[qspec] + kv_specs + [tspec, qspec, qspec],
        out_specs=(qspec, dkv, dsk),
        scratch_shapes=[pltpu.VMEM((N_KV_HEADS, ATT_BLOCK, hd), F32), pltpu.VMEM((N_KV_HEADS, ATT_BLOCK, hd), F32)]
        + [pltpu.VMEM((ATT_SPAN, PAIR), BF16)] * 4,
        compiler_params=_params("arbitrary", "arbitrary", "arbitrary"),
    )(qkv, qkv, qkv, qkv, qkv, tab2, o, do)


def _conv_tiles(S):
    ts = _pick(S, 256, CONV_HALO)
    return ts, S // ts


def _conv_chunks(C, ts):
    lane = _pick(C, 128, 128)
    return lane, C // lane, _pick(ts, 128, 8)


def _conv_weight_chunks(w_dw, C):
    lane = _pick(C, 128, 128)
    w = jnp.pad(w_dw, ((0, CONV_HALO - CONV_WIDTH), (0, 0)))
    return w.reshape(CONV_HALO, C // lane, lane).transpose(1, 0, 2)


def _conv_fwd(u, w3, b_dw, ln_g, ln_b, S):
    T, C2 = u.shape
    C = C2 // 2
    B = T // S
    ts, nj = _conv_tiles(S)
    hb = ts // CONV_HALO
    lane, nc, rc = _conv_chunks(C, ts)

    def body(a_ref, g_ref, ap_ref, gp_ref, w_ref, bdw_ref, lg_ref, lb_ref, cv_ref, s_ref, buf, cvb):
        j = pl.program_id(1)
        glu_prev = ap_ref[...] * _sigmoid(gp_ref[...]) * (j > 0).astype(F32)
        glu = a_ref[...] * _sigmoid(g_ref[...])
        for cc in range(nc):
            buf[cc, 0:CONV_HALO, :] = glu_prev[:, cc * lane:(cc + 1) * lane]
            buf[cc, CONV_HALO:, :] = glu[:, cc * lane:(cc + 1) * lane]

        def chunk(cc, carry):
            for r0 in range(0, ts, rc):
                acc = jnp.zeros((rc, lane), F32)
                for kk in range(CONV_WIDTH):
                    lo = CONV_HALO - (CONV_WIDTH - 1 - kk) + r0
                    acc = acc + w_ref[cc, kk:kk + 1, :] * buf[cc, lo:lo + rc, :]
                cvb[cc, r0:r0 + rc, :] = acc
            return carry

        lax.fori_loop(0, nc, chunk, 0)
        for cc in range(nc):
            cv_ref[:, cc * lane:(cc + 1) * lane] = cvb[cc] + bdw_ref[:, cc * lane:(cc + 1) * lane]
        cv = cv_ref[...]
        mu = jnp.mean(cv, axis=-1, keepdims=True)
        xc = cv - mu
        rstd = lax.rsqrt(jnp.mean(xc * xc, axis=-1, keepdims=True) + EPS)
        ln = xc * rstd * lg_ref[...] + lb_ref[...]
        s_ref[...] = (ln * _sigmoid(ln)).astype(BF16)

    a_cur = pl.BlockSpec((ts, C), lambda b, j: (b * nj + j, 0))
    g_cur = pl.BlockSpec((ts, C), lambda b, j: (b * nj + j, 1))
    a_prev = pl.BlockSpec((CONV_HALO, C), lambda b, j: (jnp.maximum((b * nj + j) * hb - 1, 0), 0))
    g_prev = pl.BlockSpec((CONV_HALO, C), lambda b, j: (jnp.maximum((b * nj + j) * hb - 1, 0), 1))
    wspec = pl.BlockSpec((nc, CONV_HALO, lane), lambda b, j: (0, 0, 0))
    one = pl.BlockSpec((1, C), lambda b, j: (0, 0))
    return pl.pallas_call(
        body,
        name="conv_fwd",
        grid=(B, nj),
        out_shape=(jax.ShapeDtypeStruct((T, C), F32), jax.ShapeDtypeStruct((T, C), BF16)),
        in_specs=[a_cur, g_cur, a_prev, g_prev, wspec, one, one, one],
        out_specs=(a_cur, a_cur),
        scratch_shapes=[pltpu.VMEM((nc, CONV_HALO + ts, lane), F32), pltpu.VMEM((nc, ts, lane), F32)],
        compiler_params=_params("parallel", "arbitrary"),
    )(u, u, u, u, w3, b_dw, ln_g, ln_b)


def _lnsilu_bwd(ds, cv, ln_g, ln_b, tm_pref=256):
    T, C = cv.shape
    tm = _pick(T, tm_pref, 16)

    def body(ds_ref, cv_ref, lg_ref, lb_ref, dcv_ref, dlg_ref, dlb_ref, dbdw_ref):
        @pl.when(pl.program_id(0) == 0)
        def _():
            dlg_ref[...] = jnp.zeros_like(dlg_ref)
            dlb_ref[...] = jnp.zeros_like(dlb_ref)
            dbdw_ref[...] = jnp.zeros_like(dbdw_ref)

        cv_v = cv_ref[...]
        g = lg_ref[...]
        mu = jnp.mean(cv_v, axis=-1, keepdims=True)
        xc = cv_v - mu
        rstd = lax.rsqrt(jnp.mean(xc * xc, axis=-1, keepdims=True) + EPS)
        xhat = xc * rstd
        ln = xhat * g + lb_ref[...]
        sg = _sigmoid(ln)
        dln = ds_ref[...] * (sg * (1.0 + ln * (1.0 - sg)))
        dlg_ref[...] += jnp.sum(dln * xhat, axis=0, keepdims=True)
        dlb_ref[...] += jnp.sum(dln, axis=0, keepdims=True)
        dxhat = dln * g
        dcv = rstd * (dxhat - jnp.mean(dxhat, axis=-1, keepdims=True)
                      - xhat * jnp.mean(dxhat * xhat, axis=-1, keepdims=True))
        dcv_ref[...] = dcv
        dbdw_ref[...] += jnp.sum(dcv, axis=0, keepdims=True)

    row = pl.BlockSpec((tm, C), lambda i: (i, 0))
    one = pl.BlockSpec((1, C), lambda i: (0, 0))
    return pl.pallas_call(
        body,
        name="lnsilu_bwd",
        grid=(T // tm,),
        out_shape=(jax.ShapeDtypeStruct((T, C), F32),) + (jax.ShapeDtypeStruct((1, C), F32),) * 3,
        in_specs=[row, row, one, one],
        out_specs=(row, one, one, one),
        compiler_params=_params("arbitrary"),
    )(ds, cv, ln_g, ln_b)


def _conv_bwd(dcv, u, w3, S):
    T, C2 = u.shape
    C = C2 // 2
    B = T // S
    ts, nj = _conv_tiles(S)
    hb = ts // CONV_HALO
    n_halo_blocks = T // CONV_HALO
    lane, nc, rc = _conv_chunks(C, ts)

    def body(dcv_ref, dnx_ref, a_ref, g_ref, ap_ref, gp_ref, w_ref, du_ref, dw_ref, gbuf, dbuf, dglu, dw8):
        b, j = pl.program_id(0), pl.program_id(1)

        @pl.when((b == 0) & (j == 0))
        def _():
            dw8[...] = jnp.zeros_like(dw8)

        a = a_ref[...]
        sg = _sigmoid(g_ref[...])
        glu_prev = ap_ref[...] * _sigmoid(gp_ref[...]) * (j > 0).astype(F32)
        glu = a * sg
        dcur = dcv_ref[...]
        dnext = dnx_ref[...] * (j < nj - 1).astype(F32)
        for cc in range(nc):
            cols = slice(cc * lane, (cc + 1) * lane)
            gbuf[cc, 0:CONV_HALO, :] = glu_prev[:, cols]
            gbuf[cc, CONV_HALO:, :] = glu[:, cols]
            dbuf[cc, 0:ts, :] = dcur[:, cols]
            dbuf[cc, ts:, :] = dnext[:, cols]

        def chunk(cc, carry):
            for r0 in range(0, ts, rc):
                acc = jnp.zeros((rc, lane), F32)
                for kk in range(CONV_WIDTH):
                    d = CONV_WIDTH - 1 - kk
                    acc = acc + w_ref[cc, kk:kk + 1, :] * dbuf[cc, r0 + d:r0 + d + rc, :]
                dglu[cc, r0:r0 + rc, :] = acc
            for kk in range(CONV_WIDTH):
                d = CONV_WIDTH - 1 - kk
                p = jnp.zeros((rc, lane), F32)
                for r0 in range(0, ts, rc):
                    lo = CONV_HALO - d + r0
                    p = p + dbuf[cc, r0:r0 + rc, :] * gbuf[cc, lo:lo + rc, :]
                dw8[cc, kk * 8:(kk + 1) * 8, :] += jnp.sum(p.reshape(rc // 8, 8, lane), axis=0)
            return carry

        lax.fori_loop(0, nc, chunk, 0)
        for cc in range(nc):
            cols = slice(cc * lane, (cc + 1) * lane)
            dgl = dglu[cc]
            du_ref[:, cc * lane:(cc + 1) * lane] = (dgl * sg[:, cols]).astype(BF16)
            du_ref[:, C + cc * lane:C + (cc + 1) * lane] = (dgl * a[:, cols] * sg[:, cols] * (1.0 - sg[:, cols])).astype(BF16)

        @pl.when((b == B - 1) & (j == nj - 1))
        def _():
            dw_ref[...] = jnp.zeros_like(dw_ref)
            for kk in range(CONV_WIDTH):
                dw_ref[:, kk:kk + 1, :] = jnp.sum(dw8[:, kk * 8:(kk + 1) * 8, :], axis=1, keepdims=True)

    a_cur = pl.BlockSpec((ts, C), lambda b, j: (b * nj + j, 0))
    g_cur = pl.BlockSpec((ts, C), lambda b, j: (b * nj + j, 1))
    a_prev = pl.BlockSpec((CONV_HALO, C), lambda b, j: (jnp.maximum((b * nj + j) * hb - 1, 0), 0))
    g_prev = pl.BlockSpec((CONV_HALO, C), lambda b, j: (jnp.maximum((b * nj + j) * hb - 1, 0), 1))
    d_next = pl.BlockSpec((CONV_HALO, C), lambda b, j: (jnp.minimum((b * nj + j + 1) * hb, n_halo_blocks - 1), 0))
    wspec = pl.BlockSpec((nc, CONV_HALO, lane), lambda b, j: (0, 0, 0))
    return pl.pallas_call(
        body,
        name="conv_bwd",
        grid=(B, nj),
        out_shape=(jax.ShapeDtypeStruct((T, C2), BF16), jax.ShapeDtypeStruct((nc, CONV_HALO, lane), F32)),
        in_specs=[a_cur, d_next, a_cur, g_cur, a_prev, g_prev, wspec],
        out_specs=(pl.BlockSpec((ts, C2), lambda b, j: (b * nj + j, 0)), wspec),
        scratch_shapes=[
            pltpu.VMEM((nc, CONV_HALO + ts, lane), F32),
            pltpu.VMEM((nc, ts + CONV_HALO, lane), F32),
            pltpu.VMEM((nc, ts, lane), F32),
            pltpu.VMEM((nc, CONV_HALO * 8, lane), F32),
        ],
        compiler_params=_params("arbitrary", "arbitrary"),
    )(dcv, dcv, u, u, u, u, w3)


def _loss_head(x, tgt, gfin, y, gate, S, tm_pref=256):
    T, D = x.shape
    tm = _pick(S, tm_pref, 16)
    tpb = S // tm

    def body(x_ref, t_ref, g_ref, y_ref, gt_ref, dx_ref, loss_ref, dg_ref, dy_ref, dgate_ref):
        @pl.when(pl.program_id(0) == 0)
        def _():
            loss_ref[...] = jnp.zeros_like(loss_ref)
            dg_ref[...] = jnp.zeros_like(dg_ref)

        @pl.when(pl.program_id(0) % tpb == 0)
        def _():
            dgate_ref[...] = jnp.zeros_like(dgate_ref)

        xv = x_ref[...]
        g = g_ref[...]
        r = lax.rsqrt(jnp.mean(xv * xv, axis=-1, keepdims=True) + EPS)
        xhat = xv * r
        e = xhat * g - t_ref[...]
        row_loss = jnp.mean(e * e, axis=-1, keepdims=True)
        loss_ref[...] += 0.5 * jnp.sum(row_loss, axis=0, keepdims=True)
        dy = e * (1.0 / D)
        dg_ref[...] += jnp.sum(dy * xhat, axis=0, keepdims=True)
        dxhat = dy * g
        dx = r * (dxhat - xhat * jnp.mean(dxhat * xhat, axis=-1, keepdims=True))
        dx_ref[...] = dx
        dy_ref[...] = (dx * gt_ref[...]).astype(BF16)
        dgate_ref[...] += jnp.sum(dx * y_ref[...].astype(F32), axis=0, keepdims=True)

    row = pl.BlockSpec((tm, D), lambda i: (i, 0))
    vec = pl.BlockSpec((None, 1, D), lambda i: (i // tpb, 0, 0))
    one = pl.BlockSpec((1, D), lambda i: (0, 0))
    return pl.pallas_call(
        body,
        name="loss_head",
        grid=(T // tm,),
        out_shape=(
            jax.ShapeDtypeStruct((T, D), F32),
            jax.ShapeDtypeStruct((8, 128), F32),
            jax.ShapeDtypeStruct((1, D), F32),
            jax.ShapeDtypeStruct((T, D), BF16),
            jax.ShapeDtypeStruct((T // S, 1, D), F32),
        ),
        in_specs=[row, row, one, row, vec],
        out_specs=(row, pl.BlockSpec((8, 128), lambda i: (0, 0)), one, row, vec),
        compiler_params=_params("arbitrary"),
    )(x, tgt, gfin, y, gate)


def _adam(name, parts, w, m, v, tm_pref=256):
    L = len(parts)
    P, R, C = parts[0].shape
    tm = _pick(R, tm_pref, 16)
    tiles = R // tm
    c1 = 1.0 - ADAM_B1 ** ADAM_STEP
    c2 = 1.0 - ADAM_B2 ** ADAM_STEP

    def body(*refs):
        p_refs = refs[:L]
        w_ref, m_ref, v_ref, g_ref, d_ref, mo_ref, vo_ref = refs[L:]
        for l in range(L):
            @pl.when(pl.program_id(0) == l)
            def _():
                g = p_refs[l][0].astype(F32)
                for i in range(1, P):
                    g = g + p_refs[l][i].astype(F32)
                m_new = ADAM_B1 * m_ref[...] + (1.0 - ADAM_B1) * g
                v_new = ADAM_B2 * v_ref[...] + (1.0 - ADAM_B2) * (g * g)
                m_hat = m_new / c1
                v_hat = v_new / c2
                g_ref[...] = g
                d_ref[...] = -ADAM_LR * (m_hat / (jnp.sqrt(v_hat) + ADAM_EPS) + ADAM_WD * w_ref[...])
                mo_ref[...] = m_new
                vo_ref[...] = v_new

    def p_spec(l):
        return pl.BlockSpec((P, tm, C), lambda li, t: (0, jnp.where(li == l, t, jnp.where(li < l, 0, tiles - 1)), 0))

    row = pl.BlockSpec((tm, C), lambda li, t: (li * tiles + t, 0))
    return pl.pallas_call(
        body,
        name=name,
        grid=(L, tiles),
        out_shape=(jax.ShapeDtypeStruct((L * R, C), F32),) * 4,
        in_specs=[p_spec(l) for l in range(L)] + [row, row, row],
        out_specs=(row, row, row, row),
        compiler_params=_params("arbitrary", "arbitrary"),
    )(*parts, w, m, v)


BIG = ("w_qkv", "w_o", "w_pw1", "w_pw2", "w_up", "w_down")
SMALL_SHARDED = (("b_pw1", 1), ("w_dw", 2), ("b_dw", 1), ("conv_ln_g", 1), ("conv_ln_b", 1), ("b_pw2", 1))
SMALL_REPL = ("b_mod", "norm_mix", "norm_mlp", "b_qkv", "b_o", "sinks", "final_norm")
WEIGHTS = ("w_mod", "b_mod", "norm_mix", "norm_mlp", "w_qkv", "b_qkv", "w_o", "b_o", "sinks", "w_pw1", "b_pw1",
           "w_dw", "b_dw", "conv_ln_g", "conv_ln_b", "w_pw2", "b_pw2", "w_up", "w_down", "final_norm")


def _step(x, c, loss_target, W, M, V):
    B, S, D = x.shape
    T = B * S
    L = W["w_mod"].shape[0]
    n_mod = W["w_mod"].shape[2]
    me = 4 * lax.axis_index("x") + 2 * lax.axis_index("y") + lax.axis_index("c")
    q_dim = N_HEADS * HEAD_DIM
    kv_dim = N_KV_HEADS * HEAD_DIM

    Wb = {n: W[n].astype(BF16) for n in BIG}

    def shards(i):
        jm = i // 2
        first, last = ("w_qkv", "w_o") if i % 2 == 0 else ("w_pw1", "w_pw2")
        return [Wb[first][jm], Wb[last][jm], Wb["w_up"][i], Wb["w_down"][i]]

    def carried(res, payload):
        return res if payload else (res, [])

    small_names = [n for n, _ in SMALL_SHARDED]
    small_src, small_sizes = _pack([c] + [W[n] for n in small_names], F32, 0)
    got = _exchange("gather_first", [(small_src, True), (shards(0)[0], True)])
    small_parts = _unpack(got[0], small_sizes, [c.shape] + [W[n].shape for n in small_names], 1)
    c_all = small_parts[0].reshape(N_DEV * B, D)
    full = {n: _from_slots(p, d) for (n, d), p in zip(SMALL_SHARDED, small_parts[1:])}
    gathered = [[got[1], None, None, None]] + [None] * (L - 1)

    b_mod_mine = lax.dynamic_slice_in_dim(W["b_mod"], me * n_mod, n_mod, axis=1).reshape(L, 1, n_mod)
    mod_part = _mod_fwd(c_all, W["w_mod"].astype(BF16), b_mod_mine)
    mod_slots = mod_part.reshape(L, N_DEV, B, n_mod).transpose(1, 0, 2, 3).reshape(N_DEV, L * B, n_mod)
    mod_recv = _exchange("scatter_mod", [(mod_slots, False)])[0]
    mod = mod_recv.reshape(N_DEV, L, B, n_mod).transpose(1, 2, 0, 3).reshape(L, B, N_MOD, 1, D)

    w_dw3 = [_conv_weight_chunks(full["w_dw"][j], D) for j in range(full["w_dw"].shape[0])]

    xc = x.reshape(T, D)
    saved = []
    h1 = _normmod_fwd("normmod_mix_fwd", xc, W["norm_mix"][0][None], mod[0, :, 1], mod[0, :, 0], S)
    for i in range(L):
        jm = i // 2
        sh1, sc1, g1, sh2, sc2, g2 = (mod[i, :, t] for t in range(N_MOD))
        nxt = [[(s, True)] for s in shards(i + 1)] if i + 1 < L else [[]] * 4
        g_first, g_last, g_up, g_down = gathered[i]
        sv = {"x_in": xc, "h1": h1}
        if i % 2 == 0:
            w_qkv_full = _from_slots(g_first, 1)
            sv["w_first"] = w_qkv_full
            mine = [(s, True) for s in shards(0)[1:]] if i == 0 else []
            pay = nxt[0] + mine[:1]
            qkv, n_first = carried(_mm("qkv_fwd", h1, w_qkv_full, bias=W["b_qkv"][jm], payload=pay), pay)
            tab = _pair_tables(_attn_table(W["sinks"][jm]))
            mix_in, n_attn = carried(_attn_fwd(qkv, tab, B, S, payload=mine[1:]), mine[1:])
            if i == 0:
                g_last, g_up, g_down = n_first[-1], n_attn[0], n_attn[1]
            sv.update(qkv=qkv, tab=tab)
            b_out = W["b_o"][jm]
        else:
            sv["w_first"] = g_first
            u, n_first = carried(_mm("pw1_fwd", h1, g_first, w_form="nslots", bias=full["b_pw1"][jm], out_dtype=F32,
                                     payload=nxt[0]), nxt[0])
            cv, mix_in = _conv_fwd(u, w_dw3[jm], full["b_dw"][jm][None], full["conv_ln_g"][jm][None],
                                   full["conv_ln_b"][jm][None], S)
            sv.update(u=u, cv=cv)
            b_out = full["b_pw2"][jm]
        w_last = g_last.reshape(-1, D)
        w_down_full = g_down.reshape(-1, D)
        sv.update(mix_in=mix_in, w_last=w_last, g_up=g_up, w_down=w_down_full)
        (x1, y1, h2), n_last = carried(_mm("mix_out_fwd", mix_in, w_last, bias=b_out, epi="resid", x=xc, gate=g1,
                                           norm=(W["norm_mlp"][i][None], sc2, sh2), rows_per_batch=S,
                                           payload=nxt[1]), nxt[1])
        sv.update(y1=y1, x1=x1)
        up, n_up = carried(_mm("mlp_up_fwd", h2, g_up, w_form="nslots", epi="relu", payload=nxt[2]), nxt[2])
        norm_next = (W["norm_mix"][i + 1][None], mod[i + 1, :, 1], mod[i + 1, :, 0]) if i + 1 < L else None
        res, n_down = carried(_mm("mlp_down_fwd", up, w_down_full, a_sq=True, epi="resid", x=x1, gate=g2,
                                  norm=norm_next, rows_per_batch=S, payload=nxt[3]), nxt[3])
        x2, y2 = res[0], res[1]
        h1 = res[2] if i + 1 < L else None
        sv.update(h2=h2, up=up, y2=y2)
        saved.append(sv)
        xc = x2
        if i + 1 < L:
            gathered[i + 1] = [n_first[0], n_last[0], n_up[0], n_down[0]]

    dx, loss_blk, dgfin, dy2, dg2 = _loss_head(xc, loss_target.reshape(T, D), W["final_norm"][None],
                                               saved[L - 1]["y2"], mod[L - 1, :, 5], S)
    loss = lax.psum(loss_blk[0, 0], ("x", "y", "c"))

    G = {"final_norm": dgfin.reshape(D)}
    dmod_layers = [None] * L
    small_grads = ("norm_mix", "norm_mlp", "b_qkv", "b_o", "sinks", "b_pw1", "w_dw", "b_dw", "conv_ln_g", "conv_ln_b",
                   "b_pw2")
    acc = {n: [None] * W[n].shape[0] for n in small_grads}
    reduced = {n: [None] * W[n].shape[0] for n in BIG}

    waiting_up = None
    for i in reversed(range(L)):
        jm = i // 2
        sv = saved[i]
        sh1, sc1, g1, sh2, sc2, g2 = (mod[i, :, t] for t in range(N_MOD))
        pay = [(waiting_up[1], False)] if waiting_up else []
        gw_down, r = carried(_mm_tn("w_down_grad", sv["up"], dy2, a_sq=True, out_form="kslots", tt_pref=4096,
                                    payload=pay), pay)
        if waiting_up:
            reduced["w_up"][waiting_up[0]] = r[0]
        du = _mm("mlp_down_bwd", dy2, sv["w_down"], w_form="full_t", epi="relu2d", u=sv["up"])
        gw_up, r = _mm_tn("w_up_grad", sv["h2"], du, out_form="nslots", tt_pref=4096, payload=[(gw_down, False)])
        reduced["w_down"][i] = r[0]
        waiting_up = (i, gw_up) if i > 0 else None
        pay = [] if i > 0 else [(gw_up, False)]
        (dx, dsh2, dsc2, dgn, dy1, dg1, dy1_sum), r = carried(_mm(
            "mlp_up_bwd", du, sv["g_up"], w_form="kslots_t", epi="normbwd", rows_per_batch=S,
            nb=dict(x=sv["x1"], dx_in=dx, gnorm=W["norm_mlp"][i][None], sc=sc2, gate=(sv["y1"], g1)),
            payload=pay), pay)
        if i == 0:
            reduced["w_up"][0] = r[0]
        acc["norm_mlp"][i] = dgn.reshape(D)
        gw_last = _mm_tn("mix_out_grad", sv["mix_in"], dy1)
        gw_last = gw_last.reshape((N_DEV, -1) + gw_last.shape[1:]).astype(BF16)
        below = (saved[i - 1]["y2"], mod[i - 1, :, 5]) if i > 0 else None
        nb_mix = dict(x=sv["x_in"], dx_in=dx, gnorm=W["norm_mix"][i][None], sc=sc1, gate=below)
        if i % 2 == 0:
            acc["b_o"][jm] = dy1_sum.reshape(D)
            do, r = _mm("attn_out_bwd", dy1, sv["w_last"], w_form="full_t", payload=[(gw_last, False)])
            reduced["w_o"][jm] = r[0]
            dq, dkv, dsk = _attn_bwd(sv["qkv"], sv["mix_in"], do, sv["tab"], B, S)
            acc["sinks"][jm] = jnp.sum(dsk[..., 0], axis=0).transpose(0, 2, 1).reshape(N_HEADS)
            dqkv = jnp.concatenate([dq, dkv], axis=1)
            gw_qkv, db_qkv = _mm_tn("w_qkv_grad", sv["h1"], dqkv, colsum=True)
            acc["b_qkv"][jm] = db_qkv.reshape(-1)
            gw_qkv = _to_slots(gw_qkv, 1).astype(BF16)
            res, r = _mm("qkv_bwd", dqkv, sv["w_first"], w_form="full_t", epi="normbwd", nb=nb_mix, rows_per_batch=S,
                         payload=[(gw_qkv, False)])
            reduced["w_qkv"][jm] = r[0]
        else:
            acc["b_pw2"][jm] = dy1_sum.reshape(D)
            ds, r = _mm("pw2_bwd", dy1, sv["w_last"], w_form="full_t", out_dtype=F32, payload=[(gw_last, False)])
            reduced["w_pw2"][jm] = r[0]
            dcv, dlg, dlb, dbdw = _lnsilu_bwd(ds, sv["cv"], full["conv_ln_g"][jm][None], full["conv_ln_b"][jm][None])
            acc["conv_ln_g"][jm], acc["conv_ln_b"][jm], acc["b_dw"][jm] = dlg.reshape(-1), dlb.reshape(-1), dbdw.reshape(-1)
            du1, dwdw = _conv_bwd(dcv, sv["u"], w_dw3[jm], S)
            acc["w_dw"][jm] = dwdw.transpose(1, 0, 2).reshape(CONV_HALO, D)[:CONV_WIDTH]
            gw_pw1, db_pw1 = _mm_tn("w_pw1_grad", sv["h1"], du1, out_form="nslots", colsum=True, tt_pref=4096)
            acc["b_pw1"][jm] = db_pw1.reshape(-1)
            res, r = _mm("pw1_bwd", du1, sv["w_first"], w_form="kslots_t", epi="normbwd", nb=nb_mix, rows_per_batch=S,
                         payload=[(gw_pw1, False)])
            reduced["w_pw1"][jm] = r[0]
        dx, dsh1, dsc1, dgn = res[:4]
        acc["norm_mix"][i] = dgn.reshape(D)
        dmod_layers[i] = jnp.concatenate([dsh1, dsc1, dg1, dsh2, dsc2, dg2], axis=1).reshape(B, N_MOD * D)
        if i > 0:
            dy2, dg2 = res[4], res[5]
    grad_x = dx.reshape(B, S, D)
    for n, parts in acc.items():
        G[n] = jnp.stack(parts)

    dmod = jnp.stack(dmod_layers)
    dmod_slots = dmod.reshape(L, B, N_DEV, n_mod).transpose(2, 0, 1, 3).reshape(N_DEV, L * B, n_mod)
    dmod_recv = _exchange("gather_dmod", [(dmod_slots, False)])[0]
    dmod_all = dmod_recv.reshape(N_DEV, L, B, n_mod).transpose(1, 0, 2, 3).reshape(L, N_DEV * B, n_mod)
    g_w_mod, db_mod_mine = _mod_bwd(c_all, dmod_all)
    G["b_mod"] = lax.dynamic_update_slice_in_dim(jnp.zeros_like(W["b_mod"]), db_mod_mine.reshape(L, n_mod),
                                                 me * n_mod, axis=1)

    small_items = [jnp.broadcast_to(G[n][None], (N_DEV,) + G[n].shape) for n in SMALL_REPL]
    small_items += [_to_slots(G[n], d) for n, d in SMALL_SHARDED]
    small_slots, small_sizes2 = _pack(small_items, F32, 1)
    small_recv = _exchange("reduce_small", [(small_slots, False)])[0]

    out = {}

    def run_adam(name, parts, names):
        shapes = [W[n].shape for n in names]
        wp, sizes = _pack([W[n] for n in names], F32, 0)
        mp, _ = _pack([M[n] for n in names], F32, 0)
        vp, _ = _pack([V[n] for n in names], F32, 0)
        res = _adam(name, [parts], wp, mp, vp)
        for kind, buf in zip(("grad", "delta", "new_m", "new_v"), res):
            for n, a in zip(names, _unpack(buf, sizes, shapes, 0)):
                out[kind + "_" + n] = a

    for n in BIG + ("w_mod",):
        cols = W[n].shape[-1]
        if n == "w_mod":
            parts = [g_w_mod.reshape(1, -1, cols)]
        else:
            parts = [r.reshape(N_DEV, -1, cols) for r in reduced[n]]
        res = _adam("adam_" + n, parts, W[n].reshape(-1, cols), M[n].reshape(-1, cols), V[n].reshape(-1, cols))
        for kind, buf in zip(("grad", "delta", "new_m", "new_v"), res):
            out[kind + "_" + n] = buf.reshape(W[n].shape)
    run_adam("adam_small", small_recv, list(SMALL_REPL) + [n for n, _ in SMALL_SHARDED])

    res = [loss, grad_x]
    for kind in ("grad", "delta", "new_m", "new_v"):
        res += [out[kind + "_" + n] for n in WEIGHTS]
    return tuple(res)


def kernel(x, c, w_mod, b_mod, norm_mix, norm_mlp, w_qkv, b_qkv, w_o, b_o, sinks, w_pw1, b_pw1, w_dw, b_dw, conv_ln_g, conv_ln_b, w_pw2, b_pw2, w_up, w_down, final_norm, loss_target, m_w_mod, m_b_mod, m_norm_mix, m_norm_mlp, m_w_qkv, m_b_qkv, m_w_o, m_b_o, m_sinks, m_w_pw1, m_b_pw1, m_w_dw, m_b_dw, m_conv_ln_g, m_conv_ln_b, m_w_pw2, m_b_pw2, m_w_up, m_w_down, m_final_norm, v_w_mod, v_b_mod, v_norm_mix, v_norm_mlp, v_w_qkv, v_b_qkv, v_w_o, v_b_o, v_sinks, v_w_pw1, v_b_pw1, v_w_dw, v_b_dw, v_conv_ln_g, v_conv_ln_b, v_w_pw2, v_b_pw2, v_w_up, v_w_down, v_final_norm):
    W = dict(w_mod=w_mod, b_mod=b_mod, norm_mix=norm_mix, norm_mlp=norm_mlp, w_qkv=w_qkv, b_qkv=b_qkv, w_o=w_o,
             b_o=b_o, sinks=sinks, w_pw1=w_pw1, b_pw1=b_pw1, w_dw=w_dw, b_dw=b_dw, conv_ln_g=conv_ln_g,
             conv_ln_b=conv_ln_b, w_pw2=w_pw2, b_pw2=b_pw2, w_up=w_up, w_down=w_down, final_norm=final_norm)
    M = dict(w_mod=m_w_mod, b_mod=m_b_mod, norm_mix=m_norm_mix, norm_mlp=m_norm_mlp, w_qkv=m_w_qkv, b_qkv=m_b_qkv,
             w_o=m_w_o, b_o=m_b_o, sinks=m_sinks, w_pw1=m_w_pw1, b_pw1=m_b_pw1, w_dw=m_w_dw, b_dw=m_b_dw,
             conv_ln_g=m_conv_ln_g, conv_ln_b=m_conv_ln_b, w_pw2=m_w_pw2, b_pw2=m_b_pw2, w_up=m_w_up,
             w_down=m_w_down, final_norm=m_final_norm)
    V = dict(w_mod=v_w_mod, b_mod=v_b_mod, norm_mix=v_norm_mix, norm_mlp=v_norm_mlp, w_qkv=v_w_qkv, b_qkv=v_b_qkv,
             w_o=v_w_o, b_o=v_b_o, sinks=v_sinks, w_pw1=v_w_pw1, b_pw1=v_b_pw1, w_dw=v_w_dw, b_dw=v_b_dw,
             conv_ln_g=v_conv_ln_g, conv_ln_b=v_conv_ln_b, w_pw2=v_w_pw2, b_pw2=v_b_pw2, w_up=v_w_up,
             w_down=v_w_down, final_norm=v_final_norm)
    return _step(x, c, loss_target, W, M, V)
```

```python
import functools

import numpy as np
import jax
import jax.numpy as jnp
from jax import lax
from jax.experimental import pallas as pl
from jax.experimental.pallas import tpu as pltpu

F32 = jnp.float32
BF16 = jnp.bfloat16

N_DEV = 8
N_HEADS = 16
N_KV_HEADS = 2
HEAD_DIM = 64
GROUP = N_HEADS // N_KV_HEADS
ATT_BLOCK = 128
CONV_WIDTH = 31
CONV_HALO = 32
N_MOD = 6
EPS = 1e-6
ADAM_LR = 0.001
ADAM_B1 = 0.9
ADAM_B2 = 0.999
ADAM_EPS = 1e-08
ADAM_WD = 0.01
ADAM_STEP = 10
NEG_BIG = -1e30
PACK_COLS = 1024
VMEM_LIMIT_BYTES = 56 * 1024 * 1024
MESH_ID = pl.DeviceIdType.MESH


def _params(*sem):
    return pltpu.CompilerParams(dimension_semantics=sem, vmem_limit_bytes=VMEM_LIMIT_BYTES)


def _pick(n, pref, mult=8):
    if n <= pref:
        return n
    for t in range(pref, 0, -1):
        if n % t == 0 and t % mult == 0:
            return t
    return n


def _sigmoid(z):
    return 0.5 * jnp.tanh(0.5 * z) + 0.5


def _payload_layout(payload):
    n = len(payload)
    out_shapes = [jax.ShapeDtypeStruct((N_DEV,) + tuple(a.shape if bc else a.shape[1:]), a.dtype) for a, bc in payload]
    hbm = pl.BlockSpec(memory_space=pl.ANY)
    scratch = [pltpu.SemaphoreType.DMA((n * (N_DEV - 1),)), pltpu.SemaphoreType.DMA((n * (N_DEV - 1),)),
               pltpu.SemaphoreType.DMA((n,))]
    return [a for a, _ in payload], [hbm] * n, out_shapes, [hbm] * n, scratch


class _Plan:
    def __init__(self, bcasts, src_refs, dst_refs, send_sems, recv_sems, local_sems):
        x, y, c = lax.axis_index("x"), lax.axis_index("y"), lax.axis_index("c")
        me = 4 * x + 2 * y + c
        self.first, self.landed, self.relay, self.local = [], [], [], []
        for t, (bc, s_ref, d_ref) in enumerate(zip(bcasts, src_refs, dst_refs)):
            def remote(k, src, slot, to):
                sem = t * (N_DEV - 1) + k
                return pltpu.make_async_remote_copy(src_ref=src, dst_ref=d_ref.at[slot], send_sem=send_sems.at[sem],
                                                    recv_sem=recv_sems.at[sem], device_id=to, device_id_type=MESH_ID)
            if bc:
                chips = [(1 - x, y), (x, 1 - y), (1 - x, 1 - y)]
                self.first.append(remote(0, s_ref, me, (x, y, 1 - c)))
                for j, (px, py) in enumerate(chips):
                    cp = remote(1 + j, s_ref, me, (px, py, c))
                    self.first.append(cp)
                    self.landed.append(cp)
                    theirs = 4 * px + 2 * py + c
                    self.relay.append(remote(4 + j, d_ref.at[theirs], theirs, (x, y, 1 - c)))
                self.local.append(pltpu.make_async_copy(s_ref, d_ref.at[me], local_sems.at[t]))
            else:
                for k in range(1, N_DEV):
                    px = 1 - x if (k >> 2) & 1 else x
                    py = 1 - y if (k >> 1) & 1 else y
                    pc = 1 - c if k & 1 else c
                    self.first.append(remote(k - 1, s_ref.at[4 * px + 2 * py + pc], me, (px, py, pc)))
                self.local.append(pltpu.make_async_copy(s_ref.at[me], d_ref.at[me], local_sems.at[t]))

    def start(self):
        for cp in self.first + self.local:
            cp.start()

    def pass_on(self):
        for cp in self.landed:
            cp.wait_recv()
        for cp in self.relay:
            cp.start()

    def finish(self):
        for cp in self.first:
            cp.wait_send()
            if not any(cp is l for l in self.landed):
                cp.wait_recv()
        for cp in self.relay:
            cp.wait()
        for cp in self.local:
            cp.wait()


def _exchange(name, payload):
    n = len(payload)
    bcasts = [bc for _, bc in payload]
    arrays, in_specs, out_shapes, out_specs, scratch = _payload_layout(payload)

    def body(*refs):
        plan = _Plan(bcasts, refs[:n], refs[n:2 * n], *refs[2 * n:])
        plan.start()
        plan.pass_on()
        plan.finish()

    return pl.pallas_call(
        body,
        name=name,
        out_shape=tuple(out_shapes),
        in_specs=in_specs,
        out_specs=tuple(out_specs),
        scratch_shapes=scratch,
    )(*arrays)


def _pack(arrays, dtype, lead):
    lead_shape = arrays[0].shape[:lead]
    flat = [a.astype(dtype).reshape(lead_shape + (-1,)) for a in arrays]
    sizes = [f.shape[-1] for f in flat]
    total = sum(sizes)
    chunk = 16 * PACK_COLS
    padded = -(-total // chunk) * chunk
    if padded > total:
        flat.append(jnp.zeros(lead_shape + (padded - total,), dtype))
    buf = jnp.concatenate(flat, axis=-1)
    return buf.reshape(lead_shape + (padded // PACK_COLS, PACK_COLS)), sizes


def _unpack(buf, sizes, shapes, lead):
    lead_shape = buf.shape[:lead]
    flat = buf.reshape(lead_shape + (-1,))
    out, off = [], 0
    for n, shp in zip(sizes, shapes):
        out.append(lax.slice_in_dim(flat, off, off + n, axis=lead).reshape(lead_shape + tuple(shp)))
        off += n
    return out


def _to_slots(a, dim):
    shp = a.shape
    a = a.reshape(shp[:dim] + (N_DEV, shp[dim] // N_DEV) + shp[dim + 1:])
    return jnp.moveaxis(a, dim, 0)


def _from_slots(a, dim):
    a = jnp.moveaxis(a, 0, dim)
    shp = a.shape
    return a.reshape(shp[:dim] + (shp[dim] * shp[dim + 1],) + shp[dim + 2:])


def _mod_fwd(c_all, w, b):
    L, D, n = w.shape
    M = c_all.shape[0]

    def body(c_ref, w_ref, b_ref, o_ref):
        cv = c_ref[...]
        cs = (cv * _sigmoid(cv)).astype(BF16)
        o_ref[...] = jnp.dot(cs, w_ref[...], preferred_element_type=F32) + b_ref[...]

    return pl.pallas_call(
        body,
        name="mod_fwd",
        grid=(L,),
        out_shape=jax.ShapeDtypeStruct((L, M, n), F32),
        in_specs=[
            pl.BlockSpec((M, D), lambda l: (0, 0)),
            pl.BlockSpec((None, D, n), lambda l: (l, 0, 0)),
            pl.BlockSpec((None, 1, n), lambda l: (l, 0, 0)),
        ],
        out_specs=pl.BlockSpec((None, M, n), lambda l: (l, 0, 0)),
        compiler_params=_params("arbitrary"),
    )(c_all, w, b)


def _mod_bwd(c_all, dmod_all):
    L, M, n = dmod_all.shape
    D = c_all.shape[1]

    def body(c_ref, d_ref, dw_ref, db_ref):
        cv = c_ref[...]
        cs = (cv * _sigmoid(cv)).astype(BF16)
        d = d_ref[...]
        dw_ref[...] = lax.dot_general(cs, d.astype(BF16), (((0,), (0,)), ((), ())), preferred_element_type=F32)
        db_ref[...] = jnp.sum(d, axis=0, keepdims=True)

    return pl.pallas_call(
        body,
        name="mod_bwd",
        grid=(L,),
        out_shape=(jax.ShapeDtypeStruct((L, D, n), F32), jax.ShapeDtypeStruct((L, 1, n), F32)),
        in_specs=[
            pl.BlockSpec((M, D), lambda l: (0, 0)),
            pl.BlockSpec((None, M, n), lambda l: (l, 0, 0)),
        ],
        out_specs=(
            pl.BlockSpec((None, D, n), lambda l: (l, 0, 0)),
            pl.BlockSpec((None, 1, n), lambda l: (l, 0, 0)),
        ),
        compiler_params=_params("arbitrary"),
    )(c_all, dmod_all)


def _mm(name, a, w, *, w_form="full", out_dtype=BF16, bias=None, a_sq=False, epi=None, x=None, gate=None, u=None,
        norm=None, nb=None, rows_per_batch=None, tm_pref=512, payload=()):
    M, K = a.shape
    if w_form == "full":
        N = w.shape[1]
        nc = _pick(N, 1024, 128)
    elif w_form == "full_t":
        N = w.shape[0]
        nc = _pick(N, 1024, 128)
    elif w_form == "nslots":
        nc = w.shape[2]
        N = N_DEV * nc
    else:
        N = w.shape[1]
        nc = N
    ks = K // N_DEV
    n_chunks = N // nc
    tm = _pick(M if rows_per_batch is None else rows_per_batch, tm_pref, 16)
    steps = M // tm
    relay_step = (3 * steps) // 4
    nb_gate = nb is not None and nb.get("gate") is not None
    tpb_nb = rows_per_batch // tm if epi == "normbwd" else 1
    has_bias = bias is not None
    n_pay = len(payload)
    bcasts = [bc for _, bc in payload]
    nt = (((1,), (1,)), ((), ()))


    def body(*refs):
        it = iter(refs)
        a_ref = next(it)
        w_ref = next(it)
        b_ref = next(it) if has_bias else None
        x_ref = next(it) if epi == "resid" else None
        g_ref = next(it) if epi == "resid" else None
        gn_ref, sc_ref, sh_ref = (next(it), next(it), next(it)) if norm is not None else (None, None, None)
        u_ref = next(it) if epi == "relu2d" else None
        if epi == "normbwd":
            nx_ref, ndx_ref, ngn_ref, nsc_ref = next(it), next(it), next(it), next(it)
            ny_ref, ngt_ref = (next(it), next(it)) if nb_gate else (None, None)
        pay_src = [next(it) for _ in range(n_pay)]
        o_ref = next(it)
        y_ref = next(it) if epi == "resid" else None
        h_ref = next(it) if norm is not None else None
        if epi == "normbwd":
            dsh_ref, dsc_ref, dgn_ref = next(it), next(it), next(it)
            dy_ref, dgate_ref, cs_ref = (next(it), next(it), next(it)) if nb_gate else (None, None, None)
        pay_dst = [next(it) for _ in range(n_pay)]
        sems = list(it)

        if epi == "normbwd":
            @pl.when(pl.program_id(0) % tpb_nb == 0)
            def _():
                dsh_ref[...] = jnp.zeros_like(dsh_ref)
                dsc_ref[...] = jnp.zeros_like(dsc_ref)
                if nb_gate:
                    dgate_ref[...] = jnp.zeros_like(dgate_ref)

            @pl.when(pl.program_id(0) == 0)
            def _():
                dgn_ref[...] = jnp.zeros_like(dgn_ref)
                if nb_gate:
                    cs_ref[...] = jnp.zeros_like(cs_ref)

        if n_pay:
            @pl.when(pl.program_id(0) == 0)
            def _():
                _Plan(bcasts, pay_src, pay_dst, *sems).start()

            @pl.when(pl.program_id(0) == relay_step)
            def _():
                _Plan(bcasts, pay_src, pay_dst, *sems).pass_on()

        for rows in (slice(0, tm),):
            av = None if w_form == "kslots_t" else a_ref[rows, :]
            if a_sq:
                av = av * av
            for ci in range(n_chunks):
                cols = slice(ci * nc, (ci + 1) * nc)
                if w_form == "full":
                    acc = jnp.dot(av, w_ref[:, cols], preferred_element_type=F32)
                elif w_form == "full_t":
                    acc = lax.dot_general(av, w_ref[cols, :], nt, preferred_element_type=F32)
                elif w_form == "nslots":
                    acc = jnp.dot(av, w_ref[ci], preferred_element_type=F32)
                else:
                    acc = lax.dot_general(a_ref[rows, 0:ks], w_ref[0], nt, preferred_element_type=F32)
                    for j in range(1, N_DEV):
                        acc = acc + lax.dot_general(a_ref[rows, j * ks:(j + 1) * ks], w_ref[j], nt,
                                                    preferred_element_type=F32)
                if has_bias:
                    acc = acc + b_ref[:, cols]
                if epi == "resid":
                    xn = x_ref[rows, cols] + g_ref[:, cols] * acc
                    o_ref[rows, cols] = xn
                    y_ref[rows, cols] = acc.astype(BF16)
                    if norm is not None:
                        r = lax.rsqrt(jnp.mean(xn * xn, axis=-1, keepdims=True) + EPS)
                        h_ref[rows, :] = ((xn * r * gn_ref[...]) * (1.0 + sc_ref[...]) + sh_ref[...]).astype(BF16)
                elif epi == "relu":
                    o_ref[rows, cols] = jnp.maximum(acc, 0.0).astype(out_dtype)
                elif epi == "relu2d":
                    o_ref[rows, cols] = (acc * (2.0 * u_ref[rows, cols].astype(F32))).astype(out_dtype)
                elif epi == "normbwd":
                    xv = nx_ref[rows, :]
                    gn = ngn_ref[...]
                    r = lax.rsqrt(jnp.mean(xv * xv, axis=-1, keepdims=True) + EPS)
                    xhat = xv * r
                    dsh_ref[...] += jnp.sum(acc, axis=0, keepdims=True)
                    dsc_ref[...] += jnp.sum(acc * (xhat * gn), axis=0, keepdims=True)
                    dn = acc * (1.0 + nsc_ref[...])
                    dgn_ref[...] += jnp.sum(dn * xhat, axis=0, keepdims=True)
                    dxhat = dn * gn
                    dx = ndx_ref[rows, :] + r * (dxhat - xhat * jnp.mean(dxhat * xhat, axis=-1, keepdims=True))
                    o_ref[rows, :] = dx
                    if nb_gate:
                        dy = dx * ngt_ref[...]
                        dy_ref[rows, :] = dy.astype(BF16)
                        dgate_ref[...] += jnp.sum(dx * ny_ref[rows, :].astype(F32), axis=0, keepdims=True)
                        cs_ref[...] += jnp.sum(dy, axis=0, keepdims=True)
                else:
                    o_ref[rows, cols] = acc.astype(out_dtype)

        if n_pay:
            @pl.when(pl.program_id(0) == steps - 1)
            def _():
                _Plan(bcasts, pay_src, pay_dst, *sems).finish()

    args = [a, w]
    w_block = w.shape
    specs = [pl.BlockSpec((tm, K), lambda i: (i, 0)), pl.BlockSpec(w_block, lambda i: (0,) * len(w_block))]
    if has_bias:
        args.append(bias.reshape(1, N).astype(F32))
        specs.append(pl.BlockSpec((1, N), lambda i: (0, 0)))
    row_spec = pl.BlockSpec((tm, N), lambda i: (i, 0))
    if epi == "resid":
        tpb = rows_per_batch // tm
        vec_spec = pl.BlockSpec((None, 1, N), lambda i: (i // tpb, 0, 0))
        args += [x, gate]
        specs += [row_spec, vec_spec]
        out_shape = [jax.ShapeDtypeStruct((M, N), F32), jax.ShapeDtypeStruct((M, N), BF16)]
        out_specs = [row_spec, row_spec]
        if norm is not None:
            assert n_chunks == 1
            args += list(norm)
            specs += [pl.BlockSpec((1, N), lambda i: (0, 0)), vec_spec, vec_spec]
            out_shape.append(jax.ShapeDtypeStruct((M, N), BF16))
            out_specs.append(row_spec)
    elif epi == "normbwd":
        assert n_chunks == 1
        vec_spec = pl.BlockSpec((None, 1, N), lambda i: (i // tpb_nb, 0, 0))
        one_spec = pl.BlockSpec((1, N), lambda i: (0, 0))
        nbat = M // rows_per_batch
        f_vec, f_one = jax.ShapeDtypeStruct((nbat, 1, N), F32), jax.ShapeDtypeStruct((1, N), F32)
        args += [nb["x"], nb["dx_in"], nb["gnorm"], nb["sc"]]
        specs += [row_spec, row_spec, one_spec, vec_spec]
        out_shape = [jax.ShapeDtypeStruct((M, N), F32), f_vec, f_vec, f_one]
        out_specs = [row_spec, vec_spec, vec_spec, one_spec]
        if nb_gate:
            args += list(nb["gate"])
            specs += [row_spec, vec_spec]
            out_shape += [jax.ShapeDtypeStruct((M, N), BF16), f_vec, f_one]
            out_specs += [row_spec, vec_spec, one_spec]
    else:
        if epi == "relu2d":
            args.append(u)
            specs.append(row_spec)
        out_shape = [jax.ShapeDtypeStruct((M, N), out_dtype)]
        out_specs = [row_spec]
    scratch = []
    if n_pay:
        p_arrays, p_in, p_shapes, p_out, scratch = _payload_layout(payload)
        args += p_arrays
        specs += p_in
        out_shape += p_shapes
        out_specs += p_out
    res = pl.pallas_call(
        body,
        name=name,
        grid=(steps,),
        out_shape=tuple(out_shape),
        in_specs=specs,
        out_specs=tuple(out_specs),
        scratch_shapes=scratch,
        compiler_params=_params("arbitrary" if (n_pay or epi == "normbwd") else "parallel"),
    )(*args)
    if epi == "resid":
        n_own = 3 if norm is not None else 2
    elif epi == "normbwd":
        n_own = 7 if nb_gate else 4
    else:
        n_own = 1
    own = res[0] if n_own == 1 else tuple(res[:n_own])
    return (own, list(res[n_own:])) if n_pay else own


def _mm_tn(name, a, b, *, a_sq=False, out_form="full", halves=False, colsum=False, tt_pref=1024, tk_pref=1024,
           tn_pref=1024, payload=()):
    T, K = a.shape
    N = b.shape[1]
    tt = _pick(T, tt_pref, 16)
    tk = K // N_DEV if out_form == "kslots" else _pick(K, tk_pref, 128)
    tn = N // N_DEV if out_form == "nslots" else _pick(N, tn_pref, 128)
    nt_steps = T // tt
    grid = (K // tk, N // tn, nt_steps)
    slots = out_form != "full"
    assert not colsum or K == tk
    n_pay = len(payload)
    bcasts = [bc for _, bc in payload]

    def body(*refs):
        it = iter(refs)
        a_ref, b_ref = next(it), next(it)
        pay_src = [next(it) for _ in range(n_pay)]
        o_ref = next(it)
        o2_ref = next(it) if halves else None
        cs_ref = next(it) if colsum else None
        pay_dst = [next(it) for _ in range(n_pay)]
        acc = next(it) if slots else o_ref
        sems = list(it)
        step = (pl.program_id(0) * grid[1] + pl.program_id(1)) * grid[2] + pl.program_id(2)

        if n_pay:
            @pl.when(step == 0)
            def _():
                _Plan(bcasts, pay_src, pay_dst, *sems).start()

        @pl.when(pl.program_id(2) == 0)
        def _():
            acc[...] = jnp.zeros_like(acc)
            if colsum:
                cs_ref[...] = jnp.zeros_like(cs_ref)

        bv = b_ref[...]
        av = a_ref[...]
        if a_sq:
            av = av * av
        acc[...] += lax.dot_general(av, bv, (((0,), (0,)), ((), ())), preferred_element_type=F32)
        if colsum:
            cs_ref[...] += jnp.sum(bv.astype(F32), axis=0, keepdims=True)

        if slots:
            @pl.when(pl.program_id(2) == nt_steps - 1)
            def _():
                if halves:
                    o_ref[...] = acc[0:tk // 2, :].astype(o_ref.dtype)
                    o2_ref[...] = acc[tk // 2:, :].astype(o2_ref.dtype)
                else:
                    o_ref[...] = acc[...].astype(o_ref.dtype)

        if n_pay:
            @pl.when(step == grid[0] * grid[1] * grid[2] - 1)
            def _():
                plan = _Plan(bcasts, pay_src, pay_dst, *sems)
                plan.pass_on()
                plan.finish()

    if out_form == "full":
        out_shape = [jax.ShapeDtypeStruct((K, N), F32)]
        out_specs = [pl.BlockSpec((tk, tn), lambda k, n, t: (k, n))]
    else:
        assert not halves or out_form == "kslots" or K == tk
        rows = tk // 2 if halves else tk
        if out_form == "nslots":
            shape, spec = (N_DEV, K // tk * rows, tn), pl.BlockSpec((None, rows, tn), lambda k, n, t: (n, k, 0))
        else:
            shape, spec = (N_DEV, rows, N), pl.BlockSpec((None, rows, tn), lambda k, n, t: (k, 0, n))
        out_shape = [jax.ShapeDtypeStruct(shape, BF16)] * (2 if halves else 1)
        out_specs = [spec] * (2 if halves else 1)
    if colsum:
        out_shape.append(jax.ShapeDtypeStruct((1, N), F32))
        out_specs.append(pl.BlockSpec((1, tn), lambda k, n, t: (0, n)))
    args = [a, b]
    in_specs = [pl.BlockSpec((tt, tk), lambda k, n, t: (t, k)), pl.BlockSpec((tt, tn), lambda k, n, t: (t, n))]
    scratch = [pltpu.VMEM((tk, tn), F32)] if slots else []
    n_own = len(out_shape)
    if n_pay:
        p_arrays, p_in, p_shapes, p_out, p_scratch = _payload_layout(payload)
        args += p_arrays
        in_specs += p_in
        out_shape += p_shapes
        out_specs += p_out
        scratch += p_scratch
    res = pl.pallas_call(
        body,
        name=name,
        grid=grid,
        out_shape=tuple(out_shape),
        in_specs=in_specs,
        out_specs=tuple(out_specs),
        scratch_shapes=scratch,
        compiler_params=_params(*(("arbitrary",) * 3 if n_pay else ("parallel", "parallel", "arbitrary"))),
    )(*args)
    own = res[0] if n_own == 1 else tuple(res[:n_own])
    return (own, list(res[n_own:])) if n_pay else own


def _normmod_fwd(name, x, gnorm, sc, sh, S, tm_pref=512):
    T, D = x.shape
    tm = _pick(S, tm_pref, 16)
    tpb = S // tm

    def body(x_ref, g_ref, sc_ref, sh_ref, o_ref):
        xv = x_ref[...]
        r = lax.rsqrt(jnp.mean(xv * xv, axis=-1, keepdims=True) + EPS)
        n = xv * r * g_ref[...]
        o_ref[...] = (n * (1.0 + sc_ref[...]) + sh_ref[...]).astype(BF16)

    row = pl.BlockSpec((tm, D), lambda i: (i, 0))
    vec = pl.BlockSpec((None, 1, D), lambda i: (i // tpb, 0, 0))
    return pl.pallas_call(
        body,
        name=name,
        grid=(T // tm,),
        out_shape=jax.ShapeDtypeStruct((T, D), BF16),
        in_specs=[row, pl.BlockSpec((1, D), lambda i: (0, 0)), vec, vec],
        out_specs=row,
        compiler_params=_params("parallel"),
    )(x, gnorm, sc, sh)


ATT_ROWS = GROUP * ATT_BLOCK
ATT_SPAN = 2 * ATT_BLOCK
ATT_SCALE = HEAD_DIM ** -0.5


def _attn_table(sinks):
    slopes = jnp.asarray(np.array([2.0 ** (-8.0 * (h + 1) / N_HEADS) for h in range(N_HEADS)], np.float32))
    r = jnp.arange(ATT_BLOCK)[:, None]
    cc = jnp.arange(ATT_SPAN)[None, :]
    dist = r + ATT_BLOCK - cc
    ok = (dist >= 0) & (dist < ATT_BLOCK)
    tab = jnp.where(ok[None], -slopes[:, None, None] * dist.astype(F32)[None], NEG_BIG)
    tab = jnp.where((cc == 0)[None], sinks.astype(F32)[:, None, None], tab)
    return tab.reshape(N_KV_HEADS, ATT_ROWS, ATT_SPAN)


KV_COLS = N_KV_HEADS * HEAD_DIM


PAIR = 2 * HEAD_DIM


def _pair_tables(tab):
    hkv, rows, span = tab.shape
    pairs = rows // (2 * ATT_BLOCK)
    return tab.reshape(hkv, pairs, 2, ATT_BLOCK, span).transpose(0, 2, 1, 3, 4).reshape(hkv, 2, pairs * ATT_BLOCK, span)


def _stack_pairs(ref, pairs):
    return jnp.concatenate([ref[:, j * PAIR:(j + 1) * PAIR] for j in range(pairs)], axis=0)


def _unstack_pairs(v, pairs):
    return jnp.concatenate([v[j * ATT_BLOCK:(j + 1) * ATT_BLOCK, :] for j in range(pairs)], axis=1)


def _swap_halves(x):
    return jnp.concatenate([x[:, HEAD_DIM:], x[:, :HEAD_DIM]], axis=1)


def _load_span(prev_ref, cur_ref, buf_e, buf_o, kv, mult):
    lane = lax.broadcasted_iota(jnp.int32, (ATT_BLOCK, PAIR), 1)
    row = lax.broadcasted_iota(jnp.int32, (ATT_BLOCK, PAIR), 0)
    own = lane // HEAD_DIM == kv
    for r0, ref, first in ((0, prev_ref, True), (ATT_BLOCK, cur_ref, False)):
        v = ref[...]
        if mult != 1.0:
            v = v * mult
        zero = jnp.zeros_like(v)
        mine = jnp.where(own, v, zero)
        if first:
            mine = jnp.where(row > 0, mine, zero)
        other = _swap_halves(mine)
        buf_e[r0:r0 + ATT_BLOCK, :] = jnp.where(kv == 0, mine, other)
        buf_o[r0:r0 + ATT_BLOCK, :] = jnp.where(kv == 0, other, mine)


def _first_block_penalty(i):
    col = lax.broadcasted_iota(jnp.int32, (1, ATT_SPAN), 1)
    return jnp.where((col < ATT_BLOCK) & (col > 0), jnp.where(i > 0, 0.0, NEG_BIG), 0.0).astype(F32)


def _attn_probs(qp, kbuf, bias, first_pen):
    nt = (((1,), (1,)), ((), ()))
    s = lax.dot_general(qp, kbuf, nt, preferred_element_type=F32) + bias + first_pen
    m = jnp.max(s, axis=-1, keepdims=True)
    e = jnp.exp(s - m)
    return e * (1.0 / jnp.sum(e, axis=-1, keepdims=True))


def _attn_specs(nb, pairs, q_map, row_cur, row_prev):
    gw = pairs * PAIR
    kblk = N_KV_HEADS * gw // KV_COLS
    qspec = pl.BlockSpec((ATT_BLOCK, gw), q_map)
    kv_specs = [pl.BlockSpec((ATT_BLOCK, KV_COLS), (lambda b, i, kv, r=r, c=c: (r(b, i), c)))
                for c in (kblk, kblk + 1) for r in (row_cur, row_prev)]
    tspec = pl.BlockSpec((N_KV_HEADS, 2, pairs * ATT_BLOCK, ATT_SPAN), lambda b, i, kv: (0, 0, 0, 0))
    return qspec, kv_specs, tspec


def _attn_fwd(qkv, tab2, B, S, payload=()):
    assert N_KV_HEADS == 2 and GROUP % 2 == 0
    T = qkv.shape[0]
    pairs = GROUP // 2
    nb = S // ATT_BLOCK
    q_dim = N_HEADS * HEAD_DIM
    n_pay = len(payload)
    bcasts = [bc for _, bc in payload]
    steps = B * nb * N_KV_HEADS

    def body(*refs):
        q_ref, kc_ref, kp_ref, vc_ref, vp_ref, tab_ref = refs[:6]
        pay_src = refs[6:6 + n_pay]
        o_ref = refs[6 + n_pay]
        pay_dst = refs[7 + n_pay:7 + 2 * n_pay]
        ke, ko, ve, vo = refs[7 + 2 * n_pay:11 + 2 * n_pay]
        sems = refs[11 + 2 * n_pay:]
        i, kv = pl.program_id(1), pl.program_id(2)
        if n_pay:
            step = (pl.program_id(0) * nb + i) * N_KV_HEADS + kv

            @pl.when(step == 0)
            def _():
                _Plan(bcasts, pay_src, pay_dst, *sems).start()

            @pl.when(step == (3 * steps) // 4)
            def _():
                _Plan(bcasts, pay_src, pay_dst, *sems).pass_on()

            @pl.when(step == steps - 1)
            def _():
                _Plan(bcasts, pay_src, pay_dst, *sems).finish()

        _load_span(kp_ref, kc_ref, ke, ko, kv, ATT_SCALE)
        _load_span(vp_ref, vc_ref, ve, vo, kv, 1.0)
        pen = _first_block_penalty(i)
        qp = _stack_pairs(q_ref, pairs)
        p_e = _attn_probs(qp, ke[...], tab_ref[kv, 0], pen)
        p_o = _attn_probs(qp, ko[...], tab_ref[kv, 1], pen)
        o = jnp.dot(p_e.astype(BF16), ve[...], preferred_element_type=F32)
        o = o + jnp.dot(p_o.astype(BF16), vo[...], preferred_element_type=F32)
        o_ref[...] = _unstack_pairs(o.astype(BF16), pairs)

    qspec, kv_specs, tspec = _attn_specs(nb, pairs, lambda b, i, kv: (b * nb + i, kv), lambda b, i: b * nb + i,
                                         lambda b, i: b * nb + jnp.maximum(i - 1, 0))
    args, in_specs = [qkv] * 5 + [tab2], [qspec] + kv_specs + [tspec]
    out_shape, out_specs = [jax.ShapeDtypeStruct((T, q_dim), BF16)], [qspec]
    scratch = [pltpu.VMEM((ATT_SPAN, PAIR), BF16)] * 4
    if n_pay:
        p_arrays, p_in, p_shapes, p_out, p_scratch = _payload_layout(payload)
        args += p_arrays
        in_specs += p_in
        out_shape += p_shapes
        out_specs += p_out
        scratch += p_scratch
    res = pl.pallas_call(
        body,
        name="attn_fwd",
        grid=(B, nb, N_KV_HEADS),
        out_shape=tuple(out_shape),
        in_specs=in_specs,
        out_specs=tuple(out_specs),
        scratch_shapes=scratch,
        compiler_params=_params(*(("arbitrary",) * 3 if n_pay else ("parallel", "parallel", "arbitrary"))),
    )(*args)
    return (res[0], list(res[1:])) if n_pay else res[0]


def _attn_bwd(qkv, o, do, tab2, B, S):
    T = qkv.shape[0]
    hd = HEAD_DIM
    pairs = GROUP // 2
    rows = pairs * ATT_BLOCK
    nb = S // ATT_BLOCK
    last = nb - 1
    q_dim = N_HEADS * hd
    tn = (((0,), (0,)), ((), ()))
    nt = (((1,), (1,)), ((), ()))

    def body(q_ref, kc_ref, kp_ref, vc_ref, vp_ref, tab_ref, o_ref, do_ref,
             dq_ref, dkv_ref, dsink_ref, carry_k, carry_v, ke, ko, ve, vo):
        i, kv = pl.program_id(1), pl.program_id(2)

        @pl.when((i == 0) & (kv == 0))
        def _():
            dsink_ref[...] = jnp.zeros_like(dsink_ref)

        def emit(dk_rows, dv_rows):
            for h in range(N_KV_HEADS):
                @pl.when(kv == h)
                def _():
                    dkv_ref[:, h * hd:(h + 1) * hd] = dk_rows.astype(BF16)
                    dkv_ref[:, KV_COLS + h * hd:KV_COLS + (h + 1) * hd] = dv_rows.astype(BF16)

        @pl.when(i < nb)
        def _():
            _load_span(kp_ref, kc_ref, ke, ko, kv, ATT_SCALE)
            _load_span(vp_ref, vc_ref, ve, vo, kv, 1.0)
            pen = _first_block_penalty(i)
            qp = _stack_pairs(q_ref, pairs)
            dop = _stack_pairs(do_ref, pairs)
            op = _stack_pairs(o_ref, pairs)
            p_e = _attn_probs(qp, ke[...], tab_ref[kv, 0], pen)
            p_o = _attn_probs(qp, ko[...], tab_ref[kv, 1], pen)
            prod = dop.astype(F32) * op.astype(F32)
            low = lax.broadcasted_iota(jnp.int32, (rows, PAIR), 1) < hd
            d_e = jnp.sum(jnp.where(low, prod, 0.0), axis=-1, keepdims=True)
            d_o = jnp.sum(prod, axis=-1, keepdims=True) - d_e
            ds_e = (p_e * (lax.dot_general(dop, ve[...], nt, preferred_element_type=F32) - d_e)).astype(BF16)
            ds_o = (p_o * (lax.dot_general(dop, vo[...], nt, preferred_element_type=F32) - d_o)).astype(BF16)
            dq = jnp.dot(ds_e, ke[...], preferred_element_type=F32) + jnp.dot(ds_o, ko[...], preferred_element_type=F32)
            dq_ref[...] = _unstack_pairs(dq.astype(BF16), pairs)
            hg = lax.broadcasted_iota(jnp.int32, (pairs, rows), 0)
            hr = lax.broadcasted_iota(jnp.int32, (pairs, rows), 1)
            head_of = jnp.where(hr // ATT_BLOCK == hg, 1.0, 0.0).astype(BF16)
            dsink_ref[kv, 0] += jnp.dot(head_of, ds_e, preferred_element_type=F32)[:, 0:128]
            dsink_ref[kv, 1] += jnp.dot(head_of, ds_o, preferred_element_type=F32)[:, 0:128]
            low_k = lax.broadcasted_iota(jnp.int32, (ATT_SPAN, PAIR), 1) < hd
            dk2 = jnp.where(low_k, lax.dot_general(ds_e, qp, tn, preferred_element_type=F32),
                            lax.dot_general(ds_o, qp, tn, preferred_element_type=F32))
            dv2 = jnp.where(low_k, lax.dot_general(p_e.astype(BF16), dop, tn, preferred_element_type=F32),
                            lax.dot_general(p_o.astype(BF16), dop, tn, preferred_element_type=F32))
            keep = lax.broadcasted_iota(jnp.int32, (ATT_SPAN, hd), 0) > 0
            dk = jnp.where(keep, (dk2[:, :hd] + dk2[:, hd:]) * ATT_SCALE, 0.0)
            dv = jnp.where(keep, dv2[:, :hd] + dv2[:, hd:], 0.0)

            @pl.when(i > 0)
            def _():
                emit(carry_k[kv] + dk[0:ATT_BLOCK], carry_v[kv] + dv[0:ATT_BLOCK])

            carry_k[kv] = dk[ATT_BLOCK:]
            carry_v[kv] = dv[ATT_BLOCK:]

        @pl.when(i == nb)
        def _():
            emit(carry_k[kv], carry_v[kv])

    def q_map(b, i, kv):
        return (b * nb + jnp.minimum(i, last), jnp.where(i == nb, N_KV_HEADS - 1, kv))

    qspec, kv_specs, tspec = _attn_specs(nb, pairs, q_map, lambda b, i: b * nb + jnp.minimum(i, last),
                                         lambda b, i: b * nb + jnp.clip(i - 1, 0, last))
    dkv = pl.BlockSpec((ATT_BLOCK, 2 * KV_COLS), lambda b, i, kv: (b * nb + jnp.maximum(i - 1, 0), 0))
    dsk = pl.BlockSpec((None, N_KV_HEADS, 2, pairs, 128), lambda b, i, kv: (b, 0, 0, 0, 0))
    return pl.pallas_call(
        body,
        name="attn_bwd",
        grid=(B, nb + 1, N_KV_HEADS),
        out_shape=(
            jax.ShapeDtypeStruct((T, q_dim), BF16),
            jax.ShapeDtypeStruct((T, 2 * KV_COLS), BF16),
            jax.ShapeDtypeStruct((B, N_KV_HEADS, 2, pairs, 128), F32),
        ),
        in_specs=[qspec] + kv_specs + [tspec, qspec, qspec],
        out_specs=(qspec, dkv, dsk),
        scratch_shapes=[pltpu.VMEM((N_KV_HEADS, ATT_BLOCK, hd), F32), pltpu.VMEM((N_KV_HEADS, ATT_BLOCK, hd), F32)]
        + [pltpu.VMEM((ATT_SPAN, PAIR), BF16)] * 4,
        compiler_params=_params("arbitrary", "arbitrary", "arbitrary"),
    )(qkv, qkv, qkv, qkv, qkv, tab2, o, do)


def _conv_tiles(S):
    ts = _pick(S, 256, CONV_HALO)
    return ts, S // ts


def _conv_chunks(C, ts):
    lane = _pick(C, 128, 128)
    return lane, C // lane, _pick(ts, 128, 8)


def _conv_weight_chunks(w_dw, C):
    lane = _pick(C, 128, 128)
    w = jnp.pad(w_dw, ((0, CONV_HALO - CONV_WIDTH), (0, 0)))
    return w.reshape(CONV_HALO, C // lane, lane).transpose(1, 0, 2)


def _conv_fwd(u, w3, b_dw, ln_g, ln_b, S):
    T, C2 = u.shape
    C = C2 // 2
    B = T // S
    ts, nj = _conv_tiles(S)
    hb = ts // CONV_HALO
    lane, nc, rc = _conv_chunks(C, ts)

    def body(a_ref, g_ref, ap_ref, gp_ref, w_ref, bdw_ref, lg_ref, lb_ref, cv_ref, s_ref, buf, cvb):
        j = pl.program_id(1)
        glu_prev = ap_ref[...] * _sigmoid(gp_ref[...]) * (j > 0).astype(F32)
        glu = a_ref[...] * _sigmoid(g_ref[...])
        for cc in range(nc):
            buf[cc, 0:CONV_HALO, :] = glu_prev[:, cc * lane:(cc + 1) * lane]
            buf[cc, CONV_HALO:, :] = glu[:, cc * lane:(cc + 1) * lane]

        def chunk(cc, carry):
            for r0 in range(0, ts, rc):
                acc = jnp.zeros((rc, lane), F32)
                for kk in range(CONV_WIDTH):
                    lo = CONV_HALO - (CONV_WIDTH - 1 - kk) + r0
                    acc = acc + w_ref[cc, kk:kk + 1, :] * buf[cc, lo:lo + rc, :]
                cvb[cc, r0:r0 + rc, :] = acc
            return carry

        lax.fori_loop(0, nc, chunk, 0)
        for cc in range(nc):
            cv_ref[:, cc * lane:(cc + 1) * lane] = cvb[cc] + bdw_ref[:, cc * lane:(cc + 1) * lane]
        cv = cv_ref[...]
        mu = jnp.mean(cv, axis=-1, keepdims=True)
        xc = cv - mu
        rstd = lax.rsqrt(jnp.mean(xc * xc, axis=-1, keepdims=True) + EPS)
        ln = xc * rstd * lg_ref[...] + lb_ref[...]
        s_ref[...] = (ln * _sigmoid(ln)).astype(BF16)

    a_cur = pl.BlockSpec((ts, C), lambda b, j: (b * nj + j, 0))
    g_cur = pl.BlockSpec((ts, C), lambda b, j: (b * nj + j, 1))
    a_prev = pl.BlockSpec((CONV_HALO, C), lambda b, j: (jnp.maximum((b * nj + j) * hb - 1, 0), 0))
    g_prev = pl.BlockSpec((CONV_HALO, C), lambda b, j: (jnp.maximum((b * nj + j) * hb - 1, 0), 1))
    wspec = pl.BlockSpec((nc, CONV_HALO, lane), lambda b, j: (0, 0, 0))
    one = pl.BlockSpec((1, C), lambda b, j: (0, 0))
    return pl.pallas_call(
        body,
        name="conv_fwd",
        grid=(B, nj),
        out_shape=(jax.ShapeDtypeStruct((T, C), F32), jax.ShapeDtypeStruct((T, C), BF16)),
        in_specs=[a_cur, g_cur, a_prev, g_prev, wspec, one, one, one],
        out_specs=(a_cur, a_cur),
        scratch_shapes=[pltpu.VMEM((nc, CONV_HALO + ts, lane), F32), pltpu.VMEM((nc, ts, lane), F32)],
        compiler_params=_params("parallel", "arbitrary"),
    )(u, u, u, u, w3, b_dw, ln_g, ln_b)


def _lnsilu_bwd(ds, cv, ln_g, ln_b, tm_pref=256):
    T, C = cv.shape
    tm = _pick(T, tm_pref, 16)

    def body(ds_ref, cv_ref, lg_ref, lb_ref, dcv_ref, dlg_ref, dlb_ref, dbdw_ref):
        @pl.when(pl.program_id(0) == 0)
        def _():
            dlg_ref[...] = jnp.zeros_like(dlg_ref)
            dlb_ref[...] = jnp.zeros_like(dlb_ref)
            dbdw_ref[...] = jnp.zeros_like(dbdw_ref)

        cv_v = cv_ref[...]
        g = lg_ref[...]
        mu = jnp.mean(cv_v, axis=-1, keepdims=True)
        xc = cv_v - mu
        rstd = lax.rsqrt(jnp.mean(xc * xc, axis=-1, keepdims=True) + EPS)
        xhat = xc * rstd
        ln = xhat * g + lb_ref[...]
        sg = _sigmoid(ln)
        dln = ds_ref[...] * (sg * (1.0 + ln * (1.0 - sg)))
        dlg_ref[...] += jnp.sum(dln * xhat, axis=0, keepdims=True)
        dlb_ref[...] += jnp.sum(dln, axis=0, keepdims=True)
        dxhat = dln * g
        dcv = rstd * (dxhat - jnp.mean(dxhat, axis=-1, keepdims=True)
                      - xhat * jnp.mean(dxhat * xhat, axis=-1, keepdims=True))
        dcv_ref[...] = dcv
        dbdw_ref[...] += jnp.sum(dcv, axis=0, keepdims=True)

    row = pl.BlockSpec((tm, C), lambda i: (i, 0))
    one = pl.BlockSpec((1, C), lambda i: (0, 0))
    return pl.pallas_call(
        body,
        name="lnsilu_bwd",
        grid=(T // tm,),
        out_shape=(jax.ShapeDtypeStruct((T, C), F32),) + (jax.ShapeDtypeStruct((1, C), F32),) * 3,
        in_specs=[row, row, one, one],
        out_specs=(row, one, one, one),
        compiler_params=_params("arbitrary"),
    )(ds, cv, ln_g, ln_b)


def _conv_bwd(dcv, u, w3, S):
    T, C2 = u.shape
    C = C2 // 2
    B = T // S
    ts, nj = _conv_tiles(S)
    hb = ts // CONV_HALO
    n_halo_blocks = T // CONV_HALO
    lane, nc, rc = _conv_chunks(C, ts)

    def body(dcv_ref, dnx_ref, a_ref, g_ref, ap_ref, gp_ref, w_ref, du_ref, dw_ref, gbuf, dbuf, dglu, dw8):
        b, j = pl.program_id(0), pl.program_id(1)

        @pl.when((b == 0) & (j == 0))
        def _():
            dw8[...] = jnp.zeros_like(dw8)

        a = a_ref[...]
        sg = _sigmoid(g_ref[...])
        glu_prev = ap_ref[...] * _sigmoid(gp_ref[...]) * (j > 0).astype(F32)
        glu = a * sg
        dcur = dcv_ref[...]
        dnext = dnx_ref[...] * (j < nj - 1).astype(F32)
        for cc in range(nc):
            cols = slice(cc * lane, (cc + 1) * lane)
            gbuf[cc, 0:CONV_HALO, :] = glu_prev[:, cols]
            gbuf[cc, CONV_HALO:, :] = glu[:, cols]
            dbuf[cc, 0:ts, :] = dcur[:, cols]
            dbuf[cc, ts:, :] = dnext[:, cols]

        def chunk(cc, carry):
            for r0 in range(0, ts, rc):
                acc = jnp.zeros((rc, lane), F32)
                for kk in range(CONV_WIDTH):
                    d = CONV_WIDTH - 1 - kk
                    acc = acc + w_ref[cc, kk:kk + 1, :] * dbuf[cc, r0 + d:r0 + d + rc, :]
                dglu[cc, r0:r0 + rc, :] = acc
            for kk in range(CONV_WIDTH):
                d = CONV_WIDTH - 1 - kk
                p = jnp.zeros((rc, lane), F32)
                for r0 in range(0, ts, rc):
                    lo = CONV_HALO - d + r0
                    p = p + dbuf[cc, r0:r0 + rc, :] * gbuf[cc, lo:lo + rc, :]
                dw8[cc, kk * 8:(kk + 1) * 8, :] += jnp.sum(p.reshape(rc // 8, 8, lane), axis=0)
            return carry

        lax.fori_loop(0, nc, chunk, 0)
        for cc in range(nc):
            cols = slice(cc * lane, (cc + 1) * lane)
            dgl = dglu[cc]
            du_ref[:, cc * lane:(cc + 1) * lane] = (dgl * sg[:, cols]).astype(BF16)
            du_ref[:, C + cc * lane:C + (cc + 1) * lane] = (dgl * a[:, cols] * sg[:, cols] * (1.0 - sg[:, cols])).astype(BF16)

        @pl.when((b == B - 1) & (j == nj - 1))
        def _():
            dw_ref[...] = jnp.zeros_like(dw_ref)
            for kk in range(CONV_WIDTH):
                dw_ref[:, kk:kk + 1, :] = jnp.sum(dw8[:, kk * 8:(kk + 1) * 8, :], axis=1, keepdims=True)

    a_cur = pl.BlockSpec((ts, C), lambda b, j: (b * nj + j, 0))
    g_cur = pl.BlockSpec((ts, C), lambda b, j: (b * nj + j, 1))
    a_prev = pl.BlockSpec((CONV_HALO, C), lambda b, j: (jnp.maximum((b * nj + j) * hb - 1, 0), 0))
    g_prev = pl.BlockSpec((CONV_HALO, C), lambda b, j: (jnp.maximum((b * nj + j) * hb - 1, 0), 1))
    d_next = pl.BlockSpec((CONV_HALO, C), lambda b, j: (jnp.minimum((b * nj + j + 1) * hb, n_halo_blocks - 1), 0))
    wspec = pl.BlockSpec((nc, CONV_HALO, lane), lambda b, j: (0, 0, 0))
    return pl.pallas_call(
        body,
        name="conv_bwd",
        grid=(B, nj),
        out_shape=(jax.ShapeDtypeStruct((T, C2), BF16), jax.ShapeDtypeStruct((nc, CONV_HALO, lane), F32)),
        in_specs=[a_cur, d_next, a_cur, g_cur, a_prev, g_prev, wspec],
        out_specs=(pl.BlockSpec((ts, C2), lambda b, j: (b * nj + j, 0)), wspec),
        scratch_shapes=[
            pltpu.VMEM((nc, CONV_HALO + ts, lane), F32),
            pltpu.VMEM((nc, ts + CONV_HALO, lane), F32),
            pltpu.VMEM((nc, ts, lane), F32),
            pltpu.VMEM((nc, CONV_HALO * 8, lane), F32),
        ],
        compiler_params=_params("arbitrary", "arbitrary"),
    )(dcv, dcv, u, u, u, u, w3)


def _loss_head(x, tgt, gfin, y, gate, S, tm_pref=256):
    T, D = x.shape
    tm = _pick(S, tm_pref, 16)
    tpb = S // tm

    def body(x_ref, t_ref, g_ref, y_ref, gt_ref, dx_ref, loss_ref, dg_ref, dy_ref, dgate_ref):
        @pl.when(pl.program_id(0) == 0)
        def _():
            loss_ref[...] = jnp.zeros_like(loss_ref)
            dg_ref[...] = jnp.zeros_like(dg_ref)

        @pl.when(pl.program_id(0) % tpb == 0)
        def _():
            dgate_ref[...] = jnp.zeros_like(dgate_ref)

        xv = x_ref[...]
        g = g_ref[...]
        r = lax.rsqrt(jnp.mean(xv * xv, axis=-1, keepdims=True) + EPS)
        xhat = xv * r
        e = xhat * g - t_ref[...]
        row_loss = jnp.mean(e * e, axis=-1, keepdims=True)
        loss_ref[...] += 0.5 * jnp.sum(row_loss, axis=0, keepdims=True)
        dy = e * (1.0 / D)
        dg_ref[...] += jnp.sum(dy * xhat, axis=0, keepdims=True)
        dxhat = dy * g
        dx = r * (dxhat - xhat * jnp.mean(dxhat * xhat, axis=-1, keepdims=True))
        dx_ref[...] = dx
        dy_ref[...] = (dx * gt_ref[...]).astype(BF16)
        dgate_ref[...] += jnp.sum(dx * y_ref[...].astype(F32), axis=0, keepdims=True)

    row = pl.BlockSpec((tm, D), lambda i: (i, 0))
    vec = pl.BlockSpec((None, 1, D), lambda i: (i // tpb, 0, 0))
    one = pl.BlockSpec((1, D), lambda i: (0, 0))
    return pl.pallas_call(
        body,
        name="loss_head",
        grid=(T // tm,),
        out_shape=(
            jax.ShapeDtypeStruct((T, D), F32),
            jax.ShapeDtypeStruct((8, 128), F32),
            jax.ShapeDtypeStruct((1, D), F32),
            jax.ShapeDtypeStruct((T, D), BF16),
            jax.ShapeDtypeStruct((T // S, 1, D), F32),
        ),
        in_specs=[row, row, one, row, vec],
        out_specs=(row, pl.BlockSpec((8, 128), lambda i: (0, 0)), one, row, vec),
        compiler_params=_params("arbitrary"),
    )(x, tgt, gfin, y, gate)


def _adam(name, parts, w, m, v, tm_pref=256):
    L = len(parts)
    P, R, C = parts[0].shape
    fit = VMEM_LIMIT_BYTES // 2 // (2 * L * P * C * parts[0].dtype.itemsize)
    tm = _pick(R, max(16, min(tm_pref, fit // 16 * 16)), 16)
    tiles = R // tm
    c1 = 1.0 - ADAM_B1 ** ADAM_STEP
    c2 = 1.0 - ADAM_B2 ** ADAM_STEP

    def body(*refs):
        p_refs = refs[:L]
        w_ref, m_ref, v_ref, g_ref, d_ref, mo_ref, vo_ref = refs[L:]
        for l in range(L):
            @pl.when(pl.program_id(0) == l)
            def _():
                g = p_refs[l][0].astype(F32)
                for i in range(1, P):
                    g = g + p_refs[l][i].astype(F32)
                m_new = ADAM_B1 * m_ref[...] + (1.0 - ADAM_B1) * g
                v_new = ADAM_B2 * v_ref[...] + (1.0 - ADAM_B2) * (g * g)
                m_hat = m_new / c1
                v_hat = v_new / c2
                g_ref[...] = g
                d_ref[...] = -ADAM_LR * (m_hat / (jnp.sqrt(v_hat) + ADAM_EPS) + ADAM_WD * w_ref[...])
                mo_ref[...] = m_new
                vo_ref[...] = v_new

    def p_spec(l):
        return pl.BlockSpec((P, tm, C), lambda li, t: (0, jnp.where(li == l, t, jnp.where(li < l, 0, tiles - 1)), 0))

    row = pl.BlockSpec((tm, C), lambda li, t: (li * tiles + t, 0))
    return pl.pallas_call(
        body,
        name=name,
        grid=(L, tiles),
        out_shape=(jax.ShapeDtypeStruct((L * R, C), F32),) * 4,
        in_specs=[p_spec(l) for l in range(L)] + [row, row, row],
        out_specs=(row, row, row, row),
        compiler_params=_params("arbitrary", "arbitrary"),
    )(*parts, w, m, v)


BIG = ("w_qkv", "w_o", "w_pw1", "w_pw2", "w_up", "w_down")
SMALL_SHARDED = (("b_pw1", 1), ("w_dw", 2), ("b_dw", 1), ("conv_ln_g", 1), ("conv_ln_b", 1), ("b_pw2", 1))
SMALL_REPL = ("b_mod", "norm_mix", "norm_mlp", "b_qkv", "b_o", "sinks", "final_norm")
WEIGHTS = ("w_mod", "b_mod", "norm_mix", "norm_mlp", "w_qkv", "b_qkv", "w_o", "b_o", "sinks", "w_pw1", "b_pw1",
           "w_dw", "b_dw", "conv_ln_g", "conv_ln_b", "w_pw2", "b_pw2", "w_up", "w_down", "final_norm")


def _step(x, c, loss_target, W, M, V):
    B, S, D = x.shape
    T = B * S
    L = W["w_mod"].shape[0]
    n_mod = W["w_mod"].shape[2]
    me = 4 * lax.axis_index("x") + 2 * lax.axis_index("y") + lax.axis_index("c")
    q_dim = N_HEADS * HEAD_DIM
    kv_dim = N_KV_HEADS * HEAD_DIM

    Wb = {n: W[n].astype(BF16) for n in BIG}

    def shards(i):
        jm = i // 2
        first, last = ("w_qkv", "w_o") if i % 2 == 0 else ("w_pw1", "w_pw2")
        return [Wb[first][jm], Wb[last][jm], Wb["w_up"][i], Wb["w_down"][i]]

    def carried(res, payload):
        return res if payload else (res, [])

    small_names = [n for n, _ in SMALL_SHARDED]
    small_src, small_sizes = _pack([c] + [W[n] for n in small_names], F32, 0)
    got = _exchange("gather_first", [(small_src, True), (shards(0)[0], True)])
    small_parts = _unpack(got[0], small_sizes, [c.shape] + [W[n].shape for n in small_names], 1)
    c_all = small_parts[0].reshape(N_DEV * B, D)
    full = {n: _from_slots(p, d) for (n, d), p in zip(SMALL_SHARDED, small_parts[1:])}
    gathered = [[got[1], None, None, None]] + [None] * (L - 1)

    b_mod_mine = lax.dynamic_slice_in_dim(W["b_mod"], me * n_mod, n_mod, axis=1).reshape(L, 1, n_mod)
    mod_part = _mod_fwd(c_all, W["w_mod"].astype(BF16), b_mod_mine)
    mod_slots = mod_part.reshape(L, N_DEV, B, n_mod).transpose(1, 0, 2, 3).reshape(N_DEV, L * B, n_mod)
    mod_recv = _exchange("scatter_mod", [(mod_slots, False)])[0]
    mod = mod_recv.reshape(N_DEV, L, B, n_mod).transpose(1, 2, 0, 3).reshape(L, B, N_MOD, 1, D)

    w_dw3 = [_conv_weight_chunks(full["w_dw"][j], D) for j in range(full["w_dw"].shape[0])]

    xc = x.reshape(T, D)
    saved = []
    h1 = _normmod_fwd("normmod_mix_fwd", xc, W["norm_mix"][0][None], mod[0, :, 1], mod[0, :, 0], S)
    for i in range(L):
        jm = i // 2
        sh1, sc1, g1, sh2, sc2, g2 = (mod[i, :, t] for t in range(N_MOD))
        nxt = [[(s, True)] for s in shards(i + 1)] if i + 1 < L else [[]] * 4
        g_first, g_last, g_up, g_down = gathered[i]
        sv = {"x_in": xc, "h1": h1}
        if i % 2 == 0:
            w_qkv_full = _from_slots(g_first, 1)
            sv["w_first"] = w_qkv_full
            mine = [(s, True) for s in shards(0)[1:]] if i == 0 else []
            pay = nxt[0] + mine[:1]
            qkv, n_first = carried(_mm("qkv_fwd", h1, w_qkv_full, bias=W["b_qkv"][jm], payload=pay), pay)
            tab = _pair_tables(_attn_table(W["sinks"][jm]))
            mix_in, n_attn = carried(_attn_fwd(qkv, tab, B, S, payload=mine[1:]), mine[1:])
            if i == 0:
                g_last, g_up, g_down = n_first[-1], n_attn[0], n_attn[1]
            sv.update(qkv=qkv, tab=tab)
            b_out = W["b_o"][jm]
        else:
            sv["w_first"] = g_first
            u, n_first = carried(_mm("pw1_fwd", h1, g_first, w_form="nslots", bias=full["b_pw1"][jm], out_dtype=F32,
                                     payload=nxt[0]), nxt[0])
            cv, mix_in = _conv_fwd(u, w_dw3[jm], full["b_dw"][jm][None], full["conv_ln_g"][jm][None],
                                   full["conv_ln_b"][jm][None], S)
            sv.update(u=u, cv=cv)
            b_out = full["b_pw2"][jm]
        w_last = g_last.reshape(-1, D)
        w_down_full = g_down.reshape(-1, D)
        sv.update(mix_in=mix_in, w_last=w_last, g_up=g_up, w_down=w_down_full)
        (x1, y1, h2), n_last = carried(_mm("mix_out_fwd", mix_in, w_last, bias=b_out, epi="resid", x=xc, gate=g1,
                                           norm=(W["norm_mlp"][i][None], sc2, sh2), rows_per_batch=S,
                                           payload=nxt[1]), nxt[1])
        sv.update(y1=y1, x1=x1)
        up, n_up = carried(_mm("mlp_up_fwd", h2, g_up, w_form="nslots", epi="relu", payload=nxt[2]), nxt[2])
        norm_next = (W["norm_mix"][i + 1][None], mod[i + 1, :, 1], mod[i + 1, :, 0]) if i + 1 < L else None
        res, n_down = carried(_mm("mlp_down_fwd", up, w_down_full, a_sq=True, epi="resid", x=x1, gate=g2,
                                  norm=norm_next, rows_per_batch=S, payload=nxt[3]), nxt[3])
        x2, y2 = res[0], res[1]
        h1 = res[2] if i + 1 < L else None
        sv.update(h2=h2, up=up, y2=y2)
        saved.append(sv)
        xc = x2
        if i + 1 < L:
            gathered[i + 1] = [n_first[0], n_last[0], n_up[0], n_down[0]]

    dx, loss_blk, dgfin, dy2, dg2 = _loss_head(xc, loss_target.reshape(T, D), W["final_norm"][None],
                                               saved[L - 1]["y2"], mod[L - 1, :, 5], S)
    loss = lax.psum(loss_blk[0, 0], ("x", "y", "c"))

    G = {"final_norm": dgfin.reshape(D)}
    dmod_layers = [None] * L
    small_grads = ("norm_mix", "norm_mlp", "b_qkv", "b_o", "sinks", "b_pw1", "w_dw", "b_dw", "conv_ln_g", "conv_ln_b",
                   "b_pw2")
    acc = {n: [None] * W[n].shape[0] for n in small_grads}
    reduced = {n: [None] * W[n].shape[0] for n in BIG}

    waiting_up = None
    for i in reversed(range(L)):
        jm = i // 2
        sv = saved[i]
        sh1, sc1, g1, sh2, sc2, g2 = (mod[i, :, t] for t in range(N_MOD))
        pay = [(waiting_up[1], False)] if waiting_up else []
        (gd0, gd1), r = carried(_mm_tn("w_down_grad", sv["up"], dy2, a_sq=True, out_form="kslots", halves=True,
                                       tt_pref=4096, payload=pay), pay)
        if waiting_up:
            reduced["w_up"][waiting_up[0]].append(r[0])
        du, r0 = _mm("mlp_down_bwd", dy2, sv["w_down"], w_form="full_t", epi="relu2d", u=sv["up"],
                     payload=[(gd0, False)])
        (gu0, gu1), r1 = _mm_tn("w_up_grad", sv["h2"], du, out_form="nslots", halves=True, tt_pref=4096,
                                payload=[(gd1, False)])
        reduced["w_down"][i] = [r0[0], r1[0]]
        pay = [(gu0, False)] + ([] if i > 0 else [(gu1, False)])
        (dx, dsh2, dsc2, dgn, dy1, dg1, dy1_sum), r = _mm(
            "mlp_up_bwd", du, sv["g_up"], w_form="kslots_t", epi="normbwd", rows_per_batch=S,
            nb=dict(x=sv["x1"], dx_in=dx, gnorm=W["norm_mlp"][i][None], sc=sc2, gate=(sv["y1"], g1)),
            payload=pay)
        reduced["w_up"][i] = list(r)
        waiting_up = (i, gu1) if i > 0 else None
        acc["norm_mlp"][i] = dgn.reshape(D)
        gw_last = _mm_tn("mix_out_grad", sv["mix_in"], dy1)
        gw_last = gw_last.reshape((N_DEV, -1) + gw_last.shape[1:]).astype(BF16)
        below = (saved[i - 1]["y2"], mod[i - 1, :, 5]) if i > 0 else None
        nb_mix = dict(x=sv["x_in"], dx_in=dx, gnorm=W["norm_mix"][i][None], sc=sc1, gate=below)
        if i % 2 == 0:
            acc["b_o"][jm] = dy1_sum.reshape(D)
            do, r = _mm("attn_out_bwd", dy1, sv["w_last"], w_form="full_t", payload=[(gw_last, False)])
            reduced["w_o"][jm] = r[0]
            dq, dkv, dsk = _attn_bwd(sv["qkv"], sv["mix_in"], do, sv["tab"], B, S)
            acc["sinks"][jm] = jnp.sum(dsk[..., 0], axis=0).transpose(0, 2, 1).reshape(N_HEADS)
            dqkv = jnp.concatenate([dq, dkv], axis=1)
            gw_qkv, db_qkv = _mm_tn("w_qkv_grad", sv["h1"], dqkv, colsum=True)
            acc["b_qkv"][jm] = db_qkv.reshape(-1)
            gw_qkv = _to_slots(gw_qkv, 1).astype(BF16)
            res, r = _mm("qkv_bwd", dqkv, sv["w_first"], w_form="full_t", epi="normbwd", nb=nb_mix, rows_per_batch=S,
                         payload=[(gw_qkv, False)])
            reduced["w_qkv"][jm] = r[0]
        else:
            acc["b_pw2"][jm] = dy1_sum.reshape(D)
            ds, r = _mm("pw2_bwd", dy1, sv["w_last"], w_form="full_t", out_dtype=F32, payload=[(gw_last, False)])
            reduced["w_pw2"][jm] = r[0]
            dcv, dlg, dlb, dbdw = _lnsilu_bwd(ds, sv["cv"], full["conv_ln_g"][jm][None], full["conv_ln_b"][jm][None])
            acc["conv_ln_g"][jm], acc["conv_ln_b"][jm], acc["b_dw"][jm] = dlg.reshape(-1), dlb.reshape(-1), dbdw.reshape(-1)
            du1, dwdw = _conv_bwd(dcv, sv["u"], w_dw3[jm], S)
            acc["w_dw"][jm] = dwdw.transpose(1, 0, 2).reshape(CONV_HALO, D)[:CONV_WIDTH]
            gw_pw1, db_pw1 = _mm_tn("w_pw1_grad", sv["h1"], du1, out_form="nslots", colsum=True, tt_pref=4096)
            acc["b_pw1"][jm] = db_pw1.reshape(-1)
            res, r = _mm("pw1_bwd", du1, sv["w_first"], w_form="kslots_t", epi="normbwd", nb=nb_mix, rows_per_batch=S,
                         payload=[(gw_pw1, False)])
            reduced["w_pw1"][jm] = r[0]
        dx, dsh1, dsc1, dgn = res[:4]
        acc["norm_mix"][i] = dgn.reshape(D)
        dmod_layers[i] = jnp.concatenate([dsh1, dsc1, dg1, dsh2, dsc2, dg2], axis=1).reshape(B, N_MOD * D)
        if i > 0:
            dy2, dg2 = res[4], res[5]
    grad_x = dx.reshape(B, S, D)
    for n, parts in acc.items():
        G[n] = jnp.stack(parts)

    dmod = jnp.stack(dmod_layers)
    dmod_slots = dmod.reshape(L, B, N_DEV, n_mod).transpose(2, 0, 1, 3).reshape(N_DEV, L * B, n_mod)
    dmod_recv = _exchange("gather_dmod", [(dmod_slots, False)])[0]
    dmod_all = dmod_recv.reshape(N_DEV, L, B, n_mod).transpose(1, 0, 2, 3).reshape(L, N_DEV * B, n_mod)
    g_w_mod, db_mod_mine = _mod_bwd(c_all, dmod_all)
    G["b_mod"] = lax.dynamic_update_slice_in_dim(jnp.zeros_like(W["b_mod"]), db_mod_mine.reshape(L, n_mod),
                                                 me * n_mod, axis=1)

    small_items = [jnp.broadcast_to(G[n][None], (N_DEV,) + G[n].shape) for n in SMALL_REPL]
    small_items += [_to_slots(G[n], d) for n, d in SMALL_SHARDED]
    small_slots, small_sizes2 = _pack(small_items, F32, 1)
    small_recv = _exchange("reduce_small", [(small_slots, False)])[0]

    out = {}

    def run_adam(name, parts, names):
        shapes = [W[n].shape for n in names]
        wp, sizes = _pack([W[n] for n in names], F32, 0)
        mp, _ = _pack([M[n] for n in names], F32, 0)
        vp, _ = _pack([V[n] for n in names], F32, 0)
        res = _adam(name, [parts], wp, mp, vp)
        for kind, buf in zip(("grad", "delta", "new_m", "new_v"), res):
            for n, a in zip(names, _unpack(buf, sizes, shapes, 0)):
                out[kind + "_" + n] = a

    for n in BIG + ("w_mod",):
        cols = W[n].shape[-1]
        if n == "w_mod":
            parts = [g_w_mod.reshape(1, -1, cols)]
        else:
            pieces = [p for r in reduced[n] for p in (r if isinstance(r, list) else [r])]
            parts = [p.reshape(N_DEV, -1, cols) for p in pieces]
        res = _adam("adam_" + n, parts, W[n].reshape(-1, cols), M[n].reshape(-1, cols), V[n].reshape(-1, cols))
        for kind, buf in zip(("grad", "delta", "new_m", "new_v"), res):
            out[kind + "_" + n] = buf.reshape(W[n].shape)
    run_adam("adam_small", small_recv, list(SMALL_REPL) + [n for n, _ in SMALL_SHARDED])

    res = [loss, grad_x]
    for kind in ("grad", "delta", "new_m", "new_v"):
        res += [out[kind + "_" + n] for n in WEIGHTS]
    return tuple(res)


def kernel(x, c, w_mod, b_mod, norm_mix, norm_mlp, w_qkv, b_qkv, w_o, b_o, sinks, w_pw1, b_pw1, w_dw, b_dw, conv_ln_g, conv_ln_b, w_pw2, b_pw2, w_up, w_down, final_norm, loss_target, m_w_mod, m_b_mod, m_norm_mix, m_norm_mlp, m_w_qkv, m_b_qkv, m_w_o, m_b_o, m_sinks, m_w_pw1, m_b_pw1, m_w_dw, m_b_dw, m_conv_ln_g, m_conv_ln_b, m_w_pw2, m_b_pw2, m_w_up, m_w_down, m_final_norm, v_w_mod, v_b_mod, v_norm_mix, v_norm_mlp, v_w_qkv, v_b_qkv, v_w_o, v_b_o, v_sinks, v_w_pw1, v_b_pw1, v_w_dw, v_b_dw, v_conv_ln_g, v_conv_ln_b, v_w_pw2, v_b_pw2, v_w_up, v_w_down, v_final_norm):
    W = dict(w_mod=w_mod, b_mod=b_mod, norm_mix=norm_mix, norm_mlp=norm_mlp, w_qkv=w_qkv, b_qkv=b_qkv, w_o=w_o,
             b_o=b_o, sinks=sinks, w_pw1=w_pw1, b_pw1=b_pw1, w_dw=w_dw, b_dw=b_dw, conv_ln_g=conv_ln_g,
             conv_ln_b=conv_ln_b, w_pw2=w_pw2, b_pw2=b_pw2, w_up=w_up, w_down=w_down, final_norm=final_norm)
    M = dict(w_mod=m_w_mod, b_mod=m_b_mod, norm_mix=m_norm_mix, norm_mlp=m_norm_mlp, w_qkv=m_w_qkv, b_qkv=m_b_qkv,
             w_o=m_w_o, b_o=m_b_o, sinks=m_sinks, w_pw1=m_w_pw1, b_pw1=m_b_pw1, w_dw=m_w_dw, b_dw=m_b_dw,
             conv_ln_g=m_conv_ln_g, conv_ln_b=m_conv_ln_b, w_pw2=m_w_pw2, b_pw2=m_b_pw2, w_up=m_w_up,
             w_down=m_w_down, final_norm=m_final_norm)
    V = dict(w_mod=v_w_mod, b_mod=v_b_mod, norm_mix=v_norm_mix, norm_mlp=v_norm_mlp, w_qkv=v_w_qkv, b_qkv=v_b_qkv,
             w_o=v_w_o, b_o=v_b_o, sinks=v_sinks, w_pw1=v_w_pw1, b_pw1=v_b_pw1, w_dw=v_w_dw, b_dw=v_b_dw,
             conv_ln_g=v_conv_ln_g, conv_ln_b=v_conv_ln_b, w_pw2=v_w_pw2, b_pw2=v_b_pw2, w_up=v_w_up,
             w_down=v_w_down, final_norm=v_final_norm)
    return _step(x, c, loss_target, W, M, V)
```

```python
import functools

import numpy as np
import jax
import jax.numpy as jnp
from jax import lax
from jax.experimental import pallas as pl
from jax.experimental.pallas import tpu as pltpu

F32 = jnp.float32
BF16 = jnp.bfloat16

N_DEV = 8
N_HEADS = 16
N_KV_HEADS = 2
HEAD_DIM = 64
GROUP = N_HEADS // N_KV_HEADS
ATT_BLOCK = 128
CONV_WIDTH = 31
CONV_HALO = 32
N_MOD = 6
EPS = 1e-6
ADAM_LR = 0.001
ADAM_B1 = 0.9
ADAM_B2 = 0.999
ADAM_EPS = 1e-08
ADAM_WD = 0.01
ADAM_STEP = 10
NEG_BIG = -1e30
PACK_COLS = 1024
VMEM_LIMIT_BYTES = 56 * 1024 * 1024
MESH_ID = pl.DeviceIdType.MESH


def _params(*sem):
    return pltpu.CompilerParams(dimension_semantics=sem, vmem_limit_bytes=VMEM_LIMIT_BYTES)


def _pick(n, pref, mult=8):
    if n <= pref:
        return n
    for t in range(pref, 0, -1):
        if n % t == 0 and t % mult == 0:
            return t
    return n


def _sigmoid(z):
    return 0.5 * jnp.tanh(0.5 * z) + 0.5


def _payload_layout(payload):
    n = len(payload)
    out_shapes = [jax.ShapeDtypeStruct((N_DEV,) + tuple(a.shape if bc else a.shape[1:]), a.dtype) for a, bc in payload]
    hbm = pl.BlockSpec(memory_space=pl.ANY)
    scratch = [pltpu.SemaphoreType.DMA((n * (N_DEV - 1),)), pltpu.SemaphoreType.DMA((n * (N_DEV - 1),)),
               pltpu.SemaphoreType.DMA((n,))]
    return [a for a, _ in payload], [hbm] * n, out_shapes, [hbm] * n, scratch


class _Plan:
    def __init__(self, bcasts, src_refs, dst_refs, send_sems, recv_sems, local_sems):
        x, y, c = lax.axis_index("x"), lax.axis_index("y"), lax.axis_index("c")
        me = 4 * x + 2 * y + c
        self.first, self.landed, self.relay, self.local = [], [], [], []
        for t, (bc, s_ref, d_ref) in enumerate(zip(bcasts, src_refs, dst_refs)):
            def remote(k, src, slot, to):
                sem = t * (N_DEV - 1) + k
                return pltpu.make_async_remote_copy(src_ref=src, dst_ref=d_ref.at[slot], send_sem=send_sems.at[sem],
                                                    recv_sem=recv_sems.at[sem], device_id=to, device_id_type=MESH_ID)
            if bc:
                chips = [(1 - x, y), (x, 1 - y), (1 - x, 1 - y)]
                self.first.append(remote(0, s_ref, me, (x, y, 1 - c)))
                for j, (px, py) in enumerate(chips):
                    cp = remote(1 + j, s_ref, me, (px, py, c))
                    self.first.append(cp)
                    self.landed.append(cp)
                    theirs = 4 * px + 2 * py + c
                    self.relay.append(remote(4 + j, d_ref.at[theirs], theirs, (x, y, 1 - c)))
                self.local.append(pltpu.make_async_copy(s_ref, d_ref.at[me], local_sems.at[t]))
            else:
                for k in range(1, N_DEV):
                    px = 1 - x if (k >> 2) & 1 else x
                    py = 1 - y if (k >> 1) & 1 else y
                    pc = 1 - c if k & 1 else c
                    self.first.append(remote(k - 1, s_ref.at[4 * px + 2 * py + pc], me, (px, py, pc)))
                self.local.append(pltpu.make_async_copy(s_ref.at[me], d_ref.at[me], local_sems.at[t]))

    def start(self):
        for cp in self.first + self.local:
            cp.start()

    def pass_on(self):
        for cp in self.landed:
            cp.wait_recv()
        for cp in self.relay:
            cp.start()

    def finish(self):
        for cp in self.first:
            cp.wait_send()
            if not any(cp is l for l in self.landed):
                cp.wait_recv()
        for cp in self.relay:
            cp.wait()
        for cp in self.local:
            cp.wait()


def _exchange(name, payload):
    n = len(payload)
    bcasts = [bc for _, bc in payload]
    arrays, in_specs, out_shapes, out_specs, scratch = _payload_layout(payload)

    def body(*refs):
        plan = _Plan(bcasts, refs[:n], refs[n:2 * n], *refs[2 * n:])
        plan.start()
        plan.pass_on()
        plan.finish()

    return pl.pallas_call(
        body,
        name=name,
        out_shape=tuple(out_shapes),
        in_specs=in_specs,
        out_specs=tuple(out_specs),
        scratch_shapes=scratch,
    )(*arrays)


def _pack(arrays, dtype, lead):
    lead_shape = arrays[0].shape[:lead]
    flat = [a.astype(dtype).reshape(lead_shape + (-1,)) for a in arrays]
    sizes = [f.shape[-1] for f in flat]
    total = sum(sizes)
    chunk = 16 * PACK_COLS
    padded = -(-total // chunk) * chunk
    if padded > total:
        flat.append(jnp.zeros(lead_shape + (padded - total,), dtype))
    buf = jnp.concatenate(flat, axis=-1)
    return buf.reshape(lead_shape + (padded // PACK_COLS, PACK_COLS)), sizes


def _unpack(buf, sizes, shapes, lead):
    lead_shape = buf.shape[:lead]
    flat = buf.reshape(lead_shape + (-1,))
    out, off = [], 0
    for n, shp in zip(sizes, shapes):
        out.append(lax.slice_in_dim(flat, off, off + n, axis=lead).reshape(lead_shape + tuple(shp)))
        off += n
    return out


def _to_slots(a, dim):
    shp = a.shape
    a = a.reshape(shp[:dim] + (N_DEV, shp[dim] // N_DEV) + shp[dim + 1:])
    return jnp.moveaxis(a, dim, 0)


def _from_slots(a, dim):
    a = jnp.moveaxis(a, 0, dim)
    shp = a.shape
    return a.reshape(shp[:dim] + (shp[dim] * shp[dim + 1],) + shp[dim + 2:])


def _mod_fwd(c_all, w, b):
    L, D, n = w.shape
    M = c_all.shape[0]

    def body(c_ref, w_ref, b_ref, o_ref):
        cv = c_ref[...]
        cs = (cv * _sigmoid(cv)).astype(BF16)
        o_ref[...] = jnp.dot(cs, w_ref[...], preferred_element_type=F32) + b_ref[...]

    return pl.pallas_call(
        body,
        name="mod_fwd",
        grid=(L,),
        out_shape=jax.ShapeDtypeStruct((L, M, n), F32),
        in_specs=[
            pl.BlockSpec((M, D), lambda l: (0, 0)),
            pl.BlockSpec((None, D, n), lambda l: (l, 0, 0)),
            pl.BlockSpec((None, 1, n), lambda l: (l, 0, 0)),
        ],
        out_specs=pl.BlockSpec((None, M, n), lambda l: (l, 0, 0)),
        compiler_params=_params("arbitrary"),
    )(c_all, w, b)


def _mod_bwd(c_all, dmod_all):
    L, M, n = dmod_all.shape
    D = c_all.shape[1]

    def body(c_ref, d_ref, dw_ref, db_ref):
        cv = c_ref[...]
        cs = (cv * _sigmoid(cv)).astype(BF16)
        d = d_ref[...]
        dw_ref[...] = lax.dot_general(cs, d.astype(BF16), (((0,), (0,)), ((), ())), preferred_element_type=F32)
        db_ref[...] = jnp.sum(d, axis=0, keepdims=True)

    return pl.pallas_call(
        body,
        name="mod_bwd",
        grid=(L,),
        out_shape=(jax.ShapeDtypeStruct((L, D, n), F32), jax.ShapeDtypeStruct((L, 1, n), F32)),
        in_specs=[
            pl.BlockSpec((M, D), lambda l: (0, 0)),
            pl.BlockSpec((None, M, n), lambda l: (l, 0, 0)),
        ],
        out_specs=(
            pl.BlockSpec((None, D, n), lambda l: (l, 0, 0)),
            pl.BlockSpec((None, 1, n), lambda l: (l, 0, 0)),
        ),
        compiler_params=_params("arbitrary"),
    )(c_all, dmod_all)


def _mm(name, a, w, *, w_form="full", out_dtype=BF16, bias=None, a_sq=False, epi=None, x=None, gate=None, u=None,
        norm=None, nb=None, rows_per_batch=None, tm_pref=512, payload=()):
    M, K = a.shape
    if w_form == "full":
        N = w.shape[1]
        nc = _pick(N, 1024, 128)
    elif w_form == "full_t":
        N = w.shape[0]
        nc = _pick(N, 1024, 128)
    elif w_form == "nslots":
        nc = w.shape[2]
        N = N_DEV * nc
    else:
        N = w.shape[1]
        nc = N
    ks = K // N_DEV
    n_chunks = N // nc
    tm = _pick(M if rows_per_batch is None else rows_per_batch, tm_pref, 16)
    steps = M // tm
    relay_step = (3 * steps) // 4
    nb_gate = nb is not None and nb.get("gate") is not None
    tpb_nb = rows_per_batch // tm if epi == "normbwd" else 1
    has_bias = bias is not None
    n_pay = len(payload)
    bcasts = [bc for _, bc in payload]
    nt = (((1,), (1,)), ((), ()))


    def body(*refs):
        it = iter(refs)
        a_ref = next(it)
        w_ref = next(it)
        b_ref = next(it) if has_bias else None
        x_ref = next(it) if epi == "resid" else None
        g_ref = next(it) if epi == "resid" else None
        gn_ref, sc_ref, sh_ref = (next(it), next(it), next(it)) if norm is not None else (None, None, None)
        u_ref = next(it) if epi == "relu2d" else None
        if epi == "normbwd":
            nx_ref, ndx_ref, ngn_ref, nsc_ref = next(it), next(it), next(it), next(it)
            ny_ref, ngt_ref = (next(it), next(it)) if nb_gate else (None, None)
        pay_src = [next(it) for _ in range(n_pay)]
        o_ref = next(it)
        y_ref = next(it) if epi == "resid" else None
        h_ref = next(it) if norm is not None else None
        if epi == "normbwd":
            dsh_ref, dsc_ref, dgn_ref = next(it), next(it), next(it)
            dy_ref, dgate_ref, cs_ref = (next(it), next(it), next(it)) if nb_gate else (None, None, None)
        pay_dst = [next(it) for _ in range(n_pay)]
        sems = list(it)

        if epi == "normbwd":
            @pl.when(pl.program_id(0) % tpb_nb == 0)
            def _():
                dsh_ref[...] = jnp.zeros_like(dsh_ref)
                dsc_ref[...] = jnp.zeros_like(dsc_ref)
                if nb_gate:
                    dgate_ref[...] = jnp.zeros_like(dgate_ref)

            @pl.when(pl.program_id(0) == 0)
            def _():
                dgn_ref[...] = jnp.zeros_like(dgn_ref)
                if nb_gate:
                    cs_ref[...] = jnp.zeros_like(cs_ref)

        if n_pay:
            @pl.when(pl.program_id(0) == 0)
            def _():
                _Plan(bcasts, pay_src, pay_dst, *sems).start()

            @pl.when(pl.program_id(0) == relay_step)
            def _():
                _Plan(bcasts, pay_src, pay_dst, *sems).pass_on()

        for rows in (slice(0, tm),):
            av = None if w_form == "kslots_t" else a_ref[rows, :]
            if a_sq:
                av = av * av
            for ci in range(n_chunks):
                cols = slice(ci * nc, (ci + 1) * nc)
                if w_form == "full":
                    acc = jnp.dot(av, w_ref[:, cols], preferred_element_type=F32)
                elif w_form == "full_t":
                    acc = lax.dot_general(av, w_ref[cols, :], nt, preferred_element_type=F32)
                elif w_form == "nslots":
                    acc = jnp.dot(av, w_ref[ci], preferred_element_type=F32)
                else:
                    acc = lax.dot_general(a_ref[rows, 0:ks], w_ref[0], nt, preferred_element_type=F32)
                    for j in range(1, N_DEV):
                        acc = acc + lax.dot_general(a_ref[rows, j * ks:(j + 1) * ks], w_ref[j], nt,
                                                    preferred_element_type=F32)
                if has_bias:
                    acc = acc + b_ref[:, cols]
                if epi == "resid":
                    xn = x_ref[rows, cols] + g_ref[:, cols] * acc
                    o_ref[rows, cols] = xn
                    y_ref[rows, cols] = acc.astype(BF16)
                    if norm is not None:
                        r = lax.rsqrt(jnp.mean(xn * xn, axis=-1, keepdims=True) + EPS)
                        h_ref[rows, :] = ((xn * r * gn_ref[...]) * (1.0 + sc_ref[...]) + sh_ref[...]).astype(BF16)
                elif epi == "relu":
                    o_ref[rows, cols] = jnp.maximum(acc, 0.0).astype(out_dtype)
                elif epi == "relu2d":
                    o_ref[rows, cols] = (acc * (2.0 * u_ref[rows, cols].astype(F32))).astype(out_dtype)
                elif epi == "normbwd":
                    xv = nx_ref[rows, :]
                    gn = ngn_ref[...]
                    r = lax.rsqrt(jnp.mean(xv * xv, axis=-1, keepdims=True) + EPS)
                    xhat = xv * r
                    dsh_ref[...] += jnp.sum(acc, axis=0, keepdims=True)
                    dsc_ref[...] += jnp.sum(acc * (xhat * gn), axis=0, keepdims=True)
                    dn = acc * (1.0 + nsc_ref[...])
                    dgn_ref[...] += jnp.sum(dn * xhat, axis=0, keepdims=True)
                    dxhat = dn * gn
                    dx = ndx_ref[rows, :] + r * (dxhat - xhat * jnp.mean(dxhat * xhat, axis=-1, keepdims=True))
                    o_ref[rows, :] = dx
                    if nb_gate:
                        dy = dx * ngt_ref[...]
                        dy_ref[rows, :] = dy.astype(BF16)
                        dgate_ref[...] += jnp.sum(dx * ny_ref[rows, :].astype(F32), axis=0, keepdims=True)
                        cs_ref[...] += jnp.sum(dy, axis=0, keepdims=True)
                else:
                    o_ref[rows, cols] = acc.astype(out_dtype)

        if n_pay:
            @pl.when(pl.program_id(0) == steps - 1)
            def _():
                _Plan(bcasts, pay_src, pay_dst, *sems).finish()

    args = [a, w]
    w_block = w.shape
    specs = [pl.BlockSpec((tm, K), lambda i: (i, 0)), pl.BlockSpec(w_block, lambda i: (0,) * len(w_block))]
    if has_bias:
        args.append(bias.reshape(1, N).astype(F32))
        specs.append(pl.BlockSpec((1, N), lambda i: (0, 0)))
    row_spec = pl.BlockSpec((tm, N), lambda i: (i, 0))
    if epi == "resid":
        tpb = rows_per_batch // tm
        vec_spec = pl.BlockSpec((None, 1, N), lambda i: (i // tpb, 0, 0))
        args += [x, gate]
        specs += [row_spec, vec_spec]
        out_shape = [jax.ShapeDtypeStruct((M, N), F32), jax.ShapeDtypeStruct((M, N), BF16)]
        out_specs = [row_spec, row_spec]
        if norm is not None:
            assert n_chunks == 1
            args += list(norm)
            specs += [pl.BlockSpec((1, N), lambda i: (0, 0)), vec_spec, vec_spec]
            out_shape.append(jax.ShapeDtypeStruct((M, N), BF16))
            out_specs.append(row_spec)
    elif epi == "normbwd":
        assert n_chunks == 1
        vec_spec = pl.BlockSpec((None, 1, N), lambda i: (i // tpb_nb, 0, 0))
        one_spec = pl.BlockSpec((1, N), lambda i: (0, 0))
        nbat = M // rows_per_batch
        f_vec, f_one = jax.ShapeDtypeStruct((nbat, 1, N), F32), jax.ShapeDtypeStruct((1, N), F32)
        args += [nb["x"], nb["dx_in"], nb["gnorm"], nb["sc"]]
        specs += [row_spec, row_spec, one_spec, vec_spec]
        out_shape = [jax.ShapeDtypeStruct((M, N), F32), f_vec, f_vec, f_one]
        out_specs = [row_spec, vec_spec, vec_spec, one_spec]
        if nb_gate:
            args += list(nb["gate"])
            specs += [row_spec, vec_spec]
            out_shape += [jax.ShapeDtypeStruct((M, N), BF16), f_vec, f_one]
            out_specs += [row_spec, vec_spec, one_spec]
    else:
        if epi == "relu2d":
            args.append(u)
            specs.append(row_spec)
        out_shape = [jax.ShapeDtypeStruct((M, N), out_dtype)]
        out_specs = [row_spec]
    scratch = []
    if n_pay:
        p_arrays, p_in, p_shapes, p_out, scratch = _payload_layout(payload)
        args += p_arrays
        specs += p_in
        out_shape += p_shapes
        out_specs += p_out
    res = pl.pallas_call(
        body,
        name=name,
        grid=(steps,),
        out_shape=tuple(out_shape),
        in_specs=specs,
        out_specs=tuple(out_specs),
        scratch_shapes=scratch,
        compiler_params=_params("arbitrary" if (n_pay or epi == "normbwd") else "parallel"),
    )(*args)
    if epi == "resid":
        n_own = 3 if norm is not None else 2
    elif epi == "normbwd":
        n_own = 7 if nb_gate else 4
    else:
        n_own = 1
    own = res[0] if n_own == 1 else tuple(res[:n_own])
    return (own, list(res[n_own:])) if n_pay else own


def _mm_tn(name, a, b, *, a_sq=False, out_form="full", halves=False, colsum=False, tt_pref=1024, tk_pref=1024,
           tn_pref=1024, payload=()):
    T, K = a.shape
    N = b.shape[1]
    tt = _pick(T, tt_pref, 16)
    tk = K // N_DEV if out_form == "kslots" else _pick(K, tk_pref, 128)
    tn = N // N_DEV if out_form == "nslots" else _pick(N, tn_pref, 128)
    nt_steps = T // tt
    grid = (K // tk, N // tn, nt_steps)
    slots = out_form != "full"
    assert not colsum or K == tk
    n_pay = len(payload)
    bcasts = [bc for _, bc in payload]

    def body(*refs):
        it = iter(refs)
        a_ref, b_ref = next(it), next(it)
        pay_src = [next(it) for _ in range(n_pay)]
        o_ref = next(it)
        o2_ref = next(it) if halves else None
        cs_ref = next(it) if colsum else None
        pay_dst = [next(it) for _ in range(n_pay)]
        acc = next(it) if slots else o_ref
        sems = list(it)
        step = (pl.program_id(0) * grid[1] + pl.program_id(1)) * grid[2] + pl.program_id(2)

        if n_pay:
            @pl.when(step == 0)
            def _():
                _Plan(bcasts, pay_src, pay_dst, *sems).start()

        @pl.when(pl.program_id(2) == 0)
        def _():
            acc[...] = jnp.zeros_like(acc)
            if colsum:
                cs_ref[...] = jnp.zeros_like(cs_ref)

        bv = b_ref[...]
        av = a_ref[...]
        if a_sq:
            av = av * av
        acc[...] += lax.dot_general(av, bv, (((0,), (0,)), ((), ())), preferred_element_type=F32)
        if colsum:
            cs_ref[...] += jnp.sum(bv.astype(F32), axis=0, keepdims=True)

        if slots:
            @pl.when(pl.program_id(2) == nt_steps - 1)
            def _():
                if halves:
                    o_ref[...] = acc[0:tk // 2, :].astype(o_ref.dtype)
                    o2_ref[...] = acc[tk // 2:, :].astype(o2_ref.dtype)
                else:
                    o_ref[...] = acc[...].astype(o_ref.dtype)

        if n_pay:
            @pl.when(step == grid[0] * grid[1] * grid[2] - 1)
            def _():
                plan = _Plan(bcasts, pay_src, pay_dst, *sems)
                plan.pass_on()
                plan.finish()

    if out_form == "full":
        out_shape = [jax.ShapeDtypeStruct((K, N), F32)]
        out_specs = [pl.BlockSpec((tk, tn), lambda k, n, t: (k, n))]
    else:
        assert not halves or out_form == "kslots" or K == tk
        rows = tk // 2 if halves else tk
        if out_form == "nslots":
            shape, spec = (N_DEV, K // tk * rows, tn), pl.BlockSpec((None, rows, tn), lambda k, n, t: (n, k, 0))
        else:
            shape, spec = (N_DEV, rows, N), pl.BlockSpec((None, rows, tn), lambda k, n, t: (k, 0, n))
        out_shape = [jax.ShapeDtypeStruct(shape, BF16)] * (2 if halves else 1)
        out_specs = [spec] * (2 if halves else 1)
    if colsum:
        out_shape.append(jax.ShapeDtypeStruct((1, N), F32))
        out_specs.append(pl.BlockSpec((1, tn), lambda k, n, t: (0, n)))
    args = [a, b]
    in_specs = [pl.BlockSpec((tt, tk), lambda k, n, t: (t, k)), pl.BlockSpec((tt, tn), lambda k, n, t: (t, n))]
    scratch = [pltpu.VMEM((tk, tn), F32)] if slots else []
    n_own = len(out_shape)
    if n_pay:
        p_arrays, p_in, p_shapes, p_out, p_scratch = _payload_layout(payload)
        args += p_arrays
        in_specs += p_in
        out_shape += p_shapes
        out_specs += p_out
        scratch += p_scratch
    res = pl.pallas_call(
        body,
        name=name,
        grid=grid,
        out_shape=tuple(out_shape),
        in_specs=in_specs,
        out_specs=tuple(out_specs),
        scratch_shapes=scratch,
        compiler_params=_params(*(("arbitrary",) * 3 if n_pay else ("parallel", "parallel", "arbitrary"))),
    )(*args)
    own = res[0] if n_own == 1 else tuple(res[:n_own])
    return (own, list(res[n_own:])) if n_pay else own


def _normmod_fwd(name, x, gnorm, sc, sh, S, tm_pref=512):
    T, D = x.shape
    tm = _pick(S, tm_pref, 16)
    tpb = S // tm

    def body(x_ref, g_ref, sc_ref, sh_ref, o_ref):
        xv = x_ref[...]
        r = lax.rsqrt(jnp.mean(xv * xv, axis=-1, keepdims=True) + EPS)
        n = xv * r * g_ref[...]
        o_ref[...] = (n * (1.0 + sc_ref[...]) + sh_ref[...]).astype(BF16)

    row = pl.BlockSpec((tm, D), lambda i: (i, 0))
    vec = pl.BlockSpec((None, 1, D), lambda i: (i // tpb, 0, 0))
    return pl.pallas_call(
        body,
        name=name,
        grid=(T // tm,),
        out_shape=jax.ShapeDtypeStruct((T, D), BF16),
        in_specs=[row, pl.BlockSpec((1, D), lambda i: (0, 0)), vec, vec],
        out_specs=row,
        compiler_params=_params("parallel"),
    )(x, gnorm, sc, sh)


ATT_ROWS = GROUP * ATT_BLOCK
ATT_SPAN = 2 * ATT_BLOCK
ATT_SCALE = HEAD_DIM ** -0.5


def _attn_table(sinks):
    slopes = jnp.asarray(np.array([2.0 ** (-8.0 * (h + 1) / N_HEADS) for h in range(N_HEADS)], np.float32))
    r = jnp.arange(ATT_BLOCK)[:, None]
    cc = jnp.arange(ATT_SPAN)[None, :]
    dist = r + ATT_BLOCK - cc
    ok = (dist >= 0) & (dist < ATT_BLOCK)
    tab = jnp.where(ok[None], -slopes[:, None, None] * dist.astype(F32)[None], NEG_BIG)
    tab = jnp.where((cc == 0)[None], sinks.astype(F32)[:, None, None], tab)
    return tab.reshape(N_KV_HEADS, ATT_ROWS, ATT_SPAN)


KV_COLS = N_KV_HEADS * HEAD_DIM


PAIR = 2 * HEAD_DIM


def _pair_tables(tab):
    hkv, rows, span = tab.shape
    pairs = rows // (2 * ATT_BLOCK)
    return tab.reshape(hkv, pairs, 2, ATT_BLOCK, span).transpose(0, 2, 1, 3, 4).reshape(hkv, 2, pairs * ATT_BLOCK, span)


def _stack_pairs(ref, pairs, col0):
    return jnp.concatenate([ref[:, col0 + j * PAIR:col0 + (j + 1) * PAIR] for j in range(pairs)], axis=0)


def _unstack_pairs(v, pairs):
    return jnp.concatenate([v[j * ATT_BLOCK:(j + 1) * ATT_BLOCK, :] for j in range(pairs)], axis=1)


def _swap_halves(x):
    return jnp.concatenate([x[:, HEAD_DIM:], x[:, :HEAD_DIM]], axis=1)


def _load_span(prev_ref, cur_ref, buf_e, buf_o, kv, mult):
    lane = lax.broadcasted_iota(jnp.int32, (ATT_BLOCK, PAIR), 1)
    row = lax.broadcasted_iota(jnp.int32, (ATT_BLOCK, PAIR), 0)
    own = lane // HEAD_DIM == kv
    for r0, ref, first in ((0, prev_ref, True), (ATT_BLOCK, cur_ref, False)):
        v = ref[...]
        if mult != 1.0:
            v = v * mult
        zero = jnp.zeros_like(v)
        mine = jnp.where(own, v, zero)
        if first:
            mine = jnp.where(row > 0, mine, zero)
        other = _swap_halves(mine)
        buf_e[r0:r0 + ATT_BLOCK, :] = mine if kv == 0 else other
        buf_o[r0:r0 + ATT_BLOCK, :] = other if kv == 0 else mine


def _first_block_penalty(i):
    col = lax.broadcasted_iota(jnp.int32, (1, ATT_SPAN), 1)
    return jnp.where((col < ATT_BLOCK) & (col > 0), jnp.where(i > 0, 0.0, NEG_BIG), 0.0).astype(F32)


def _attn_probs(qp, kbuf, bias, first_pen):
    nt = (((1,), (1,)), ((), ()))
    s = lax.dot_general(qp, kbuf, nt, preferred_element_type=F32) + bias + first_pen
    m = jnp.max(s, axis=-1, keepdims=True)
    e = jnp.exp(s - m)
    return e * (1.0 / jnp.sum(e, axis=-1, keepdims=True))


def _attn_specs(nb, q_dim, pairs, row_q, row_cur, row_prev):
    kblk = q_dim // KV_COLS
    qspec = pl.BlockSpec((ATT_BLOCK, q_dim), lambda b, i: (row_q(b, i), 0))
    kv_specs = [pl.BlockSpec((ATT_BLOCK, KV_COLS), (lambda b, i, r=r, c=c: (r(b, i), c)))
                for c in (kblk, kblk + 1) for r in (row_cur, row_prev)]
    tspec = pl.BlockSpec((N_KV_HEADS, 2, pairs * ATT_BLOCK, ATT_SPAN), lambda b, i: (0, 0, 0, 0))
    return qspec, kv_specs, tspec


def _attn_fwd(qkv, tab2, B, S, payload=()):
    assert N_KV_HEADS == 2 and GROUP % 2 == 0
    T = qkv.shape[0]
    pairs = GROUP // 2
    gw = pairs * PAIR
    nb = S // ATT_BLOCK
    q_dim = N_HEADS * HEAD_DIM
    n_pay = len(payload)
    bcasts = [bc for _, bc in payload]
    steps = B * nb

    def body(*refs):
        q_ref, kc_ref, kp_ref, vc_ref, vp_ref, tab_ref = refs[:6]
        pay_src = refs[6:6 + n_pay]
        o_ref = refs[6 + n_pay]
        pay_dst = refs[7 + n_pay:7 + 2 * n_pay]
        ke, ko, ve, vo = refs[7 + 2 * n_pay:11 + 2 * n_pay]
        sems = refs[11 + 2 * n_pay:]
        i = pl.program_id(1)
        if n_pay:
            step = pl.program_id(0) * nb + i

            @pl.when(step == 0)
            def _():
                _Plan(bcasts, pay_src, pay_dst, *sems).start()

            @pl.when(step == (3 * steps) // 4)
            def _():
                _Plan(bcasts, pay_src, pay_dst, *sems).pass_on()

            @pl.when(step == steps - 1)
            def _():
                _Plan(bcasts, pay_src, pay_dst, *sems).finish()

        pen = _first_block_penalty(i)
        for kv in range(N_KV_HEADS):
            _load_span(kp_ref, kc_ref, ke.at[kv], ko.at[kv], kv, ATT_SCALE)
            _load_span(vp_ref, vc_ref, ve.at[kv], vo.at[kv], kv, 1.0)
            qp = _stack_pairs(q_ref, pairs, kv * gw)
            p_e = _attn_probs(qp, ke[kv], tab_ref[kv, 0], pen)
            p_o = _attn_probs(qp, ko[kv], tab_ref[kv, 1], pen)
            o = jnp.dot(p_e.astype(BF16), ve[kv], preferred_element_type=F32)
            o = o + jnp.dot(p_o.astype(BF16), vo[kv], preferred_element_type=F32)
            o_ref[:, kv * gw:(kv + 1) * gw] = _unstack_pairs(o.astype(BF16), pairs)

    qspec, kv_specs, tspec = _attn_specs(nb, q_dim, pairs, lambda b, i: b * nb + i, lambda b, i: b * nb + i,
                                         lambda b, i: b * nb + jnp.maximum(i - 1, 0))
    args, in_specs = [qkv] * 5 + [tab2], [qspec] + kv_specs + [tspec]
    out_shape, out_specs = [jax.ShapeDtypeStruct((T, q_dim), BF16)], [qspec]
    scratch = [pltpu.VMEM((N_KV_HEADS, ATT_SPAN, PAIR), BF16)] * 4
    if n_pay:
        p_arrays, p_in, p_shapes, p_out, p_scratch = _payload_layout(payload)
        args += p_arrays
        in_specs += p_in
        out_shape += p_shapes
        out_specs += p_out
        scratch += p_scratch
    res = pl.pallas_call(
        body,
        name="attn_fwd",
        grid=(B, nb),
        out_shape=tuple(out_shape),
        in_specs=in_specs,
        out_specs=tuple(out_specs),
        scratch_shapes=scratch,
        compiler_params=_params(*(("arbitrary",) * 2 if n_pay else ("parallel", "arbitrary"))),
    )(*args)
    return (res[0], list(res[1:])) if n_pay else res[0]


def _attn_bwd(qkv, o, do, tab2, B, S):
    T = qkv.shape[0]
    hd = HEAD_DIM
    pairs = GROUP // 2
    gw = pairs * PAIR
    rows = pairs * ATT_BLOCK
    nb = S // ATT_BLOCK
    last = nb - 1
    q_dim = N_HEADS * hd
    tn = (((0,), (0,)), ((), ()))
    nt = (((1,), (1,)), ((), ()))

    def body(q_ref, kc_ref, kp_ref, vc_ref, vp_ref, tab_ref, o_ref, do_ref,
             dq_ref, dkv_ref, dsink_ref, carry_k, carry_v, ke, ko, ve, vo):
        i = pl.program_id(1)

        @pl.when(i == 0)
        def _():
            dsink_ref[...] = jnp.zeros_like(dsink_ref)

        def emit(kv, dk_rows, dv_rows):
            dkv_ref[:, kv * hd:(kv + 1) * hd] = dk_rows.astype(BF16)
            dkv_ref[:, KV_COLS + kv * hd:KV_COLS + (kv + 1) * hd] = dv_rows.astype(BF16)

        @pl.when(i < nb)
        def _():
            pen = _first_block_penalty(i)
            low = lax.broadcasted_iota(jnp.int32, (rows, PAIR), 1) < hd
            low_k = lax.broadcasted_iota(jnp.int32, (ATT_SPAN, PAIR), 1) < hd
            keep = lax.broadcasted_iota(jnp.int32, (ATT_SPAN, hd), 0) > 0
            hg = lax.broadcasted_iota(jnp.int32, (pairs, rows), 0)
            hr = lax.broadcasted_iota(jnp.int32, (pairs, rows), 1)
            head_of = jnp.where(hr // ATT_BLOCK == hg, 1.0, 0.0).astype(BF16)
            dks, dvs = [], []
            for kv in range(N_KV_HEADS):
                _load_span(kp_ref, kc_ref, ke.at[kv], ko.at[kv], kv, ATT_SCALE)
                _load_span(vp_ref, vc_ref, ve.at[kv], vo.at[kv], kv, 1.0)
                qp = _stack_pairs(q_ref, pairs, kv * gw)
                dop = _stack_pairs(do_ref, pairs, kv * gw)
                op = _stack_pairs(o_ref, pairs, kv * gw)
                p_e = _attn_probs(qp, ke[kv], tab_ref[kv, 0], pen)
                p_o = _attn_probs(qp, ko[kv], tab_ref[kv, 1], pen)
                prod = dop.astype(F32) * op.astype(F32)
                d_e = jnp.sum(jnp.where(low, prod, 0.0), axis=-1, keepdims=True)
                d_o = jnp.sum(prod, axis=-1, keepdims=True) - d_e
                ds_e = (p_e * (lax.dot_general(dop, ve[kv], nt, preferred_element_type=F32) - d_e)).astype(BF16)
                ds_o = (p_o * (lax.dot_general(dop, vo[kv], nt, preferred_element_type=F32) - d_o)).astype(BF16)
                dq = jnp.dot(ds_e, ke[kv], preferred_element_type=F32) + jnp.dot(ds_o, ko[kv], preferred_element_type=F32)
                dq_ref[:, kv * gw:(kv + 1) * gw] = _unstack_pairs(dq.astype(BF16), pairs)
                dsink_ref[kv, 0] += jnp.dot(head_of, ds_e, preferred_element_type=F32)[:, 0:128]
                dsink_ref[kv, 1] += jnp.dot(head_of, ds_o, preferred_element_type=F32)[:, 0:128]
                dk2 = jnp.where(low_k, lax.dot_general(ds_e, qp, tn, preferred_element_type=F32),
                                lax.dot_general(ds_o, qp, tn, preferred_element_type=F32))
                dv2 = jnp.where(low_k, lax.dot_general(p_e.astype(BF16), dop, tn, preferred_element_type=F32),
                                lax.dot_general(p_o.astype(BF16), dop, tn, preferred_element_type=F32))
                dks.append(jnp.where(keep, (dk2[:, :hd] + dk2[:, hd:]) * ATT_SCALE, 0.0))
                dvs.append(jnp.where(keep, dv2[:, :hd] + dv2[:, hd:], 0.0))

            @pl.when(i > 0)
            def _():
                for kv in range(N_KV_HEADS):
                    emit(kv, carry_k[kv] + dks[kv][0:ATT_BLOCK], carry_v[kv] + dvs[kv][0:ATT_BLOCK])

            for kv in range(N_KV_HEADS):
                carry_k[kv] = dks[kv][ATT_BLOCK:]
                carry_v[kv] = dvs[kv][ATT_BLOCK:]

        @pl.when(i == nb)
        def _():
            for kv in range(N_KV_HEADS):
                emit(kv, carry_k[kv], carry_v[kv])

    qspec, kv_specs, tspec = _attn_specs(nb, q_dim, pairs, lambda b, i: b * nb + jnp.minimum(i, last),
                                         lambda b, i: b * nb + jnp.minimum(i, last),
                                         lambda b, i: b * nb + jnp.clip(i - 1, 0, last))
    dkv = pl.BlockSpec((ATT_BLOCK, 2 * KV_COLS), lambda b, i: (b * nb + jnp.maximum(i - 1, 0), 0))
    dsk = pl.BlockSpec((None, N_KV_HEADS, 2, pairs, 128), lambda b, i: (b, 0, 0, 0, 0))
    return pl.pallas_call(
        body,
        name="attn_bwd",
        grid=(B, nb + 1),
        out_shape=(
            jax.ShapeDtypeStruct((T, q_dim), BF16),
            jax.ShapeDtypeStruct((T, 2 * KV_COLS), BF16),
            jax.ShapeDtypeStruct((B, N_KV_HEADS, 2, pairs, 128), F32),
        ),
        in_specs=[qspec] + kv_specs + [tspec, qspec, qspec],
        out_specs=(qspec, dkv, dsk),
        scratch_shapes=[pltpu.VMEM((N_KV_HEADS, ATT_BLOCK, hd), F32), pltpu.VMEM((N_KV_HEADS, ATT_BLOCK, hd), F32)]
        + [pltpu.VMEM((N_KV_HEADS, ATT_SPAN, PAIR), BF16)] * 4,
        compiler_params=_params("arbitrary", "arbitrary"),
    )(qkv, qkv, qkv, qkv, qkv, tab2, o, do)


def _conv_tiles(S):
    ts = _pick(S, 256, CONV_HALO)
    return ts, S // ts


def _conv_chunks(C, ts):
    lane = _pick(C, 128, 128)
    return lane, C // lane, _pick(ts, 128, 8)


def _conv_weight_chunks(w_dw, C):
    lane = _pick(C, 128, 128)
    w = jnp.pad(w_dw, ((0, CONV_HALO - CONV_WIDTH), (0, 0)))
    return w.reshape(CONV_HALO, C // lane, lane).transpose(1, 0, 2)


def _conv_fwd(u, w3, b_dw, ln_g, ln_b, S):
    T, C2 = u.shape
    C = C2 // 2
    B = T // S
    ts, nj = _conv_tiles(S)
    hb = ts // CONV_HALO
    lane, nc, rc = _conv_chunks(C, ts)

    def body(a_ref, g_ref, ap_ref, gp_ref, w_ref, bdw_ref, lg_ref, lb_ref, cv_ref, s_ref, buf, cvb):
        j = pl.program_id(1)
        glu_prev = ap_ref[...] * _sigmoid(gp_ref[...]) * (j > 0).astype(F32)
        glu = a_ref[...] * _sigmoid(g_ref[...])
        for cc in range(nc):
            buf[cc, 0:CONV_HALO, :] = glu_prev[:, cc * lane:(cc + 1) * lane]
            buf[cc, CONV_HALO:, :] = glu[:, cc * lane:(cc + 1) * lane]

        def chunk(cc, carry):
            for r0 in range(0, ts, rc):
                acc = jnp.zeros((rc, lane), F32)
                for kk in range(CONV_WIDTH):
                    lo = CONV_HALO - (CONV_WIDTH - 1 - kk) + r0
                    acc = acc + w_ref[cc, kk:kk + 1, :] * buf[cc, lo:lo + rc, :]
                cvb[cc, r0:r0 + rc, :] = acc
            return carry

        lax.fori_loop(0, nc, chunk, 0)
        for cc in range(nc):
            cv_ref[:, cc * lane:(cc + 1) * lane] = cvb[cc] + bdw_ref[:, cc * lane:(cc + 1) * lane]
        cv = cv_ref[...]
        mu = jnp.mean(cv, axis=-1, keepdims=True)
        xc = cv - mu
        rstd = lax.rsqrt(jnp.mean(xc * xc, axis=-1, keepdims=True) + EPS)
        ln = xc * rstd * lg_ref[...] + lb_ref[...]
        s_ref[...] = (ln * _sigmoid(ln)).astype(BF16)

    a_cur = pl.BlockSpec((ts, C), lambda b, j: (b * nj + j, 0))
    g_cur = pl.BlockSpec((ts, C), lambda b, j: (b * nj + j, 1))
    a_prev = pl.BlockSpec((CONV_HALO, C), lambda b, j: (jnp.maximum((b * nj + j) * hb - 1, 0), 0))
    g_prev = pl.BlockSpec((CONV_HALO, C), lambda b, j: (jnp.maximum((b * nj + j) * hb - 1, 0), 1))
    wspec = pl.BlockSpec((nc, CONV_HALO, lane), lambda b, j: (0, 0, 0))
    one = pl.BlockSpec((1, C), lambda b, j: (0, 0))
    return pl.pallas_call(
        body,
        name="conv_fwd",
        grid=(B, nj),
        out_shape=(jax.ShapeDtypeStruct((T, C), F32), jax.ShapeDtypeStruct((T, C), BF16)),
        in_specs=[a_cur, g_cur, a_prev, g_prev, wspec, one, one, one],
        out_specs=(a_cur, a_cur),
        scratch_shapes=[pltpu.VMEM((nc, CONV_HALO + ts, lane), F32), pltpu.VMEM((nc, ts, lane), F32)],
        compiler_params=_params("parallel", "arbitrary"),
    )(u, u, u, u, w3, b_dw, ln_g, ln_b)


def _lnsilu_bwd(ds, cv, ln_g, ln_b, tm_pref=256):
    T, C = cv.shape
    tm = _pick(T, tm_pref, 16)

    def body(ds_ref, cv_ref, lg_ref, lb_ref, dcv_ref, dlg_ref, dlb_ref, dbdw_ref):
        @pl.when(pl.program_id(0) == 0)
        def _():
            dlg_ref[...] = jnp.zeros_like(dlg_ref)
            dlb_ref[...] = jnp.zeros_like(dlb_ref)
            dbdw_ref[...] = jnp.zeros_like(dbdw_ref)

        cv_v = cv_ref[...]
        g = lg_ref[...]
        mu = jnp.mean(cv_v, axis=-1, keepdims=True)
        xc = cv_v - mu
        rstd = lax.rsqrt(jnp.mean(xc * xc, axis=-1, keepdims=True) + EPS)
        xhat = xc * rstd
        ln = xhat * g + lb_ref[...]
        sg = _sigmoid(ln)
        dln = ds_ref[...] * (sg * (1.0 + ln * (1.0 - sg)))
        dlg_ref[...] += jnp.sum(dln * xhat, axis=0, keepdims=True)
        dlb_ref[...] += jnp.sum(dln, axis=0, keepdims=True)
        dxhat = dln * g
        dcv = rstd * (dxhat - jnp.mean(dxhat, axis=-1, keepdims=True)
                      - xhat * jnp.mean(dxhat * xhat, axis=-1, keepdims=True))
        dcv_ref[...] = dcv
        dbdw_ref[...] += jnp.sum(dcv, axis=0, keepdims=True)

    row = pl.BlockSpec((tm, C), lambda i: (i, 0))
    one = pl.BlockSpec((1, C), lambda i: (0, 0))
    return pl.pallas_call(
        body,
        name="lnsilu_bwd",
        grid=(T // tm,),
        out_shape=(jax.ShapeDtypeStruct((T, C), F32),) + (jax.ShapeDtypeStruct((1, C), F32),) * 3,
        in_specs=[row, row, one, one],
        out_specs=(row, one, one, one),
        compiler_params=_params("arbitrary"),
    )(ds, cv, ln_g, ln_b)


def _conv_bwd(dcv, u, w3, S):
    T, C2 = u.shape
    C = C2 // 2
    B = T // S
    ts, nj = _conv_tiles(S)
    hb = ts // CONV_HALO
    n_halo_blocks = T // CONV_HALO
    lane, nc, rc = _conv_chunks(C, ts)

    def body(dcv_ref, dnx_ref, a_ref, g_ref, ap_ref, gp_ref, w_ref, du_ref, dw_ref, gbuf, dbuf, dglu, dw8):
        b, j = pl.program_id(0), pl.program_id(1)

        @pl.when((b == 0) & (j == 0))
        def _():
            dw8[...] = jnp.zeros_like(dw8)

        a = a_ref[...]
        sg = _sigmoid(g_ref[...])
        glu_prev = ap_ref[...] * _sigmoid(gp_ref[...]) * (j > 0).astype(F32)
        glu = a * sg
        dcur = dcv_ref[...]
        dnext = dnx_ref[...] * (j < nj - 1).astype(F32)
        for cc in range(nc):
            cols = slice(cc * lane, (cc + 1) * lane)
            gbuf[cc, 0:CONV_HALO, :] = glu_prev[:, cols]
            gbuf[cc, CONV_HALO:, :] = glu[:, cols]
            dbuf[cc, 0:ts, :] = dcur[:, cols]
            dbuf[cc, ts:, :] = dnext[:, cols]

        def chunk(cc, carry):
            for r0 in range(0, ts, rc):
                acc = jnp.zeros((rc, lane), F32)
                for kk in range(CONV_WIDTH):
                    d = CONV_WIDTH - 1 - kk
                    acc = acc + w_ref[cc, kk:kk + 1, :] * dbuf[cc, r0 + d:r0 + d + rc, :]
                dglu[cc, r0:r0 + rc, :] = acc
            for kk in range(CONV_WIDTH):
                d = CONV_WIDTH - 1 - kk
                p = jnp.zeros((rc, lane), F32)
                for r0 in range(0, ts, rc):
                    lo = CONV_HALO - d + r0
                    p = p + dbuf[cc, r0:r0 + rc, :] * gbuf[cc, lo:lo + rc, :]
                dw8[cc, kk * 8:(kk + 1) * 8, :] += jnp.sum(p.reshape(rc // 8, 8, lane), axis=0)
            return carry

        lax.fori_loop(0, nc, chunk, 0)
        for cc in range(nc):
            cols = slice(cc * lane, (cc + 1) * lane)
            dgl = dglu[cc]
            du_ref[:, cc * lane:(cc + 1) * lane] = (dgl * sg[:, cols]).astype(BF16)
            du_ref[:, C + cc * lane:C + (cc + 1) * lane] = (dgl * a[:, cols] * sg[:, cols] * (1.0 - sg[:, cols])).astype(BF16)

        @pl.when((b == B - 1) & (j == nj - 1))
        def _():
            dw_ref[...] = jnp.zeros_like(dw_ref)
            for kk in range(CONV_WIDTH):
                dw_ref[:, kk:kk + 1, :] = jnp.sum(dw8[:, kk * 8:(kk + 1) * 8, :], axis=1, keepdims=True)

    a_cur = pl.BlockSpec((ts, C), lambda b, j: (b * nj + j, 0))
    g_cur = pl.BlockSpec((ts, C), lambda b, j: (b * nj + j, 1))
    a_prev = pl.BlockSpec((CONV_HALO, C), lambda b, j: (jnp.maximum((b * nj + j) * hb - 1, 0), 0))
    g_prev = pl.BlockSpec((CONV_HALO, C), lambda b, j: (jnp.maximum((b * nj + j) * hb - 1, 0), 1))
    d_next = pl.BlockSpec((CONV_HALO, C), lambda b, j: (jnp.minimum((b * nj + j + 1) * hb, n_halo_blocks - 1), 0))
    wspec = pl.BlockSpec((nc, CONV_HALO, lane), lambda b, j: (0, 0, 0))
    return pl.pallas_call(
        body,
        name="conv_bwd",
        grid=(B, nj),
        out_shape=(jax.ShapeDtypeStruct((T, C2), BF16), jax.ShapeDtypeStruct((nc, CONV_HALO, lane), F32)),
        in_specs=[a_cur, d_next, a_cur, g_cur, a_prev, g_prev, wspec],
        out_specs=(pl.BlockSpec((ts, C2), lambda b, j: (b * nj + j, 0)), wspec),
        scratch_shapes=[
            pltpu.VMEM((nc, CONV_HALO + ts, lane), F32),
            pltpu.VMEM((nc, ts + CONV_HALO, lane), F32),
            pltpu.VMEM((nc, ts, lane), F32),
            pltpu.VMEM((nc, CONV_HALO * 8, lane), F32),
        ],
        compiler_params=_params("arbitrary", "arbitrary"),
    )(dcv, dcv, u, u, u, u, w3)


def _loss_head(x, tgt, gfin, y, gate, S, tm_pref=256):
    T, D = x.shape
    tm = _pick(S, tm_pref, 16)
    tpb = S // tm

    def body(x_ref, t_ref, g_ref, y_ref, gt_ref, dx_ref, loss_ref, dg_ref, dy_ref, dgate_ref):
        @pl.when(pl.program_id(0) == 0)
        def _():
            loss_ref[...] = jnp.zeros_like(loss_ref)
            dg_ref[...] = jnp.zeros_like(dg_ref)

        @pl.when(pl.program_id(0) % tpb == 0)
        def _():
            dgate_ref[...] = jnp.zeros_like(dgate_ref)

        xv = x_ref[...]
        g = g_ref[...]
        r = lax.rsqrt(jnp.mean(xv * xv, axis=-1, keepdims=True) + EPS)
        xhat = xv * r
        e = xhat * g - t_ref[...]
        row_loss = jnp.mean(e * e, axis=-1, keepdims=True)
        loss_ref[...] += 0.5 * jnp.sum(row_loss, axis=0, keepdims=True)
        dy = e * (1.0 / D)
        dg_ref[...] += jnp.sum(dy * xhat, axis=0, keepdims=True)
        dxhat = dy * g
        dx = r * (dxhat - xhat * jnp.mean(dxhat * xhat, axis=-1, keepdims=True))
        dx_ref[...] = dx
        dy_ref[...] = (dx * gt_ref[...]).astype(BF16)
        dgate_ref[...] += jnp.sum(dx * y_ref[...].astype(F32), axis=0, keepdims=True)

    row = pl.BlockSpec((tm, D), lambda i: (i, 0))
    vec = pl.BlockSpec((None, 1, D), lambda i: (i // tpb, 0, 0))
    one = pl.BlockSpec((1, D), lambda i: (0, 0))
    return pl.pallas_call(
        body,
        name="loss_head",
        grid=(T // tm,),
        out_shape=(
            jax.ShapeDtypeStruct((T, D), F32),
            jax.ShapeDtypeStruct((8, 128), F32),
            jax.ShapeDtypeStruct((1, D), F32),
            jax.ShapeDtypeStruct((T, D), BF16),
            jax.ShapeDtypeStruct((T // S, 1, D), F32),
        ),
        in_specs=[row, row, one, row, vec],
        out_specs=(row, pl.BlockSpec((8, 128), lambda i: (0, 0)), one, row, vec),
        compiler_params=_params("arbitrary"),
    )(x, tgt, gfin, y, gate)


def _adam(name, parts, w, m, v, tm_pref=256):
    L = len(parts)
    P, R, C = parts[0].shape
    fit = VMEM_LIMIT_BYTES // 2 // (2 * L * P * C * parts[0].dtype.itemsize)
    tm = _pick(R, max(16, min(tm_pref, fit // 16 * 16)), 16)
    tiles = R // tm
    c1 = 1.0 - ADAM_B1 ** ADAM_STEP
    c2 = 1.0 - ADAM_B2 ** ADAM_STEP

    def body(*refs):
        p_refs = refs[:L]
        w_ref, m_ref, v_ref, g_ref, d_ref, mo_ref, vo_ref = refs[L:]
        for l in range(L):
            @pl.when(pl.program_id(0) == l)
            def _():
                g = p_refs[l][0].astype(F32)
                for i in range(1, P):
                    g = g + p_refs[l][i].astype(F32)
                m_new = ADAM_B1 * m_ref[...] + (1.0 - ADAM_B1) * g
                v_new = ADAM_B2 * v_ref[...] + (1.0 - ADAM_B2) * (g * g)
                m_hat = m_new / c1
                v_hat = v_new / c2
                g_ref[...] = g
                d_ref[...] = -ADAM_LR * (m_hat / (jnp.sqrt(v_hat) + ADAM_EPS) + ADAM_WD * w_ref[...])
                mo_ref[...] = m_new
                vo_ref[...] = v_new

    def p_spec(l):
        return pl.BlockSpec((P, tm, C), lambda li, t: (0, jnp.where(li == l, t, jnp.where(li < l, 0, tiles - 1)), 0))

    row = pl.BlockSpec((tm, C), lambda li, t: (li * tiles + t, 0))
    return pl.pallas_call(
        body,
        name=name,
        grid=(L, tiles),
        out_shape=(jax.ShapeDtypeStruct((L * R, C), F32),) * 4,
        in_specs=[p_spec(l) for l in range(L)] + [row, row, row],
        out_specs=(row, row, row, row),
        compiler_params=_params("arbitrary", "arbitrary"),
    )(*parts, w, m, v)


BIG = ("w_qkv", "w_o", "w_pw1", "w_pw2", "w_up", "w_down")
SMALL_SHARDED = (("b_pw1", 1), ("w_dw", 2), ("b_dw", 1), ("conv_ln_g", 1), ("conv_ln_b", 1), ("b_pw2", 1))
SMALL_REPL = ("b_mod", "norm_mix", "norm_mlp", "b_qkv", "b_o", "sinks", "final_norm")
WEIGHTS = ("w_mod", "b_mod", "norm_mix", "norm_mlp", "w_qkv", "b_qkv", "w_o", "b_o", "sinks", "w_pw1", "b_pw1",
           "w_dw", "b_dw", "conv_ln_g", "conv_ln_b", "w_pw2", "b_pw2", "w_up", "w_down", "final_norm")


def _step(x, c, loss_target, W, M, V):
    B, S, D = x.shape
    T = B * S
    L = W["w_mod"].shape[0]
    n_mod = W["w_mod"].shape[2]
    me = 4 * lax.axis_index("x") + 2 * lax.axis_index("y") + lax.axis_index("c")
    q_dim = N_HEADS * HEAD_DIM
    kv_dim = N_KV_HEADS * HEAD_DIM

    Wb = {n: W[n].astype(BF16) for n in BIG}

    def shards(i):
        jm = i // 2
        first, last = ("w_qkv", "w_o") if i % 2 == 0 else ("w_pw1", "w_pw2")
        return [Wb[first][jm], Wb[last][jm], Wb["w_up"][i], Wb["w_down"][i]]

    def carried(res, payload):
        return res if payload else (res, [])

    small_names = [n for n, _ in SMALL_SHARDED]
    small_src, small_sizes = _pack([c] + [W[n] for n in small_names], F32, 0)
    got = _exchange("gather_first", [(small_src, True), (shards(0)[0], True)])
    small_parts = _unpack(got[0], small_sizes, [c.shape] + [W[n].shape for n in small_names], 1)
    c_all = small_parts[0].reshape(N_DEV * B, D)
    full = {n: _from_slots(p, d) for (n, d), p in zip(SMALL_SHARDED, small_parts[1:])}
    gathered = [[got[1], None, None, None]] + [None] * (L - 1)

    b_mod_mine = lax.dynamic_slice_in_dim(W["b_mod"], me * n_mod, n_mod, axis=1).reshape(L, 1, n_mod)
    mod_part = _mod_fwd(c_all, W["w_mod"].astype(BF16), b_mod_mine)
    mod_slots = mod_part.reshape(L, N_DEV, B, n_mod).transpose(1, 0, 2, 3).reshape(N_DEV, L * B, n_mod)
    mod_recv = _exchange("scatter_mod", [(mod_slots, False)])[0]
    mod = mod_recv.reshape(N_DEV, L, B, n_mod).transpose(1, 2, 0, 3).reshape(L, B, N_MOD, 1, D)

    w_dw3 = [_conv_weight_chunks(full["w_dw"][j], D) for j in range(full["w_dw"].shape[0])]

    xc = x.reshape(T, D)
    saved = []
    h1 = _normmod_fwd("normmod_mix_fwd", xc, W["norm_mix"][0][None], mod[0, :, 1], mod[0, :, 0], S)
    for i in range(L):
        jm = i // 2
        sh1, sc1, g1, sh2, sc2, g2 = (mod[i, :, t] for t in range(N_MOD))
        nxt = [[(s, True)] for s in shards(i + 1)] if i + 1 < L else [[]] * 4
        g_first, g_last, g_up, g_down = gathered[i]
        sv = {"x_in": xc, "h1": h1}
        if i % 2 == 0:
            w_qkv_full = _from_slots(g_first, 1)
            sv["w_first"] = w_qkv_full
            mine = [(s, True) for s in shards(0)[1:]] if i == 0 else []
            pay = nxt[0] + mine[:1]
            qkv, n_first = carried(_mm("qkv_fwd", h1, w_qkv_full, bias=W["b_qkv"][jm], payload=pay), pay)
            tab = _pair_tables(_attn_table(W["sinks"][jm]))
            mix_in, n_attn = carried(_attn_fwd(qkv, tab, B, S, payload=mine[1:]), mine[1:])
            if i == 0:
                g_last, g_up, g_down = n_first[-1], n_attn[0], n_attn[1]
            sv.update(qkv=qkv, tab=tab)
            b_out = W["b_o"][jm]
        else:
            sv["w_first"] = g_first
            u, n_first = carried(_mm("pw1_fwd", h1, g_first, w_form="nslots", bias=full["b_pw1"][jm], out_dtype=F32,
                                     payload=nxt[0]), nxt[0])
            cv, mix_in = _conv_fwd(u, w_dw3[jm], full["b_dw"][jm][None], full["conv_ln_g"][jm][None],
                                   full["conv_ln_b"][jm][None], S)
            sv.update(u=u, cv=cv)
            b_out = full["b_pw2"][jm]
        w_last = g_last.reshape(-1, D)
        w_down_full = g_down.reshape(-1, D)
        sv.update(mix_in=mix_in, w_last=w_last, g_up=g_up, w_down=w_down_full)
        (x1, y1, h2), n_last = carried(_mm("mix_out_fwd", mix_in, w_last, bias=b_out, epi="resid", x=xc, gate=g1,
                                           norm=(W["norm_mlp"][i][None], sc2, sh2), rows_per_batch=S,
                                           payload=nxt[1]), nxt[1])
        sv.update(y1=y1, x1=x1)
        up, n_up = carried(_mm("mlp_up_fwd", h2, g_up, w_form="nslots", epi="relu", payload=nxt[2]), nxt[2])
        norm_next = (W["norm_mix"][i + 1][None], mod[i + 1, :, 1], mod[i + 1, :, 0]) if i + 1 < L else None
        res, n_down = carried(_mm("mlp_down_fwd", up, w_down_full, a_sq=True, epi="resid", x=x1, gate=g2,
                                  norm=norm_next, rows_per_batch=S, payload=nxt[3]), nxt[3])
        x2, y2 = res[0], res[1]
        h1 = res[2] if i + 1 < L else None
        sv.update(h2=h2, up=up, y2=y2)
        saved.append(sv)
        xc = x2
        if i + 1 < L:
            gathered[i + 1] = [n_first[0], n_last[0], n_up[0], n_down[0]]

    dx, loss_blk, dgfin, dy2, dg2 = _loss_head(xc, loss_target.reshape(T, D), W["final_norm"][None],
                                               saved[L - 1]["y2"], mod[L - 1, :, 5], S)
    loss = lax.psum(loss_blk[0, 0], ("x", "y", "c"))

    G = {"final_norm": dgfin.reshape(D)}
    dmod_layers = [None] * L
    small_grads = ("norm_mix", "norm_mlp", "b_qkv", "b_o", "sinks", "b_pw1", "w_dw", "b_dw", "conv_ln_g", "conv_ln_b",
                   "b_pw2")
    acc = {n: [None] * W[n].shape[0] for n in small_grads}
    reduced = {n: [None] * W[n].shape[0] for n in BIG}

    waiting_up = None
    for i in reversed(range(L)):
        jm = i // 2
        sv = saved[i]
        sh1, sc1, g1, sh2, sc2, g2 = (mod[i, :, t] for t in range(N_MOD))
        pay = [(waiting_up[1], False)] if waiting_up else []
        (gd0, gd1), r = carried(_mm_tn("w_down_grad", sv["up"], dy2, a_sq=True, out_form="kslots", halves=True,
                                       tt_pref=4096, payload=pay), pay)
        if waiting_up:
            reduced["w_up"][waiting_up[0]].append(r[0])
        du, r0 = _mm("mlp_down_bwd", dy2, sv["w_down"], w_form="full_t", epi="relu2d", u=sv["up"],
                     payload=[(gd0, False)])
        (gu0, gu1), r1 = _mm_tn("w_up_grad", sv["h2"], du, out_form="nslots", halves=True, tt_pref=4096,
                                payload=[(gd1, False)])
        reduced["w_down"][i] = [r0[0], r1[0]]
        pay = [(gu0, False)] + ([] if i > 0 else [(gu1, False)])
        (dx, dsh2, dsc2, dgn, dy1, dg1, dy1_sum), r = _mm(
            "mlp_up_bwd", du, sv["g_up"], w_form="kslots_t", epi="normbwd", rows_per_batch=S,
            nb=dict(x=sv["x1"], dx_in=dx, gnorm=W["norm_mlp"][i][None], sc=sc2, gate=(sv["y1"], g1)),
            payload=pay)
        reduced["w_up"][i] = list(r)
        waiting_up = (i, gu1) if i > 0 else None
        acc["norm_mlp"][i] = dgn.reshape(D)
        gw_last = _mm_tn("mix_out_grad", sv["mix_in"], dy1)
        gw_last = gw_last.reshape((N_DEV, -1) + gw_last.shape[1:]).astype(BF16)
        below = (saved[i - 1]["y2"], mod[i - 1, :, 5]) if i > 0 else None
        nb_mix = dict(x=sv["x_in"], dx_in=dx, gnorm=W["norm_mix"][i][None], sc=sc1, gate=below)
        if i % 2 == 0:
            acc["b_o"][jm] = dy1_sum.reshape(D)
            do, r = _mm("attn_out_bwd", dy1, sv["w_last"], w_form="full_t", payload=[(gw_last, False)])
            reduced["w_o"][jm] = r[0]
            dq, dkv, dsk = _attn_bwd(sv["qkv"], sv["mix_in"], do, sv["tab"], B, S)
            acc["sinks"][jm] = jnp.sum(dsk[..., 0], axis=0).transpose(0, 2, 1).reshape(N_HEADS)
            dqkv = jnp.concatenate([dq, dkv], axis=1)
            gw_qkv, db_qkv = _mm_tn("w_qkv_grad", sv["h1"], dqkv, colsum=True)
            acc["b_qkv"][jm] = db_qkv.reshape(-1)
            gw_qkv = _to_slots(gw_qkv, 1).astype(BF16)
            res, r = _mm("qkv_bwd", dqkv, sv["w_first"], w_form="full_t", epi="normbwd", nb=nb_mix, rows_per_batch=S,
                         payload=[(gw_qkv, False)])
            reduced["w_qkv"][jm] = r[0]
        else:
            acc["b_pw2"][jm] = dy1_sum.reshape(D)
            ds, r = _mm("pw2_bwd", dy1, sv["w_last"], w_form="full_t", out_dtype=F32, payload=[(gw_last, False)])
            reduced["w_pw2"][jm] = r[0]
            dcv, dlg, dlb, dbdw = _lnsilu_bwd(ds, sv["cv"], full["conv_ln_g"][jm][None], full["conv_ln_b"][jm][None])
            acc["conv_ln_g"][jm], acc["conv_ln_b"][jm], acc["b_dw"][jm] = dlg.reshape(-1), dlb.reshape(-1), dbdw.reshape(-1)
            du1, dwdw = _conv_bwd(dcv, sv["u"], w_dw3[jm], S)
            acc["w_dw"][jm] = dwdw.transpose(1, 0, 2).reshape(CONV_HALO, D)[:CONV_WIDTH]
            gw_pw1, db_pw1 = _mm_tn("w_pw1_grad", sv["h1"], du1, out_form="nslots", colsum=True, tt_pref=4096)
            acc["b_pw1"][jm] = db_pw1.reshape(-1)
            res, r = _mm("pw1_bwd", du1, sv["w_first"], w_form="kslots_t", epi="normbwd", nb=nb_mix, rows_per_batch=S,
                         payload=[(gw_pw1, False)])
            reduced["w_pw1"][jm] = r[0]
        dx, dsh1, dsc1, dgn = res[:4]
        acc["norm_mix"][i] = dgn.reshape(D)
        dmod_layers[i] = jnp.concatenate([dsh1, dsc1, dg1, dsh2, dsc2, dg2], axis=1).reshape(B, N_MOD * D)
        if i > 0:
            dy2, dg2 = res[4], res[5]
    grad_x = dx.reshape(B, S, D)
    for n, parts in acc.items():
        G[n] = jnp.stack(parts)

    dmod = jnp.stack(dmod_layers)
    dmod_slots = dmod.reshape(L, B, N_DEV, n_mod).transpose(2, 0, 1, 3).reshape(N_DEV, L * B, n_mod)
    dmod_recv = _exchange("gather_dmod", [(dmod_slots, False)])[0]
    dmod_all = dmod_recv.reshape(N_DEV, L, B, n_mod).transpose(1, 0, 2, 3).reshape(L, N_DEV * B, n_mod)
    g_w_mod, db_mod_mine = _mod_bwd(c_all, dmod_all)
    G["b_mod"] = lax.dynamic_update_slice_in_dim(jnp.zeros_like(W["b_mod"]), db_mod_mine.reshape(L, n_mod),
                                                 me * n_mod, axis=1)

    small_items = [jnp.broadcast_to(G[n][None], (N_DEV,) + G[n].shape) for n in SMALL_REPL]
    small_items += [_to_slots(G[n], d) for n, d in SMALL_SHARDED]
    small_slots, small_sizes2 = _pack(small_items, F32, 1)
    small_recv = _exchange("reduce_small", [(small_slots, False)])[0]

    out = {}

    def run_adam(name, parts, names):
        shapes = [W[n].shape for n in names]
        wp, sizes = _pack([W[n] for n in names], F32, 0)
        mp, _ = _pack([M[n] for n in names], F32, 0)
        vp, _ = _pack([V[n] for n in names], F32, 0)
        res = _adam(name, [parts], wp, mp, vp)
        for kind, buf in zip(("grad", "delta", "new_m", "new_v"), res):
            for n, a in zip(names, _unpack(buf, sizes, shapes, 0)):
                out[kind + "_" + n] = a

    for n in BIG + ("w_mod",):
        cols = W[n].shape[-1]
        if n == "w_mod":
            parts = [g_w_mod.reshape(1, -1, cols)]
        else:
            pieces = [p for r in reduced[n] for p in (r if isinstance(r, list) else [r])]
            parts = [p.reshape(N_DEV, -1, cols) for p in pieces]
        res = _adam("adam_" + n, parts, W[n].reshape(-1, cols), M[n].reshape(-1, cols), V[n].reshape(-1, cols))
        for kind, buf in zip(("grad", "delta", "new_m", "new_v"), res):
            out[kind + "_" + n] = buf.reshape(W[n].shape)
    run_adam("adam_small", small_recv, list(SMALL_REPL) + [n for n, _ in SMALL_SHARDED])

    res = [loss, grad_x]
    for kind in ("grad", "delta", "new_m", "new_v"):
        res += [out[kind + "_" + n] for n in WEIGHTS]
    return tuple(res)


def kernel(x, c, w_mod, b_mod, norm_mix, norm_mlp, w_qkv, b_qkv, w_o, b_o, sinks, w_pw1, b_pw1, w_dw, b_dw, conv_ln_g, conv_ln_b, w_pw2, b_pw2, w_up, w_down, final_norm, loss_target, m_w_mod, m_b_mod, m_norm_mix, m_norm_mlp, m_w_qkv, m_b_qkv, m_w_o, m_b_o, m_sinks, m_w_pw1, m_b_pw1, m_w_dw, m_b_dw, m_conv_ln_g, m_conv_ln_b, m_w_pw2, m_b_pw2, m_w_up, m_w_down, m_final_norm, v_w_mod, v_b_mod, v_norm_mix, v_norm_mlp, v_w_qkv, v_b_qkv, v_w_o, v_b_o, v_sinks, v_w_pw1, v_b_pw1, v_w_dw, v_b_dw, v_conv_ln_g, v_conv_ln_b, v_w_pw2, v_b_pw2, v_w_up, v_w_down, v_final_norm):
    W = dict(w_mod=w_mod, b_mod=b_mod, norm_mix=norm_mix, norm_mlp=norm_mlp, w_qkv=w_qkv, b_qkv=b_qkv, w_o=w_o,
             b_o=b_o, sinks=sinks, w_pw1=w_pw1, b_pw1=b_pw1, w_dw=w_dw, b_dw=b_dw, conv_ln_g=conv_ln_g,
             conv_ln_b=conv_ln_b, w_pw2=w_pw2, b_pw2=b_pw2, w_up=w_up, w_down=w_down, final_norm=final_norm)
    M = dict(w_mod=m_w_mod, b_mod=m_b_mod, norm_mix=m_norm_mix, norm_mlp=m_norm_mlp, w_qkv=m_w_qkv, b_qkv=m_b_qkv,
             w_o=m_w_o, b_o=m_b_o, sinks=m_sinks, w_pw1=m_w_pw1, b_pw1=m_b_pw1, w_dw=m_w_dw, b_dw=m_b_dw,
             conv_ln_g=m_conv_ln_g, conv_ln_b=m_conv_ln_b, w_pw2=m_w_pw2, b_pw2=m_b_pw2, w_up=m_w_up,
             w_down=m_w_down, final_norm=m_final_norm)
    V = dict(w_mod=v_w_mod, b_mod=v_b_mod, norm_mix=v_norm_mix, norm_mlp=v_norm_mlp, w_qkv=v_w_qkv, b_qkv=v_b_qkv,
             w_o=v_w_o, b_o=v_b_o, sinks=v_sinks, w_pw1=v_w_pw1, b_pw1=v_b_pw1, w_dw=v_w_dw, b_dw=v_b_dw,
             conv_ln_g=v_conv_ln_g, conv_ln_b=v_conv_ln_b, w_pw2=v_w_pw2, b_pw2=v_b_pw2, w_up=v_w_up,
             w_down=v_w_down, final_norm=v_final_norm)
    return _step(x, c, loss_target, W, M, V)
```

```python
import numpy as np
import jax
import jax.numpy as jnp
from jax import lax
from jax.experimental import pallas as pl
from jax.experimental.pallas import tpu as pltpu

F32 = jnp.float32
BF16 = jnp.bfloat16

N_DEV = 8
N_HEADS = 16
N_KV_HEADS = 2
HEAD_DIM = 64
GROUP = N_HEADS // N_KV_HEADS
ATT_BLOCK = 128
CONV_WIDTH = 31
CONV_HALO = 32
N_MOD = 6
EPS = 1e-6
ADAM_LR = 0.001
ADAM_B1 = 0.9
ADAM_B2 = 0.999
ADAM_EPS = 1e-08
ADAM_WD = 0.01
ADAM_STEP = 10
NEG_BIG = -1e30
PACK_COLS = 1024
VMEM_LIMIT_BYTES = 56 * 1024 * 1024
MESH_ID = pl.DeviceIdType.MESH


def _params(*sem):
    return pltpu.CompilerParams(dimension_semantics=sem, vmem_limit_bytes=VMEM_LIMIT_BYTES)


def _pick(n, pref, mult=8):
    if n <= pref:
        return n
    for t in range(pref, 0, -1):
        if n % t == 0 and t % mult == 0:
            return t
    return n


def _sigmoid(z):
    return 0.5 * jnp.tanh(0.5 * z) + 0.5


def _payload_layout(payload):
    n = len(payload)
    out_shapes = [jax.ShapeDtypeStruct((N_DEV,) + tuple(a.shape if bc else a.shape[1:]), a.dtype) for a, bc in payload]
    hbm = pl.BlockSpec(memory_space=pl.ANY)
    scratch = [pltpu.SemaphoreType.DMA((n * (N_DEV - 1),)), pltpu.SemaphoreType.DMA((n * (N_DEV - 1),)),
               pltpu.SemaphoreType.DMA((n,))]
    return [a for a, _ in payload], [hbm] * n, out_shapes, [hbm] * n, scratch


class _Plan:
    def __init__(self, bcasts, src_refs, dst_refs, send_sems, recv_sems, local_sems):
        x, y, c = lax.axis_index("x"), lax.axis_index("y"), lax.axis_index("c")
        me = 4 * x + 2 * y + c
        self.first, self.landed, self.relay, self.local = [], [], [], []
        for t, (bc, s_ref, d_ref) in enumerate(zip(bcasts, src_refs, dst_refs)):
            def remote(k, src, slot, to):
                sem = t * (N_DEV - 1) + k
                return pltpu.make_async_remote_copy(src_ref=src, dst_ref=d_ref.at[slot], send_sem=send_sems.at[sem],
                                                    recv_sem=recv_sems.at[sem], device_id=to, device_id_type=MESH_ID)
            if bc:
                chips = [(1 - x, y), (x, 1 - y), (1 - x, 1 - y)]
                self.first.append(remote(0, s_ref, me, (x, y, 1 - c)))
                for j, (px, py) in enumerate(chips):
                    cp = remote(1 + j, s_ref, me, (px, py, c))
                    self.first.append(cp)
                    self.landed.append(cp)
                    theirs = 4 * px + 2 * py + c
                    self.relay.append(remote(4 + j, d_ref.at[theirs], theirs, (x, y, 1 - c)))
                self.local.append(pltpu.make_async_copy(s_ref, d_ref.at[me], local_sems.at[t]))
            else:
                for k in range(1, N_DEV):
                    px = 1 - x if (k >> 2) & 1 else x
                    py = 1 - y if (k >> 1) & 1 else y
                    pc = 1 - c if k & 1 else c
                    self.first.append(remote(k - 1, s_ref.at[4 * px + 2 * py + pc], me, (px, py, pc)))
                self.local.append(pltpu.make_async_copy(s_ref.at[me], d_ref.at[me], local_sems.at[t]))

    def start(self):
        for cp in self.first + self.local:
            cp.start()

    def pass_on(self):
        for cp in self.landed:
            cp.wait_recv()
        for cp in self.relay:
            cp.start()

    def finish(self):
        for cp in self.first:
            cp.wait_send()
            if not any(cp is l for l in self.landed):
                cp.wait_recv()
        for cp in self.relay:
            cp.wait()
        for cp in self.local:
            cp.wait()


def _exchange(name, payload):
    n = len(payload)
    bcasts = [bc for _, bc in payload]
    arrays, in_specs, out_shapes, out_specs, scratch = _payload_layout(payload)

    def body(*refs):
        plan = _Plan(bcasts, refs[:n], refs[n:2 * n], *refs[2 * n:])
        plan.start()
        plan.pass_on()
        plan.finish()

    return pl.pallas_call(
        body,
        name=name,
        out_shape=tuple(out_shapes),
        in_specs=in_specs,
        out_specs=tuple(out_specs),
        scratch_shapes=scratch,
    )(*arrays)


def _pack(arrays, dtype, lead):
    lead_shape = arrays[0].shape[:lead]
    flat = [a.astype(dtype).reshape(lead_shape + (-1,)) for a in arrays]
    sizes = [f.shape[-1] for f in flat]
    total = sum(sizes)
    chunk = 16 * PACK_COLS
    padded = -(-total // chunk) * chunk
    if padded > total:
        flat.append(jnp.zeros(lead_shape + (padded - total,), dtype))
    buf = jnp.concatenate(flat, axis=-1)
    return buf.reshape(lead_shape + (padded // PACK_COLS, PACK_COLS)), sizes


def _unpack(buf, sizes, shapes, lead):
    lead_shape = buf.shape[:lead]
    flat = buf.reshape(lead_shape + (-1,))
    out, off = [], 0
    for n, shp in zip(sizes, shapes):
        out.append(lax.slice_in_dim(flat, off, off + n, axis=lead).reshape(lead_shape + tuple(shp)))
        off += n
    return out


def _to_slots(a, dim):
    shp = a.shape
    a = a.reshape(shp[:dim] + (N_DEV, shp[dim] // N_DEV) + shp[dim + 1:])
    return jnp.moveaxis(a, dim, 0)


def _from_slots(a, dim):
    a = jnp.moveaxis(a, 0, dim)
    shp = a.shape
    return a.reshape(shp[:dim] + (shp[dim] * shp[dim + 1],) + shp[dim + 2:])


def _mod_fwd(c_all, w, b):
    L, D, n = w.shape
    M = c_all.shape[0]

    def body(c_ref, w_ref, b_ref, o_ref):
        cv = c_ref[...]
        cs = (cv * _sigmoid(cv)).astype(BF16)
        o_ref[...] = jnp.dot(cs, w_ref[...], preferred_element_type=F32) + b_ref[...]

    return pl.pallas_call(
        body,
        name="mod_fwd",
        grid=(L,),
        out_shape=jax.ShapeDtypeStruct((L, M, n), F32),
        in_specs=[
            pl.BlockSpec((M, D), lambda l: (0, 0)),
            pl.BlockSpec((None, D, n), lambda l: (l, 0, 0)),
            pl.BlockSpec((None, 1, n), lambda l: (l, 0, 0)),
        ],
        out_specs=pl.BlockSpec((None, M, n), lambda l: (l, 0, 0)),
        compiler_params=_params("arbitrary"),
    )(c_all, w, b)


def _mod_bwd(c_all, dmod_all):
    L, M, n = dmod_all.shape
    D = c_all.shape[1]

    def body(c_ref, d_ref, dw_ref, db_ref):
        cv = c_ref[...]
        cs = (cv * _sigmoid(cv)).astype(BF16)
        d = d_ref[...]
        dw_ref[...] = lax.dot_general(cs, d.astype(BF16), (((0,), (0,)), ((), ())), preferred_element_type=F32)
        db_ref[...] = jnp.sum(d, axis=0, keepdims=True)

    return pl.pallas_call(
        body,
        name="mod_bwd",
        grid=(L,),
        out_shape=(jax.ShapeDtypeStruct((L, D, n), F32), jax.ShapeDtypeStruct((L, 1, n), F32)),
        in_specs=[
            pl.BlockSpec((M, D), lambda l: (0, 0)),
            pl.BlockSpec((None, M, n), lambda l: (l, 0, 0)),
        ],
        out_specs=(
            pl.BlockSpec((None, D, n), lambda l: (l, 0, 0)),
            pl.BlockSpec((None, 1, n), lambda l: (l, 0, 0)),
        ),
        compiler_params=_params("arbitrary"),
    )(c_all, dmod_all)


def _mm(name, a, w, *, w_form="full", out_dtype=BF16, bias=None, a_sq=False, epi=None, x=None, gate=None, u=None,
        norm=None, nb=None, ln=None, rows_per_batch=None, tm_pref=512, payload=()):
    M, K = a.shape
    if w_form == "full":
        N = w.shape[1]
        nc = _pick(N, 1024, 128)
    elif w_form == "full_t":
        N = w.shape[0]
        nc = _pick(N, 1024, 128)
    elif w_form == "nslots":
        nc = w.shape[2]
        N = N_DEV * nc
    else:
        N = w.shape[1]
        nc = N
    ks = K // N_DEV
    n_chunks = N // nc
    tm = _pick(M if rows_per_batch is None else rows_per_batch, tm_pref, 16)
    steps = M // tm
    relay_step = (3 * steps) // 4
    nb_gate = nb is not None and nb.get("gate") is not None
    tpb_nb = rows_per_batch // tm if epi == "normbwd" else 1
    has_bias = bias is not None
    n_pay = len(payload)
    bcasts = [bc for _, bc in payload]
    nt = (((1,), (1,)), ((), ()))


    def body(*refs):
        it = iter(refs)
        a_ref = next(it)
        w_ref = next(it)
        b_ref = next(it) if has_bias else None
        x_ref = next(it) if epi == "resid" else None
        g_ref = next(it) if epi == "resid" else None
        gn_ref, sc_ref, sh_ref = (next(it), next(it), next(it)) if norm is not None else (None, None, None)
        u_ref = next(it) if epi == "relu2d" else None
        if epi == "normbwd":
            nx_ref, ndx_ref, ngn_ref, nsc_ref = next(it), next(it), next(it), next(it)
            ny_ref, ngt_ref = (next(it), next(it)) if nb_gate else (None, None)
        if epi == "lnbwd":
            cv_ref, lg_ref, lb_ref = next(it), next(it), next(it)
        pay_src = [next(it) for _ in range(n_pay)]
        o_ref = next(it)
        y_ref = next(it) if epi == "resid" else None
        h_ref = next(it) if norm is not None else None
        if epi == "normbwd":
            dsh_ref, dsc_ref, dgn_ref = next(it), next(it), next(it)
            dy_ref, dgate_ref, cs_ref = (next(it), next(it), next(it)) if nb_gate else (None, None, None)
        if epi == "lnbwd":
            dlg_ref, dlb_ref, dcs_ref = next(it), next(it), next(it)
        pay_dst = [next(it) for _ in range(n_pay)]
        sems = list(it)

        if epi == "lnbwd":
            @pl.when(pl.program_id(0) == 0)
            def _():
                dlg_ref[...] = jnp.zeros_like(dlg_ref)
                dlb_ref[...] = jnp.zeros_like(dlb_ref)
                dcs_ref[...] = jnp.zeros_like(dcs_ref)

        if epi == "normbwd":
            @pl.when(pl.program_id(0) % tpb_nb == 0)
            def _():
                dsh_ref[...] = jnp.zeros_like(dsh_ref)
                dsc_ref[...] = jnp.zeros_like(dsc_ref)
                if nb_gate:
                    dgate_ref[...] = jnp.zeros_like(dgate_ref)

            @pl.when(pl.program_id(0) == 0)
            def _():
                dgn_ref[...] = jnp.zeros_like(dgn_ref)
                if nb_gate:
                    cs_ref[...] = jnp.zeros_like(cs_ref)

        if n_pay:
            @pl.when(pl.program_id(0) == 0)
            def _():
                _Plan(bcasts, pay_src, pay_dst, *sems).start()

            @pl.when(pl.program_id(0) == relay_step)
            def _():
                _Plan(bcasts, pay_src, pay_dst, *sems).pass_on()

        av = None if w_form == "kslots_t" else a_ref[...]
        if a_sq:
            av = av * av
        for ci in range(n_chunks):
            cols = slice(ci * nc, (ci + 1) * nc)
            if w_form == "full":
                acc = jnp.dot(av, w_ref[:, cols], preferred_element_type=F32)
            elif w_form == "full_t":
                acc = lax.dot_general(av, w_ref[cols, :], nt, preferred_element_type=F32)
            elif w_form == "nslots":
                acc = jnp.dot(av, w_ref[ci], preferred_element_type=F32)
            else:
                acc = lax.dot_general(a_ref[:, 0:ks], w_ref[0], nt, preferred_element_type=F32)
                for j in range(1, N_DEV):
                    acc = acc + lax.dot_general(a_ref[:, j * ks:(j + 1) * ks], w_ref[j], nt,
                                                preferred_element_type=F32)
            if has_bias:
                acc = acc + b_ref[:, cols]
            if epi == "resid":
                xn = x_ref[:, cols] + g_ref[:, cols] * acc
                o_ref[:, cols] = xn
                y_ref[:, cols] = acc.astype(BF16)
                if norm is not None:
                    r = lax.rsqrt(jnp.mean(xn * xn, axis=-1, keepdims=True) + EPS)
                    h_ref[...] = ((xn * r * gn_ref[...]) * (1.0 + sc_ref[...]) + sh_ref[...]).astype(BF16)
            elif epi == "relu":
                o_ref[:, cols] = jnp.maximum(acc, 0.0).astype(out_dtype)
            elif epi == "relu2d":
                o_ref[:, cols] = (acc * (2.0 * u_ref[:, cols].astype(F32))).astype(out_dtype)
            elif epi == "normbwd":
                xv = nx_ref[...]
                gn = ngn_ref[...]
                r = lax.rsqrt(jnp.mean(xv * xv, axis=-1, keepdims=True) + EPS)
                xhat = xv * r
                dsh_ref[...] += jnp.sum(acc, axis=0, keepdims=True)
                dsc_ref[...] += jnp.sum(acc * (xhat * gn), axis=0, keepdims=True)
                dn = acc * (1.0 + nsc_ref[...])
                dgn_ref[...] += jnp.sum(dn * xhat, axis=0, keepdims=True)
                dxhat = dn * gn
                dx = ndx_ref[...] + r * (dxhat - xhat * jnp.mean(dxhat * xhat, axis=-1, keepdims=True))
                o_ref[...] = dx
                if nb_gate:
                    dy = dx * ngt_ref[...]
                    dy_ref[...] = dy.astype(BF16)
                    dgate_ref[...] += jnp.sum(dx * ny_ref[...].astype(F32), axis=0, keepdims=True)
                    cs_ref[...] += jnp.sum(dy, axis=0, keepdims=True)
            elif epi == "lnbwd":
                cv = cv_ref[...]
                lg = lg_ref[...]
                xc = cv - jnp.mean(cv, axis=-1, keepdims=True)
                rstd = lax.rsqrt(jnp.mean(xc * xc, axis=-1, keepdims=True) + EPS)
                xhat = xc * rstd
                ln = xhat * lg + lb_ref[...]
                sg = _sigmoid(ln)
                dln = acc * (sg * (1.0 + ln * (1.0 - sg)))
                dlg_ref[...] += jnp.sum(dln * xhat, axis=0, keepdims=True)
                dlb_ref[...] += jnp.sum(dln, axis=0, keepdims=True)
                dxhat = dln * lg
                dcv = rstd * (dxhat - jnp.mean(dxhat, axis=-1, keepdims=True)
                              - xhat * jnp.mean(dxhat * xhat, axis=-1, keepdims=True))
                o_ref[...] = dcv
                dcs_ref[...] += jnp.sum(dcv, axis=0, keepdims=True)
            else:
                o_ref[:, cols] = acc.astype(out_dtype)

        if n_pay:
            @pl.when(pl.program_id(0) == steps - 1)
            def _():
                _Plan(bcasts, pay_src, pay_dst, *sems).finish()

    args = [a, w]
    w_block = w.shape
    specs = [pl.BlockSpec((tm, K), lambda i: (i, 0)), pl.BlockSpec(w_block, lambda i: (0,) * len(w_block))]
    if has_bias:
        args.append(bias.reshape(1, N).astype(F32))
        specs.append(pl.BlockSpec((1, N), lambda i: (0, 0)))
    row_spec = pl.BlockSpec((tm, N), lambda i: (i, 0))
    if epi == "resid":
        tpb = rows_per_batch // tm
        vec_spec = pl.BlockSpec((None, 1, N), lambda i: (i // tpb, 0, 0))
        args += [x, gate]
        specs += [row_spec, vec_spec]
        out_shape = [jax.ShapeDtypeStruct((M, N), F32), jax.ShapeDtypeStruct((M, N), BF16)]
        out_specs = [row_spec, row_spec]
        if norm is not None:
            assert n_chunks == 1
            args += list(norm)
            specs += [pl.BlockSpec((1, N), lambda i: (0, 0)), vec_spec, vec_spec]
            out_shape.append(jax.ShapeDtypeStruct((M, N), BF16))
            out_specs.append(row_spec)
    elif epi == "normbwd":
        assert n_chunks == 1
        vec_spec = pl.BlockSpec((None, 1, N), lambda i: (i // tpb_nb, 0, 0))
        one_spec = pl.BlockSpec((1, N), lambda i: (0, 0))
        nbat = M // rows_per_batch
        f_vec, f_one = jax.ShapeDtypeStruct((nbat, 1, N), F32), jax.ShapeDtypeStruct((1, N), F32)
        args += [nb["x"], nb["dx_in"], nb["gnorm"], nb["sc"]]
        specs += [row_spec, row_spec, one_spec, vec_spec]
        out_shape = [jax.ShapeDtypeStruct((M, N), F32), f_vec, f_vec, f_one]
        out_specs = [row_spec, vec_spec, vec_spec, one_spec]
        if nb_gate:
            args += list(nb["gate"])
            specs += [row_spec, vec_spec]
            out_shape += [jax.ShapeDtypeStruct((M, N), BF16), f_vec, f_one]
            out_specs += [row_spec, vec_spec, one_spec]
    elif epi == "lnbwd":
        assert n_chunks == 1
        one_spec = pl.BlockSpec((1, N), lambda i: (0, 0))
        f_one = jax.ShapeDtypeStruct((1, N), F32)
        args += list(ln)
        specs += [row_spec, one_spec, one_spec]
        out_shape = [jax.ShapeDtypeStruct((M, N), F32), f_one, f_one, f_one]
        out_specs = [row_spec, one_spec, one_spec, one_spec]
    else:
        if epi == "relu2d":
            args.append(u)
            specs.append(row_spec)
        out_shape = [jax.ShapeDtypeStruct((M, N), out_dtype)]
        out_specs = [row_spec]
    scratch = []
    if n_pay:
        p_arrays, p_in, p_shapes, p_out, scratch = _payload_layout(payload)
        args += p_arrays
        specs += p_in
        out_shape += p_shapes
        out_specs += p_out
    res = pl.pallas_call(
        body,
        name=name,
        grid=(steps,),
        out_shape=tuple(out_shape),
        in_specs=specs,
        out_specs=tuple(out_specs),
        scratch_shapes=scratch,
        compiler_params=_params("arbitrary" if (n_pay or epi in ("normbwd", "lnbwd")) else "parallel"),
    )(*args)
    if epi == "resid":
        n_own = 3 if norm is not None else 2
    elif epi == "normbwd":
        n_own = 7 if nb_gate else 4
    elif epi == "lnbwd":
        n_own = 4
    else:
        n_own = 1
    own = res[0] if n_own == 1 else tuple(res[:n_own])
    return (own, list(res[n_own:])) if n_pay else own


def _mm_tn(name, a, b, *, a_sq=False, out_form="full", halves=False, colsum=False, tt_pref=1024, tk_pref=1024,
           tn_pref=1024, payload=()):
    T, K = a.shape
    N = b.shape[1]
    tt = _pick(T, tt_pref, 16)
    tk = K // N_DEV if out_form == "kslots" else _pick(K, tk_pref, 128)
    tn = N // N_DEV if out_form == "nslots" else _pick(N, tn_pref, 128)
    nt_steps = T // tt
    grid = (K // tk, N // tn, nt_steps)
    slots = out_form != "full"
    assert not colsum or K == tk
    n_pay = len(payload)
    bcasts = [bc for _, bc in payload]

    def body(*refs):
        it = iter(refs)
        a_ref, b_ref = next(it), next(it)
        pay_src = [next(it) for _ in range(n_pay)]
        o_ref = next(it)
        o2_ref = next(it) if halves else None
        cs_ref = next(it) if colsum else None
        pay_dst = [next(it) for _ in range(n_pay)]
        acc = next(it) if slots else o_ref
        sems = list(it)
        step = (pl.program_id(0) * grid[1] + pl.program_id(1)) * grid[2] + pl.program_id(2)

        if n_pay:
            @pl.when(step == 0)
            def _():
                _Plan(bcasts, pay_src, pay_dst, *sems).start()

        @pl.when(pl.program_id(2) == 0)
        def _():
            acc[...] = jnp.zeros_like(acc)
            if colsum:
                cs_ref[...] = jnp.zeros_like(cs_ref)

        bv = b_ref[...]
        av = a_ref[...]
        if a_sq:
            av = av * av
        acc[...] += lax.dot_general(av, bv, (((0,), (0,)), ((), ())), preferred_element_type=F32)
        if colsum:
            cs_ref[...] += jnp.sum(bv.astype(F32), axis=0, keepdims=True)

        if slots:
            @pl.when(pl.program_id(2) == nt_steps - 1)
            def _():
                if halves:
                    o_ref[...] = acc[0:tk // 2, :].astype(o_ref.dtype)
                    o2_ref[...] = acc[tk // 2:, :].astype(o2_ref.dtype)
                else:
                    o_ref[...] = acc[...].astype(o_ref.dtype)

        if n_pay:
            @pl.when(step == grid[0] * grid[1] * grid[2] - 1)
            def _():
                plan = _Plan(bcasts, pay_src, pay_dst, *sems)
                plan.pass_on()
                plan.finish()

    if out_form == "full":
        out_shape = [jax.ShapeDtypeStruct((K, N), F32)]
        out_specs = [pl.BlockSpec((tk, tn), lambda k, n, t: (k, n))]
    else:
        assert not halves or out_form == "kslots" or K == tk
        rows = tk // 2 if halves else tk
        if out_form == "nslots":
            shape, spec = (N_DEV, K // tk * rows, tn), pl.BlockSpec((None, rows, tn), lambda k, n, t: (n, k, 0))
        else:
            shape, spec = (N_DEV, rows, N), pl.BlockSpec((None, rows, tn), lambda k, n, t: (k, 0, n))
        out_shape = [jax.ShapeDtypeStruct(shape, BF16)] * (2 if halves else 1)
        out_specs = [spec] * (2 if halves else 1)
    if colsum:
        out_shape.append(jax.ShapeDtypeStruct((1, N), F32))
        out_specs.append(pl.BlockSpec((1, tn), lambda k, n, t: (0, n)))
    args = [a, b]
    in_specs = [pl.BlockSpec((tt, tk), lambda k, n, t: (t, k)), pl.BlockSpec((tt, tn), lambda k, n, t: (t, n))]
    scratch = [pltpu.VMEM((tk, tn), F32)] if slots else []
    n_own = len(out_shape)
    if n_pay:
        p_arrays, p_in, p_shapes, p_out, p_scratch = _payload_layout(payload)
        args += p_arrays
        in_specs += p_in
        out_shape += p_shapes
        out_specs += p_out
        scratch += p_scratch
    res = pl.pallas_call(
        body,
        name=name,
        grid=grid,
        out_shape=tuple(out_shape),
        in_specs=in_specs,
        out_specs=tuple(out_specs),
        scratch_shapes=scratch,
        compiler_params=_params(*(("arbitrary",) * 3 if n_pay else ("parallel", "parallel", "arbitrary"))),
    )(*args)
    own = res[0] if n_own == 1 else tuple(res[:n_own])
    return (own, list(res[n_own:])) if n_pay else own


def _normmod_fwd(name, x, gnorm, sc, sh, S, tm_pref=512):
    T, D = x.shape
    tm = _pick(S, tm_pref, 16)
    tpb = S // tm

    def body(x_ref, g_ref, sc_ref, sh_ref, o_ref):
        xv = x_ref[...]
        r = lax.rsqrt(jnp.mean(xv * xv, axis=-1, keepdims=True) + EPS)
        n = xv * r * g_ref[...]
        o_ref[...] = (n * (1.0 + sc_ref[...]) + sh_ref[...]).astype(BF16)

    row = pl.BlockSpec((tm, D), lambda i: (i, 0))
    vec = pl.BlockSpec((None, 1, D), lambda i: (i // tpb, 0, 0))
    return pl.pallas_call(
        body,
        name=name,
        grid=(T // tm,),
        out_shape=jax.ShapeDtypeStruct((T, D), BF16),
        in_specs=[row, pl.BlockSpec((1, D), lambda i: (0, 0)), vec, vec],
        out_specs=row,
        compiler_params=_params("parallel"),
    )(x, gnorm, sc, sh)


ATT_ROWS = GROUP * ATT_BLOCK
ATT_SPAN = 2 * ATT_BLOCK
ATT_SCALE = HEAD_DIM ** -0.5


def _attn_table(sinks):
    slopes = jnp.asarray(np.array([2.0 ** (-8.0 * (h + 1) / N_HEADS) for h in range(N_HEADS)], np.float32))
    r = jnp.arange(ATT_BLOCK)[:, None]
    cc = jnp.arange(ATT_SPAN)[None, :]
    dist = r + ATT_BLOCK - cc
    ok = (dist >= 0) & (dist < ATT_BLOCK)
    tab = jnp.where(ok[None], -slopes[:, None, None] * dist.astype(F32)[None], NEG_BIG)
    tab = jnp.where((cc == 0)[None], sinks.astype(F32)[:, None, None], tab)
    return tab.reshape(N_KV_HEADS, ATT_ROWS, ATT_SPAN)


KV_COLS = N_KV_HEADS * HEAD_DIM


PAIR = 2 * HEAD_DIM


def _pair_tables(tab):
    hkv, rows, span = tab.shape
    pairs = rows // (2 * ATT_BLOCK)
    return tab.reshape(hkv, pairs, 2, ATT_BLOCK, span).transpose(0, 2, 1, 3, 4).reshape(hkv, 2, pairs * ATT_BLOCK, span)


def _stack_pairs(ref, pairs, col0):
    return jnp.concatenate([ref[:, col0 + j * PAIR:col0 + (j + 1) * PAIR] for j in range(pairs)], axis=0)


def _unstack_pairs(v, pairs):
    return jnp.concatenate([v[j * ATT_BLOCK:(j + 1) * ATT_BLOCK, :] for j in range(pairs)], axis=1)


def _swap_halves(x):
    return jnp.concatenate([x[:, HEAD_DIM:], x[:, :HEAD_DIM]], axis=1)


def _load_span(prev_ref, cur_ref, buf_e, buf_o, kv, mult):
    lane = lax.broadcasted_iota(jnp.int32, (ATT_BLOCK, PAIR), 1)
    row = lax.broadcasted_iota(jnp.int32, (ATT_BLOCK, PAIR), 0)
    own = lane // HEAD_DIM == kv
    for r0, ref, first in ((0, prev_ref, True), (ATT_BLOCK, cur_ref, False)):
        v = ref[...]
        if mult != 1.0:
            v = v * mult
        zero = jnp.zeros_like(v)
        mine = jnp.where(own, v, zero)
        if first:
            mine = jnp.where(row > 0, mine, zero)
        other = _swap_halves(mine)
        buf_e[r0:r0 + ATT_BLOCK, :] = mine if kv == 0 else other
        buf_o[r0:r0 + ATT_BLOCK, :] = other if kv == 0 else mine


def _first_block_penalty(i):
    col = lax.broadcasted_iota(jnp.int32, (1, ATT_SPAN), 1)
    return jnp.where((col < ATT_BLOCK) & (col > 0), jnp.where(i > 0, 0.0, NEG_BIG), 0.0).astype(F32)


def _attn_probs(qp, kbuf, bias, first_pen):
    nt = (((1,), (1,)), ((), ()))
    s = lax.dot_general(qp, kbuf, nt, preferred_element_type=F32) + bias + first_pen
    m = jnp.max(s, axis=-1, keepdims=True)
    e = jnp.exp(s - m)
    return e * (1.0 / jnp.sum(e, axis=-1, keepdims=True))


def _attn_specs(nb, q_dim, pairs, row_q, row_cur, row_prev):
    kblk = q_dim // KV_COLS
    qspec = pl.BlockSpec((ATT_BLOCK, q_dim), lambda b, i: (row_q(b, i), 0))
    kv_specs = [pl.BlockSpec((ATT_BLOCK, KV_COLS), (lambda b, i, r=r, c=c: (r(b, i), c)))
                for c in (kblk, kblk + 1) for r in (row_cur, row_prev)]
    tspec = pl.BlockSpec((N_KV_HEADS, 2, pairs * ATT_BLOCK, ATT_SPAN), lambda b, i: (0, 0, 0, 0))
    return qspec, kv_specs, tspec


def _attn_fwd(qkv, tab2, B, S, payload=()):
    assert N_KV_HEADS == 2 and GROUP % 2 == 0
    T = qkv.shape[0]
    pairs = GROUP // 2
    gw = pairs * PAIR
    nb = S // ATT_BLOCK
    q_dim = N_HEADS * HEAD_DIM
    n_pay = len(payload)
    bcasts = [bc for _, bc in payload]
    steps = B * nb

    def body(*refs):
        q_ref, kc_ref, kp_ref, vc_ref, vp_ref, tab_ref = refs[:6]
        pay_src = refs[6:6 + n_pay]
        o_ref = refs[6 + n_pay]
        pay_dst = refs[7 + n_pay:7 + 2 * n_pay]
        ke, ko, ve, vo = refs[7 + 2 * n_pay:11 + 2 * n_pay]
        sems = refs[11 + 2 * n_pay:]
        i = pl.program_id(1)
        if n_pay:
            step = pl.program_id(0) * nb + i

            @pl.when(step == 0)
            def _():
                _Plan(bcasts, pay_src, pay_dst, *sems).start()

            @pl.when(step == (3 * steps) // 4)
            def _():
                _Plan(bcasts, pay_src, pay_dst, *sems).pass_on()

            @pl.when(step == steps - 1)
            def _():
                _Plan(bcasts, pay_src, pay_dst, *sems).finish()

        pen = _first_block_penalty(i)
        for kv in range(N_KV_HEADS):
            _load_span(kp_ref, kc_ref, ke.at[kv], ko.at[kv], kv, ATT_SCALE)
            _load_span(vp_ref, vc_ref, ve.at[kv], vo.at[kv], kv, 1.0)
            qp = _stack_pairs(q_ref, pairs, kv * gw)
            p_e = _attn_probs(qp, ke[kv], tab_ref[kv, 0], pen)
            p_o = _attn_probs(qp, ko[kv], tab_ref[kv, 1], pen)
            o = jnp.dot(p_e.astype(BF16), ve[kv], preferred_element_type=F32)
            o = o + jnp.dot(p_o.astype(BF16), vo[kv], preferred_element_type=F32)
            o_ref[:, kv * gw:(kv + 1) * gw] = _unstack_pairs(o.astype(BF16), pairs)

    qspec, kv_specs, tspec = _attn_specs(nb, q_dim, pairs, lambda b, i: b * nb + i, lambda b, i: b * nb + i,
                                         lambda b, i: b * nb + jnp.maximum(i - 1, 0))
    args, in_specs = [qkv] * 5 + [tab2], [qspec] + kv_specs + [tspec]
    out_shape, out_specs = [jax.ShapeDtypeStruct((T, q_dim), BF16)], [qspec]
    scratch = [pltpu.VMEM((N_KV_HEADS, ATT_SPAN, PAIR), BF16)] * 4
    if n_pay:
        p_arrays, p_in, p_shapes, p_out, p_scratch = _payload_layout(payload)
        args += p_arrays
        in_specs += p_in
        out_shape += p_shapes
        out_specs += p_out
        scratch += p_scratch
    res = pl.pallas_call(
        body,
        name="attn_fwd",
        grid=(B, nb),
        out_shape=tuple(out_shape),
        in_specs=in_specs,
        out_specs=tuple(out_specs),
        scratch_shapes=scratch,
        compiler_params=_params(*(("arbitrary",) * 2 if n_pay else ("parallel", "arbitrary"))),
    )(*args)
    return (res[0], list(res[1:])) if n_pay else res[0]


def _attn_bwd(qkv, o, do, tab2, B, S):
    T = qkv.shape[0]
    hd = HEAD_DIM
    pairs = GROUP // 2
    gw = pairs * PAIR
    rows = pairs * ATT_BLOCK
    nb = S // ATT_BLOCK
    last = nb - 1
    q_dim = N_HEADS * hd
    tn = (((0,), (0,)), ((), ()))
    nt = (((1,), (1,)), ((), ()))

    def body(q_ref, kc_ref, kp_ref, vc_ref, vp_ref, tab_ref, o_ref, do_ref,
             dq_ref, dkv_ref, dsink_ref, carry_k, carry_v, ke, ko, ve, vo):
        i = pl.program_id(1)

        @pl.when(i == 0)
        def _():
            dsink_ref[...] = jnp.zeros_like(dsink_ref)

        def emit(kv, dk_rows, dv_rows):
            dkv_ref[:, kv * hd:(kv + 1) * hd] = dk_rows.astype(BF16)
            dkv_ref[:, KV_COLS + kv * hd:KV_COLS + (kv + 1) * hd] = dv_rows.astype(BF16)

        @pl.when(i < nb)
        def _():
            pen = _first_block_penalty(i)
            low = lax.broadcasted_iota(jnp.int32, (rows, PAIR), 1) < hd
            low_k = lax.broadcasted_iota(jnp.int32, (ATT_SPAN, PAIR), 1) < hd
            keep = lax.broadcasted_iota(jnp.int32, (ATT_SPAN, hd), 0) > 0
            hg = lax.broadcasted_iota(jnp.int32, (pairs, rows), 0)
            hr = lax.broadcasted_iota(jnp.int32, (pairs, rows), 1)
            head_of = jnp.where(hr // ATT_BLOCK == hg, 1.0, 0.0).astype(BF16)
            dks, dvs = [], []
            for kv in range(N_KV_HEADS):
                _load_span(kp_ref, kc_ref, ke.at[kv], ko.at[kv], kv, ATT_SCALE)
                _load_span(vp_ref, vc_ref, ve.at[kv], vo.at[kv], kv, 1.0)
                qp = _stack_pairs(q_ref, pairs, kv * gw)
                dop = _stack_pairs(do_ref, pairs, kv * gw)
                op = _stack_pairs(o_ref, pairs, kv * gw)
                p_e = _attn_probs(qp, ke[kv], tab_ref[kv, 0], pen)
                p_o = _attn_probs(qp, ko[kv], tab_ref[kv, 1], pen)
                prod = dop.astype(F32) * op.astype(F32)
                d_e = jnp.sum(jnp.where(low, prod, 0.0), axis=-1, keepdims=True)
                d_o = jnp.sum(prod, axis=-1, keepdims=True) - d_e
                ds_e = (p_e * (lax.dot_general(dop, ve[kv], nt, preferred_element_type=F32) - d_e)).astype(BF16)
                ds_o = (p_o * (lax.dot_general(dop, vo[kv], nt, preferred_element_type=F32) - d_o)).astype(BF16)
                dq = jnp.dot(ds_e, ke[kv], preferred_element_type=F32) + jnp.dot(ds_o, ko[kv], preferred_element_type=F32)
                dq_ref[:, kv * gw:(kv + 1) * gw] = _unstack_pairs(dq.astype(BF16), pairs)
                dsink_ref[kv, 0] += jnp.dot(head_of, ds_e, preferred_element_type=F32)[:, 0:128]
                dsink_ref[kv, 1] += jnp.dot(head_of, ds_o, preferred_element_type=F32)[:, 0:128]
                dk2 = jnp.where(low_k, lax.dot_general(ds_e, qp, tn, preferred_element_type=F32),
                                lax.dot_general(ds_o, qp, tn, preferred_element_type=F32))
                dv2 = jnp.where(low_k, lax.dot_general(p_e.astype(BF16), dop, tn, preferred_element_type=F32),
                                lax.dot_general(p_o.astype(BF16), dop, tn, preferred_element_type=F32))
                dks.append(jnp.where(keep, (dk2[:, :hd] + dk2[:, hd:]) * ATT_SCALE, 0.0))
                dvs.append(jnp.where(keep, dv2[:, :hd] + dv2[:, hd:], 0.0))

            @pl.when(i > 0)
            def _():
                for kv in range(N_KV_HEADS):
                    emit(kv, carry_k[kv] + dks[kv][0:ATT_BLOCK], carry_v[kv] + dvs[kv][0:ATT_BLOCK])

            for kv in range(N_KV_HEADS):
                carry_k[kv] = dks[kv][ATT_BLOCK:]
                carry_v[kv] = dvs[kv][ATT_BLOCK:]

        @pl.when(i == nb)
        def _():
            for kv in range(N_KV_HEADS):
                emit(kv, carry_k[kv], carry_v[kv])

    qspec, kv_specs, tspec = _attn_specs(nb, q_dim, pairs, lambda b, i: b * nb + jnp.minimum(i, last),
                                         lambda b, i: b * nb + jnp.minimum(i, last),
                                         lambda b, i: b * nb + jnp.clip(i - 1, 0, last))
    dkv = pl.BlockSpec((ATT_BLOCK, 2 * KV_COLS), lambda b, i: (b * nb + jnp.maximum(i - 1, 0), 0))
    dsk = pl.BlockSpec((None, N_KV_HEADS, 2, pairs, 128), lambda b, i: (b, 0, 0, 0, 0))
    return pl.pallas_call(
        body,
        name="attn_bwd",
        grid=(B, nb + 1),
        out_shape=(
            jax.ShapeDtypeStruct((T, q_dim), BF16),
            jax.ShapeDtypeStruct((T, 2 * KV_COLS), BF16),
            jax.ShapeDtypeStruct((B, N_KV_HEADS, 2, pairs, 128), F32),
        ),
        in_specs=[qspec] + kv_specs + [tspec, qspec, qspec],
        out_specs=(qspec, dkv, dsk),
        scratch_shapes=[pltpu.VMEM((N_KV_HEADS, ATT_BLOCK, hd), F32), pltpu.VMEM((N_KV_HEADS, ATT_BLOCK, hd), F32)]
        + [pltpu.VMEM((N_KV_HEADS, ATT_SPAN, PAIR), BF16)] * 4,
        compiler_params=_params("arbitrary", "arbitrary"),
    )(qkv, qkv, qkv, qkv, qkv, tab2, o, do)


def _conv_tiles(S):
    ts = _pick(S, 256, CONV_HALO)
    return ts, S // ts


def _conv_chunks(C, ts):
    lane = _pick(C, 128, 128)
    return lane, C // lane, _pick(ts, 128, 8)


def _conv_weight_chunks(w_dw, C):
    lane = _pick(C, 128, 128)
    w = jnp.pad(w_dw, ((0, CONV_HALO - CONV_WIDTH), (0, 0)))
    return w.reshape(CONV_HALO, C // lane, lane).transpose(1, 0, 2)


def _conv_fwd(u, w3, b_dw, ln_g, ln_b, S):
    T, C2 = u.shape
    C = C2 // 2
    B = T // S
    ts, nj = _conv_tiles(S)
    hb = ts // CONV_HALO
    lane, nc, rc = _conv_chunks(C, ts)

    def body(a_ref, g_ref, ap_ref, gp_ref, w_ref, bdw_ref, lg_ref, lb_ref, cv_ref, s_ref, buf, cvb):
        j = pl.program_id(1)
        glu_prev = ap_ref[...] * _sigmoid(gp_ref[...]) * (j > 0).astype(F32)
        glu = a_ref[...] * _sigmoid(g_ref[...])
        for cc in range(nc):
            buf[cc, 0:CONV_HALO, :] = glu_prev[:, cc * lane:(cc + 1) * lane]
            buf[cc, CONV_HALO:, :] = glu[:, cc * lane:(cc + 1) * lane]

        def chunk(cc, carry):
            for r0 in range(0, ts, rc):
                acc = jnp.zeros((rc, lane), F32)
                for kk in range(CONV_WIDTH):
                    lo = CONV_HALO - (CONV_WIDTH - 1 - kk) + r0
                    acc = acc + w_ref[cc, kk:kk + 1, :] * buf[cc, lo:lo + rc, :]
                cvb[cc, r0:r0 + rc, :] = acc
            return carry

        lax.fori_loop(0, nc, chunk, 0)
        for cc in range(nc):
            cv_ref[:, cc * lane:(cc + 1) * lane] = cvb[cc] + bdw_ref[:, cc * lane:(cc + 1) * lane]
        cv = cv_ref[...]
        mu = jnp.mean(cv, axis=-1, keepdims=True)
        xc = cv - mu
        rstd = lax.rsqrt(jnp.mean(xc * xc, axis=-1, keepdims=True) + EPS)
        ln = xc * rstd * lg_ref[...] + lb_ref[...]
        s_ref[...] = (ln * _sigmoid(ln)).astype(BF16)

    a_cur = pl.BlockSpec((ts, C), lambda b, j: (b * nj + j, 0))
    g_cur = pl.BlockSpec((ts, C), lambda b, j: (b * nj + j, 1))
    a_prev = pl.BlockSpec((CONV_HALO, C), lambda b, j: (jnp.maximum((b * nj + j) * hb - 1, 0), 0))
    g_prev = pl.BlockSpec((CONV_HALO, C), lambda b, j: (jnp.maximum((b * nj + j) * hb - 1, 0), 1))
    wspec = pl.BlockSpec((nc, CONV_HALO, lane), lambda b, j: (0, 0, 0))
    one = pl.BlockSpec((1, C), lambda b, j: (0, 0))
    return pl.pallas_call(
        body,
        name="conv_fwd",
        grid=(B, nj),
        out_shape=(jax.ShapeDtypeStruct((T, C), F32), jax.ShapeDtypeStruct((T, C), BF16)),
        in_specs=[a_cur, g_cur, a_prev, g_prev, wspec, one, one, one],
        out_specs=(a_cur, a_cur),
        scratch_shapes=[pltpu.VMEM((nc, CONV_HALO + ts, lane), F32), pltpu.VMEM((nc, ts, lane), F32)],
        compiler_params=_params("parallel", "arbitrary"),
    )(u, u, u, u, w3, b_dw, ln_g, ln_b)


def _conv_bwd(dcv, u, w3, S):
    T, C2 = u.shape
    C = C2 // 2
    B = T // S
    ts, nj = _conv_tiles(S)
    hb = ts // CONV_HALO
    n_halo_blocks = T // CONV_HALO
    lane, nc, rc = _conv_chunks(C, ts)

    def body(dcv_ref, dnx_ref, a_ref, g_ref, ap_ref, gp_ref, w_ref, du_ref, dw_ref, gbuf, dbuf, dglu, dw8):
        b, j = pl.program_id(0), pl.program_id(1)

        @pl.when((b == 0) & (j == 0))
        def _():
            dw8[...] = jnp.zeros_like(dw8)

        a = a_ref[...]
        sg = _sigmoid(g_ref[...])
        glu_prev = ap_ref[...] * _sigmoid(gp_ref[...]) * (j > 0).astype(F32)
        glu = a * sg
        dcur = dcv_ref[...]
        dnext = dnx_ref[...] * (j < nj - 1).astype(F32)
        for cc in range(nc):
            cols = slice(cc * lane, (cc + 1) * lane)
            gbuf[cc, 0:CONV_HALO, :] = glu_prev[:, cols]
            gbuf[cc, CONV_HALO:, :] = glu[:, cols]
            dbuf[cc, 0:ts, :] = dcur[:, cols]
            dbuf[cc, ts:, :] = dnext[:, cols]

        def chunk(cc, carry):
            for r0 in range(0, ts, rc):
                acc = jnp.zeros((rc, lane), F32)
                for kk in range(CONV_WIDTH):
                    d = CONV_WIDTH - 1 - kk
                    acc = acc + w_ref[cc, kk:kk + 1, :] * dbuf[cc, r0 + d:r0 + d + rc, :]
                dglu[cc, r0:r0 + rc, :] = acc
            for kk in range(CONV_WIDTH):
                d = CONV_WIDTH - 1 - kk
                p = jnp.zeros((rc, lane), F32)
                for r0 in range(0, ts, rc):
                    lo = CONV_HALO - d + r0
                    p = p + dbuf[cc, r0:r0 + rc, :] * gbuf[cc, lo:lo + rc, :]
                dw8[cc, kk * 8:(kk + 1) * 8, :] += jnp.sum(p.reshape(rc // 8, 8, lane), axis=0)
            return carry

        lax.fori_loop(0, nc, chunk, 0)
        for cc in range(nc):
            cols = slice(cc * lane, (cc + 1) * lane)
            dgl = dglu[cc]
            du_ref[:, cc * lane:(cc + 1) * lane] = (dgl * sg[:, cols]).astype(BF16)
            du_ref[:, C + cc * lane:C + (cc + 1) * lane] = (dgl * a[:, cols] * sg[:, cols] * (1.0 - sg[:, cols])).astype(BF16)

        @pl.when((b == B - 1) & (j == nj - 1))
        def _():
            dw_ref[...] = jnp.zeros_like(dw_ref)
            for kk in range(CONV_WIDTH):
                dw_ref[:, kk:kk + 1, :] = jnp.sum(dw8[:, kk * 8:(kk + 1) * 8, :], axis=1, keepdims=True)

    a_cur = pl.BlockSpec((ts, C), lambda b, j: (b * nj + j, 0))
    g_cur = pl.BlockSpec((ts, C), lambda b, j: (b * nj + j, 1))
    a_prev = pl.BlockSpec((CONV_HALO, C), lambda b, j: (jnp.maximum((b * nj + j) * hb - 1, 0), 0))
    g_prev = pl.BlockSpec((CONV_HALO, C), lambda b, j: (jnp.maximum((b * nj + j) * hb - 1, 0), 1))
    d_next = pl.BlockSpec((CONV_HALO, C), lambda b, j: (jnp.minimum((b * nj + j + 1) * hb, n_halo_blocks - 1), 0))
    wspec = pl.BlockSpec((nc, CONV_HALO, lane), lambda b, j: (0, 0, 0))
    return pl.pallas_call(
        body,
        name="conv_bwd",
        grid=(B, nj),
        out_shape=(jax.ShapeDtypeStruct((T, C2), BF16), jax.ShapeDtypeStruct((nc, CONV_HALO, lane), F32)),
        in_specs=[a_cur, d_next, a_cur, g_cur, a_prev, g_prev, wspec],
        out_specs=(pl.BlockSpec((ts, C2), lambda b, j: (b * nj + j, 0)), wspec),
        scratch_shapes=[
            pltpu.VMEM((nc, CONV_HALO + ts, lane), F32),
            pltpu.VMEM((nc, ts + CONV_HALO, lane), F32),
            pltpu.VMEM((nc, ts, lane), F32),
            pltpu.VMEM((nc, CONV_HALO * 8, lane), F32),
        ],
        compiler_params=_params("arbitrary", "arbitrary"),
    )(dcv, dcv, u, u, u, u, w3)


def _loss_head(x, tgt, gfin, y, gate, S, tm_pref=256):
    T, D = x.shape
    tm = _pick(S, tm_pref, 16)
    tpb = S // tm

    def body(x_ref, t_ref, g_ref, y_ref, gt_ref, dx_ref, loss_ref, dg_ref, dy_ref, dgate_ref):
        @pl.when(pl.program_id(0) == 0)
        def _():
            loss_ref[...] = jnp.zeros_like(loss_ref)
            dg_ref[...] = jnp.zeros_like(dg_ref)

        @pl.when(pl.program_id(0) % tpb == 0)
        def _():
            dgate_ref[...] = jnp.zeros_like(dgate_ref)

        xv = x_ref[...]
        g = g_ref[...]
        r = lax.rsqrt(jnp.mean(xv * xv, axis=-1, keepdims=True) + EPS)
        xhat = xv * r
        e = xhat * g - t_ref[...]
        row_loss = jnp.mean(e * e, axis=-1, keepdims=True)
        loss_ref[...] += 0.5 * jnp.sum(row_loss, axis=0, keepdims=True)
        dy = e * (1.0 / D)
        dg_ref[...] += jnp.sum(dy * xhat, axis=0, keepdims=True)
        dxhat = dy * g
        dx = r * (dxhat - xhat * jnp.mean(dxhat * xhat, axis=-1, keepdims=True))
        dx_ref[...] = dx
        dy_ref[...] = (dx * gt_ref[...]).astype(BF16)
        dgate_ref[...] += jnp.sum(dx * y_ref[...].astype(F32), axis=0, keepdims=True)

    row = pl.BlockSpec((tm, D), lambda i: (i, 0))
    vec = pl.BlockSpec((None, 1, D), lambda i: (i // tpb, 0, 0))
    one = pl.BlockSpec((1, D), lambda i: (0, 0))
    return pl.pallas_call(
        body,
        name="loss_head",
        grid=(T // tm,),
        out_shape=(
            jax.ShapeDtypeStruct((T, D), F32),
            jax.ShapeDtypeStruct((8, 128), F32),
            jax.ShapeDtypeStruct((1, D), F32),
            jax.ShapeDtypeStruct((T, D), BF16),
            jax.ShapeDtypeStruct((T // S, 1, D), F32),
        ),
        in_specs=[row, row, one, row, vec],
        out_specs=(row, pl.BlockSpec((8, 128), lambda i: (0, 0)), one, row, vec),
        compiler_params=_params("arbitrary"),
    )(x, tgt, gfin, y, gate)


def _adam(name, parts, w, m, v, tm_pref=256):
    L = len(parts)
    P, R, C = parts[0].shape
    fit = VMEM_LIMIT_BYTES // 2 // (2 * L * P * C * parts[0].dtype.itemsize)
    tm = _pick(R, max(16, min(tm_pref, fit // 16 * 16)), 16)
    tiles = R // tm
    c1 = 1.0 - ADAM_B1 ** ADAM_STEP
    c2 = 1.0 - ADAM_B2 ** ADAM_STEP

    def body(*refs):
        p_refs = refs[:L]
        w_ref, m_ref, v_ref, g_ref, d_ref, mo_ref, vo_ref = refs[L:]
        for l in range(L):
            @pl.when(pl.program_id(0) == l)
            def _():
                g = p_refs[l][0].astype(F32)
                for i in range(1, P):
                    g = g + p_refs[l][i].astype(F32)
                m_new = ADAM_B1 * m_ref[...] + (1.0 - ADAM_B1) * g
                v_new = ADAM_B2 * v_ref[...] + (1.0 - ADAM_B2) * (g * g)
                m_hat = m_new / c1
                v_hat = v_new / c2
                g_ref[...] = g
                d_ref[...] = -ADAM_LR * (m_hat / (jnp.sqrt(v_hat) + ADAM_EPS) + ADAM_WD * w_ref[...])
                mo_ref[...] = m_new
                vo_ref[...] = v_new

    def p_spec(l):
        return pl.BlockSpec((P, tm, C), lambda li, t: (0, jnp.where(li == l, t, jnp.where(li < l, 0, tiles - 1)), 0))

    row = pl.BlockSpec((tm, C), lambda li, t: (li * tiles + t, 0))
    return pl.pallas_call(
        body,
        name=name,
        grid=(L, tiles),
        out_shape=(jax.ShapeDtypeStruct((L * R, C), F32),) * 4,
        in_specs=[p_spec(l) for l in range(L)] + [row, row, row],
        out_specs=(row, row, row, row),
        compiler_params=_params("arbitrary", "arbitrary"),
    )(*parts, w, m, v)


BIG = ("w_qkv", "w_o", "w_pw1", "w_pw2", "w_up", "w_down")
SMALL_SHARDED = (("b_pw1", 1), ("w_dw", 2), ("b_dw", 1), ("conv_ln_g", 1), ("conv_ln_b", 1), ("b_pw2", 1))
SMALL_REPL = ("b_mod", "norm_mix", "norm_mlp", "b_qkv", "b_o", "sinks", "final_norm")
WEIGHTS = ("w_mod", "b_mod", "norm_mix", "norm_mlp", "w_qkv", "b_qkv", "w_o", "b_o", "sinks", "w_pw1", "b_pw1",
           "w_dw", "b_dw", "conv_ln_g", "conv_ln_b", "w_pw2", "b_pw2", "w_up", "w_down", "final_norm")


def _step(x, c, loss_target, W, M, V):
    B, S, D = x.shape
    T = B * S
    L = W["w_mod"].shape[0]
    n_mod = W["w_mod"].shape[2]
    me = 4 * lax.axis_index("x") + 2 * lax.axis_index("y") + lax.axis_index("c")

    Wb = {n: W[n].astype(BF16) for n in BIG}

    def shards(i):
        jm = i // 2
        first, last = ("w_qkv", "w_o") if i % 2 == 0 else ("w_pw1", "w_pw2")
        return [Wb[first][jm], Wb[last][jm], Wb["w_up"][i], Wb["w_down"][i]]

    def carried(res, payload):
        return res if payload else (res, [])

    small_names = [n for n, _ in SMALL_SHARDED]
    small_src, small_sizes = _pack([c] + [W[n] for n in small_names], F32, 0)
    got = _exchange("gather_first", [(small_src, True), (shards(0)[0], True)])
    small_parts = _unpack(got[0], small_sizes, [c.shape] + [W[n].shape for n in small_names], 1)
    c_all = small_parts[0].reshape(N_DEV * B, D)
    full = {n: _from_slots(p, d) for (n, d), p in zip(SMALL_SHARDED, small_parts[1:])}
    gathered = [[got[1], None, None, None]] + [None] * (L - 1)

    b_mod_mine = lax.dynamic_slice_in_dim(W["b_mod"], me * n_mod, n_mod, axis=1).reshape(L, 1, n_mod)
    mod_part = _mod_fwd(c_all, W["w_mod"].astype(BF16), b_mod_mine)
    mod_slots = mod_part.reshape(L, N_DEV, B, n_mod).transpose(1, 0, 2, 3).reshape(N_DEV, L * B, n_mod)
    mod_recv = _exchange("scatter_mod", [(mod_slots, False)])[0]
    mod = mod_recv.reshape(N_DEV, L, B, n_mod).transpose(1, 2, 0, 3).reshape(L, B, N_MOD, 1, D)

    w_dw3 = [_conv_weight_chunks(full["w_dw"][j], D) for j in range(full["w_dw"].shape[0])]

    xc = x.reshape(T, D)
    saved = []
    h1 = _normmod_fwd("normmod_mix_fwd", xc, W["norm_mix"][0][None], mod[0, :, 1], mod[0, :, 0], S)
    for i in range(L):
        jm = i // 2
        sh1, sc1, g1, sh2, sc2, g2 = (mod[i, :, t] for t in range(N_MOD))
        nxt = [[(s, True)] for s in shards(i + 1)] if i + 1 < L else [[]] * 4
        g_first, g_last, g_up, g_down = gathered[i]
        sv = {"x_in": xc, "h1": h1}
        if i % 2 == 0:
            w_qkv_full = _from_slots(g_first, 1)
            sv["w_first"] = w_qkv_full
            mine = [(s, True) for s in shards(0)[1:]] if i == 0 else []
            pay = nxt[0] + mine[:1]
            qkv, n_first = carried(_mm("qkv_fwd", h1, w_qkv_full, bias=W["b_qkv"][jm], payload=pay), pay)
            tab = _pair_tables(_attn_table(W["sinks"][jm]))
            mix_in, n_attn = carried(_attn_fwd(qkv, tab, B, S, payload=mine[1:]), mine[1:])
            if i == 0:
                g_last, g_up, g_down = n_first[-1], n_attn[0], n_attn[1]
            sv.update(qkv=qkv, tab=tab)
            b_out = W["b_o"][jm]
        else:
            sv["w_first"] = g_first
            u, n_first = carried(_mm("pw1_fwd", h1, g_first, w_form="nslots", bias=full["b_pw1"][jm], out_dtype=F32,
                                     payload=nxt[0]), nxt[0])
            cv, mix_in = _conv_fwd(u, w_dw3[jm], full["b_dw"][jm][None], full["conv_ln_g"][jm][None],
                                   full["conv_ln_b"][jm][None], S)
            sv.update(u=u, cv=cv)
            b_out = full["b_pw2"][jm]
        w_last = g_last.reshape(-1, D)
        w_down_full = g_down.reshape(-1, D)
        sv.update(mix_in=mix_in, w_last=w_last, g_up=g_up, w_down=w_down_full)
        (x1, y1, h2), n_last = carried(_mm("mix_out_fwd", mix_in, w_last, bias=b_out, epi="resid", x=xc, gate=g1,
                                           norm=(W["norm_mlp"][i][None], sc2, sh2), rows_per_batch=S,
                                           payload=nxt[1]), nxt[1])
        sv.update(y1=y1, x1=x1)
        up, n_up = carried(_mm("mlp_up_fwd", h2, g_up, w_form="nslots", epi="relu", payload=nxt[2]), nxt[2])
        norm_next = (W["norm_mix"][i + 1][None], mod[i + 1, :, 1], mod[i + 1, :, 0]) if i + 1 < L else None
        res, n_down = carried(_mm("mlp_down_fwd", up, w_down_full, a_sq=True, epi="resid", x=x1, gate=g2,
                                  norm=norm_next, rows_per_batch=S, payload=nxt[3]), nxt[3])
        x2, y2 = res[0], res[1]
        h1 = res[2] if i + 1 < L else None
        sv.update(h2=h2, up=up, y2=y2)
        saved.append(sv)
        xc = x2
        if i + 1 < L:
            gathered[i + 1] = [n_first[0], n_last[0], n_up[0], n_down[0]]

    dx, loss_blk, dgfin, dy2, dg2 = _loss_head(xc, loss_target.reshape(T, D), W["final_norm"][None],
                                               saved[L - 1]["y2"], mod[L - 1, :, 5], S)
    loss = lax.psum(loss_blk[0, 0], ("x", "y", "c"))

    G = {"final_norm": dgfin.reshape(D)}
    dmod_layers = [None] * L
    small_grads = ("norm_mix", "norm_mlp", "b_qkv", "b_o", "sinks", "b_pw1", "w_dw", "b_dw", "conv_ln_g", "conv_ln_b",
                   "b_pw2")
    acc = {n: [None] * W[n].shape[0] for n in small_grads}
    reduced = {n: [None] * W[n].shape[0] for n in BIG}

    waiting_up = None
    for i in reversed(range(L)):
        jm = i // 2
        sv = saved[i]
        sh1, sc1, g1, sh2, sc2, g2 = (mod[i, :, t] for t in range(N_MOD))
        pay = [(waiting_up[1], False)] if waiting_up else []
        (gd0, gd1), r = carried(_mm_tn("w_down_grad", sv["up"], dy2, a_sq=True, out_form="kslots", halves=True,
                                       tt_pref=4096, payload=pay), pay)
        if waiting_up:
            reduced["w_up"][waiting_up[0]].append(r[0])
        du, r0 = _mm("mlp_down_bwd", dy2, sv["w_down"], w_form="full_t", epi="relu2d", u=sv["up"],
                     payload=[(gd0, False)])
        (gu0, gu1), r1 = _mm_tn("w_up_grad", sv["h2"], du, out_form="nslots", halves=True, tt_pref=4096,
                                payload=[(gd1, False)])
        reduced["w_down"][i] = [r0[0], r1[0]]
        pay = [(gu0, False)] + ([] if i > 0 else [(gu1, False)])
        (dx, dsh2, dsc2, dgn, dy1, dg1, dy1_sum), r = _mm(
            "mlp_up_bwd", du, sv["g_up"], w_form="kslots_t", epi="normbwd", rows_per_batch=S,
            nb=dict(x=sv["x1"], dx_in=dx, gnorm=W["norm_mlp"][i][None], sc=sc2, gate=(sv["y1"], g1)),
            payload=pay)
        reduced["w_up"][i] = list(r)
        waiting_up = (i, gu1) if i > 0 else None
        acc["norm_mlp"][i] = dgn.reshape(D)
        gw_last = _mm_tn("mix_out_grad", sv["mix_in"], dy1)
        gw_last = gw_last.reshape((N_DEV, -1) + gw_last.shape[1:]).astype(BF16)
        below = (saved[i - 1]["y2"], mod[i - 1, :, 5]) if i > 0 else None
        nb_mix = dict(x=sv["x_in"], dx_in=dx, gnorm=W["norm_mix"][i][None], sc=sc1, gate=below)
        if i % 2 == 0:
            acc["b_o"][jm] = dy1_sum.reshape(D)
            do, r = _mm("attn_out_bwd", dy1, sv["w_last"], w_form="full_t", payload=[(gw_last, False)])
            reduced["w_o"][jm] = r[0]
            dq, dkv, dsk = _attn_bwd(sv["qkv"], sv["mix_in"], do, sv["tab"], B, S)
            acc["sinks"][jm] = jnp.sum(dsk[..., 0], axis=0).transpose(0, 2, 1).reshape(N_HEADS)
            dqkv = jnp.concatenate([dq, dkv], axis=1)
            gw_qkv, db_qkv = _mm_tn("w_qkv_grad", sv["h1"], dqkv, colsum=True)
            acc["b_qkv"][jm] = db_qkv.reshape(-1)
            gw_qkv = _to_slots(gw_qkv, 1).astype(BF16)
            res, r = _mm("qkv_bwd", dqkv, sv["w_first"], w_form="full_t", epi="normbwd", nb=nb_mix, rows_per_batch=S,
                         payload=[(gw_qkv, False)])
            reduced["w_qkv"][jm] = r[0]
        else:
            acc["b_pw2"][jm] = dy1_sum.reshape(D)
            (dcv, dlg, dlb, dbdw), r = _mm(
                "pw2_bwd", dy1, sv["w_last"], w_form="full_t", epi="lnbwd",
                ln=(sv["cv"], full["conv_ln_g"][jm][None], full["conv_ln_b"][jm][None]), payload=[(gw_last, False)])
            reduced["w_pw2"][jm] = r[0]
            acc["conv_ln_g"][jm], acc["conv_ln_b"][jm], acc["b_dw"][jm] = dlg.reshape(-1), dlb.reshape(-1), dbdw.reshape(-1)
            du1, dwdw = _conv_bwd(dcv, sv["u"], w_dw3[jm], S)
            acc["w_dw"][jm] = dwdw.transpose(1, 0, 2).reshape(CONV_HALO, D)[:CONV_WIDTH]
            gw_pw1, db_pw1 = _mm_tn("w_pw1_grad", sv["h1"], du1, out_form="nslots", colsum=True, tt_pref=4096)
            acc["b_pw1"][jm] = db_pw1.reshape(-1)
            res, r = _mm("pw1_bwd", du1, sv["w_first"], w_form="kslots_t", epi="normbwd", nb=nb_mix, rows_per_batch=S,
                         payload=[(gw_pw1, False)])
            reduced["w_pw1"][jm] = r[0]
        dx, dsh1, dsc1, dgn = res[:4]
        acc["norm_mix"][i] = dgn.reshape(D)
        dmod_layers[i] = jnp.concatenate([dsh1, dsc1, dg1, dsh2, dsc2, dg2], axis=1).reshape(B, N_MOD * D)
        if i > 0:
            dy2, dg2 = res[4], res[5]
    grad_x = dx.reshape(B, S, D)
    for n, parts in acc.items():
        G[n] = jnp.stack(parts)

    dmod = jnp.stack(dmod_layers)
    dmod_slots = dmod.reshape(L, B, N_DEV, n_mod).transpose(2, 0, 1, 3).reshape(N_DEV, L * B, n_mod)
    dmod_recv = _exchange("gather_dmod", [(dmod_slots, False)])[0]
    dmod_all = dmod_recv.reshape(N_DEV, L, B, n_mod).transpose(1, 0, 2, 3).reshape(L, N_DEV * B, n_mod)
    g_w_mod, db_mod_mine = _mod_bwd(c_all, dmod_all)
    G["b_mod"] = lax.dynamic_update_slice_in_dim(jnp.zeros_like(W["b_mod"]), db_mod_mine.reshape(L, n_mod),
                                                 me * n_mod, axis=1)

    small_items = [jnp.broadcast_to(G[n][None], (N_DEV,) + G[n].shape) for n in SMALL_REPL]
    small_items += [_to_slots(G[n], d) for n, d in SMALL_SHARDED]
    small_slots, small_sizes2 = _pack(small_items, F32, 1)
    small_recv = _exchange("reduce_small", [(small_slots, False)])[0]

    out = {}

    def run_adam(name, parts, names):
        shapes = [W[n].shape for n in names]
        wp, sizes = _pack([W[n] for n in names], F32, 0)
        mp, _ = _pack([M[n] for n in names], F32, 0)
        vp, _ = _pack([V[n] for n in names], F32, 0)
        res = _adam(name, [parts], wp, mp, vp)
        for kind, buf in zip(("grad", "delta", "new_m", "new_v"), res):
            for n, a in zip(names, _unpack(buf, sizes, shapes, 0)):
                out[kind + "_" + n] = a

    for n in BIG + ("w_mod",):
        cols = W[n].shape[-1]
        if n == "w_mod":
            parts = [g_w_mod.reshape(1, -1, cols)]
        else:
            pieces = [p for r in reduced[n] for p in (r if isinstance(r, list) else [r])]
            parts = [p.reshape(N_DEV, -1, cols) for p in pieces]
        res = _adam("adam_" + n, parts, W[n].reshape(-1, cols), M[n].reshape(-1, cols), V[n].reshape(-1, cols))
        for kind, buf in zip(("grad", "delta", "new_m", "new_v"), res):
            out[kind + "_" + n] = buf.reshape(W[n].shape)
    run_adam("adam_small", small_recv, list(SMALL_REPL) + [n for n, _ in SMALL_SHARDED])

    res = [loss, grad_x]
    for kind in ("grad", "delta", "new_m", "new_v"):
        res += [out[kind + "_" + n] for n in WEIGHTS]
    return tuple(res)


def kernel(x, c, w_mod, b_mod, norm_mix, norm_mlp, w_qkv, b_qkv, w_o, b_o, sinks, w_pw1, b_pw1, w_dw, b_dw, conv_ln_g, conv_ln_b, w_pw2, b_pw2, w_up, w_down, final_norm, loss_target, m_w_mod, m_b_mod, m_norm_mix, m_norm_mlp, m_w_qkv, m_b_qkv, m_w_o, m_b_o, m_sinks, m_w_pw1, m_b_pw1, m_w_dw, m_b_dw, m_conv_ln_g, m_conv_ln_b, m_w_pw2, m_b_pw2, m_w_up, m_w_down, m_final_norm, v_w_mod, v_b_mod, v_norm_mix, v_norm_mlp, v_w_qkv, v_b_qkv, v_w_o, v_b_o, v_sinks, v_w_pw1, v_b_pw1, v_w_dw, v_b_dw, v_conv_ln_g, v_conv_ln_b, v_w_pw2, v_b_pw2, v_w_up, v_w_down, v_final_norm):
    W = dict(w_mod=w_mod, b_mod=b_mod, norm_mix=norm_mix, norm_mlp=norm_mlp, w_qkv=w_qkv, b_qkv=b_qkv, w_o=w_o,
             b_o=b_o, sinks=sinks, w_pw1=w_pw1, b_pw1=b_pw1, w_dw=w_dw, b_dw=b_dw, conv_ln_g=conv_ln_g,
             conv_ln_b=conv_ln_b, w_pw2=w_pw2, b_pw2=b_pw2, w_up=w_up, w_down=w_down, final_norm=final_norm)
    M = dict(w_mod=m_w_mod, b_mod=m_b_mod, norm_mix=m_norm_mix, norm_mlp=m_norm_mlp, w_qkv=m_w_qkv, b_qkv=m_b_qkv,
             w_o=m_w_o, b_o=m_b_o, sinks=m_sinks, w_pw1=m_w_pw1, b_pw1=m_b_pw1, w_dw=m_w_dw, b_dw=m_b_dw,
             conv_ln_g=m_conv_ln_g, conv_ln_b=m_conv_ln_b, w_pw2=m_w_pw2, b_pw2=m_b_pw2, w_up=m_w_up,
             w_down=m_w_down, final_norm=m_final_norm)
    V = dict(w_mod=v_w_mod, b_mod=v_b_mod, norm_mix=v_norm_mix, norm_mlp=v_norm_mlp, w_qkv=v_w_qkv, b_qkv=v_b_qkv,
             w_o=v_w_o, b_o=v_b_o, sinks=v_sinks, w_pw1=v_w_pw1, b_pw1=v_b_pw1, w_dw=v_w_dw, b_dw=v_b_dw,
             conv_ln_g=v_conv_ln_g, conv_ln_b=v_conv_ln_b, w_pw2=v_w_pw2, b_pw2=v_b_pw2, w_up=v_w_up,
             w_down=v_w_down, final_norm=v_final_norm)
    return _step(x, c, loss_target, W, M, V)
```

```python
import numpy as np
import jax
import jax.numpy as jnp
from jax import lax
from jax.experimental import pallas as pl
from jax.experimental.pallas import tpu as pltpu

F32 = jnp.float32
BF16 = jnp.bfloat16

N_DEV = 8
N_HEADS = 16
N_KV_HEADS = 2
HEAD_DIM = 64
GROUP = N_HEADS // N_KV_HEADS
ATT_BLOCK = 128
CONV_WIDTH = 31
CONV_HALO = 32
N_MOD = 6
EPS = 1e-6
ADAM_LR = 0.001
ADAM_B1 = 0.9
ADAM_B2 = 0.999
ADAM_EPS = 1e-08
ADAM_WD = 0.01
ADAM_STEP = 10
NEG_BIG = -1e30
PACK_COLS = 1024
VMEM_LIMIT_BYTES = 56 * 1024 * 1024
MESH_ID = pl.DeviceIdType.MESH


def _params(*sem):
    return pltpu.CompilerParams(dimension_semantics=sem, vmem_limit_bytes=VMEM_LIMIT_BYTES)


def _pick(n, pref, mult=8):
    if n <= pref:
        return n
    for t in range(pref, 0, -1):
        if n % t == 0 and t % mult == 0:
            return t
    return n


def _sigmoid(z):
    return 0.5 * jnp.tanh(0.5 * z) + 0.5


def _payload_layout(payload):
    n = len(payload)
    out_shapes = [jax.ShapeDtypeStruct((N_DEV,) + tuple(a.shape if bc else a.shape[1:]), a.dtype) for a, bc in payload]
    hbm = pl.BlockSpec(memory_space=pl.ANY)
    scratch = [pltpu.SemaphoreType.DMA((n * (N_DEV - 1),)), pltpu.SemaphoreType.DMA((n * (N_DEV - 1),)),
               pltpu.SemaphoreType.DMA((n,))]
    return [a for a, _ in payload], [hbm] * n, out_shapes, [hbm] * n, scratch


class _Plan:
    def __init__(self, bcasts, src_refs, dst_refs, send_sems, recv_sems, local_sems):
        x, y, c = lax.axis_index("x"), lax.axis_index("y"), lax.axis_index("c")
        me = 4 * x + 2 * y + c
        self.first, self.landed, self.relay, self.local = [], [], [], []
        for t, (bc, s_ref, d_ref) in enumerate(zip(bcasts, src_refs, dst_refs)):
            def remote(k, src, slot, to):
                sem = t * (N_DEV - 1) + k
                return pltpu.make_async_remote_copy(src_ref=src, dst_ref=d_ref.at[slot], send_sem=send_sems.at[sem],
                                                    recv_sem=recv_sems.at[sem], device_id=to, device_id_type=MESH_ID)
            if bc:
                chips = [(1 - x, y), (x, 1 - y), (1 - x, 1 - y)]
                self.first.append(remote(0, s_ref, me, (x, y, 1 - c)))
                for j, (px, py) in enumerate(chips):
                    cp = remote(1 + j, s_ref, me, (px, py, c))
                    self.first.append(cp)
                    self.landed.append(cp)
                    theirs = 4 * px + 2 * py + c
                    self.relay.append(remote(4 + j, d_ref.at[theirs], theirs, (x, y, 1 - c)))
                self.local.append(pltpu.make_async_copy(s_ref, d_ref.at[me], local_sems.at[t]))
            else:
                for k in range(1, N_DEV):
                    px = 1 - x if (k >> 2) & 1 else x
                    py = 1 - y if (k >> 1) & 1 else y
                    pc = 1 - c if k & 1 else c
                    self.first.append(remote(k - 1, s_ref.at[4 * px + 2 * py + pc], me, (px, py, pc)))
                self.local.append(pltpu.make_async_copy(s_ref.at[me], d_ref.at[me], local_sems.at[t]))

    def start(self):
        for cp in self.first + self.local:
            cp.start()

    def pass_on(self):
        for cp in self.landed:
            cp.wait_recv()
        for cp in self.relay:
            cp.start()

    def finish(self):
        for cp in self.first:
            cp.wait_send()
            if not any(cp is l for l in self.landed):
                cp.wait_recv()
        for cp in self.relay:
            cp.wait()
        for cp in self.local:
            cp.wait()


def _exchange(name, payload):
    n = len(payload)
    bcasts = [bc for _, bc in payload]
    arrays, in_specs, out_shapes, out_specs, scratch = _payload_layout(payload)

    def body(*refs):
        plan = _Plan(bcasts, refs[:n], refs[n:2 * n], *refs[2 * n:])
        plan.start()
        plan.pass_on()
        plan.finish()

    return pl.pallas_call(
        body,
        name=name,
        out_shape=tuple(out_shapes),
        in_specs=in_specs,
        out_specs=tuple(out_specs),
        scratch_shapes=scratch,
    )(*arrays)


def _pack(arrays, dtype, lead):
    lead_shape = arrays[0].shape[:lead]
    flat = [a.astype(dtype).reshape(lead_shape + (-1,)) for a in arrays]
    sizes = [f.shape[-1] for f in flat]
    total = sum(sizes)
    chunk = 16 * PACK_COLS
    padded = -(-total // chunk) * chunk
    if padded > total:
        flat.append(jnp.zeros(lead_shape + (padded - total,), dtype))
    buf = jnp.concatenate(flat, axis=-1)
    return buf.reshape(lead_shape + (padded // PACK_COLS, PACK_COLS)), sizes


def _unpack(buf, sizes, shapes, lead):
    lead_shape = buf.shape[:lead]
    flat = buf.reshape(lead_shape + (-1,))
    out, off = [], 0
    for n, shp in zip(sizes, shapes):
        out.append(lax.slice_in_dim(flat, off, off + n, axis=lead).reshape(lead_shape + tuple(shp)))
        off += n
    return out


def _to_slots(a, dim):
    shp = a.shape
    a = a.reshape(shp[:dim] + (N_DEV, shp[dim] // N_DEV) + shp[dim + 1:])
    return jnp.moveaxis(a, dim, 0)


def _from_slots(a, dim):
    a = jnp.moveaxis(a, 0, dim)
    shp = a.shape
    return a.reshape(shp[:dim] + (shp[dim] * shp[dim + 1],) + shp[dim + 2:])


def _mod_fwd(c_all, w, b):
    L, D, n = w.shape
    M = c_all.shape[0]

    def body(c_ref, w_ref, b_ref, o_ref):
        cv = c_ref[...]
        cs = (cv * _sigmoid(cv)).astype(BF16)
        o_ref[...] = jnp.dot(cs, w_ref[...], preferred_element_type=F32) + b_ref[...]

    return pl.pallas_call(
        body,
        name="mod_fwd",
        grid=(L,),
        out_shape=jax.ShapeDtypeStruct((L, M, n), F32),
        in_specs=[
            pl.BlockSpec((M, D), lambda l: (0, 0)),
            pl.BlockSpec((None, D, n), lambda l: (l, 0, 0)),
            pl.BlockSpec((None, 1, n), lambda l: (l, 0, 0)),
        ],
        out_specs=pl.BlockSpec((None, M, n), lambda l: (l, 0, 0)),
        compiler_params=_params("arbitrary"),
    )(c_all, w, b)


def _mod_bwd(c_all, dmod_all):
    L, M, n = dmod_all.shape
    D = c_all.shape[1]

    def body(c_ref, d_ref, dw_ref, db_ref):
        cv = c_ref[...]
        cs = (cv * _sigmoid(cv)).astype(BF16)
        d = d_ref[...]
        dw_ref[...] = lax.dot_general(cs, d.astype(BF16), (((0,), (0,)), ((), ())), preferred_element_type=F32)
        db_ref[...] = jnp.sum(d, axis=0, keepdims=True)

    return pl.pallas_call(
        body,
        name="mod_bwd",
        grid=(L,),
        out_shape=(jax.ShapeDtypeStruct((L, D, n), F32), jax.ShapeDtypeStruct((L, 1, n), F32)),
        in_specs=[
            pl.BlockSpec((M, D), lambda l: (0, 0)),
            pl.BlockSpec((None, M, n), lambda l: (l, 0, 0)),
        ],
        out_specs=(
            pl.BlockSpec((None, D, n), lambda l: (l, 0, 0)),
            pl.BlockSpec((None, 1, n), lambda l: (l, 0, 0)),
        ),
        compiler_params=_params("arbitrary"),
    )(c_all, dmod_all)


def _mm(name, a, w, *, w_form="full", out_dtype=BF16, bias=None, a_sq=False, epi=None, x=None, gate=None, u=None,
        norm=None, nb=None, ln=None, rows_per_batch=None, tm_pref=512, payload=()):
    M, K = a.shape
    if w_form == "full":
        N = w.shape[1]
        nc = _pick(N, 1024, 128)
    elif w_form == "full_t":
        N = w.shape[0]
        nc = _pick(N, 1024, 128)
    elif w_form == "nslots":
        nc = w.shape[2]
        N = N_DEV * nc
    else:
        N = w.shape[1]
        nc = N
    ks = K // N_DEV
    n_chunks = N // nc
    tm = _pick(M if rows_per_batch is None else rows_per_batch, tm_pref, 16)
    steps = M // tm
    relay_step = (3 * steps) // 4
    nb_gate = nb is not None and nb.get("gate") is not None
    tpb_nb = rows_per_batch // tm if epi == "normbwd" else 1
    has_bias = bias is not None
    n_pay = len(payload)
    bcasts = [bc for _, bc in payload]
    nt = (((1,), (1,)), ((), ()))


    def body(*refs):
        it = iter(refs)
        a_ref = next(it)
        w_ref = next(it)
        b_ref = next(it) if has_bias else None
        x_ref = next(it) if epi == "resid" else None
        g_ref = next(it) if epi == "resid" else None
        gn_ref, sc_ref, sh_ref = (next(it), next(it), next(it)) if norm is not None else (None, None, None)
        u_ref = next(it) if epi == "relu2d" else None
        if epi == "normbwd":
            nx_ref, ndx_ref, ngn_ref, nsc_ref = next(it), next(it), next(it), next(it)
            ny_ref, ngt_ref = (next(it), next(it)) if nb_gate else (None, None)
        if epi == "lnbwd":
            cv_ref, lg_ref, lb_ref = next(it), next(it), next(it)
        pay_src = [next(it) for _ in range(n_pay)]
        o_ref = next(it)
        y_ref = next(it) if epi == "resid" else None
        h_ref = next(it) if norm is not None else None
        if epi == "normbwd":
            dsh_ref, dsc_ref, dgn_ref = next(it), next(it), next(it)
            dy_ref, dgate_ref, cs_ref = (next(it), next(it), next(it)) if nb_gate else (None, None, None)
        if epi == "lnbwd":
            dlg_ref, dlb_ref, dcs_ref = next(it), next(it), next(it)
        pay_dst = [next(it) for _ in range(n_pay)]
        sems = list(it)

        if epi == "lnbwd":
            @pl.when(pl.program_id(0) == 0)
            def _():
                dlg_ref[...] = jnp.zeros_like(dlg_ref)
                dlb_ref[...] = jnp.zeros_like(dlb_ref)
                dcs_ref[...] = jnp.zeros_like(dcs_ref)

        if epi == "normbwd":
            @pl.when(pl.program_id(0) % tpb_nb == 0)
            def _():
                dsh_ref[...] = jnp.zeros_like(dsh_ref)
                dsc_ref[...] = jnp.zeros_like(dsc_ref)
                if nb_gate:
                    dgate_ref[...] = jnp.zeros_like(dgate_ref)

            @pl.when(pl.program_id(0) == 0)
            def _():
                dgn_ref[...] = jnp.zeros_like(dgn_ref)
                if nb_gate:
                    cs_ref[...] = jnp.zeros_like(cs_ref)

        if n_pay:
            @pl.when(pl.program_id(0) == 0)
            def _():
                _Plan(bcasts, pay_src, pay_dst, *sems).start()

            @pl.when(pl.program_id(0) == relay_step)
            def _():
                _Plan(bcasts, pay_src, pay_dst, *sems).pass_on()

        av = None if w_form == "kslots_t" else a_ref[...]
        if a_sq:
            av = av * av
        for ci in range(n_chunks):
            cols = slice(ci * nc, (ci + 1) * nc)
            if w_form == "full":
                acc = jnp.dot(av, w_ref[:, cols], preferred_element_type=F32)
            elif w_form == "full_t":
                acc = lax.dot_general(av, w_ref[cols, :], nt, preferred_element_type=F32)
            elif w_form == "nslots":
                acc = jnp.dot(av, w_ref[ci], preferred_element_type=F32)
            else:
                acc = lax.dot_general(a_ref[:, 0:ks], w_ref[0], nt, preferred_element_type=F32)
                for j in range(1, N_DEV):
                    acc = acc + lax.dot_general(a_ref[:, j * ks:(j + 1) * ks], w_ref[j], nt,
                                                preferred_element_type=F32)
            if has_bias:
                acc = acc + b_ref[:, cols]
            if epi == "resid":
                xn = x_ref[:, cols] + g_ref[:, cols] * acc
                o_ref[:, cols] = xn
                y_ref[:, cols] = acc.astype(BF16)
                if norm is not None:
                    r = lax.rsqrt(jnp.mean(xn * xn, axis=-1, keepdims=True) + EPS)
                    h_ref[...] = ((xn * r * gn_ref[...]) * (1.0 + sc_ref[...]) + sh_ref[...]).astype(BF16)
            elif epi == "relu":
                o_ref[:, cols] = jnp.maximum(acc, 0.0).astype(out_dtype)
            elif epi == "relu2d":
                o_ref[:, cols] = (acc * (2.0 * u_ref[:, cols].astype(F32))).astype(out_dtype)
            elif epi == "normbwd":
                xv = nx_ref[...]
                gn = ngn_ref[...]
                r = lax.rsqrt(jnp.mean(xv * xv, axis=-1, keepdims=True) + EPS)
                xhat = xv * r
                dsh_ref[...] += jnp.sum(acc, axis=0, keepdims=True)
                dsc_ref[...] += jnp.sum(acc * (xhat * gn), axis=0, keepdims=True)
                dn = acc * (1.0 + nsc_ref[...])
                dgn_ref[...] += jnp.sum(dn * xhat, axis=0, keepdims=True)
                dxhat = dn * gn
                dx = ndx_ref[...] + r * (dxhat - xhat * jnp.mean(dxhat * xhat, axis=-1, keepdims=True))
                o_ref[...] = dx
                if nb_gate:
                    dy = dx * ngt_ref[...]
                    dy_ref[...] = dy.astype(BF16)
                    dgate_ref[...] += jnp.sum(dx * ny_ref[...].astype(F32), axis=0, keepdims=True)
                    cs_ref[...] += jnp.sum(dy, axis=0, keepdims=True)
            elif epi == "lnbwd":
                cv = cv_ref[...]
                lg = lg_ref[...]
                xc = cv - jnp.mean(cv, axis=-1, keepdims=True)
                rstd = lax.rsqrt(jnp.mean(xc * xc, axis=-1, keepdims=True) + EPS)
                xhat = xc * rstd
                ln = xhat * lg + lb_ref[...]
                sg = _sigmoid(ln)
                dln = acc * (sg * (1.0 + ln * (1.0 - sg)))
                dlg_ref[...] += jnp.sum(dln * xhat, axis=0, keepdims=True)
                dlb_ref[...] += jnp.sum(dln, axis=0, keepdims=True)
                dxhat = dln * lg
                dcv = rstd * (dxhat - jnp.mean(dxhat, axis=-1, keepdims=True)
                              - xhat * jnp.mean(dxhat * xhat, axis=-1, keepdims=True))
                o_ref[...] = dcv
                dcs_ref[...] += jnp.sum(dcv, axis=0, keepdims=True)
            else:
                o_ref[:, cols] = acc.astype(out_dtype)

        if n_pay:
            @pl.when(pl.program_id(0) == steps - 1)
            def _():
                _Plan(bcasts, pay_src, pay_dst, *sems).finish()

    args = [a, w]
    w_block = w.shape
    specs = [pl.BlockSpec((tm, K), lambda i: (i, 0)), pl.BlockSpec(w_block, lambda i: (0,) * len(w_block))]
    if has_bias:
        args.append(bias.reshape(1, N).astype(F32))
        specs.append(pl.BlockSpec((1, N), lambda i: (0, 0)))
    row_spec = pl.BlockSpec((tm, N), lambda i: (i, 0))
    if epi == "resid":
        tpb = rows_per_batch // tm
        vec_spec = pl.BlockSpec((None, 1, N), lambda i: (i // tpb, 0, 0))
        args += [x, gate]
        specs += [row_spec, vec_spec]
        out_shape = [jax.ShapeDtypeStruct((M, N), F32), jax.ShapeDtypeStruct((M, N), BF16)]
        out_specs = [row_spec, row_spec]
        if norm is not None:
            assert n_chunks == 1
            args += list(norm)
            specs += [pl.BlockSpec((1, N), lambda i: (0, 0)), vec_spec, vec_spec]
            out_shape.append(jax.ShapeDtypeStruct((M, N), BF16))
            out_specs.append(row_spec)
    elif epi == "normbwd":
        assert n_chunks == 1
        vec_spec = pl.BlockSpec((None, 1, N), lambda i: (i // tpb_nb, 0, 0))
        one_spec = pl.BlockSpec((1, N), lambda i: (0, 0))
        nbat = M // rows_per_batch
        f_vec, f_one = jax.ShapeDtypeStruct((nbat, 1, N), F32), jax.ShapeDtypeStruct((1, N), F32)
        args += [nb["x"], nb["dx_in"], nb["gnorm"], nb["sc"]]
        specs += [row_spec, row_spec, one_spec, vec_spec]
        out_shape = [jax.ShapeDtypeStruct((M, N), F32), f_vec, f_vec, f_one]
        out_specs = [row_spec, vec_spec, vec_spec, one_spec]
        if nb_gate:
            args += list(nb["gate"])
            specs += [row_spec, vec_spec]
            out_shape += [jax.ShapeDtypeStruct((M, N), BF16), f_vec, f_one]
            out_specs += [row_spec, vec_spec, one_spec]
    elif epi == "lnbwd":
        assert n_chunks == 1
        one_spec = pl.BlockSpec((1, N), lambda i: (0, 0))
        f_one = jax.ShapeDtypeStruct((1, N), F32)
        args += list(ln)
        specs += [row_spec, one_spec, one_spec]
        out_shape = [jax.ShapeDtypeStruct((M, N), F32), f_one, f_one, f_one]
        out_specs = [row_spec, one_spec, one_spec, one_spec]
    else:
        if epi == "relu2d":
            args.append(u)
            specs.append(row_spec)
        out_shape = [jax.ShapeDtypeStruct((M, N), out_dtype)]
        out_specs = [row_spec]
    scratch = []
    if n_pay:
        p_arrays, p_in, p_shapes, p_out, scratch = _payload_layout(payload)
        args += p_arrays
        specs += p_in
        out_shape += p_shapes
        out_specs += p_out
    res = pl.pallas_call(
        body,
        name=name,
        grid=(steps,),
        out_shape=tuple(out_shape),
        in_specs=specs,
        out_specs=tuple(out_specs),
        scratch_shapes=scratch,
        compiler_params=_params("arbitrary" if (n_pay or epi in ("normbwd", "lnbwd")) else "parallel"),
    )(*args)
    if epi == "resid":
        n_own = 3 if norm is not None else 2
    elif epi == "normbwd":
        n_own = 7 if nb_gate else 4
    elif epi == "lnbwd":
        n_own = 4
    else:
        n_own = 1
    own = res[0] if n_own == 1 else tuple(res[:n_own])
    return (own, list(res[n_own:])) if n_pay else own


def _mm_tn(name, a, b, *, a_sq=False, out_form="full", halves=False, colsum=False, tt_pref=1024, tk_pref=1024,
           tn_pref=1024, payload=()):
    T, K = a.shape
    N = b.shape[1]
    tt = _pick(T, tt_pref, 16)
    tk = K // N_DEV if out_form == "kslots" else _pick(K, tk_pref, 128)
    tn = N // N_DEV if out_form == "nslots" else _pick(N, tn_pref, 128)
    nt_steps = T // tt
    grid = (K // tk, N // tn, nt_steps)
    slots = out_form != "full"
    assert not colsum or K == tk
    n_pay = len(payload)
    bcasts = [bc for _, bc in payload]

    def body(*refs):
        it = iter(refs)
        a_ref, b_ref = next(it), next(it)
        pay_src = [next(it) for _ in range(n_pay)]
        o_ref = next(it)
        o2_ref = next(it) if halves else None
        cs_ref = next(it) if colsum else None
        pay_dst = [next(it) for _ in range(n_pay)]
        acc = next(it) if slots else o_ref
        sems = list(it)
        step = (pl.program_id(0) * grid[1] + pl.program_id(1)) * grid[2] + pl.program_id(2)

        if n_pay:
            @pl.when(step == 0)
            def _():
                _Plan(bcasts, pay_src, pay_dst, *sems).start()

        @pl.when(pl.program_id(2) == 0)
        def _():
            acc[...] = jnp.zeros_like(acc)
            if colsum:
                cs_ref[...] = jnp.zeros_like(cs_ref)

        bv = b_ref[...]
        av = a_ref[...]
        if a_sq:
            av = av * av
        acc[...] += lax.dot_general(av, bv, (((0,), (0,)), ((), ())), preferred_element_type=F32)
        if colsum:
            cs_ref[...] += jnp.sum(bv.astype(F32), axis=0, keepdims=True)

        if slots:
            @pl.when(pl.program_id(2) == nt_steps - 1)
            def _():
                if halves:
                    o_ref[...] = acc[0:tk // 2, :].astype(o_ref.dtype)
                    o2_ref[...] = acc[tk // 2:, :].astype(o2_ref.dtype)
                else:
                    o_ref[...] = acc[...].astype(o_ref.dtype)

        if n_pay:
            @pl.when(step == grid[0] * grid[1] * grid[2] - 1)
            def _():
                plan = _Plan(bcasts, pay_src, pay_dst, *sems)
                plan.pass_on()
                plan.finish()

    if out_form == "full":
        out_shape = [jax.ShapeDtypeStruct((K, N), F32)]
        out_specs = [pl.BlockSpec((tk, tn), lambda k, n, t: (k, n))]
    else:
        assert not halves or out_form == "kslots" or K == tk
        rows = tk // 2 if halves else tk
        if out_form == "nslots":
            shape, spec = (N_DEV, K // tk * rows, tn), pl.BlockSpec((None, rows, tn), lambda k, n, t: (n, k, 0))
        else:
            shape, spec = (N_DEV, rows, N), pl.BlockSpec((None, rows, tn), lambda k, n, t: (k, 0, n))
        out_shape = [jax.ShapeDtypeStruct(shape, BF16)] * (2 if halves else 1)
        out_specs = [spec] * (2 if halves else 1)
    if colsum:
        out_shape.append(jax.ShapeDtypeStruct((1, N), F32))
        out_specs.append(pl.BlockSpec((1, tn), lambda k, n, t: (0, n)))
    args = [a, b]
    in_specs = [pl.BlockSpec((tt, tk), lambda k, n, t: (t, k)), pl.BlockSpec((tt, tn), lambda k, n, t: (t, n))]
    scratch = [pltpu.VMEM((tk, tn), F32)] if slots else []
    n_own = len(out_shape)
    if n_pay:
        p_arrays, p_in, p_shapes, p_out, p_scratch = _payload_layout(payload)
        args += p_arrays
        in_specs += p_in
        out_shape += p_shapes
        out_specs += p_out
        scratch += p_scratch
    res = pl.pallas_call(
        body,
        name=name,
        grid=grid,
        out_shape=tuple(out_shape),
        in_specs=in_specs,
        out_specs=tuple(out_specs),
        scratch_shapes=scratch,
        compiler_params=_params(*(("arbitrary",) * 3 if n_pay else ("parallel", "parallel", "arbitrary"))),
    )(*args)
    own = res[0] if n_own == 1 else tuple(res[:n_own])
    return (own, list(res[n_own:])) if n_pay else own


def _normmod_fwd(name, x, gnorm, sc, sh, S, tm_pref=512):
    T, D = x.shape
    tm = _pick(S, tm_pref, 16)
    tpb = S // tm

    def body(x_ref, g_ref, sc_ref, sh_ref, o_ref):
        xv = x_ref[...]
        r = lax.rsqrt(jnp.mean(xv * xv, axis=-1, keepdims=True) + EPS)
        n = xv * r * g_ref[...]
        o_ref[...] = (n * (1.0 + sc_ref[...]) + sh_ref[...]).astype(BF16)

    row = pl.BlockSpec((tm, D), lambda i: (i, 0))
    vec = pl.BlockSpec((None, 1, D), lambda i: (i // tpb, 0, 0))
    return pl.pallas_call(
        body,
        name=name,
        grid=(T // tm,),
        out_shape=jax.ShapeDtypeStruct((T, D), BF16),
        in_specs=[row, pl.BlockSpec((1, D), lambda i: (0, 0)), vec, vec],
        out_specs=row,
        compiler_params=_params("parallel"),
    )(x, gnorm, sc, sh)


ATT_ROWS = GROUP * ATT_BLOCK
ATT_SPAN = 2 * ATT_BLOCK
ATT_SCALE = HEAD_DIM ** -0.5


def _attn_table(sinks):
    slopes = jnp.asarray(np.array([2.0 ** (-8.0 * (h + 1) / N_HEADS) for h in range(N_HEADS)], np.float32))
    r = jnp.arange(ATT_BLOCK)[:, None]
    cc = jnp.arange(ATT_SPAN)[None, :]
    dist = r + ATT_BLOCK - cc
    ok = (dist >= 0) & (dist < ATT_BLOCK)
    tab = jnp.where(ok[None], -slopes[:, None, None] * dist.astype(F32)[None], NEG_BIG)
    tab = jnp.where((cc == 0)[None], sinks.astype(F32)[:, None, None], tab)
    return tab.reshape(N_KV_HEADS, ATT_ROWS, ATT_SPAN)


KV_COLS = N_KV_HEADS * HEAD_DIM


PAIR = 2 * HEAD_DIM


def _pair_tables(tab):
    hkv, rows, span = tab.shape
    pairs = rows // (2 * ATT_BLOCK)
    return tab.reshape(hkv, pairs, 2, ATT_BLOCK, span).transpose(0, 2, 1, 3, 4).reshape(hkv, 2, pairs * ATT_BLOCK, span)


def _stack_pairs(ref, pairs, col0):
    return jnp.concatenate([ref[:, col0 + j * PAIR:col0 + (j + 1) * PAIR] for j in range(pairs)], axis=0)


def _unstack_pairs(v, pairs):
    return jnp.concatenate([v[j * ATT_BLOCK:(j + 1) * ATT_BLOCK, :] for j in range(pairs)], axis=1)


def _swap_halves(x):
    return jnp.concatenate([x[:, HEAD_DIM:], x[:, :HEAD_DIM]], axis=1)


def _load_span(prev_ref, cur_ref, buf_e, buf_o, kv, mult):
    lane = lax.broadcasted_iota(jnp.int32, (ATT_BLOCK, PAIR), 1)
    row = lax.broadcasted_iota(jnp.int32, (ATT_BLOCK, PAIR), 0)
    own = lane // HEAD_DIM == kv
    for r0, ref, first in ((0, prev_ref, True), (ATT_BLOCK, cur_ref, False)):
        v = ref[...]
        if mult != 1.0:
            v = v * mult
        zero = jnp.zeros_like(v)
        mine = jnp.where(own, v, zero)
        if first:
            mine = jnp.where(row > 0, mine, zero)
        other = _swap_halves(mine)
        buf_e[r0:r0 + ATT_BLOCK, :] = mine if kv == 0 else other
        buf_o[r0:r0 + ATT_BLOCK, :] = other if kv == 0 else mine


def _first_block_penalty(i):
    col = lax.broadcasted_iota(jnp.int32, (1, ATT_SPAN), 1)
    return jnp.where((col < ATT_BLOCK) & (col > 0), jnp.where(i > 0, 0.0, NEG_BIG), 0.0).astype(F32)


def _attn_probs(qp, kbuf, bias, first_pen):
    nt = (((1,), (1,)), ((), ()))
    s = lax.dot_general(qp, kbuf, nt, preferred_element_type=F32) + bias + first_pen
    m = jnp.max(s, axis=-1, keepdims=True)
    e = jnp.exp(s - m)
    return e * (1.0 / jnp.sum(e, axis=-1, keepdims=True))


def _attn_specs(nb, q_dim, pairs, row_q, row_cur, row_prev):
    kblk = q_dim // KV_COLS
    qspec = pl.BlockSpec((ATT_BLOCK, q_dim), lambda b, i: (row_q(b, i), 0))
    kv_specs = [pl.BlockSpec((ATT_BLOCK, KV_COLS), (lambda b, i, r=r, c=c: (r(b, i), c)))
                for c in (kblk, kblk + 1) for r in (row_cur, row_prev)]
    tspec = pl.BlockSpec((N_KV_HEADS, 2, pairs * ATT_BLOCK, ATT_SPAN), lambda b, i: (0, 0, 0, 0))
    return qspec, kv_specs, tspec


def _attn_fwd(qkv, tab2, B, S, payload=()):
    assert N_KV_HEADS == 2 and GROUP % 2 == 0
    T = qkv.shape[0]
    pairs = GROUP // 2
    gw = pairs * PAIR
    nb = S // ATT_BLOCK
    q_dim = N_HEADS * HEAD_DIM
    n_pay = len(payload)
    bcasts = [bc for _, bc in payload]
    steps = B * nb

    def body(*refs):
        q_ref, kc_ref, kp_ref, vc_ref, vp_ref, tab_ref = refs[:6]
        pay_src = refs[6:6 + n_pay]
        o_ref = refs[6 + n_pay]
        pay_dst = refs[7 + n_pay:7 + 2 * n_pay]
        ke, ko, ve, vo = refs[7 + 2 * n_pay:11 + 2 * n_pay]
        sems = refs[11 + 2 * n_pay:]
        i = pl.program_id(1)
        if n_pay:
            step = pl.program_id(0) * nb + i

            @pl.when(step == 0)
            def _():
                _Plan(bcasts, pay_src, pay_dst, *sems).start()

            @pl.when(step == (3 * steps) // 4)
            def _():
                _Plan(bcasts, pay_src, pay_dst, *sems).pass_on()

            @pl.when(step == steps - 1)
            def _():
                _Plan(bcasts, pay_src, pay_dst, *sems).finish()

        pen = _first_block_penalty(i)
        for kv in range(N_KV_HEADS):
            _load_span(kp_ref, kc_ref, ke.at[kv], ko.at[kv], kv, ATT_SCALE)
            _load_span(vp_ref, vc_ref, ve.at[kv], vo.at[kv], kv, 1.0)
            qp = _stack_pairs(q_ref, pairs, kv * gw)
            p_e = _attn_probs(qp, ke[kv], tab_ref[kv, 0], pen)
            p_o = _attn_probs(qp, ko[kv], tab_ref[kv, 1], pen)
            o = jnp.dot(p_e.astype(BF16), ve[kv], preferred_element_type=F32)
            o = o + jnp.dot(p_o.astype(BF16), vo[kv], preferred_element_type=F32)
            o_ref[:, kv * gw:(kv + 1) * gw] = _unstack_pairs(o.astype(BF16), pairs)

    qspec, kv_specs, tspec = _attn_specs(nb, q_dim, pairs, lambda b, i: b * nb + i, lambda b, i: b * nb + i,
                                         lambda b, i: b * nb + jnp.maximum(i - 1, 0))
    args, in_specs = [qkv] * 5 + [tab2], [qspec] + kv_specs + [tspec]
    out_shape, out_specs = [jax.ShapeDtypeStruct((T, q_dim), BF16)], [qspec]
    scratch = [pltpu.VMEM((N_KV_HEADS, ATT_SPAN, PAIR), BF16)] * 4
    if n_pay:
        p_arrays, p_in, p_shapes, p_out, p_scratch = _payload_layout(payload)
        args += p_arrays
        in_specs += p_in
        out_shape += p_shapes
        out_specs += p_out
        scratch += p_scratch
    res = pl.pallas_call(
        body,
        name="attn_fwd",
        grid=(B, nb),
        out_shape=tuple(out_shape),
        in_specs=in_specs,
        out_specs=tuple(out_specs),
        scratch_shapes=scratch,
        compiler_params=_params(*(("arbitrary",) * 2 if n_pay else ("parallel", "arbitrary"))),
    )(*args)
    return (res[0], list(res[1:])) if n_pay else res[0]


def _attn_bwd(qkv, o, do, tab2, B, S):
    T = qkv.shape[0]
    hd = HEAD_DIM
    pairs = GROUP // 2
    gw = pairs * PAIR
    rows = pairs * ATT_BLOCK
    nb = S // ATT_BLOCK
    last = nb - 1
    q_dim = N_HEADS * hd
    tn = (((0,), (0,)), ((), ()))
    nt = (((1,), (1,)), ((), ()))

    def body(q_ref, kc_ref, kp_ref, vc_ref, vp_ref, tab_ref, o_ref, do_ref,
             dq_ref, dkv_ref, dsink_ref, carry_k, carry_v, ke, ko, ve, vo):
        i = pl.program_id(1)

        @pl.when(i == 0)
        def _():
            dsink_ref[...] = jnp.zeros_like(dsink_ref)

        def emit(kv, dk_rows, dv_rows):
            dkv_ref[:, kv * hd:(kv + 1) * hd] = dk_rows.astype(BF16)
            dkv_ref[:, KV_COLS + kv * hd:KV_COLS + (kv + 1) * hd] = dv_rows.astype(BF16)

        @pl.when(i < nb)
        def _():
            pen = _first_block_penalty(i)
            low = lax.broadcasted_iota(jnp.int32, (rows, PAIR), 1) < hd
            low_k = lax.broadcasted_iota(jnp.int32, (ATT_SPAN, PAIR), 1) < hd
            keep = lax.broadcasted_iota(jnp.int32, (ATT_SPAN, hd), 0) > 0
            hg = lax.broadcasted_iota(jnp.int32, (pairs, rows), 0)
            hr = lax.broadcasted_iota(jnp.int32, (pairs, rows), 1)
            head_of = jnp.where(hr // ATT_BLOCK == hg, 1.0, 0.0).astype(BF16)
            dks, dvs = [], []
            for kv in range(N_KV_HEADS):
                _load_span(kp_ref, kc_ref, ke.at[kv], ko.at[kv], kv, ATT_SCALE)
                _load_span(vp_ref, vc_ref, ve.at[kv], vo.at[kv], kv, 1.0)
                qp = _stack_pairs(q_ref, pairs, kv * gw)
                dop = _stack_pairs(do_ref, pairs, kv * gw)
                op = _stack_pairs(o_ref, pairs, kv * gw)
                p_e = _attn_probs(qp, ke[kv], tab_ref[kv, 0], pen)
                p_o = _attn_probs(qp, ko[kv], tab_ref[kv, 1], pen)
                prod = dop.astype(F32) * op.astype(F32)
                d_e = jnp.sum(jnp.where(low, prod, 0.0), axis=-1, keepdims=True)
                d_o = jnp.sum(prod, axis=-1, keepdims=True) - d_e
                ds_e = (p_e * (lax.dot_general(dop, ve[kv], nt, preferred_element_type=F32) - d_e)).astype(BF16)
                ds_o = (p_o * (lax.dot_general(dop, vo[kv], nt, preferred_element_type=F32) - d_o)).astype(BF16)
                dq = jnp.dot(ds_e, ke[kv], preferred_element_type=F32) + jnp.dot(ds_o, ko[kv], preferred_element_type=F32)
                dq_ref[:, kv * gw:(kv + 1) * gw] = _unstack_pairs(dq.astype(BF16), pairs)
                dsink_ref[kv, 0] += jnp.dot(head_of, ds_e, preferred_element_type=F32)[:, 0:128]
                dsink_ref[kv, 1] += jnp.dot(head_of, ds_o, preferred_element_type=F32)[:, 0:128]
                dk2 = jnp.where(low_k, lax.dot_general(ds_e, qp, tn, preferred_element_type=F32),
                                lax.dot_general(ds_o, qp, tn, preferred_element_type=F32))
                dv2 = jnp.where(low_k, lax.dot_general(p_e.astype(BF16), dop, tn, preferred_element_type=F32),
                                lax.dot_general(p_o.astype(BF16), dop, tn, preferred_element_type=F32))
                dks.append(jnp.where(keep, (dk2[:, :hd] + dk2[:, hd:]) * ATT_SCALE, 0.0))
                dvs.append(jnp.where(keep, dv2[:, :hd] + dv2[:, hd:], 0.0))

            @pl.when(i > 0)
            def _():
                for kv in range(N_KV_HEADS):
                    emit(kv, carry_k[kv] + dks[kv][0:ATT_BLOCK], carry_v[kv] + dvs[kv][0:ATT_BLOCK])

            for kv in range(N_KV_HEADS):
                carry_k[kv] = dks[kv][ATT_BLOCK:]
                carry_v[kv] = dvs[kv][ATT_BLOCK:]

        @pl.when(i == nb)
        def _():
            for kv in range(N_KV_HEADS):
                emit(kv, carry_k[kv], carry_v[kv])

    qspec, kv_specs, tspec = _attn_specs(nb, q_dim, pairs, lambda b, i: b * nb + jnp.minimum(i, last),
                                         lambda b, i: b * nb + jnp.minimum(i, last),
                                         lambda b, i: b * nb + jnp.clip(i - 1, 0, last))
    dkv = pl.BlockSpec((ATT_BLOCK, 2 * KV_COLS), lambda b, i: (b * nb + jnp.maximum(i - 1, 0), 0))
    dsk = pl.BlockSpec((None, N_KV_HEADS, 2, pairs, 128), lambda b, i: (b, 0, 0, 0, 0))
    return pl.pallas_call(
        body,
        name="attn_bwd",
        grid=(B, nb + 1),
        out_shape=(
            jax.ShapeDtypeStruct((T, q_dim), BF16),
            jax.ShapeDtypeStruct((T, 2 * KV_COLS), BF16),
            jax.ShapeDtypeStruct((B, N_KV_HEADS, 2, pairs, 128), F32),
        ),
        in_specs=[qspec] + kv_specs + [tspec, qspec, qspec],
        out_specs=(qspec, dkv, dsk),
        scratch_shapes=[pltpu.VMEM((N_KV_HEADS, ATT_BLOCK, hd), F32), pltpu.VMEM((N_KV_HEADS, ATT_BLOCK, hd), F32)]
        + [pltpu.VMEM((N_KV_HEADS, ATT_SPAN, PAIR), BF16)] * 4,
        compiler_params=_params("arbitrary", "arbitrary"),
    )(qkv, qkv, qkv, qkv, qkv, tab2, o, do)


def _conv_tiles(S):
    ts = _pick(S, 256, CONV_HALO)
    return ts, S // ts


def _conv_chunks(C, ts):
    lane = _pick(C, 128, 128)
    return lane, C // lane, _pick(ts, 128, 8)


def _conv_weight_chunks(w_dw, C):
    lane = _pick(C, 128, 128)
    w = jnp.pad(w_dw, ((0, CONV_HALO - CONV_WIDTH), (0, 0)))
    return w.reshape(CONV_HALO, C // lane, lane).transpose(1, 0, 2)


def _conv_fwd(u, w3, b_dw, ln_g, ln_b, S):
    T, C2 = u.shape
    C = C2 // 2
    B = T // S
    ts, nj = _conv_tiles(S)
    hb = ts // CONV_HALO
    lane, nc, rc = _conv_chunks(C, ts)

    def body(a_ref, g_ref, ap_ref, gp_ref, w_ref, bdw_ref, lg_ref, lb_ref, cv_ref, s_ref, buf, cvb):
        j = pl.program_id(1)
        glu_prev = ap_ref[...] * _sigmoid(gp_ref[...]) * (j > 0).astype(F32)
        glu = a_ref[...] * _sigmoid(g_ref[...])
        for cc in range(nc):
            buf[cc, 0:CONV_HALO, :] = glu_prev[:, cc * lane:(cc + 1) * lane]
            buf[cc, CONV_HALO:, :] = glu[:, cc * lane:(cc + 1) * lane]

        def chunk(cc, carry):
            for r0 in range(0, ts, rc):
                acc = jnp.zeros((rc, lane), F32)
                for kk in range(CONV_WIDTH):
                    lo = CONV_HALO - (CONV_WIDTH - 1 - kk) + r0
                    acc = acc + w_ref[cc, kk:kk + 1, :] * buf[cc, lo:lo + rc, :]
                cvb[cc, r0:r0 + rc, :] = acc
            return carry

        lax.fori_loop(0, nc, chunk, 0)
        for cc in range(nc):
            cv_ref[:, cc * lane:(cc + 1) * lane] = cvb[cc] + bdw_ref[:, cc * lane:(cc + 1) * lane]
        cv = cv_ref[...]
        mu = jnp.mean(cv, axis=-1, keepdims=True)
        xc = cv - mu
        rstd = lax.rsqrt(jnp.mean(xc * xc, axis=-1, keepdims=True) + EPS)
        ln = xc * rstd * lg_ref[...] + lb_ref[...]
        s_ref[...] = (ln * _sigmoid(ln)).astype(BF16)

    a_cur = pl.BlockSpec((ts, C), lambda b, j: (b * nj + j, 0))
    g_cur = pl.BlockSpec((ts, C), lambda b, j: (b * nj + j, 1))
    a_prev = pl.BlockSpec((CONV_HALO, C), lambda b, j: (jnp.maximum((b * nj + j) * hb - 1, 0), 0))
    g_prev = pl.BlockSpec((CONV_HALO, C), lambda b, j: (jnp.maximum((b * nj + j) * hb - 1, 0), 1))
    wspec = pl.BlockSpec((nc, CONV_HALO, lane), lambda b, j: (0, 0, 0))
    one = pl.BlockSpec((1, C), lambda b, j: (0, 0))
    return pl.pallas_call(
        body,
        name="conv_fwd",
        grid=(B, nj),
        out_shape=(jax.ShapeDtypeStruct((T, C), F32), jax.ShapeDtypeStruct((T, C), BF16)),
        in_specs=[a_cur, g_cur, a_prev, g_prev, wspec, one, one, one],
        out_specs=(a_cur, a_cur),
        scratch_shapes=[pltpu.VMEM((nc, CONV_HALO + ts, lane), F32), pltpu.VMEM((nc, ts, lane), F32)],
        compiler_params=_params("parallel", "arbitrary"),
    )(u, u, u, u, w3, b_dw, ln_g, ln_b)


def _conv_bwd(dcv, u, w3, S):
    T, C2 = u.shape
    C = C2 // 2
    B = T // S
    ts, nj = _conv_tiles(S)
    hb = ts // CONV_HALO
    n_halo_blocks = T // CONV_HALO
    lane, nc, rc = _conv_chunks(C, ts)

    def body(dcv_ref, dnx_ref, a_ref, g_ref, ap_ref, gp_ref, w_ref, du_ref, dw_ref, gbuf, dbuf, dglu, dw8):
        b, j = pl.program_id(0), pl.program_id(1)

        @pl.when((b == 0) & (j == 0))
        def _():
            dw8[...] = jnp.zeros_like(dw8)

        a = a_ref[...]
        sg = _sigmoid(g_ref[...])
        glu_prev = ap_ref[...] * _sigmoid(gp_ref[...]) * (j > 0).astype(F32)
        glu = a * sg
        dcur = dcv_ref[...]
        dnext = dnx_ref[...] * (j < nj - 1).astype(F32)
        for cc in range(nc):
            cols = slice(cc * lane, (cc + 1) * lane)
            gbuf[cc, 0:CONV_HALO, :] = glu_prev[:, cols]
            gbuf[cc, CONV_HALO:, :] = glu[:, cols]
            dbuf[cc, 0:ts, :] = dcur[:, cols]
            dbuf[cc, ts:, :] = dnext[:, cols]

        def chunk(cc, carry):
            for r0 in range(0, ts, rc):
                acc = jnp.zeros((rc, lane), F32)
                for kk in range(CONV_WIDTH):
                    d = CONV_WIDTH - 1 - kk
                    acc = acc + w_ref[cc, kk:kk + 1, :] * dbuf[cc, r0 + d:r0 + d + rc, :]
                dglu[cc, r0:r0 + rc, :] = acc
            for kk in range(CONV_WIDTH):
                d = CONV_WIDTH - 1 - kk
                p = jnp.zeros((rc, lane), F32)
                for r0 in range(0, ts, rc):
                    lo = CONV_HALO - d + r0
                    p = p + dbuf[cc, r0:r0 + rc, :] * gbuf[cc, lo:lo + rc, :]
                dw8[cc, kk * 8:(kk + 1) * 8, :] += jnp.sum(p.reshape(rc // 8, 8, lane), axis=0)
            return carry

        lax.fori_loop(0, nc, chunk, 0)
        for cc in range(nc):
            cols = slice(cc * lane, (cc + 1) * lane)
            dgl = dglu[cc]
            du_ref[:, cc * lane:(cc + 1) * lane] = (dgl * sg[:, cols]).astype(BF16)
            du_ref[:, C + cc * lane:C + (cc + 1) * lane] = (dgl * a[:, cols] * sg[:, cols] * (1.0 - sg[:, cols])).astype(BF16)

        @pl.when((b == B - 1) & (j == nj - 1))
        def _():
            dw_ref[...] = jnp.zeros_like(dw_ref)
            for kk in range(CONV_WIDTH):
                dw_ref[:, kk:kk + 1, :] = jnp.sum(dw8[:, kk * 8:(kk + 1) * 8, :], axis=1, keepdims=True)

    a_cur = pl.BlockSpec((ts, C), lambda b, j: (b * nj + j, 0))
    g_cur = pl.BlockSpec((ts, C), lambda b, j: (b * nj + j, 1))
    a_prev = pl.BlockSpec((CONV_HALO, C), lambda b, j: (jnp.maximum((b * nj + j) * hb - 1, 0), 0))
    g_prev = pl.BlockSpec((CONV_HALO, C), lambda b, j: (jnp.maximum((b * nj + j) * hb - 1, 0), 1))
    d_next = pl.BlockSpec((CONV_HALO, C), lambda b, j: (jnp.minimum((b * nj + j + 1) * hb, n_halo_blocks - 1), 0))
    wspec = pl.BlockSpec((nc, CONV_HALO, lane), lambda b, j: (0, 0, 0))
    return pl.pallas_call(
        body,
        name="conv_bwd",
        grid=(B, nj),
        out_shape=(jax.ShapeDtypeStruct((T, C2), BF16), jax.ShapeDtypeStruct((nc, CONV_HALO, lane), F32)),
        in_specs=[a_cur, d_next, a_cur, g_cur, a_prev, g_prev, wspec],
        out_specs=(pl.BlockSpec((ts, C2), lambda b, j: (b * nj + j, 0)), wspec),
        scratch_shapes=[
            pltpu.VMEM((nc, CONV_HALO + ts, lane), F32),
            pltpu.VMEM((nc, ts + CONV_HALO, lane), F32),
            pltpu.VMEM((nc, ts, lane), F32),
            pltpu.VMEM((nc, CONV_HALO * 8, lane), F32),
        ],
        compiler_params=_params("arbitrary", "arbitrary"),
    )(dcv, dcv, u, u, u, u, w3)


def _loss_head(x, tgt, gfin, y, gate, S, tm_pref=256):
    T, D = x.shape
    tm = _pick(S, tm_pref, 16)
    tpb = S // tm

    def body(x_ref, t_ref, g_ref, y_ref, gt_ref, dx_ref, loss_ref, dg_ref, dy_ref, dgate_ref):
        @pl.when(pl.program_id(0) == 0)
        def _():
            loss_ref[...] = jnp.zeros_like(loss_ref)
            dg_ref[...] = jnp.zeros_like(dg_ref)

        @pl.when(pl.program_id(0) % tpb == 0)
        def _():
            dgate_ref[...] = jnp.zeros_like(dgate_ref)

        xv = x_ref[...]
        g = g_ref[...]
        r = lax.rsqrt(jnp.mean(xv * xv, axis=-1, keepdims=True) + EPS)
        xhat = xv * r
        e = xhat * g - t_ref[...]
        row_loss = jnp.mean(e * e, axis=-1, keepdims=True)
        loss_ref[...] += 0.5 * jnp.sum(row_loss, axis=0, keepdims=True)
        dy = e * (1.0 / D)
        dg_ref[...] += jnp.sum(dy * xhat, axis=0, keepdims=True)
        dxhat = dy * g
        dx = r * (dxhat - xhat * jnp.mean(dxhat * xhat, axis=-1, keepdims=True))
        dx_ref[...] = dx
        dy_ref[...] = (dx * gt_ref[...]).astype(BF16)
        dgate_ref[...] += jnp.sum(dx * y_ref[...].astype(F32), axis=0, keepdims=True)

    row = pl.BlockSpec((tm, D), lambda i: (i, 0))
    vec = pl.BlockSpec((None, 1, D), lambda i: (i // tpb, 0, 0))
    one = pl.BlockSpec((1, D), lambda i: (0, 0))
    return pl.pallas_call(
        body,
        name="loss_head",
        grid=(T // tm,),
        out_shape=(
            jax.ShapeDtypeStruct((T, D), F32),
            jax.ShapeDtypeStruct((8, 128), F32),
            jax.ShapeDtypeStruct((1, D), F32),
            jax.ShapeDtypeStruct((T, D), BF16),
            jax.ShapeDtypeStruct((T // S, 1, D), F32),
        ),
        in_specs=[row, row, one, row, vec],
        out_specs=(row, pl.BlockSpec((8, 128), lambda i: (0, 0)), one, row, vec),
        compiler_params=_params("arbitrary"),
    )(x, tgt, gfin, y, gate)


def _adam(name, parts, w, m, v, tm_pref=256):
    L = len(parts)
    P, R, C = parts[0].shape
    fit = VMEM_LIMIT_BYTES * 5 // 8 // (2 * L * P * C * parts[0].dtype.itemsize)
    tm = _pick(R, max(16, min(tm_pref, fit // 16 * 16)), 16)
    tiles = R // tm
    c1 = 1.0 - ADAM_B1 ** ADAM_STEP
    c2 = 1.0 - ADAM_B2 ** ADAM_STEP

    def body(*refs):
        p_refs = refs[:L]
        w_ref, m_ref, v_ref, g_ref, d_ref, mo_ref, vo_ref = refs[L:]
        for l in range(L):
            @pl.when(pl.program_id(0) == l)
            def _():
                g = p_refs[l][0].astype(F32)
                for i in range(1, P):
                    g = g + p_refs[l][i].astype(F32)
                m_new = ADAM_B1 * m_ref[...] + (1.0 - ADAM_B1) * g
                v_new = ADAM_B2 * v_ref[...] + (1.0 - ADAM_B2) * (g * g)
                m_hat = m_new / c1
                v_hat = v_new / c2
                g_ref[...] = g
                d_ref[...] = -ADAM_LR * (m_hat / (jnp.sqrt(v_hat) + ADAM_EPS) + ADAM_WD * w_ref[...])
                mo_ref[...] = m_new
                vo_ref[...] = v_new

    def p_spec(l):
        return pl.BlockSpec((P, tm, C), lambda li, t: (0, jnp.where(li == l, t, jnp.where(li < l, 0, tiles - 1)), 0))

    row = pl.BlockSpec((tm, C), lambda li, t: (li * tiles + t, 0))
    return pl.pallas_call(
        body,
        name=name,
        grid=(L, tiles),
        out_shape=(jax.ShapeDtypeStruct((L * R, C), F32),) * 4,
        in_specs=[p_spec(l) for l in range(L)] + [row, row, row],
        out_specs=(row, row, row, row),
        compiler_params=_params("arbitrary", "arbitrary"),
    )(*parts, w, m, v)


BIG = ("w_qkv", "w_o", "w_pw1", "w_pw2", "w_up", "w_down")
SMALL_SHARDED = (("b_pw1", 1), ("w_dw", 2), ("b_dw", 1), ("conv_ln_g", 1), ("conv_ln_b", 1), ("b_pw2", 1))
SMALL_REPL = ("b_mod", "norm_mix", "norm_mlp", "b_qkv", "b_o", "sinks", "final_norm")
WEIGHTS = ("w_mod", "b_mod", "norm_mix", "norm_mlp", "w_qkv", "b_qkv", "w_o", "b_o", "sinks", "w_pw1", "b_pw1",
           "w_dw", "b_dw", "conv_ln_g", "conv_ln_b", "w_pw2", "b_pw2", "w_up", "w_down", "final_norm")


def _step(x, c, loss_target, W, M, V):
    B, S, D = x.shape
    T = B * S
    L = W["w_mod"].shape[0]
    n_mod = W["w_mod"].shape[2]
    me = 4 * lax.axis_index("x") + 2 * lax.axis_index("y") + lax.axis_index("c")

    Wb = {n: W[n].astype(BF16) for n in BIG}

    def shards(i):
        jm = i // 2
        first, last = ("w_qkv", "w_o") if i % 2 == 0 else ("w_pw1", "w_pw2")
        return [Wb[first][jm], Wb[last][jm], Wb["w_up"][i], Wb["w_down"][i]]

    def carried(res, payload):
        return res if payload else (res, [])

    small_names = [n for n, _ in SMALL_SHARDED]
    small_src, small_sizes = _pack([c] + [W[n] for n in small_names], F32, 0)
    got = _exchange("gather_first", [(small_src, True), (shards(0)[0], True)])
    small_parts = _unpack(got[0], small_sizes, [c.shape] + [W[n].shape for n in small_names], 1)
    c_all = small_parts[0].reshape(N_DEV * B, D)
    full = {n: _from_slots(p, d) for (n, d), p in zip(SMALL_SHARDED, small_parts[1:])}
    gathered = [[got[1], None, None, None]] + [None] * (L - 1)

    b_mod_mine = lax.dynamic_slice_in_dim(W["b_mod"], me * n_mod, n_mod, axis=1).reshape(L, 1, n_mod)
    mod_part = _mod_fwd(c_all, W["w_mod"].astype(BF16), b_mod_mine)
    mod_slots = mod_part.reshape(L, N_DEV, B, n_mod).transpose(1, 0, 2, 3).reshape(N_DEV, L * B, n_mod)
    mod_recv = _exchange("scatter_mod", [(mod_slots, False)])[0]
    mod = mod_recv.reshape(N_DEV, L, B, n_mod).transpose(1, 2, 0, 3).reshape(L, B, N_MOD, 1, D)

    w_dw3 = [_conv_weight_chunks(full["w_dw"][j], D) for j in range(full["w_dw"].shape[0])]

    xc = x.reshape(T, D)
    saved = []
    h1 = _normmod_fwd("normmod_mix_fwd", xc, W["norm_mix"][0][None], mod[0, :, 1], mod[0, :, 0], S)
    for i in range(L):
        jm = i // 2
        sh1, sc1, g1, sh2, sc2, g2 = (mod[i, :, t] for t in range(N_MOD))
        nxt = [[(s, True)] for s in shards(i + 1)] if i + 1 < L else [[]] * 4
        g_first, g_last, g_up, g_down = gathered[i]
        sv = {"x_in": xc, "h1": h1}
        if i % 2 == 0:
            w_qkv_full = _from_slots(g_first, 1)
            sv["w_first"] = w_qkv_full
            mine = [(s, True) for s in shards(0)[1:]] if i == 0 else []
            pay = nxt[0] + mine[:1]
            qkv, n_first = carried(_mm("qkv_fwd", h1, w_qkv_full, bias=W["b_qkv"][jm], payload=pay), pay)
            tab = _pair_tables(_attn_table(W["sinks"][jm]))
            mix_in, n_attn = carried(_attn_fwd(qkv, tab, B, S, payload=mine[1:]), mine[1:])
            if i == 0:
                g_last, g_up, g_down = n_first[-1], n_attn[0], n_attn[1]
            sv.update(qkv=qkv, tab=tab)
            b_out = W["b_o"][jm]
        else:
            sv["w_first"] = g_first
            u, n_first = carried(_mm("pw1_fwd", h1, g_first, w_form="nslots", bias=full["b_pw1"][jm], out_dtype=F32,
                                     payload=nxt[0]), nxt[0])
            cv, mix_in = _conv_fwd(u, w_dw3[jm], full["b_dw"][jm][None], full["conv_ln_g"][jm][None],
                                   full["conv_ln_b"][jm][None], S)
            sv.update(u=u, cv=cv)
            b_out = full["b_pw2"][jm]
        w_last = g_last.reshape(-1, D)
        w_down_full = g_down.reshape(-1, D)
        sv.update(mix_in=mix_in, w_last=w_last, g_up=g_up, w_down=w_down_full)
        (x1, y1, h2), n_last = carried(_mm("mix_out_fwd", mix_in, w_last, bias=b_out, epi="resid", x=xc, gate=g1,
                                           norm=(W["norm_mlp"][i][None], sc2, sh2), rows_per_batch=S,
                                           payload=nxt[1]), nxt[1])
        sv.update(y1=y1, x1=x1)
        up, n_up = carried(_mm("mlp_up_fwd", h2, g_up, w_form="nslots", epi="relu", payload=nxt[2]), nxt[2])
        norm_next = (W["norm_mix"][i + 1][None], mod[i + 1, :, 1], mod[i + 1, :, 0]) if i + 1 < L else None
        res, n_down = carried(_mm("mlp_down_fwd", up, w_down_full, a_sq=True, epi="resid", x=x1, gate=g2,
                                  norm=norm_next, rows_per_batch=S, payload=nxt[3]), nxt[3])
        x2, y2 = res[0], res[1]
        h1 = res[2] if i + 1 < L else None
        sv.update(h2=h2, up=up, y2=y2)
        saved.append(sv)
        xc = x2
        if i + 1 < L:
            gathered[i + 1] = [n_first[0], n_last[0], n_up[0], n_down[0]]

    dx, loss_blk, dgfin, dy2, dg2 = _loss_head(xc, loss_target.reshape(T, D), W["final_norm"][None],
                                               saved[L - 1]["y2"], mod[L - 1, :, 5], S)
    loss = lax.psum(loss_blk[0, 0], ("x", "y", "c"))

    G = {"final_norm": dgfin.reshape(D)}
    dmod_layers = [None] * L
    small_grads = ("norm_mix", "norm_mlp", "b_qkv", "b_o", "sinks", "b_pw1", "w_dw", "b_dw", "conv_ln_g", "conv_ln_b",
                   "b_pw2")
    acc = {n: [None] * W[n].shape[0] for n in small_grads}
    reduced = {n: [None] * W[n].shape[0] for n in BIG}

    waiting_up = None
    for i in reversed(range(L)):
        jm = i // 2
        sv = saved[i]
        sh1, sc1, g1, sh2, sc2, g2 = (mod[i, :, t] for t in range(N_MOD))
        pay = [(waiting_up[1], False)] if waiting_up else []
        (gd0, gd1), r = carried(_mm_tn("w_down_grad", sv["up"], dy2, a_sq=True, out_form="kslots", halves=True,
                                       tt_pref=4096, payload=pay), pay)
        if waiting_up:
            reduced["w_up"][waiting_up[0]].append(r[0])
        du, r0 = _mm("mlp_down_bwd", dy2, sv["w_down"], w_form="full_t", epi="relu2d", u=sv["up"],
                     payload=[(gd0, False)])
        (gu0, gu1), r1 = _mm_tn("w_up_grad", sv["h2"], du, out_form="nslots", halves=True, tt_pref=4096,
                                payload=[(gd1, False)])
        reduced["w_down"][i] = [r0[0], r1[0]]
        pay = [(gu0, False)] + ([] if i > 0 else [(gu1, False)])
        (dx, dsh2, dsc2, dgn, dy1, dg1, dy1_sum), r = _mm(
            "mlp_up_bwd", du, sv["g_up"], w_form="kslots_t", epi="normbwd", rows_per_batch=S,
            nb=dict(x=sv["x1"], dx_in=dx, gnorm=W["norm_mlp"][i][None], sc=sc2, gate=(sv["y1"], g1)),
            payload=pay)
        reduced["w_up"][i] = list(r)
        waiting_up = (i, gu1) if i > 0 else None
        acc["norm_mlp"][i] = dgn.reshape(D)
        gw_last = _mm_tn("mix_out_grad", sv["mix_in"], dy1, tt_pref=4096)
        gw_last = gw_last.reshape((N_DEV, -1) + gw_last.shape[1:]).astype(BF16)
        below = (saved[i - 1]["y2"], mod[i - 1, :, 5]) if i > 0 else None
        nb_mix = dict(x=sv["x_in"], dx_in=dx, gnorm=W["norm_mix"][i][None], sc=sc1, gate=below)
        if i % 2 == 0:
            acc["b_o"][jm] = dy1_sum.reshape(D)
            do, r = _mm("attn_out_bwd", dy1, sv["w_last"], w_form="full_t", payload=[(gw_last, False)])
            reduced["w_o"][jm] = r[0]
            dq, dkv, dsk = _attn_bwd(sv["qkv"], sv["mix_in"], do, sv["tab"], B, S)
            acc["sinks"][jm] = jnp.sum(dsk[..., 0], axis=0).transpose(0, 2, 1).reshape(N_HEADS)
            dqkv = jnp.concatenate([dq, dkv], axis=1)
            gw_qkv, db_qkv = _mm_tn("w_qkv_grad", sv["h1"], dqkv, colsum=True, tt_pref=4096)
            acc["b_qkv"][jm] = db_qkv.reshape(-1)
            gw_qkv = _to_slots(gw_qkv, 1).astype(BF16)
            res, r = _mm("qkv_bwd", dqkv, sv["w_first"], w_form="full_t", epi="normbwd", nb=nb_mix, rows_per_batch=S,
                         payload=[(gw_qkv, False)])
            reduced["w_qkv"][jm] = r[0]
        else:
            acc["b_pw2"][jm] = dy1_sum.reshape(D)
            (dcv, dlg, dlb, dbdw), r = _mm(
                "pw2_bwd", dy1, sv["w_last"], w_form="full_t", epi="lnbwd",
                ln=(sv["cv"], full["conv_ln_g"][jm][None], full["conv_ln_b"][jm][None]), payload=[(gw_last, False)])
            reduced["w_pw2"][jm] = r[0]
            acc["conv_ln_g"][jm], acc["conv_ln_b"][jm], acc["b_dw"][jm] = dlg.reshape(-1), dlb.reshape(-1), dbdw.reshape(-1)
            du1, dwdw = _conv_bwd(dcv, sv["u"], w_dw3[jm], S)
            acc["w_dw"][jm] = dwdw.transpose(1, 0, 2).reshape(CONV_HALO, D)[:CONV_WIDTH]
            gw_pw1, db_pw1 = _mm_tn("w_pw1_grad", sv["h1"], du1, colsum=True, tt_pref=4096)
            gw_pw1 = _to_slots(gw_pw1, 1).astype(BF16)
            acc["b_pw1"][jm] = db_pw1.reshape(-1)
            res, r = _mm("pw1_bwd", du1, sv["w_first"], w_form="kslots_t", epi="normbwd", nb=nb_mix, rows_per_batch=S,
                         payload=[(gw_pw1, False)])
            reduced["w_pw1"][jm] = r[0]
        dx, dsh1, dsc1, dgn = res[:4]
        acc["norm_mix"][i] = dgn.reshape(D)
        dmod_layers[i] = jnp.concatenate([dsh1, dsc1, dg1, dsh2, dsc2, dg2], axis=1).reshape(B, N_MOD * D)
        if i > 0:
            dy2, dg2 = res[4], res[5]
    grad_x = dx.reshape(B, S, D)
    for n, parts in acc.items():
        G[n] = jnp.stack(parts)

    dmod = jnp.stack(dmod_layers)
    dmod_slots = dmod.reshape(L, B, N_DEV, n_mod).transpose(2, 0, 1, 3).reshape(N_DEV, L * B, n_mod)
    dmod_recv = _exchange("gather_dmod", [(dmod_slots, False)])[0]
    dmod_all = dmod_recv.reshape(N_DEV, L, B, n_mod).transpose(1, 0, 2, 3).reshape(L, N_DEV * B, n_mod)
    g_w_mod, db_mod_mine = _mod_bwd(c_all, dmod_all)
    G["b_mod"] = lax.dynamic_update_slice_in_dim(jnp.zeros_like(W["b_mod"]), db_mod_mine.reshape(L, n_mod),
                                                 me * n_mod, axis=1)

    small_items = [jnp.broadcast_to(G[n][None], (N_DEV,) + G[n].shape) for n in SMALL_REPL]
    small_items += [_to_slots(G[n], d) for n, d in SMALL_SHARDED]
    small_slots, small_sizes2 = _pack(small_items, F32, 1)
    small_recv = _exchange("reduce_small", [(small_slots, False)])[0]

    out = {}

    def run_adam(name, parts, names):
        shapes = [W[n].shape for n in names]
        wp, sizes = _pack([W[n] for n in names], F32, 0)
        mp, _ = _pack([M[n] for n in names], F32, 0)
        vp, _ = _pack([V[n] for n in names], F32, 0)
        res = _adam(name, [parts], wp, mp, vp)
        for kind, buf in zip(("grad", "delta", "new_m", "new_v"), res):
            for n, a in zip(names, _unpack(buf, sizes, shapes, 0)):
                out[kind + "_" + n] = a

    for n in BIG + ("w_mod",):
        cols = W[n].shape[-1]
        if n == "w_mod":
            parts = [g_w_mod.reshape(1, -1, cols)]
        else:
            pieces = [p for r in reduced[n] for p in (r if isinstance(r, list) else [r])]
            parts = [p.reshape(N_DEV, -1, cols) for p in pieces]
        res = _adam("adam_" + n, parts, W[n].reshape(-1, cols), M[n].reshape(-1, cols), V[n].reshape(-1, cols))
        for kind, buf in zip(("grad", "delta", "new_m", "new_v"), res):
            out[kind + "_" + n] = buf.reshape(W[n].shape)
    run_adam("adam_small", small_recv, list(SMALL_REPL) + [n for n, _ in SMALL_SHARDED])

    res = [loss, grad_x]
    for kind in ("grad", "delta", "new_m", "new_v"):
        res += [out[kind + "_" + n] for n in WEIGHTS]
    return tuple(res)


def kernel(x, c, w_mod, b_mod, norm_mix, norm_mlp, w_qkv, b_qkv, w_o, b_o, sinks, w_pw1, b_pw1, w_dw, b_dw, conv_ln_g, conv_ln_b, w_pw2, b_pw2, w_up, w_down, final_norm, loss_target, m_w_mod, m_b_mod, m_norm_mix, m_norm_mlp, m_w_qkv, m_b_qkv, m_w_o, m_b_o, m_sinks, m_w_pw1, m_b_pw1, m_w_dw, m_b_dw, m_conv_ln_g, m_conv_ln_b, m_w_pw2, m_b_pw2, m_w_up, m_w_down, m_final_norm, v_w_mod, v_b_mod, v_norm_mix, v_norm_mlp, v_w_qkv, v_b_qkv, v_w_o, v_b_o, v_sinks, v_w_pw1, v_b_pw1, v_w_dw, v_b_dw, v_conv_ln_g, v_conv_ln_b, v_w_pw2, v_b_pw2, v_w_up, v_w_down, v_final_norm):
    W = dict(w_mod=w_mod, b_mod=b_mod, norm_mix=norm_mix, norm_mlp=norm_mlp, w_qkv=w_qkv, b_qkv=b_qkv, w_o=w_o,
             b_o=b_o, sinks=sinks, w_pw1=w_pw1, b_pw1=b_pw1, w_dw=w_dw, b_dw=b_dw, conv_ln_g=conv_ln_g,
             conv_ln_b=conv_ln_b, w_pw2=w_pw2, b_pw2=b_pw2, w_up=w_up, w_down=w_down, final_norm=final_norm)
    M = dict(w_mod=m_w_mod, b_mod=m_b_mod, norm_mix=m_norm_mix, norm_mlp=m_norm_mlp, w_qkv=m_w_qkv, b_qkv=m_b_qkv,
             w_o=m_w_o, b_o=m_b_o, sinks=m_sinks, w_pw1=m_w_pw1, b_pw1=m_b_pw1, w_dw=m_w_dw, b_dw=m_b_dw,
             conv_ln_g=m_conv_ln_g, conv_ln_b=m_conv_ln_b, w_pw2=m_w_pw2, b_pw2=m_b_pw2, w_up=m_w_up,
             w_down=m_w_down, final_norm=m_final_norm)
    V = dict(w_mod=v_w_mod, b_mod=v_b_mod, norm_mix=v_norm_mix, norm_mlp=v_norm_mlp, w_qkv=v_w_qkv, b_qkv=v_b_qkv,
             w_o=v_w_o, b_o=v_b_o, sinks=v_sinks, w_pw1=v_w_pw1, b_pw1=v_b_pw1, w_dw=v_w_dw, b_dw=v_b_dw,
             conv_ln_g=v_conv_ln_g, conv_ln_b=v_conv_ln_b, w_pw2=v_w_pw2, b_pw2=v_b_pw2, w_up=v_w_up,
             w_down=v_w_down, final_norm=v_final_norm)
    return _step(x, c, loss_target, W, M, V)
```

```python
import numpy as np
import jax
import jax.numpy as jnp
from jax import lax
from jax.experimental import pallas as pl
from jax.experimental.pallas import tpu as pltpu

F32 = jnp.float32
BF16 = jnp.bfloat16

N_DEV = 8
N_HEADS = 16
N_KV_HEADS = 2
HEAD_DIM = 64
GROUP = N_HEADS // N_KV_HEADS
ATT_BLOCK = 128
CONV_WIDTH = 31
CONV_HALO = 32
N_MOD = 6
EPS = 1e-6
ADAM_LR = 0.001
ADAM_B1 = 0.9
ADAM_B2 = 0.999
ADAM_EPS = 1e-08
ADAM_WD = 0.01
ADAM_STEP = 10
NEG_BIG = -1e30
PACK_COLS = 1024
VMEM_LIMIT_BYTES = 56 * 1024 * 1024
MESH_ID = pl.DeviceIdType.MESH


def _params(*sem):
    return pltpu.CompilerParams(dimension_semantics=sem, vmem_limit_bytes=VMEM_LIMIT_BYTES)


def _pick(n, pref, mult=8):
    if n <= pref:
        return n
    for t in range(pref, 0, -1):
        if n % t == 0 and t % mult == 0:
            return t
    return n


def _sigmoid(z):
    return 0.5 * jnp.tanh(0.5 * z) + 0.5


def _payload_layout(payload):
    n = len(payload)
    out_shapes = [jax.ShapeDtypeStruct((N_DEV,) + tuple(a.shape if bc else a.shape[1:]), a.dtype) for a, bc in payload]
    hbm = pl.BlockSpec(memory_space=pl.ANY)
    scratch = [pltpu.SemaphoreType.DMA((n * (N_DEV - 1),)), pltpu.SemaphoreType.DMA((n * (N_DEV - 1),)),
               pltpu.SemaphoreType.DMA((n,))]
    return [a for a, _ in payload], [hbm] * n, out_shapes, [hbm] * n, scratch


class _Plan:
    def __init__(self, bcasts, src_refs, dst_refs, send_sems, recv_sems, local_sems):
        x, y, c = lax.axis_index("x"), lax.axis_index("y"), lax.axis_index("c")
        me = 4 * x + 2 * y + c
        self.first, self.landed, self.relay, self.local = [], [], [], []
        for t, (bc, s_ref, d_ref) in enumerate(zip(bcasts, src_refs, dst_refs)):
            def remote(k, src, slot, to):
                sem = t * (N_DEV - 1) + k
                return pltpu.make_async_remote_copy(src_ref=src, dst_ref=d_ref.at[slot], send_sem=send_sems.at[sem],
                                                    recv_sem=recv_sems.at[sem], device_id=to, device_id_type=MESH_ID)
            if bc:
                chips = [(1 - x, y), (x, 1 - y), (1 - x, 1 - y)]
                self.first.append(remote(0, s_ref, me, (x, y, 1 - c)))
                for j, (px, py) in enumerate(chips):
                    cp = remote(1 + j, s_ref, me, (px, py, c))
                    self.first.append(cp)
                    self.landed.append(cp)
                    theirs = 4 * px + 2 * py + c
                    self.relay.append(remote(4 + j, d_ref.at[theirs], theirs, (x, y, 1 - c)))
                self.local.append(pltpu.make_async_copy(s_ref, d_ref.at[me], local_sems.at[t]))
            else:
                for k in range(1, N_DEV):
                    px = 1 - x if (k >> 2) & 1 else x
                    py = 1 - y if (k >> 1) & 1 else y
                    pc = 1 - c if k & 1 else c
                    self.first.append(remote(k - 1, s_ref.at[4 * px + 2 * py + pc], me, (px, py, pc)))
                self.local.append(pltpu.make_async_copy(s_ref.at[me], d_ref.at[me], local_sems.at[t]))

    def start(self):
        for cp in self.first + self.local:
            cp.start()

    def pass_on(self):
        for cp in self.landed:
            cp.wait_recv()
        for cp in self.relay:
            cp.start()

    def finish(self):
        for cp in self.first:
            cp.wait_send()
            if not any(cp is l for l in self.landed):
                cp.wait_recv()
        for cp in self.relay:
            cp.wait()
        for cp in self.local:
            cp.wait()


def _exchange(name, payload):
    n = len(payload)
    bcasts = [bc for _, bc in payload]
    arrays, in_specs, out_shapes, out_specs, scratch = _payload_layout(payload)

    def body(*refs):
        plan = _Plan(bcasts, refs[:n], refs[n:2 * n], *refs[2 * n:])
        plan.start()
        plan.pass_on()
        plan.finish()

    return pl.pallas_call(
        body,
        name=name,
        out_shape=tuple(out_shapes),
        in_specs=in_specs,
        out_specs=tuple(out_specs),
        scratch_shapes=scratch,
    )(*arrays)


def _pack(arrays, dtype, lead):
    lead_shape = arrays[0].shape[:lead]
    flat = [a.astype(dtype).reshape(lead_shape + (-1,)) for a in arrays]
    sizes = [f.shape[-1] for f in flat]
    total = sum(sizes)
    chunk = 16 * PACK_COLS
    padded = -(-total // chunk) * chunk
    if padded > total:
        flat.append(jnp.zeros(lead_shape + (padded - total,), dtype))
    buf = jnp.concatenate(flat, axis=-1)
    return buf.reshape(lead_shape + (padded // PACK_COLS, PACK_COLS)), sizes


def _unpack(buf, sizes, shapes, lead):
    lead_shape = buf.shape[:lead]
    flat = buf.reshape(lead_shape + (-1,))
    out, off = [], 0
    for n, shp in zip(sizes, shapes):
        out.append(lax.slice_in_dim(flat, off, off + n, axis=lead).reshape(lead_shape + tuple(shp)))
        off += n
    return out


def _to_slots(a, dim):
    shp = a.shape
    a = a.reshape(shp[:dim] + (N_DEV, shp[dim] // N_DEV) + shp[dim + 1:])
    return jnp.moveaxis(a, dim, 0)


def _from_slots(a, dim):
    a = jnp.moveaxis(a, 0, dim)
    shp = a.shape
    return a.reshape(shp[:dim] + (shp[dim] * shp[dim + 1],) + shp[dim + 2:])


def _mod_fwd(c_all, w, b):
    L, D, n = w.shape
    M = c_all.shape[0]

    def body(c_ref, w_ref, b_ref, o_ref):
        cv = c_ref[...]
        cs = (cv * _sigmoid(cv)).astype(BF16)
        o_ref[...] = jnp.dot(cs, w_ref[...], preferred_element_type=F32) + b_ref[...]

    return pl.pallas_call(
        body,
        name="mod_fwd",
        grid=(L,),
        out_shape=jax.ShapeDtypeStruct((L, M, n), F32),
        in_specs=[
            pl.BlockSpec((M, D), lambda l: (0, 0)),
            pl.BlockSpec((None, D, n), lambda l: (l, 0, 0)),
            pl.BlockSpec((None, 1, n), lambda l: (l, 0, 0)),
        ],
        out_specs=pl.BlockSpec((None, M, n), lambda l: (l, 0, 0)),
        compiler_params=_params("arbitrary"),
    )(c_all, w, b)


def _mod_bwd(c_all, dmod_all):
    L, M, n = dmod_all.shape
    D = c_all.shape[1]

    def body(c_ref, d_ref, dw_ref, db_ref):
        cv = c_ref[...]
        cs = (cv * _sigmoid(cv)).astype(BF16)
        d = d_ref[...]
        dw_ref[...] = lax.dot_general(cs, d.astype(BF16), (((0,), (0,)), ((), ())), preferred_element_type=F32)
        db_ref[...] = jnp.sum(d, axis=0, keepdims=True)

    return pl.pallas_call(
        body,
        name="mod_bwd",
        grid=(L,),
        out_shape=(jax.ShapeDtypeStruct((L, D, n), F32), jax.ShapeDtypeStruct((L, 1, n), F32)),
        in_specs=[
            pl.BlockSpec((M, D), lambda l: (0, 0)),
            pl.BlockSpec((None, M, n), lambda l: (l, 0, 0)),
        ],
        out_specs=(
            pl.BlockSpec((None, D, n), lambda l: (l, 0, 0)),
            pl.BlockSpec((None, 1, n), lambda l: (l, 0, 0)),
        ),
        compiler_params=_params("arbitrary"),
    )(c_all, dmod_all)


def _mm(name, a, w, *, w_form="full", out_dtype=BF16, bias=None, a_sq=False, epi=None, x=None, gate=None, u=None,
        norm=None, nb=None, ln=None, rows_per_batch=None, tm_pref=512, payload=()):
    M, K = a.shape
    if w_form == "full":
        N = w.shape[1]
        nc = _pick(N, 1024, 128)
    elif w_form == "full_t":
        N = w.shape[0]
        nc = _pick(N, 1024, 128)
    elif w_form == "nslots":
        nc = w.shape[2]
        N = N_DEV * nc
    else:
        N = w.shape[1]
        nc = N
    ks = K // N_DEV
    n_chunks = N // nc
    tm = _pick(M if rows_per_batch is None else rows_per_batch, tm_pref, 16)
    steps = M // tm
    relay_step = (3 * steps) // 4
    nb_gate = nb is not None and nb.get("gate") is not None
    tpb_nb = rows_per_batch // tm if epi == "normbwd" else 1
    has_bias = bias is not None
    n_pay = len(payload)
    bcasts = [bc for _, bc in payload]
    nt = (((1,), (1,)), ((), ()))


    def body(*refs):
        it = iter(refs)
        a_ref = next(it)
        w_ref = next(it)
        b_ref = next(it) if has_bias else None
        x_ref = next(it) if epi == "resid" else None
        g_ref = next(it) if epi == "resid" else None
        gn_ref, sc_ref, sh_ref = (next(it), next(it), next(it)) if norm is not None else (None, None, None)
        u_ref = next(it) if epi == "relu2d" else None
        if epi == "normbwd":
            nx_ref, ndx_ref, ngn_ref, nsc_ref = next(it), next(it), next(it), next(it)
            ny_ref, ngt_ref = (next(it), next(it)) if nb_gate else (None, None)
        if epi == "lnbwd":
            cv_ref, lg_ref, lb_ref = next(it), next(it), next(it)
        pay_src = [next(it) for _ in range(n_pay)]
        o_ref = next(it)
        y_ref = next(it) if epi == "resid" else None
        h_ref = next(it) if norm is not None else None
        if epi == "normbwd":
            dsh_ref, dsc_ref, dgn_ref = next(it), next(it), next(it)
            dy_ref, dgate_ref, cs_ref = (next(it), next(it), next(it)) if nb_gate else (None, None, None)
        if epi == "lnbwd":
            dlg_ref, dlb_ref, dcs_ref = next(it), next(it), next(it)
        pay_dst = [next(it) for _ in range(n_pay)]
        sems = list(it)

        if epi == "lnbwd":
            @pl.when(pl.program_id(0) == 0)
            def _():
                dlg_ref[...] = jnp.zeros_like(dlg_ref)
                dlb_ref[...] = jnp.zeros_like(dlb_ref)
                dcs_ref[...] = jnp.zeros_like(dcs_ref)

        if epi == "normbwd":
            @pl.when(pl.program_id(0) % tpb_nb == 0)
            def _():
                dsh_ref[...] = jnp.zeros_like(dsh_ref)
                dsc_ref[...] = jnp.zeros_like(dsc_ref)
                if nb_gate:
                    dgate_ref[...] = jnp.zeros_like(dgate_ref)

            @pl.when(pl.program_id(0) == 0)
            def _():
                dgn_ref[...] = jnp.zeros_like(dgn_ref)
                if nb_gate:
                    cs_ref[...] = jnp.zeros_like(cs_ref)

        if n_pay:
            @pl.when(pl.program_id(0) == 0)
            def _():
                _Plan(bcasts, pay_src, pay_dst, *sems).start()

            @pl.when(pl.program_id(0) == relay_step)
            def _():
                _Plan(bcasts, pay_src, pay_dst, *sems).pass_on()

        av = None if w_form == "kslots_t" else a_ref[...]
        if a_sq:
            av = av * av
        for ci in range(n_chunks):
            cols = slice(ci * nc, (ci + 1) * nc)
            if w_form == "full":
                acc = jnp.dot(av, w_ref[:, cols], preferred_element_type=F32)
            elif w_form == "full_t":
                acc = lax.dot_general(av, w_ref[cols, :], nt, preferred_element_type=F32)
            elif w_form == "nslots":
                acc = jnp.dot(av, w_ref[ci], preferred_element_type=F32)
            else:
                acc = lax.dot_general(a_ref[:, 0:ks], w_ref[0], nt, preferred_element_type=F32)
                for j in range(1, N_DEV):
                    acc = acc + lax.dot_general(a_ref[:, j * ks:(j + 1) * ks], w_ref[j], nt,
                                                preferred_element_type=F32)
            if has_bias:
                acc = acc + b_ref[:, cols]
            if epi == "resid":
                xn = x_ref[:, cols] + g_ref[:, cols] * acc
                o_ref[:, cols] = xn
                y_ref[:, cols] = acc.astype(BF16)
                if norm is not None:
                    r = lax.rsqrt(jnp.mean(xn * xn, axis=-1, keepdims=True) + EPS)
                    h_ref[...] = ((xn * r * gn_ref[...]) * (1.0 + sc_ref[...]) + sh_ref[...]).astype(BF16)
            elif epi == "relu":
                o_ref[:, cols] = jnp.maximum(acc, 0.0).astype(out_dtype)
            elif epi == "relu2d":
                o_ref[:, cols] = (acc * (2.0 * u_ref[:, cols].astype(F32))).astype(out_dtype)
            elif epi == "normbwd":
                xv = nx_ref[...]
                gn = ngn_ref[...]
                r = lax.rsqrt(jnp.mean(xv * xv, axis=-1, keepdims=True) + EPS)
                xhat = xv * r
                dsh_ref[...] += jnp.sum(acc, axis=0, keepdims=True)
                dsc_ref[...] += jnp.sum(acc * (xhat * gn), axis=0, keepdims=True)
                dn = acc * (1.0 + nsc_ref[...])
                dgn_ref[...] += jnp.sum(dn * xhat, axis=0, keepdims=True)
                dxhat = dn * gn
                dx = ndx_ref[...] + r * (dxhat - xhat * jnp.mean(dxhat * xhat, axis=-1, keepdims=True))
                o_ref[...] = dx
                if nb_gate:
                    dy = dx * ngt_ref[...]
                    dy_ref[...] = dy.astype(BF16)
                    dgate_ref[...] += jnp.sum(dx * ny_ref[...].astype(F32), axis=0, keepdims=True)
                    cs_ref[...] += jnp.sum(dy, axis=0, keepdims=True)
            elif epi == "lnbwd":
                cv = cv_ref[...]
                lg = lg_ref[...]
                xc = cv - jnp.mean(cv, axis=-1, keepdims=True)
                rstd = lax.rsqrt(jnp.mean(xc * xc, axis=-1, keepdims=True) + EPS)
                xhat = xc * rstd
                ln = xhat * lg + lb_ref[...]
                sg = _sigmoid(ln)
                dln = acc * (sg * (1.0 + ln * (1.0 - sg)))
                dlg_ref[...] += jnp.sum(dln * xhat, axis=0, keepdims=True)
                dlb_ref[...] += jnp.sum(dln, axis=0, keepdims=True)
                dxhat = dln * lg
                dcv = rstd * (dxhat - jnp.mean(dxhat, axis=-1, keepdims=True)
                              - xhat * jnp.mean(dxhat * xhat, axis=-1, keepdims=True))
                o_ref[...] = dcv
                dcs_ref[...] += jnp.sum(dcv, axis=0, keepdims=True)
            else:
                o_ref[:, cols] = acc.astype(out_dtype)

        if n_pay:
            @pl.when(pl.program_id(0) == steps - 1)
            def _():
                _Plan(bcasts, pay_src, pay_dst, *sems).finish()

    args = [a, w]
    w_block = w.shape
    specs = [pl.BlockSpec((tm, K), lambda i: (i, 0)), pl.BlockSpec(w_block, lambda i: (0,) * len(w_block))]
    if has_bias:
        args.append(bias.reshape(1, N).astype(F32))
        specs.append(pl.BlockSpec((1, N), lambda i: (0, 0)))
    row_spec = pl.BlockSpec((tm, N), lambda i: (i, 0))
    if epi == "resid":
        tpb = rows_per_batch // tm
        vec_spec = pl.BlockSpec((None, 1, N), lambda i: (i // tpb, 0, 0))
        args += [x, gate]
        specs += [row_spec, vec_spec]
        out_shape = [jax.ShapeDtypeStruct((M, N), F32), jax.ShapeDtypeStruct((M, N), BF16)]
        out_specs = [row_spec, row_spec]
        if norm is not None:
            assert n_chunks == 1
            args += list(norm)
            specs += [pl.BlockSpec((1, N), lambda i: (0, 0)), vec_spec, vec_spec]
            out_shape.append(jax.ShapeDtypeStruct((M, N), BF16))
            out_specs.append(row_spec)
    elif epi == "normbwd":
        assert n_chunks == 1
        vec_spec = pl.BlockSpec((None, 1, N), lambda i: (i // tpb_nb, 0, 0))
        one_spec = pl.BlockSpec((1, N), lambda i: (0, 0))
        nbat = M // rows_per_batch
        f_vec, f_one = jax.ShapeDtypeStruct((nbat, 1, N), F32), jax.ShapeDtypeStruct((1, N), F32)
        args += [nb["x"], nb["dx_in"], nb["gnorm"], nb["sc"]]
        specs += [row_spec, row_spec, one_spec, vec_spec]
        out_shape = [jax.ShapeDtypeStruct((M, N), F32), f_vec, f_vec, f_one]
        out_specs = [row_spec, vec_spec, vec_spec, one_spec]
        if nb_gate:
            args += list(nb["gate"])
            specs += [row_spec, vec_spec]
            out_shape += [jax.ShapeDtypeStruct((M, N), BF16), f_vec, f_one]
            out_specs += [row_spec, vec_spec, one_spec]
    elif epi == "lnbwd":
        assert n_chunks == 1
        one_spec = pl.BlockSpec((1, N), lambda i: (0, 0))
        f_one = jax.ShapeDtypeStruct((1, N), F32)
        args += list(ln)
        specs += [row_spec, one_spec, one_spec]
        out_shape = [jax.ShapeDtypeStruct((M, N), F32), f_one, f_one, f_one]
        out_specs = [row_spec, one_spec, one_spec, one_spec]
    else:
        if epi == "relu2d":
            args.append(u)
            specs.append(row_spec)
        out_shape = [jax.ShapeDtypeStruct((M, N), out_dtype)]
        out_specs = [row_spec]
    scratch = []
    if n_pay:
        p_arrays, p_in, p_shapes, p_out, scratch = _payload_layout(payload)
        args += p_arrays
        specs += p_in
        out_shape += p_shapes
        out_specs += p_out
    res = pl.pallas_call(
        body,
        name=name,
        grid=(steps,),
        out_shape=tuple(out_shape),
        in_specs=specs,
        out_specs=tuple(out_specs),
        scratch_shapes=scratch,
        compiler_params=_params("arbitrary" if (n_pay or epi in ("normbwd", "lnbwd")) else "parallel"),
    )(*args)
    if epi == "resid":
        n_own = 3 if norm is not None else 2
    elif epi == "normbwd":
        n_own = 7 if nb_gate else 4
    elif epi == "lnbwd":
        n_own = 4
    else:
        n_own = 1
    own = res[0] if n_own == 1 else tuple(res[:n_own])
    return (own, list(res[n_own:])) if n_pay else own


def _mm_tn(name, a, b, *, a_sq=False, out_form="full", halves=False, colsum=False, tt_pref=1024, tk_pref=1024,
           tn_pref=1024, payload=()):
    T, K = a.shape
    N = b.shape[1]
    tt = _pick(T, tt_pref, 16)
    tk = K // N_DEV if out_form == "kslots" else _pick(K, tk_pref, 128)
    tn = N // N_DEV if out_form == "nslots" else _pick(N, tn_pref, 128)
    nt_steps = T // tt
    grid = (K // tk, N // tn, nt_steps)
    slots = out_form != "full"
    assert not colsum or K == tk
    n_pay = len(payload)
    bcasts = [bc for _, bc in payload]

    def body(*refs):
        it = iter(refs)
        a_ref, b_ref = next(it), next(it)
        pay_src = [next(it) for _ in range(n_pay)]
        o_ref = next(it)
        o2_ref = next(it) if halves else None
        cs_ref = next(it) if colsum else None
        pay_dst = [next(it) for _ in range(n_pay)]
        acc = next(it) if slots else o_ref
        sems = list(it)
        step = (pl.program_id(0) * grid[1] + pl.program_id(1)) * grid[2] + pl.program_id(2)

        if n_pay:
            @pl.when(step == 0)
            def _():
                _Plan(bcasts, pay_src, pay_dst, *sems).start()

        @pl.when(pl.program_id(2) == 0)
        def _():
            acc[...] = jnp.zeros_like(acc)
            if colsum:
                cs_ref[...] = jnp.zeros_like(cs_ref)

        bv = b_ref[...]
        av = a_ref[...]
        if a_sq:
            av = av * av
        acc[...] += lax.dot_general(av, bv, (((0,), (0,)), ((), ())), preferred_element_type=F32)
        if colsum:
            cs_ref[...] += jnp.sum(bv.astype(F32), axis=0, keepdims=True)

        if slots:
            @pl.when(pl.program_id(2) == nt_steps - 1)
            def _():
                if halves:
                    o_ref[...] = acc[0:tk // 2, :].astype(o_ref.dtype)
                    o2_ref[...] = acc[tk // 2:, :].astype(o2_ref.dtype)
                else:
                    o_ref[...] = acc[...].astype(o_ref.dtype)

        if n_pay:
            @pl.when(step == grid[0] * grid[1] * grid[2] - 1)
            def _():
                plan = _Plan(bcasts, pay_src, pay_dst, *sems)
                plan.pass_on()
                plan.finish()

    if out_form == "full":
        out_shape = [jax.ShapeDtypeStruct((K, N), F32)]
        out_specs = [pl.BlockSpec((tk, tn), lambda k, n, t: (k, n))]
    else:
        assert not halves or out_form == "kslots" or K == tk
        rows = tk // 2 if halves else tk
        if out_form == "nslots":
            shape, spec = (N_DEV, K // tk * rows, tn), pl.BlockSpec((None, rows, tn), lambda k, n, t: (n, k, 0))
        else:
            shape, spec = (N_DEV, rows, N), pl.BlockSpec((None, rows, tn), lambda k, n, t: (k, 0, n))
        out_shape = [jax.ShapeDtypeStruct(shape, BF16)] * (2 if halves else 1)
        out_specs = [spec] * (2 if halves else 1)
    if colsum:
        out_shape.append(jax.ShapeDtypeStruct((1, N), F32))
        out_specs.append(pl.BlockSpec((1, tn), lambda k, n, t: (0, n)))
    args = [a, b]
    in_specs = [pl.BlockSpec((tt, tk), lambda k, n, t: (t, k)), pl.BlockSpec((tt, tn), lambda k, n, t: (t, n))]
    scratch = [pltpu.VMEM((tk, tn), F32)] if slots else []
    n_own = len(out_shape)
    if n_pay:
        p_arrays, p_in, p_shapes, p_out, p_scratch = _payload_layout(payload)
        args += p_arrays
        in_specs += p_in
        out_shape += p_shapes
        out_specs += p_out
        scratch += p_scratch
    res = pl.pallas_call(
        body,
        name=name,
        grid=grid,
        out_shape=tuple(out_shape),
        in_specs=in_specs,
        out_specs=tuple(out_specs),
        scratch_shapes=scratch,
        compiler_params=_params(*(("arbitrary",) * 3 if n_pay else ("parallel", "parallel", "arbitrary"))),
    )(*args)
    own = res[0] if n_own == 1 else tuple(res[:n_own])
    return (own, list(res[n_own:])) if n_pay else own


def _normmod_fwd(name, x, gnorm, sc, sh, S, tm_pref=512):
    T, D = x.shape
    tm = _pick(S, tm_pref, 16)
    tpb = S // tm

    def body(x_ref, g_ref, sc_ref, sh_ref, o_ref):
        xv = x_ref[...]
        r = lax.rsqrt(jnp.mean(xv * xv, axis=-1, keepdims=True) + EPS)
        n = xv * r * g_ref[...]
        o_ref[...] = (n * (1.0 + sc_ref[...]) + sh_ref[...]).astype(BF16)

    row = pl.BlockSpec((tm, D), lambda i: (i, 0))
    vec = pl.BlockSpec((None, 1, D), lambda i: (i // tpb, 0, 0))
    return pl.pallas_call(
        body,
        name=name,
        grid=(T // tm,),
        out_shape=jax.ShapeDtypeStruct((T, D), BF16),
        in_specs=[row, pl.BlockSpec((1, D), lambda i: (0, 0)), vec, vec],
        out_specs=row,
        compiler_params=_params("parallel"),
    )(x, gnorm, sc, sh)


ATT_ROWS = GROUP * ATT_BLOCK
ATT_SPAN = 2 * ATT_BLOCK
ATT_SCALE = HEAD_DIM ** -0.5


def _attn_table(sinks):
    slopes = jnp.asarray(np.array([2.0 ** (-8.0 * (h + 1) / N_HEADS) for h in range(N_HEADS)], np.float32))
    r = jnp.arange(ATT_BLOCK)[:, None]
    cc = jnp.arange(ATT_SPAN)[None, :]
    dist = r + ATT_BLOCK - cc
    ok = (dist >= 0) & (dist < ATT_BLOCK)
    tab = jnp.where(ok[None], -slopes[:, None, None] * dist.astype(F32)[None], NEG_BIG)
    tab = jnp.where((cc == 0)[None], sinks.astype(F32)[:, None, None], tab)
    return tab.reshape(N_KV_HEADS, ATT_ROWS, ATT_SPAN)


KV_COLS = N_KV_HEADS * HEAD_DIM


PAIR = 2 * HEAD_DIM


def _pair_tables(tab):
    hkv, rows, span = tab.shape
    pairs = rows // (2 * ATT_BLOCK)
    return tab.reshape(hkv, pairs, 2, ATT_BLOCK, span).transpose(0, 2, 1, 3, 4).reshape(hkv, 2, pairs * ATT_BLOCK, span)


def _stack_pairs(ref, pairs, col0):
    return jnp.concatenate([ref[:, col0 + j * PAIR:col0 + (j + 1) * PAIR] for j in range(pairs)], axis=0)


def _unstack_pairs(v, pairs):
    return jnp.concatenate([v[j * ATT_BLOCK:(j + 1) * ATT_BLOCK, :] for j in range(pairs)], axis=1)


def _swap_halves(x):
    return jnp.concatenate([x[:, HEAD_DIM:], x[:, :HEAD_DIM]], axis=1)


def _load_span(prev_ref, cur_ref, buf_e, buf_o, kv, mult):
    lane = lax.broadcasted_iota(jnp.int32, (ATT_BLOCK, PAIR), 1)
    row = lax.broadcasted_iota(jnp.int32, (ATT_BLOCK, PAIR), 0)
    own = lane // HEAD_DIM == kv
    for r0, ref, first in ((0, prev_ref, True), (ATT_BLOCK, cur_ref, False)):
        v = ref[...]
        if mult != 1.0:
            v = v * mult
        zero = jnp.zeros_like(v)
        mine = jnp.where(own, v, zero)
        if first:
            mine = jnp.where(row > 0, mine, zero)
        other = _swap_halves(mine)
        buf_e[r0:r0 + ATT_BLOCK, :] = mine if kv == 0 else other
        buf_o[r0:r0 + ATT_BLOCK, :] = other if kv == 0 else mine


def _first_block_penalty(i):
    col = lax.broadcasted_iota(jnp.int32, (1, ATT_SPAN), 1)
    return jnp.where((col < ATT_BLOCK) & (col > 0), jnp.where(i > 0, 0.0, NEG_BIG), 0.0).astype(F32)


def _attn_probs(qp, kbuf, bias, first_pen):
    nt = (((1,), (1,)), ((), ()))
    s = lax.dot_general(qp, kbuf, nt, preferred_element_type=F32) + bias + first_pen
    m = jnp.max(s, axis=-1, keepdims=True)
    e = jnp.exp(s - m)
    return e * (1.0 / jnp.sum(e, axis=-1, keepdims=True))


def _attn_specs(nb, q_dim, pairs, row_q, row_cur, row_prev):
    kblk = q_dim // KV_COLS
    qspec = pl.BlockSpec((ATT_BLOCK, q_dim), lambda b, i: (row_q(b, i), 0))
    kv_specs = [pl.BlockSpec((ATT_BLOCK, KV_COLS), (lambda b, i, r=r, c=c: (r(b, i), c)))
                for c in (kblk, kblk + 1) for r in (row_cur, row_prev)]
    tspec = pl.BlockSpec((N_KV_HEADS, 2, pairs * ATT_BLOCK, ATT_SPAN), lambda b, i: (0, 0, 0, 0))
    return qspec, kv_specs, tspec


def _attn_fwd(qkv, tab2, B, S, payload=()):
    assert N_KV_HEADS == 2 and GROUP % 2 == 0
    T = qkv.shape[0]
    pairs = GROUP // 2
    gw = pairs * PAIR
    nb = S // ATT_BLOCK
    q_dim = N_HEADS * HEAD_DIM
    n_pay = len(payload)
    bcasts = [bc for _, bc in payload]
    steps = B * nb

    def body(*refs):
        q_ref, kc_ref, kp_ref, vc_ref, vp_ref, tab_ref = refs[:6]
        pay_src = refs[6:6 + n_pay]
        o_ref = refs[6 + n_pay]
        pay_dst = refs[7 + n_pay:7 + 2 * n_pay]
        ke, ko, ve, vo = refs[7 + 2 * n_pay:11 + 2 * n_pay]
        sems = refs[11 + 2 * n_pay:]
        i = pl.program_id(1)
        if n_pay:
            step = pl.program_id(0) * nb + i

            @pl.when(step == 0)
            def _():
                _Plan(bcasts, pay_src, pay_dst, *sems).start()

            @pl.when(step == (3 * steps) // 4)
            def _():
                _Plan(bcasts, pay_src, pay_dst, *sems).pass_on()

            @pl.when(step == steps - 1)
            def _():
                _Plan(bcasts, pay_src, pay_dst, *sems).finish()

        pen = _first_block_penalty(i)
        for kv in range(N_KV_HEADS):
            _load_span(kp_ref, kc_ref, ke.at[kv], ko.at[kv], kv, ATT_SCALE)
            _load_span(vp_ref, vc_ref, ve.at[kv], vo.at[kv], kv, 1.0)
            qp = _stack_pairs(q_ref, pairs, kv * gw)
            p_e = _attn_probs(qp, ke[kv], tab_ref[kv, 0], pen)
            p_o = _attn_probs(qp, ko[kv], tab_ref[kv, 1], pen)
            o = jnp.dot(p_e.astype(BF16), ve[kv], preferred_element_type=F32)
            o = o + jnp.dot(p_o.astype(BF16), vo[kv], preferred_element_type=F32)
            o_ref[:, kv * gw:(kv + 1) * gw] = _unstack_pairs(o.astype(BF16), pairs)

    qspec, kv_specs, tspec = _attn_specs(nb, q_dim, pairs, lambda b, i: b * nb + i, lambda b, i: b * nb + i,
                                         lambda b, i: b * nb + jnp.maximum(i - 1, 0))
    args, in_specs = [qkv] * 5 + [tab2], [qspec] + kv_specs + [tspec]
    out_shape, out_specs = [jax.ShapeDtypeStruct((T, q_dim), BF16)], [qspec]
    scratch = [pltpu.VMEM((N_KV_HEADS, ATT_SPAN, PAIR), BF16)] * 4
    if n_pay:
        p_arrays, p_in, p_shapes, p_out, p_scratch = _payload_layout(payload)
        args += p_arrays
        in_specs += p_in
        out_shape += p_shapes
        out_specs += p_out
        scratch += p_scratch
    res = pl.pallas_call(
        body,
        name="attn_fwd",
        grid=(B, nb),
        out_shape=tuple(out_shape),
        in_specs=in_specs,
        out_specs=tuple(out_specs),
        scratch_shapes=scratch,
        compiler_params=_params(*(("arbitrary",) * 2 if n_pay else ("parallel", "arbitrary"))),
    )(*args)
    return (res[0], list(res[1:])) if n_pay else res[0]


def _attn_bwd(qkv, o, do, tab2, B, S):
    T = qkv.shape[0]
    hd = HEAD_DIM
    pairs = GROUP // 2
    gw = pairs * PAIR
    rows = pairs * ATT_BLOCK
    nb = S // ATT_BLOCK
    last = nb - 1
    q_dim = N_HEADS * hd
    tn = (((0,), (0,)), ((), ()))
    nt = (((1,), (1,)), ((), ()))

    def body(q_ref, kc_ref, kp_ref, vc_ref, vp_ref, tab_ref, o_ref, do_ref,
             dq_ref, dkv_ref, dsink_ref, carry_k, carry_v, ke, ko, ve, vo):
        i = pl.program_id(1)

        @pl.when(i == 0)
        def _():
            dsink_ref[...] = jnp.zeros_like(dsink_ref)

        def emit(kv, dk_rows, dv_rows):
            dkv_ref[:, kv * hd:(kv + 1) * hd] = dk_rows.astype(BF16)
            dkv_ref[:, KV_COLS + kv * hd:KV_COLS + (kv + 1) * hd] = dv_rows.astype(BF16)

        @pl.when(i < nb)
        def _():
            pen = _first_block_penalty(i)
            low = lax.broadcasted_iota(jnp.int32, (rows, PAIR), 1) < hd
            low_k = lax.broadcasted_iota(jnp.int32, (ATT_SPAN, PAIR), 1) < hd
            keep = lax.broadcasted_iota(jnp.int32, (ATT_SPAN, hd), 0) > 0
            hg = lax.broadcasted_iota(jnp.int32, (pairs, rows), 0)
            hr = lax.broadcasted_iota(jnp.int32, (pairs, rows), 1)
            head_of = jnp.where(hr // ATT_BLOCK == hg, 1.0, 0.0).astype(BF16)
            dks, dvs = [], []
            for kv in range(N_KV_HEADS):
                _load_span(kp_ref, kc_ref, ke.at[kv], ko.at[kv], kv, ATT_SCALE)
                _load_span(vp_ref, vc_ref, ve.at[kv], vo.at[kv], kv, 1.0)
                qp = _stack_pairs(q_ref, pairs, kv * gw)
                dop = _stack_pairs(do_ref, pairs, kv * gw)
                op = _stack_pairs(o_ref, pairs, kv * gw)
                p_e = _attn_probs(qp, ke[kv], tab_ref[kv, 0], pen)
                p_o = _attn_probs(qp, ko[kv], tab_ref[kv, 1], pen)
                prod = dop.astype(F32) * op.astype(F32)
                d_e = jnp.sum(jnp.where(low, prod, 0.0), axis=-1, keepdims=True)
                d_o = jnp.sum(prod, axis=-1, keepdims=True) - d_e
                ds_e = (p_e * (lax.dot_general(dop, ve[kv], nt, preferred_element_type=F32) - d_e)).astype(BF16)
                ds_o = (p_o * (lax.dot_general(dop, vo[kv], nt, preferred_element_type=F32) - d_o)).astype(BF16)
                dq = jnp.dot(ds_e, ke[kv], preferred_element_type=F32) + jnp.dot(ds_o, ko[kv], preferred_element_type=F32)
                dq_ref[:, kv * gw:(kv + 1) * gw] = _unstack_pairs(dq.astype(BF16), pairs)
                dsink_ref[kv, 0] += jnp.dot(head_of, ds_e, preferred_element_type=F32)[:, 0:128]
                dsink_ref[kv, 1] += jnp.dot(head_of, ds_o, preferred_element_type=F32)[:, 0:128]
                dk2 = jnp.where(low_k, lax.dot_general(ds_e, qp, tn, preferred_element_type=F32),
                                lax.dot_general(ds_o, qp, tn, preferred_element_type=F32))
                dv2 = jnp.where(low_k, lax.dot_general(p_e.astype(BF16), dop, tn, preferred_element_type=F32),
                                lax.dot_general(p_o.astype(BF16), dop, tn, preferred_element_type=F32))
                dks.append(jnp.where(keep, (dk2[:, :hd] + dk2[:, hd:]) * ATT_SCALE, 0.0))
                dvs.append(jnp.where(keep, dv2[:, :hd] + dv2[:, hd:], 0.0))

            @pl.when(i > 0)
            def _():
                for kv in range(N_KV_HEADS):
                    emit(kv, carry_k[kv] + dks[kv][0:ATT_BLOCK], carry_v[kv] + dvs[kv][0:ATT_BLOCK])

            for kv in range(N_KV_HEADS):
                carry_k[kv] = dks[kv][ATT_BLOCK:]
                carry_v[kv] = dvs[kv][ATT_BLOCK:]

        @pl.when(i == nb)
        def _():
            for kv in range(N_KV_HEADS):
                emit(kv, carry_k[kv], carry_v[kv])

    qspec, kv_specs, tspec = _attn_specs(nb, q_dim, pairs, lambda b, i: b * nb + jnp.minimum(i, last),
                                         lambda b, i: b * nb + jnp.minimum(i, last),
                                         lambda b, i: b * nb + jnp.clip(i - 1, 0, last))
    dkv = pl.BlockSpec((ATT_BLOCK, 2 * KV_COLS), lambda b, i: (b * nb + jnp.maximum(i - 1, 0), 0))
    dsk = pl.BlockSpec((None, N_KV_HEADS, 2, pairs, 128), lambda b, i: (b, 0, 0, 0, 0))
    return pl.pallas_call(
        body,
        name="attn_bwd",
        grid=(B, nb + 1),
        out_shape=(
            jax.ShapeDtypeStruct((T, q_dim), BF16),
            jax.ShapeDtypeStruct((T, 2 * KV_COLS), BF16),
            jax.ShapeDtypeStruct((B, N_KV_HEADS, 2, pairs, 128), F32),
        ),
        in_specs=[qspec] + kv_specs + [tspec, qspec, qspec],
        out_specs=(qspec, dkv, dsk),
        scratch_shapes=[pltpu.VMEM((N_KV_HEADS, ATT_BLOCK, hd), F32), pltpu.VMEM((N_KV_HEADS, ATT_BLOCK, hd), F32)]
        + [pltpu.VMEM((N_KV_HEADS, ATT_SPAN, PAIR), BF16)] * 4,
        compiler_params=_params("arbitrary", "arbitrary"),
    )(qkv, qkv, qkv, qkv, qkv, tab2, o, do)


def _conv_tiles(S):
    ts = _pick(S, 256, CONV_HALO)
    return ts, S // ts


def _conv_chunks(C, ts):
    lane = _pick(C, 128, 128)
    return lane, C // lane, _pick(ts, 128, 8)


def _conv_weight_chunks(w_dw, C):
    lane = _pick(C, 128, 128)
    w = jnp.pad(w_dw, ((0, CONV_HALO - CONV_WIDTH), (0, 0)))
    return w.reshape(CONV_HALO, C // lane, lane).transpose(1, 0, 2)


def _conv_fwd(u, w3, b_dw, ln_g, ln_b, S):
    T, C2 = u.shape
    C = C2 // 2
    B = T // S
    ts, nj = _conv_tiles(S)
    hb = ts // CONV_HALO
    lane, nc, rc = _conv_chunks(C, ts)

    def body(a_ref, g_ref, ap_ref, gp_ref, w_ref, bdw_ref, lg_ref, lb_ref, cv_ref, s_ref, buf, cvb):
        j = pl.program_id(1)
        glu_prev = ap_ref[...] * _sigmoid(gp_ref[...]) * (j > 0).astype(F32)
        glu = a_ref[...] * _sigmoid(g_ref[...])
        for cc in range(nc):
            buf[cc, 0:CONV_HALO, :] = glu_prev[:, cc * lane:(cc + 1) * lane]
            buf[cc, CONV_HALO:, :] = glu[:, cc * lane:(cc + 1) * lane]

        def chunk(cc, carry):
            for r0 in range(0, ts, rc):
                acc = jnp.zeros((rc, lane), F32)
                for kk in range(CONV_WIDTH):
                    lo = CONV_HALO - (CONV_WIDTH - 1 - kk) + r0
                    acc = acc + w_ref[cc, kk:kk + 1, :] * buf[cc, lo:lo + rc, :]
                cvb[cc, r0:r0 + rc, :] = acc
            return carry

        lax.fori_loop(0, nc, chunk, 0)
        for cc in range(nc):
            cv_ref[:, cc * lane:(cc + 1) * lane] = cvb[cc] + bdw_ref[:, cc * lane:(cc + 1) * lane]
        cv = cv_ref[...]
        mu = jnp.mean(cv, axis=-1, keepdims=True)
        xc = cv - mu
        rstd = lax.rsqrt(jnp.mean(xc * xc, axis=-1, keepdims=True) + EPS)
        ln = xc * rstd * lg_ref[...] + lb_ref[...]
        s_ref[...] = (ln * _sigmoid(ln)).astype(BF16)

    a_cur = pl.BlockSpec((ts, C), lambda b, j: (b * nj + j, 0))
    g_cur = pl.BlockSpec((ts, C), lambda b, j: (b * nj + j, 1))
    a_prev = pl.BlockSpec((CONV_HALO, C), lambda b, j: (jnp.maximum((b * nj + j) * hb - 1, 0), 0))
    g_prev = pl.BlockSpec((CONV_HALO, C), lambda b, j: (jnp.maximum((b * nj + j) * hb - 1, 0), 1))
    wspec = pl.BlockSpec((nc, CONV_HALO, lane), lambda b, j: (0, 0, 0))
    one = pl.BlockSpec((1, C), lambda b, j: (0, 0))
    return pl.pallas_call(
        body,
        name="conv_fwd",
        grid=(B, nj),
        out_shape=(jax.ShapeDtypeStruct((T, C), F32), jax.ShapeDtypeStruct((T, C), BF16)),
        in_specs=[a_cur, g_cur, a_prev, g_prev, wspec, one, one, one],
        out_specs=(a_cur, a_cur),
        scratch_shapes=[pltpu.VMEM((nc, CONV_HALO + ts, lane), F32), pltpu.VMEM((nc, ts, lane), F32)],
        compiler_params=_params("parallel", "arbitrary"),
    )(u, u, u, u, w3, b_dw, ln_g, ln_b)


def _conv_bwd(dcv, u, w3, S):
    T, C2 = u.shape
    C = C2 // 2
    B = T // S
    ts, nj = _conv_tiles(S)
    hb = ts // CONV_HALO
    n_halo_blocks = T // CONV_HALO
    lane, nc, rc = _conv_chunks(C, ts)

    def body(dcv_ref, dnx_ref, a_ref, g_ref, ap_ref, gp_ref, w_ref, du_ref, dw_ref, gbuf, dbuf, dglu, dw8):
        b, j = pl.program_id(0), pl.program_id(1)

        @pl.when((b == 0) & (j == 0))
        def _():
            dw8[...] = jnp.zeros_like(dw8)

        a = a_ref[...]
        sg = _sigmoid(g_ref[...])
        glu_prev = ap_ref[...] * _sigmoid(gp_ref[...]) * (j > 0).astype(F32)
        glu = a * sg
        dcur = dcv_ref[...]
        dnext = dnx_ref[...] * (j < nj - 1).astype(F32)
        for cc in range(nc):
            cols = slice(cc * lane, (cc + 1) * lane)
            gbuf[cc, 0:CONV_HALO, :] = glu_prev[:, cols]
            gbuf[cc, CONV_HALO:, :] = glu[:, cols]
            dbuf[cc, 0:ts, :] = dcur[:, cols]
            dbuf[cc, ts:, :] = dnext[:, cols]

        def chunk(cc, carry):
            for r0 in range(0, ts, rc):
                acc = jnp.zeros((rc, lane), F32)
                for kk in range(CONV_WIDTH):
                    d = CONV_WIDTH - 1 - kk
                    acc = acc + w_ref[cc, kk:kk + 1, :] * dbuf[cc, r0 + d:r0 + d + rc, :]
                dglu[cc, r0:r0 + rc, :] = acc
            for kk in range(CONV_WIDTH):
                d = CONV_WIDTH - 1 - kk
                p = jnp.zeros((rc, lane), F32)
                for r0 in range(0, ts, rc):
                    lo = CONV_HALO - d + r0
                    p = p + dbuf[cc, r0:r0 + rc, :] * gbuf[cc, lo:lo + rc, :]
                dw8[cc, kk * 8:(kk + 1) * 8, :] += jnp.sum(p.reshape(rc // 8, 8, lane), axis=0)
            return carry

        lax.fori_loop(0, nc, chunk, 0)
        for cc in range(nc):
            cols = slice(cc * lane, (cc + 1) * lane)
            dgl = dglu[cc]
            du_ref[:, cc * lane:(cc + 1) * lane] = (dgl * sg[:, cols]).astype(BF16)
            du_ref[:, C + cc * lane:C + (cc + 1) * lane] = (dgl * a[:, cols] * sg[:, cols] * (1.0 - sg[:, cols])).astype(BF16)

        @pl.when((b == B - 1) & (j == nj - 1))
        def _():
            dw_ref[...] = jnp.zeros_like(dw_ref)
            for kk in range(CONV_WIDTH):
                dw_ref[:, kk:kk + 1, :] = jnp.sum(dw8[:, kk * 8:(kk + 1) * 8, :], axis=1, keepdims=True)

    a_cur = pl.BlockSpec((ts, C), lambda b, j: (b * nj + j, 0))
    g_cur = pl.BlockSpec((ts, C), lambda b, j: (b * nj + j, 1))
    a_prev = pl.BlockSpec((CONV_HALO, C), lambda b, j: (jnp.maximum((b * nj + j) * hb - 1, 0), 0))
    g_prev = pl.BlockSpec((CONV_HALO, C), lambda b, j: (jnp.maximum((b * nj + j) * hb - 1, 0), 1))
    d_next = pl.BlockSpec((CONV_HALO, C), lambda b, j: (jnp.minimum((b * nj + j + 1) * hb, n_halo_blocks - 1), 0))
    wspec = pl.BlockSpec((nc, CONV_HALO, lane), lambda b, j: (0, 0, 0))
    return pl.pallas_call(
        body,
        name="conv_bwd",
        grid=(B, nj),
        out_shape=(jax.ShapeDtypeStruct((T, C2), BF16), jax.ShapeDtypeStruct((nc, CONV_HALO, lane), F32)),
        in_specs=[a_cur, d_next, a_cur, g_cur, a_prev, g_prev, wspec],
        out_specs=(pl.BlockSpec((ts, C2), lambda b, j: (b * nj + j, 0)), wspec),
        scratch_shapes=[
            pltpu.VMEM((nc, CONV_HALO + ts, lane), F32),
            pltpu.VMEM((nc, ts + CONV_HALO, lane), F32),
            pltpu.VMEM((nc, ts, lane), F32),
            pltpu.VMEM((nc, CONV_HALO * 8, lane), F32),
        ],
        compiler_params=_params("arbitrary", "arbitrary"),
    )(dcv, dcv, u, u, u, u, w3)


def _loss_head(x, tgt, gfin, y, gate, S, tm_pref=256):
    T, D = x.shape
    tm = _pick(S, tm_pref, 16)
    tpb = S // tm

    def body(x_ref, t_ref, g_ref, y_ref, gt_ref, dx_ref, loss_ref, dg_ref, dy_ref, dgate_ref):
        @pl.when(pl.program_id(0) == 0)
        def _():
            loss_ref[...] = jnp.zeros_like(loss_ref)
            dg_ref[...] = jnp.zeros_like(dg_ref)

        @pl.when(pl.program_id(0) % tpb == 0)
        def _():
            dgate_ref[...] = jnp.zeros_like(dgate_ref)

        xv = x_ref[...]
        g = g_ref[...]
        r = lax.rsqrt(jnp.mean(xv * xv, axis=-1, keepdims=True) + EPS)
        xhat = xv * r
        e = xhat * g - t_ref[...]
        row_loss = jnp.mean(e * e, axis=-1, keepdims=True)
        loss_ref[...] += 0.5 * jnp.sum(row_loss, axis=0, keepdims=True)
        dy = e * (1.0 / D)
        dg_ref[...] += jnp.sum(dy * xhat, axis=0, keepdims=True)
        dxhat = dy * g
        dx = r * (dxhat - xhat * jnp.mean(dxhat * xhat, axis=-1, keepdims=True))
        dx_ref[...] = dx
        dy_ref[...] = (dx * gt_ref[...]).astype(BF16)
        dgate_ref[...] += jnp.sum(dx * y_ref[...].astype(F32), axis=0, keepdims=True)

    row = pl.BlockSpec((tm, D), lambda i: (i, 0))
    vec = pl.BlockSpec((None, 1, D), lambda i: (i // tpb, 0, 0))
    one = pl.BlockSpec((1, D), lambda i: (0, 0))
    return pl.pallas_call(
        body,
        name="loss_head",
        grid=(T // tm,),
        out_shape=(
            jax.ShapeDtypeStruct((T, D), F32),
            jax.ShapeDtypeStruct((8, 128), F32),
            jax.ShapeDtypeStruct((1, D), F32),
            jax.ShapeDtypeStruct((T, D), BF16),
            jax.ShapeDtypeStruct((T // S, 1, D), F32),
        ),
        in_specs=[row, row, one, row, vec],
        out_specs=(row, pl.BlockSpec((8, 128), lambda i: (0, 0)), one, row, vec),
        compiler_params=_params("arbitrary"),
    )(x, tgt, gfin, y, gate)


def _adam(name, parts, w, m, v, tm_pref=256):
    L = len(parts)
    P, R, C = parts[0].shape
    fit = VMEM_LIMIT_BYTES * 5 // 8 // (2 * L * P * C * parts[0].dtype.itemsize)
    tm = _pick(R, max(16, min(tm_pref, fit // 16 * 16)), 16)
    tiles = R // tm
    c1 = 1.0 - ADAM_B1 ** ADAM_STEP
    c2 = 1.0 - ADAM_B2 ** ADAM_STEP

    def body(*refs):
        p_refs = refs[:L]
        w_ref, m_ref, v_ref, g_ref, d_ref, mo_ref, vo_ref = refs[L:]
        for l in range(L):
            @pl.when(pl.program_id(0) == l)
            def _():
                g = p_refs[l][0].astype(F32)
                for i in range(1, P):
                    g = g + p_refs[l][i].astype(F32)
                m_new = ADAM_B1 * m_ref[...] + (1.0 - ADAM_B1) * g
                v_new = ADAM_B2 * v_ref[...] + (1.0 - ADAM_B2) * (g * g)
                m_hat = m_new / c1
                v_hat = v_new / c2
                g_ref[...] = g
                d_ref[...] = -ADAM_LR * (m_hat / (jnp.sqrt(v_hat) + ADAM_EPS) + ADAM_WD * w_ref[...])
                mo_ref[...] = m_new
                vo_ref[...] = v_new

    def p_spec(l):
        return pl.BlockSpec((P, tm, C), lambda li, t: (0, jnp.where(li == l, t, jnp.where(li < l, 0, tiles - 1)), 0))

    row = pl.BlockSpec((tm, C), lambda li, t: (li * tiles + t, 0))
    return pl.pallas_call(
        body,
        name=name,
        grid=(L, tiles),
        out_shape=(jax.ShapeDtypeStruct((L * R, C), F32),) * 4,
        in_specs=[p_spec(l) for l in range(L)] + [row, row, row],
        out_specs=(row, row, row, row),
        compiler_params=_params("arbitrary", "arbitrary"),
    )(*parts, w, m, v)


BIG = ("w_qkv", "w_o", "w_pw1", "w_pw2", "w_up", "w_down")
SMALL_SHARDED = (("b_pw1", 1), ("w_dw", 2), ("b_dw", 1), ("conv_ln_g", 1), ("conv_ln_b", 1), ("b_pw2", 1))
SMALL_REPL = ("b_mod", "norm_mix", "norm_mlp", "b_qkv", "b_o", "sinks", "final_norm")
WEIGHTS = ("w_mod", "b_mod", "norm_mix", "norm_mlp", "w_qkv", "b_qkv", "w_o", "b_o", "sinks", "w_pw1", "b_pw1",
           "w_dw", "b_dw", "conv_ln_g", "conv_ln_b", "w_pw2", "b_pw2", "w_up", "w_down", "final_norm")


def _step(x, c, loss_target, W, M, V):
    B, S, D = x.shape
    T = B * S
    L = W["w_mod"].shape[0]
    n_mod = W["w_mod"].shape[2]
    me = 4 * lax.axis_index("x") + 2 * lax.axis_index("y") + lax.axis_index("c")

    Wb = {n: W[n].astype(BF16) for n in BIG}

    def shards(i):
        jm = i // 2
        first, last = ("w_qkv", "w_o") if i % 2 == 0 else ("w_pw1", "w_pw2")
        return [Wb[first][jm], Wb[last][jm], Wb["w_up"][i], Wb["w_down"][i]]

    def carried(res, payload):
        return res if payload else (res, [])

    small_names = [n for n, _ in SMALL_SHARDED]
    small_src, small_sizes = _pack([c] + [W[n] for n in small_names], F32, 0)
    got = _exchange("gather_first", [(small_src, True), (shards(0)[0], True)])
    small_parts = _unpack(got[0], small_sizes, [c.shape] + [W[n].shape for n in small_names], 1)
    c_all = small_parts[0].reshape(N_DEV * B, D)
    full = {n: _from_slots(p, d) for (n, d), p in zip(SMALL_SHARDED, small_parts[1:])}
    gathered = [[got[1], None, None, None]] + [None] * (L - 1)

    b_mod_mine = lax.dynamic_slice_in_dim(W["b_mod"], me * n_mod, n_mod, axis=1).reshape(L, 1, n_mod)
    mod_part = _mod_fwd(c_all, W["w_mod"].astype(BF16), b_mod_mine)
    mod_slots = mod_part.reshape(L, N_DEV, B, n_mod).transpose(1, 0, 2, 3).reshape(N_DEV, L * B, n_mod)
    mod_recv = _exchange("scatter_mod", [(mod_slots, False)])[0]
    mod = mod_recv.reshape(N_DEV, L, B, n_mod).transpose(1, 2, 0, 3).reshape(L, B, N_MOD, 1, D)

    w_dw3 = [_conv_weight_chunks(full["w_dw"][j], D) for j in range(full["w_dw"].shape[0])]

    xc = x.reshape(T, D)
    saved = []
    h1 = _normmod_fwd("normmod_mix_fwd", xc, W["norm_mix"][0][None], mod[0, :, 1], mod[0, :, 0], S)
    for i in range(L):
        jm = i // 2
        sh1, sc1, g1, sh2, sc2, g2 = (mod[i, :, t] for t in range(N_MOD))
        nxt = [[(s, True)] for s in shards(i + 1)] if i + 1 < L else [[]] * 4
        if i > 0:
            nxt = [nxt[0] + nxt[1], [], nxt[2], nxt[3]]
        g_first, g_last, g_up, g_down = gathered[i]
        sv = {"x_in": xc, "h1": h1}
        if i % 2 == 0:
            w_qkv_full = _from_slots(g_first, 1)
            sv["w_first"] = w_qkv_full
            mine = [(s, True) for s in shards(0)[1:]] if i == 0 else []
            pay = nxt[0] + mine[:1]
            qkv, n_first = carried(_mm("qkv_fwd", h1, w_qkv_full, bias=W["b_qkv"][jm], payload=pay), pay)
            tab = _pair_tables(_attn_table(W["sinks"][jm]))
            mix_in, n_attn = carried(_attn_fwd(qkv, tab, B, S, payload=mine[1:]), mine[1:])
            if i == 0:
                g_last, g_up, g_down = n_first[-1], n_attn[0], n_attn[1]
            sv.update(qkv=qkv, tab=tab)
            b_out = W["b_o"][jm]
        else:
            sv["w_first"] = g_first
            u, n_first = carried(_mm("pw1_fwd", h1, g_first, w_form="nslots", bias=full["b_pw1"][jm], out_dtype=F32,
                                     payload=nxt[0]), nxt[0])
            cv, mix_in = _conv_fwd(u, w_dw3[jm], full["b_dw"][jm][None], full["conv_ln_g"][jm][None],
                                   full["conv_ln_b"][jm][None], S)
            sv.update(u=u, cv=cv)
            b_out = full["b_pw2"][jm]
        w_last = g_last.reshape(-1, D)
        w_down_full = g_down.reshape(-1, D)
        sv.update(mix_in=mix_in, w_last=w_last, g_up=g_up, w_down=w_down_full)
        (x1, y1, h2), n_last = carried(_mm("mix_out_fwd", mix_in, w_last, bias=b_out, epi="resid", x=xc, gate=g1,
                                           norm=(W["norm_mlp"][i][None], sc2, sh2), rows_per_batch=S,
                                           payload=nxt[1]), nxt[1])
        sv.update(y1=y1, x1=x1)
        up, n_up = carried(_mm("mlp_up_fwd", h2, g_up, w_form="nslots", epi="relu", payload=nxt[2]), nxt[2])
        norm_next = (W["norm_mix"][i + 1][None], mod[i + 1, :, 1], mod[i + 1, :, 0]) if i + 1 < L else None
        res, n_down = carried(_mm("mlp_down_fwd", up, w_down_full, a_sq=True, epi="resid", x=x1, gate=g2,
                                  norm=norm_next, rows_per_batch=S, payload=nxt[3]), nxt[3])
        x2, y2 = res[0], res[1]
        h1 = res[2] if i + 1 < L else None
        sv.update(h2=h2, up=up, y2=y2)
        saved.append(sv)
        xc = x2
        if i + 1 < L:
            gathered[i + 1] = [n_first[0], n_last[0] if i == 0 else n_first[1], n_up[0], n_down[0]]

    dx, loss_blk, dgfin, dy2, dg2 = _loss_head(xc, loss_target.reshape(T, D), W["final_norm"][None],
                                               saved[L - 1]["y2"], mod[L - 1, :, 5], S)
    loss = lax.psum(loss_blk[0, 0], ("x", "y", "c"))

    G = {"final_norm": dgfin.reshape(D)}
    dmod_layers = [None] * L
    small_grads = ("norm_mix", "norm_mlp", "b_qkv", "b_o", "sinks", "b_pw1", "w_dw", "b_dw", "conv_ln_g", "conv_ln_b",
                   "b_pw2")
    acc = {n: [None] * W[n].shape[0] for n in small_grads}
    reduced = {n: [None] * W[n].shape[0] for n in BIG}

    waiting_up = None
    for i in reversed(range(L)):
        jm = i // 2
        sv = saved[i]
        sh1, sc1, g1, sh2, sc2, g2 = (mod[i, :, t] for t in range(N_MOD))
        pay = [(waiting_up[1], False)] if waiting_up else []
        (gd0, gd1), r = carried(_mm_tn("w_down_grad", sv["up"], dy2, a_sq=True, out_form="kslots", halves=True,
                                       tt_pref=4096, payload=pay), pay)
        if waiting_up:
            reduced["w_up"][waiting_up[0]].append(r[0])
        du, r0 = _mm("mlp_down_bwd", dy2, sv["w_down"], w_form="full_t", epi="relu2d", u=sv["up"],
                     payload=[(gd0, False)])
        (gu0, gu1), r1 = _mm_tn("w_up_grad", sv["h2"], du, out_form="nslots", halves=True, tt_pref=4096,
                                payload=[(gd1, False)])
        reduced["w_down"][i] = [r0[0], r1[0]]
        pay = [(gu0, False)] + ([] if i > 0 else [(gu1, False)])
        (dx, dsh2, dsc2, dgn, dy1, dg1, dy1_sum), r = _mm(
            "mlp_up_bwd", du, sv["g_up"], w_form="kslots_t", epi="normbwd", rows_per_batch=S,
            nb=dict(x=sv["x1"], dx_in=dx, gnorm=W["norm_mlp"][i][None], sc=sc2, gate=(sv["y1"], g1)),
            payload=pay)
        reduced["w_up"][i] = list(r)
        waiting_up = (i, gu1) if i > 0 else None
        acc["norm_mlp"][i] = dgn.reshape(D)
        gw_last = _mm_tn("mix_out_grad", sv["mix_in"], dy1)
        gw_last = gw_last.reshape((N_DEV, -1) + gw_last.shape[1:]).astype(BF16)
        below = (saved[i - 1]["y2"], mod[i - 1, :, 5]) if i > 0 else None
        nb_mix = dict(x=sv["x_in"], dx_in=dx, gnorm=W["norm_mix"][i][None], sc=sc1, gate=below)
        if i % 2 == 0:
            acc["b_o"][jm] = dy1_sum.reshape(D)
            do = _mm("attn_out_bwd", dy1, sv["w_last"], w_form="full_t")
            dq, dkv, dsk = _attn_bwd(sv["qkv"], sv["mix_in"], do, sv["tab"], B, S)
            acc["sinks"][jm] = jnp.sum(dsk[..., 0], axis=0).transpose(0, 2, 1).reshape(N_HEADS)
            dqkv = jnp.concatenate([dq, dkv], axis=1)
            gw_qkv, db_qkv = _mm_tn("w_qkv_grad", sv["h1"], dqkv, colsum=True, tt_pref=4096)
            acc["b_qkv"][jm] = db_qkv.reshape(-1)
            gw_qkv = _to_slots(gw_qkv, 1).astype(BF16)
            res, r = _mm("qkv_bwd", dqkv, sv["w_first"], w_form="full_t", epi="normbwd", nb=nb_mix, rows_per_batch=S,
                         payload=[(gw_last, False), (gw_qkv, False)])
            reduced["w_o"][jm], reduced["w_qkv"][jm] = r[0], r[1]
        else:
            acc["b_pw2"][jm] = dy1_sum.reshape(D)
            dcv, dlg, dlb, dbdw = _mm(
                "pw2_bwd", dy1, sv["w_last"], w_form="full_t", epi="lnbwd",
                ln=(sv["cv"], full["conv_ln_g"][jm][None], full["conv_ln_b"][jm][None]))
            acc["conv_ln_g"][jm], acc["conv_ln_b"][jm], acc["b_dw"][jm] = dlg.reshape(-1), dlb.reshape(-1), dbdw.reshape(-1)
            du1, dwdw = _conv_bwd(dcv, sv["u"], w_dw3[jm], S)
            acc["w_dw"][jm] = dwdw.transpose(1, 0, 2).reshape(CONV_HALO, D)[:CONV_WIDTH]
            gw_pw1, db_pw1 = _mm_tn("w_pw1_grad", sv["h1"], du1, out_form="nslots", colsum=True, tt_pref=4096)
            acc["b_pw1"][jm] = db_pw1.reshape(-1)
            res, r = _mm("pw1_bwd", du1, sv["w_first"], w_form="kslots_t", epi="normbwd", nb=nb_mix, rows_per_batch=S,
                         payload=[(gw_last, False), (gw_pw1, False)])
            reduced["w_pw2"][jm], reduced["w_pw1"][jm] = r[0], r[1]
        dx, dsh1, dsc1, dgn = res[:4]
        acc["norm_mix"][i] = dgn.reshape(D)
        dmod_layers[i] = jnp.concatenate([dsh1, dsc1, dg1, dsh2, dsc2, dg2], axis=1).reshape(B, N_MOD * D)
        if i > 0:
            dy2, dg2 = res[4], res[5]
    grad_x = dx.reshape(B, S, D)
    for n, parts in acc.items():
        G[n] = jnp.stack(parts)

    dmod = jnp.stack(dmod_layers)
    dmod_slots = dmod.reshape(L, B, N_DEV, n_mod).transpose(2, 0, 1, 3).reshape(N_DEV, L * B, n_mod)
    dmod_recv = _exchange("gather_dmod", [(dmod_slots, False)])[0]
    dmod_all = dmod_recv.reshape(N_DEV, L, B, n_mod).transpose(1, 0, 2, 3).reshape(L, N_DEV * B, n_mod)
    g_w_mod, db_mod_mine = _mod_bwd(c_all, dmod_all)
    G["b_mod"] = lax.dynamic_update_slice_in_dim(jnp.zeros_like(W["b_mod"]), db_mod_mine.reshape(L, n_mod),
                                                 me * n_mod, axis=1)

    small_items = [jnp.broadcast_to(G[n][None], (N_DEV,) + G[n].shape) for n in SMALL_REPL]
    small_items += [_to_slots(G[n], d) for n, d in SMALL_SHARDED]
    small_slots, small_sizes2 = _pack(small_items, F32, 1)
    small_recv = _exchange("reduce_small", [(small_slots, False)])[0]

    out = {}

    def run_adam(name, parts, names):
        shapes = [W[n].shape for n in names]
        wp, sizes = _pack([W[n] for n in names], F32, 0)
        mp, _ = _pack([M[n] for n in names], F32, 0)
        vp, _ = _pack([V[n] for n in names], F32, 0)
        res = _adam(name, [parts], wp, mp, vp)
        for kind, buf in zip(("grad", "delta", "new_m", "new_v"), res):
            for n, a in zip(names, _unpack(buf, sizes, shapes, 0)):
                out[kind + "_" + n] = a

    for n in BIG + ("w_mod",):
        cols = W[n].shape[-1]
        if n == "w_mod":
            parts = [g_w_mod.reshape(1, -1, cols)]
        else:
            pieces = [p for r in reduced[n] for p in (r if isinstance(r, list) else [r])]
            parts = [p.reshape(N_DEV, -1, cols) for p in pieces]
        res = _adam("adam_" + n, parts, W[n].reshape(-1, cols), M[n].reshape(-1, cols), V[n].reshape(-1, cols))
        for kind, buf in zip(("grad", "delta", "new_m", "new_v"), res):
            out[kind + "_" + n] = buf.reshape(W[n].shape)
    run_adam("adam_small", small_recv, list(SMALL_REPL) + [n for n, _ in SMALL_SHARDED])

    res = [loss, grad_x]
    for kind in ("grad", "delta", "new_m", "new_v"):
        res += [out[kind + "_" + n] for n in WEIGHTS]
    return tuple(res)


def kernel(x, c, w_mod, b_mod, norm_mix, norm_mlp, w_qkv, b_qkv, w_o, b_o, sinks, w_pw1, b_pw1, w_dw, b_dw, conv_ln_g, conv_ln_b, w_pw2, b_pw2, w_up, w_down, final_norm, loss_target, m_w_mod, m_b_mod, m_norm_mix, m_norm_mlp, m_w_qkv, m_b_qkv, m_w_o, m_b_o, m_sinks, m_w_pw1, m_b_pw1, m_w_dw, m_b_dw, m_conv_ln_g, m_conv_ln_b, m_w_pw2, m_b_pw2, m_w_up, m_w_down, m_final_norm, v_w_mod, v_b_mod, v_norm_mix, v_norm_mlp, v_w_qkv, v_b_qkv, v_w_o, v_b_o, v_sinks, v_w_pw1, v_b_pw1, v_w_dw, v_b_dw, v_conv_ln_g, v_conv_ln_b, v_w_pw2, v_b_pw2, v_w_up, v_w_down, v_final_norm):
    W = dict(w_mod=w_mod, b_mod=b_mod, norm_mix=norm_mix, norm_mlp=norm_mlp, w_qkv=w_qkv, b_qkv=b_qkv, w_o=w_o,
             b_o=b_o, sinks=sinks, w_pw1=w_pw1, b_pw1=b_pw1, w_dw=w_dw, b_dw=b_dw, conv_ln_g=conv_ln_g,
             conv_ln_b=conv_ln_b, w_pw2=w_pw2, b_pw2=b_pw2, w_up=w_up, w_down=w_down, final_norm=final_norm)
    M = dict(w_mod=m_w_mod, b_mod=m_b_mod, norm_mix=m_norm_mix, norm_mlp=m_norm_mlp, w_qkv=m_w_qkv, b_qkv=m_b_qkv,
             w_o=m_w_o, b_o=m_b_o, sinks=m_sinks, w_pw1=m_w_pw1, b_pw1=m_b_pw1, w_dw=m_w_dw, b_dw=m_b_dw,
             conv_ln_g=m_conv_ln_g, conv_ln_b=m_conv_ln_b, w_pw2=m_w_pw2, b_pw2=m_b_pw2, w_up=m_w_up,
             w_down=m_w_down, final_norm=m_final_norm)
    V = dict(w_mod=v_w_mod, b_mod=v_b_mod, norm_mix=v_norm_mix, norm_mlp=v_norm_mlp, w_qkv=v_w_qkv, b_qkv=v_b_qkv,
             w_o=v_w_o, b_o=v_b_o, sinks=v_sinks, w_pw1=v_w_pw1, b_pw1=v_b_pw1, w_dw=v_w_dw, b_dw=v_b_dw,
             conv_ln_g=v_conv_ln_g, conv_ln_b=v_conv_ln_b, w_pw2=v_w_pw2, b_pw2=v_b_pw2, w_up=v_w_up,
             w_down=v_w_down, final_norm=v_final_norm)
    return _step(x, c, loss_target, W, M, V)
```

```python
import numpy as np
import jax
import jax.numpy as jnp
from jax import lax
from jax.experimental import pallas as pl
from jax.experimental.pallas import tpu as pltpu

F32 = jnp.float32
BF16 = jnp.bfloat16

N_DEV = 8
N_HEADS = 16
N_KV_HEADS = 2
HEAD_DIM = 64
GROUP = N_HEADS // N_KV_HEADS
ATT_BLOCK = 128
CONV_WIDTH = 31
CONV_HALO = 32
N_MOD = 6
EPS = 1e-6
ADAM_LR = 0.001
ADAM_B1 = 0.9
ADAM_B2 = 0.999
ADAM_EPS = 1e-08
ADAM_WD = 0.01
ADAM_STEP = 10
NEG_BIG = -1e30
PACK_COLS = 1024
VMEM_LIMIT_BYTES = 56 * 1024 * 1024
MESH_ID = pl.DeviceIdType.MESH


def _params(*sem):
    return pltpu.CompilerParams(dimension_semantics=sem, vmem_limit_bytes=VMEM_LIMIT_BYTES)


def _pick(n, pref, mult=8):
    if n <= pref:
        return n
    for t in range(pref, 0, -1):
        if n % t == 0 and t % mult == 0:
            return t
    return n


def _sigmoid(z):
    return 0.5 * jnp.tanh(0.5 * z) + 0.5


def _payload_layout(payload):
    n = len(payload)
    out_shapes = [jax.ShapeDtypeStruct((N_DEV,) + tuple(a.shape if bc else a.shape[1:]), a.dtype) for a, bc in payload]
    hbm = pl.BlockSpec(memory_space=pl.ANY)
    scratch = [pltpu.SemaphoreType.DMA((n * (N_DEV - 1),)), pltpu.SemaphoreType.DMA((n * (N_DEV - 1),)),
               pltpu.SemaphoreType.DMA((n,))]
    return [a for a, _ in payload], [hbm] * n, out_shapes, [hbm] * n, scratch


class _Plan:
    def __init__(self, bcasts, src_refs, dst_refs, send_sems, recv_sems, local_sems):
        x, y, c = lax.axis_index("x"), lax.axis_index("y"), lax.axis_index("c")
        me = 4 * x + 2 * y + c
        self.first, self.landed, self.relay, self.local = [], [], [], []
        for t, (bc, s_ref, d_ref) in enumerate(zip(bcasts, src_refs, dst_refs)):
            def remote(k, src, slot, to):
                sem = t * (N_DEV - 1) + k
                return pltpu.make_async_remote_copy(src_ref=src, dst_ref=d_ref.at[slot], send_sem=send_sems.at[sem],
                                                    recv_sem=recv_sems.at[sem], device_id=to, device_id_type=MESH_ID)
            if bc:
                chips = [(1 - x, y), (x, 1 - y), (1 - x, 1 - y)]
                self.first.append(remote(0, s_ref, me, (x, y, 1 - c)))
                for j, (px, py) in enumerate(chips):
                    cp = remote(1 + j, s_ref, me, (px, py, c))
                    self.first.append(cp)
                    self.landed.append(cp)
                    theirs = 4 * px + 2 * py + c
                    self.relay.append(remote(4 + j, d_ref.at[theirs], theirs, (x, y, 1 - c)))
                self.local.append(pltpu.make_async_copy(s_ref, d_ref.at[me], local_sems.at[t]))
            else:
                for k in range(1, N_DEV):
                    px = 1 - x if (k >> 2) & 1 else x
                    py = 1 - y if (k >> 1) & 1 else y
                    pc = 1 - c if k & 1 else c
                    self.first.append(remote(k - 1, s_ref.at[4 * px + 2 * py + pc], me, (px, py, pc)))
                self.local.append(pltpu.make_async_copy(s_ref.at[me], d_ref.at[me], local_sems.at[t]))

    def start(self):
        for cp in self.first + self.local:
            cp.start()

    def pass_on(self):
        for cp in self.landed:
            cp.wait_recv()
        for cp in self.relay:
            cp.start()

    def finish(self):
        for cp in self.first:
            cp.wait_send()
            if not any(cp is l for l in self.landed):
                cp.wait_recv()
        for cp in self.relay:
            cp.wait()
        for cp in self.local:
            cp.wait()


def _exchange(name, payload):
    n = len(payload)
    bcasts = [bc for _, bc in payload]
    arrays, in_specs, out_shapes, out_specs, scratch = _payload_layout(payload)

    def body(*refs):
        plan = _Plan(bcasts, refs[:n], refs[n:2 * n], *refs[2 * n:])
        plan.start()
        plan.pass_on()
        plan.finish()

    return pl.pallas_call(
        body,
        name=name,
        out_shape=tuple(out_shapes),
        in_specs=in_specs,
        out_specs=tuple(out_specs),
        scratch_shapes=scratch,
    )(*arrays)


def _pack(arrays, dtype, lead):
    lead_shape = arrays[0].shape[:lead]
    flat = [a.astype(dtype).reshape(lead_shape + (-1,)) for a in arrays]
    sizes = [f.shape[-1] for f in flat]
    total = sum(sizes)
    chunk = 16 * PACK_COLS
    padded = -(-total // chunk) * chunk
    if padded > total:
        flat.append(jnp.zeros(lead_shape + (padded - total,), dtype))
    buf = jnp.concatenate(flat, axis=-1)
    return buf.reshape(lead_shape + (padded // PACK_COLS, PACK_COLS)), sizes


def _unpack(buf, sizes, shapes, lead):
    lead_shape = buf.shape[:lead]
    flat = buf.reshape(lead_shape + (-1,))
    out, off = [], 0
    for n, shp in zip(sizes, shapes):
        out.append(lax.slice_in_dim(flat, off, off + n, axis=lead).reshape(lead_shape + tuple(shp)))
        off += n
    return out


def _to_slots(a, dim):
    shp = a.shape
    a = a.reshape(shp[:dim] + (N_DEV, shp[dim] // N_DEV) + shp[dim + 1:])
    return jnp.moveaxis(a, dim, 0)


def _from_slots(a, dim):
    a = jnp.moveaxis(a, 0, dim)
    shp = a.shape
    return a.reshape(shp[:dim] + (shp[dim] * shp[dim + 1],) + shp[dim + 2:])


def _mod_fwd(c_all, w, b):
    L, D, n = w.shape
    M = c_all.shape[0]

    def body(c_ref, w_ref, b_ref, o_ref):
        cv = c_ref[...]
        cs = (cv * _sigmoid(cv)).astype(BF16)
        o_ref[...] = jnp.dot(cs, w_ref[...], preferred_element_type=F32) + b_ref[...]

    return pl.pallas_call(
        body,
        name="mod_fwd",
        grid=(L,),
        out_shape=jax.ShapeDtypeStruct((L, M, n), F32),
        in_specs=[
            pl.BlockSpec((M, D), lambda l: (0, 0)),
            pl.BlockSpec((None, D, n), lambda l: (l, 0, 0)),
            pl.BlockSpec((None, 1, n), lambda l: (l, 0, 0)),
        ],
        out_specs=pl.BlockSpec((None, M, n), lambda l: (l, 0, 0)),
        compiler_params=_params("arbitrary"),
    )(c_all, w, b)


def _mod_bwd(c_all, dmod_all):
    L, M, n = dmod_all.shape
    D = c_all.shape[1]

    def body(c_ref, d_ref, dw_ref, db_ref):
        cv = c_ref[...]
        cs = (cv * _sigmoid(cv)).astype(BF16)
        d = d_ref[...]
        dw_ref[...] = lax.dot_general(cs, d.astype(BF16), (((0,), (0,)), ((), ())), preferred_element_type=F32)
        db_ref[...] = jnp.sum(d, axis=0, keepdims=True)

    return pl.pallas_call(
        body,
        name="mod_bwd",
        grid=(L,),
        out_shape=(jax.ShapeDtypeStruct((L, D, n), F32), jax.ShapeDtypeStruct((L, 1, n), F32)),
        in_specs=[
            pl.BlockSpec((M, D), lambda l: (0, 0)),
            pl.BlockSpec((None, M, n), lambda l: (l, 0, 0)),
        ],
        out_specs=(
            pl.BlockSpec((None, D, n), lambda l: (l, 0, 0)),
            pl.BlockSpec((None, 1, n), lambda l: (l, 0, 0)),
        ),
        compiler_params=_params("arbitrary"),
    )(c_all, dmod_all)


def _mm(name, a, w, *, w_form="full", out_dtype=BF16, bias=None, a_sq=False, epi=None, x=None, gate=None, u=None,
        norm=None, nb=None, ln=None, rows_per_batch=None, tm_pref=512, payload=()):
    M, K = a.shape
    if w_form == "full":
        N = w.shape[1]
        nc = _pick(N, 1024, 128)
    elif w_form == "full_t":
        N = w.shape[0]
        nc = _pick(N, 1024, 128)
    elif w_form == "nslots":
        nc = w.shape[2]
        N = N_DEV * nc
    else:
        N = w.shape[1]
        nc = N
    ks = K // N_DEV
    n_chunks = N // nc
    tm = _pick(M if rows_per_batch is None else rows_per_batch, tm_pref, 16)
    steps = M // tm
    relay_step = (3 * steps) // 4
    nb_gate = nb is not None and nb.get("gate") is not None
    tpb_nb = rows_per_batch // tm if epi == "normbwd" else 1
    has_bias = bias is not None
    n_pay = len(payload)
    bcasts = [bc for _, bc in payload]
    nt = (((1,), (1,)), ((), ()))


    def body(*refs):
        it = iter(refs)
        a_ref = next(it)
        w_ref = next(it)
        b_ref = next(it) if has_bias else None
        x_ref = next(it) if epi == "resid" else None
        g_ref = next(it) if epi == "resid" else None
        gn_ref, sc_ref, sh_ref = (next(it), next(it), next(it)) if norm is not None else (None, None, None)
        u_ref = next(it) if epi == "relu2d" else None
        if epi == "normbwd":
            nx_ref, ndx_ref, ngn_ref, nsc_ref = next(it), next(it), next(it), next(it)
            ny_ref, ngt_ref = (next(it), next(it)) if nb_gate else (None, None)
        if epi == "lnbwd":
            cv_ref, lg_ref, lb_ref = next(it), next(it), next(it)
        pay_src = [next(it) for _ in range(n_pay)]
        o_ref = next(it)
        y_ref = next(it) if epi == "resid" else None
        h_ref = next(it) if norm is not None else None
        if epi == "normbwd":
            dsh_ref, dsc_ref, dgn_ref = next(it), next(it), next(it)
            dy_ref, dgate_ref, cs_ref = (next(it), next(it), next(it)) if nb_gate else (None, None, None)
        if epi == "lnbwd":
            dlg_ref, dlb_ref, dcs_ref = next(it), next(it), next(it)
        pay_dst = [next(it) for _ in range(n_pay)]
        sems = list(it)

        if epi == "lnbwd":
            @pl.when(pl.program_id(0) == 0)
            def _():
                dlg_ref[...] = jnp.zeros_like(dlg_ref)
                dlb_ref[...] = jnp.zeros_like(dlb_ref)
                dcs_ref[...] = jnp.zeros_like(dcs_ref)

        if epi == "normbwd":
            @pl.when(pl.program_id(0) % tpb_nb == 0)
            def _():
                dsh_ref[...] = jnp.zeros_like(dsh_ref)
                dsc_ref[...] = jnp.zeros_like(dsc_ref)
                if nb_gate:
                    dgate_ref[...] = jnp.zeros_like(dgate_ref)

            @pl.when(pl.program_id(0) == 0)
            def _():
                dgn_ref[...] = jnp.zeros_like(dgn_ref)
                if nb_gate:
                    cs_ref[...] = jnp.zeros_like(cs_ref)

        if n_pay:
            @pl.when(pl.program_id(0) == 0)
            def _():
                _Plan(bcasts, pay_src, pay_dst, *sems).start()

            @pl.when(pl.program_id(0) == relay_step)
            def _():
                _Plan(bcasts, pay_src, pay_dst, *sems).pass_on()

        av = None if w_form == "kslots_t" else a_ref[...]
        if a_sq:
            av = av * av
        for ci in range(n_chunks):
            cols = slice(ci * nc, (ci + 1) * nc)
            if w_form == "full":
                acc = jnp.dot(av, w_ref[:, cols], preferred_element_type=F32)
            elif w_form == "full_t":
                acc = lax.dot_general(av, w_ref[cols, :], nt, preferred_element_type=F32)
            elif w_form == "nslots":
                acc = jnp.dot(av, w_ref[ci], preferred_element_type=F32)
            else:
                acc = lax.dot_general(a_ref[:, 0:ks], w_ref[0], nt, preferred_element_type=F32)
                for j in range(1, N_DEV):
                    acc = acc + lax.dot_general(a_ref[:, j * ks:(j + 1) * ks], w_ref[j], nt,
                                                preferred_element_type=F32)
            if has_bias:
                acc = acc + b_ref[:, cols]
            if epi == "resid":
                xn = x_ref[:, cols] + g_ref[:, cols] * acc
                o_ref[:, cols] = xn
                y_ref[:, cols] = acc.astype(BF16)
                if norm is not None:
                    r = lax.rsqrt(jnp.mean(xn * xn, axis=-1, keepdims=True) + EPS)
                    h_ref[...] = ((xn * r * gn_ref[...]) * (1.0 + sc_ref[...]) + sh_ref[...]).astype(BF16)
            elif epi == "relu":
                o_ref[:, cols] = jnp.maximum(acc, 0.0).astype(out_dtype)
            elif epi == "relu2d":
                o_ref[:, cols] = (acc * (2.0 * u_ref[:, cols].astype(F32))).astype(out_dtype)
            elif epi == "normbwd":
                xv = nx_ref[...]
                gn = ngn_ref[...]
                r = lax.rsqrt(jnp.mean(xv * xv, axis=-1, keepdims=True) + EPS)
                xhat = xv * r
                dsh_ref[...] += jnp.sum(acc, axis=0, keepdims=True)
                dsc_ref[...] += jnp.sum(acc * (xhat * gn), axis=0, keepdims=True)
                dn = acc * (1.0 + nsc_ref[...])
                dgn_ref[...] += jnp.sum(dn * xhat, axis=0, keepdims=True)
                dxhat = dn * gn
                dx = ndx_ref[...] + r * (dxhat - xhat * jnp.mean(dxhat * xhat, axis=-1, keepdims=True))
                o_ref[...] = dx
                if nb_gate:
                    dy = dx * ngt_ref[...]
                    dy_ref[...] = dy.astype(BF16)
                    dgate_ref[...] += jnp.sum(dx * ny_ref[...].astype(F32), axis=0, keepdims=True)
                    cs_ref[...] += jnp.sum(dy, axis=0, keepdims=True)
            elif epi == "lnbwd":
                cv = cv_ref[...]
                lg = lg_ref[...]
                xc = cv - jnp.mean(cv, axis=-1, keepdims=True)
                rstd = lax.rsqrt(jnp.mean(xc * xc, axis=-1, keepdims=True) + EPS)
                xhat = xc * rstd
                ln = xhat * lg + lb_ref[...]
                sg = _sigmoid(ln)
                dln = acc * (sg * (1.0 + ln * (1.0 - sg)))
                dlg_ref[...] += jnp.sum(dln * xhat, axis=0, keepdims=True)
                dlb_ref[...] += jnp.sum(dln, axis=0, keepdims=True)
                dxhat = dln * lg
                dcv = rstd * (dxhat - jnp.mean(dxhat, axis=-1, keepdims=True)
                              - xhat * jnp.mean(dxhat * xhat, axis=-1, keepdims=True))
                o_ref[...] = dcv
                dcs_ref[...] += jnp.sum(dcv, axis=0, keepdims=True)
            else:
                o_ref[:, cols] = acc.astype(out_dtype)

        if n_pay:
            @pl.when(pl.program_id(0) == steps - 1)
            def _():
                _Plan(bcasts, pay_src, pay_dst, *sems).finish()

    args = [a, w]
    w_block = w.shape
    specs = [pl.BlockSpec((tm, K), lambda i: (i, 0)), pl.BlockSpec(w_block, lambda i: (0,) * len(w_block))]
    if has_bias:
        args.append(bias.reshape(1, N).astype(F32))
        specs.append(pl.BlockSpec((1, N), lambda i: (0, 0)))
    row_spec = pl.BlockSpec((tm, N), lambda i: (i, 0))
    if epi == "resid":
        tpb = rows_per_batch // tm
        vec_spec = pl.BlockSpec((None, 1, N), lambda i: (i // tpb, 0, 0))
        args += [x, gate]
        specs += [row_spec, vec_spec]
        out_shape = [jax.ShapeDtypeStruct((M, N), F32), jax.ShapeDtypeStruct((M, N), BF16)]
        out_specs = [row_spec, row_spec]
        if norm is not None:
            assert n_chunks == 1
            args += list(norm)
            specs += [pl.BlockSpec((1, N), lambda i: (0, 0)), vec_spec, vec_spec]
            out_shape.append(jax.ShapeDtypeStruct((M, N), BF16))
            out_specs.append(row_spec)
    elif epi == "normbwd":
        assert n_chunks == 1
        vec_spec = pl.BlockSpec((None, 1, N), lambda i: (i // tpb_nb, 0, 0))
        one_spec = pl.BlockSpec((1, N), lambda i: (0, 0))
        nbat = M // rows_per_batch
        f_vec, f_one = jax.ShapeDtypeStruct((nbat, 1, N), F32), jax.ShapeDtypeStruct((1, N), F32)
        args += [nb["x"], nb["dx_in"], nb["gnorm"], nb["sc"]]
        specs += [row_spec, row_spec, one_spec, vec_spec]
        out_shape = [jax.ShapeDtypeStruct((M, N), F32), f_vec, f_vec, f_one]
        out_specs = [row_spec, vec_spec, vec_spec, one_spec]
        if nb_gate:
            args += list(nb["gate"])
            specs += [row_spec, vec_spec]
            out_shape += [jax.ShapeDtypeStruct((M, N), BF16), f_vec, f_one]
            out_specs += [row_spec, vec_spec, one_spec]
    elif epi == "lnbwd":
        assert n_chunks == 1
        one_spec = pl.BlockSpec((1, N), lambda i: (0, 0))
        f_one = jax.ShapeDtypeStruct((1, N), F32)
        args += list(ln)
        specs += [row_spec, one_spec, one_spec]
        out_shape = [jax.ShapeDtypeStruct((M, N), F32), f_one, f_one, f_one]
        out_specs = [row_spec, one_spec, one_spec, one_spec]
    else:
        if epi == "relu2d":
            args.append(u)
            specs.append(row_spec)
        out_shape = [jax.ShapeDtypeStruct((M, N), out_dtype)]
        out_specs = [row_spec]
    scratch = []
    if n_pay:
        p_arrays, p_in, p_shapes, p_out, scratch = _payload_layout(payload)
        args += p_arrays
        specs += p_in
        out_shape += p_shapes
        out_specs += p_out
    res = pl.pallas_call(
        body,
        name=name,
        grid=(steps,),
        out_shape=tuple(out_shape),
        in_specs=specs,
        out_specs=tuple(out_specs),
        scratch_shapes=scratch,
        compiler_params=_params("arbitrary" if (n_pay or epi in ("normbwd", "lnbwd")) else "parallel"),
    )(*args)
    if epi == "resid":
        n_own = 3 if norm is not None else 2
    elif epi == "normbwd":
        n_own = 7 if nb_gate else 4
    elif epi == "lnbwd":
        n_own = 4
    else:
        n_own = 1
    own = res[0] if n_own == 1 else tuple(res[:n_own])
    return (own, list(res[n_own:])) if n_pay else own


def _mm_tn(name, a, b, *, a_sq=False, out_form="full", halves=False, colsum=False, tt_pref=1024, tk_pref=1024,
           tn_pref=1024, payload=()):
    T, K = a.shape
    N = b.shape[1]
    tt = _pick(T, tt_pref, 16)
    tk = K // N_DEV if out_form == "kslots" else _pick(K, tk_pref, 128)
    tn = N // N_DEV if out_form == "nslots" else _pick(N, tn_pref, 128)
    nt_steps = T // tt
    grid = (K // tk, N // tn, nt_steps)
    slots = out_form != "full"
    assert not colsum or K == tk
    n_pay = len(payload)
    bcasts = [bc for _, bc in payload]

    def body(*refs):
        it = iter(refs)
        a_ref, b_ref = next(it), next(it)
        pay_src = [next(it) for _ in range(n_pay)]
        o_ref = next(it)
        o2_ref = next(it) if halves else None
        cs_ref = next(it) if colsum else None
        pay_dst = [next(it) for _ in range(n_pay)]
        acc = next(it) if slots else o_ref
        sems = list(it)
        step = (pl.program_id(0) * grid[1] + pl.program_id(1)) * grid[2] + pl.program_id(2)

        if n_pay:
            @pl.when(step == 0)
            def _():
                _Plan(bcasts, pay_src, pay_dst, *sems).start()

        @pl.when(pl.program_id(2) == 0)
        def _():
            acc[...] = jnp.zeros_like(acc)
            if colsum:
                cs_ref[...] = jnp.zeros_like(cs_ref)

        bv = b_ref[...]
        av = a_ref[...]
        if a_sq:
            av = av * av
        acc[...] += lax.dot_general(av, bv, (((0,), (0,)), ((), ())), preferred_element_type=F32)
        if colsum:
            cs_ref[...] += jnp.sum(bv.astype(F32), axis=0, keepdims=True)

        if slots:
            @pl.when(pl.program_id(2) == nt_steps - 1)
            def _():
                if halves:
                    o_ref[...] = acc[0:tk // 2, :].astype(o_ref.dtype)
                    o2_ref[...] = acc[tk // 2:, :].astype(o2_ref.dtype)
                else:
                    o_ref[...] = acc[...].astype(o_ref.dtype)

        if n_pay:
            @pl.when(step == grid[0] * grid[1] * grid[2] - 1)
            def _():
                plan = _Plan(bcasts, pay_src, pay_dst, *sems)
                plan.pass_on()
                plan.finish()

    if out_form == "full":
        out_shape = [jax.ShapeDtypeStruct((K, N), F32)]
        out_specs = [pl.BlockSpec((tk, tn), lambda k, n, t: (k, n))]
    else:
        assert not halves or out_form == "kslots" or K == tk
        rows = tk // 2 if halves else tk
        if out_form == "nslots":
            shape, spec = (N_DEV, K // tk * rows, tn), pl.BlockSpec((None, rows, tn), lambda k, n, t: (n, k, 0))
        else:
            shape, spec = (N_DEV, rows, N), pl.BlockSpec((None, rows, tn), lambda k, n, t: (k, 0, n))
        out_shape = [jax.ShapeDtypeStruct(shape, BF16)] * (2 if halves else 1)
        out_specs = [spec] * (2 if halves else 1)
    if colsum:
        out_shape.append(jax.ShapeDtypeStruct((1, N), F32))
        out_specs.append(pl.BlockSpec((1, tn), lambda k, n, t: (0, n)))
    args = [a, b]
    in_specs = [pl.BlockSpec((tt, tk), lambda k, n, t: (t, k)), pl.BlockSpec((tt, tn), lambda k, n, t: (t, n))]
    scratch = [pltpu.VMEM((tk, tn), F32)] if slots else []
    n_own = len(out_shape)
    if n_pay:
        p_arrays, p_in, p_shapes, p_out, p_scratch = _payload_layout(payload)
        args += p_arrays
        in_specs += p_in
        out_shape += p_shapes
        out_specs += p_out
        scratch += p_scratch
    res = pl.pallas_call(
        body,
        name=name,
        grid=grid,
        out_shape=tuple(out_shape),
        in_specs=in_specs,
        out_specs=tuple(out_specs),
        scratch_shapes=scratch,
        compiler_params=_params(*(("arbitrary",) * 3 if n_pay else ("parallel", "parallel", "arbitrary"))),
    )(*args)
    own = res[0] if n_own == 1 else tuple(res[:n_own])
    return (own, list(res[n_own:])) if n_pay else own


def _normmod_fwd(name, x, gnorm, sc, sh, S, tm_pref=512):
    T, D = x.shape
    tm = _pick(S, tm_pref, 16)
    tpb = S // tm

    def body(x_ref, g_ref, sc_ref, sh_ref, o_ref):
        xv = x_ref[...]
        r = lax.rsqrt(jnp.mean(xv * xv, axis=-1, keepdims=True) + EPS)
        n = xv * r * g_ref[...]
        o_ref[...] = (n * (1.0 + sc_ref[...]) + sh_ref[...]).astype(BF16)

    row = pl.BlockSpec((tm, D), lambda i: (i, 0))
    vec = pl.BlockSpec((None, 1, D), lambda i: (i // tpb, 0, 0))
    return pl.pallas_call(
        body,
        name=name,
        grid=(T // tm,),
        out_shape=jax.ShapeDtypeStruct((T, D), BF16),
        in_specs=[row, pl.BlockSpec((1, D), lambda i: (0, 0)), vec, vec],
        out_specs=row,
        compiler_params=_params("parallel"),
    )(x, gnorm, sc, sh)


ATT_ROWS = GROUP * ATT_BLOCK
ATT_SPAN = 2 * ATT_BLOCK
ATT_SCALE = HEAD_DIM ** -0.5


def _attn_table(sinks):
    slopes = jnp.asarray(np.array([2.0 ** (-8.0 * (h + 1) / N_HEADS) for h in range(N_HEADS)], np.float32))
    r = jnp.arange(ATT_BLOCK)[:, None]
    cc = jnp.arange(ATT_SPAN)[None, :]
    dist = r + ATT_BLOCK - cc
    ok = (dist >= 0) & (dist < ATT_BLOCK)
    tab = jnp.where(ok[None], -slopes[:, None, None] * dist.astype(F32)[None], NEG_BIG)
    tab = jnp.where((cc == 0)[None], sinks.astype(F32)[:, None, None], tab)
    return tab.reshape(N_KV_HEADS, ATT_ROWS, ATT_SPAN)


KV_COLS = N_KV_HEADS * HEAD_DIM


PAIR = 2 * HEAD_DIM


def _pair_tables(tab):
    hkv, rows, span = tab.shape
    pairs = rows // (2 * ATT_BLOCK)
    return tab.reshape(hkv, pairs, 2, ATT_BLOCK, span).transpose(0, 2, 1, 3, 4).reshape(hkv, 2, pairs * ATT_BLOCK, span)


def _stack_pairs(ref, pairs, col0):
    return jnp.concatenate([ref[:, col0 + j * PAIR:col0 + (j + 1) * PAIR] for j in range(pairs)], axis=0)


def _unstack_pairs(v, pairs):
    return jnp.concatenate([v[j * ATT_BLOCK:(j + 1) * ATT_BLOCK, :] for j in range(pairs)], axis=1)


def _swap_halves(x):
    return jnp.concatenate([x[:, HEAD_DIM:], x[:, :HEAD_DIM]], axis=1)


def _load_span(prev_ref, cur_ref, buf_e, buf_o, kv, mult):
    lane = lax.broadcasted_iota(jnp.int32, (ATT_BLOCK, PAIR), 1)
    row = lax.broadcasted_iota(jnp.int32, (ATT_BLOCK, PAIR), 0)
    own = lane // HEAD_DIM == kv
    for r0, ref, first in ((0, prev_ref, True), (ATT_BLOCK, cur_ref, False)):
        v = ref[...]
        if mult != 1.0:
            v = v * mult
        zero = jnp.zeros_like(v)
        mine = jnp.where(own, v, zero)
        if first:
            mine = jnp.where(row > 0, mine, zero)
        other = _swap_halves(mine)
        buf_e[r0:r0 + ATT_BLOCK, :] = mine if kv == 0 else other
        buf_o[r0:r0 + ATT_BLOCK, :] = other if kv == 0 else mine


def _first_block_penalty(i):
    col = lax.broadcasted_iota(jnp.int32, (1, ATT_SPAN), 1)
    return jnp.where((col < ATT_BLOCK) & (col > 0), jnp.where(i > 0, 0.0, NEG_BIG), 0.0).astype(F32)


def _attn_probs(qp, kbuf, bias, first_pen):
    nt = (((1,), (1,)), ((), ()))
    s = lax.dot_general(qp, kbuf, nt, preferred_element_type=F32) + bias + first_pen
    m = jnp.max(s, axis=-1, keepdims=True)
    e = jnp.exp(s - m)
    return e * (1.0 / jnp.sum(e, axis=-1, keepdims=True))


def _attn_specs(nb, q_dim, pairs, row_q, row_cur, row_prev):
    kblk = q_dim // KV_COLS
    qspec = pl.BlockSpec((ATT_BLOCK, q_dim), lambda b, i: (row_q(b, i), 0))
    kv_specs = [pl.BlockSpec((ATT_BLOCK, KV_COLS), (lambda b, i, r=r, c=c: (r(b, i), c)))
                for c in (kblk, kblk + 1) for r in (row_cur, row_prev)]
    tspec = pl.BlockSpec((N_KV_HEADS, 2, pairs * ATT_BLOCK, ATT_SPAN), lambda b, i: (0, 0, 0, 0))
    return qspec, kv_specs, tspec


def _attn_fwd(qkv, tab2, B, S, payload=()):
    assert N_KV_HEADS == 2 and GROUP % 2 == 0
    T = qkv.shape[0]
    pairs = GROUP // 2
    gw = pairs * PAIR
    nb = S // ATT_BLOCK
    q_dim = N_HEADS * HEAD_DIM
    n_pay = len(payload)
    bcasts = [bc for _, bc in payload]
    steps = B * nb

    def body(*refs):
        q_ref, kc_ref, kp_ref, vc_ref, vp_ref, tab_ref = refs[:6]
        pay_src = refs[6:6 + n_pay]
        o_ref = refs[6 + n_pay]
        pay_dst = refs[7 + n_pay:7 + 2 * n_pay]
        ke, ko, ve, vo = refs[7 + 2 * n_pay:11 + 2 * n_pay]
        sems = refs[11 + 2 * n_pay:]
        i = pl.program_id(1)
        if n_pay:
            step = pl.program_id(0) * nb + i

            @pl.when(step == 0)
            def _():
                _Plan(bcasts, pay_src, pay_dst, *sems).start()

            @pl.when(step == (3 * steps) // 4)
            def _():
                _Plan(bcasts, pay_src, pay_dst, *sems).pass_on()

            @pl.when(step == steps - 1)
            def _():
                _Plan(bcasts, pay_src, pay_dst, *sems).finish()

        pen = _first_block_penalty(i)
        for kv in range(N_KV_HEADS):
            _load_span(kp_ref, kc_ref, ke.at[kv], ko.at[kv], kv, ATT_SCALE)
            _load_span(vp_ref, vc_ref, ve.at[kv], vo.at[kv], kv, 1.0)
            qp = _stack_pairs(q_ref, pairs, kv * gw)
            p_e = _attn_probs(qp, ke[kv], tab_ref[kv, 0], pen)
            p_o = _attn_probs(qp, ko[kv], tab_ref[kv, 1], pen)
            o = jnp.dot(p_e.astype(BF16), ve[kv], preferred_element_type=F32)
            o = o + jnp.dot(p_o.astype(BF16), vo[kv], preferred_element_type=F32)
            o_ref[:, kv * gw:(kv + 1) * gw] = _unstack_pairs(o.astype(BF16), pairs)

    qspec, kv_specs, tspec = _attn_specs(nb, q_dim, pairs, lambda b, i: b * nb + i, lambda b, i: b * nb + i,
                                         lambda b, i: b * nb + jnp.maximum(i - 1, 0))
    args, in_specs = [qkv] * 5 + [tab2], [qspec] + kv_specs + [tspec]
    out_shape, out_specs = [jax.ShapeDtypeStruct((T, q_dim), BF16)], [qspec]
    scratch = [pltpu.VMEM((N_KV_HEADS, ATT_SPAN, PAIR), BF16)] * 4
    if n_pay:
        p_arrays, p_in, p_shapes, p_out, p_scratch = _payload_layout(payload)
        args += p_arrays
        in_specs += p_in
        out_shape += p_shapes
        out_specs += p_out
        scratch += p_scratch
    res = pl.pallas_call(
        body,
        name="attn_fwd",
        grid=(B, nb),
        out_shape=tuple(out_shape),
        in_specs=in_specs,
        out_specs=tuple(out_specs),
        scratch_shapes=scratch,
        compiler_params=_params(*(("arbitrary",) * 2 if n_pay else ("parallel", "arbitrary"))),
    )(*args)
    return (res[0], list(res[1:])) if n_pay else res[0]


def _attn_bwd(qkv, o, do, tab2, B, S):
    T = qkv.shape[0]
    hd = HEAD_DIM
    pairs = GROUP // 2
    gw = pairs * PAIR
    rows = pairs * ATT_BLOCK
    nb = S // ATT_BLOCK
    last = nb - 1
    q_dim = N_HEADS * hd
    tn = (((0,), (0,)), ((), ()))
    nt = (((1,), (1,)), ((), ()))

    def body(q_ref, kc_ref, kp_ref, vc_ref, vp_ref, tab_ref, o_ref, do_ref,
             dq_ref, dkv_ref, dsink_ref, carry_k, carry_v, ke, ko, ve, vo):
        i = pl.program_id(1)

        @pl.when(i == 0)
        def _():
            dsink_ref[...] = jnp.zeros_like(dsink_ref)

        def emit(kv, dk_rows, dv_rows):
            dkv_ref[:, kv * hd:(kv + 1) * hd] = dk_rows.astype(BF16)
            dkv_ref[:, KV_COLS + kv * hd:KV_COLS + (kv + 1) * hd] = dv_rows.astype(BF16)

        @pl.when(i < nb)
        def _():
            pen = _first_block_penalty(i)
            low = lax.broadcasted_iota(jnp.int32, (rows, PAIR), 1) < hd
            low_k = lax.broadcasted_iota(jnp.int32, (ATT_SPAN, PAIR), 1) < hd
            keep = lax.broadcasted_iota(jnp.int32, (ATT_SPAN, hd), 0) > 0
            hg = lax.broadcasted_iota(jnp.int32, (pairs, rows), 0)
            hr = lax.broadcasted_iota(jnp.int32, (pairs, rows), 1)
            head_of = jnp.where(hr // ATT_BLOCK == hg, 1.0, 0.0).astype(BF16)
            dks, dvs = [], []
            for kv in range(N_KV_HEADS):
                _load_span(kp_ref, kc_ref, ke.at[kv], ko.at[kv], kv, ATT_SCALE)
                _load_span(vp_ref, vc_ref, ve.at[kv], vo.at[kv], kv, 1.0)
                qp = _stack_pairs(q_ref, pairs, kv * gw)
                dop = _stack_pairs(do_ref, pairs, kv * gw)
                op = _stack_pairs(o_ref, pairs, kv * gw)
                p_e = _attn_probs(qp, ke[kv], tab_ref[kv, 0], pen)
                p_o = _attn_probs(qp, ko[kv], tab_ref[kv, 1], pen)
                prod = dop.astype(F32) * op.astype(F32)
                d_e = jnp.sum(jnp.where(low, prod, 0.0), axis=-1, keepdims=True)
                d_o = jnp.sum(prod, axis=-1, keepdims=True) - d_e
                ds_e = (p_e * (lax.dot_general(dop, ve[kv], nt, preferred_element_type=F32) - d_e)).astype(BF16)
                ds_o = (p_o * (lax.dot_general(dop, vo[kv], nt, preferred_element_type=F32) - d_o)).astype(BF16)
                dq = jnp.dot(ds_e, ke[kv], preferred_element_type=F32) + jnp.dot(ds_o, ko[kv], preferred_element_type=F32)
                dq_ref[:, kv * gw:(kv + 1) * gw] = _unstack_pairs(dq.astype(BF16), pairs)
                dsink_ref[kv, 0] += jnp.dot(head_of, ds_e, preferred_element_type=F32)[:, 0:128]
                dsink_ref[kv, 1] += jnp.dot(head_of, ds_o, preferred_element_type=F32)[:, 0:128]
                dk2 = jnp.where(low_k, lax.dot_general(ds_e, qp, tn, preferred_element_type=F32),
                                lax.dot_general(ds_o, qp, tn, preferred_element_type=F32))
                dv2 = jnp.where(low_k, lax.dot_general(p_e.astype(BF16), dop, tn, preferred_element_type=F32),
                                lax.dot_general(p_o.astype(BF16), dop, tn, preferred_element_type=F32))
                dks.append(jnp.where(keep, (dk2[:, :hd] + dk2[:, hd:]) * ATT_SCALE, 0.0))
                dvs.append(jnp.where(keep, dv2[:, :hd] + dv2[:, hd:], 0.0))

            @pl.when(i > 0)
            def _():
                for kv in range(N_KV_HEADS):
                    emit(kv, carry_k[kv] + dks[kv][0:ATT_BLOCK], carry_v[kv] + dvs[kv][0:ATT_BLOCK])

            for kv in range(N_KV_HEADS):
                carry_k[kv] = dks[kv][ATT_BLOCK:]
                carry_v[kv] = dvs[kv][ATT_BLOCK:]

        @pl.when(i == nb)
        def _():
            for kv in range(N_KV_HEADS):
                emit(kv, carry_k[kv], carry_v[kv])

    qspec, kv_specs, tspec = _attn_specs(nb, q_dim, pairs, lambda b, i: b * nb + jnp.minimum(i, last),
                                         lambda b, i: b * nb + jnp.minimum(i, last),
                                         lambda b, i: b * nb + jnp.clip(i - 1, 0, last))
    dkv = pl.BlockSpec((ATT_BLOCK, 2 * KV_COLS), lambda b, i: (b * nb + jnp.maximum(i - 1, 0), 0))
    dsk = pl.BlockSpec((None, N_KV_HEADS, 2, pairs, 128), lambda b, i: (b, 0, 0, 0, 0))
    return pl.pallas_call(
        body,
        name="attn_bwd",
        grid=(B, nb + 1),
        out_shape=(
            jax.ShapeDtypeStruct((T, q_dim), BF16),
            jax.ShapeDtypeStruct((T, 2 * KV_COLS), BF16),
            jax.ShapeDtypeStruct((B, N_KV_HEADS, 2, pairs, 128), F32),
        ),
        in_specs=[qspec] + kv_specs + [tspec, qspec, qspec],
        out_specs=(qspec, dkv, dsk),
        scratch_shapes=[pltpu.VMEM((N_KV_HEADS, ATT_BLOCK, hd), F32), pltpu.VMEM((N_KV_HEADS, ATT_BLOCK, hd), F32)]
        + [pltpu.VMEM((N_KV_HEADS, ATT_SPAN, PAIR), BF16)] * 4,
        compiler_params=_params("arbitrary", "arbitrary"),
    )(qkv, qkv, qkv, qkv, qkv, tab2, o, do)


def _conv_tiles(S):
    ts = _pick(S, 256, CONV_HALO)
    return ts, S // ts


def _conv_chunks(C, ts):
    lane = _pick(C, 128, 128)
    return lane, C // lane, _pick(ts, 128, 8)


def _conv_weight_chunks(w_dw, C):
    lane = _pick(C, 128, 128)
    w = jnp.pad(w_dw, ((0, CONV_HALO - CONV_WIDTH), (0, 0)))
    return w.reshape(CONV_HALO, C // lane, lane).transpose(1, 0, 2)


def _conv_fwd(u, w3, b_dw, ln_g, ln_b, S):
    T, C2 = u.shape
    C = C2 // 2
    B = T // S
    ts, nj = _conv_tiles(S)
    hb = ts // CONV_HALO
    lane, nc, rc = _conv_chunks(C, ts)

    def body(a_ref, g_ref, ap_ref, gp_ref, w_ref, bdw_ref, lg_ref, lb_ref, cv_ref, s_ref, buf, cvb):
        j = pl.program_id(1)
        glu_prev = ap_ref[...] * _sigmoid(gp_ref[...]) * (j > 0).astype(F32)
        glu = a_ref[...] * _sigmoid(g_ref[...])
        for cc in range(nc):
            buf[cc, 0:CONV_HALO, :] = glu_prev[:, cc * lane:(cc + 1) * lane]
            buf[cc, CONV_HALO:, :] = glu[:, cc * lane:(cc + 1) * lane]

        def chunk(cc, carry):
            for r0 in range(0, ts, rc):
                acc = jnp.zeros((rc, lane), F32)
                for kk in range(CONV_WIDTH):
                    lo = CONV_HALO - (CONV_WIDTH - 1 - kk) + r0
                    acc = acc + w_ref[cc, kk:kk + 1, :] * buf[cc, lo:lo + rc, :]
                cvb[cc, r0:r0 + rc, :] = acc
            return carry

        lax.fori_loop(0, nc, chunk, 0)
        for cc in range(nc):
            cv_ref[:, cc * lane:(cc + 1) * lane] = cvb[cc] + bdw_ref[:, cc * lane:(cc + 1) * lane]
        cv = cv_ref[...]
        mu = jnp.mean(cv, axis=-1, keepdims=True)
        xc = cv - mu
        rstd = lax.rsqrt(jnp.mean(xc * xc, axis=-1, keepdims=True) + EPS)
        ln = xc * rstd * lg_ref[...] + lb_ref[...]
        s_ref[...] = (ln * _sigmoid(ln)).astype(BF16)

    a_cur = pl.BlockSpec((ts, C), lambda b, j: (b * nj + j, 0))
    g_cur = pl.BlockSpec((ts, C), lambda b, j: (b * nj + j, 1))
    a_prev = pl.BlockSpec((CONV_HALO, C), lambda b, j: (jnp.maximum((b * nj + j) * hb - 1, 0), 0))
    g_prev = pl.BlockSpec((CONV_HALO, C), lambda b, j: (jnp.maximum((b * nj + j) * hb - 1, 0), 1))
    wspec = pl.BlockSpec((nc, CONV_HALO, lane), lambda b, j: (0, 0, 0))
    one = pl.BlockSpec((1, C), lambda b, j: (0, 0))
    return pl.pallas_call(
        body,
        name="conv_fwd",
        grid=(B, nj),
        out_shape=(jax.ShapeDtypeStruct((T, C), F32), jax.ShapeDtypeStruct((T, C), BF16)),
        in_specs=[a_cur, g_cur, a_prev, g_prev, wspec, one, one, one],
        out_specs=(a_cur, a_cur),
        scratch_shapes=[pltpu.VMEM((nc, CONV_HALO + ts, lane), F32), pltpu.VMEM((nc, ts, lane), F32)],
        compiler_params=_params("parallel", "arbitrary"),
    )(u, u, u, u, w3, b_dw, ln_g, ln_b)


def _conv_bwd(dcv, u, w3, S):
    T, C2 = u.shape
    C = C2 // 2
    B = T // S
    ts, nj = _conv_tiles(S)
    hb = ts // CONV_HALO
    n_halo_blocks = T // CONV_HALO
    lane, nc, rc = _conv_chunks(C, ts)

    def body(dcv_ref, dnx_ref, a_ref, g_ref, ap_ref, gp_ref, w_ref, du_ref, dw_ref, gbuf, dbuf, dglu, dw8):
        b, j = pl.program_id(0), pl.program_id(1)

        @pl.when((b == 0) & (j == 0))
        def _():
            dw8[...] = jnp.zeros_like(dw8)

        a = a_ref[...]
        sg = _sigmoid(g_ref[...])
        glu_prev = ap_ref[...] * _sigmoid(gp_ref[...]) * (j > 0).astype(F32)
        glu = a * sg
        dcur = dcv_ref[...]
        dnext = dnx_ref[...] * (j < nj - 1).astype(F32)
        for cc in range(nc):
            cols = slice(cc * lane, (cc + 1) * lane)
            gbuf[cc, 0:CONV_HALO, :] = glu_prev[:, cols]
            gbuf[cc, CONV_HALO:, :] = glu[:, cols]
            dbuf[cc, 0:ts, :] = dcur[:, cols]
            dbuf[cc, ts:, :] = dnext[:, cols]

        def chunk(cc, carry):
            for r0 in range(0, ts, rc):
                acc = jnp.zeros((rc, lane), F32)
                for kk in range(CONV_WIDTH):
                    d = CONV_WIDTH - 1 - kk
                    acc = acc + w_ref[cc, kk:kk + 1, :] * dbuf[cc, r0 + d:r0 + d + rc, :]
                dglu[cc, r0:r0 + rc, :] = acc
            for kk in range(CONV_WIDTH):
                d = CONV_WIDTH - 1 - kk
                p = jnp.zeros((rc, lane), F32)
                for r0 in range(0, ts, rc):
                    lo = CONV_HALO - d + r0
                    p = p + dbuf[cc, r0:r0 + rc, :] * gbuf[cc, lo:lo + rc, :]
                dw8[cc, kk * 8:(kk + 1) * 8, :] += jnp.sum(p.reshape(rc // 8, 8, lane), axis=0)
            return carry

        lax.fori_loop(0, nc, chunk, 0)
        for cc in range(nc):
            cols = slice(cc * lane, (cc + 1) * lane)
            dgl = dglu[cc]
            du_ref[:, cc * lane:(cc + 1) * lane] = (dgl * sg[:, cols]).astype(BF16)
            du_ref[:, C + cc * lane:C + (cc + 1) * lane] = (dgl * a[:, cols] * sg[:, cols] * (1.0 - sg[:, cols])).astype(BF16)

        @pl.when((b == B - 1) & (j == nj - 1))
        def _():
            dw_ref[...] = jnp.zeros_like(dw_ref)
            for kk in range(CONV_WIDTH):
                dw_ref[:, kk:kk + 1, :] = jnp.sum(dw8[:, kk * 8:(kk + 1) * 8, :], axis=1, keepdims=True)

    a_cur = pl.BlockSpec((ts, C), lambda b, j: (b * nj + j, 0))
    g_cur = pl.BlockSpec((ts, C), lambda b, j: (b * nj + j, 1))
    a_prev = pl.BlockSpec((CONV_HALO, C), lambda b, j: (jnp.maximum((b * nj + j) * hb - 1, 0), 0))
    g_prev = pl.BlockSpec((CONV_HALO, C), lambda b, j: (jnp.maximum((b * nj + j) * hb - 1, 0), 1))
    d_next = pl.BlockSpec((CONV_HALO, C), lambda b, j: (jnp.minimum((b * nj + j + 1) * hb, n_halo_blocks - 1), 0))
    wspec = pl.BlockSpec((nc, CONV_HALO, lane), lambda b, j: (0, 0, 0))
    return pl.pallas_call(
        body,
        name="conv_bwd",
        grid=(B, nj),
        out_shape=(jax.ShapeDtypeStruct((T, C2), BF16), jax.ShapeDtypeStruct((nc, CONV_HALO, lane), F32)),
        in_specs=[a_cur, d_next, a_cur, g_cur, a_prev, g_prev, wspec],
        out_specs=(pl.BlockSpec((ts, C2), lambda b, j: (b * nj + j, 0)), wspec),
        scratch_shapes=[
            pltpu.VMEM((nc, CONV_HALO + ts, lane), F32),
            pltpu.VMEM((nc, ts + CONV_HALO, lane), F32),
            pltpu.VMEM((nc, ts, lane), F32),
            pltpu.VMEM((nc, CONV_HALO * 8, lane), F32),
        ],
        compiler_params=_params("arbitrary", "arbitrary"),
    )(dcv, dcv, u, u, u, u, w3)


def _loss_head(x, tgt, gfin, y, gate, S, tm_pref=256):
    T, D = x.shape
    tm = _pick(S, tm_pref, 16)
    tpb = S // tm

    def body(x_ref, t_ref, g_ref, y_ref, gt_ref, dx_ref, loss_ref, dg_ref, dy_ref, dgate_ref):
        @pl.when(pl.program_id(0) == 0)
        def _():
            loss_ref[...] = jnp.zeros_like(loss_ref)
            dg_ref[...] = jnp.zeros_like(dg_ref)

        @pl.when(pl.program_id(0) % tpb == 0)
        def _():
            dgate_ref[...] = jnp.zeros_like(dgate_ref)

        xv = x_ref[...]
        g = g_ref[...]
        r = lax.rsqrt(jnp.mean(xv * xv, axis=-1, keepdims=True) + EPS)
        xhat = xv * r
        e = xhat * g - t_ref[...]
        row_loss = jnp.mean(e * e, axis=-1, keepdims=True)
        loss_ref[...] += 0.5 * jnp.sum(row_loss, axis=0, keepdims=True)
        dy = e * (1.0 / D)
        dg_ref[...] += jnp.sum(dy * xhat, axis=0, keepdims=True)
        dxhat = dy * g
        dx = r * (dxhat - xhat * jnp.mean(dxhat * xhat, axis=-1, keepdims=True))
        dx_ref[...] = dx
        dy_ref[...] = (dx * gt_ref[...]).astype(BF16)
        dgate_ref[...] += jnp.sum(dx * y_ref[...].astype(F32), axis=0, keepdims=True)

    row = pl.BlockSpec((tm, D), lambda i: (i, 0))
    vec = pl.BlockSpec((None, 1, D), lambda i: (i // tpb, 0, 0))
    one = pl.BlockSpec((1, D), lambda i: (0, 0))
    return pl.pallas_call(
        body,
        name="loss_head",
        grid=(T // tm,),
        out_shape=(
            jax.ShapeDtypeStruct((T, D), F32),
            jax.ShapeDtypeStruct((8, 128), F32),
            jax.ShapeDtypeStruct((1, D), F32),
            jax.ShapeDtypeStruct((T, D), BF16),
            jax.ShapeDtypeStruct((T // S, 1, D), F32),
        ),
        in_specs=[row, row, one, row, vec],
        out_specs=(row, pl.BlockSpec((8, 128), lambda i: (0, 0)), one, row, vec),
        compiler_params=_params("arbitrary"),
    )(x, tgt, gfin, y, gate)


def _adam(name, parts, w, m, v, tm_pref=256):
    L = len(parts)
    P, R, C = parts[0].shape
    fit = VMEM_LIMIT_BYTES * 5 // 8 // (2 * L * P * C * parts[0].dtype.itemsize)
    tm = _pick(R, max(16, min(tm_pref, fit // 16 * 16)), 16)
    tiles = R // tm
    c1 = 1.0 - ADAM_B1 ** ADAM_STEP
    c2 = 1.0 - ADAM_B2 ** ADAM_STEP

    def body(*refs):
        p_refs = refs[:L]
        w_ref, m_ref, v_ref, g_ref, d_ref, mo_ref, vo_ref = refs[L:]
        for l in range(L):
            @pl.when(pl.program_id(0) == l)
            def _():
                g = p_refs[l][0].astype(F32)
                for i in range(1, P):
                    g = g + p_refs[l][i].astype(F32)
                m_new = ADAM_B1 * m_ref[...] + (1.0 - ADAM_B1) * g
                v_new = ADAM_B2 * v_ref[...] + (1.0 - ADAM_B2) * (g * g)
                m_hat = m_new / c1
                v_hat = v_new / c2
                g_ref[...] = g
                d_ref[...] = -ADAM_LR * (m_hat / (jnp.sqrt(v_hat) + ADAM_EPS) + ADAM_WD * w_ref[...])
                mo_ref[...] = m_new
                vo_ref[...] = v_new

    def p_spec(l):
        return pl.BlockSpec((P, tm, C), lambda li, t: (0, jnp.where(li == l, t, jnp.where(li < l, 0, tiles - 1)), 0))

    row = pl.BlockSpec((tm, C), lambda li, t: (li * tiles + t, 0))
    return pl.pallas_call(
        body,
        name=name,
        grid=(L, tiles),
        out_shape=(jax.ShapeDtypeStruct((L * R, C), F32),) * 4,
        in_specs=[p_spec(l) for l in range(L)] + [row, row, row],
        out_specs=(row, row, row, row),
        compiler_params=_params("arbitrary", "arbitrary"),
    )(*parts, w, m, v)


BIG = ("w_qkv", "w_o", "w_pw1", "w_pw2", "w_up", "w_down")
SMALL_SHARDED = (("b_pw1", 1), ("w_dw", 2), ("b_dw", 1), ("conv_ln_g", 1), ("conv_ln_b", 1), ("b_pw2", 1))
SMALL_REPL = ("b_mod", "norm_mix", "norm_mlp", "b_qkv", "b_o", "sinks", "final_norm")
WEIGHTS = ("w_mod", "b_mod", "norm_mix", "norm_mlp", "w_qkv", "b_qkv", "w_o", "b_o", "sinks", "w_pw1", "b_pw1",
           "w_dw", "b_dw", "conv_ln_g", "conv_ln_b", "w_pw2", "b_pw2", "w_up", "w_down", "final_norm")


def _step(x, c, loss_target, W, M, V):
    B, S, D = x.shape
    T = B * S
    L = W["w_mod"].shape[0]
    n_mod = W["w_mod"].shape[2]
    me = 4 * lax.axis_index("x") + 2 * lax.axis_index("y") + lax.axis_index("c")

    Wb = {n: W[n].astype(BF16) for n in BIG}

    def shards(i):
        jm = i // 2
        first, last = ("w_qkv", "w_o") if i % 2 == 0 else ("w_pw1", "w_pw2")
        return [Wb[first][jm], Wb[last][jm], Wb["w_up"][i], Wb["w_down"][i]]

    def carried(res, payload):
        return res if payload else (res, [])

    small_names = [n for n, _ in SMALL_SHARDED]
    small_src, small_sizes = _pack([c] + [W[n] for n in small_names], F32, 0)
    got = _exchange("gather_first", [(small_src, True), (shards(0)[0], True)])
    small_parts = _unpack(got[0], small_sizes, [c.shape] + [W[n].shape for n in small_names], 1)
    c_all = small_parts[0].reshape(N_DEV * B, D)
    full = {n: _from_slots(p, d) for (n, d), p in zip(SMALL_SHARDED, small_parts[1:])}
    gathered = [[got[1], None, None, None]] + [None] * (L - 1)

    b_mod_mine = lax.dynamic_slice_in_dim(W["b_mod"], me * n_mod, n_mod, axis=1).reshape(L, 1, n_mod)
    mod_part = _mod_fwd(c_all, W["w_mod"].astype(BF16), b_mod_mine)
    mod_slots = mod_part.reshape(L, N_DEV, B, n_mod).transpose(1, 0, 2, 3).reshape(N_DEV, L * B, n_mod)
    mod_recv = _exchange("scatter_mod", [(mod_slots, False)])[0]
    mod = mod_recv.reshape(N_DEV, L, B, n_mod).transpose(1, 2, 0, 3).reshape(L, B, N_MOD, 1, D)

    w_dw3 = [_conv_weight_chunks(full["w_dw"][j], D) for j in range(full["w_dw"].shape[0])]

    xc = x.reshape(T, D)
    saved = []
    h1 = _normmod_fwd("normmod_mix_fwd", xc, W["norm_mix"][0][None], mod[0, :, 1], mod[0, :, 0], S)
    for i in range(L):
        jm = i // 2
        sh1, sc1, g1, sh2, sc2, g2 = (mod[i, :, t] for t in range(N_MOD))
        nxt = [[(s, True)] for s in shards(i + 1)] if i + 1 < L else [[]] * 4
        g_first, g_last, g_up, g_down = gathered[i]
        sv = {"x_in": xc, "h1": h1}
        if i % 2 == 0:
            w_qkv_full = _from_slots(g_first, 1)
            sv["w_first"] = w_qkv_full
            mine = [(s, True) for s in shards(0)[1:]] if i == 0 else []
            pay = nxt[0] + mine[:1]
            qkv, n_first = carried(_mm("qkv_fwd", h1, w_qkv_full, bias=W["b_qkv"][jm], tm_pref=1024, payload=pay), pay)
            tab = _pair_tables(_attn_table(W["sinks"][jm]))
            mix_in, n_attn = carried(_attn_fwd(qkv, tab, B, S, payload=mine[1:]), mine[1:])
            if i == 0:
                g_last, g_up, g_down = n_first[-1], n_attn[0], n_attn[1]
            sv.update(qkv=qkv, tab=tab)
            b_out = W["b_o"][jm]
        else:
            sv["w_first"] = g_first
            u, n_first = carried(_mm("pw1_fwd", h1, g_first, w_form="nslots", bias=full["b_pw1"][jm], out_dtype=F32,
                                     payload=nxt[0]), nxt[0])
            cv, mix_in = _conv_fwd(u, w_dw3[jm], full["b_dw"][jm][None], full["conv_ln_g"][jm][None],
                                   full["conv_ln_b"][jm][None], S)
            sv.update(u=u, cv=cv)
            b_out = full["b_pw2"][jm]
        w_last = g_last.reshape(-1, D)
        w_down_full = g_down.reshape(-1, D)
        sv.update(mix_in=mix_in, w_last=w_last, g_up=g_up, w_down=w_down_full)
        (x1, y1, h2), n_last = carried(_mm("mix_out_fwd", mix_in, w_last, bias=b_out, epi="resid", x=xc, gate=g1,
                                           norm=(W["norm_mlp"][i][None], sc2, sh2), rows_per_batch=S,
                                           tm_pref=1024, payload=nxt[1]), nxt[1])
        sv.update(y1=y1, x1=x1)
        up, n_up = carried(_mm("mlp_up_fwd", h2, g_up, w_form="nslots", epi="relu", payload=nxt[2]), nxt[2])
        norm_next = (W["norm_mix"][i + 1][None], mod[i + 1, :, 1], mod[i + 1, :, 0]) if i + 1 < L else None
        res, n_down = carried(_mm("mlp_down_fwd", up, w_down_full, a_sq=True, epi="resid", x=x1, gate=g2,
                                  norm=norm_next, rows_per_batch=S, payload=nxt[3]), nxt[3])
        x2, y2 = res[0], res[1]
        h1 = res[2] if i + 1 < L else None
        sv.update(h2=h2, up=up, y2=y2)
        saved.append(sv)
        xc = x2
        if i + 1 < L:
            gathered[i + 1] = [n_first[0], n_last[0], n_up[0], n_down[0]]

    dx, loss_blk, dgfin, dy2, dg2 = _loss_head(xc, loss_target.reshape(T, D), W["final_norm"][None],
                                               saved[L - 1]["y2"], mod[L - 1, :, 5], S)
    loss = lax.psum(loss_blk[0, 0], ("x", "y", "c"))

    G = {"final_norm": dgfin.reshape(D)}
    dmod_layers = [None] * L
    small_grads = ("norm_mix", "norm_mlp", "b_qkv", "b_o", "sinks", "b_pw1", "w_dw", "b_dw", "conv_ln_g", "conv_ln_b",
                   "b_pw2")
    acc = {n: [None] * W[n].shape[0] for n in small_grads}
    reduced = {n: [None] * W[n].shape[0] for n in BIG}

    waiting_up = None
    for i in reversed(range(L)):
        jm = i // 2
        sv = saved[i]
        sh1, sc1, g1, sh2, sc2, g2 = (mod[i, :, t] for t in range(N_MOD))
        pay = [(waiting_up[1], False)] if waiting_up else []
        (gd0, gd1), r = carried(_mm_tn("w_down_grad", sv["up"], dy2, a_sq=True, out_form="kslots", halves=True,
                                       tt_pref=4096, payload=pay), pay)
        if waiting_up:
            reduced["w_up"][waiting_up[0]].append(r[0])
        du, r0 = _mm("mlp_down_bwd", dy2, sv["w_down"], w_form="full_t", epi="relu2d", u=sv["up"],
                     payload=[(gd0, False)])
        (gu0, gu1), r1 = _mm_tn("w_up_grad", sv["h2"], du, out_form="nslots", halves=True, tt_pref=4096,
                                payload=[(gd1, False)])
        reduced["w_down"][i] = [r0[0], r1[0]]
        pay = [(gu0, False)] + ([] if i > 0 else [(gu1, False)])
        (dx, dsh2, dsc2, dgn, dy1, dg1, dy1_sum), r = _mm(
            "mlp_up_bwd", du, sv["g_up"], w_form="kslots_t", epi="normbwd", rows_per_batch=S,
            nb=dict(x=sv["x1"], dx_in=dx, gnorm=W["norm_mlp"][i][None], sc=sc2, gate=(sv["y1"], g1)),
            payload=pay)
        reduced["w_up"][i] = list(r)
        waiting_up = (i, gu1) if i > 0 else None
        acc["norm_mlp"][i] = dgn.reshape(D)
        gw_last = _mm_tn("mix_out_grad", sv["mix_in"], dy1)
        gw_last = gw_last.reshape((N_DEV, -1) + gw_last.shape[1:]).astype(BF16)
        below = (saved[i - 1]["y2"], mod[i - 1, :, 5]) if i > 0 else None
        nb_mix = dict(x=sv["x_in"], dx_in=dx, gnorm=W["norm_mix"][i][None], sc=sc1, gate=below)
        if i % 2 == 0:
            acc["b_o"][jm] = dy1_sum.reshape(D)
            do, r = _mm("attn_out_bwd", dy1, sv["w_last"], w_form="full_t", payload=[(gw_last, False)])
            reduced["w_o"][jm] = r[0]
            dq, dkv, dsk = _attn_bwd(sv["qkv"], sv["mix_in"], do, sv["tab"], B, S)
            acc["sinks"][jm] = jnp.sum(dsk[..., 0], axis=0).transpose(0, 2, 1).reshape(N_HEADS)
            dqkv = jnp.concatenate([dq, dkv], axis=1)
            gw_qkv, db_qkv = _mm_tn("w_qkv_grad", sv["h1"], dqkv, colsum=True, tt_pref=4096)
            acc["b_qkv"][jm] = db_qkv.reshape(-1)
            gw_qkv = _to_slots(gw_qkv, 1).astype(BF16)
            res, r = _mm("qkv_bwd", dqkv, sv["w_first"], w_form="full_t", epi="normbwd", nb=nb_mix, rows_per_batch=S,
                         payload=[(gw_qkv, False)])
            reduced["w_qkv"][jm] = r[0]
        else:
            acc["b_pw2"][jm] = dy1_sum.reshape(D)
            (dcv, dlg, dlb, dbdw), r = _mm(
                "pw2_bwd", dy1, sv["w_last"], w_form="full_t", epi="lnbwd",
                ln=(sv["cv"], full["conv_ln_g"][jm][None], full["conv_ln_b"][jm][None]), payload=[(gw_last, False)])
            reduced["w_pw2"][jm] = r[0]
            acc["conv_ln_g"][jm], acc["conv_ln_b"][jm], acc["b_dw"][jm] = dlg.reshape(-1), dlb.reshape(-1), dbdw.reshape(-1)
            du1, dwdw = _conv_bwd(dcv, sv["u"], w_dw3[jm], S)
            acc["w_dw"][jm] = dwdw.transpose(1, 0, 2).reshape(CONV_HALO, D)[:CONV_WIDTH]
            gw_pw1, db_pw1 = _mm_tn("w_pw1_grad", sv["h1"], du1, out_form="nslots", colsum=True, tt_pref=4096)
            acc["b_pw1"][jm] = db_pw1.reshape(-1)
            res, r = _mm("pw1_bwd", du1, sv["w_first"], w_form="kslots_t", epi="normbwd", nb=nb_mix, rows_per_batch=S,
                         payload=[(gw_pw1, False)])
            reduced["w_pw1"][jm] = r[0]
        dx, dsh1, dsc1, dgn = res[:4]
        acc["norm_mix"][i] = dgn.reshape(D)
        dmod_layers[i] = jnp.concatenate([dsh1, dsc1, dg1, dsh2, dsc2, dg2], axis=1).reshape(B, N_MOD * D)
        if i > 0:
            dy2, dg2 = res[4], res[5]
    grad_x = dx.reshape(B, S, D)
    for n, parts in acc.items():
        G[n] = jnp.stack(parts)

    dmod = jnp.stack(dmod_layers)
    dmod_slots = dmod.reshape(L, B, N_DEV, n_mod).transpose(2, 0, 1, 3).reshape(N_DEV, L * B, n_mod)
    dmod_recv = _exchange("gather_dmod", [(dmod_slots, False)])[0]
    dmod_all = dmod_recv.reshape(N_DEV, L, B, n_mod).transpose(1, 0, 2, 3).reshape(L, N_DEV * B, n_mod)
    g_w_mod, db_mod_mine = _mod_bwd(c_all, dmod_all)
    G["b_mod"] = lax.dynamic_update_slice_in_dim(jnp.zeros_like(W["b_mod"]), db_mod_mine.reshape(L, n_mod),
                                                 me * n_mod, axis=1)

    small_items = [jnp.broadcast_to(G[n][None], (N_DEV,) + G[n].shape) for n in SMALL_REPL]
    small_items += [_to_slots(G[n], d) for n, d in SMALL_SHARDED]
    small_slots, small_sizes2 = _pack(small_items, F32, 1)
    small_recv = _exchange("reduce_small", [(small_slots, False)])[0]

    out = {}

    def run_adam(name, parts, names):
        shapes = [W[n].shape for n in names]
        wp, sizes = _pack([W[n] for n in names], F32, 0)
        mp, _ = _pack([M[n] for n in names], F32, 0)
        vp, _ = _pack([V[n] for n in names], F32, 0)
        res = _adam(name, [parts], wp, mp, vp)
        for kind, buf in zip(("grad", "delta", "new_m", "new_v"), res):
            for n, a in zip(names, _unpack(buf, sizes, shapes, 0)):
                out[kind + "_" + n] = a

    for n in BIG + ("w_mod",):
        cols = W[n].shape[-1]
        if n == "w_mod":
            parts = [g_w_mod.reshape(1, -1, cols)]
        else:
            pieces = [p for r in reduced[n] for p in (r if isinstance(r, list) else [r])]
            parts = [p.reshape(N_DEV, -1, cols) for p in pieces]
        res = _adam("adam_" + n, parts, W[n].reshape(-1, cols), M[n].reshape(-1, cols), V[n].reshape(-1, cols))
        for kind, buf in zip(("grad", "delta", "new_m", "new_v"), res):
            out[kind + "_" + n] = buf.reshape(W[n].shape)
    run_adam("adam_small", small_recv, list(SMALL_REPL) + [n for n, _ in SMALL_SHARDED])

    res = [loss, grad_x]
    for kind in ("grad", "delta", "new_m", "new_v"):
        res += [out[kind + "_" + n] for n in WEIGHTS]
    return tuple(res)


def kernel(x, c, w_mod, b_mod, norm_mix, norm_mlp, w_qkv, b_qkv, w_o, b_o, sinks, w_pw1, b_pw1, w_dw, b_dw, conv_ln_g, conv_ln_b, w_pw2, b_pw2, w_up, w_down, final_norm, loss_target, m_w_mod, m_b_mod, m_norm_mix, m_norm_mlp, m_w_qkv, m_b_qkv, m_w_o, m_b_o, m_sinks, m_w_pw1, m_b_pw1, m_w_dw, m_b_dw, m_conv_ln_g, m_conv_ln_b, m_w_pw2, m_b_pw2, m_w_up, m_w_down, m_final_norm, v_w_mod, v_b_mod, v_norm_mix, v_norm_mlp, v_w_qkv, v_b_qkv, v_w_o, v_b_o, v_sinks, v_w_pw1, v_b_pw1, v_w_dw, v_b_dw, v_conv_ln_g, v_conv_ln_b, v_w_pw2, v_b_pw2, v_w_up, v_w_down, v_final_norm):
    W = dict(w_mod=w_mod, b_mod=b_mod, norm_mix=norm_mix, norm_mlp=norm_mlp, w_qkv=w_qkv, b_qkv=b_qkv, w_o=w_o,
             b_o=b_o, sinks=sinks, w_pw1=w_pw1, b_pw1=b_pw1, w_dw=w_dw, b_dw=b_dw, conv_ln_g=conv_ln_g,
             conv_ln_b=conv_ln_b, w_pw2=w_pw2, b_pw2=b_pw2, w_up=w_up, w_down=w_down, final_norm=final_norm)
    M = dict(w_mod=m_w_mod, b_mod=m_b_mod, norm_mix=m_norm_mix, norm_mlp=m_norm_mlp, w_qkv=m_w_qkv, b_qkv=m_b_qkv,
             w_o=m_w_o, b_o=m_b_o, sinks=m_sinks, w_pw1=m_w_pw1, b_pw1=m_b_pw1, w_dw=m_w_dw, b_dw=m_b_dw,
             conv_ln_g=m_conv_ln_g, conv_ln_b=m_conv_ln_b, w_pw2=m_w_pw2, b_pw2=m_b_pw2, w_up=m_w_up,
             w_down=m_w_down, final_norm=m_final_norm)
    V = dict(w_mod=v_w_mod, b_mod=v_b_mod, norm_mix=v_norm_mix, norm_mlp=v_norm_mlp, w_qkv=v_w_qkv, b_qkv=v_b_qkv,
             w_o=v_w_o, b_o=v_b_o, sinks=v_sinks, w_pw1=v_w_pw1, b_pw1=v_b_pw1, w_dw=v_w_dw, b_dw=v_b_dw,
             conv_ln_g=v_conv_ln_g, conv_ln_b=v_conv_ln_b, w_pw2=v_w_pw2, b_pw2=v_b_pw2, w_up=v_w_up,
             w_down=v_w_down, final_norm=v_final_norm)
    return _step(x, c, loss_target, W, M, V)
```

```python
import numpy as np
import jax
import jax.numpy as jnp
from jax import lax
from jax.experimental import pallas as pl
from jax.experimental.pallas import tpu as pltpu

F32 = jnp.float32
BF16 = jnp.bfloat16

N_DEV = 8
N_HEADS = 16
N_KV_HEADS = 2
HEAD_DIM = 64
GROUP = N_HEADS // N_KV_HEADS
ATT_BLOCK = 128
CONV_WIDTH = 31
CONV_HALO = 32
N_MOD = 6
EPS = 1e-6
ADAM_LR = 0.001
ADAM_B1 = 0.9
ADAM_B2 = 0.999
ADAM_EPS = 1e-08
ADAM_WD = 0.01
ADAM_STEP = 10
NEG_BIG = -1e30
PACK_COLS = 1024
VMEM_LIMIT_BYTES = 56 * 1024 * 1024
MESH_ID = pl.DeviceIdType.MESH


def _params(*sem):
    return pltpu.CompilerParams(dimension_semantics=sem, vmem_limit_bytes=VMEM_LIMIT_BYTES)


def _pick(n, pref, mult=8):
    if n <= pref:
        return n
    for t in range(pref, 0, -1):
        if n % t == 0 and t % mult == 0:
            return t
    return n


def _sigmoid(z):
    return 0.5 * jnp.tanh(0.5 * z) + 0.5


def _payload_layout(payload):
    n = len(payload)
    out_shapes = [jax.ShapeDtypeStruct((N_DEV,) + tuple(a.shape if bc else a.shape[1:]), a.dtype) for a, bc in payload]
    hbm = pl.BlockSpec(memory_space=pl.ANY)
    scratch = [pltpu.SemaphoreType.DMA((n * (N_DEV - 1),)), pltpu.SemaphoreType.DMA((n * (N_DEV - 1),)),
               pltpu.SemaphoreType.DMA((n,))]
    return [a for a, _ in payload], [hbm] * n, out_shapes, [hbm] * n, scratch


class _Plan:
    def __init__(self, bcasts, src_refs, dst_refs, send_sems, recv_sems, local_sems):
        x, y, c = lax.axis_index("x"), lax.axis_index("y"), lax.axis_index("c")
        me = 4 * x + 2 * y + c
        self.first, self.landed, self.relay, self.local = [], [], [], []
        for t, (bc, s_ref, d_ref) in enumerate(zip(bcasts, src_refs, dst_refs)):
            def remote(k, src, slot, to):
                sem = t * (N_DEV - 1) + k
                return pltpu.make_async_remote_copy(src_ref=src, dst_ref=d_ref.at[slot], send_sem=send_sems.at[sem],
                                                    recv_sem=recv_sems.at[sem], device_id=to, device_id_type=MESH_ID)
            if bc:
                chips = [(1 - x, y), (x, 1 - y), (1 - x, 1 - y)]
                self.first.append(remote(0, s_ref, me, (x, y, 1 - c)))
                for j, (px, py) in enumerate(chips):
                    cp = remote(1 + j, s_ref, me, (px, py, c))
                    self.first.append(cp)
                    self.landed.append(cp)
                    theirs = 4 * px + 2 * py + c
                    self.relay.append(remote(4 + j, d_ref.at[theirs], theirs, (x, y, 1 - c)))
                self.local.append(pltpu.make_async_copy(s_ref, d_ref.at[me], local_sems.at[t]))
            else:
                for k in range(1, N_DEV):
                    px = 1 - x if (k >> 2) & 1 else x
                    py = 1 - y if (k >> 1) & 1 else y
                    pc = 1 - c if k & 1 else c
                    self.first.append(remote(k - 1, s_ref.at[4 * px + 2 * py + pc], me, (px, py, pc)))
                self.local.append(pltpu.make_async_copy(s_ref.at[me], d_ref.at[me], local_sems.at[t]))

    def start(self):
        for cp in self.first + self.local:
            cp.start()

    def pass_on(self):
        for cp in self.landed:
            cp.wait_recv()
        for cp in self.relay:
            cp.start()

    def finish(self):
        for cp in self.first:
            cp.wait_send()
            if not any(cp is l for l in self.landed):
                cp.wait_recv()
        for cp in self.relay:
            cp.wait()
        for cp in self.local:
            cp.wait()


def _exchange(name, payload):
    n = len(payload)
    bcasts = [bc for _, bc in payload]
    arrays, in_specs, out_shapes, out_specs, scratch = _payload_layout(payload)

    def body(*refs):
        plan = _Plan(bcasts, refs[:n], refs[n:2 * n], *refs[2 * n:])
        plan.start()
        plan.pass_on()
        plan.finish()

    return pl.pallas_call(
        body,
        name=name,
        out_shape=tuple(out_shapes),
        in_specs=in_specs,
        out_specs=tuple(out_specs),
        scratch_shapes=scratch,
    )(*arrays)


def _pack(arrays, dtype, lead):
    lead_shape = arrays[0].shape[:lead]
    flat = [a.astype(dtype).reshape(lead_shape + (-1,)) for a in arrays]
    sizes = [f.shape[-1] for f in flat]
    total = sum(sizes)
    chunk = 16 * PACK_COLS
    padded = -(-total // chunk) * chunk
    if padded > total:
        flat.append(jnp.zeros(lead_shape + (padded - total,), dtype))
    buf = jnp.concatenate(flat, axis=-1)
    return buf.reshape(lead_shape + (padded // PACK_COLS, PACK_COLS)), sizes


def _unpack(buf, sizes, shapes, lead):
    lead_shape = buf.shape[:lead]
    flat = buf.reshape(lead_shape + (-1,))
    out, off = [], 0
    for n, shp in zip(sizes, shapes):
        out.append(lax.slice_in_dim(flat, off, off + n, axis=lead).reshape(lead_shape + tuple(shp)))
        off += n
    return out


def _to_slots(a, dim):
    shp = a.shape
    a = a.reshape(shp[:dim] + (N_DEV, shp[dim] // N_DEV) + shp[dim + 1:])
    return jnp.moveaxis(a, dim, 0)


def _from_slots(a, dim):
    a = jnp.moveaxis(a, 0, dim)
    shp = a.shape
    return a.reshape(shp[:dim] + (shp[dim] * shp[dim + 1],) + shp[dim + 2:])


def _mod_fwd(c_all, w, b):
    L, D, n = w.shape
    M = c_all.shape[0]

    def body(c_ref, w_ref, b_ref, o_ref):
        cv = c_ref[...]
        cs = (cv * _sigmoid(cv)).astype(BF16)
        o_ref[...] = jnp.dot(cs, w_ref[...], preferred_element_type=F32) + b_ref[...]

    return pl.pallas_call(
        body,
        name="mod_fwd",
        grid=(L,),
        out_shape=jax.ShapeDtypeStruct((L, M, n), F32),
        in_specs=[
            pl.BlockSpec((M, D), lambda l: (0, 0)),
            pl.BlockSpec((None, D, n), lambda l: (l, 0, 0)),
            pl.BlockSpec((None, 1, n), lambda l: (l, 0, 0)),
        ],
        out_specs=pl.BlockSpec((None, M, n), lambda l: (l, 0, 0)),
        compiler_params=_params("arbitrary"),
    )(c_all, w, b)


def _mod_bwd(c_all, dmod_all):
    L, M, n = dmod_all.shape
    D = c_all.shape[1]

    def body(c_ref, d_ref, dw_ref, db_ref):
        cv = c_ref[...]
        cs = (cv * _sigmoid(cv)).astype(BF16)
        d = d_ref[...]
        dw_ref[...] = lax.dot_general(cs, d.astype(BF16), (((0,), (0,)), ((), ())), preferred_element_type=F32)
        db_ref[...] = jnp.sum(d, axis=0, keepdims=True)

    return pl.pallas_call(
        body,
        name="mod_bwd",
        grid=(L,),
        out_shape=(jax.ShapeDtypeStruct((L, D, n), F32), jax.ShapeDtypeStruct((L, 1, n), F32)),
        in_specs=[
            pl.BlockSpec((M, D), lambda l: (0, 0)),
            pl.BlockSpec((None, M, n), lambda l: (l, 0, 0)),
        ],
        out_specs=(
            pl.BlockSpec((None, D, n), lambda l: (l, 0, 0)),
            pl.BlockSpec((None, 1, n), lambda l: (l, 0, 0)),
        ),
        compiler_params=_params("arbitrary"),
    )(c_all, dmod_all)


def _mm(name, a, w, *, w_form="full", out_dtype=BF16, bias=None, a_sq=False, epi=None, x=None, gate=None, u=None,
        norm=None, nb=None, ln=None, rows_per_batch=None, tm_pref=512, payload=()):
    M, K = a.shape
    if w_form == "full":
        N = w.shape[1]
        nc = _pick(N, 1024, 128)
    elif w_form == "full_t":
        N = w.shape[0]
        nc = _pick(N, 1024, 128)
    elif w_form == "nslots":
        nc = w.shape[2]
        N = N_DEV * nc
    else:
        N = w.shape[1]
        nc = N
    ks = K // N_DEV
    n_chunks = N // nc
    tm = _pick(M if rows_per_batch is None else rows_per_batch, tm_pref, 16)
    steps = M // tm
    relay_step = (3 * steps) // 4
    nb_gate = nb is not None and nb.get("gate") is not None
    tpb_nb = rows_per_batch // tm if epi == "normbwd" else 1
    has_bias = bias is not None
    n_pay = len(payload)
    bcasts = [bc for _, bc in payload]
    nt = (((1,), (1,)), ((), ()))


    def body(*refs):
        it = iter(refs)
        a_ref = next(it)
        w_ref = next(it)
        b_ref = next(it) if has_bias else None
        x_ref = next(it) if epi == "resid" else None
        g_ref = next(it) if epi == "resid" else None
        gn_ref, sc_ref, sh_ref = (next(it), next(it), next(it)) if norm is not None else (None, None, None)
        u_ref = next(it) if epi == "relu2d" else None
        if epi == "normbwd":
            nx_ref, ndx_ref, ngn_ref, nsc_ref = next(it), next(it), next(it), next(it)
            ny_ref, ngt_ref = (next(it), next(it)) if nb_gate else (None, None)
        if epi == "lnbwd":
            cv_ref, lg_ref, lb_ref = next(it), next(it), next(it)
        pay_src = [next(it) for _ in range(n_pay)]
        o_ref = next(it)
        y_ref = next(it) if epi == "resid" else None
        h_ref = next(it) if norm is not None else None
        if epi == "normbwd":
            dsh_ref, dsc_ref, dgn_ref = next(it), next(it), next(it)
            dy_ref, dgate_ref, cs_ref = (next(it), next(it), next(it)) if nb_gate else (None, None, None)
        if epi == "lnbwd":
            dlg_ref, dlb_ref, dcs_ref = next(it), next(it), next(it)
        pay_dst = [next(it) for _ in range(n_pay)]
        sems = list(it)

        if epi == "lnbwd":
            @pl.when(pl.program_id(0) == 0)
            def _():
                dlg_ref[...] = jnp.zeros_like(dlg_ref)
                dlb_ref[...] = jnp.zeros_like(dlb_ref)
                dcs_ref[...] = jnp.zeros_like(dcs_ref)

        if epi == "normbwd":
            @pl.when(pl.program_id(0) % tpb_nb == 0)
            def _():
                dsh_ref[...] = jnp.zeros_like(dsh_ref)
                dsc_ref[...] = jnp.zeros_like(dsc_ref)
                if nb_gate:
                    dgate_ref[...] = jnp.zeros_like(dgate_ref)

            @pl.when(pl.program_id(0) == 0)
            def _():
                dgn_ref[...] = jnp.zeros_like(dgn_ref)
                if nb_gate:
                    cs_ref[...] = jnp.zeros_like(cs_ref)

        if n_pay:
            @pl.when(pl.program_id(0) == 0)
            def _():
                _Plan(bcasts, pay_src, pay_dst, *sems).start()

            @pl.when(pl.program_id(0) == relay_step)
            def _():
                _Plan(bcasts, pay_src, pay_dst, *sems).pass_on()

        av = None if w_form == "kslots_t" else a_ref[...]
        if a_sq:
            av = av * av
        for ci in range(n_chunks):
            cols = slice(ci * nc, (ci + 1) * nc)
            if w_form == "full":
                acc = jnp.dot(av, w_ref[:, cols], preferred_element_type=F32)
            elif w_form == "full_t":
                acc = lax.dot_general(av, w_ref[cols, :], nt, preferred_element_type=F32)
            elif w_form == "nslots":
                acc = jnp.dot(av, w_ref[ci], preferred_element_type=F32)
            else:
                acc = lax.dot_general(a_ref[:, 0:ks], w_ref[0], nt, preferred_element_type=F32)
                for j in range(1, N_DEV):
                    acc = acc + lax.dot_general(a_ref[:, j * ks:(j + 1) * ks], w_ref[j], nt,
                                                preferred_element_type=F32)
            if has_bias:
                acc = acc + b_ref[:, cols]
            if epi == "resid":
                xn = x_ref[:, cols] + g_ref[:, cols] * acc
                o_ref[:, cols] = xn
                y_ref[:, cols] = acc.astype(BF16)
                if norm is not None:
                    r = lax.rsqrt(jnp.mean(xn * xn, axis=-1, keepdims=True) + EPS)
                    h_ref[...] = ((xn * r * gn_ref[...]) * (1.0 + sc_ref[...]) + sh_ref[...]).astype(BF16)
            elif epi == "relu":
                o_ref[:, cols] = jnp.maximum(acc, 0.0).astype(out_dtype)
            elif epi == "relu2d":
                o_ref[:, cols] = (acc * (2.0 * u_ref[:, cols].astype(F32))).astype(out_dtype)
            elif epi == "normbwd":
                xv = nx_ref[...]
                gn = ngn_ref[...]
                r = lax.rsqrt(jnp.mean(xv * xv, axis=-1, keepdims=True) + EPS)
                xhat = xv * r
                dsh_ref[...] += jnp.sum(acc, axis=0, keepdims=True)
                dsc_ref[...] += jnp.sum(acc * (xhat * gn), axis=0, keepdims=True)
                dn = acc * (1.0 + nsc_ref[...])
                dgn_ref[...] += jnp.sum(dn * xhat, axis=0, keepdims=True)
                dxhat = dn * gn
                dx = ndx_ref[...] + r * (dxhat - xhat * jnp.mean(dxhat * xhat, axis=-1, keepdims=True))
                o_ref[...] = dx
                if nb_gate:
                    dy = dx * ngt_ref[...]
                    dy_ref[...] = dy.astype(BF16)
                    dgate_ref[...] += jnp.sum(dx * ny_ref[...].astype(F32), axis=0, keepdims=True)
                    cs_ref[...] += jnp.sum(dy, axis=0, keepdims=True)
            elif epi == "lnbwd":
                cv = cv_ref[...]
                lg = lg_ref[...]
                xc = cv - jnp.mean(cv, axis=-1, keepdims=True)
                rstd = lax.rsqrt(jnp.mean(xc * xc, axis=-1, keepdims=True) + EPS)
                xhat = xc * rstd
                ln = xhat * lg + lb_ref[...]
                sg = _sigmoid(ln)
                dln = acc * (sg * (1.0 + ln * (1.0 - sg)))
                dlg_ref[...] += jnp.sum(dln * xhat, axis=0, keepdims=True)
                dlb_ref[...] += jnp.sum(dln, axis=0, keepdims=True)
                dxhat = dln * lg
                dcv = rstd * (dxhat - jnp.mean(dxhat, axis=-1, keepdims=True)
                              - xhat * jnp.mean(dxhat * xhat, axis=-1, keepdims=True))
                o_ref[...] = dcv
                dcs_ref[...] += jnp.sum(dcv, axis=0, keepdims=True)
            else:
                o_ref[:, cols] = acc.astype(out_dtype)

        if n_pay:
            @pl.when(pl.program_id(0) == steps - 1)
            def _():
                _Plan(bcasts, pay_src, pay_dst, *sems).finish()

    args = [a, w]
    w_block = w.shape
    specs = [pl.BlockSpec((tm, K), lambda i: (i, 0)), pl.BlockSpec(w_block, lambda i: (0,) * len(w_block))]
    if has_bias:
        args.append(bias.reshape(1, N).astype(F32))
        specs.append(pl.BlockSpec((1, N), lambda i: (0, 0)))
    row_spec = pl.BlockSpec((tm, N), lambda i: (i, 0))
    if epi == "resid":
        tpb = rows_per_batch // tm
        vec_spec = pl.BlockSpec((None, 1, N), lambda i: (i // tpb, 0, 0))
        args += [x, gate]
        specs += [row_spec, vec_spec]
        out_shape = [jax.ShapeDtypeStruct((M, N), F32), jax.ShapeDtypeStruct((M, N), BF16)]
        out_specs = [row_spec, row_spec]
        if norm is not None:
            assert n_chunks == 1
            args += list(norm)
            specs += [pl.BlockSpec((1, N), lambda i: (0, 0)), vec_spec, vec_spec]
            out_shape.append(jax.ShapeDtypeStruct((M, N), BF16))
            out_specs.append(row_spec)
    elif epi == "normbwd":
        assert n_chunks == 1
        vec_spec = pl.BlockSpec((None, 1, N), lambda i: (i // tpb_nb, 0, 0))
        one_spec = pl.BlockSpec((1, N), lambda i: (0, 0))
        nbat = M // rows_per_batch
        f_vec, f_one = jax.ShapeDtypeStruct((nbat, 1, N), F32), jax.ShapeDtypeStruct((1, N), F32)
        args += [nb["x"], nb["dx_in"], nb["gnorm"], nb["sc"]]
        specs += [row_spec, row_spec, one_spec, vec_spec]
        out_shape = [jax.ShapeDtypeStruct((M, N), F32), f_vec, f_vec, f_one]
        out_specs = [row_spec, vec_spec, vec_spec, one_spec]
        if nb_gate:
            args += list(nb["gate"])
            specs += [row_spec, vec_spec]
            out_shape += [jax.ShapeDtypeStruct((M, N), BF16), f_vec, f_one]
            out_specs += [row_spec, vec_spec, one_spec]
    elif epi == "lnbwd":
        assert n_chunks == 1
        one_spec = pl.BlockSpec((1, N), lambda i: (0, 0))
        f_one = jax.ShapeDtypeStruct((1, N), F32)
        args += list(ln)
        specs += [row_spec, one_spec, one_spec]
        out_shape = [jax.ShapeDtypeStruct((M, N), F32), f_one, f_one, f_one]
        out_specs = [row_spec, one_spec, one_spec, one_spec]
    else:
        if epi == "relu2d":
            args.append(u)
            specs.append(row_spec)
        out_shape = [jax.ShapeDtypeStruct((M, N), out_dtype)]
        out_specs = [row_spec]
    scratch = []
    if n_pay:
        p_arrays, p_in, p_shapes, p_out, scratch = _payload_layout(payload)
        args += p_arrays
        specs += p_in
        out_shape += p_shapes
        out_specs += p_out
    res = pl.pallas_call(
        body,
        name=name,
        grid=(steps,),
        out_shape=tuple(out_shape),
        in_specs=specs,
        out_specs=tuple(out_specs),
        scratch_shapes=scratch,
        compiler_params=_params("arbitrary" if (n_pay or epi in ("normbwd", "lnbwd")) else "parallel"),
    )(*args)
    if epi == "resid":
        n_own = 3 if norm is not None else 2
    elif epi == "normbwd":
        n_own = 7 if nb_gate else 4
    elif epi == "lnbwd":
        n_own = 4
    else:
        n_own = 1
    own = res[0] if n_own == 1 else tuple(res[:n_own])
    return (own, list(res[n_own:])) if n_pay else own


def _mm_tn(name, a, b, *, a_sq=False, out_form="full", halves=False, colsum=False, tt_pref=1024, tk_pref=1024,
           tn_pref=1024, payload=()):
    T, K = a.shape
    N = b.shape[1]
    tt = _pick(T, tt_pref, 16)
    tk = K // N_DEV if out_form == "kslots" else _pick(K, tk_pref, 128)
    tn = N // N_DEV if out_form == "nslots" else _pick(N, tn_pref, 128)
    nt_steps = T // tt
    grid = (K // tk, N // tn, nt_steps)
    slots = out_form != "full"
    assert not colsum or K == tk
    n_pay = len(payload)
    bcasts = [bc for _, bc in payload]

    def body(*refs):
        it = iter(refs)
        a_ref, b_ref = next(it), next(it)
        pay_src = [next(it) for _ in range(n_pay)]
        o_ref = next(it)
        o2_ref = next(it) if halves else None
        cs_ref = next(it) if colsum else None
        pay_dst = [next(it) for _ in range(n_pay)]
        acc = next(it) if slots else o_ref
        sems = list(it)
        step = (pl.program_id(0) * grid[1] + pl.program_id(1)) * grid[2] + pl.program_id(2)

        if n_pay:
            @pl.when(step == 0)
            def _():
                _Plan(bcasts, pay_src, pay_dst, *sems).start()

        @pl.when(pl.program_id(2) == 0)
        def _():
            acc[...] = jnp.zeros_like(acc)
            if colsum:
                cs_ref[...] = jnp.zeros_like(cs_ref)

        bv = b_ref[...]
        av = a_ref[...]
        if a_sq:
            av = av * av
        acc[...] += lax.dot_general(av, bv, (((0,), (0,)), ((), ())), preferred_element_type=F32)
        if colsum:
            cs_ref[...] += jnp.sum(bv.astype(F32), axis=0, keepdims=True)

        if slots:
            @pl.when(pl.program_id(2) == nt_steps - 1)
            def _():
                if halves:
                    o_ref[...] = acc[0:tk // 2, :].astype(o_ref.dtype)
                    o2_ref[...] = acc[tk // 2:, :].astype(o2_ref.dtype)
                else:
                    o_ref[...] = acc[...].astype(o_ref.dtype)

        if n_pay:
            @pl.when(step == grid[0] * grid[1] * grid[2] - 1)
            def _():
                plan = _Plan(bcasts, pay_src, pay_dst, *sems)
                plan.pass_on()
                plan.finish()

    if out_form == "full":
        out_shape = [jax.ShapeDtypeStruct((K, N), F32)]
        out_specs = [pl.BlockSpec((tk, tn), lambda k, n, t: (k, n))]
    else:
        assert not halves or out_form == "kslots" or K == tk
        rows = tk // 2 if halves else tk
        if out_form == "nslots":
            shape, spec = (N_DEV, K // tk * rows, tn), pl.BlockSpec((None, rows, tn), lambda k, n, t: (n, k, 0))
        else:
            shape, spec = (N_DEV, rows, N), pl.BlockSpec((None, rows, tn), lambda k, n, t: (k, 0, n))
        out_shape = [jax.ShapeDtypeStruct(shape, BF16)] * (2 if halves else 1)
        out_specs = [spec] * (2 if halves else 1)
    if colsum:
        out_shape.append(jax.ShapeDtypeStruct((1, N), F32))
        out_specs.append(pl.BlockSpec((1, tn), lambda k, n, t: (0, n)))
    args = [a, b]
    in_specs = [pl.BlockSpec((tt, tk), lambda k, n, t: (t, k)), pl.BlockSpec((tt, tn), lambda k, n, t: (t, n))]
    scratch = [pltpu.VMEM((tk, tn), F32)] if slots else []
    n_own = len(out_shape)
    if n_pay:
        p_arrays, p_in, p_shapes, p_out, p_scratch = _payload_layout(payload)
        args += p_arrays
        in_specs += p_in
        out_shape += p_shapes
        out_specs += p_out
        scratch += p_scratch
    res = pl.pallas_call(
        body,
        name=name,
        grid=grid,
        out_shape=tuple(out_shape),
        in_specs=in_specs,
        out_specs=tuple(out_specs),
        scratch_shapes=scratch,
        compiler_params=_params(*(("arbitrary",) * 3 if n_pay else ("parallel", "parallel", "arbitrary"))),
    )(*args)
    own = res[0] if n_own == 1 else tuple(res[:n_own])
    return (own, list(res[n_own:])) if n_pay else own


def _normmod_fwd(name, x, gnorm, sc, sh, S, tm_pref=512):
    T, D = x.shape
    tm = _pick(S, tm_pref, 16)
    tpb = S // tm

    def body(x_ref, g_ref, sc_ref, sh_ref, o_ref):
        xv = x_ref[...]
        r = lax.rsqrt(jnp.mean(xv * xv, axis=-1, keepdims=True) + EPS)
        n = xv * r * g_ref[...]
        o_ref[...] = (n * (1.0 + sc_ref[...]) + sh_ref[...]).astype(BF16)

    row = pl.BlockSpec((tm, D), lambda i: (i, 0))
    vec = pl.BlockSpec((None, 1, D), lambda i: (i // tpb, 0, 0))
    return pl.pallas_call(
        body,
        name=name,
        grid=(T // tm,),
        out_shape=jax.ShapeDtypeStruct((T, D), BF16),
        in_specs=[row, pl.BlockSpec((1, D), lambda i: (0, 0)), vec, vec],
        out_specs=row,
        compiler_params=_params("parallel"),
    )(x, gnorm, sc, sh)


ATT_ROWS = GROUP * ATT_BLOCK
ATT_SPAN = 2 * ATT_BLOCK
ATT_SCALE = HEAD_DIM ** -0.5


def _attn_table(sinks):
    slopes = jnp.asarray(np.array([2.0 ** (-8.0 * (h + 1) / N_HEADS) for h in range(N_HEADS)], np.float32))
    r = jnp.arange(ATT_BLOCK)[:, None]
    cc = jnp.arange(ATT_SPAN)[None, :]
    dist = r + ATT_BLOCK - cc
    ok = (dist >= 0) & (dist < ATT_BLOCK)
    tab = jnp.where(ok[None], -slopes[:, None, None] * dist.astype(F32)[None], NEG_BIG)
    tab = jnp.where((cc == 0)[None], sinks.astype(F32)[:, None, None], tab)
    return tab.reshape(N_KV_HEADS, ATT_ROWS, ATT_SPAN)


KV_COLS = N_KV_HEADS * HEAD_DIM


PAIR = 2 * HEAD_DIM


def _pair_tables(tab):
    hkv, rows, span = tab.shape
    pairs = rows // (2 * ATT_BLOCK)
    return tab.reshape(hkv, pairs, 2, ATT_BLOCK, span).transpose(0, 2, 1, 3, 4).reshape(hkv, 2, pairs * ATT_BLOCK, span)


def _stack_pairs(ref, pairs, col0):
    return jnp.concatenate([ref[:, col0 + j * PAIR:col0 + (j + 1) * PAIR] for j in range(pairs)], axis=0)


def _unstack_pairs(v, pairs):
    return jnp.concatenate([v[j * ATT_BLOCK:(j + 1) * ATT_BLOCK, :] for j in range(pairs)], axis=1)


def _swap_halves(x):
    return jnp.concatenate([x[:, HEAD_DIM:], x[:, :HEAD_DIM]], axis=1)


def _load_span(prev_ref, cur_ref, buf_e, buf_o, kv, mult):
    lane = lax.broadcasted_iota(jnp.int32, (ATT_BLOCK, PAIR), 1)
    row = lax.broadcasted_iota(jnp.int32, (ATT_BLOCK, PAIR), 0)
    own = lane // HEAD_DIM == kv
    for r0, ref, first in ((0, prev_ref, True), (ATT_BLOCK, cur_ref, False)):
        v = ref[...]
        if mult != 1.0:
            v = v * mult
        zero = jnp.zeros_like(v)
        mine = jnp.where(own, v, zero)
        if first:
            mine = jnp.where(row > 0, mine, zero)
        other = _swap_halves(mine)
        buf_e[r0:r0 + ATT_BLOCK, :] = mine if kv == 0 else other
        buf_o[r0:r0 + ATT_BLOCK, :] = other if kv == 0 else mine


def _first_block_penalty(i):
    col = lax.broadcasted_iota(jnp.int32, (1, ATT_SPAN), 1)
    return jnp.where((col < ATT_BLOCK) & (col > 0), jnp.where(i > 0, 0.0, NEG_BIG), 0.0).astype(F32)


def _attn_probs(qp, kbuf, bias, first_pen):
    nt = (((1,), (1,)), ((), ()))
    s = lax.dot_general(qp, kbuf, nt, preferred_element_type=F32) + bias + first_pen
    m = jnp.max(s, axis=-1, keepdims=True)
    e = jnp.exp(s - m)
    return e * (1.0 / jnp.sum(e, axis=-1, keepdims=True))


def _attn_specs(nb, q_dim, pairs, row_q, row_cur, row_prev):
    kblk = q_dim // KV_COLS
    qspec = pl.BlockSpec((ATT_BLOCK, q_dim), lambda b, i: (row_q(b, i), 0))
    kv_specs = [pl.BlockSpec((ATT_BLOCK, KV_COLS), (lambda b, i, r=r, c=c: (r(b, i), c)))
                for c in (kblk, kblk + 1) for r in (row_cur, row_prev)]
    tspec = pl.BlockSpec((N_KV_HEADS, 2, pairs * ATT_BLOCK, ATT_SPAN), lambda b, i: (0, 0, 0, 0))
    return qspec, kv_specs, tspec


def _attn_fwd(qkv, tab2, B, S, payload=()):
    assert N_KV_HEADS == 2 and GROUP % 2 == 0
    T = qkv.shape[0]
    pairs = GROUP // 2
    gw = pairs * PAIR
    nb = S // ATT_BLOCK
    q_dim = N_HEADS * HEAD_DIM
    n_pay = len(payload)
    bcasts = [bc for _, bc in payload]
    steps = B * nb

    def body(*refs):
        q_ref, kc_ref, kp_ref, vc_ref, vp_ref, tab_ref = refs[:6]
        pay_src = refs[6:6 + n_pay]
        o_ref = refs[6 + n_pay]
        pay_dst = refs[7 + n_pay:7 + 2 * n_pay]
        ke, ko, ve, vo = refs[7 + 2 * n_pay:11 + 2 * n_pay]
        sems = refs[11 + 2 * n_pay:]
        i = pl.program_id(1)
        if n_pay:
            step = pl.program_id(0) * nb + i

            @pl.when(step == 0)
            def _():
                _Plan(bcasts, pay_src, pay_dst, *sems).start()

            @pl.when(step == (3 * steps) // 4)
            def _():
                _Plan(bcasts, pay_src, pay_dst, *sems).pass_on()

            @pl.when(step == steps - 1)
            def _():
                _Plan(bcasts, pay_src, pay_dst, *sems).finish()

        pen = _first_block_penalty(i)
        for kv in range(N_KV_HEADS):
            _load_span(kp_ref, kc_ref, ke.at[kv], ko.at[kv], kv, ATT_SCALE)
            _load_span(vp_ref, vc_ref, ve.at[kv], vo.at[kv], kv, 1.0)
            qp = _stack_pairs(q_ref, pairs, kv * gw)
            p_e = _attn_probs(qp, ke[kv], tab_ref[kv, 0], pen)
            p_o = _attn_probs(qp, ko[kv], tab_ref[kv, 1], pen)
            o = jnp.dot(p_e.astype(BF16), ve[kv], preferred_element_type=F32)
            o = o + jnp.dot(p_o.astype(BF16), vo[kv], preferred_element_type=F32)
            o_ref[:, kv * gw:(kv + 1) * gw] = _unstack_pairs(o.astype(BF16), pairs)

    qspec, kv_specs, tspec = _attn_specs(nb, q_dim, pairs, lambda b, i: b * nb + i, lambda b, i: b * nb + i,
                                         lambda b, i: b * nb + jnp.maximum(i - 1, 0))
    args, in_specs = [qkv] * 5 + [tab2], [qspec] + kv_specs + [tspec]
    out_shape, out_specs = [jax.ShapeDtypeStruct((T, q_dim), BF16)], [qspec]
    scratch = [pltpu.VMEM((N_KV_HEADS, ATT_SPAN, PAIR), BF16)] * 4
    if n_pay:
        p_arrays, p_in, p_shapes, p_out, p_scratch = _payload_layout(payload)
        args += p_arrays
        in_specs += p_in
        out_shape += p_shapes
        out_specs += p_out
        scratch += p_scratch
    res = pl.pallas_call(
        body,
        name="attn_fwd",
        grid=(B, nb),
        out_shape=tuple(out_shape),
        in_specs=in_specs,
        out_specs=tuple(out_specs),
        scratch_shapes=scratch,
        compiler_params=_params(*(("arbitrary",) * 2 if n_pay else ("parallel", "arbitrary"))),
    )(*args)
    return (res[0], list(res[1:])) if n_pay else res[0]


def _attn_bwd(qkv, o, do, tab2, B, S):
    T = qkv.shape[0]
    hd = HEAD_DIM
    pairs = GROUP // 2
    gw = pairs * PAIR
    rows = pairs * ATT_BLOCK
    nb = S // ATT_BLOCK
    last = nb - 1
    q_dim = N_HEADS * hd
    tn = (((0,), (0,)), ((), ()))
    nt = (((1,), (1,)), ((), ()))

    def body(q_ref, kc_ref, kp_ref, vc_ref, vp_ref, tab_ref, o_ref, do_ref,
             dq_ref, dkv_ref, dsink_ref, carry_k, carry_v, ke, ko, ve, vo):
        i = pl.program_id(1)

        @pl.when(i == 0)
        def _():
            dsink_ref[...] = jnp.zeros_like(dsink_ref)

        def emit(kv, dk_rows, dv_rows):
            dkv_ref[:, kv * hd:(kv + 1) * hd] = dk_rows.astype(BF16)
            dkv_ref[:, KV_COLS + kv * hd:KV_COLS + (kv + 1) * hd] = dv_rows.astype(BF16)

        @pl.when(i < nb)
        def _():
            pen = _first_block_penalty(i)
            low = lax.broadcasted_iota(jnp.int32, (rows, PAIR), 1) < hd
            low_k = lax.broadcasted_iota(jnp.int32, (ATT_SPAN, PAIR), 1) < hd
            keep = lax.broadcasted_iota(jnp.int32, (ATT_SPAN, hd), 0) > 0
            hg = lax.broadcasted_iota(jnp.int32, (pairs, rows), 0)
            hr = lax.broadcasted_iota(jnp.int32, (pairs, rows), 1)
            head_of = jnp.where(hr // ATT_BLOCK == hg, 1.0, 0.0).astype(BF16)
            dks, dvs = [], []
            for kv in range(N_KV_HEADS):
                _load_span(kp_ref, kc_ref, ke.at[kv], ko.at[kv], kv, ATT_SCALE)
                _load_span(vp_ref, vc_ref, ve.at[kv], vo.at[kv], kv, 1.0)
                qp = _stack_pairs(q_ref, pairs, kv * gw)
                dop = _stack_pairs(do_ref, pairs, kv * gw)
                op = _stack_pairs(o_ref, pairs, kv * gw)
                p_e = _attn_probs(qp, ke[kv], tab_ref[kv, 0], pen)
                p_o = _attn_probs(qp, ko[kv], tab_ref[kv, 1], pen)
                prod = dop.astype(F32) * op.astype(F32)
                d_e = jnp.sum(jnp.where(low, prod, 0.0), axis=-1, keepdims=True)
                d_o = jnp.sum(prod, axis=-1, keepdims=True) - d_e
                ds_e = (p_e * (lax.dot_general(dop, ve[kv], nt, preferred_element_type=F32) - d_e)).astype(BF16)
                ds_o = (p_o * (lax.dot_general(dop, vo[kv], nt, preferred_element_type=F32) - d_o)).astype(BF16)
                dq = jnp.dot(ds_e, ke[kv], preferred_element_type=F32) + jnp.dot(ds_o, ko[kv], preferred_element_type=F32)
                dq_ref[:, kv * gw:(kv + 1) * gw] = _unstack_pairs(dq.astype(BF16), pairs)
                dsink_ref[kv, 0] += jnp.dot(head_of, ds_e, preferred_element_type=F32)[:, 0:128]
                dsink_ref[kv, 1] += jnp.dot(head_of, ds_o, preferred_element_type=F32)[:, 0:128]
                dk2 = jnp.where(low_k, lax.dot_general(ds_e, qp, tn, preferred_element_type=F32),
                                lax.dot_general(ds_o, qp, tn, preferred_element_type=F32))
                dv2 = jnp.where(low_k, lax.dot_general(p_e.astype(BF16), dop, tn, preferred_element_type=F32),
                                lax.dot_general(p_o.astype(BF16), dop, tn, preferred_element_type=F32))
                dks.append(jnp.where(keep, (dk2[:, :hd] + dk2[:, hd:]) * ATT_SCALE, 0.0))
                dvs.append(jnp.where(keep, dv2[:, :hd] + dv2[:, hd:], 0.0))

            @pl.when(i > 0)
            def _():
                for kv in range(N_KV_HEADS):
                    emit(kv, carry_k[kv] + dks[kv][0:ATT_BLOCK], carry_v[kv] + dvs[kv][0:ATT_BLOCK])

            for kv in range(N_KV_HEADS):
                carry_k[kv] = dks[kv][ATT_BLOCK:]
                carry_v[kv] = dvs[kv][ATT_BLOCK:]

        @pl.when(i == nb)
        def _():
            for kv in range(N_KV_HEADS):
                emit(kv, carry_k[kv], carry_v[kv])

    qspec, kv_specs, tspec = _attn_specs(nb, q_dim, pairs, lambda b, i: b * nb + jnp.minimum(i, last),
                                         lambda b, i: b * nb + jnp.minimum(i, last),
                                         lambda b, i: b * nb + jnp.clip(i - 1, 0, last))
    dkv = pl.BlockSpec((ATT_BLOCK, 2 * KV_COLS), lambda b, i: (b * nb + jnp.maximum(i - 1, 0), 0))
    dsk = pl.BlockSpec((None, N_KV_HEADS, 2, pairs, 128), lambda b, i: (b, 0, 0, 0, 0))
    return pl.pallas_call(
        body,
        name="attn_bwd",
        grid=(B, nb + 1),
        out_shape=(
            jax.ShapeDtypeStruct((T, q_dim), BF16),
            jax.ShapeDtypeStruct((T, 2 * KV_COLS), BF16),
            jax.ShapeDtypeStruct((B, N_KV_HEADS, 2, pairs, 128), F32),
        ),
        in_specs=[qspec] + kv_specs + [tspec, qspec, qspec],
        out_specs=(qspec, dkv, dsk),
        scratch_shapes=[pltpu.VMEM((N_KV_HEADS, ATT_BLOCK, hd), F32), pltpu.VMEM((N_KV_HEADS, ATT_BLOCK, hd), F32)]
        + [pltpu.VMEM((N_KV_HEADS, ATT_SPAN, PAIR), BF16)] * 4,
        compiler_params=_params("arbitrary", "arbitrary"),
    )(qkv, qkv, qkv, qkv, qkv, tab2, o, do)


def _conv_tiles(S):
    ts = _pick(S, 256, CONV_HALO)
    return ts, S // ts


def _conv_chunks(C, ts):
    lane = _pick(C, 128, 128)
    return lane, C // lane, _pick(ts, 128, 8)


def _conv_weight_chunks(w_dw, C):
    lane = _pick(C, 128, 128)
    w = jnp.pad(w_dw, ((0, CONV_HALO - CONV_WIDTH), (0, 0)))
    return w.reshape(CONV_HALO, C // lane, lane).transpose(1, 0, 2)


def _conv_fwd(u, w3, b_dw, ln_g, ln_b, S):
    T, C2 = u.shape
    C = C2 // 2
    B = T // S
    ts, nj = _conv_tiles(S)
    hb = ts // CONV_HALO
    lane, nc, rc = _conv_chunks(C, ts)

    def body(a_ref, g_ref, ap_ref, gp_ref, w_ref, bdw_ref, lg_ref, lb_ref, cv_ref, s_ref, buf, cvb):
        j = pl.program_id(1)
        glu_prev = ap_ref[...] * _sigmoid(gp_ref[...]) * (j > 0).astype(F32)
        glu = a_ref[...] * _sigmoid(g_ref[...])
        for cc in range(nc):
            buf[cc, 0:CONV_HALO, :] = glu_prev[:, cc * lane:(cc + 1) * lane]
            buf[cc, CONV_HALO:, :] = glu[:, cc * lane:(cc + 1) * lane]

        def chunk(cc, carry):
            for r0 in range(0, ts, rc):
                acc = jnp.zeros((rc, lane), F32)
                for kk in range(CONV_WIDTH):
                    lo = CONV_HALO - (CONV_WIDTH - 1 - kk) + r0
                    acc = acc + w_ref[cc, kk:kk + 1, :] * buf[cc, lo:lo + rc, :]
                cvb[cc, r0:r0 + rc, :] = acc
            return carry

        lax.fori_loop(0, nc, chunk, 0)
        for cc in range(nc):
            cv_ref[:, cc * lane:(cc + 1) * lane] = cvb[cc] + bdw_ref[:, cc * lane:(cc + 1) * lane]
        cv = cv_ref[...]
        mu = jnp.mean(cv, axis=-1, keepdims=True)
        xc = cv - mu
        rstd = lax.rsqrt(jnp.mean(xc * xc, axis=-1, keepdims=True) + EPS)
        ln = xc * rstd * lg_ref[...] + lb_ref[...]
        s_ref[...] = (ln * _sigmoid(ln)).astype(BF16)

    a_cur = pl.BlockSpec((ts, C), lambda b, j: (b * nj + j, 0))
    g_cur = pl.BlockSpec((ts, C), lambda b, j: (b * nj + j, 1))
    a_prev = pl.BlockSpec((CONV_HALO, C), lambda b, j: (jnp.maximum((b * nj + j) * hb - 1, 0), 0))
    g_prev = pl.BlockSpec((CONV_HALO, C), lambda b, j: (jnp.maximum((b * nj + j) * hb - 1, 0), 1))
    wspec = pl.BlockSpec((nc, CONV_HALO, lane), lambda b, j: (0, 0, 0))
    one = pl.BlockSpec((1, C), lambda b, j: (0, 0))
    return pl.pallas_call(
        body,
        name="conv_fwd",
        grid=(B, nj),
        out_shape=(jax.ShapeDtypeStruct((T, C), F32), jax.ShapeDtypeStruct((T, C), BF16)),
        in_specs=[a_cur, g_cur, a_prev, g_prev, wspec, one, one, one],
        out_specs=(a_cur, a_cur),
        scratch_shapes=[pltpu.VMEM((nc, CONV_HALO + ts, lane), F32), pltpu.VMEM((nc, ts, lane), F32)],
        compiler_params=_params("parallel", "arbitrary"),
    )(u, u, u, u, w3, b_dw, ln_g, ln_b)


def _conv_bwd(dcv, u, w3, S):
    T, C2 = u.shape
    C = C2 // 2
    B = T // S
    ts, nj = _conv_tiles(S)
    hb = ts // CONV_HALO
    n_halo_blocks = T // CONV_HALO
    lane, nc, rc = _conv_chunks(C, ts)

    def body(dcv_ref, dnx_ref, a_ref, g_ref, ap_ref, gp_ref, w_ref, du_ref, dw_ref, gbuf, dbuf, dglu, dw8):
        b, j = pl.program_id(0), pl.program_id(1)

        @pl.when((b == 0) & (j == 0))
        def _():
            dw8[...] = jnp.zeros_like(dw8)

        a = a_ref[...]
        sg = _sigmoid(g_ref[...])
        glu_prev = ap_ref[...] * _sigmoid(gp_ref[...]) * (j > 0).astype(F32)
        glu = a * sg
        dcur = dcv_ref[...]
        dnext = dnx_ref[...] * (j < nj - 1).astype(F32)
        for cc in range(nc):
            cols = slice(cc * lane, (cc + 1) * lane)
            gbuf[cc, 0:CONV_HALO, :] = glu_prev[:, cols]
            gbuf[cc, CONV_HALO:, :] = glu[:, cols]
            dbuf[cc, 0:ts, :] = dcur[:, cols]
            dbuf[cc, ts:, :] = dnext[:, cols]

        def chunk(cc, carry):
            for r0 in range(0, ts, rc):
                acc = jnp.zeros((rc, lane), F32)
                for kk in range(CONV_WIDTH):
                    d = CONV_WIDTH - 1 - kk
                    acc = acc + w_ref[cc, kk:kk + 1, :] * dbuf[cc, r0 + d:r0 + d + rc, :]
                dglu[cc, r0:r0 + rc, :] = acc
            for kk in range(CONV_WIDTH):
                d = CONV_WIDTH - 1 - kk
                p = jnp.zeros((rc, lane), F32)
                for r0 in range(0, ts, rc):
                    lo = CONV_HALO - d + r0
                    p = p + dbuf[cc, r0:r0 + rc, :] * gbuf[cc, lo:lo + rc, :]
                dw8[cc, kk * 8:(kk + 1) * 8, :] += jnp.sum(p.reshape(rc // 8, 8, lane), axis=0)
            return carry

        lax.fori_loop(0, nc, chunk, 0)
        for cc in range(nc):
            cols = slice(cc * lane, (cc + 1) * lane)
            dgl = dglu[cc]
            du_ref[:, cc * lane:(cc + 1) * lane] = (dgl * sg[:, cols]).astype(BF16)
            du_ref[:, C + cc * lane:C + (cc + 1) * lane] = (dgl * a[:, cols] * sg[:, cols] * (1.0 - sg[:, cols])).astype(BF16)

        @pl.when((b == B - 1) & (j == nj - 1))
        def _():
            dw_ref[...] = jnp.zeros_like(dw_ref)
            for kk in range(CONV_WIDTH):
                dw_ref[:, kk:kk + 1, :] = jnp.sum(dw8[:, kk * 8:(kk + 1) * 8, :], axis=1, keepdims=True)

    a_cur = pl.BlockSpec((ts, C), lambda b, j: (b * nj + j, 0))
    g_cur = pl.BlockSpec((ts, C), lambda b, j: (b * nj + j, 1))
    a_prev = pl.BlockSpec((CONV_HALO, C), lambda b, j: (jnp.maximum((b * nj + j) * hb - 1, 0), 0))
    g_prev = pl.BlockSpec((CONV_HALO, C), lambda b, j: (jnp.maximum((b * nj + j) * hb - 1, 0), 1))
    d_next = pl.BlockSpec((CONV_HALO, C), lambda b, j: (jnp.minimum((b * nj + j + 1) * hb, n_halo_blocks - 1), 0))
    wspec = pl.BlockSpec((nc, CONV_HALO, lane), lambda b, j: (0, 0, 0))
    return pl.pallas_call(
        body,
        name="conv_bwd",
        grid=(B, nj),
        out_shape=(jax.ShapeDtypeStruct((T, C2), BF16), jax.ShapeDtypeStruct((nc, CONV_HALO, lane), F32)),
        in_specs=[a_cur, d_next, a_cur, g_cur, a_prev, g_prev, wspec],
        out_specs=(pl.BlockSpec((ts, C2), lambda b, j: (b * nj + j, 0)), wspec),
        scratch_shapes=[
            pltpu.VMEM((nc, CONV_HALO + ts, lane), F32),
            pltpu.VMEM((nc, ts + CONV_HALO, lane), F32),
            pltpu.VMEM((nc, ts, lane), F32),
            pltpu.VMEM((nc, CONV_HALO * 8, lane), F32),
        ],
        compiler_params=_params("arbitrary", "arbitrary"),
    )(dcv, dcv, u, u, u, u, w3)


def _loss_head(x, tgt, gfin, y, gate, S, tm_pref=256):
    T, D = x.shape
    tm = _pick(S, tm_pref, 16)
    tpb = S // tm

    def body(x_ref, t_ref, g_ref, y_ref, gt_ref, dx_ref, loss_ref, dg_ref, dy_ref, dgate_ref):
        @pl.when(pl.program_id(0) == 0)
        def _():
            loss_ref[...] = jnp.zeros_like(loss_ref)
            dg_ref[...] = jnp.zeros_like(dg_ref)

        @pl.when(pl.program_id(0) % tpb == 0)
        def _():
            dgate_ref[...] = jnp.zeros_like(dgate_ref)

        xv = x_ref[...]
        g = g_ref[...]
        r = lax.rsqrt(jnp.mean(xv * xv, axis=-1, keepdims=True) + EPS)
        xhat = xv * r
        e = xhat * g - t_ref[...]
        row_loss = jnp.mean(e * e, axis=-1, keepdims=True)
        loss_ref[...] += 0.5 * jnp.sum(row_loss, axis=0, keepdims=True)
        dy = e * (1.0 / D)
        dg_ref[...] += jnp.sum(dy * xhat, axis=0, keepdims=True)
        dxhat = dy * g
        dx = r * (dxhat - xhat * jnp.mean(dxhat * xhat, axis=-1, keepdims=True))
        dx_ref[...] = dx
        dy_ref[...] = (dx * gt_ref[...]).astype(BF16)
        dgate_ref[...] += jnp.sum(dx * y_ref[...].astype(F32), axis=0, keepdims=True)

    row = pl.BlockSpec((tm, D), lambda i: (i, 0))
    vec = pl.BlockSpec((None, 1, D), lambda i: (i // tpb, 0, 0))
    one = pl.BlockSpec((1, D), lambda i: (0, 0))
    return pl.pallas_call(
        body,
        name="loss_head",
        grid=(T // tm,),
        out_shape=(
            jax.ShapeDtypeStruct((T, D), F32),
            jax.ShapeDtypeStruct((8, 128), F32),
            jax.ShapeDtypeStruct((1, D), F32),
            jax.ShapeDtypeStruct((T, D), BF16),
            jax.ShapeDtypeStruct((T // S, 1, D), F32),
        ),
        in_specs=[row, row, one, row, vec],
        out_specs=(row, pl.BlockSpec((8, 128), lambda i: (0, 0)), one, row, vec),
        compiler_params=_params("arbitrary"),
    )(x, tgt, gfin, y, gate)


def _adam(name, parts, w, m, v, tm_pref=256):
    L = len(parts)
    P, R, C = parts[0].shape
    fit = VMEM_LIMIT_BYTES * 5 // 8 // (2 * L * P * C * parts[0].dtype.itemsize)
    tm = _pick(R, max(16, min(tm_pref, fit // 16 * 16)), 16)
    tiles = R // tm
    c1 = 1.0 - ADAM_B1 ** ADAM_STEP
    c2 = 1.0 - ADAM_B2 ** ADAM_STEP

    def body(*refs):
        p_refs = refs[:L]
        w_ref, m_ref, v_ref, g_ref, d_ref, mo_ref, vo_ref = refs[L:]
        for l in range(L):
            @pl.when(pl.program_id(0) == l)
            def _():
                g = p_refs[l][0].astype(F32)
                for i in range(1, P):
                    g = g + p_refs[l][i].astype(F32)
                m_new = ADAM_B1 * m_ref[...] + (1.0 - ADAM_B1) * g
                v_new = ADAM_B2 * v_ref[...] + (1.0 - ADAM_B2) * (g * g)
                m_hat = m_new / c1
                v_hat = v_new / c2
                g_ref[...] = g
                d_ref[...] = -ADAM_LR * (m_hat / (jnp.sqrt(v_hat) + ADAM_EPS) + ADAM_WD * w_ref[...])
                mo_ref[...] = m_new
                vo_ref[...] = v_new

    def p_spec(l):
        return pl.BlockSpec((P, tm, C), lambda li, t: (0, jnp.where(li == l, t, jnp.where(li < l, 0, tiles - 1)), 0))

    row = pl.BlockSpec((tm, C), lambda li, t: (li * tiles + t, 0))
    return pl.pallas_call(
        body,
        name=name,
        grid=(L, tiles),
        out_shape=(jax.ShapeDtypeStruct((L * R, C), F32),) * 4,
        in_specs=[p_spec(l) for l in range(L)] + [row, row, row],
        out_specs=(row, row, row, row),
        compiler_params=_params("arbitrary", "arbitrary"),
    )(*parts, w, m, v)


BIG = ("w_qkv", "w_o", "w_pw1", "w_pw2", "w_up", "w_down")
SMALL_SHARDED = (("b_pw1", 1), ("w_dw", 2), ("b_dw", 1), ("conv_ln_g", 1), ("conv_ln_b", 1), ("b_pw2", 1))
SMALL_REPL = ("b_mod", "norm_mix", "norm_mlp", "b_qkv", "b_o", "sinks", "final_norm")
WEIGHTS = ("w_mod", "b_mod", "norm_mix", "norm_mlp", "w_qkv", "b_qkv", "w_o", "b_o", "sinks", "w_pw1", "b_pw1",
           "w_dw", "b_dw", "conv_ln_g", "conv_ln_b", "w_pw2", "b_pw2", "w_up", "w_down", "final_norm")


def _step(x, c, loss_target, W, M, V):
    B, S, D = x.shape
    T = B * S
    L = W["w_mod"].shape[0]
    n_mod = W["w_mod"].shape[2]
    me = 4 * lax.axis_index("x") + 2 * lax.axis_index("y") + lax.axis_index("c")

    Wb = {n: W[n].astype(BF16) for n in BIG}

    def shards(i):
        jm = i // 2
        first, last = ("w_qkv", "w_o") if i % 2 == 0 else ("w_pw1", "w_pw2")
        return [Wb[first][jm], Wb[last][jm], Wb["w_up"][i], Wb["w_down"][i]]

    def carried(res, payload):
        return res if payload else (res, [])

    small_names = [n for n, _ in SMALL_SHARDED]
    small_src, small_sizes = _pack([c] + [W[n] for n in small_names], F32, 0)
    got = _exchange("gather_first", [(small_src, True), (shards(0)[0], True)])
    small_parts = _unpack(got[0], small_sizes, [c.shape] + [W[n].shape for n in small_names], 1)
    c_all = small_parts[0].reshape(N_DEV * B, D)
    full = {n: _from_slots(p, d) for (n, d), p in zip(SMALL_SHARDED, small_parts[1:])}
    gathered = [[got[1], None, None, None]] + [None] * (L - 1)

    b_mod_mine = lax.dynamic_slice_in_dim(W["b_mod"], me * n_mod, n_mod, axis=1).reshape(L, 1, n_mod)
    mod_part = _mod_fwd(c_all, W["w_mod"].astype(BF16), b_mod_mine)
    mod_slots = mod_part.reshape(L, N_DEV, B, n_mod).transpose(1, 0, 2, 3).reshape(N_DEV, L * B, n_mod)
    mod_recv = _exchange("scatter_mod", [(mod_slots, False)])[0]
    mod = mod_recv.reshape(N_DEV, L, B, n_mod).transpose(1, 2, 0, 3).reshape(L, B, N_MOD, 1, D)

    w_dw3 = [_conv_weight_chunks(full["w_dw"][j], D) for j in range(full["w_dw"].shape[0])]

    xc = x.reshape(T, D)
    saved = []
    h1 = _normmod_fwd("normmod_mix_fwd", xc, W["norm_mix"][0][None], mod[0, :, 1], mod[0, :, 0], S)
    for i in range(L):
        jm = i // 2
        sh1, sc1, g1, sh2, sc2, g2 = (mod[i, :, t] for t in range(N_MOD))
        nxt = [[(s, True)] for s in shards(i + 1)] if i + 1 < L else [[]] * 4
        g_first, g_last, g_up, g_down = gathered[i]
        sv = {"x_in": xc, "h1": h1}
        if i % 2 == 0:
            w_qkv_full = _from_slots(g_first, 1)
            sv["w_first"] = w_qkv_full
            mine = [(s, True) for s in shards(0)[1:]] if i == 0 else []
            pay = nxt[0] + mine[:1]
            qkv, n_first = carried(_mm("qkv_fwd", h1, w_qkv_full, bias=W["b_qkv"][jm], tm_pref=1024, payload=pay), pay)
            tab = _pair_tables(_attn_table(W["sinks"][jm]))
            mix_in, n_attn = carried(_attn_fwd(qkv, tab, B, S, payload=mine[1:]), mine[1:])
            if i == 0:
                g_last, g_up, g_down = n_first[-1], n_attn[0], n_attn[1]
            sv.update(qkv=qkv, tab=tab)
            b_out = W["b_o"][jm]
        else:
            sv["w_first"] = g_first
            u, n_first = carried(_mm("pw1_fwd", h1, g_first, w_form="nslots", bias=full["b_pw1"][jm], out_dtype=F32,
                                     tm_pref=1024, payload=nxt[0]), nxt[0])
            cv, mix_in = _conv_fwd(u, w_dw3[jm], full["b_dw"][jm][None], full["conv_ln_g"][jm][None],
                                   full["conv_ln_b"][jm][None], S)
            sv.update(u=u, cv=cv)
            b_out = full["b_pw2"][jm]
        w_last = g_last.reshape(-1, D)
        w_down_full = g_down.reshape(-1, D)
        sv.update(mix_in=mix_in, w_last=w_last, g_up=g_up, w_down=w_down_full)
        (x1, y1, h2), n_last = carried(_mm("mix_out_fwd", mix_in, w_last, bias=b_out, epi="resid", x=xc, gate=g1,
                                           norm=(W["norm_mlp"][i][None], sc2, sh2), rows_per_batch=S,
                                           tm_pref=1024, payload=nxt[1]), nxt[1])
        sv.update(y1=y1, x1=x1)
        up, n_up = carried(_mm("mlp_up_fwd", h2, g_up, w_form="nslots", epi="relu", tm_pref=1024, payload=nxt[2]),
                           nxt[2])
        norm_next = (W["norm_mix"][i + 1][None], mod[i + 1, :, 1], mod[i + 1, :, 0]) if i + 1 < L else None
        res, n_down = carried(_mm("mlp_down_fwd", up, w_down_full, a_sq=True, epi="resid", x=x1, gate=g2,
                                  norm=norm_next, rows_per_batch=S, payload=nxt[3]), nxt[3])
        x2, y2 = res[0], res[1]
        h1 = res[2] if i + 1 < L else None
        sv.update(h2=h2, up=up, y2=y2)
        saved.append(sv)
        xc = x2
        if i + 1 < L:
            gathered[i + 1] = [n_first[0], n_last[0], n_up[0], n_down[0]]

    dx, loss_blk, dgfin, dy2, dg2 = _loss_head(xc, loss_target.reshape(T, D), W["final_norm"][None],
                                               saved[L - 1]["y2"], mod[L - 1, :, 5], S)
    loss = lax.psum(loss_blk[0, 0], ("x", "y", "c"))

    G = {"final_norm": dgfin.reshape(D)}
    dmod_layers = [None] * L
    small_grads = ("norm_mix", "norm_mlp", "b_qkv", "b_o", "sinks", "b_pw1", "w_dw", "b_dw", "conv_ln_g", "conv_ln_b",
                   "b_pw2")
    acc = {n: [None] * W[n].shape[0] for n in small_grads}
    reduced = {n: [None] * W[n].shape[0] for n in BIG}

    waiting_up = None
    for i in reversed(range(L)):
        jm = i // 2
        sv = saved[i]
        sh1, sc1, g1, sh2, sc2, g2 = (mod[i, :, t] for t in range(N_MOD))
        pay = [(waiting_up[1], False)] if waiting_up else []
        (gd0, gd1), r = carried(_mm_tn("w_down_grad", sv["up"], dy2, a_sq=True, out_form="kslots", halves=True,
                                       tt_pref=4096, payload=pay), pay)
        if waiting_up:
            reduced["w_up"][waiting_up[0]].append(r[0])
        du, r0 = _mm("mlp_down_bwd", dy2, sv["w_down"], w_form="full_t", epi="relu2d", u=sv["up"],
                     payload=[(gd0, False)])
        (gu0, gu1), r1 = _mm_tn("w_up_grad", sv["h2"], du, out_form="nslots", halves=True, tt_pref=4096,
                                payload=[(gd1, False)])
        reduced["w_down"][i] = [r0[0], r1[0]]
        pay = [(gu0, False)] + ([] if i > 0 else [(gu1, False)])
        (dx, dsh2, dsc2, dgn, dy1, dg1, dy1_sum), r = _mm(
            "mlp_up_bwd", du, sv["g_up"], w_form="kslots_t", epi="normbwd", rows_per_batch=S,
            nb=dict(x=sv["x1"], dx_in=dx, gnorm=W["norm_mlp"][i][None], sc=sc2, gate=(sv["y1"], g1)),
            payload=pay)
        reduced["w_up"][i] = list(r)
        waiting_up = (i, gu1) if i > 0 else None
        acc["norm_mlp"][i] = dgn.reshape(D)
        gw_last = _mm_tn("mix_out_grad", sv["mix_in"], dy1)
        gw_last = gw_last.reshape((N_DEV, -1) + gw_last.shape[1:]).astype(BF16)
        below = (saved[i - 1]["y2"], mod[i - 1, :, 5]) if i > 0 else None
        nb_mix = dict(x=sv["x_in"], dx_in=dx, gnorm=W["norm_mix"][i][None], sc=sc1, gate=below)
        if i % 2 == 0:
            acc["b_o"][jm] = dy1_sum.reshape(D)
            do, r = _mm("attn_out_bwd", dy1, sv["w_last"], w_form="full_t", tm_pref=1024, payload=[(gw_last, False)])
            reduced["w_o"][jm] = r[0]
            dq, dkv, dsk = _attn_bwd(sv["qkv"], sv["mix_in"], do, sv["tab"], B, S)
            acc["sinks"][jm] = jnp.sum(dsk[..., 0], axis=0).transpose(0, 2, 1).reshape(N_HEADS)
            dqkv = jnp.concatenate([dq, dkv], axis=1)
            gw_qkv, db_qkv = _mm_tn("w_qkv_grad", sv["h1"], dqkv, colsum=True, tt_pref=4096)
            acc["b_qkv"][jm] = db_qkv.reshape(-1)
            gw_qkv = _to_slots(gw_qkv, 1).astype(BF16)
            res, r = _mm("qkv_bwd", dqkv, sv["w_first"], w_form="full_t", epi="normbwd", nb=nb_mix, rows_per_batch=S,
                         payload=[(gw_qkv, False)])
            reduced["w_qkv"][jm] = r[0]
        else:
            acc["b_pw2"][jm] = dy1_sum.reshape(D)
            (dcv, dlg, dlb, dbdw), r = _mm(
                "pw2_bwd", dy1, sv["w_last"], w_form="full_t", epi="lnbwd", tm_pref=1024,
                ln=(sv["cv"], full["conv_ln_g"][jm][None], full["conv_ln_b"][jm][None]), payload=[(gw_last, False)])
            reduced["w_pw2"][jm] = r[0]
            acc["conv_ln_g"][jm], acc["conv_ln_b"][jm], acc["b_dw"][jm] = dlg.reshape(-1), dlb.reshape(-1), dbdw.reshape(-1)
            du1, dwdw = _conv_bwd(dcv, sv["u"], w_dw3[jm], S)
            acc["w_dw"][jm] = dwdw.transpose(1, 0, 2).reshape(CONV_HALO, D)[:CONV_WIDTH]
            gw_pw1, db_pw1 = _mm_tn("w_pw1_grad", sv["h1"], du1, out_form="nslots", colsum=True, tt_pref=4096)
            acc["b_pw1"][jm] = db_pw1.reshape(-1)
            res, r = _mm("pw1_bwd", du1, sv["w_first"], w_form="kslots_t", epi="normbwd", nb=nb_mix, rows_per_batch=S,
                         payload=[(gw_pw1, False)])
            reduced["w_pw1"][jm] = r[0]
        dx, dsh1, dsc1, dgn = res[:4]
        acc["norm_mix"][i] = dgn.reshape(D)
        dmod_layers[i] = jnp.concatenate([dsh1, dsc1, dg1, dsh2, dsc2, dg2], axis=1).reshape(B, N_MOD * D)
        if i > 0:
            dy2, dg2 = res[4], res[5]
    grad_x = dx.reshape(B, S, D)
    for n, parts in acc.items():
        G[n] = jnp.stack(parts)

    dmod = jnp.stack(dmod_layers)
    dmod_slots = dmod.reshape(L, B, N_DEV, n_mod).transpose(2, 0, 1, 3).reshape(N_DEV, L * B, n_mod)
    dmod_recv = _exchange("gather_dmod", [(dmod_slots, False)])[0]
    dmod_all = dmod_recv.reshape(N_DEV, L, B, n_mod).transpose(1, 0, 2, 3).reshape(L, N_DEV * B, n_mod)
    g_w_mod, db_mod_mine = _mod_bwd(c_all, dmod_all)
    G["b_mod"] = lax.dynamic_update_slice_in_dim(jnp.zeros_like(W["b_mod"]), db_mod_mine.reshape(L, n_mod),
                                                 me * n_mod, axis=1)

    small_items = [jnp.broadcast_to(G[n][None], (N_DEV,) + G[n].shape) for n in SMALL_REPL]
    small_items += [_to_slots(G[n], d) for n, d in SMALL_SHARDED]
    small_slots, small_sizes2 = _pack(small_items, F32, 1)
    small_recv = _exchange("reduce_small", [(small_slots, False)])[0]

    out = {}

    def run_adam(name, parts, names):
        shapes = [W[n].shape for n in names]
        wp, sizes = _pack([W[n] for n in names], F32, 0)
        mp, _ = _pack([M[n] for n in names], F32, 0)
        vp, _ = _pack([V[n] for n in names], F32, 0)
        res = _adam(name, [parts], wp, mp, vp)
        for kind, buf in zip(("grad", "delta", "new_m", "new_v"), res):
            for n, a in zip(names, _unpack(buf, sizes, shapes, 0)):
                out[kind + "_" + n] = a

    for n in BIG + ("w_mod",):
        cols = W[n].shape[-1]
        if n == "w_mod":
            parts = [g_w_mod.reshape(1, -1, cols)]
        else:
            pieces = [p for r in reduced[n] for p in (r if isinstance(r, list) else [r])]
            parts = [p.reshape(N_DEV, -1, cols) for p in pieces]
        res = _adam("adam_" + n, parts, W[n].reshape(-1, cols), M[n].reshape(-1, cols), V[n].reshape(-1, cols))
        for kind, buf in zip(("grad", "delta", "new_m", "new_v"), res):
            out[kind + "_" + n] = buf.reshape(W[n].shape)
    run_adam("adam_small", small_recv, list(SMALL_REPL) + [n for n, _ in SMALL_SHARDED])

    res = [loss, grad_x]
    for kind in ("grad", "delta", "new_m", "new_v"):
        res += [out[kind + "_" + n] for n in WEIGHTS]
    return tuple(res)


def kernel(x, c, w_mod, b_mod, norm_mix, norm_mlp, w_qkv, b_qkv, w_o, b_o, sinks, w_pw1, b_pw1, w_dw, b_dw, conv_ln_g, conv_ln_b, w_pw2, b_pw2, w_up, w_down, final_norm, loss_target, m_w_mod, m_b_mod, m_norm_mix, m_norm_mlp, m_w_qkv, m_b_qkv, m_w_o, m_b_o, m_sinks, m_w_pw1, m_b_pw1, m_w_dw, m_b_dw, m_conv_ln_g, m_conv_ln_b, m_w_pw2, m_b_pw2, m_w_up, m_w_down, m_final_norm, v_w_mod, v_b_mod, v_norm_mix, v_norm_mlp, v_w_qkv, v_b_qkv, v_w_o, v_b_o, v_sinks, v_w_pw1, v_b_pw1, v_w_dw, v_b_dw, v_conv_ln_g, v_conv_ln_b, v_w_pw2, v_b_pw2, v_w_up, v_w_down, v_final_norm):
    W = dict(w_mod=w_mod, b_mod=b_mod, norm_mix=norm_mix, norm_mlp=norm_mlp, w_qkv=w_qkv, b_qkv=b_qkv, w_o=w_o,
             b_o=b_o, sinks=sinks, w_pw1=w_pw1, b_pw1=b_pw1, w_dw=w_dw, b_dw=b_dw, conv_ln_g=conv_ln_g,
             conv_ln_b=conv_ln_b, w_pw2=w_pw2, b_pw2=b_pw2, w_up=w_up, w_down=w_down, final_norm=final_norm)
    M = dict(w_mod=m_w_mod, b_mod=m_b_mod, norm_mix=m_norm_mix, norm_mlp=m_norm_mlp, w_qkv=m_w_qkv, b_qkv=m_b_qkv,
             w_o=m_w_o, b_o=m_b_o, sinks=m_sinks, w_pw1=m_w_pw1, b_pw1=m_b_pw1, w_dw=m_w_dw, b_dw=m_b_dw,
             conv_ln_g=m_conv_ln_g, conv_ln_b=m_conv_ln_b, w_pw2=m_w_pw2, b_pw2=m_b_pw2, w_up=m_w_up,
             w_down=m_w_down, final_norm=m_final_norm)
    V = dict(w_mod=v_w_mod, b_mod=v_b_mod, norm_mix=v_norm_mix, norm_mlp=v_norm_mlp, w_qkv=v_w_qkv, b_qkv=v_b_qkv,
             w_o=v_w_o, b_o=v_b_o, sinks=v_sinks, w_pw1=v_w_pw1, b_pw1=v_b_pw1, w_dw=v_w_dw, b_dw=v_b_dw,
             conv_ln_g=v_conv_ln_g, conv_ln_b=v_conv_ln_b, w_pw2=v_w_pw2, b_pw2=v_b_pw2, w_up=v_w_up,
             w_down=v_w_down, final_norm=v_final_norm)
    return _step(x, c, loss_target, W, M, V)
```

```python
import numpy as np
import jax
import jax.numpy as jnp
from jax import lax
from jax.experimental import pallas as pl
from jax.experimental.pallas import tpu as pltpu

F32 = jnp.float32
BF16 = jnp.bfloat16

N_DEV = 8
N_HEADS = 16
N_KV_HEADS = 2
HEAD_DIM = 64
GROUP = N_HEADS // N_KV_HEADS
ATT_BLOCK = 128
CONV_WIDTH = 31
CONV_HALO = 32
N_MOD = 6
EPS = 1e-6
ADAM_LR = 0.001
ADAM_B1 = 0.9
ADAM_B2 = 0.999
ADAM_EPS = 1e-08
ADAM_WD = 0.01
ADAM_STEP = 10
NEG_BIG = -1e30
PACK_COLS = 1024
VMEM_LIMIT_BYTES = 56 * 1024 * 1024
MESH_ID = pl.DeviceIdType.MESH


def _params(*sem):
    return pltpu.CompilerParams(dimension_semantics=sem, vmem_limit_bytes=VMEM_LIMIT_BYTES)


def _pick(n, pref, mult=8):
    if n <= pref:
        return n
    for t in range(pref, 0, -1):
        if n % t == 0 and t % mult == 0:
            return t
    return n


def _sigmoid(z):
    return 0.5 * jnp.tanh(0.5 * z) + 0.5


def _payload_layout(payload):
    n = len(payload)
    out_shapes = [jax.ShapeDtypeStruct((N_DEV,) + tuple(a.shape if bc else a.shape[1:]), a.dtype) for a, bc in payload]
    hbm = pl.BlockSpec(memory_space=pl.ANY)
    scratch = [pltpu.SemaphoreType.DMA((n * (N_DEV - 1),)), pltpu.SemaphoreType.DMA((n * (N_DEV - 1),)),
               pltpu.SemaphoreType.DMA((n,))]
    return [a for a, _ in payload], [hbm] * n, out_shapes, [hbm] * n, scratch


class _Plan:
    def __init__(self, bcasts, src_refs, dst_refs, send_sems, recv_sems, local_sems):
        x, y, c = lax.axis_index("x"), lax.axis_index("y"), lax.axis_index("c")
        me = 4 * x + 2 * y + c
        self.first, self.landed, self.relay, self.local = [], [], [], []
        for t, (bc, s_ref, d_ref) in enumerate(zip(bcasts, src_refs, dst_refs)):
            def remote(k, src, slot, to):
                sem = t * (N_DEV - 1) + k
                return pltpu.make_async_remote_copy(src_ref=src, dst_ref=d_ref.at[slot], send_sem=send_sems.at[sem],
                                                    recv_sem=recv_sems.at[sem], device_id=to, device_id_type=MESH_ID)
            if bc:
                chips = [(1 - x, y), (x, 1 - y), (1 - x, 1 - y)]
                self.first.append(remote(0, s_ref, me, (x, y, 1 - c)))
                for j, (px, py) in enumerate(chips):
                    cp = remote(1 + j, s_ref, me, (px, py, c))
                    self.first.append(cp)
                    self.landed.append(cp)
                    theirs = 4 * px + 2 * py + c
                    self.relay.append(remote(4 + j, d_ref.at[theirs], theirs, (x, y, 1 - c)))
                self.local.append(pltpu.make_async_copy(s_ref, d_ref.at[me], local_sems.at[t]))
            else:
                for k in range(1, N_DEV):
                    px = 1 - x if (k >> 2) & 1 else x
                    py = 1 - y if (k >> 1) & 1 else y
                    pc = 1 - c if k & 1 else c
                    self.first.append(remote(k - 1, s_ref.at[4 * px + 2 * py + pc], me, (px, py, pc)))
                self.local.append(pltpu.make_async_copy(s_ref.at[me], d_ref.at[me], local_sems.at[t]))

    def start(self):
        for cp in self.first + self.local:
            cp.start()

    def pass_on(self):
        for cp in self.landed:
            cp.wait_recv()
        for cp in self.relay:
            cp.start()

    def finish(self):
        for cp in self.first:
            cp.wait_send()
            if not any(cp is l for l in self.landed):
                cp.wait_recv()
        for cp in self.relay:
            cp.wait()
        for cp in self.local:
            cp.wait()


def _exchange(name, payload):
    n = len(payload)
    bcasts = [bc for _, bc in payload]
    arrays, in_specs, out_shapes, out_specs, scratch = _payload_layout(payload)

    def body(*refs):
        plan = _Plan(bcasts, refs[:n], refs[n:2 * n], *refs[2 * n:])
        plan.start()
        plan.pass_on()
        plan.finish()

    return pl.pallas_call(
        body,
        name=name,
        out_shape=tuple(out_shapes),
        in_specs=in_specs,
        out_specs=tuple(out_specs),
        scratch_shapes=scratch,
    )(*arrays)


def _pack(arrays, dtype, lead):
    lead_shape = arrays[0].shape[:lead]
    flat = [a.astype(dtype).reshape(lead_shape + (-1,)) for a in arrays]
    sizes = [f.shape[-1] for f in flat]
    total = sum(sizes)
    chunk = 16 * PACK_COLS
    padded = -(-total // chunk) * chunk
    if padded > total:
        flat.append(jnp.zeros(lead_shape + (padded - total,), dtype))
    buf = jnp.concatenate(flat, axis=-1)
    return buf.reshape(lead_shape + (padded // PACK_COLS, PACK_COLS)), sizes


def _unpack(buf, sizes, shapes, lead):
    lead_shape = buf.shape[:lead]
    flat = buf.reshape(lead_shape + (-1,))
    out, off = [], 0
    for n, shp in zip(sizes, shapes):
        out.append(lax.slice_in_dim(flat, off, off + n, axis=lead).reshape(lead_shape + tuple(shp)))
        off += n
    return out


def _to_slots(a, dim):
    shp = a.shape
    a = a.reshape(shp[:dim] + (N_DEV, shp[dim] // N_DEV) + shp[dim + 1:])
    return jnp.moveaxis(a, dim, 0)


def _from_slots(a, dim):
    a = jnp.moveaxis(a, 0, dim)
    shp = a.shape
    return a.reshape(shp[:dim] + (shp[dim] * shp[dim + 1],) + shp[dim + 2:])


def _mod_fwd(c_all, w, b):
    L, D, n = w.shape
    M = c_all.shape[0]

    def body(c_ref, w_ref, b_ref, o_ref):
        cv = c_ref[...]
        cs = (cv * _sigmoid(cv)).astype(BF16)
        o_ref[...] = jnp.dot(cs, w_ref[...], preferred_element_type=F32) + b_ref[...]

    return pl.pallas_call(
        body,
        name="mod_fwd",
        grid=(L,),
        out_shape=jax.ShapeDtypeStruct((L, M, n), F32),
        in_specs=[
            pl.BlockSpec((M, D), lambda l: (0, 0)),
            pl.BlockSpec((None, D, n), lambda l: (l, 0, 0)),
            pl.BlockSpec((None, 1, n), lambda l: (l, 0, 0)),
        ],
        out_specs=pl.BlockSpec((None, M, n), lambda l: (l, 0, 0)),
        compiler_params=_params("arbitrary"),
    )(c_all, w, b)


def _mod_bwd(c_all, dmod_all):
    L, M, n = dmod_all.shape
    D = c_all.shape[1]

    def body(c_ref, d_ref, dw_ref, db_ref):
        cv = c_ref[...]
        cs = (cv * _sigmoid(cv)).astype(BF16)
        d = d_ref[...]
        dw_ref[...] = lax.dot_general(cs, d.astype(BF16), (((0,), (0,)), ((), ())), preferred_element_type=F32)
        db_ref[...] = jnp.sum(d, axis=0, keepdims=True)

    return pl.pallas_call(
        body,
        name="mod_bwd",
        grid=(L,),
        out_shape=(jax.ShapeDtypeStruct((L, D, n), F32), jax.ShapeDtypeStruct((L, 1, n), F32)),
        in_specs=[
            pl.BlockSpec((M, D), lambda l: (0, 0)),
            pl.BlockSpec((None, M, n), lambda l: (l, 0, 0)),
        ],
        out_specs=(
            pl.BlockSpec((None, D, n), lambda l: (l, 0, 0)),
            pl.BlockSpec((None, 1, n), lambda l: (l, 0, 0)),
        ),
        compiler_params=_params("arbitrary"),
    )(c_all, dmod_all)


def _mm(name, a, w, *, w_form="full", out_dtype=BF16, bias=None, a_sq=False, epi=None, x=None, gate=None, u=None,
        norm=None, nb=None, ln=None, rows_per_batch=None, tm_pref=512, payload=()):
    M, K = a.shape
    if w_form == "full":
        N = w.shape[1]
        nc = _pick(N, 1024, 128)
    elif w_form == "full_t":
        N = w.shape[0]
        nc = _pick(N, 1024, 128)
    elif w_form == "nslots":
        nc = w.shape[2]
        N = N_DEV * nc
    else:
        N = w.shape[1]
        nc = N
    ks = K // N_DEV
    n_chunks = N // nc
    tm = _pick(M if rows_per_batch is None else rows_per_batch, tm_pref, 16)
    steps = M // tm
    relay_step = (3 * steps) // 4
    nb_gate = nb is not None and nb.get("gate") is not None
    tpb_nb = rows_per_batch // tm if epi == "normbwd" else 1
    has_bias = bias is not None
    n_pay = len(payload)
    bcasts = [bc for _, bc in payload]
    nt = (((1,), (1,)), ((), ()))


    def body(*refs):
        it = iter(refs)
        a_ref = next(it)
        w_ref = next(it)
        b_ref = next(it) if has_bias else None
        x_ref = next(it) if epi == "resid" else None
        g_ref = next(it) if epi == "resid" else None
        gn_ref, sc_ref, sh_ref = (next(it), next(it), next(it)) if norm is not None else (None, None, None)
        u_ref = next(it) if epi == "relu2d" else None
        if epi == "normbwd":
            nx_ref, ndx_ref, ngn_ref, nsc_ref = next(it), next(it), next(it), next(it)
            ny_ref, ngt_ref = (next(it), next(it)) if nb_gate else (None, None)
        if epi == "lnbwd":
            cv_ref, lg_ref, lb_ref = next(it), next(it), next(it)
        pay_src = [next(it) for _ in range(n_pay)]
        o_ref = next(it)
        y_ref = next(it) if epi == "resid" else None
        h_ref = next(it) if norm is not None else None
        if epi == "normbwd":
            dsh_ref, dsc_ref, dgn_ref = next(it), next(it), next(it)
            dy_ref, dgate_ref, cs_ref = (next(it), next(it), next(it)) if nb_gate else (None, None, None)
        if epi == "lnbwd":
            dlg_ref, dlb_ref, dcs_ref = next(it), next(it), next(it)
        pay_dst = [next(it) for _ in range(n_pay)]
        sems = list(it)

        if epi == "lnbwd":
            @pl.when(pl.program_id(0) == 0)
            def _():
                dlg_ref[...] = jnp.zeros_like(dlg_ref)
                dlb_ref[...] = jnp.zeros_like(dlb_ref)
                dcs_ref[...] = jnp.zeros_like(dcs_ref)

        if epi == "normbwd":
            @pl.when(pl.program_id(0) % tpb_nb == 0)
            def _():
                dsh_ref[...] = jnp.zeros_like(dsh_ref)
                dsc_ref[...] = jnp.zeros_like(dsc_ref)
                if nb_gate:
                    dgate_ref[...] = jnp.zeros_like(dgate_ref)

            @pl.when(pl.program_id(0) == 0)
            def _():
                dgn_ref[...] = jnp.zeros_like(dgn_ref)
                if nb_gate:
                    cs_ref[...] = jnp.zeros_like(cs_ref)

        if n_pay:
            @pl.when(pl.program_id(0) == 0)
            def _():
                _Plan(bcasts, pay_src, pay_dst, *sems).start()

            @pl.when(pl.program_id(0) == relay_step)
            def _():
                _Plan(bcasts, pay_src, pay_dst, *sems).pass_on()

        av = None if w_form == "kslots_t" else a_ref[...]
        if a_sq:
            av = av * av
        for ci in range(n_chunks):
            cols = slice(ci * nc, (ci + 1) * nc)
            if w_form == "full":
                acc = jnp.dot(av, w_ref[:, cols], preferred_element_type=F32)
            elif w_form == "full_t":
                acc = lax.dot_general(av, w_ref[cols, :], nt, preferred_element_type=F32)
            elif w_form == "nslots":
                acc = jnp.dot(av, w_ref[ci], preferred_element_type=F32)
            else:
                acc = lax.dot_general(a_ref[:, 0:ks], w_ref[0], nt, preferred_element_type=F32)
                for j in range(1, N_DEV):
                    acc = acc + lax.dot_general(a_ref[:, j * ks:(j + 1) * ks], w_ref[j], nt,
                                                preferred_element_type=F32)
            if has_bias:
                acc = acc + b_ref[:, cols]
            if epi == "resid":
                xn = x_ref[:, cols] + g_ref[:, cols] * acc
                o_ref[:, cols] = xn
                y_ref[:, cols] = acc.astype(BF16)
                if norm is not None:
                    r = lax.rsqrt(jnp.mean(xn * xn, axis=-1, keepdims=True) + EPS)
                    h_ref[...] = ((xn * r * gn_ref[...]) * (1.0 + sc_ref[...]) + sh_ref[...]).astype(BF16)
            elif epi == "relu":
                o_ref[:, cols] = jnp.maximum(acc, 0.0).astype(out_dtype)
            elif epi == "relu2d":
                o_ref[:, cols] = (acc * (2.0 * u_ref[:, cols].astype(F32))).astype(out_dtype)
            elif epi == "normbwd":
                xv = nx_ref[...]
                gn = ngn_ref[...]
                r = lax.rsqrt(jnp.mean(xv * xv, axis=-1, keepdims=True) + EPS)
                xhat = xv * r
                dsh_ref[...] += jnp.sum(acc, axis=0, keepdims=True)
                dsc_ref[...] += jnp.sum(acc * (xhat * gn), axis=0, keepdims=True)
                dn = acc * (1.0 + nsc_ref[...])
                dgn_ref[...] += jnp.sum(dn * xhat, axis=0, keepdims=True)
                dxhat = dn * gn
                dx = ndx_ref[...] + r * (dxhat - xhat * jnp.mean(dxhat * xhat, axis=-1, keepdims=True))
                o_ref[...] = dx
                if nb_gate:
                    dy = dx * ngt_ref[...]
                    dy_ref[...] = dy.astype(BF16)
                    dgate_ref[...] += jnp.sum(dx * ny_ref[...].astype(F32), axis=0, keepdims=True)
                    cs_ref[...] += jnp.sum(dy, axis=0, keepdims=True)
            elif epi == "lnbwd":
                cv = cv_ref[...]
                lg = lg_ref[...]
                xc = cv - jnp.mean(cv, axis=-1, keepdims=True)
                rstd = lax.rsqrt(jnp.mean(xc * xc, axis=-1, keepdims=True) + EPS)
                xhat = xc * rstd
                ln = xhat * lg + lb_ref[...]
                sg = _sigmoid(ln)
                dln = acc * (sg * (1.0 + ln * (1.0 - sg)))
                dlg_ref[...] += jnp.sum(dln * xhat, axis=0, keepdims=True)
                dlb_ref[...] += jnp.sum(dln, axis=0, keepdims=True)
                dxhat = dln * lg
                dcv = rstd * (dxhat - jnp.mean(dxhat, axis=-1, keepdims=True)
                              - xhat * jnp.mean(dxhat * xhat, axis=-1, keepdims=True))
                o_ref[...] = dcv
                dcs_ref[...] += jnp.sum(dcv, axis=0, keepdims=True)
            else:
                o_ref[:, cols] = acc.astype(out_dtype)

        if n_pay:
            @pl.when(pl.program_id(0) == steps - 1)
            def _():
                _Plan(bcasts, pay_src, pay_dst, *sems).finish()

    args = [a, w]
    w_block = w.shape
    specs = [pl.BlockSpec((tm, K), lambda i: (i, 0)), pl.BlockSpec(w_block, lambda i: (0,) * len(w_block))]
    if has_bias:
        args.append(bias.reshape(1, N).astype(F32))
        specs.append(pl.BlockSpec((1, N), lambda i: (0, 0)))
    row_spec = pl.BlockSpec((tm, N), lambda i: (i, 0))
    if epi == "resid":
        tpb = rows_per_batch // tm
        vec_spec = pl.BlockSpec((None, 1, N), lambda i: (i // tpb, 0, 0))
        args += [x, gate]
        specs += [row_spec, vec_spec]
        out_shape = [jax.ShapeDtypeStruct((M, N), F32), jax.ShapeDtypeStruct((M, N), BF16)]
        out_specs = [row_spec, row_spec]
        if norm is not None:
            assert n_chunks == 1
            args += list(norm)
            specs += [pl.BlockSpec((1, N), lambda i: (0, 0)), vec_spec, vec_spec]
            out_shape.append(jax.ShapeDtypeStruct((M, N), BF16))
            out_specs.append(row_spec)
    elif epi == "normbwd":
        assert n_chunks == 1
        vec_spec = pl.BlockSpec((None, 1, N), lambda i: (i // tpb_nb, 0, 0))
        one_spec = pl.BlockSpec((1, N), lambda i: (0, 0))
        nbat = M // rows_per_batch
        f_vec, f_one = jax.ShapeDtypeStruct((nbat, 1, N), F32), jax.ShapeDtypeStruct((1, N), F32)
        args += [nb["x"], nb["dx_in"], nb["gnorm"], nb["sc"]]
        specs += [row_spec, row_spec, one_spec, vec_spec]
        out_shape = [jax.ShapeDtypeStruct((M, N), F32), f_vec, f_vec, f_one]
        out_specs = [row_spec, vec_spec, vec_spec, one_spec]
        if nb_gate:
            args += list(nb["gate"])
            specs += [row_spec, vec_spec]
            out_shape += [jax.ShapeDtypeStruct((M, N), BF16), f_vec, f_one]
            out_specs += [row_spec, vec_spec, one_spec]
    elif epi == "lnbwd":
        assert n_chunks == 1
        one_spec = pl.BlockSpec((1, N), lambda i: (0, 0))
        f_one = jax.ShapeDtypeStruct((1, N), F32)
        args += list(ln)
        specs += [row_spec, one_spec, one_spec]
        out_shape = [jax.ShapeDtypeStruct((M, N), F32), f_one, f_one, f_one]
        out_specs = [row_spec, one_spec, one_spec, one_spec]
    else:
        if epi == "relu2d":
            args.append(u)
            specs.append(row_spec)
        out_shape = [jax.ShapeDtypeStruct((M, N), out_dtype)]
        out_specs = [row_spec]
    scratch = []
    if n_pay:
        p_arrays, p_in, p_shapes, p_out, scratch = _payload_layout(payload)
        args += p_arrays
        specs += p_in
        out_shape += p_shapes
        out_specs += p_out
    res = pl.pallas_call(
        body,
        name=name,
        grid=(steps,),
        out_shape=tuple(out_shape),
        in_specs=specs,
        out_specs=tuple(out_specs),
        scratch_shapes=scratch,
        compiler_params=_params("arbitrary" if (n_pay or epi in ("normbwd", "lnbwd")) else "parallel"),
    )(*args)
    if epi == "resid":
        n_own = 3 if norm is not None else 2
    elif epi == "normbwd":
        n_own = 7 if nb_gate else 4
    elif epi == "lnbwd":
        n_own = 4
    else:
        n_own = 1
    own = res[0] if n_own == 1 else tuple(res[:n_own])
    return (own, list(res[n_own:])) if n_pay else own


def _mm_tn(name, a, b, *, a_sq=False, out_form="full", halves=False, colsum=False, tt_pref=1024, tk_pref=1024,
           tn_pref=1024, payload=()):
    T, K = a.shape
    N = b.shape[1]
    tt = _pick(T, tt_pref, 16)
    tk = K // N_DEV if out_form == "kslots" else _pick(K, tk_pref, 128)
    tn = N // N_DEV if out_form == "nslots" else _pick(N, tn_pref, 128)
    nt_steps = T // tt
    grid = (K // tk, N // tn, nt_steps)
    slots = out_form != "full"
    assert not colsum or K == tk
    n_pay = len(payload)
    bcasts = [bc for _, bc in payload]

    def body(*refs):
        it = iter(refs)
        a_ref, b_ref = next(it), next(it)
        pay_src = [next(it) for _ in range(n_pay)]
        o_ref = next(it)
        o2_ref = next(it) if halves else None
        cs_ref = next(it) if colsum else None
        pay_dst = [next(it) for _ in range(n_pay)]
        acc = next(it) if slots else o_ref
        sems = list(it)
        step = (pl.program_id(0) * grid[1] + pl.program_id(1)) * grid[2] + pl.program_id(2)

        if n_pay:
            @pl.when(step == 0)
            def _():
                _Plan(bcasts, pay_src, pay_dst, *sems).start()

        @pl.when(pl.program_id(2) == 0)
        def _():
            acc[...] = jnp.zeros_like(acc)
            if colsum:
                cs_ref[...] = jnp.zeros_like(cs_ref)

        bv = b_ref[...]
        av = a_ref[...]
        if a_sq:
            av = av * av
        acc[...] += lax.dot_general(av, bv, (((0,), (0,)), ((), ())), preferred_element_type=F32)
        if colsum:
            cs_ref[...] += jnp.sum(bv.astype(F32), axis=0, keepdims=True)

        if slots:
            @pl.when(pl.program_id(2) == nt_steps - 1)
            def _():
                if halves:
                    o_ref[...] = acc[0:tk // 2, :].astype(o_ref.dtype)
                    o2_ref[...] = acc[tk // 2:, :].astype(o2_ref.dtype)
                else:
                    o_ref[...] = acc[...].astype(o_ref.dtype)

        if n_pay:
            @pl.when(step == grid[0] * grid[1] * grid[2] - 1)
            def _():
                plan = _Plan(bcasts, pay_src, pay_dst, *sems)
                plan.pass_on()
                plan.finish()

    if out_form == "full":
        out_shape = [jax.ShapeDtypeStruct((K, N), F32)]
        out_specs = [pl.BlockSpec((tk, tn), lambda k, n, t: (k, n))]
    else:
        assert not halves or out_form == "kslots" or K == tk
        rows = tk // 2 if halves else tk
        if out_form == "nslots":
            shape, spec = (N_DEV, K // tk * rows, tn), pl.BlockSpec((None, rows, tn), lambda k, n, t: (n, k, 0))
        else:
            shape, spec = (N_DEV, rows, N), pl.BlockSpec((None, rows, tn), lambda k, n, t: (k, 0, n))
        out_shape = [jax.ShapeDtypeStruct(shape, BF16)] * (2 if halves else 1)
        out_specs = [spec] * (2 if halves else 1)
    if colsum:
        out_shape.append(jax.ShapeDtypeStruct((1, N), F32))
        out_specs.append(pl.BlockSpec((1, tn), lambda k, n, t: (0, n)))
    args = [a, b]
    in_specs = [pl.BlockSpec((tt, tk), lambda k, n, t: (t, k)), pl.BlockSpec((tt, tn), lambda k, n, t: (t, n))]
    scratch = [pltpu.VMEM((tk, tn), F32)] if slots else []
    n_own = len(out_shape)
    if n_pay:
        p_arrays, p_in, p_shapes, p_out, p_scratch = _payload_layout(payload)
        args += p_arrays
        in_specs += p_in
        out_shape += p_shapes
        out_specs += p_out
        scratch += p_scratch
    res = pl.pallas_call(
        body,
        name=name,
        grid=grid,
        out_shape=tuple(out_shape),
        in_specs=in_specs,
        out_specs=tuple(out_specs),
        scratch_shapes=scratch,
        compiler_params=_params(*(("arbitrary",) * 3 if n_pay else ("parallel", "parallel", "arbitrary"))),
    )(*args)
    own = res[0] if n_own == 1 else tuple(res[:n_own])
    return (own, list(res[n_own:])) if n_pay else own


def _normmod_fwd(name, x, gnorm, sc, sh, S, tm_pref=512):
    T, D = x.shape
    tm = _pick(S, tm_pref, 16)
    tpb = S // tm

    def body(x_ref, g_ref, sc_ref, sh_ref, o_ref):
        xv = x_ref[...]
        r = lax.rsqrt(jnp.mean(xv * xv, axis=-1, keepdims=True) + EPS)
        n = xv * r * g_ref[...]
        o_ref[...] = (n * (1.0 + sc_ref[...]) + sh_ref[...]).astype(BF16)

    row = pl.BlockSpec((tm, D), lambda i: (i, 0))
    vec = pl.BlockSpec((None, 1, D), lambda i: (i // tpb, 0, 0))
    return pl.pallas_call(
        body,
        name=name,
        grid=(T // tm,),
        out_shape=jax.ShapeDtypeStruct((T, D), BF16),
        in_specs=[row, pl.BlockSpec((1, D), lambda i: (0, 0)), vec, vec],
        out_specs=row,
        compiler_params=_params("parallel"),
    )(x, gnorm, sc, sh)


ATT_ROWS = GROUP * ATT_BLOCK
ATT_SPAN = 2 * ATT_BLOCK
ATT_SCALE = HEAD_DIM ** -0.5


def _attn_table(sinks):
    slopes = jnp.asarray(np.array([2.0 ** (-8.0 * (h + 1) / N_HEADS) for h in range(N_HEADS)], np.float32))
    r = jnp.arange(ATT_BLOCK)[:, None]
    cc = jnp.arange(ATT_SPAN)[None, :]
    dist = r + ATT_BLOCK - cc
    ok = (dist >= 0) & (dist < ATT_BLOCK)
    tab = jnp.where(ok[None], -slopes[:, None, None] * dist.astype(F32)[None], NEG_BIG)
    tab = jnp.where((cc == 0)[None], sinks.astype(F32)[:, None, None], tab)
    return tab.reshape(N_KV_HEADS, ATT_ROWS, ATT_SPAN)


KV_COLS = N_KV_HEADS * HEAD_DIM


PAIR = 2 * HEAD_DIM


def _pair_tables(tab):
    hkv, rows, span = tab.shape
    pairs = rows // (2 * ATT_BLOCK)
    return tab.reshape(hkv, pairs, 2, ATT_BLOCK, span).transpose(0, 2, 1, 3, 4).reshape(hkv, 2, pairs * ATT_BLOCK, span)


def _stack_pairs(ref, pairs, col0):
    return jnp.concatenate([ref[:, col0 + j * PAIR:col0 + (j + 1) * PAIR] for j in range(pairs)], axis=0)


def _unstack_pairs(v, pairs):
    return jnp.concatenate([v[j * ATT_BLOCK:(j + 1) * ATT_BLOCK, :] for j in range(pairs)], axis=1)


def _swap_halves(x):
    return jnp.concatenate([x[:, HEAD_DIM:], x[:, :HEAD_DIM]], axis=1)


def _load_span(prev_ref, cur_ref, buf_e, buf_o, kv, mult):
    lane = lax.broadcasted_iota(jnp.int32, (ATT_BLOCK, PAIR), 1)
    row = lax.broadcasted_iota(jnp.int32, (ATT_BLOCK, PAIR), 0)
    own = lane // HEAD_DIM == kv
    for r0, ref, first in ((0, prev_ref, True), (ATT_BLOCK, cur_ref, False)):
        v = ref[...]
        if mult != 1.0:
            v = v * mult
        zero = jnp.zeros_like(v)
        mine = jnp.where(own, v, zero)
        if first:
            mine = jnp.where(row > 0, mine, zero)
        other = _swap_halves(mine)
        buf_e[r0:r0 + ATT_BLOCK, :] = mine if kv == 0 else other
        buf_o[r0:r0 + ATT_BLOCK, :] = other if kv == 0 else mine


def _first_block_penalty(i):
    col = lax.broadcasted_iota(jnp.int32, (1, ATT_SPAN), 1)
    return jnp.where((col < ATT_BLOCK) & (col > 0), jnp.where(i > 0, 0.0, NEG_BIG), 0.0).astype(F32)


def _attn_probs(qp, kbuf, bias, first_pen):
    nt = (((1,), (1,)), ((), ()))
    s = lax.dot_general(qp, kbuf, nt, preferred_element_type=F32) + bias + first_pen
    m = jnp.max(s, axis=-1, keepdims=True)
    e = jnp.exp(s - m)
    return e * (1.0 / jnp.sum(e, axis=-1, keepdims=True))


def _attn_specs(nb, q_dim, pairs, row_q, row_cur, row_prev):
    kblk = q_dim // KV_COLS
    qspec = pl.BlockSpec((ATT_BLOCK, q_dim), lambda b, i: (row_q(b, i), 0))
    kv_specs = [pl.BlockSpec((ATT_BLOCK, KV_COLS), (lambda b, i, r=r, c=c: (r(b, i), c)))
                for c in (kblk, kblk + 1) for r in (row_cur, row_prev)]
    tspec = pl.BlockSpec((N_KV_HEADS, 2, pairs * ATT_BLOCK, ATT_SPAN), lambda b, i: (0, 0, 0, 0))
    return qspec, kv_specs, tspec


def _attn_fwd(qkv, tab2, B, S, payload=()):
    assert N_KV_HEADS == 2 and GROUP % 2 == 0
    T = qkv.shape[0]
    pairs = GROUP // 2
    gw = pairs * PAIR
    nb = S // ATT_BLOCK
    q_dim = N_HEADS * HEAD_DIM
    n_pay = len(payload)
    bcasts = [bc for _, bc in payload]
    steps = B * nb

    def body(*refs):
        q_ref, kc_ref, kp_ref, vc_ref, vp_ref, tab_ref = refs[:6]
        pay_src = refs[6:6 + n_pay]
        o_ref = refs[6 + n_pay]
        pay_dst = refs[7 + n_pay:7 + 2 * n_pay]
        ke, ko, ve, vo = refs[7 + 2 * n_pay:11 + 2 * n_pay]
        sems = refs[11 + 2 * n_pay:]
        i = pl.program_id(1)
        if n_pay:
            step = pl.program_id(0) * nb + i

            @pl.when(step == 0)
            def _():
                _Plan(bcasts, pay_src, pay_dst, *sems).start()

            @pl.when(step == (3 * steps) // 4)
            def _():
                _Plan(bcasts, pay_src, pay_dst, *sems).pass_on()

            @pl.when(step == steps - 1)
            def _():
                _Plan(bcasts, pay_src, pay_dst, *sems).finish()

        pen = _first_block_penalty(i)
        for kv in range(N_KV_HEADS):
            _load_span(kp_ref, kc_ref, ke.at[kv], ko.at[kv], kv, ATT_SCALE)
            _load_span(vp_ref, vc_ref, ve.at[kv], vo.at[kv], kv, 1.0)
            qp = _stack_pairs(q_ref, pairs, kv * gw)
            p_e = _attn_probs(qp, ke[kv], tab_ref[kv, 0], pen)
            p_o = _attn_probs(qp, ko[kv], tab_ref[kv, 1], pen)
            o = jnp.dot(p_e.astype(BF16), ve[kv], preferred_element_type=F32)
            o = o + jnp.dot(p_o.astype(BF16), vo[kv], preferred_element_type=F32)
            o_ref[:, kv * gw:(kv + 1) * gw] = _unstack_pairs(o.astype(BF16), pairs)

    qspec, kv_specs, tspec = _attn_specs(nb, q_dim, pairs, lambda b, i: b * nb + i, lambda b, i: b * nb + i,
                                         lambda b, i: b * nb + jnp.maximum(i - 1, 0))
    args, in_specs = [qkv] * 5 + [tab2], [qspec] + kv_specs + [tspec]
    out_shape, out_specs = [jax.ShapeDtypeStruct((T, q_dim), BF16)], [qspec]
    scratch = [pltpu.VMEM((N_KV_HEADS, ATT_SPAN, PAIR), BF16)] * 4
    if n_pay:
        p_arrays, p_in, p_shapes, p_out, p_scratch = _payload_layout(payload)
        args += p_arrays
        in_specs += p_in
        out_shape += p_shapes
        out_specs += p_out
        scratch += p_scratch
    res = pl.pallas_call(
        body,
        name="attn_fwd",
        grid=(B, nb),
        out_shape=tuple(out_shape),
        in_specs=in_specs,
        out_specs=tuple(out_specs),
        scratch_shapes=scratch,
        compiler_params=_params(*(("arbitrary",) * 2 if n_pay else ("parallel", "arbitrary"))),
    )(*args)
    return (res[0], list(res[1:])) if n_pay else res[0]


def _attn_bwd(qkv, o, do, tab2, B, S):
    T = qkv.shape[0]
    hd = HEAD_DIM
    pairs = GROUP // 2
    gw = pairs * PAIR
    rows = pairs * ATT_BLOCK
    nb = S // ATT_BLOCK
    last = nb - 1
    q_dim = N_HEADS * hd
    tn = (((0,), (0,)), ((), ()))
    nt = (((1,), (1,)), ((), ()))

    def body(q_ref, kc_ref, kp_ref, vc_ref, vp_ref, tab_ref, o_ref, do_ref,
             dq_ref, dkv_ref, dsink_ref, carry_k, carry_v, ke, ko, ve, vo):
        i = pl.program_id(1)

        @pl.when(i == 0)
        def _():
            dsink_ref[...] = jnp.zeros_like(dsink_ref)

        def emit(kv, dk_rows, dv_rows):
            dkv_ref[:, kv * hd:(kv + 1) * hd] = dk_rows.astype(BF16)
            dkv_ref[:, KV_COLS + kv * hd:KV_COLS + (kv + 1) * hd] = dv_rows.astype(BF16)

        @pl.when(i < nb)
        def _():
            pen = _first_block_penalty(i)
            low = lax.broadcasted_iota(jnp.int32, (rows, PAIR), 1) < hd
            low_k = lax.broadcasted_iota(jnp.int32, (ATT_SPAN, PAIR), 1) < hd
            keep = lax.broadcasted_iota(jnp.int32, (ATT_SPAN, hd), 0) > 0
            hg = lax.broadcasted_iota(jnp.int32, (pairs, rows), 0)
            hr = lax.broadcasted_iota(jnp.int32, (pairs, rows), 1)
            head_of = jnp.where(hr // ATT_BLOCK == hg, 1.0, 0.0).astype(BF16)
            dks, dvs = [], []
            for kv in range(N_KV_HEADS):
                _load_span(kp_ref, kc_ref, ke.at[kv], ko.at[kv], kv, ATT_SCALE)
                _load_span(vp_ref, vc_ref, ve.at[kv], vo.at[kv], kv, 1.0)
                qp = _stack_pairs(q_ref, pairs, kv * gw)
                dop = _stack_pairs(do_ref, pairs, kv * gw)
                op = _stack_pairs(o_ref, pairs, kv * gw)
                p_e = _attn_probs(qp, ke[kv], tab_ref[kv, 0], pen)
                p_o = _attn_probs(qp, ko[kv], tab_ref[kv, 1], pen)
                prod = dop.astype(F32) * op.astype(F32)
                d_e = jnp.sum(jnp.where(low, prod, 0.0), axis=-1, keepdims=True)
                d_o = jnp.sum(prod, axis=-1, keepdims=True) - d_e
                ds_e = (p_e * (lax.dot_general(dop, ve[kv], nt, preferred_element_type=F32) - d_e)).astype(BF16)
                ds_o = (p_o * (lax.dot_general(dop, vo[kv], nt, preferred_element_type=F32) - d_o)).astype(BF16)
                dq = jnp.dot(ds_e, ke[kv], preferred_element_type=F32) + jnp.dot(ds_o, ko[kv], preferred_element_type=F32)
                dq_ref[:, kv * gw:(kv + 1) * gw] = _unstack_pairs(dq.astype(BF16), pairs)
                dsink_ref[kv, 0] += jnp.dot(head_of, ds_e, preferred_element_type=F32)[:, 0:128]
                dsink_ref[kv, 1] += jnp.dot(head_of, ds_o, preferred_element_type=F32)[:, 0:128]
                dk2 = jnp.where(low_k, lax.dot_general(ds_e, qp, tn, preferred_element_type=F32),
                                lax.dot_general(ds_o, qp, tn, preferred_element_type=F32))
                dv2 = jnp.where(low_k, lax.dot_general(p_e.astype(BF16), dop, tn, preferred_element_type=F32),
                                lax.dot_general(p_o.astype(BF16), dop, tn, preferred_element_type=F32))
                dks.append(jnp.where(keep, (dk2[:, :hd] + dk2[:, hd:]) * ATT_SCALE, 0.0))
                dvs.append(jnp.where(keep, dv2[:, :hd] + dv2[:, hd:], 0.0))

            @pl.when(i > 0)
            def _():
                for kv in range(N_KV_HEADS):
                    emit(kv, carry_k[kv] + dks[kv][0:ATT_BLOCK], carry_v[kv] + dvs[kv][0:ATT_BLOCK])

            for kv in range(N_KV_HEADS):
                carry_k[kv] = dks[kv][ATT_BLOCK:]
                carry_v[kv] = dvs[kv][ATT_BLOCK:]

        @pl.when(i == nb)
        def _():
            for kv in range(N_KV_HEADS):
                emit(kv, carry_k[kv], carry_v[kv])

    qspec, kv_specs, tspec = _attn_specs(nb, q_dim, pairs, lambda b, i: b * nb + jnp.minimum(i, last),
                                         lambda b, i: b * nb + jnp.minimum(i, last),
                                         lambda b, i: b * nb + jnp.clip(i - 1, 0, last))
    dkv = pl.BlockSpec((ATT_BLOCK, 2 * KV_COLS), lambda b, i: (b * nb + jnp.maximum(i - 1, 0), 0))
    dsk = pl.BlockSpec((None, N_KV_HEADS, 2, pairs, 128), lambda b, i: (b, 0, 0, 0, 0))
    return pl.pallas_call(
        body,
        name="attn_bwd",
        grid=(B, nb + 1),
        out_shape=(
            jax.ShapeDtypeStruct((T, q_dim), BF16),
            jax.ShapeDtypeStruct((T, 2 * KV_COLS), BF16),
            jax.ShapeDtypeStruct((B, N_KV_HEADS, 2, pairs, 128), F32),
        ),
        in_specs=[qspec] + kv_specs + [tspec, qspec, qspec],
        out_specs=(qspec, dkv, dsk),
        scratch_shapes=[pltpu.VMEM((N_KV_HEADS, ATT_BLOCK, hd), F32), pltpu.VMEM((N_KV_HEADS, ATT_BLOCK, hd), F32)]
        + [pltpu.VMEM((N_KV_HEADS, ATT_SPAN, PAIR), BF16)] * 4,
        compiler_params=_params("arbitrary", "arbitrary"),
    )(qkv, qkv, qkv, qkv, qkv, tab2, o, do)


def _conv_tiles(S):
    ts = _pick(S, 256, CONV_HALO)
    return ts, S // ts


def _conv_chunks(C, ts):
    lane = _pick(C, 128, 128)
    return lane, C // lane, _pick(ts, 128, 8)


def _conv_weight_chunks(w_dw, C):
    lane = _pick(C, 128, 128)
    w = jnp.pad(w_dw, ((0, CONV_HALO - CONV_WIDTH), (0, 0)))
    return w.reshape(CONV_HALO, C // lane, lane).transpose(1, 0, 2)


def _conv_fwd(u, w3, b_dw, ln_g, ln_b, S):
    T, C2 = u.shape
    C = C2 // 2
    B = T // S
    ts, nj = _conv_tiles(S)
    hb = ts // CONV_HALO
    lane, nc, rc = _conv_chunks(C, ts)

    def body(a_ref, g_ref, ap_ref, gp_ref, w_ref, bdw_ref, lg_ref, lb_ref, cv_ref, s_ref, buf, cvb):
        j = pl.program_id(1)
        glu_prev = ap_ref[...] * _sigmoid(gp_ref[...]) * (j > 0).astype(F32)
        glu = a_ref[...] * _sigmoid(g_ref[...])
        for cc in range(nc):
            buf[cc, 0:CONV_HALO, :] = glu_prev[:, cc * lane:(cc + 1) * lane]
            buf[cc, CONV_HALO:, :] = glu[:, cc * lane:(cc + 1) * lane]

        def chunk(cc, carry):
            for r0 in range(0, ts, rc):
                acc = jnp.zeros((rc, lane), F32)
                for kk in range(CONV_WIDTH):
                    lo = CONV_HALO - (CONV_WIDTH - 1 - kk) + r0
                    acc = acc + w_ref[cc, kk:kk + 1, :] * buf[cc, lo:lo + rc, :]
                cvb[cc, r0:r0 + rc, :] = acc
            return carry

        lax.fori_loop(0, nc, chunk, 0)
        for cc in range(nc):
            cv_ref[:, cc * lane:(cc + 1) * lane] = cvb[cc] + bdw_ref[:, cc * lane:(cc + 1) * lane]
        cv = cv_ref[...]
        mu = jnp.mean(cv, axis=-1, keepdims=True)
        xc = cv - mu
        rstd = lax.rsqrt(jnp.mean(xc * xc, axis=-1, keepdims=True) + EPS)
        ln = xc * rstd * lg_ref[...] + lb_ref[...]
        s_ref[...] = (ln * _sigmoid(ln)).astype(BF16)

    a_cur = pl.BlockSpec((ts, C), lambda b, j: (b * nj + j, 0))
    g_cur = pl.BlockSpec((ts, C), lambda b, j: (b * nj + j, 1))
    a_prev = pl.BlockSpec((CONV_HALO, C), lambda b, j: (jnp.maximum((b * nj + j) * hb - 1, 0), 0))
    g_prev = pl.BlockSpec((CONV_HALO, C), lambda b, j: (jnp.maximum((b * nj + j) * hb - 1, 0), 1))
    wspec = pl.BlockSpec((nc, CONV_HALO, lane), lambda b, j: (0, 0, 0))
    one = pl.BlockSpec((1, C), lambda b, j: (0, 0))
    return pl.pallas_call(
        body,
        name="conv_fwd",
        grid=(B, nj),
        out_shape=(jax.ShapeDtypeStruct((T, C), F32), jax.ShapeDtypeStruct((T, C), BF16)),
        in_specs=[a_cur, g_cur, a_prev, g_prev, wspec, one, one, one],
        out_specs=(a_cur, a_cur),
        scratch_shapes=[pltpu.VMEM((nc, CONV_HALO + ts, lane), F32), pltpu.VMEM((nc, ts, lane), F32)],
        compiler_params=_params("parallel", "arbitrary"),
    )(u, u, u, u, w3, b_dw, ln_g, ln_b)


def _conv_bwd(dcv, u, w3, S):
    T, C2 = u.shape
    C = C2 // 2
    B = T // S
    ts, nj = _conv_tiles(S)
    hb = ts // CONV_HALO
    n_halo_blocks = T // CONV_HALO
    lane, nc, rc = _conv_chunks(C, ts)

    def body(dcv_ref, dnx_ref, a_ref, g_ref, ap_ref, gp_ref, w_ref, du_ref, dw_ref, gbuf, dbuf, dglu, dw8):
        b, j = pl.program_id(0), pl.program_id(1)

        @pl.when((b == 0) & (j == 0))
        def _():
            dw8[...] = jnp.zeros_like(dw8)

        a = a_ref[...]
        sg = _sigmoid(g_ref[...])
        glu_prev = ap_ref[...] * _sigmoid(gp_ref[...]) * (j > 0).astype(F32)
        glu = a * sg
        dcur = dcv_ref[...]
        dnext = dnx_ref[...] * (j < nj - 1).astype(F32)
        for cc in range(nc):
            cols = slice(cc * lane, (cc + 1) * lane)
            gbuf[cc, 0:CONV_HALO, :] = glu_prev[:, cols]
            gbuf[cc, CONV_HALO:, :] = glu[:, cols]
            dbuf[cc, 0:ts, :] = dcur[:, cols]
            dbuf[cc, ts:, :] = dnext[:, cols]

        def chunk(cc, carry):
            for r0 in range(0, ts, rc):
                acc = jnp.zeros((rc, lane), F32)
                for kk in range(CONV_WIDTH):
                    d = CONV_WIDTH - 1 - kk
                    acc = acc + w_ref[cc, kk:kk + 1, :] * dbuf[cc, r0 + d:r0 + d + rc, :]
                dglu[cc, r0:r0 + rc, :] = acc
            for kk in range(CONV_WIDTH):
                d = CONV_WIDTH - 1 - kk
                p = jnp.zeros((rc, lane), F32)
                for r0 in range(0, ts, rc):
                    lo = CONV_HALO - d + r0
                    p = p + dbuf[cc, r0:r0 + rc, :] * gbuf[cc, lo:lo + rc, :]
                dw8[cc, kk * 8:(kk + 1) * 8, :] += jnp.sum(p.reshape(rc // 8, 8, lane), axis=0)
            return carry

        lax.fori_loop(0, nc, chunk, 0)
        for cc in range(nc):
            cols = slice(cc * lane, (cc + 1) * lane)
            dgl = dglu[cc]
            du_ref[:, cc * lane:(cc + 1) * lane] = (dgl * sg[:, cols]).astype(BF16)
            du_ref[:, C + cc * lane:C + (cc + 1) * lane] = (dgl * a[:, cols] * sg[:, cols] * (1.0 - sg[:, cols])).astype(BF16)

        @pl.when((b == B - 1) & (j == nj - 1))
        def _():
            dw_ref[...] = jnp.zeros_like(dw_ref)
            for kk in range(CONV_WIDTH):
                dw_ref[:, kk:kk + 1, :] = jnp.sum(dw8[:, kk * 8:(kk + 1) * 8, :], axis=1, keepdims=True)

    a_cur = pl.BlockSpec((ts, C), lambda b, j: (b * nj + j, 0))
    g_cur = pl.BlockSpec((ts, C), lambda b, j: (b * nj + j, 1))
    a_prev = pl.BlockSpec((CONV_HALO, C), lambda b, j: (jnp.maximum((b * nj + j) * hb - 1, 0), 0))
    g_prev = pl.BlockSpec((CONV_HALO, C), lambda b, j: (jnp.maximum((b * nj + j) * hb - 1, 0), 1))
    d_next = pl.BlockSpec((CONV_HALO, C), lambda b, j: (jnp.minimum((b * nj + j + 1) * hb, n_halo_blocks - 1), 0))
    wspec = pl.BlockSpec((nc, CONV_HALO, lane), lambda b, j: (0, 0, 0))
    return pl.pallas_call(
        body,
        name="conv_bwd",
        grid=(B, nj),
        out_shape=(jax.ShapeDtypeStruct((T, C2), BF16), jax.ShapeDtypeStruct((nc, CONV_HALO, lane), F32)),
        in_specs=[a_cur, d_next, a_cur, g_cur, a_prev, g_prev, wspec],
        out_specs=(pl.BlockSpec((ts, C2), lambda b, j: (b * nj + j, 0)), wspec),
        scratch_shapes=[
            pltpu.VMEM((nc, CONV_HALO + ts, lane), F32),
            pltpu.VMEM((nc, ts + CONV_HALO, lane), F32),
            pltpu.VMEM((nc, ts, lane), F32),
            pltpu.VMEM((nc, CONV_HALO * 8, lane), F32),
        ],
        compiler_params=_params("arbitrary", "arbitrary"),
    )(dcv, dcv, u, u, u, u, w3)


def _loss_head(x, tgt, gfin, y, gate, S, tm_pref=256):
    T, D = x.shape
    tm = _pick(S, tm_pref, 16)
    tpb = S // tm

    def body(x_ref, t_ref, g_ref, y_ref, gt_ref, dx_ref, loss_ref, dg_ref, dy_ref, dgate_ref):
        @pl.when(pl.program_id(0) == 0)
        def _():
            loss_ref[...] = jnp.zeros_like(loss_ref)
            dg_ref[...] = jnp.zeros_like(dg_ref)

        @pl.when(pl.program_id(0) % tpb == 0)
        def _():
            dgate_ref[...] = jnp.zeros_like(dgate_ref)

        xv = x_ref[...]
        g = g_ref[...]
        r = lax.rsqrt(jnp.mean(xv * xv, axis=-1, keepdims=True) + EPS)
        xhat = xv * r
        e = xhat * g - t_ref[...]
        row_loss = jnp.mean(e * e, axis=-1, keepdims=True)
        loss_ref[...] += 0.5 * jnp.sum(row_loss, axis=0, keepdims=True)
        dy = e * (1.0 / D)
        dg_ref[...] += jnp.sum(dy * xhat, axis=0, keepdims=True)
        dxhat = dy * g
        dx = r * (dxhat - xhat * jnp.mean(dxhat * xhat, axis=-1, keepdims=True))
        dx_ref[...] = dx
        dy_ref[...] = (dx * gt_ref[...]).astype(BF16)
        dgate_ref[...] += jnp.sum(dx * y_ref[...].astype(F32), axis=0, keepdims=True)

    row = pl.BlockSpec((tm, D), lambda i: (i, 0))
    vec = pl.BlockSpec((None, 1, D), lambda i: (i // tpb, 0, 0))
    one = pl.BlockSpec((1, D), lambda i: (0, 0))
    return pl.pallas_call(
        body,
        name="loss_head",
        grid=(T // tm,),
        out_shape=(
            jax.ShapeDtypeStruct((T, D), F32),
            jax.ShapeDtypeStruct((8, 128), F32),
            jax.ShapeDtypeStruct((1, D), F32),
            jax.ShapeDtypeStruct((T, D), BF16),
            jax.ShapeDtypeStruct((T // S, 1, D), F32),
        ),
        in_specs=[row, row, one, row, vec],
        out_specs=(row, pl.BlockSpec((8, 128), lambda i: (0, 0)), one, row, vec),
        compiler_params=_params("arbitrary"),
    )(x, tgt, gfin, y, gate)


def _adam(name, parts, w, m, v, tm_pref=256):
    L = len(parts)
    P, R, C = parts[0].shape
    fit = VMEM_LIMIT_BYTES * 5 // 8 // (2 * L * P * C * parts[0].dtype.itemsize)
    tm = _pick(R, max(16, min(tm_pref, fit // 16 * 16)), 16)
    tiles = R // tm
    c1 = 1.0 - ADAM_B1 ** ADAM_STEP
    c2 = 1.0 - ADAM_B2 ** ADAM_STEP

    def body(*refs):
        p_refs = refs[:L]
        w_ref, m_ref, v_ref, g_ref, d_ref, mo_ref, vo_ref = refs[L:]
        for l in range(L):
            @pl.when(pl.program_id(0) == l)
            def _():
                g = p_refs[l][0].astype(F32)
                for i in range(1, P):
                    g = g + p_refs[l][i].astype(F32)
                m_new = ADAM_B1 * m_ref[...] + (1.0 - ADAM_B1) * g
                v_new = ADAM_B2 * v_ref[...] + (1.0 - ADAM_B2) * (g * g)
                m_hat = m_new / c1
                v_hat = v_new / c2
                g_ref[...] = g
                d_ref[...] = -ADAM_LR * (m_hat / (jnp.sqrt(v_hat) + ADAM_EPS) + ADAM_WD * w_ref[...])
                mo_ref[...] = m_new
                vo_ref[...] = v_new

    def p_spec(l):
        return pl.BlockSpec((P, tm, C), lambda li, t: (0, jnp.where(li == l, t, jnp.where(li < l, 0, tiles - 1)), 0))

    row = pl.BlockSpec((tm, C), lambda li, t: (li * tiles + t, 0))
    return pl.pallas_call(
        body,
        name=name,
        grid=(L, tiles),
        out_shape=(jax.ShapeDtypeStruct((L * R, C), F32),) * 4,
        in_specs=[p_spec(l) for l in range(L)] + [row, row, row],
        out_specs=(row, row, row, row),
        compiler_params=_params("arbitrary", "arbitrary"),
    )(*parts, w, m, v)


BIG = ("w_qkv", "w_o", "w_pw1", "w_pw2", "w_up", "w_down")
SMALL_SHARDED = (("b_pw1", 1), ("w_dw", 2), ("b_dw", 1), ("conv_ln_g", 1), ("conv_ln_b", 1), ("b_pw2", 1))
SMALL_REPL = ("b_mod", "norm_mix", "norm_mlp", "b_qkv", "b_o", "sinks", "final_norm")
WEIGHTS = ("w_mod", "b_mod", "norm_mix", "norm_mlp", "w_qkv", "b_qkv", "w_o", "b_o", "sinks", "w_pw1", "b_pw1",
           "w_dw", "b_dw", "conv_ln_g", "conv_ln_b", "w_pw2", "b_pw2", "w_up", "w_down", "final_norm")


def _step(x, c, loss_target, W, M, V):
    B, S, D = x.shape
    T = B * S
    L = W["w_mod"].shape[0]
    n_mod = W["w_mod"].shape[2]
    me = 4 * lax.axis_index("x") + 2 * lax.axis_index("y") + lax.axis_index("c")

    Wb = {n: W[n].astype(BF16) for n in BIG}

    def shards(i):
        jm = i // 2
        first, last = ("w_qkv", "w_o") if i % 2 == 0 else ("w_pw1", "w_pw2")
        return [Wb[first][jm], Wb[last][jm], Wb["w_up"][i], Wb["w_down"][i]]

    def carried(res, payload):
        return res if payload else (res, [])

    small_names = [n for n, _ in SMALL_SHARDED]
    small_src, small_sizes = _pack([c] + [W[n] for n in small_names], F32, 0)
    got = _exchange("gather_first", [(small_src, True), (shards(0)[0], True)])
    small_parts = _unpack(got[0], small_sizes, [c.shape] + [W[n].shape for n in small_names], 1)
    c_all = small_parts[0].reshape(N_DEV * B, D)
    full = {n: _from_slots(p, d) for (n, d), p in zip(SMALL_SHARDED, small_parts[1:])}
    gathered = [[got[1], None, None, None]] + [None] * (L - 1)

    b_mod_mine = lax.dynamic_slice_in_dim(W["b_mod"], me * n_mod, n_mod, axis=1).reshape(L, 1, n_mod)
    mod_part = _mod_fwd(c_all, W["w_mod"].astype(BF16), b_mod_mine)
    mod_slots = mod_part.reshape(L, N_DEV, B, n_mod).transpose(1, 0, 2, 3).reshape(N_DEV, L * B, n_mod)
    mod_recv = _exchange("scatter_mod", [(mod_slots, False)])[0]
    mod = mod_recv.reshape(N_DEV, L, B, n_mod).transpose(1, 2, 0, 3).reshape(L, B, N_MOD, 1, D)

    w_dw3 = [_conv_weight_chunks(full["w_dw"][j], D) for j in range(full["w_dw"].shape[0])]

    xc = x.reshape(T, D)
    saved = []
    h1 = _normmod_fwd("normmod_mix_fwd", xc, W["norm_mix"][0][None], mod[0, :, 1], mod[0, :, 0], S)
    for i in range(L):
        jm = i // 2
        sh1, sc1, g1, sh2, sc2, g2 = (mod[i, :, t] for t in range(N_MOD))
        nxt = [[(s, True)] for s in shards(i + 1)] if i + 1 < L else [[]] * 4
        g_first, g_last, g_up, g_down = gathered[i]
        sv = {"x_in": xc, "h1": h1}
        if i % 2 == 0:
            w_qkv_full = _from_slots(g_first, 1)
            sv["w_first"] = w_qkv_full
            mine = [(s, True) for s in shards(0)[1:]] if i == 0 else []
            pay = nxt[0] + mine[:1]
            qkv, n_first = carried(_mm("qkv_fwd", h1, w_qkv_full, bias=W["b_qkv"][jm], tm_pref=1024, payload=pay), pay)
            tab = _pair_tables(_attn_table(W["sinks"][jm]))
            mix_in, n_attn = carried(_attn_fwd(qkv, tab, B, S, payload=mine[1:]), mine[1:])
            if i == 0:
                g_last, g_up, g_down = n_first[-1], n_attn[0], n_attn[1]
            sv.update(qkv=qkv, tab=tab)
            b_out = W["b_o"][jm]
        else:
            sv["w_first"] = g_first
            u, n_first = carried(_mm("pw1_fwd", h1, g_first, w_form="nslots", bias=full["b_pw1"][jm], out_dtype=F32,
                                     tm_pref=1024, payload=nxt[0]), nxt[0])
            cv, mix_in = _conv_fwd(u, w_dw3[jm], full["b_dw"][jm][None], full["conv_ln_g"][jm][None],
                                   full["conv_ln_b"][jm][None], S)
            sv.update(u=u, cv=cv)
            b_out = full["b_pw2"][jm]
        w_last = g_last.reshape(-1, D)
        w_down_full = g_down.reshape(-1, D)
        sv.update(mix_in=mix_in, w_last=w_last, g_up=g_up, w_down=w_down_full)
        (x1, y1, h2), n_last = carried(_mm("mix_out_fwd", mix_in, w_last, bias=b_out, epi="resid", x=xc, gate=g1,
                                           norm=(W["norm_mlp"][i][None], sc2, sh2), rows_per_batch=S,
                                           tm_pref=1024, payload=nxt[1]), nxt[1])
        sv.update(y1=y1, x1=x1)
        up, n_up = carried(_mm("mlp_up_fwd", h2, g_up, w_form="nslots", epi="relu", tm_pref=1024, payload=nxt[2]),
                           nxt[2])
        norm_next = (W["norm_mix"][i + 1][None], mod[i + 1, :, 1], mod[i + 1, :, 0]) if i + 1 < L else None
        res, n_down = carried(_mm("mlp_down_fwd", up, w_down_full, a_sq=True, epi="resid", x=x1, gate=g2,
                                  norm=norm_next, rows_per_batch=S, payload=nxt[3]), nxt[3])
        x2, y2 = res[0], res[1]
        h1 = res[2] if i + 1 < L else None
        sv.update(h2=h2, up=up, y2=y2)
        saved.append(sv)
        xc = x2
        if i + 1 < L:
            gathered[i + 1] = [n_first[0], n_last[0], n_up[0], n_down[0]]

    dx, loss_blk, dgfin, dy2, dg2 = _loss_head(xc, loss_target.reshape(T, D), W["final_norm"][None],
                                               saved[L - 1]["y2"], mod[L - 1, :, 5], S)
    loss = lax.psum(loss_blk[0, 0], ("x", "y", "c"))

    G = {"final_norm": dgfin.reshape(D)}
    dmod_layers = [None] * L
    small_grads = ("norm_mix", "norm_mlp", "b_qkv", "b_o", "sinks", "b_pw1", "w_dw", "b_dw", "conv_ln_g", "conv_ln_b",
                   "b_pw2")
    acc = {n: [None] * W[n].shape[0] for n in small_grads}
    reduced = {n: [None] * W[n].shape[0] for n in BIG}

    waiting_up = None
    for i in reversed(range(L)):
        jm = i // 2
        sv = saved[i]
        sh1, sc1, g1, sh2, sc2, g2 = (mod[i, :, t] for t in range(N_MOD))
        pay = [(waiting_up[1], False)] if waiting_up else []
        (gd0, gd1), r = carried(_mm_tn("w_down_grad", sv["up"], dy2, a_sq=True, out_form="kslots", halves=True,
                                       tt_pref=4096, payload=pay), pay)
        if waiting_up:
            reduced["w_up"][waiting_up[0]].append(r[0])
        du, r0 = _mm("mlp_down_bwd", dy2, sv["w_down"], w_form="full_t", epi="relu2d", u=sv["up"],
                     payload=[(gd0, False)])
        (gu0, gu1), r1 = _mm_tn("w_up_grad", sv["h2"], du, out_form="nslots", halves=True, tt_pref=4096,
                                payload=[(gd1, False)])
        reduced["w_down"][i] = [r0[0], r1[0]]
        pay = [(gu0, False)] + ([] if i > 0 else [(gu1, False)])
        (dx, dsh2, dsc2, dgn, dy1, dg1, dy1_sum), r = _mm(
            "mlp_up_bwd", du, sv["g_up"], w_form="kslots_t", epi="normbwd", rows_per_batch=S,
            nb=dict(x=sv["x1"], dx_in=dx, gnorm=W["norm_mlp"][i][None], sc=sc2, gate=(sv["y1"], g1)),
            payload=pay)
        reduced["w_up"][i] = list(r)
        waiting_up = (i, gu1) if i > 0 else None
        acc["norm_mlp"][i] = dgn.reshape(D)
        gw_last = _mm_tn("mix_out_grad", sv["mix_in"], dy1)
        gw_last = gw_last.reshape((N_DEV, -1) + gw_last.shape[1:]).astype(BF16)
        below = (saved[i - 1]["y2"], mod[i - 1, :, 5]) if i > 0 else None
        nb_mix = dict(x=sv["x_in"], dx_in=dx, gnorm=W["norm_mix"][i][None], sc=sc1, gate=below)
        if i % 2 == 0:
            acc["b_o"][jm] = dy1_sum.reshape(D)
            do, r = _mm("attn_out_bwd", dy1, sv["w_last"], w_form="full_t", tm_pref=1024, payload=[(gw_last, False)])
            reduced["w_o"][jm] = r[0]
            dq, dkv, dsk = _attn_bwd(sv["qkv"], sv["mix_in"], do, sv["tab"], B, S)
            acc["sinks"][jm] = jnp.sum(dsk[..., 0], axis=0).transpose(0, 2, 1).reshape(N_HEADS)
            dqkv = jnp.concatenate([dq, dkv], axis=1)
            gw_qkv, db_qkv = _mm_tn("w_qkv_grad", sv["h1"], dqkv, colsum=True, tt_pref=4096)
            acc["b_qkv"][jm] = db_qkv.reshape(-1)
            gw_qkv = _to_slots(gw_qkv, 1).astype(BF16)
            res, r = _mm("qkv_bwd", dqkv, sv["w_first"], w_form="full_t", epi="normbwd", nb=nb_mix, rows_per_batch=S,
                         tm_pref=1024, payload=[(gw_qkv, False)])
            reduced["w_qkv"][jm] = r[0]
        else:
            acc["b_pw2"][jm] = dy1_sum.reshape(D)
            (dcv, dlg, dlb, dbdw), r = _mm(
                "pw2_bwd", dy1, sv["w_last"], w_form="full_t", epi="lnbwd", tm_pref=1024,
                ln=(sv["cv"], full["conv_ln_g"][jm][None], full["conv_ln_b"][jm][None]), payload=[(gw_last, False)])
            reduced["w_pw2"][jm] = r[0]
            acc["conv_ln_g"][jm], acc["conv_ln_b"][jm], acc["b_dw"][jm] = dlg.reshape(-1), dlb.reshape(-1), dbdw.reshape(-1)
            du1, dwdw = _conv_bwd(dcv, sv["u"], w_dw3[jm], S)
            acc["w_dw"][jm] = dwdw.transpose(1, 0, 2).reshape(CONV_HALO, D)[:CONV_WIDTH]
            gw_pw1, db_pw1 = _mm_tn("w_pw1_grad", sv["h1"], du1, out_form="nslots", colsum=True, tt_pref=4096)
            acc["b_pw1"][jm] = db_pw1.reshape(-1)
            res, r = _mm("pw1_bwd", du1, sv["w_first"], w_form="kslots_t", epi="normbwd", nb=nb_mix, rows_per_batch=S,
                         payload=[(gw_pw1, False)])
            reduced["w_pw1"][jm] = r[0]
        dx, dsh1, dsc1, dgn = res[:4]
        acc["norm_mix"][i] = dgn.reshape(D)
        dmod_layers[i] = jnp.concatenate([dsh1, dsc1, dg1, dsh2, dsc2, dg2], axis=1).reshape(B, N_MOD * D)
        if i > 0:
            dy2, dg2 = res[4], res[5]
    grad_x = dx.reshape(B, S, D)
    for n, parts in acc.items():
        G[n] = jnp.stack(parts)

    dmod = jnp.stack(dmod_layers)
    dmod_slots = dmod.reshape(L, B, N_DEV, n_mod).transpose(2, 0, 1, 3).reshape(N_DEV, L * B, n_mod)
    dmod_recv = _exchange("gather_dmod", [(dmod_slots, False)])[0]
    dmod_all = dmod_recv.reshape(N_DEV, L, B, n_mod).transpose(1, 0, 2, 3).reshape(L, N_DEV * B, n_mod)
    g_w_mod, db_mod_mine = _mod_bwd(c_all, dmod_all)
    G["b_mod"] = lax.dynamic_update_slice_in_dim(jnp.zeros_like(W["b_mod"]), db_mod_mine.reshape(L, n_mod),
                                                 me * n_mod, axis=1)

    small_items = [jnp.broadcast_to(G[n][None], (N_DEV,) + G[n].shape) for n in SMALL_REPL]
    small_items += [_to_slots(G[n], d) for n, d in SMALL_SHARDED]
    small_slots, small_sizes2 = _pack(small_items, F32, 1)
    small_recv = _exchange("reduce_small", [(small_slots, False)])[0]

    out = {}

    def run_adam(name, parts, names):
        shapes = [W[n].shape for n in names]
        wp, sizes = _pack([W[n] for n in names], F32, 0)
        mp, _ = _pack([M[n] for n in names], F32, 0)
        vp, _ = _pack([V[n] for n in names], F32, 0)
        res = _adam(name, [parts], wp, mp, vp)
        for kind, buf in zip(("grad", "delta", "new_m", "new_v"), res):
            for n, a in zip(names, _unpack(buf, sizes, shapes, 0)):
                out[kind + "_" + n] = a

    for n in BIG + ("w_mod",):
        cols = W[n].shape[-1]
        if n == "w_mod":
            parts = [g_w_mod.reshape(1, -1, cols)]
        else:
            pieces = [p for r in reduced[n] for p in (r if isinstance(r, list) else [r])]
            parts = [p.reshape(N_DEV, -1, cols) for p in pieces]
        res = _adam("adam_" + n, parts, W[n].reshape(-1, cols), M[n].reshape(-1, cols), V[n].reshape(-1, cols))
        for kind, buf in zip(("grad", "delta", "new_m", "new_v"), res):
            out[kind + "_" + n] = buf.reshape(W[n].shape)
    run_adam("adam_small", small_recv, list(SMALL_REPL) + [n for n, _ in SMALL_SHARDED])

    res = [loss, grad_x]
    for kind in ("grad", "delta", "new_m", "new_v"):
        res += [out[kind + "_" + n] for n in WEIGHTS]
    return tuple(res)


def kernel(x, c, w_mod, b_mod, norm_mix, norm_mlp, w_qkv, b_qkv, w_o, b_o, sinks, w_pw1, b_pw1, w_dw, b_dw, conv_ln_g, conv_ln_b, w_pw2, b_pw2, w_up, w_down, final_norm, loss_target, m_w_mod, m_b_mod, m_norm_mix, m_norm_mlp, m_w_qkv, m_b_qkv, m_w_o, m_b_o, m_sinks, m_w_pw1, m_b_pw1, m_w_dw, m_b_dw, m_conv_ln_g, m_conv_ln_b, m_w_pw2, m_b_pw2, m_w_up, m_w_down, m_final_norm, v_w_mod, v_b_mod, v_norm_mix, v_norm_mlp, v_w_qkv, v_b_qkv, v_w_o, v_b_o, v_sinks, v_w_pw1, v_b_pw1, v_w_dw, v_b_dw, v_conv_ln_g, v_conv_ln_b, v_w_pw2, v_b_pw2, v_w_up, v_w_down, v_final_norm):
    W = dict(w_mod=w_mod, b_mod=b_mod, norm_mix=norm_mix, norm_mlp=norm_mlp, w_qkv=w_qkv, b_qkv=b_qkv, w_o=w_o,
             b_o=b_o, sinks=sinks, w_pw1=w_pw1, b_pw1=b_pw1, w_dw=w_dw, b_dw=b_dw, conv_ln_g=conv_ln_g,
             conv_ln_b=conv_ln_b, w_pw2=w_pw2, b_pw2=b_pw2, w_up=w_up, w_down=w_down, final_norm=final_norm)
    M = dict(w_mod=m_w_mod, b_mod=m_b_mod, norm_mix=m_norm_mix, norm_mlp=m_norm_mlp, w_qkv=m_w_qkv, b_qkv=m_b_qkv,
             w_o=m_w_o, b_o=m_b_o, sinks=m_sinks, w_pw1=m_w_pw1, b_pw1=m_b_pw1, w_dw=m_w_dw, b_dw=m_b_dw,
             conv_ln_g=m_conv_ln_g, conv_ln_b=m_conv_ln_b, w_pw2=m_w_pw2, b_pw2=m_b_pw2, w_up=m_w_up,
             w_down=m_w_down, final_norm=m_final_norm)
    V = dict(w_mod=v_w_mod, b_mod=v_b_mod, norm_mix=v_norm_mix, norm_mlp=v_norm_mlp, w_qkv=v_w_qkv, b_qkv=v_b_qkv,
             w_o=v_w_o, b_o=v_b_o, sinks=v_sinks, w_pw1=v_w_pw1, b_pw1=v_b_pw1, w_dw=v_w_dw, b_dw=v_b_dw,
             conv_ln_g=v_conv_ln_g, conv_ln_b=v_conv_ln_b, w_pw2=v_w_pw2, b_pw2=v_b_pw2, w_up=v_w_up,
             w_down=v_w_down, final_norm=v_final_norm)
    return _step(x, c, loss_target, W, M, V)
```
